```python
import jax, jax.numpy as jnp
from jax import lax
import numpy as np

D_MODEL = 1024
BATCH = 8
SEQ = 4096
DEPTH = 1

MLA_HEADS = 4
QK_NOPE = 128
QK_ROPE = 64
V_HEAD = 128
Q_LORA = 256
KV_LORA = 256
QK_HEAD = QK_NOPE + QK_ROPE
MLA_WIDTH = MLA_HEADS * V_HEAD
HG_HEADS = 4
HG_DK = 128
HG_DV = 128
HG_FDIM = HG_HEADS * HG_DK
HG_WIDTH = HG_HEADS * HG_DV
CHUNK = 64
D_MIX = MLA_WIDTH + HG_WIDTH
D_FF = -(-8 * D_MODEL // (3 * 256)) * 256
PLE_DIM = 256
ROPE_THETA = 10000.0
EPS = 1e-6
Q_BLOCK = 128
IN_SIZES = (Q_LORA, KV_LORA, QK_ROPE, HG_FDIM, HG_FDIM, HG_FDIM, HG_WIDTH, HG_WIDTH)
D_IN = sum(IN_SIZES)
IN_SPLITS = tuple(int(v) for v in np.cumsum(IN_SIZES)[:-1])

kernel_name = "hymba_mla_hgrn2_ple_block"


def rms_norm(x, g):
    xf = x.astype(jnp.float32)
    y = xf * lax.rsqrt(jnp.mean(xf * xf, axis=-1, keepdims=True) + EPS)
    return (y * g.astype(jnp.float32)).astype(x.dtype)


def rope(x, cos, sin):
    x1, x2 = jnp.split(x.astype(jnp.float32), 2, axis=-1)
    return jnp.concatenate([x1 * cos - x2 * sin, x1 * sin + x2 * cos], axis=-1).astype(x.dtype)


def blocked_attention(q, k, v):
    B, S, H, D = q.shape
    nb = S // Q_BLOCK
    qb = q.reshape(B, nb, Q_BLOCK, H, D).transpose(1, 0, 2, 3, 4)
    scale = QK_HEAD ** -0.5

    def one_block(q_blk):
        s = jnp.einsum('bqhd,bkhd->bhqk', q_blk, k, preferred_element_type=jnp.float32) * scale
        w = jax.nn.softmax(s, axis=-1).astype(v.dtype)
        return jnp.einsum('bhqk,bkhd->bqhd', w, v)

    o = lax.map(one_block, qb)
    return o.transpose(1, 0, 2, 3, 4).reshape(B, S, H * v.shape[-1])


def mla(c_q, c_kv, k_r, cos, sin, g_qa, g_kva, w_qb, w_kvb, g_qn, g_kn):
    B, S, _ = c_q.shape
    q = (rms_norm(c_q, g_qa) @ w_qb).reshape(B, S, MLA_HEADS, QK_HEAD)
    kv = (rms_norm(c_kv, g_kva) @ w_kvb).reshape(B, S, MLA_HEADS, QK_NOPE + V_HEAD)
    k_nope, v = kv[..., :QK_NOPE], kv[..., QK_NOPE:]
    k = jnp.concatenate([k_nope, jnp.broadcast_to(k_r[:, :, None, :], (B, S, MLA_HEADS, QK_ROPE))], axis=-1)
    q = rms_norm(q, g_qn)
    k = rms_norm(k, g_kn)
    c, s = cos[:, :, None, :], sin[:, :, None, :]
    q = jnp.concatenate([q[..., :QK_NOPE], rope(q[..., QK_NOPE:], c, s)], axis=-1)
    k = jnp.concatenate([k[..., :QK_NOPE], rope(k[..., QK_NOPE:], c, s)], axis=-1)
    return blocked_attention(q, k, v)


def gla_chunkwise(q, k, v, logf):
    B, H, S, DK = q.shape
    DV = v.shape[-1]
    N = S // CHUNK
    q = q.reshape(B, H, N, CHUNK, DK)
    k = k.reshape(B, H, N, CHUNK, DK)
    v = v.reshape(B, H, N, CHUNK, DV)
    b = jnp.cumsum(logf.reshape(B, H, N, CHUNK, DK), axis=3)
    b_last = b[:, :, :, -1:, :]
    b_mid = b[:, :, :, CHUNK // 2 - 1:CHUNK // 2, :]
    q_intra = q * jnp.exp(b - b_mid)
    k_intra = k * jnp.exp(b_mid - b)
    mask = jnp.tril(jnp.ones((CHUNK, CHUNK), jnp.float32))
    A = jnp.einsum('bhntd,bhnsd->bhnts', q_intra, k_intra) * mask
    o_intra = jnp.einsum('bhnts,bhnsv->bhntv', A, v)
    kv = jnp.einsum('bhnsd,bhnsv->bhndv', k * jnp.exp(b_last - b), v)
    decay = jnp.exp(b_last[:, :, :, 0, :])

    def step(state, inp):
        dec, kv_n = inp
        return dec[..., None] * state + kv_n, state

    _, s_before = lax.scan(step, jnp.zeros((B, H, DK, DV), jnp.float32),
                           (jnp.moveaxis(decay, 2, 0), jnp.moveaxis(kv, 2, 0)))
    s_before = jnp.moveaxis(s_before, 0, 2)
    o_inter = jnp.einsum('bhntd,bhndv->bhntv', q * jnp.exp(b), s_before)
    return (o_intra + o_inter).reshape(B, H, S, DV)


def hgrn2(hq, hf_fwd, hf_bwd, hi, hg, lb, g_out):
    B, S, _ = hq.shape

    def heads(t):
        return t.reshape(B, S, HG_HEADS, -1).transpose(0, 2, 1, 3)

    q = heads(jax.nn.silu(hq.astype(jnp.float32)))
    v = heads(hi.astype(jnp.float32))

    def gates(f_pre, lower):
        f = lower + (1.0 - lower) * jax.nn.sigmoid(f_pre.astype(jnp.float32))
        return heads(1.0 - f), heads(jnp.log(f))

    k_f, lf_f = gates(hf_fwd, lb[0])
    k_b, lf_b = gates(hf_bwd, lb[1])
    o_f = gla_chunkwise(q, k_f, v, lf_f)
    o_b = gla_chunkwise(q[:, :, ::-1], k_b[:, :, ::-1], v[:, :, ::-1], lf_b[:, :, ::-1])[:, :, ::-1]
    o = (o_f + o_b).transpose(0, 2, 1, 3)
    o = rms_norm(o, g_out).reshape(B, S, HG_WIDTH)
    return (o * jax.nn.silu(hg.astype(jnp.float32))).astype(hq.dtype)


def _fwd_setup_inputs(seed: int = 0) -> dict:
    key = jax.random.key(seed)
    ks = jax.random.split(key, 32)
    f32 = jnp.float32

    def w(k, shape, fan_in):
        return jax.random.normal(k, shape, f32) * (fan_in ** -0.5)

    def gain(k, shape):
        return 1.0 + 0.05 * jax.random.normal(k, shape, f32)

    x = jax.random.normal(ks[0], (BATCH, SEQ, D_MODEL), f32)
    p = jax.random.normal(ks[1], (DEPTH, BATCH, SEQ, PLE_DIM), f32)
    offsets = jax.random.randint(ks[2], (BATCH, 1), 0, 1024, jnp.int32)
    positions = offsets + jnp.arange(SEQ, dtype=jnp.int32)[None, :]
    return {
        "x": x,
        "p": p,
        "positions": positions,
        "g_mix": gain(ks[3], (DEPTH, D_MODEL)),
        "w_in": w(ks[4], (DEPTH, D_MODEL, D_IN), D_MODEL),
        "g_qa": gain(ks[5], (DEPTH, Q_LORA)),
        "g_kva": gain(ks[6], (DEPTH, KV_LORA)),
        "w_qb": w(ks[7], (DEPTH, Q_LORA, MLA_HEADS * QK_HEAD), Q_LORA),
        "w_kvb": w(ks[8], (DEPTH, KV_LORA, MLA_HEADS * (QK_NOPE + V_HEAD)), KV_LORA),
        "g_qn": gain(ks[9], (DEPTH, QK_HEAD)),
        "g_kn": gain(ks[10], (DEPTH, QK_HEAD)),
        "lb_param": 0.1 * jax.random.normal(ks[11], (DEPTH + 1, 2, HG_FDIM), f32),
        "g_hgo": gain(ks[12], (DEPTH, HG_HEADS, HG_DV)),
        "w_o": w(ks[13], (DEPTH, D_MIX, D_MODEL), D_MIX),
        "g_ffn": gain(ks[14], (DEPTH, D_MODEL)),
        "w_gate": w(ks[15], (DEPTH, D_MODEL, D_FF), D_MODEL),
        "w_up": w(ks[16], (DEPTH, D_MODEL, D_FF), D_MODEL),
        "w_down": w(ks[17], (DEPTH, D_FF, D_MODEL), D_FF),
        "g_ple": gain(ks[18], (DEPTH, D_MODEL)),
        "w_ple_gate": w(ks[19], (DEPTH, D_MODEL, D_MODEL), D_MODEL),
        "w_ple_proj": w(ks[20], (DEPTH, PLE_DIM, D_MODEL), PLE_DIM),
    }


def _fwd_reference(x, p, positions, g_mix, w_in, g_qa, g_kva, w_qb, w_kvb, g_qn, g_kn,
              lb_param, g_hgo, w_o, g_ffn, w_gate, w_up, w_down, g_ple, w_ple_gate, w_ple_proj):
    inv_freq = ROPE_THETA ** (-jnp.arange(0, QK_ROPE, 2, dtype=jnp.float32) / QK_ROPE)
    ang = positions.astype(jnp.float32)[..., None] * inv_freq
    cos, sin = jnp.cos(ang), jnp.sin(ang)
    lower_bounds = jnp.cumsum(jax.nn.softmax(lb_param.astype(jnp.float32), axis=0), axis=0)

    for l in range(DEPTH):
        h = rms_norm(x, g_mix[l])
        z = h @ w_in[l]
        c_q, c_kv, k_r, hq, hf_f, hf_b, hi, hg = jnp.split(z, IN_SPLITS, axis=-1)
        a = mla(c_q, c_kv, k_r, cos, sin, g_qa[l], g_kva[l], w_qb[l], w_kvb[l], g_qn[l], g_kn[l])
        r = hgrn2(hq, hf_f, hf_b, hi, hg, lower_bounds[l], g_hgo[l])
        x = x + jnp.concatenate([a, r], axis=-1) @ w_o[l]
        h = rms_norm(x, g_ffn[l])
        x = x + (jax.nn.silu(h @ w_gate[l]) * (h @ w_up[l])) @ w_down[l]
        gate = jax.nn.sigmoid(rms_norm(x, g_ple[l]) @ w_ple_gate[l])
        x = x + gate * (p[l].astype(x.dtype) @ w_ple_proj[l])
    return x


import jax as _jax
import jax.numpy as _jnp

TWIN_FORMAT = 'train_step'
FWD_PARAMS = ['x', 'p', 'positions', 'g_mix', 'w_in', 'g_qa', 'g_kva', 'w_qb', 'w_kvb', 'g_qn', 'g_kn', 'lb_param', 'g_hgo', 'w_o', 'g_ffn', 'w_gate', 'w_up', 'w_down', 'g_ple', 'w_ple_gate', 'w_ple_proj']
TWIN_WEIGHTS = ['g_mix', 'w_in', 'g_qa', 'g_kva', 'w_qb', 'w_kvb', 'g_qn', 'g_kn', 'lb_param', 'g_hgo', 'w_o', 'g_ffn', 'w_gate', 'w_up', 'w_down', 'g_ple', 'w_ple_gate', 'w_ple_proj']
TWIN_DIFF_INPUT = 'x'
TWIN_INPUTS = ['x', 'p', 'positions', 'g_mix', 'w_in', 'g_qa', 'g_kva', 'w_qb', 'w_kvb', 'g_qn', 'g_kn', 'lb_param', 'g_hgo', 'w_o', 'g_ffn', 'w_gate', 'w_up', 'w_down', 'g_ple', 'w_ple_gate', 'w_ple_proj', 'loss_target', 'm_g_mix', 'm_w_in', 'm_g_qa', 'm_g_kva', 'm_w_qb', 'm_w_kvb', 'm_g_qn', 'm_g_kn', 'm_lb_param', 'm_g_hgo', 'm_w_o', 'm_g_ffn', 'm_w_gate', 'm_w_up', 'm_w_down', 'm_g_ple', 'm_w_ple_gate', 'm_w_ple_proj', 'v_g_mix', 'v_w_in', 'v_g_qa', 'v_g_kva', 'v_w_qb', 'v_w_kvb', 'v_g_qn', 'v_g_kn', 'v_lb_param', 'v_g_hgo', 'v_w_o', 'v_g_ffn', 'v_w_gate', 'v_w_up', 'v_w_down', 'v_g_ple', 'v_w_ple_gate', 'v_w_ple_proj']
TWIN_OUTPUTS = ['loss', 'grad_x', 'grad_g_mix', 'grad_w_in', 'grad_g_qa', 'grad_g_kva', 'grad_w_qb', 'grad_w_kvb', 'grad_g_qn', 'grad_g_kn', 'grad_lb_param', 'grad_g_hgo', 'grad_w_o', 'grad_g_ffn', 'grad_w_gate', 'grad_w_up', 'grad_w_down', 'grad_g_ple', 'grad_w_ple_gate', 'grad_w_ple_proj', 'delta_g_mix', 'delta_w_in', 'delta_g_qa', 'delta_g_kva', 'delta_w_qb', 'delta_w_kvb', 'delta_g_qn', 'delta_g_kn', 'delta_lb_param', 'delta_g_hgo', 'delta_w_o', 'delta_g_ffn', 'delta_w_gate', 'delta_w_up', 'delta_w_down', 'delta_g_ple', 'delta_w_ple_gate', 'delta_w_ple_proj', 'new_m_g_mix', 'new_m_w_in', 'new_m_g_qa', 'new_m_g_kva', 'new_m_w_qb', 'new_m_w_kvb', 'new_m_g_qn', 'new_m_g_kn', 'new_m_lb_param', 'new_m_g_hgo', 'new_m_w_o', 'new_m_g_ffn', 'new_m_w_gate', 'new_m_w_up', 'new_m_w_down', 'new_m_g_ple', 'new_m_w_ple_gate', 'new_m_w_ple_proj', 'new_v_g_mix', 'new_v_w_in', 'new_v_g_qa', 'new_v_g_kva', 'new_v_w_qb', 'new_v_w_kvb', 'new_v_g_qn', 'new_v_g_kn', 'new_v_lb_param', 'new_v_g_hgo', 'new_v_w_o', 'new_v_g_ffn', 'new_v_w_gate', 'new_v_w_up', 'new_v_w_down', 'new_v_g_ple', 'new_v_w_ple_gate', 'new_v_w_ple_proj']
TWIN_LEAF_KINDS = {'loss': 'loss', 'grad_x': 'grad_x', 'grad_g_mix': 'grad_w', 'grad_w_in': 'grad_w', 'grad_g_qa': 'grad_w', 'grad_g_kva': 'grad_w', 'grad_w_qb': 'grad_w', 'grad_w_kvb': 'grad_w', 'grad_g_qn': 'grad_w', 'grad_g_kn': 'grad_w', 'grad_lb_param': 'grad_w', 'grad_g_hgo': 'grad_w', 'grad_w_o': 'grad_w', 'grad_g_ffn': 'grad_w', 'grad_w_gate': 'grad_w', 'grad_w_up': 'grad_w', 'grad_w_down': 'grad_w', 'grad_g_ple': 'grad_w', 'grad_w_ple_gate': 'grad_w', 'grad_w_ple_proj': 'grad_w', 'delta_g_mix': 'delta_w', 'delta_w_in': 'delta_w', 'delta_g_qa': 'delta_w', 'delta_g_kva': 'delta_w', 'delta_w_qb': 'delta_w', 'delta_w_kvb': 'delta_w', 'delta_g_qn': 'delta_w', 'delta_g_kn': 'delta_w', 'delta_lb_param': 'delta_w', 'delta_g_hgo': 'delta_w', 'delta_w_o': 'delta_w', 'delta_g_ffn': 'delta_w', 'delta_w_gate': 'delta_w', 'delta_w_up': 'delta_w', 'delta_w_down': 'delta_w', 'delta_g_ple': 'delta_w', 'delta_w_ple_gate': 'delta_w', 'delta_w_ple_proj': 'delta_w', 'new_m_g_mix': 'new_m', 'new_m_w_in': 'new_m', 'new_m_g_qa': 'new_m', 'new_m_g_kva': 'new_m', 'new_m_w_qb': 'new_m', 'new_m_w_kvb': 'new_m', 'new_m_g_qn': 'new_m', 'new_m_g_kn': 'new_m', 'new_m_lb_param': 'new_m', 'new_m_g_hgo': 'new_m', 'new_m_w_o': 'new_m', 'new_m_g_ffn': 'new_m', 'new_m_w_gate': 'new_m', 'new_m_w_up': 'new_m', 'new_m_w_down': 'new_m', 'new_m_g_ple': 'new_m', 'new_m_w_ple_gate': 'new_m', 'new_m_w_ple_proj': 'new_m', 'new_v_g_mix': 'new_v', 'new_v_w_in': 'new_v', 'new_v_g_qa': 'new_v', 'new_v_g_kva': 'new_v', 'new_v_w_qb': 'new_v', 'new_v_w_kvb': 'new_v', 'new_v_g_qn': 'new_v', 'new_v_g_kn': 'new_v', 'new_v_lb_param': 'new_v', 'new_v_g_hgo': 'new_v', 'new_v_w_o': 'new_v', 'new_v_g_ffn': 'new_v', 'new_v_w_gate': 'new_v', 'new_v_w_up': 'new_v', 'new_v_w_down': 'new_v', 'new_v_g_ple': 'new_v', 'new_v_w_ple_gate': 'new_v', 'new_v_w_ple_proj': 'new_v'}


def _forward(args):
    return _fwd_reference(*[args[k] for k in FWD_PARAMS])


def _output_shape():
    out = _jax.eval_shape(lambda: _forward(_fwd_setup_inputs(0)))
    return out.shape, out.dtype

N_MICROBATCH = 1
ADAM_LR = 0.001
ADAM_B1 = 0.9
ADAM_B2 = 0.999
ADAM_EPS = 1e-08
ADAM_WD = 0.01
ADAM_STEP = 10
PER_EXAMPLE_BATCH_AXIS = {'x': 0, 'p': 1, 'positions': 0, 'loss_target': 0}
SHARED_INPUTS = []
_WEIGHT_DTYPES = {'g_mix': _jnp.float32, 'w_in': _jnp.float32, 'g_qa': _jnp.float32, 'g_kva': _jnp.float32, 'w_qb': _jnp.float32, 'w_kvb': _jnp.float32, 'g_qn': _jnp.float32, 'g_kn': _jnp.float32, 'lb_param': _jnp.float32, 'g_hgo': _jnp.float32, 'w_o': _jnp.float32, 'g_ffn': _jnp.float32, 'w_gate': _jnp.float32, 'w_up': _jnp.float32, 'w_down': _jnp.float32, 'g_ple': _jnp.float32, 'w_ple_gate': _jnp.float32, 'w_ple_proj': _jnp.float32}
MOMENT_SCALE = {'g_mix': 6.608488e+00, 'w_in': 2.049625e-01, 'g_qa': 6.600482e-02, 'g_kva': 1.952785e-01, 'w_qb': 3.971587e-02, 'w_kvb': 5.205287e-02, 'g_qn': 2.064485e-01, 'g_kn': 2.081488e-01, 'lb_param': 1.505406e-02, 'g_hgo': 1.143859e+01, 'w_o': 2.465615e-01, 'g_ffn': 2.473568e+01, 'w_gate': 1.613772e-01, 'w_up': 1.687598e-01, 'w_down': 2.595229e-01, 'g_ple': 9.457090e-01, 'w_ple_gate': 7.707643e-02, 'w_ple_proj': 4.220218e-01}


def _to_microbatches(a, axis):
    t = _jnp.moveaxis(a, axis, 0)
    t = t.reshape((N_MICROBATCH, t.shape[0] // N_MICROBATCH) + t.shape[1:])
    return _jnp.moveaxis(t, 1, axis + 1)


def setup_inputs(seed: int = 0) -> dict:
    inp = _fwd_setup_inputs(seed)
    key = _jax.random.fold_in(_jax.random.key(seed), 7919)
    shape, _ = _output_shape()
    out = dict(inp)
    out["loss_target"] = _jax.random.normal(_jax.random.fold_in(key, 0), shape, _jnp.float32)
    for i, name in enumerate(TWIN_WEIGHTS):
        w = inp[name].astype(_jnp.float32)
        if MOMENT_SCALE is None:
            s = _jnp.sqrt(_jnp.mean(_jnp.square(w)) + 1e-30)
        else:
            s = MOMENT_SCALE[name]
        km, kv = _jax.random.split(_jax.random.fold_in(key, i + 1))
        out[name] = w
        out["m_" + name] = s * _jax.random.normal(km, w.shape, _jnp.float32)
        out["v_" + name] = (s * s) * _jax.random.uniform(kv, w.shape, _jnp.float32, 0.5, 1.5)
    if N_MICROBATCH > 1:
        for name, axis in PER_EXAMPLE_BATCH_AXIS.items():
            out[name] = _to_microbatches(out[name], axis)
    return {'x': out['x'], 'p': out['p'], 'positions': out['positions'], 'g_mix': out['g_mix'], 'w_in': out['w_in'], 'g_qa': out['g_qa'], 'g_kva': out['g_kva'], 'w_qb': out['w_qb'], 'w_kvb': out['w_kvb'], 'g_qn': out['g_qn'], 'g_kn': out['g_kn'], 'lb_param': out['lb_param'], 'g_hgo': out['g_hgo'], 'w_o': out['w_o'], 'g_ffn': out['g_ffn'], 'w_gate': out['w_gate'], 'w_up': out['w_up'], 'w_down': out['w_down'], 'g_ple': out['g_ple'], 'w_ple_gate': out['w_ple_gate'], 'w_ple_proj': out['w_ple_proj'], 'loss_target': out['loss_target'], 'm_g_mix': out['m_g_mix'], 'm_w_in': out['m_w_in'], 'm_g_qa': out['m_g_qa'], 'm_g_kva': out['m_g_kva'], 'm_w_qb': out['m_w_qb'], 'm_w_kvb': out['m_w_kvb'], 'm_g_qn': out['m_g_qn'], 'm_g_kn': out['m_g_kn'], 'm_lb_param': out['m_lb_param'], 'm_g_hgo': out['m_g_hgo'], 'm_w_o': out['m_w_o'], 'm_g_ffn': out['m_g_ffn'], 'm_w_gate': out['m_w_gate'], 'm_w_up': out['m_w_up'], 'm_w_down': out['m_w_down'], 'm_g_ple': out['m_g_ple'], 'm_w_ple_gate': out['m_w_ple_gate'], 'm_w_ple_proj': out['m_w_ple_proj'], 'v_g_mix': out['v_g_mix'], 'v_w_in': out['v_w_in'], 'v_g_qa': out['v_g_qa'], 'v_g_kva': out['v_g_kva'], 'v_w_qb': out['v_w_qb'], 'v_w_kvb': out['v_w_kvb'], 'v_g_qn': out['v_g_qn'], 'v_g_kn': out['v_g_kn'], 'v_lb_param': out['v_lb_param'], 'v_g_hgo': out['v_g_hgo'], 'v_w_o': out['v_w_o'], 'v_g_ffn': out['v_g_ffn'], 'v_w_gate': out['v_w_gate'], 'v_w_up': out['v_w_up'], 'v_w_down': out['v_w_down'], 'v_g_ple': out['v_g_ple'], 'v_w_ple_gate': out['v_w_ple_gate'], 'v_w_ple_proj': out['v_w_ple_proj']}


def _loss(weights, diff, rest, loss_target):
    with _jax.named_scope("forward"):
        args = {**rest, TWIN_DIFF_INPUT: diff, **{k: w.astype(_WEIGHT_DTYPES[k]) for k, w in weights.items()}}
        y = _forward(args)
    with _jax.named_scope("loss_head"):
        err = _jnp.square(y.astype(_jnp.float32) - loss_target)
        return 0.5 * _jnp.sum(_jnp.mean(err, axis=-1)) if err.ndim else 0.5 * err


def _adamw(w, g, m, v):
    m = ADAM_B1 * m + (1.0 - ADAM_B1) * g
    v = ADAM_B2 * v + (1.0 - ADAM_B2) * _jnp.square(g)
    m_hat = m / (1.0 - ADAM_B1 ** ADAM_STEP)
    v_hat = v / (1.0 - ADAM_B2 ** ADAM_STEP)
    delta = -ADAM_LR * (m_hat / (_jnp.sqrt(v_hat) + ADAM_EPS) + ADAM_WD * w)
    return delta, m, v


def reference(x, p, positions, g_mix, w_in, g_qa, g_kva, w_qb, w_kvb, g_qn, g_kn, lb_param, g_hgo, w_o, g_ffn, w_gate, w_up, w_down, g_ple, w_ple_gate, w_ple_proj, loss_target, m_g_mix, m_w_in, m_g_qa, m_g_kva, m_w_qb, m_w_kvb, m_g_qn, m_g_kn, m_lb_param, m_g_hgo, m_w_o, m_g_ffn, m_w_gate, m_w_up, m_w_down, m_g_ple, m_w_ple_gate, m_w_ple_proj, v_g_mix, v_w_in, v_g_qa, v_g_kva, v_w_qb, v_w_kvb, v_g_qn, v_g_kn, v_lb_param, v_g_hgo, v_w_o, v_g_ffn, v_w_gate, v_w_up, v_w_down, v_g_ple, v_w_ple_gate, v_w_ple_proj):
    given = dict(x=x, p=p, positions=positions, g_mix=g_mix, w_in=w_in, g_qa=g_qa, g_kva=g_kva, w_qb=w_qb, w_kvb=w_kvb, g_qn=g_qn, g_kn=g_kn, lb_param=lb_param, g_hgo=g_hgo, w_o=w_o, g_ffn=g_ffn, w_gate=w_gate, w_up=w_up, w_down=w_down, g_ple=g_ple, w_ple_gate=w_ple_gate, w_ple_proj=w_ple_proj, loss_target=loss_target, m_g_mix=m_g_mix, m_w_in=m_w_in, m_g_qa=m_g_qa, m_g_kva=m_g_kva, m_w_qb=m_w_qb, m_w_kvb=m_w_kvb, m_g_qn=m_g_qn, m_g_kn=m_g_kn, m_lb_param=m_lb_param, m_g_hgo=m_g_hgo, m_w_o=m_w_o, m_g_ffn=m_g_ffn, m_w_gate=m_w_gate, m_w_up=m_w_up, m_w_down=m_w_down, m_g_ple=m_g_ple, m_w_ple_gate=m_w_ple_gate, m_w_ple_proj=m_w_ple_proj, v_g_mix=v_g_mix, v_w_in=v_w_in, v_g_qa=v_g_qa, v_g_kva=v_g_kva, v_w_qb=v_w_qb, v_w_kvb=v_w_kvb, v_g_qn=v_g_qn, v_g_kn=v_g_kn, v_lb_param=v_lb_param, v_g_hgo=v_g_hgo, v_w_o=v_w_o, v_g_ffn=v_g_ffn, v_w_gate=v_w_gate, v_w_up=v_w_up, v_w_down=v_w_down, v_g_ple=v_g_ple, v_w_ple_gate=v_w_ple_gate, v_w_ple_proj=v_w_ple_proj)
    weights = {n: given[n] for n in TWIN_WEIGHTS}
    shared = {n: given[n] for n in SHARED_INPUTS}
    per_example = {n: given[n] for n in ['x', 'p', 'positions']}
    grad_fn = _jax.value_and_grad(_loss, argnums=(0, 1))

    def one_microbatch(ex, loss_target):
        ex = dict(ex)
        diff = ex.pop(TWIN_DIFF_INPUT)
        return grad_fn(weights, diff, {**shared, **ex}, loss_target)

    if N_MICROBATCH == 1:
        loss, (grad_w, grad_x) = one_microbatch(per_example, given["loss_target"])
    else:
        def body(carry, xs):
            loss_sum, grad_sum = carry
            l_k, (gw_k, gx_k) = one_microbatch(xs[0], xs[1])
            with _jax.named_scope("update"):
                return (loss_sum + l_k, _jax.tree.map(_jnp.add, grad_sum, gw_k)), gx_k

        init = (_jnp.zeros((), _jnp.float32), _jax.tree.map(_jnp.zeros_like, weights))
        (loss, grad_w), grad_x = _jax.lax.scan(body, init, (per_example, given["loss_target"]))
    with _jax.named_scope("update"):
        delta_w, new_m, new_v = {}, {}, {}
        for n in TWIN_WEIGHTS:
            delta_w[n], new_m[n], new_v[n] = _adamw(weights[n], grad_w[n], given["m_" + n], given["v_" + n])
    return (loss, grad_x, *[grad_w[n] for n in TWIN_WEIGHTS], *[delta_w[n] for n in TWIN_WEIGHTS],
            *[new_m[n] for n in TWIN_WEIGHTS], *[new_v[n] for n in TWIN_WEIGHTS])
```

```python
import functools

import jax
import jax.numpy as jnp
from jax import lax
from jax.experimental import pallas as pl
from jax.experimental.pallas import tpu as pltpu

F32 = jnp.float32
BF16 = jnp.bfloat16
MESH = pl.DeviceIdType.MESH

EPS = 1e-6
ROPE_THETA = 10000.0
MLA_HEADS = 4
QK_NOPE = 128
QK_ROPE = 64
QK_HEAD = QK_NOPE + QK_ROPE
V_HEAD = 128
HG_HEADS = 4
HG_DK = 128
CHUNK = 64
ADAM_LR = 0.001
ADAM_B1 = 0.9
ADAM_B2 = 0.999
ADAM_EPS = 1e-08
ADAM_WD = 0.01
ADAM_STEP = 10

LANE = 128
VMEM_LIMIT = 56 * 1024 * 1024
TOK_TILE = 256
GLA_GROUP = 8
ATT_TQ = 256
ATT_TK = 256
N_CHIPS = 4
N_DEV = 8


def _params(dims=None, **kw):
    return pltpu.CompilerParams(dimension_semantics=dims, vmem_limit_bytes=VMEM_LIMIT, **kw)


def _tile(n, target):
    if n <= target:
        return n
    best = None
    for d in range(LANE, target + 1, LANE):
        if n % d == 0:
            best = d
    return best if best is not None else n


def _dot_raw(a, b, kind):
    nb = a.ndim - 2
    batch = ((0,), (0,)) if nb else ((), ())
    ca = nb if kind == "tn" else nb + 1
    cb = nb + 1 if kind == "nt" else nb
    return lax.dot_general(a.astype(BF16), b.astype(BF16), (((ca,), (cb,)), batch), preferred_element_type=F32)


@functools.partial(jax.custom_vjp, nondiff_argnums=(2,))
def _bdot(a, b, kind):
    return _dot_raw(a, b, kind)


def _bdot_fwd(a, b, kind):
    return _dot_raw(a, b, kind), (a, b)


def _bdot_bwd(kind, res, g):
    a, b = res
    if kind == "nn":
        da, db = _bdot(g, b, "nt"), _bdot(a, g, "tn")
    elif kind == "nt":
        da, db = _bdot(g, b, "nn"), _bdot(g, a, "tn")
    else:
        da, db = _bdot(b, g, "nt"), _bdot(a, g, "nn")
    return da.astype(a.dtype), db.astype(b.dtype)


_bdot.defvjp(_bdot_fwd, _bdot_bwd)


def _mm(a, b, *, name, ta=False, tb=False, add=None, out_dtype=F32):
    K, M = a.shape if ta else a.shape[::-1]
    N, Kb = b.shape if tb else b.shape[::-1]
    assert K == Kb, (a.shape, b.shape, ta, tb)
    tm, tn, tk = _tile(M, 512), _tile(N, 1536), _tile(K, 1536)
    nk = K // tk
    kind = "tn" if ta else ("nt" if tb else "nn")
    assert not (ta and tb)
    a_spec = pl.BlockSpec((tk, tm), lambda i, j, k: (k, i)) if ta else pl.BlockSpec((tm, tk), lambda i, j, k: (i, k))
    b_spec = pl.BlockSpec((tn, tk), lambda i, j, k: (j, k)) if tb else pl.BlockSpec((tk, tn), lambda i, j, k: (k, j))
    o_spec = pl.BlockSpec((tm, tn), lambda i, j, k: (i, j))
    has_add = add is not None

    def body(*refs):
        a_ref, b_ref = refs[0], refs[1]
        add_ref = refs[2] if has_add else None
        o_ref = refs[3] if has_add else refs[2]
        part = _dot_raw(a_ref[...], b_ref[...], kind)
        if nk == 1:
            if has_add:
                part = part + add_ref[...].astype(F32)
            o_ref[...] = part.astype(o_ref.dtype)
            return
        acc_ref = refs[-1]
        k = pl.program_id(2)

        @pl.when(k == 0)
        def _():
            acc_ref[...] = part

        @pl.when(k > 0)
        def _():
            acc_ref[...] += part

        @pl.when(k == nk - 1)
        def _():
            r = acc_ref[...]
            if has_add:
                r = r + add_ref[...].astype(F32)
            o_ref[...] = r.astype(o_ref.dtype)

    ins = [a, b] + ([add] if has_add else [])
    in_specs = [a_spec, b_spec] + ([o_spec] if has_add else [])
    return pl.pallas_call(
        body, name=name, grid=(M // tm, N // tn, nk), in_specs=in_specs, out_specs=o_spec,
        out_shape=jax.ShapeDtypeStruct((M, N), out_dtype),
        scratch_shapes=[pltpu.VMEM((tm, tn), F32)] if nk > 1 else [],
        compiler_params=_params(("parallel", "parallel", "arbitrary")),
    )(*ins)


def _cols(arr, width, block):
    return (arr, width, block)


def _stage(fn, tiles, params, out_tiles, out_sums, *, name, tile=TOK_TILE):
    def tok_spec(shape, width=None, block=0):
        if len(shape) == 2:
            w = shape[1] if width is None else width
            return pl.BlockSpec((tile, w), lambda i: (i, block))
        return pl.BlockSpec((shape[0], tile, shape[2]), lambda i: (0, i, 0))

    arrays, in_specs = [], []
    for t in tiles:
        if isinstance(t, tuple):
            arr, width, block = t
            arrays.append(arr)
            in_specs.append(tok_spec(arr.shape, width, block))
        else:
            arrays.append(t)
            in_specs.append(tok_spec(t.shape))
    n_tok = arrays[0].shape[0] if arrays[0].ndim == 2 else arrays[0].shape[1]
    for p in params:
        arrays.append(p)
        in_specs.append(pl.BlockSpec(p.shape, lambda i, nd=p.ndim: (0,) * nd))
    out_shape = list(out_tiles) + list(out_sums)
    out_specs = [tok_spec(o.shape) for o in out_tiles]
    out_specs += [pl.BlockSpec(o.shape, lambda i, nd=len(o.shape): (0,) * nd) for o in out_sums]
    n_in, n_ot = len(arrays), len(out_tiles)

    def body(*refs):
        res = fn(*[r[...] for r in refs[:n_in]])
        if not isinstance(res, (tuple, list)):
            res = (res,)
        outs = refs[n_in:]
        for o_ref, r in zip(outs[:n_ot], res[:n_ot]):
            o_ref[...] = r.astype(o_ref.dtype)
        i = pl.program_id(0)
        for o_ref, r in zip(outs[n_ot:], res[n_ot:]):
            @pl.when(i == 0)
            def _(o_ref=o_ref, r=r):
                o_ref[...] = r.astype(o_ref.dtype)

            @pl.when(i > 0)
            def _(o_ref=o_ref, r=r):
                o_ref[...] += r.astype(o_ref.dtype)

    res = pl.pallas_call(
        body, name=name, grid=(n_tok // tile,), in_specs=in_specs, out_specs=out_specs, out_shape=out_shape,
        compiler_params=_params(("arbitrary",)),
    )(*arrays)
    return res


def _sds(shape, dtype):
    return jax.ShapeDtypeStruct(tuple(shape), dtype)


def _sigmoid(x):
    return 1.0 / (1.0 + jnp.exp(-x))


def _rms(x, g):
    return x * lax.rsqrt(jnp.mean(x * x, axis=-1, keepdims=True) + EPS) * g


def _norm_bwd_fn(x, dh, dres, g):
    _, vjp = jax.vjp(_rms, x, g)
    dx, dg = vjp(dh)
    return dx + dres, dg


def _mla_a_fn(cq, ckv, g_qa, g_kva):
    return _rms(cq, g_qa), _rms(ckv, g_kva)


def _mla_a_bwd_fn(cq, ckv, dqn, dkvn, g_qa, g_kva):
    _, vjp = jax.vjp(_mla_a_fn, cq, ckv, g_qa, g_kva)
    return vjp((dqn, dkvn))


def _rope(t, cos, sin):
    t1, t2 = t[:, :QK_ROPE // 2], t[:, QK_ROPE // 2:]
    return jnp.concatenate([t1 * cos - t2 * sin, t1 * sin + t2 * cos], axis=-1)


def _mla_b_fn(q_raw, kv_raw, kr, cos, sin, g_qn, g_kn):
    krope = kr[:, :QK_ROPE]
    qs, ks, vs = [], [], []
    for h in range(MLA_HEADS):
        qh = _rms(q_raw[:, h * QK_HEAD:(h + 1) * QK_HEAD], g_qn)
        kvh = kv_raw[:, h * (QK_NOPE + V_HEAD):(h + 1) * (QK_NOPE + V_HEAD)]
        kh = _rms(jnp.concatenate([kvh[:, :QK_NOPE], krope], axis=-1), g_kn)
        qs.append(jnp.concatenate([qh[:, :QK_NOPE], _rope(qh[:, QK_NOPE:], cos, sin)], axis=-1))
        ks.append(jnp.concatenate([kh[:, :QK_NOPE], _rope(kh[:, QK_NOPE:], cos, sin)], axis=-1))
        vs.append(kvh[:, QK_NOPE:])
    return jnp.stack(qs), jnp.stack(ks), jnp.stack(vs)


def _mla_b_bwd_fn(q_raw, kv_raw, kr, cos, sin, dq, dk, dv, g_qn, g_kn):
    _, vjp = jax.vjp(lambda a, b, c, d, e: _mla_b_fn(a, b, c, cos, sin, d, e), q_raw, kv_raw, kr, g_qn, g_kn)
    return vjp((dq, dk, dv))


def _post_fn(a, o_f, o_b, hg, g_hgo):
    o = o_f + o_b
    parts = [a]
    for h in range(HG_HEADS):
        s = slice(h * HG_DK, (h + 1) * HG_DK)
        gate = hg[:, s]
        parts.append(_rms(o[:, s], g_hgo[:, s]) * (gate * _sigmoid(gate)))
    return jnp.concatenate(parts, axis=-1)


def _post_bwd_fn(o_f, o_b, hg, dr, g_hgo):
    def f(o, hg, g):
        return _post_fn(jnp.zeros_like(o), o, jnp.zeros_like(o), hg, g)[:, o.shape[1]:]
    _, vjp = jax.vjp(f, o_f + o_b, hg, g_hgo)
    return vjp(dr)


def _swiglu_fn(gt, up):
    return gt * _sigmoid(gt) * up


def _swiglu_bwd_fn(gt, up, dact):
    _, vjp = jax.vjp(_swiglu_fn, gt, up)
    return vjp(dact)


def _ple_loss_fn(x2, pg, pp, target):
    gate = _sigmoid(pg)
    err = x2 + gate * pp - target
    dx3 = err * (1.0 / err.shape[-1])
    loss = 0.5 * jnp.sum(jnp.mean(err * err, axis=-1, keepdims=True), axis=0, keepdims=True)
    return dx3, dx3 * pp * gate * (1.0 - gate), dx3 * gate, loss


def _attention_fwd(q, k, v):
    H, T, D = q.shape
    DV = v.shape[-1]
    tq = min(ATT_TQ, T)
    scale = D ** -0.5

    def body(q_ref, k_ref, v_ref, o_ref, lse_ref):
        s = _dot_raw(q_ref[0], k_ref[0], "nt") * scale
        m = jnp.max(s, axis=-1, keepdims=True)
        e = jnp.exp(s - m)
        l = jnp.sum(e, axis=-1, keepdims=True)
        o_ref[...] = _dot_raw(e / l, v_ref[0], "nn")
        lse_ref[0] = m + jnp.log(l)

    return pl.pallas_call(
        body, name="attention_fwd", grid=(H, T // tq),
        in_specs=[pl.BlockSpec((1, tq, D), lambda h, i: (h, i, 0)),
                  pl.BlockSpec((1, T, D), lambda h, i: (h, 0, 0)),
                  pl.BlockSpec((1, T, DV), lambda h, i: (h, 0, 0))],
        out_specs=[pl.BlockSpec((tq, DV), lambda h, i: (i, h)),
                   pl.BlockSpec((1, tq, 1), lambda h, i: (h, i, 0))],
        out_shape=[_sds((T, H * DV), F32), _sds((H, T, 1), F32)],
        compiler_params=_params(("parallel", "parallel")),
    )(q, k, v)


def _attention_bwd(q, k, v, o, lse_row, dmix):
    H, T, D = q.shape
    DV = v.shape[-1]
    tk = min(ATT_TK, T)
    scale = D ** -0.5

    def body(q_ref, k_ref, v_ref, o_ref, lse_ref, do_ref, dq_ref, dk_ref, dv_ref, delta_ref):
        j = pl.program_id(1)
        do = do_ref[...]

        @pl.when(j == 0)
        def _():
            delta_ref[...] = lax.dot_general(jnp.ones((8, DV), F32), do * o_ref[...], (((1,), (1,)), ((), ())),
                                             precision=lax.Precision.HIGHEST, preferred_element_type=F32)

        q_all, k_j, v_j = q_ref[0], k_ref[0], v_ref[0]
        st = _dot_raw(k_j, q_all, "nt") * scale
        pt = jnp.exp(st - lse_ref[0])
        dv_ref[0] = _dot_raw(pt, do, "nn")
        dpt = _dot_raw(v_j, do, "nt")
        dst = pt * (dpt - delta_ref[0:1, :]) * scale
        dk_ref[0] = _dot_raw(dst, q_all, "nn")
        dq_part = _dot_raw(dst, k_j, "tn")

        @pl.when(j == 0)
        def _():
            dq_ref[0] = dq_part

        @pl.when(j > 0)
        def _():
            dq_ref[0] += dq_part

    return pl.pallas_call(
        body, name="attention_bwd", grid=(H, T // tk),
        in_specs=[pl.BlockSpec((1, T, D), lambda h, j: (h, 0, 0)),
                  pl.BlockSpec((1, tk, D), lambda h, j: (h, j, 0)),
                  pl.BlockSpec((1, tk, DV), lambda h, j: (h, j, 0)),
                  pl.BlockSpec((T, DV), lambda h, j: (0, h)),
                  pl.BlockSpec((1, 1, T), lambda h, j: (h, 0, 0)),
                  pl.BlockSpec((T, DV), lambda h, j: (0, h))],
        out_specs=[pl.BlockSpec((1, T, D), lambda h, j: (h, 0, 0)),
                   pl.BlockSpec((1, tk, D), lambda h, j: (h, j, 0)),
                   pl.BlockSpec((1, tk, DV), lambda h, j: (h, j, 0))],
        out_shape=[_sds((H, T, D), F32), _sds((H, T, D), F32), _sds((H, T, DV), F32)],
        scratch_shapes=[pltpu.VMEM((8, T), F32)],
        compiler_params=_params(("parallel", "arbitrary")),
    )(q, k, v, o, lse_row, dmix)


def _gla_block(hq, hf, hi, lower, st_in, *, rev, dot):
    rows, dk = hq.shape
    G, C = rows // CHUNK, CHUNK
    q = hq * _sigmoid(hq)
    f = lower + (1.0 - lower) * _sigmoid(hf)
    k = 1.0 - f
    logf = jnp.log(f)
    q3, k3, v3, lf3 = (t.reshape(G, C, dk) for t in (q, k, hi, logf))
    r = lax.broadcasted_iota(jnp.int32, (C, C), 0)
    c = lax.broadcasted_iota(jnp.int32, (C, C), 1)
    tri = ((r <= c) if rev else (r >= c)).astype(F32)
    b = lax.dot_general(jnp.broadcast_to(tri, (G, C, C)), lf3, (((2,), (1,)), ((0,), (0,))),
                        precision=lax.Precision.HIGHEST, preferred_element_type=F32)
    tpos = lax.broadcasted_iota(jnp.int32, (1, C, 1), 1)
    first_half = (tpos >= C // 2) if rev else (tpos <= C // 2 - 1)
    b_mid = jnp.sum(jnp.where(first_half, lf3, 0.0), axis=1, keepdims=True)
    b_last = jnp.sum(lf3, axis=1, keepdims=True)
    a = dot(q3 * jnp.exp(b - b_mid), k3 * jnp.exp(b_mid - b), "nt") * tri
    o_intra = dot(a, v3, "nn")
    kv_t = dot(v3, k3 * jnp.exp(b_last - b), "tn")
    decay = jnp.exp(b_last)
    qd = q3 * jnp.exp(b)
    st = st_in
    o_inter = [None] * G
    for g in (reversed(range(G)) if rev else range(G)):
        o_inter[g] = dot(qd[g], st, "nt")
        st = st * decay[g] + kv_t[g]
    o = o_intra.reshape(rows, dk) + jnp.concatenate(o_inter, axis=0)
    return o, st


def _gla_fwd(z, lower3, *, rev, col_q, col_f, col_v):
    T = z.shape[0]
    rows = min(GLA_GROUP * CHUNK, T)
    nb = T // rows
    blk = (lambda n: nb - 1 - n) if rev else (lambda n: n)

    def body(hq_ref, hf_ref, hi_ref, low_ref, o_ref, st_out_ref, st_ref):
        @pl.when(pl.program_id(1) == 0)
        def _():
            st_ref[...] = jnp.zeros_like(st_ref)

        st_in = st_ref[...]
        st_out_ref[0, 0] = st_in
        o, st = _gla_block(hq_ref[...], hf_ref[...], hi_ref[...], low_ref[0], st_in, rev=rev, dot=_dot_raw)
        o_ref[...] = o
        st_ref[...] = st

    def zspec(col):
        return pl.BlockSpec((rows, HG_DK), lambda h, n: (blk(n), col + h))

    return pl.pallas_call(
        body, name="gla_fwd_rev" if rev else "gla_fwd", grid=(HG_HEADS, nb),
        in_specs=[zspec(col_q), zspec(col_f), zspec(col_v), pl.BlockSpec((1, 1, HG_DK), lambda h, n: (h, 0, 0))],
        out_specs=[pl.BlockSpec((rows, HG_DK), lambda h, n: (blk(n), h)),
                   pl.BlockSpec((1, 1, HG_DK, HG_DK), lambda h, n: (h, blk(n), 0, 0))],
        out_shape=[_sds((T, HG_HEADS * HG_DK), F32), _sds((HG_HEADS, nb, HG_DK, HG_DK), F32)],
        scratch_shapes=[pltpu.VMEM((HG_DK, HG_DK), F32)],
        compiler_params=_params(("parallel", "arbitrary")),
    )(z, z, z, lower3)


def _gla_bwd(z, lower3, states, do, prev, *, rev, col_q, col_f, col_v):
    T = z.shape[0]
    rows = min(GLA_GROUP * CHUNK, T)
    nb = T // rows
    blk = (lambda n: n) if rev else (lambda n: nb - 1 - n)
    has_prev = prev is not None
    fn = functools.partial(_gla_block, rev=rev, dot=_bdot)

    def body(*refs):
        hq_ref, hf_ref, hi_ref, low_ref, st_ref, do_ref = refs[:6]
        rest = refs[6:]
        if has_prev:
            pq_ref, pi_ref = rest[:2]
            rest = rest[2:]
        dhq_ref, dhi_ref, dhf_ref, dlow_ref, dst_ref = rest
        n = pl.program_id(1)

        @pl.when(n == 0)
        def _():
            dst_ref[...] = jnp.zeros_like(dst_ref)

        _, vjp = jax.vjp(fn, hq_ref[...], hf_ref[...], hi_ref[...], low_ref[0], st_ref[0, 0])
        dhq, dhf, dhi, dlow, dst = vjp((do_ref[...], dst_ref[...]))
        dst_ref[...] = dst
        if has_prev:
            dhq = dhq + pq_ref[...]
            dhi = dhi + pi_ref[...]
        dhq_ref[...] = dhq.astype(dhq_ref.dtype)
        dhi_ref[...] = dhi.astype(dhi_ref.dtype)
        dhf_ref[...] = dhf.astype(dhf_ref.dtype)

        @pl.when(n == 0)
        def _():
            dlow_ref[0] = dlow

        @pl.when(n > 0)
        def _():
            dlow_ref[0] += dlow

    def zspec(col):
        return pl.BlockSpec((rows, HG_DK), lambda h, n: (blk(n), col + h))

    hspec = pl.BlockSpec((rows, HG_DK), lambda h, n: (blk(n), h))
    in_specs = [zspec(col_q), zspec(col_f), zspec(col_v), pl.BlockSpec((1, 1, HG_DK), lambda h, n: (h, 0, 0)),
                pl.BlockSpec((1, 1, HG_DK, HG_DK), lambda h, n: (h, blk(n), 0, 0)), hspec]
    ins = [z, z, z, lower3, states, do]
    if has_prev:
        in_specs += [hspec, hspec]
        ins += list(prev)
    wide = HG_HEADS * HG_DK
    acc_dtype = BF16 if has_prev else F32
    return pl.pallas_call(
        body, name="gla_bwd_rev" if rev else "gla_bwd", grid=(HG_HEADS, nb),
        in_specs=in_specs,
        out_specs=[hspec, hspec, hspec, pl.BlockSpec((1, 1, HG_DK), lambda h, n: (h, 0, 0))],
        out_shape=[_sds((T, wide), acc_dtype), _sds((T, wide), acc_dtype), _sds((T, wide), BF16),
                   _sds((HG_HEADS, 1, HG_DK), F32)],
        scratch_shapes=[pltpu.VMEM((HG_DK, HG_DK), F32)],
        compiler_params=_params(("parallel", "arbitrary")),
    )(*ins)


def _lower_fn(lb):
    e = jnp.exp(lb - jnp.max(lb, axis=0, keepdims=True))
    return (e / jnp.sum(e, axis=0, keepdims=True))[0]


def _lower_bounds(lb):
    def body(lb_ref, o_ref):
        o_ref[...] = _lower_fn(lb_ref[...])
    return pl.pallas_call(body, name="lower_bounds", out_shape=_sds(lb.shape[1:], F32))(lb)


def _lower_bounds_bwd(lb, dlower):
    def body(lb_ref, d_ref, o_ref):
        _, vjp = jax.vjp(_lower_fn, lb_ref[...])
        o_ref[...] = vjp(d_ref[...])[0]
    return pl.pallas_call(body, name="lower_bounds_bwd", out_shape=_sds(lb.shape, F32))(lb, dlower)


def _row_tile(r):
    for t in (512, 256, 128, 64, 32, 16, 8):
        if r % t == 0:
            return t
    return r


def _sum4(own, recv, *, name):
    R, C = own.shape
    tr = _row_tile(R)

    def body(o_ref, r_ref, out_ref):
        out_ref[...] = ((o_ref[...].astype(F32) + r_ref[0].astype(F32)) + r_ref[1].astype(F32)) + r_ref[2].astype(F32)

    return pl.pallas_call(
        body, name=name, grid=(R // tr,),
        in_specs=[pl.BlockSpec((tr, C), lambda i: (i, 0)), pl.BlockSpec((3, tr, C), lambda i: (0, i, 0))],
        out_specs=pl.BlockSpec((tr, C), lambda i: (i, 0)), out_shape=_sds((R, C), F32),
        compiler_params=_params(("parallel",)),
    )(own, recv)


def _adamw_math(w, g, m, v):
    m = ADAM_B1 * m + (1.0 - ADAM_B1) * g
    v = ADAM_B2 * v + (1.0 - ADAM_B2) * (g * g)
    m_hat = m / (1.0 - ADAM_B1 ** ADAM_STEP)
    v_hat = v / (1.0 - ADAM_B2 ** ADAM_STEP)
    delta = -ADAM_LR * (m_hat / (jnp.sqrt(v_hat) + ADAM_EPS) + ADAM_WD * w)
    return delta, m, v


def _adamw(w, g_a, g_b, m, v, *, name):
    R, C = w.shape
    tr = _row_tile(R)
    two = g_b is not None

    def body(*refs):
        w_ref, ga_ref = refs[0], refs[1]
        rest = refs[2:]
        g = ga_ref[...]
        if two:
            g = g + rest[0][...]
            rest = rest[1:]
        m_ref, v_ref, g_out, d_out, m_out, v_out = rest
        delta, m_new, v_new = _adamw_math(w_ref[...], g, m_ref[...], v_ref[...])
        g_out[...] = g
        d_out[...] = delta
        m_out[...] = m_new
        v_out[...] = v_new

    spec = pl.BlockSpec((tr, C), lambda i: (i, 0))
    ins = [w, g_a] + ([g_b] if two else []) + [m, v]
    return pl.pallas_call(
        body, name=name, grid=(R // tr,), in_specs=[spec] * len(ins), out_specs=[spec] * 4,
        out_shape=[_sds((R, C), F32)] * 4, compiler_params=_params(("parallel",)),
    )(*ins)


def _chip_peers():
    x, y, c = lax.axis_index("x"), lax.axis_index("y"), lax.axis_index("c")
    return (x, y, c), 2 * x + y, [(1 - x, y), (x, 1 - y), (1 - x, 1 - y)]


_ANY = pl.BlockSpec(memory_space=pl.ANY)


def _gather_chips(shards):
    n = len(shards)

    def body(*refs):
        ins, outs = refs[:n], refs[n:2 * n]
        send_sems, recv_sems, local_sems = refs[2 * n:]
        (x, y, c), me, chips = _chip_peers()
        copies = []
        for t in range(n):
            cp = pltpu.make_async_copy(ins[t], outs[t].at[me], local_sems.at[t])
            cp.start()
            copies.append(cp)
            for k, (px, py) in enumerate(chips):
                cp = pltpu.make_async_remote_copy(ins[t], outs[t].at[me], send_sems.at[t, k], recv_sems.at[t, k],
                                                  device_id=(px, py, c), device_id_type=MESH)
                cp.start()
                copies.append(cp)
        for cp in copies:
            cp.wait()

    return pl.pallas_call(
        body, name="gather_weights", in_specs=[_ANY] * n, out_specs=[_ANY] * n,
        out_shape=[_sds((N_CHIPS,) + s.shape, s.dtype) for s in shards],
        scratch_shapes=[pltpu.SemaphoreType.DMA((n, 3)), pltpu.SemaphoreType.DMA((n, 3)), pltpu.SemaphoreType.DMA((n,))],
        compiler_params=_params(),
    )(*shards)


def _scatter_chips(grads):
    n = len(grads)

    def body(*refs):
        ins, owns, recvs = refs[:n], refs[n:2 * n], refs[2 * n:3 * n]
        send_sems, recv_sems, local_sems = refs[3 * n:]
        (x, y, c), me, chips = _chip_peers()
        copies = []
        for t in range(n):
            cp = pltpu.make_async_copy(ins[t].at[me], owns[t], local_sems.at[t])
            cp.start()
            copies.append(cp)
            for k, (px, py) in enumerate(chips):
                cp = pltpu.make_async_remote_copy(ins[t].at[2 * px + py], recvs[t].at[k], send_sems.at[t, k],
                                                  recv_sems.at[t, k], device_id=(px, py, c), device_id_type=MESH)
                cp.start()
                copies.append(cp)
        for cp in copies:
            cp.wait()

    return pl.pallas_call(
        body, name="scatter_grads", in_specs=[_ANY] * n, out_specs=[_ANY] * (2 * n),
        out_shape=[_sds(g.shape[1:], g.dtype) for g in grads] + [_sds((3,) + g.shape[1:], g.dtype) for g in grads],
        scratch_shapes=[pltpu.SemaphoreType.DMA((n, 3)), pltpu.SemaphoreType.DMA((n, 3)), pltpu.SemaphoreType.DMA((n,))],
        compiler_params=_params(),
    )(*grads)


def _swap_sibling(parts):
    n = len(parts)

    def body(*refs):
        ins, outs = refs[:n], refs[n:2 * n]
        send_sems, recv_sems = refs[2 * n:]
        x, y, c = lax.axis_index("x"), lax.axis_index("y"), lax.axis_index("c")
        copies = []
        for t in range(n):
            cp = pltpu.make_async_remote_copy(ins[t], outs[t], send_sems.at[t], recv_sems.at[t],
                                              device_id=(x, y, 1 - c), device_id_type=MESH)
            cp.start()
            copies.append(cp)
        for cp in copies:
            cp.wait()

    return pl.pallas_call(
        body, name="swap_sibling", in_specs=[_ANY] * n, out_specs=[_ANY] * n,
        out_shape=[_sds(p.shape, p.dtype) for p in parts],
        scratch_shapes=[pltpu.SemaphoreType.DMA((n,)), pltpu.SemaphoreType.DMA((n,))],
        compiler_params=_params(),
    )(*parts)


def _allreduce_small(pack):
    R, C = pack.shape

    def body(in_ref, out_ref, slots, send_sems, recv_sems):
        x, y, c = lax.axis_index("x"), lax.axis_index("y"), lax.axis_index("c")
        me = 4 * x + 2 * y + c
        slots[me] = in_ref[...]
        copies = []
        for k in range(1, N_DEV):
            peer = (x ^ ((k >> 2) & 1), y ^ ((k >> 1) & 1), c ^ (k & 1))
            cp = pltpu.make_async_remote_copy(in_ref, slots.at[me], send_sems.at[k - 1], recv_sems.at[k - 1],
                                              device_id=peer, device_id_type=MESH)
            cp.start()
            copies.append(cp)
        for cp in copies:
            cp.wait()
        acc = slots[0]
        for d in range(1, N_DEV):
            acc = acc + slots[d]
        out_ref[...] = acc

    return pl.pallas_call(
        body, name="allreduce_small", out_shape=_sds((R, C), F32),
        in_specs=[pl.BlockSpec(memory_space=pltpu.VMEM)], out_specs=pl.BlockSpec(memory_space=pltpu.VMEM),
        scratch_shapes=[pltpu.VMEM((N_DEV, R, C), F32), pltpu.SemaphoreType.DMA((N_DEV - 1,)),
                        pltpu.SemaphoreType.DMA((N_DEV - 1,))],
        compiler_params=_params(),
    )(pack)


_Z_CQ, _Z_CKV, _Z_HQ, _Z_HFF, _Z_HFB, _Z_HI, _Z_HG, _Z_KR, _Z_END = 0, 256, 512, 1024, 1536, 2048, 2560, 3072, 3200


def _to_z_layout(w):
    pad = jnp.zeros((w.shape[0], _Z_END - _Z_KR - QK_ROPE), w.dtype)
    return jnp.concatenate([w[:, :512], w[:, 512 + QK_ROPE:], w[:, 512:512 + QK_ROPE], pad], axis=1)


def _from_z_layout(w):
    return jnp.concatenate([w[:, :512], w[:, _Z_KR:_Z_KR + QK_ROPE], w[:, 512:_Z_KR]], axis=1)


def _col_shards_to_full(g):
    return jnp.transpose(g, (1, 0, 2)).reshape(g.shape[1], -1)


def _full_to_col_shards(w):
    r, c = w.shape
    return jnp.transpose(w.reshape(r, N_CHIPS, c // N_CHIPS), (1, 0, 2))


def _full_to_row_shards(w):
    r, c = w.shape
    return w.reshape(N_CHIPS, r // N_CHIPS, c)


def kernel(x, p, positions, g_mix, w_in, g_qa, g_kva, w_qb, w_kvb, g_qn, g_kn, lb_param, g_hgo, w_o, g_ffn, w_gate, w_up, w_down, g_ple, w_ple_gate, w_ple_proj, loss_target, m_g_mix, m_w_in, m_g_qa, m_g_kva, m_w_qb, m_w_kvb, m_g_qn, m_g_kn, m_lb_param, m_g_hgo, m_w_o, m_g_ffn, m_w_gate, m_w_up, m_w_down, m_g_ple, m_w_ple_gate, m_w_ple_proj, v_g_mix, v_w_in, v_g_qa, v_g_kva, v_w_qb, v_w_kvb, v_g_qn, v_g_kn, v_lb_param, v_g_hgo, v_w_o, v_g_ffn, v_w_gate, v_w_up, v_w_down, v_g_ple, v_w_ple_gate, v_w_ple_proj):
    w_named = dict(g_mix=g_mix, w_in=w_in, g_qa=g_qa, g_kva=g_kva, w_qb=w_qb, w_kvb=w_kvb, g_qn=g_qn, g_kn=g_kn,
                   lb_param=lb_param, g_hgo=g_hgo, w_o=w_o, g_ffn=g_ffn, w_gate=w_gate, w_up=w_up, w_down=w_down,
                   g_ple=g_ple, w_ple_gate=w_ple_gate, w_ple_proj=w_ple_proj)
    m_named = dict(g_mix=m_g_mix, w_in=m_w_in, g_qa=m_g_qa, g_kva=m_g_kva, w_qb=m_w_qb, w_kvb=m_w_kvb, g_qn=m_g_qn,
                   g_kn=m_g_kn, lb_param=m_lb_param, g_hgo=m_g_hgo, w_o=m_w_o, g_ffn=m_g_ffn, w_gate=m_w_gate,
                   w_up=m_w_up, w_down=m_w_down, g_ple=m_g_ple, w_ple_gate=m_w_ple_gate, w_ple_proj=m_w_ple_proj)
    v_named = dict(g_mix=v_g_mix, w_in=v_w_in, g_qa=v_g_qa, g_kva=v_g_kva, w_qb=v_w_qb, w_kvb=v_w_kvb, g_qn=v_g_qn,
                   g_kn=v_g_kn, lb_param=v_lb_param, g_hgo=v_g_hgo, w_o=v_w_o, g_ffn=v_g_ffn, w_gate=v_w_gate,
                   w_up=v_w_up, w_down=v_w_down, g_ple=v_g_ple, w_ple_gate=v_w_ple_gate, w_ple_proj=v_w_ple_proj)
    order = list(w_named)
    col_sharded = ("w_in", "w_qb", "w_kvb", "w_gate", "w_up", "w_ple_proj")
    row_sharded = ("w_o", "w_down", "w_ple_gate")
    big = col_sharded + row_sharded

    x2d, p2d, tgt = x[0], p[0, 0], loss_target[0]
    T, D = x2d.shape

    lb_flat = lb_param.reshape(-1, lb_param.shape[-1])
    gathered = _gather_chips([w_named[n][0].astype(BF16) for n in big] + [lb_flat])
    full = {}
    for n, g in zip(big, gathered[:-1]):
        full[n] = _col_shards_to_full(g) if n in col_sharded else g.reshape(-1, g.shape[-1])
    lb_full = _col_shards_to_full(gathered[-1]).reshape(lb_param.shape[0], lb_param.shape[1], -1)
    w_in_z = _to_z_layout(full["w_in"])

    inv_freq = ROPE_THETA ** (-jnp.arange(0, QK_ROPE, 2, dtype=F32) / QK_ROPE)
    ang = positions[0].astype(F32)[:, None] * inv_freq
    cos, sin = jnp.cos(ang), jnp.sin(ang)
    g_hgo_row = g_hgo.reshape(1, -1)

    h1 = _stage(_rms, [x2d], [g_mix], [_sds((T, D), BF16)], [], name="norm_mix")[0]
    z = _mm(h1, w_in_z, name="in_proj")
    qn, kvn = _stage(_mla_a_fn, [_cols(z, 256, 0), _cols(z, 256, 1)], [g_qa, g_kva],
                     [_sds((T, 256), BF16), _sds((T, 256), BF16)], [], name="mla_latent_norm")
    q_raw = _mm(qn, full["w_qb"], name="q_up")
    kv_raw = _mm(kvn, full["w_kvb"], name="kv_up")
    kr = _cols(z, LANE, _Z_KR // LANE)
    q, k, v = _stage(_mla_b_fn, [q_raw, kv_raw, kr, cos, sin], [g_qn, g_kn],
                     [_sds((MLA_HEADS, T, QK_HEAD), BF16), _sds((MLA_HEADS, T, QK_HEAD), BF16),
                      _sds((MLA_HEADS, T, V_HEAD), BF16)], [], name="mla_qk_norm_rope")
    att, lse = _attention_fwd(q, k, v)

    lower = _lower_bounds(lb_full)
    lower3 = lower.reshape(2, HG_HEADS, 1, HG_DK)
    cq_blk, cf_blk, cb_blk, cv_blk = _Z_HQ // LANE, _Z_HFF // LANE, _Z_HFB // LANE, _Z_HI // LANE
    o_f, st_f = _gla_fwd(z, lower3[0], rev=False, col_q=cq_blk, col_f=cf_blk, col_v=cv_blk)
    o_b, st_b = _gla_fwd(z, lower3[1], rev=True, col_q=cq_blk, col_f=cb_blk, col_v=cv_blk)
    hg = _cols(z, 512, _Z_HG // 512)
    mix = _stage(_post_fn, [att, o_f, o_b, hg], [g_hgo_row], [_sds((T, att.shape[1] + o_f.shape[1]), BF16)], [],
                 name="mix_out")[0]
    x1 = _mm(mix, full["w_o"], add=x2d, name="out_proj")
    h2 = _stage(_rms, [x1], [g_ffn], [_sds((T, D), BF16)], [], name="norm_ffn")[0]
    gt = _mm(h2, full["w_gate"], name="ffn_gate")
    up = _mm(h2, full["w_up"], name="ffn_up")
    act = _stage(_swiglu_fn, [gt, up], [], [_sds(gt.shape, BF16)], [], name="swiglu")[0]
    x2 = _mm(act, full["w_down"], add=x1, name="ffn_down")
    h3 = _stage(_rms, [x2], [g_ple], [_sds((T, D), BF16)], [], name="norm_ple")[0]
    pg = _mm(h3, full["w_ple_gate"], name="ple_gate")
    pp = _mm(p2d, full["w_ple_proj"], name="ple_proj")
    dx3, dpg, dpp, loss_part = _stage(_ple_loss_fn, [x2, pg, pp, tgt], [],
                                      [_sds((T, D), F32), _sds((T, D), BF16), _sds((T, D), BF16)],
                                      [_sds((1, 1), F32)], name="ple_loss")

    grads = {}
    grads["w_ple_proj"] = _mm(p2d, dpp, ta=True, out_dtype=BF16, name="d_w_ple_proj")
    grads["w_ple_gate"] = _mm(h3, dpg, ta=True, out_dtype=BF16, name="d_w_ple_gate")
    dh3 = _mm(dpg, full["w_ple_gate"], tb=True, name="d_h3")
    dx2, grads["g_ple"] = _stage(_norm_bwd_fn, [x2, dh3, dx3], [g_ple], [_sds((T, D), F32)], [_sds((1, D), F32)],
                                 name="norm_ple_bwd")
    dact = _mm(dx2, full["w_down"], tb=True, name="d_act")
    grads["w_down"] = _mm(act, dx2, ta=True, out_dtype=BF16, name="d_w_down")
    dgt, dup = _stage(_swiglu_bwd_fn, [gt, up, dact], [], [_sds(gt.shape, BF16), _sds(gt.shape, BF16)], [],
                      name="swiglu_bwd")
    grads["w_gate"] = _mm(h2, dgt, ta=True, out_dtype=BF16, name="d_w_gate")
    grads["w_up"] = _mm(h2, dup, ta=True, out_dtype=BF16, name="d_w_up")
    dh2 = _mm(dgt, full["w_gate"], tb=True, name="d_h2_gate")
    dh2 = _mm(dup, full["w_up"], tb=True, add=dh2, name="d_h2_up")
    dx1, grads["g_ffn"] = _stage(_norm_bwd_fn, [x1, dh2, dx2], [g_ffn], [_sds((T, D), F32)], [_sds((1, D), F32)],
                                 name="norm_ffn_bwd")
    dmix = _mm(dx1, full["w_o"], tb=True, name="d_mix")
    grads["w_o"] = _mm(mix, dx1, ta=True, out_dtype=BF16, name="d_w_o")

    half = MLA_HEADS * V_HEAD
    do, dhg, dg_hgo = _stage(_post_bwd_fn, [o_f, o_b, hg, _cols(dmix, half, 1)], [g_hgo_row],
                             [_sds((T, half), F32), _sds((T, half), BF16)], [_sds((1, half), F32)], name="mix_out_bwd")
    grads["g_hgo"] = dg_hgo
    dhq_f, dhi_f, dhf_f, dlow_f = _gla_bwd(z, lower3[0], st_f, do, None, rev=False,
                                           col_q=cq_blk, col_f=cf_blk, col_v=cv_blk)
    dhq, dhi, dhf_b, dlow_b = _gla_bwd(z, lower3[1], st_b, do, (dhq_f, dhi_f), rev=True,
                                       col_q=cq_blk, col_f=cb_blk, col_v=cv_blk)

    lse_row = lse.reshape(MLA_HEADS, 1, T)
    dq, dk, dv = _attention_bwd(q, k, v, att, lse_row, dmix)
    dq_raw, dkv_raw, dkr, grads["g_qn"], grads["g_kn"] = _stage(
        _mla_b_bwd_fn, [q_raw, kv_raw, kr, cos, sin, dq, dk, dv], [g_qn, g_kn],
        [_sds(q_raw.shape, BF16), _sds(kv_raw.shape, BF16), _sds((T, LANE), BF16)],
        [_sds(g_qn.shape, F32), _sds(g_kn.shape, F32)], name="mla_qk_norm_rope_bwd")
    grads["w_qb"] = _mm(qn, dq_raw, ta=True, out_dtype=BF16, name="d_w_qb")
    grads["w_kvb"] = _mm(kvn, dkv_raw, ta=True, out_dtype=BF16, name="d_w_kvb")
    dqn = _mm(dq_raw, full["w_qb"], tb=True, name="d_qn")
    dkvn = _mm(dkv_raw, full["w_kvb"], tb=True, name="d_kvn")
    dcq, dckv, grads["g_qa"], grads["g_kva"] = _stage(
        _mla_a_bwd_fn, [_cols(z, 256, 0), _cols(z, 256, 1), dqn, dkvn], [g_qa, g_kva],
        [_sds((T, 256), BF16), _sds((T, 256), BF16)], [_sds(g_qa.shape, F32), _sds(g_kva.shape, F32)],
        name="mla_latent_norm_bwd")
    dz = jnp.concatenate([dcq, dckv, dhq, dhf_f, dhf_b, dhi, dhg, dkr], axis=1)
    grads["w_in"] = _from_z_layout(_mm(h1, dz, ta=True, out_dtype=BF16, name="d_w_in"))
    dh1 = _mm(dz, w_in_z, tb=True, name="d_h1")
    grad_x, grads["g_mix"] = _stage(_norm_bwd_fn, [x2d, dh1, dx1], [g_mix], [_sds((T, D), F32)], [_sds((1, D), F32)],
                                    name="norm_mix_bwd")

    shard_major = [_full_to_col_shards(grads[n]) if n in col_sharded else _full_to_row_shards(grads[n]) for n in big]
    scattered = _scatter_chips(shard_major)
    partial_sums = [_sum4(own, recv, name="sum_" + n)
                    for n, own, recv in zip(big, scattered[:len(big)], scattered[len(big):])]
    sibling_sums = _swap_sibling(partial_sums)

    small = ("g_mix", "g_qa", "g_kva", "g_qn", "g_kn", "g_hgo", "g_ffn", "g_ple")
    small_all = small + ("lb_param",)
    width = -(-max(w_named[n].size for n in small_all) // LANE) * LANE

    def row(a):
        a = a.reshape(1, -1)
        return jnp.pad(a, ((0, 0), (0, width - a.shape[1])))

    dlower = jnp.concatenate([dlow_f.reshape(1, -1), dlow_b.reshape(1, -1)], axis=0)
    pack = jnp.concatenate([row(grads[n]) for n in small] + [row(dlower[0]), row(dlower[1]), row(loss_part)]
                           + [jnp.zeros((5, width), F32)], axis=0)
    red = _allreduce_small(pack)
    loss = red[10, 0]
    dlower_sum = red[8:10, :lb_full.shape[-1]]
    dlb_full = _lower_bounds_bwd(lb_full, dlower_sum)
    chip = 2 * lax.axis_index("x") + lax.axis_index("y")
    fshard = lb_param.shape[-1]
    dlb = lax.dynamic_slice_in_dim(dlb_full, chip * fshard, fshard, axis=2)

    out_g, out_d, out_m, out_v = {}, {}, {}, {}
    for n, mine, theirs in zip(big, partial_sums, sibling_sums):
        out_g[n], out_d[n], out_m[n], out_v[n] = (
            t[None] for t in _adamw(w_named[n][0], mine, theirs, m_named[n][0], v_named[n][0], name="adamw_" + n))
    g_rows = [red[i:i + 1] for i in range(len(small))] + [row(dlb)]
    g_pack = jnp.concatenate(g_rows + [jnp.zeros((16 - len(g_rows), width), F32)], axis=0)

    def packed(named):
        rows = [row(named[n]) for n in small_all]
        return jnp.concatenate(rows + [jnp.ones((16 - len(rows), width), F32)], axis=0)

    s_g, s_d, s_m, s_v = _adamw(packed(w_named), g_pack, None, packed(m_named), packed(v_named), name="adamw_small")
    for i, n in enumerate(small_all):
        size = w_named[n].size
        for src, dst in ((s_g, out_g), (s_d, out_d), (s_m, out_m), (s_v, out_v)):
            dst[n] = src[i, :size].reshape(w_named[n].shape)

    return (loss, grad_x[None], *[out_g[n] for n in order], *[out_d[n] for n in order],
            *[out_m[n] for n in order], *[out_v[n] for n in order])
```

```python
import functools

import jax
import jax.numpy as jnp
from jax import lax
from jax.experimental import pallas as pl
from jax.experimental.pallas import tpu as pltpu

F32 = jnp.float32
BF16 = jnp.bfloat16
MESH = pl.DeviceIdType.MESH

EPS = 1e-6
ROPE_THETA = 10000.0
MLA_HEADS = 4
QK_NOPE = 128
QK_ROPE = 64
QK_HEAD = QK_NOPE + QK_ROPE
V_HEAD = 128
HG_HEADS = 4
HG_DK = 128
CHUNK = 64
ADAM_LR = 0.001
ADAM_B1 = 0.9
ADAM_B2 = 0.999
ADAM_EPS = 1e-08
ADAM_WD = 0.01
ADAM_STEP = 10

LANE = 128
VMEM_LIMIT = 56 * 1024 * 1024
TOK_TILE = 256
GLA_GROUP = 8
ATT_TQ = 256
ATT_TK = 256
N_CHIPS = 4
N_DEV = 8


def _params(dims=None, **kw):
    return pltpu.CompilerParams(dimension_semantics=dims, vmem_limit_bytes=VMEM_LIMIT, **kw)


def _tile(n, target):
    if n <= target:
        return n
    best = None
    for d in range(LANE, target + 1, LANE):
        if n % d == 0:
            best = d
    return best if best is not None else n


def _dot_raw(a, b, kind):
    nb = a.ndim - 2
    batch = ((0,), (0,)) if nb else ((), ())
    ca = nb if kind == "tn" else nb + 1
    cb = nb + 1 if kind == "nt" else nb
    return lax.dot_general(a.astype(BF16), b.astype(BF16), (((ca,), (cb,)), batch), preferred_element_type=F32)


@functools.partial(jax.custom_vjp, nondiff_argnums=(2,))
def _bdot(a, b, kind):
    return _dot_raw(a, b, kind)


def _bdot_fwd(a, b, kind):
    return _dot_raw(a, b, kind), (a, b)


def _bdot_bwd(kind, res, g):
    a, b = res
    if kind == "nn":
        da, db = _bdot(g, b, "nt"), _bdot(a, g, "tn")
    elif kind == "nt":
        da, db = _bdot(g, b, "nn"), _bdot(g, a, "tn")
    else:
        da, db = _bdot(b, g, "nt"), _bdot(a, g, "nn")
    return da.astype(a.dtype), db.astype(b.dtype)


_bdot.defvjp(_bdot_fwd, _bdot_bwd)


def _mm(a, b, *, name, ta=False, tb=False, add=None, out_dtype=F32):
    K, M = a.shape if ta else a.shape[::-1]
    N, Kb = b.shape if tb else b.shape[::-1]
    assert K == Kb, (a.shape, b.shape, ta, tb)
    tm, tn, tk = _tile(M, 512), _tile(N, 1536), _tile(K, 1536)
    nk = K // tk
    kind = "tn" if ta else ("nt" if tb else "nn")
    assert not (ta and tb)
    a_spec = pl.BlockSpec((tk, tm), lambda i, j, k: (k, i)) if ta else pl.BlockSpec((tm, tk), lambda i, j, k: (i, k))
    b_spec = pl.BlockSpec((tn, tk), lambda i, j, k: (j, k)) if tb else pl.BlockSpec((tk, tn), lambda i, j, k: (k, j))
    o_spec = pl.BlockSpec((tm, tn), lambda i, j, k: (i, j))
    has_add = add is not None

    def body(*refs):
        a_ref, b_ref = refs[0], refs[1]
        add_ref = refs[2] if has_add else None
        o_ref = refs[3] if has_add else refs[2]
        part = _dot_raw(a_ref[...], b_ref[...], kind)
        if nk == 1:
            if has_add:
                part = part + add_ref[...].astype(F32)
            o_ref[...] = part.astype(o_ref.dtype)
            return
        acc_ref = refs[-1]
        k = pl.program_id(2)

        @pl.when(k == 0)
        def _():
            acc_ref[...] = part

        @pl.when(k > 0)
        def _():
            acc_ref[...] += part

        @pl.when(k == nk - 1)
        def _():
            r = acc_ref[...]
            if has_add:
                r = r + add_ref[...].astype(F32)
            o_ref[...] = r.astype(o_ref.dtype)

    ins = [a, b] + ([add] if has_add else [])
    in_specs = [a_spec, b_spec] + ([o_spec] if has_add else [])
    return pl.pallas_call(
        body, name=name, grid=(M // tm, N // tn, nk), in_specs=in_specs, out_specs=o_spec,
        out_shape=jax.ShapeDtypeStruct((M, N), out_dtype),
        scratch_shapes=[pltpu.VMEM((tm, tn), F32)] if nk > 1 else [],
        compiler_params=_params(("parallel", "parallel", "arbitrary")),
    )(*ins)


def _cols(arr, width, block):
    return (arr, width, block)


def _stage(fn, tiles, params, out_tiles, out_sums, *, name, tile=TOK_TILE):
    def tok_spec(shape, width=None, block=0):
        if len(shape) == 2:
            w = shape[1] if width is None else width
            return pl.BlockSpec((tile, w), lambda i: (i, block))
        return pl.BlockSpec((shape[0], tile, shape[2]), lambda i: (0, i, 0))

    arrays, in_specs = [], []
    for t in tiles:
        if isinstance(t, tuple):
            arr, width, block = t
            arrays.append(arr)
            in_specs.append(tok_spec(arr.shape, width, block))
        else:
            arrays.append(t)
            in_specs.append(tok_spec(t.shape))
    n_tok = arrays[0].shape[0] if arrays[0].ndim == 2 else arrays[0].shape[1]
    for p in params:
        arrays.append(p)
        in_specs.append(pl.BlockSpec(p.shape, lambda i, nd=p.ndim: (0,) * nd))
    out_shape = list(out_tiles) + list(out_sums)
    out_specs = [tok_spec(o.shape) for o in out_tiles]
    out_specs += [pl.BlockSpec(o.shape, lambda i, nd=len(o.shape): (0,) * nd) for o in out_sums]
    n_in, n_ot = len(arrays), len(out_tiles)

    def body(*refs):
        res = fn(*[r[...] for r in refs[:n_in]])
        if not isinstance(res, (tuple, list)):
            res = (res,)
        outs = refs[n_in:]
        for o_ref, r in zip(outs[:n_ot], res[:n_ot]):
            o_ref[...] = r.astype(o_ref.dtype)
        i = pl.program_id(0)
        for o_ref, r in zip(outs[n_ot:], res[n_ot:]):
            @pl.when(i == 0)
            def _(o_ref=o_ref, r=r):
                o_ref[...] = r.astype(o_ref.dtype)

            @pl.when(i > 0)
            def _(o_ref=o_ref, r=r):
                o_ref[...] += r.astype(o_ref.dtype)

    res = pl.pallas_call(
        body, name=name, grid=(n_tok // tile,), in_specs=in_specs, out_specs=out_specs, out_shape=out_shape,
        compiler_params=_params(("arbitrary",)),
    )(*arrays)
    return res


def _sds(shape, dtype):
    return jax.ShapeDtypeStruct(tuple(shape), dtype)


def _sigmoid(x):
    return 1.0 / (1.0 + jnp.exp(-x))


def _rms(x, g):
    return x * lax.rsqrt(jnp.mean(x * x, axis=-1, keepdims=True) + EPS) * g


def _norm_bwd_fn(x, dh, dres, g):
    _, vjp = jax.vjp(_rms, x, g)
    dx, dg = vjp(dh)
    return dx + dres, dg


def _mla_a_fn(cq, ckv, g_qa, g_kva):
    return _rms(cq, g_qa), _rms(ckv, g_kva)


def _mla_a_bwd_fn(cq, ckv, dqn, dkvn, g_qa, g_kva):
    _, vjp = jax.vjp(_mla_a_fn, cq, ckv, g_qa, g_kva)
    return vjp((dqn, dkvn))


def _rope(t, cos, sin):
    t1, t2 = t[:, :QK_ROPE // 2], t[:, QK_ROPE // 2:]
    return jnp.concatenate([t1 * cos - t2 * sin, t1 * sin + t2 * cos], axis=-1)


def _mla_b_fn(q_raw, kv_raw, kr, cos, sin, g_qn, g_kn):
    krope = kr[:, :QK_ROPE]
    qs, ks, vs = [], [], []
    for h in range(MLA_HEADS):
        qh = _rms(q_raw[:, h * QK_HEAD:(h + 1) * QK_HEAD], g_qn)
        kvh = kv_raw[:, h * (QK_NOPE + V_HEAD):(h + 1) * (QK_NOPE + V_HEAD)]
        kh = _rms(jnp.concatenate([kvh[:, :QK_NOPE], krope], axis=-1), g_kn)
        qs.append(jnp.concatenate([qh[:, :QK_NOPE], _rope(qh[:, QK_NOPE:], cos, sin)], axis=-1))
        ks.append(jnp.concatenate([kh[:, :QK_NOPE], _rope(kh[:, QK_NOPE:], cos, sin)], axis=-1))
        vs.append(kvh[:, QK_NOPE:])
    return jnp.stack(qs), jnp.stack(ks), jnp.stack(vs)


def _mla_b_bwd_fn(q_raw, kv_raw, kr, cos, sin, dq, dk, dv, g_qn, g_kn):
    _, vjp = jax.vjp(lambda a, b, c, d, e: _mla_b_fn(a, b, c, cos, sin, d, e), q_raw, kv_raw, kr, g_qn, g_kn)
    return vjp((dq, dk, dv))


def _post_fn(a, o_f, o_b, hg, g_hgo):
    o = o_f + o_b
    parts = [a]
    for h in range(HG_HEADS):
        s = slice(h * HG_DK, (h + 1) * HG_DK)
        gate = hg[:, s]
        parts.append(_rms(o[:, s], g_hgo[:, s]) * (gate * _sigmoid(gate)))
    return jnp.concatenate(parts, axis=-1)


def _post_bwd_fn(o_f, o_b, hg, dr, g_hgo):
    def f(o, hg, g):
        return _post_fn(jnp.zeros_like(o), o, jnp.zeros_like(o), hg, g)[:, o.shape[1]:]
    _, vjp = jax.vjp(f, o_f + o_b, hg, g_hgo)
    return vjp(dr)


def _swiglu_fn(gt, up):
    return gt * _sigmoid(gt) * up


def _swiglu_bwd_fn(gt, up, dact):
    _, vjp = jax.vjp(_swiglu_fn, gt, up)
    return vjp(dact)


def _ple_loss_fn(x2, pg, pp, target):
    gate = _sigmoid(pg)
    err = x2 + gate * pp - target
    dx3 = err * (1.0 / err.shape[-1])
    loss = 0.5 * jnp.sum(jnp.mean(err * err, axis=-1, keepdims=True), axis=0, keepdims=True)
    return dx3, dx3 * pp * gate * (1.0 - gate), dx3 * gate, loss


def _attention_fwd(q, k, v):
    H, T, D = q.shape
    DV = v.shape[-1]
    tq = min(ATT_TQ, T)
    scale = D ** -0.5

    def body(q_ref, k_ref, v_ref, o_ref, lse_ref):
        s = _dot_raw(q_ref[0], k_ref[0], "nt") * scale
        m = jnp.max(s, axis=-1, keepdims=True)
        e = jnp.exp(s - m)
        l = jnp.sum(e, axis=-1, keepdims=True)
        o_ref[...] = _dot_raw(e / l, v_ref[0], "nn")
        lse_ref[0] = m + jnp.log(l)

    return pl.pallas_call(
        body, name="attention_fwd", grid=(H, T // tq),
        in_specs=[pl.BlockSpec((1, tq, D), lambda h, i: (h, i, 0)),
                  pl.BlockSpec((1, T, D), lambda h, i: (h, 0, 0)),
                  pl.BlockSpec((1, T, DV), lambda h, i: (h, 0, 0))],
        out_specs=[pl.BlockSpec((tq, DV), lambda h, i: (i, h)),
                   pl.BlockSpec((1, tq, 1), lambda h, i: (h, i, 0))],
        out_shape=[_sds((T, H * DV), F32), _sds((H, T, 1), F32)],
        compiler_params=_params(("parallel", "parallel")),
    )(q, k, v)


def _attention_bwd(q, k, v, o, lse_row, dmix):
    H, T, D = q.shape
    DV = v.shape[-1]
    tk = min(ATT_TK, T)
    scale = D ** -0.5

    def body(q_ref, k_ref, v_ref, o_ref, lse_ref, do_ref, dq_ref, dk_ref, dv_ref, delta_ref):
        j = pl.program_id(1)
        do = do_ref[...]

        @pl.when(j == 0)
        def _():
            delta_ref[...] = lax.dot_general(jnp.ones((8, DV), F32), do * o_ref[...], (((1,), (1,)), ((), ())),
                                             precision=lax.Precision.HIGHEST, preferred_element_type=F32)

        q_all, k_j, v_j = q_ref[0], k_ref[0], v_ref[0]
        st = _dot_raw(k_j, q_all, "nt") * scale
        pt = jnp.exp(st - lse_ref[0])
        dv_ref[0] = _dot_raw(pt, do, "nn")
        dpt = _dot_raw(v_j, do, "nt")
        dst = pt * (dpt - delta_ref[0:1, :]) * scale
        dk_ref[0] = _dot_raw(dst, q_all, "nn")
        dq_part = _dot_raw(dst, k_j, "tn")

        @pl.when(j == 0)
        def _():
            dq_ref[0] = dq_part

        @pl.when(j > 0)
        def _():
            dq_ref[0] += dq_part

    return pl.pallas_call(
        body, name="attention_bwd", grid=(H, T // tk),
        in_specs=[pl.BlockSpec((1, T, D), lambda h, j: (h, 0, 0)),
                  pl.BlockSpec((1, tk, D), lambda h, j: (h, j, 0)),
                  pl.BlockSpec((1, tk, DV), lambda h, j: (h, j, 0)),
                  pl.BlockSpec((T, DV), lambda h, j: (0, h)),
                  pl.BlockSpec((1, 1, T), lambda h, j: (h, 0, 0)),
                  pl.BlockSpec((T, DV), lambda h, j: (0, h))],
        out_specs=[pl.BlockSpec((1, T, D), lambda h, j: (h, 0, 0)),
                   pl.BlockSpec((1, tk, D), lambda h, j: (h, j, 0)),
                   pl.BlockSpec((1, tk, DV), lambda h, j: (h, j, 0))],
        out_shape=[_sds((H, T, D), F32), _sds((H, T, D), F32), _sds((H, T, DV), F32)],
        scratch_shapes=[pltpu.VMEM((8, T), F32)],
        compiler_params=_params(("parallel", "arbitrary")),
    )(q, k, v, o, lse_row, dmix)


def _gla_block(hq, hf, hi, lower, st_in, *, rev, dot):
    rows, dk = hq.shape
    G, C = rows // CHUNK, CHUNK
    q = hq * _sigmoid(hq)
    f = lower + (1.0 - lower) * _sigmoid(hf)
    k = 1.0 - f
    logf = jnp.log(f)
    q3, k3, v3, lf3 = (t.reshape(G, C, dk) for t in (q, k, hi, logf))
    r = lax.broadcasted_iota(jnp.int32, (C, C), 0)
    c = lax.broadcasted_iota(jnp.int32, (C, C), 1)
    tri = ((r <= c) if rev else (r >= c)).astype(F32)
    b = lax.dot_general(jnp.broadcast_to(tri, (G, C, C)), lf3, (((2,), (1,)), ((0,), (0,))),
                        precision=lax.Precision.HIGHEST, preferred_element_type=F32)
    tpos = lax.broadcasted_iota(jnp.int32, (1, C, 1), 1)
    first_half = (tpos >= C // 2) if rev else (tpos <= C // 2 - 1)
    b_mid = jnp.sum(jnp.where(first_half, lf3, 0.0), axis=1, keepdims=True)
    b_last = jnp.sum(lf3, axis=1, keepdims=True)
    a = dot(q3 * jnp.exp(b - b_mid), k3 * jnp.exp(b_mid - b), "nt") * tri
    o_intra = dot(a, v3, "nn")
    kv_t = dot(v3, k3 * jnp.exp(b_last - b), "tn")
    decay = jnp.exp(b_last)
    qd = q3 * jnp.exp(b)
    st = st_in
    o_inter = [None] * G
    for g in (reversed(range(G)) if rev else range(G)):
        o_inter[g] = dot(qd[g], st, "nt")
        st = st * decay[g] + kv_t[g]
    o = o_intra.reshape(rows, dk) + jnp.concatenate(o_inter, axis=0)
    return o, st


def _gla_fwd(z, lower3, *, rev, col_q, col_f, col_v):
    T = z.shape[0]
    rows = min(GLA_GROUP * CHUNK, T)
    nb = T // rows
    blk = (lambda n: nb - 1 - n) if rev else (lambda n: n)

    def body(hq_ref, hf_ref, hi_ref, low_ref, o_ref, st_out_ref, st_ref):
        @pl.when(pl.program_id(1) == 0)
        def _():
            st_ref[...] = jnp.zeros_like(st_ref)

        st_in = st_ref[...]
        st_out_ref[0, 0] = st_in
        o, st = _gla_block(hq_ref[...], hf_ref[...], hi_ref[...], low_ref[0], st_in, rev=rev, dot=_dot_raw)
        o_ref[...] = o
        st_ref[...] = st

    def zspec(col):
        return pl.BlockSpec((rows, HG_DK), lambda h, n: (blk(n), col + h))

    return pl.pallas_call(
        body, name="gla_fwd_rev" if rev else "gla_fwd", grid=(HG_HEADS, nb),
        in_specs=[zspec(col_q), zspec(col_f), zspec(col_v), pl.BlockSpec((1, 1, HG_DK), lambda h, n: (h, 0, 0))],
        out_specs=[pl.BlockSpec((rows, HG_DK), lambda h, n: (blk(n), h)),
                   pl.BlockSpec((1, 1, HG_DK, HG_DK), lambda h, n: (h, blk(n), 0, 0))],
        out_shape=[_sds((T, HG_HEADS * HG_DK), F32), _sds((HG_HEADS, nb, HG_DK, HG_DK), F32)],
        scratch_shapes=[pltpu.VMEM((HG_DK, HG_DK), F32)],
        compiler_params=_params(("parallel", "arbitrary")),
    )(z, z, z, lower3)


def _gla_bwd(z, lower3, states, do, prev, *, rev, col_q, col_f, col_v):
    T = z.shape[0]
    rows = min(GLA_GROUP * CHUNK, T)
    nb = T // rows
    blk = (lambda n: n) if rev else (lambda n: nb - 1 - n)
    has_prev = prev is not None
    fn = functools.partial(_gla_block, rev=rev, dot=_bdot)

    def body(*refs):
        hq_ref, hf_ref, hi_ref, low_ref, st_ref, do_ref = refs[:6]
        rest = refs[6:]
        if has_prev:
            pq_ref, pi_ref = rest[:2]
            rest = rest[2:]
        dhq_ref, dhi_ref, dhf_ref, dlow_ref, dst_ref = rest
        n = pl.program_id(1)

        @pl.when(n == 0)
        def _():
            dst_ref[...] = jnp.zeros_like(dst_ref)

        _, vjp = jax.vjp(fn, hq_ref[...], hf_ref[...], hi_ref[...], low_ref[0], st_ref[0, 0])
        dhq, dhf, dhi, dlow, dst = vjp((do_ref[...], dst_ref[...]))
        dst_ref[...] = dst
        if has_prev:
            dhq = dhq + pq_ref[...]
            dhi = dhi + pi_ref[...]
        dhq_ref[...] = dhq.astype(dhq_ref.dtype)
        dhi_ref[...] = dhi.astype(dhi_ref.dtype)
        dhf_ref[...] = dhf.astype(dhf_ref.dtype)

        @pl.when(n == 0)
        def _():
            dlow_ref[0] = dlow

        @pl.when(n > 0)
        def _():
            dlow_ref[0] += dlow

    def zspec(col):
        return pl.BlockSpec((rows, HG_DK), lambda h, n: (blk(n), col + h))

    hspec = pl.BlockSpec((rows, HG_DK), lambda h, n: (blk(n), h))
    in_specs = [zspec(col_q), zspec(col_f), zspec(col_v), pl.BlockSpec((1, 1, HG_DK), lambda h, n: (h, 0, 0)),
                pl.BlockSpec((1, 1, HG_DK, HG_DK), lambda h, n: (h, blk(n), 0, 0)), hspec]
    ins = [z, z, z, lower3, states, do]
    if has_prev:
        in_specs += [hspec, hspec]
        ins += list(prev)
    wide = HG_HEADS * HG_DK
    acc_dtype = BF16 if has_prev else F32
    return pl.pallas_call(
        body, name="gla_bwd_rev" if rev else "gla_bwd", grid=(HG_HEADS, nb),
        in_specs=in_specs,
        out_specs=[hspec, hspec, hspec, pl.BlockSpec((1, 1, HG_DK), lambda h, n: (h, 0, 0))],
        out_shape=[_sds((T, wide), acc_dtype), _sds((T, wide), acc_dtype), _sds((T, wide), BF16),
                   _sds((HG_HEADS, 1, HG_DK), F32)],
        scratch_shapes=[pltpu.VMEM((HG_DK, HG_DK), F32)],
        compiler_params=_params(("parallel", "arbitrary")),
    )(*ins)


def _lower_fn(lb):
    e = jnp.exp(lb - jnp.max(lb, axis=0, keepdims=True))
    return (e / jnp.sum(e, axis=0, keepdims=True))[0]


def _lower_bounds(lb):
    def body(lb_ref, o_ref):
        o_ref[...] = _lower_fn(lb_ref[...])
    return pl.pallas_call(body, name="lower_bounds", out_shape=_sds(lb.shape[1:], F32))(lb)


def _lower_bounds_bwd(lb, dlower):
    def body(lb_ref, d_ref, o_ref):
        _, vjp = jax.vjp(_lower_fn, lb_ref[...])
        o_ref[...] = vjp(d_ref[...])[0]
    return pl.pallas_call(body, name="lower_bounds_bwd", out_shape=_sds(lb.shape, F32))(lb, dlower)


def _row_tile(r):
    for t in (512, 256, 128, 64, 32, 16, 8):
        if r % t == 0:
            return t
    return r


def _sum4(shards, recv, chip, *, name):
    _, R, C = shards.shape
    tr = _row_tile(R)

    def body(chip_ref, o_ref, r_ref, out_ref):
        out_ref[...] = ((o_ref[0].astype(F32) + r_ref[0].astype(F32)) + r_ref[1].astype(F32)) + r_ref[2].astype(F32)

    grid_spec = pltpu.PrefetchScalarGridSpec(
        num_scalar_prefetch=1, grid=(R // tr,),
        in_specs=[pl.BlockSpec((1, tr, C), lambda i, chip_ref: (chip_ref[0], i, 0)),
                  pl.BlockSpec((3, tr, C), lambda i, chip_ref: (0, i, 0))],
        out_specs=pl.BlockSpec((tr, C), lambda i, chip_ref: (i, 0)))
    return pl.pallas_call(
        body, name=name, grid_spec=grid_spec, out_shape=_sds((R, C), F32), compiler_params=_params(("parallel",)),
    )(chip, shards, recv)


def _adamw_math(w, g, m, v):
    m = ADAM_B1 * m + (1.0 - ADAM_B1) * g
    v = ADAM_B2 * v + (1.0 - ADAM_B2) * (g * g)
    m_hat = m / (1.0 - ADAM_B1 ** ADAM_STEP)
    v_hat = v / (1.0 - ADAM_B2 ** ADAM_STEP)
    delta = -ADAM_LR * (m_hat / (jnp.sqrt(v_hat) + ADAM_EPS) + ADAM_WD * w)
    return delta, m, v


def _adamw(w, g_a, g_b, m, v, *, name):
    R, C = w.shape
    tr = _row_tile(R)
    two = g_b is not None

    def body(*refs):
        w_ref, ga_ref = refs[0], refs[1]
        rest = refs[2:]
        g = ga_ref[...]
        if two:
            g = g + rest[0][...]
            rest = rest[1:]
        m_ref, v_ref, g_out, d_out, m_out, v_out = rest
        delta, m_new, v_new = _adamw_math(w_ref[...], g, m_ref[...], v_ref[...])
        g_out[...] = g
        d_out[...] = delta
        m_out[...] = m_new
        v_out[...] = v_new

    spec = pl.BlockSpec((tr, C), lambda i: (i, 0))
    ins = [w, g_a] + ([g_b] if two else []) + [m, v]
    return pl.pallas_call(
        body, name=name, grid=(R // tr,), in_specs=[spec] * len(ins), out_specs=[spec] * 4,
        out_shape=[_sds((R, C), F32)] * 4, compiler_params=_params(("parallel",)),
    )(*ins)


def _chip_peers():
    x, y, c = lax.axis_index("x"), lax.axis_index("y"), lax.axis_index("c")
    return (x, y, c), 2 * x + y, [(1 - x, y), (x, 1 - y), (1 - x, 1 - y)]


_ANY = pl.BlockSpec(memory_space=pl.ANY)


_HBM = pl.BlockSpec(memory_space=pltpu.HBM)
_SEM = pl.BlockSpec(memory_space=pltpu.SEMAPHORE)
_EFFECT = pltpu.SideEffectType.DATAFLOW_SIDE_EFFECTING


def _chip_copies(srcs, lands, sems, gather):
    (x, y, c), me, chips = _chip_peers()
    copies = []
    for t, (src, land) in enumerate(zip(srcs, lands)):
        for k, (px, py) in enumerate(chips):
            copies.append(pltpu.make_async_remote_copy(
                src if gather else src.at[2 * px + py], land.at[me] if gather else land.at[k],
                sems[0].at[3 * t + k], sems[1].at[3 * t + k], device_id=(px, py, c), device_id_type=MESH))
        if gather:
            copies.append(pltpu.make_async_copy(src, land.at[me], sems[2].at[t]))
    return copies


def _exchange_start(srcs, *, gather, name):
    n = len(srcs)
    n_sem = 3 if gather else 2
    lands = [_sds(((N_CHIPS,) + s.shape) if gather else ((3,) + s.shape[1:]), s.dtype) for s in srcs]

    def body(*refs):
        for cp in _chip_copies(refs[:n], refs[n:2 * n], refs[2 * n:2 * n + n_sem], gather):
            cp.start()

    sem_shapes = [pltpu.SemaphoreType.DMA((3 * n,)), pltpu.SemaphoreType.DMA((3 * n,))]
    sem_shapes += [pltpu.SemaphoreType.DMA((n,))] if gather else []
    thru = [pltpu.HBM(s.shape, s.dtype) for s in srcs] + [pltpu.HBM(l.shape, l.dtype) for l in lands]
    res = pl.pallas_call(
        body, name=name, in_specs=[_HBM] * (2 * n), out_specs=[_SEM] * n_sem + [_HBM] * (2 * n),
        out_shape=sem_shapes + thru, input_output_aliases={i: n_sem + i for i in range(2 * n)},
        compiler_params=pltpu.CompilerParams(has_side_effects=_EFFECT),
    )(*[pltpu.with_memory_space_constraint(s, pltpu.HBM) for s in srcs],
      *[pltpu.with_memory_space_constraint(lax.empty(l.shape, l.dtype), pltpu.HBM) for l in lands])
    return res[:n_sem], res[n_sem:n_sem + n], res[n_sem + n:]


def _exchange_wait(started, after, *, name):
    sems, srcs, lands = started
    n, n_sem = len(srcs), len(sems)

    def body(*refs):
        for cp in _chip_copies(refs[:n], refs[n:2 * n], refs[2 * n:2 * n + n_sem], n_sem == 3):
            cp.wait()

    res = pl.pallas_call(
        body, name=name, in_specs=[_HBM] * (2 * n) + [_SEM] * n_sem + [_ANY], out_specs=[_HBM] * (2 * n),
        out_shape=[pltpu.HBM(a.shape, a.dtype) for a in list(srcs) + list(lands)],
        input_output_aliases={i: i for i in range(2 * n)},
        compiler_params=pltpu.CompilerParams(has_side_effects=_EFFECT),
    )(*srcs, *lands, *sems, after)
    return res[:n], res[n:]


def _swap_sibling(parts):
    n = len(parts)

    def body(*refs):
        ins, outs = refs[:n], refs[n:2 * n]
        send_sems, recv_sems = refs[2 * n:]
        x, y, c = lax.axis_index("x"), lax.axis_index("y"), lax.axis_index("c")
        copies = []
        for t in range(n):
            cp = pltpu.make_async_remote_copy(ins[t], outs[t], send_sems.at[t], recv_sems.at[t],
                                              device_id=(x, y, 1 - c), device_id_type=MESH)
            cp.start()
            copies.append(cp)
        for cp in copies:
            cp.wait()

    return pl.pallas_call(
        body, name="swap_sibling", in_specs=[_ANY] * n, out_specs=[_ANY] * n,
        out_shape=[_sds(p.shape, p.dtype) for p in parts],
        scratch_shapes=[pltpu.SemaphoreType.DMA((n,)), pltpu.SemaphoreType.DMA((n,))],
        compiler_params=_params(),
    )(*parts)


def _allreduce_small(pack):
    R, C = pack.shape

    def body(in_ref, out_ref, slots, send_sems, recv_sems):
        x, y, c = lax.axis_index("x"), lax.axis_index("y"), lax.axis_index("c")
        me = 4 * x + 2 * y + c
        slots[me] = in_ref[...]
        copies = []
        for k in range(1, N_DEV):
            peer = (x ^ ((k >> 2) & 1), y ^ ((k >> 1) & 1), c ^ (k & 1))
            cp = pltpu.make_async_remote_copy(in_ref, slots.at[me], send_sems.at[k - 1], recv_sems.at[k - 1],
                                              device_id=peer, device_id_type=MESH)
            cp.start()
            copies.append(cp)
        for cp in copies:
            cp.wait()
        acc = slots[0]
        for d in range(1, N_DEV):
            acc = acc + slots[d]
        out_ref[...] = acc

    return pl.pallas_call(
        body, name="allreduce_small", out_shape=_sds((R, C), F32),
        in_specs=[pl.BlockSpec(memory_space=pltpu.VMEM)], out_specs=pl.BlockSpec(memory_space=pltpu.VMEM),
        scratch_shapes=[pltpu.VMEM((N_DEV, R, C), F32), pltpu.SemaphoreType.DMA((N_DEV - 1,)),
                        pltpu.SemaphoreType.DMA((N_DEV - 1,))],
        compiler_params=_params(),
    )(pack)


_Z_CQ, _Z_CKV, _Z_HQ, _Z_HFF, _Z_HFB, _Z_HI, _Z_HG, _Z_KR, _Z_END = 0, 256, 512, 1024, 1536, 2048, 2560, 3072, 3200


def _to_z_layout(w):
    pad = jnp.zeros((w.shape[0], _Z_END - _Z_KR - QK_ROPE), w.dtype)
    return jnp.concatenate([w[:, :512], w[:, 512 + QK_ROPE:], w[:, 512:512 + QK_ROPE], pad], axis=1)


def _from_z_layout(w):
    return jnp.concatenate([w[:, :512], w[:, _Z_KR:_Z_KR + QK_ROPE], w[:, 512:_Z_KR]], axis=1)


def _col_shards_to_full(g):
    return jnp.transpose(g, (1, 0, 2)).reshape(g.shape[1], -1)


def _full_to_col_shards(w):
    r, c = w.shape
    return jnp.transpose(w.reshape(r, N_CHIPS, c // N_CHIPS), (1, 0, 2))


def _full_to_row_shards(w):
    r, c = w.shape
    return w.reshape(N_CHIPS, r // N_CHIPS, c)


def kernel(x, p, positions, g_mix, w_in, g_qa, g_kva, w_qb, w_kvb, g_qn, g_kn, lb_param, g_hgo, w_o, g_ffn, w_gate, w_up, w_down, g_ple, w_ple_gate, w_ple_proj, loss_target, m_g_mix, m_w_in, m_g_qa, m_g_kva, m_w_qb, m_w_kvb, m_g_qn, m_g_kn, m_lb_param, m_g_hgo, m_w_o, m_g_ffn, m_w_gate, m_w_up, m_w_down, m_g_ple, m_w_ple_gate, m_w_ple_proj, v_g_mix, v_w_in, v_g_qa, v_g_kva, v_w_qb, v_w_kvb, v_g_qn, v_g_kn, v_lb_param, v_g_hgo, v_w_o, v_g_ffn, v_w_gate, v_w_up, v_w_down, v_g_ple, v_w_ple_gate, v_w_ple_proj):
    w_named = dict(g_mix=g_mix, w_in=w_in, g_qa=g_qa, g_kva=g_kva, w_qb=w_qb, w_kvb=w_kvb, g_qn=g_qn, g_kn=g_kn,
                   lb_param=lb_param, g_hgo=g_hgo, w_o=w_o, g_ffn=g_ffn, w_gate=w_gate, w_up=w_up, w_down=w_down,
                   g_ple=g_ple, w_ple_gate=w_ple_gate, w_ple_proj=w_ple_proj)
    m_named = dict(g_mix=m_g_mix, w_in=m_w_in, g_qa=m_g_qa, g_kva=m_g_kva, w_qb=m_w_qb, w_kvb=m_w_kvb, g_qn=m_g_qn,
                   g_kn=m_g_kn, lb_param=m_lb_param, g_hgo=m_g_hgo, w_o=m_w_o, g_ffn=m_g_ffn, w_gate=m_w_gate,
                   w_up=m_w_up, w_down=m_w_down, g_ple=m_g_ple, w_ple_gate=m_w_ple_gate, w_ple_proj=m_w_ple_proj)
    v_named = dict(g_mix=v_g_mix, w_in=v_w_in, g_qa=v_g_qa, g_kva=v_g_kva, w_qb=v_w_qb, w_kvb=v_w_kvb, g_qn=v_g_qn,
                   g_kn=v_g_kn, lb_param=v_lb_param, g_hgo=v_g_hgo, w_o=v_w_o, g_ffn=v_g_ffn, w_gate=v_w_gate,
                   w_up=v_w_up, w_down=v_w_down, g_ple=v_g_ple, w_ple_gate=v_w_ple_gate, w_ple_proj=v_w_ple_proj)
    order = list(w_named)
    col_sharded = ("w_in", "w_qb", "w_kvb", "w_gate", "w_up", "w_ple_proj")
    row_sharded = ("w_o", "w_down", "w_ple_gate")
    big = col_sharded + row_sharded

    x2d, p2d, tgt = x[0], p[0, 0], loss_target[0]
    T, D = x2d.shape

    lb_flat = lb_param.reshape(-1, lb_param.shape[-1])
    gather_groups = (("w_in", "w_qb", "w_kvb"), ("w_o", "w_gate", "w_up"), ("w_down", "w_ple_gate", "w_ple_proj"))
    gather_started = []
    for gi, names in enumerate(gather_groups):
        srcs = [w_named[n][0].astype(BF16) for n in names] + ([lb_flat] if gi == 0 else [])
        gather_started.append(_exchange_start(srcs, gather=True, name=f"gather_start_{gi}"))
    full = {}

    def gather_wait(gi, after):
        _, got = _exchange_wait(gather_started[gi], after, name=f"gather_wait_{gi}")
        for n, g in zip(gather_groups[gi], got):
            full[n] = _col_shards_to_full(g) if n in col_sharded else g.reshape(-1, g.shape[-1])
        return got

    inv_freq = ROPE_THETA ** (-jnp.arange(0, QK_ROPE, 2, dtype=F32) / QK_ROPE)
    ang = positions[0].astype(F32)[:, None] * inv_freq
    cos, sin = jnp.cos(ang), jnp.sin(ang)
    g_hgo_row = g_hgo.reshape(1, -1)

    h1 = _stage(_rms, [x2d], [g_mix], [_sds((T, D), BF16)], [], name="norm_mix")[0]
    got = gather_wait(0, h1)
    lb_full = _col_shards_to_full(got[-1]).reshape(lb_param.shape[0], lb_param.shape[1], -1)
    w_in_z = _to_z_layout(full["w_in"])
    z = _mm(h1, w_in_z, name="in_proj")
    qn, kvn = _stage(_mla_a_fn, [_cols(z, 256, 0), _cols(z, 256, 1)], [g_qa, g_kva],
                     [_sds((T, 256), BF16), _sds((T, 256), BF16)], [], name="mla_latent_norm")
    q_raw = _mm(qn, full["w_qb"], name="q_up")
    kv_raw = _mm(kvn, full["w_kvb"], name="kv_up")
    kr = _cols(z, LANE, _Z_KR // LANE)
    q, k, v = _stage(_mla_b_fn, [q_raw, kv_raw, kr, cos, sin], [g_qn, g_kn],
                     [_sds((MLA_HEADS, T, QK_HEAD), BF16), _sds((MLA_HEADS, T, QK_HEAD), BF16),
                      _sds((MLA_HEADS, T, V_HEAD), BF16)], [], name="mla_qk_norm_rope")
    att, lse = _attention_fwd(q, k, v)

    lower = _lower_bounds(lb_full)
    lower3 = lower.reshape(2, HG_HEADS, 1, HG_DK)
    cq_blk, cf_blk, cb_blk, cv_blk = _Z_HQ // LANE, _Z_HFF // LANE, _Z_HFB // LANE, _Z_HI // LANE
    o_f, st_f = _gla_fwd(z, lower3[0], rev=False, col_q=cq_blk, col_f=cf_blk, col_v=cv_blk)
    o_b, st_b = _gla_fwd(z, lower3[1], rev=True, col_q=cq_blk, col_f=cb_blk, col_v=cv_blk)
    hg = _cols(z, 512, _Z_HG // 512)
    mix = _stage(_post_fn, [att, o_f, o_b, hg], [g_hgo_row], [_sds((T, att.shape[1] + o_f.shape[1]), BF16)], [],
                 name="mix_out")[0]
    gather_wait(1, mix)
    x1 = _mm(mix, full["w_o"], add=x2d, name="out_proj")
    h2 = _stage(_rms, [x1], [g_ffn], [_sds((T, D), BF16)], [], name="norm_ffn")[0]
    gt = _mm(h2, full["w_gate"], name="ffn_gate")
    up = _mm(h2, full["w_up"], name="ffn_up")
    act = _stage(_swiglu_fn, [gt, up], [], [_sds(gt.shape, BF16)], [], name="swiglu")[0]
    gather_wait(2, act)
    x2 = _mm(act, full["w_down"], add=x1, name="ffn_down")
    h3 = _stage(_rms, [x2], [g_ple], [_sds((T, D), BF16)], [], name="norm_ple")[0]
    pg = _mm(h3, full["w_ple_gate"], name="ple_gate")
    pp = _mm(p2d, full["w_ple_proj"], name="ple_proj")
    dx3, dpg, dpp, loss_part = _stage(_ple_loss_fn, [x2, pg, pp, tgt], [],
                                      [_sds((T, D), F32), _sds((T, D), BF16), _sds((T, D), BF16)],
                                      [_sds((1, 1), F32)], name="ple_loss")

    grads = {}
    scatter_groups = (("w_ple_proj", "w_ple_gate", "w_down"), ("w_gate", "w_up", "w_o"), ("w_qb", "w_kvb", "w_in"))
    scatter_started = []

    def scatter_start(gi):
        srcs = [_full_to_col_shards(grads[n]) if n in col_sharded else _full_to_row_shards(grads[n])
                for n in scatter_groups[gi]]
        scatter_started.append(_exchange_start(srcs, gather=False, name=f"scatter_start_{gi}"))

    grads["w_ple_proj"] = _mm(p2d, dpp, ta=True, out_dtype=BF16, name="d_w_ple_proj")
    grads["w_ple_gate"] = _mm(h3, dpg, ta=True, out_dtype=BF16, name="d_w_ple_gate")
    dh3 = _mm(dpg, full["w_ple_gate"], tb=True, name="d_h3")
    dx2, grads["g_ple"] = _stage(_norm_bwd_fn, [x2, dh3, dx3], [g_ple], [_sds((T, D), F32)], [_sds((1, D), F32)],
                                 name="norm_ple_bwd")
    dact = _mm(dx2, full["w_down"], tb=True, name="d_act")
    grads["w_down"] = _mm(act, dx2, ta=True, out_dtype=BF16, name="d_w_down")
    scatter_start(0)
    dgt, dup = _stage(_swiglu_bwd_fn, [gt, up, dact], [], [_sds(gt.shape, BF16), _sds(gt.shape, BF16)], [],
                      name="swiglu_bwd")
    grads["w_gate"] = _mm(h2, dgt, ta=True, out_dtype=BF16, name="d_w_gate")
    grads["w_up"] = _mm(h2, dup, ta=True, out_dtype=BF16, name="d_w_up")
    dh2 = _mm(dgt, full["w_gate"], tb=True, name="d_h2_gate")
    dh2 = _mm(dup, full["w_up"], tb=True, add=dh2, name="d_h2_up")
    dx1, grads["g_ffn"] = _stage(_norm_bwd_fn, [x1, dh2, dx2], [g_ffn], [_sds((T, D), F32)], [_sds((1, D), F32)],
                                 name="norm_ffn_bwd")
    dmix = _mm(dx1, full["w_o"], tb=True, name="d_mix")
    grads["w_o"] = _mm(mix, dx1, ta=True, out_dtype=BF16, name="d_w_o")

    scatter_start(1)
    half = MLA_HEADS * V_HEAD
    do, dhg, dg_hgo = _stage(_post_bwd_fn, [o_f, o_b, hg, _cols(dmix, half, 1)], [g_hgo_row],
                             [_sds((T, half), F32), _sds((T, half), BF16)], [_sds((1, half), F32)], name="mix_out_bwd")
    grads["g_hgo"] = dg_hgo
    dhq_f, dhi_f, dhf_f, dlow_f = _gla_bwd(z, lower3[0], st_f, do, None, rev=False,
                                           col_q=cq_blk, col_f=cf_blk, col_v=cv_blk)
    dhq, dhi, dhf_b, dlow_b = _gla_bwd(z, lower3[1], st_b, do, (dhq_f, dhi_f), rev=True,
                                       col_q=cq_blk, col_f=cb_blk, col_v=cv_blk)

    lse_row = lse.reshape(MLA_HEADS, 1, T)
    dq, dk, dv = _attention_bwd(q, k, v, att, lse_row, dmix)
    dq_raw, dkv_raw, dkr, grads["g_qn"], grads["g_kn"] = _stage(
        _mla_b_bwd_fn, [q_raw, kv_raw, kr, cos, sin, dq, dk, dv], [g_qn, g_kn],
        [_sds(q_raw.shape, BF16), _sds(kv_raw.shape, BF16), _sds((T, LANE), BF16)],
        [_sds(g_qn.shape, F32), _sds(g_kn.shape, F32)], name="mla_qk_norm_rope_bwd")
    grads["w_qb"] = _mm(qn, dq_raw, ta=True, out_dtype=BF16, name="d_w_qb")
    grads["w_kvb"] = _mm(kvn, dkv_raw, ta=True, out_dtype=BF16, name="d_w_kvb")
    dqn = _mm(dq_raw, full["w_qb"], tb=True, name="d_qn")
    dkvn = _mm(dkv_raw, full["w_kvb"], tb=True, name="d_kvn")
    dcq, dckv, grads["g_qa"], grads["g_kva"] = _stage(
        _mla_a_bwd_fn, [_cols(z, 256, 0), _cols(z, 256, 1), dqn, dkvn], [g_qa, g_kva],
        [_sds((T, 256), BF16), _sds((T, 256), BF16)], [_sds(g_qa.shape, F32), _sds(g_kva.shape, F32)],
        name="mla_latent_norm_bwd")
    dz = jnp.concatenate([dcq, dckv, dhq, dhf_f, dhf_b, dhi, dhg, dkr], axis=1)
    grads["w_in"] = _from_z_layout(_mm(h1, dz, ta=True, out_dtype=BF16, name="d_w_in"))
    scatter_start(2)
    dh1 = _mm(dz, w_in_z, tb=True, name="d_h1")
    grad_x, grads["g_mix"] = _stage(_norm_bwd_fn, [x2d, dh1, dx1], [g_mix], [_sds((T, D), F32)], [_sds((1, D), F32)],
                                    name="norm_mix_bwd")

    chip = 2 * lax.axis_index("x") + lax.axis_index("y")
    partial_sum = {}
    after = grad_x
    for gi, names in enumerate(scatter_groups):
        shards, recvs = _exchange_wait(scatter_started[gi], after, name=f"scatter_wait_{gi}")
        for n, s, r in zip(names, shards, recvs):
            partial_sum[n] = _sum4(s, r, chip.reshape(1), name="sum_" + n)
        after = partial_sum[names[-1]]
    partial_sums = [partial_sum[n] for n in big]
    sibling_sums = _swap_sibling(partial_sums)

    small = ("g_mix", "g_qa", "g_kva", "g_qn", "g_kn", "g_hgo", "g_ffn", "g_ple")
    small_all = small + ("lb_param",)
    width = -(-max(w_named[n].size for n in small_all) // LANE) * LANE

    def row(a):
        a = a.reshape(1, -1)
        return jnp.pad(a, ((0, 0), (0, width - a.shape[1])))

    dlower = jnp.concatenate([dlow_f.reshape(1, -1), dlow_b.reshape(1, -1)], axis=0)
    pack = jnp.concatenate([row(grads[n]) for n in small] + [row(dlower[0]), row(dlower[1]), row(loss_part)]
                           + [jnp.zeros((5, width), F32)], axis=0)
    red = _allreduce_small(pack)
    loss = red[10, 0]
    dlower_sum = red[8:10, :lb_full.shape[-1]]
    dlb_full = _lower_bounds_bwd(lb_full, dlower_sum)
    fshard = lb_param.shape[-1]
    dlb = lax.dynamic_slice_in_dim(dlb_full, chip * fshard, fshard, axis=2)

    out_g, out_d, out_m, out_v = {}, {}, {}, {}
    for n, mine, theirs in zip(big, partial_sums, sibling_sums):
        out_g[n], out_d[n], out_m[n], out_v[n] = (
            t[None] for t in _adamw(w_named[n][0], mine, theirs, m_named[n][0], v_named[n][0], name="adamw_" + n))
    g_rows = [red[i:i + 1] for i in range(len(small))] + [row(dlb)]
    g_pack = jnp.concatenate(g_rows + [jnp.zeros((16 - len(g_rows), width), F32)], axis=0)

    def packed(named):
        rows = [row(named[n]) for n in small_all]
        return jnp.concatenate(rows + [jnp.ones((16 - len(rows), width), F32)], axis=0)

    s_g, s_d, s_m, s_v = _adamw(packed(w_named), g_pack, None, packed(m_named), packed(v_named), name="adamw_small")
    for i, n in enumerate(small_all):
        size = w_named[n].size
        for src, dst in ((s_g, out_g), (s_d, out_d), (s_m, out_m), (s_v, out_v)):
            dst[n] = src[i, :size].reshape(w_named[n].shape)

    return (loss, grad_x[None], *[out_g[n] for n in order], *[out_d[n] for n in order],
            *[out_m[n] for n in order], *[out_v[n] for n in order])
```

```python
import functools

import jax
import jax.numpy as jnp
from jax import lax
from jax.experimental import pallas as pl
from jax.experimental.pallas import tpu as pltpu

F32 = jnp.float32
BF16 = jnp.bfloat16
MESH = pl.DeviceIdType.MESH

EPS = 1e-6
ROPE_THETA = 10000.0
MLA_HEADS = 4
QK_NOPE = 128
QK_ROPE = 64
QK_HEAD = QK_NOPE + QK_ROPE
V_HEAD = 128
HG_HEADS = 4
HG_DK = 128
CHUNK = 64
ADAM_LR = 0.001
ADAM_B1 = 0.9
ADAM_B2 = 0.999
ADAM_EPS = 1e-08
ADAM_WD = 0.01
ADAM_STEP = 10

LANE = 128
VMEM_LIMIT = 56 * 1024 * 1024
TOK_TILE = 256
GLA_GROUP = 8
ATT_TQ = 256
ATT_TK = 256
N_CHIPS = 4
N_DEV = 8


_ANY = pl.BlockSpec(memory_space=pl.ANY)


def _params(dims=None, **kw):
    return pltpu.CompilerParams(dimension_semantics=dims, vmem_limit_bytes=VMEM_LIMIT, **kw)


def _tile(n, target):
    if n <= target:
        return n
    best = None
    for d in range(LANE, target + 1, LANE):
        if n % d == 0:
            best = d
    return best if best is not None else n


def _dot_raw(a, b, kind):
    nb = a.ndim - 2
    batch = ((0,), (0,)) if nb else ((), ())
    ca = nb if kind == "tn" else nb + 1
    cb = nb + 1 if kind == "nt" else nb
    return lax.dot_general(a.astype(BF16), b.astype(BF16), (((ca,), (cb,)), batch), preferred_element_type=F32)


@functools.partial(jax.custom_vjp, nondiff_argnums=(2,))
def _bdot(a, b, kind):
    return _dot_raw(a, b, kind)


def _bdot_fwd(a, b, kind):
    return _dot_raw(a, b, kind), (a, b)


def _bdot_bwd(kind, res, g):
    a, b = res
    if kind == "nn":
        da, db = _bdot(g, b, "nt"), _bdot(a, g, "tn")
    elif kind == "nt":
        da, db = _bdot(g, b, "nn"), _bdot(g, a, "tn")
    else:
        da, db = _bdot(b, g, "nt"), _bdot(a, g, "nn")
    return da.astype(a.dtype), db.astype(b.dtype)


_bdot.defvjp(_bdot_fwd, _bdot_bwd)


def _mm(a, b, *, name, ta=False, tb=False, add=None, out_dtype=F32, after=None):
    K, M = a.shape if ta else a.shape[::-1]
    N, Kb = b.shape if tb else b.shape[::-1]
    assert K == Kb, (a.shape, b.shape, ta, tb)
    tm, tn, tk = _tile(M, 512), _tile(N, 1536), _tile(K, 1536)
    nk = K // tk
    kind = "tn" if ta else ("nt" if tb else "nn")
    assert not (ta and tb)
    a_spec = pl.BlockSpec((tk, tm), lambda i, j, k: (k, i)) if ta else pl.BlockSpec((tm, tk), lambda i, j, k: (i, k))
    b_spec = pl.BlockSpec((tn, tk), lambda i, j, k: (j, k)) if tb else pl.BlockSpec((tk, tn), lambda i, j, k: (k, j))
    o_spec = pl.BlockSpec((tm, tn), lambda i, j, k: (i, j))
    has_add = add is not None

    def body(*refs):
        a_ref, b_ref = refs[0], refs[1]
        add_ref = refs[2] if has_add else None
        o_ref = refs[n_in]
        part = _dot_raw(a_ref[...], b_ref[...], kind)
        if nk == 1:
            if has_add:
                part = part + add_ref[...].astype(F32)
            o_ref[...] = part.astype(o_ref.dtype)
            return
        acc_ref = refs[-1]
        k = pl.program_id(2)

        @pl.when(k == 0)
        def _():
            acc_ref[...] = part

        @pl.when(k > 0)
        def _():
            acc_ref[...] += part

        @pl.when(k == nk - 1)
        def _():
            r = acc_ref[...]
            if has_add:
                r = r + add_ref[...].astype(F32)
            o_ref[...] = r.astype(o_ref.dtype)

    ins = [a, b] + ([add] if has_add else []) + ([after] if after is not None else [])
    in_specs = [a_spec, b_spec] + ([o_spec] if has_add else []) + ([_ANY] if after is not None else [])
    n_in = len(ins)
    return pl.pallas_call(
        body, name=name, grid=(M // tm, N // tn, nk), in_specs=in_specs, out_specs=o_spec,
        out_shape=jax.ShapeDtypeStruct((M, N), out_dtype),
        scratch_shapes=[pltpu.VMEM((tm, tn), F32)] if nk > 1 else [],
        compiler_params=_params(("parallel", "parallel", "arbitrary")),
    )(*ins)


def _cols(arr, width, block):
    return (arr, width, block)


def _stage(fn, tiles, params, out_tiles, out_sums, *, name, tile=TOK_TILE, after=None):
    def tok_spec(shape, width=None, block=0):
        if len(shape) == 2:
            w = shape[1] if width is None else width
            return pl.BlockSpec((tile, w), lambda i: (i, block))
        return pl.BlockSpec((shape[0], tile, shape[2]), lambda i: (0, i, 0))

    arrays, in_specs = [], []
    for t in tiles:
        if isinstance(t, tuple):
            arr, width, block = t
            arrays.append(arr)
            in_specs.append(tok_spec(arr.shape, width, block))
        else:
            arrays.append(t)
            in_specs.append(tok_spec(t.shape))
    n_tok = arrays[0].shape[0] if arrays[0].ndim == 2 else arrays[0].shape[1]
    for p in params:
        arrays.append(p)
        in_specs.append(pl.BlockSpec(p.shape, lambda i, nd=p.ndim: (0,) * nd))
    out_shape = list(out_tiles) + list(out_sums)
    out_specs = [tok_spec(o.shape) for o in out_tiles]
    out_specs += [pl.BlockSpec(o.shape, lambda i, nd=len(o.shape): (0,) * nd) for o in out_sums]
    n_fn, n_ot = len(arrays), len(out_tiles)
    if after is not None:
        arrays.append(after)
        in_specs.append(_ANY)
    n_in = len(arrays)

    def body(*refs):
        res = fn(*[r[...] for r in refs[:n_fn]])
        if not isinstance(res, (tuple, list)):
            res = (res,)
        outs = refs[n_in:]
        for o_ref, r in zip(outs[:n_ot], res[:n_ot]):
            o_ref[...] = r.astype(o_ref.dtype)
        i = pl.program_id(0)
        for o_ref, r in zip(outs[n_ot:], res[n_ot:]):
            @pl.when(i == 0)
            def _(o_ref=o_ref, r=r):
                o_ref[...] = r.astype(o_ref.dtype)

            @pl.when(i > 0)
            def _(o_ref=o_ref, r=r):
                o_ref[...] += r.astype(o_ref.dtype)

    res = pl.pallas_call(
        body, name=name, grid=(n_tok // tile,), in_specs=in_specs, out_specs=out_specs, out_shape=out_shape,
        compiler_params=_params(("arbitrary",)),
    )(*arrays)
    return res


def _sds(shape, dtype):
    return jax.ShapeDtypeStruct(tuple(shape), dtype)


def _sigmoid(x):
    return 1.0 / (1.0 + jnp.exp(-x))


def _rms(x, g):
    return x * lax.rsqrt(jnp.mean(x * x, axis=-1, keepdims=True) + EPS) * g


def _norm_bwd_fn(x, dh, dres, g):
    _, vjp = jax.vjp(_rms, x, g)
    dx, dg = vjp(dh)
    return dx + dres, dg


def _mla_a_fn(cq, ckv, g_qa, g_kva):
    return _rms(cq, g_qa), _rms(ckv, g_kva)


def _mla_a_bwd_fn(cq, ckv, dqn, dkvn, g_qa, g_kva):
    _, vjp = jax.vjp(_mla_a_fn, cq, ckv, g_qa, g_kva)
    return vjp((dqn, dkvn))


def _rope(t, cos, sin):
    t1, t2 = t[:, :QK_ROPE // 2], t[:, QK_ROPE // 2:]
    return jnp.concatenate([t1 * cos - t2 * sin, t1 * sin + t2 * cos], axis=-1)


def _mla_b_fn(q_raw, kv_raw, kr, cos, sin, g_qn, g_kn):
    krope = kr[:, :QK_ROPE]
    qs, ks, vs = [], [], []
    for h in range(MLA_HEADS):
        qh = _rms(q_raw[:, h * QK_HEAD:(h + 1) * QK_HEAD], g_qn)
        kvh = kv_raw[:, h * (QK_NOPE + V_HEAD):(h + 1) * (QK_NOPE + V_HEAD)]
        kh = _rms(jnp.concatenate([kvh[:, :QK_NOPE], krope], axis=-1), g_kn)
        qs.append(jnp.concatenate([qh[:, :QK_NOPE], _rope(qh[:, QK_NOPE:], cos, sin)], axis=-1))
        ks.append(jnp.concatenate([kh[:, :QK_NOPE], _rope(kh[:, QK_NOPE:], cos, sin)], axis=-1))
        vs.append(kvh[:, QK_NOPE:])
    return jnp.stack(qs), jnp.stack(ks), jnp.stack(vs)


def _mla_b_bwd_fn(q_raw, kv_raw, kr, cos, sin, dq, dk, dv, g_qn, g_kn):
    _, vjp = jax.vjp(lambda a, b, c, d, e: _mla_b_fn(a, b, c, cos, sin, d, e), q_raw, kv_raw, kr, g_qn, g_kn)
    return vjp((dq, dk, dv))


def _post_fn(a, o_f, o_b, hg, g_hgo):
    o = o_f + o_b
    parts = [a]
    for h in range(HG_HEADS):
        s = slice(h * HG_DK, (h + 1) * HG_DK)
        gate = hg[:, s]
        parts.append(_rms(o[:, s], g_hgo[:, s]) * (gate * _sigmoid(gate)))
    return jnp.concatenate(parts, axis=-1)


def _post_bwd_fn(o_f, o_b, hg, dr, g_hgo):
    def f(o, hg, g):
        return _post_fn(jnp.zeros_like(o), o, jnp.zeros_like(o), hg, g)[:, o.shape[1]:]
    _, vjp = jax.vjp(f, o_f + o_b, hg, g_hgo)
    return vjp(dr)


def _swiglu_fn(gt, up):
    return gt * _sigmoid(gt) * up


def _swiglu_bwd_fn(gt, up, dact):
    _, vjp = jax.vjp(_swiglu_fn, gt, up)
    return vjp(dact)


def _ple_loss_fn(x2, pg, pp, target):
    gate = _sigmoid(pg)
    err = x2 + gate * pp - target
    dx3 = err * (1.0 / err.shape[-1])
    loss = 0.5 * jnp.sum(jnp.mean(err * err, axis=-1, keepdims=True), axis=0, keepdims=True)
    return dx3, dx3 * pp * gate * (1.0 - gate), dx3 * gate, loss


def _attention_fwd(q, k, v):
    H, T, D = q.shape
    DV = v.shape[-1]
    tq = min(ATT_TQ, T)
    scale = D ** -0.5

    def body(q_ref, k_ref, v_ref, o_ref, lse_ref):
        s = _dot_raw(q_ref[0], k_ref[0], "nt") * scale
        m = jnp.max(s, axis=-1, keepdims=True)
        e = jnp.exp(s - m)
        l = jnp.sum(e, axis=-1, keepdims=True)
        o_ref[...] = _dot_raw(e / l, v_ref[0], "nn")
        lse_ref[0] = m + jnp.log(l)

    return pl.pallas_call(
        body, name="attention_fwd", grid=(H, T // tq),
        in_specs=[pl.BlockSpec((1, tq, D), lambda h, i: (h, i, 0)),
                  pl.BlockSpec((1, T, D), lambda h, i: (h, 0, 0)),
                  pl.BlockSpec((1, T, DV), lambda h, i: (h, 0, 0))],
        out_specs=[pl.BlockSpec((tq, DV), lambda h, i: (i, h)),
                   pl.BlockSpec((1, tq, 1), lambda h, i: (h, i, 0))],
        out_shape=[_sds((T, H * DV), F32), _sds((H, T, 1), F32)],
        compiler_params=_params(("parallel", "parallel")),
    )(q, k, v)


def _attention_bwd(q, k, v, o, lse_row, dmix):
    H, T, D = q.shape
    DV = v.shape[-1]
    tk = min(ATT_TK, T)
    scale = D ** -0.5

    def body(q_ref, k_ref, v_ref, o_ref, lse_ref, do_ref, dq_ref, dk_ref, dv_ref, delta_ref):
        j = pl.program_id(1)
        do = do_ref[...]

        @pl.when(j == 0)
        def _():
            delta_ref[...] = lax.dot_general(jnp.ones((8, DV), F32), do * o_ref[...], (((1,), (1,)), ((), ())),
                                             precision=lax.Precision.HIGHEST, preferred_element_type=F32)

        q_all, k_j, v_j = q_ref[0], k_ref[0], v_ref[0]
        st = _dot_raw(k_j, q_all, "nt") * scale
        pt = jnp.exp(st - lse_ref[0])
        dv_ref[0] = _dot_raw(pt, do, "nn")
        dpt = _dot_raw(v_j, do, "nt")
        dst = pt * (dpt - delta_ref[0:1, :]) * scale
        dk_ref[0] = _dot_raw(dst, q_all, "nn")
        dq_part = _dot_raw(dst, k_j, "tn")

        @pl.when(j == 0)
        def _():
            dq_ref[0] = dq_part

        @pl.when(j > 0)
        def _():
            dq_ref[0] += dq_part

    return pl.pallas_call(
        body, name="attention_bwd", grid=(H, T // tk),
        in_specs=[pl.BlockSpec((1, T, D), lambda h, j: (h, 0, 0)),
                  pl.BlockSpec((1, tk, D), lambda h, j: (h, j, 0)),
                  pl.BlockSpec((1, tk, DV), lambda h, j: (h, j, 0)),
                  pl.BlockSpec((T, DV), lambda h, j: (0, h)),
                  pl.BlockSpec((1, 1, T), lambda h, j: (h, 0, 0)),
                  pl.BlockSpec((T, DV), lambda h, j: (0, h))],
        out_specs=[pl.BlockSpec((1, T, D), lambda h, j: (h, 0, 0)),
                   pl.BlockSpec((1, tk, D), lambda h, j: (h, j, 0)),
                   pl.BlockSpec((1, tk, DV), lambda h, j: (h, j, 0))],
        out_shape=[_sds((H, T, D), F32), _sds((H, T, D), F32), _sds((H, T, DV), F32)],
        scratch_shapes=[pltpu.VMEM((8, T), F32)],
        compiler_params=_params(("parallel", "arbitrary")),
    )(q, k, v, o, lse_row, dmix)


def _gla_block(hq, hf, hi, lower, st_in, *, rev, dot):
    rows, dk = hq.shape
    G, C = rows // CHUNK, CHUNK
    q = hq * _sigmoid(hq)
    f = lower + (1.0 - lower) * _sigmoid(hf)
    k = 1.0 - f
    logf = jnp.log(f)
    q3, k3, v3, lf3 = (t.reshape(G, C, dk) for t in (q, k, hi, logf))
    r = lax.broadcasted_iota(jnp.int32, (C, C), 0)
    c = lax.broadcasted_iota(jnp.int32, (C, C), 1)
    tri = ((r <= c) if rev else (r >= c)).astype(F32)
    b = lax.dot_general(jnp.broadcast_to(tri, (G, C, C)), lf3, (((2,), (1,)), ((0,), (0,))),
                        precision=lax.Precision.HIGHEST, preferred_element_type=F32)
    tpos = lax.broadcasted_iota(jnp.int32, (1, C, 1), 1)
    first_half = (tpos >= C // 2) if rev else (tpos <= C // 2 - 1)
    b_mid = jnp.sum(jnp.where(first_half, lf3, 0.0), axis=1, keepdims=True)
    b_last = jnp.sum(lf3, axis=1, keepdims=True)
    a = dot(q3 * jnp.exp(b - b_mid), k3 * jnp.exp(b_mid - b), "nt") * tri
    o_intra = dot(a, v3, "nn")
    kv_t = dot(v3, k3 * jnp.exp(b_last - b), "tn")
    decay = jnp.exp(b_last)
    qd = q3 * jnp.exp(b)
    st = st_in
    o_inter = [None] * G
    for g in (reversed(range(G)) if rev else range(G)):
        o_inter[g] = dot(qd[g], st, "nt")
        st = st * decay[g] + kv_t[g]
    o = o_intra.reshape(rows, dk) + jnp.concatenate(o_inter, axis=0)
    return o, st


def _gla_fwd(z, lower3, *, rev, col_q, col_f, col_v):
    T = z.shape[0]
    rows = min(GLA_GROUP * CHUNK, T)
    nb = T // rows
    blk = (lambda n: nb - 1 - n) if rev else (lambda n: n)

    def body(hq_ref, hf_ref, hi_ref, low_ref, o_ref, st_out_ref, st_ref):
        @pl.when(pl.program_id(1) == 0)
        def _():
            st_ref[...] = jnp.zeros_like(st_ref)

        st_in = st_ref[...]
        st_out_ref[0, 0] = st_in
        o, st = _gla_block(hq_ref[...], hf_ref[...], hi_ref[...], low_ref[0], st_in, rev=rev, dot=_dot_raw)
        o_ref[...] = o
        st_ref[...] = st

    def zspec(col):
        return pl.BlockSpec((rows, HG_DK), lambda h, n: (blk(n), col + h))

    return pl.pallas_call(
        body, name="gla_fwd_rev" if rev else "gla_fwd", grid=(HG_HEADS, nb),
        in_specs=[zspec(col_q), zspec(col_f), zspec(col_v), pl.BlockSpec((1, 1, HG_DK), lambda h, n: (h, 0, 0))],
        out_specs=[pl.BlockSpec((rows, HG_DK), lambda h, n: (blk(n), h)),
                   pl.BlockSpec((1, 1, HG_DK, HG_DK), lambda h, n: (h, blk(n), 0, 0))],
        out_shape=[_sds((T, HG_HEADS * HG_DK), F32), _sds((HG_HEADS, nb, HG_DK, HG_DK), F32)],
        scratch_shapes=[pltpu.VMEM((HG_DK, HG_DK), F32)],
        compiler_params=_params(("parallel", "arbitrary")),
    )(z, z, z, lower3)


def _gla_bwd(z, lower3, states, do, prev, *, rev, col_q, col_f, col_v):
    T = z.shape[0]
    rows = min(GLA_GROUP * CHUNK, T)
    nb = T // rows
    blk = (lambda n: n) if rev else (lambda n: nb - 1 - n)
    has_prev = prev is not None
    fn = functools.partial(_gla_block, rev=rev, dot=_bdot)

    def body(*refs):
        hq_ref, hf_ref, hi_ref, low_ref, st_ref, do_ref = refs[:6]
        rest = refs[6:]
        if has_prev:
            pq_ref, pi_ref = rest[:2]
            rest = rest[2:]
        dhq_ref, dhi_ref, dhf_ref, dlow_ref, dst_ref = rest
        n = pl.program_id(1)

        @pl.when(n == 0)
        def _():
            dst_ref[...] = jnp.zeros_like(dst_ref)

        _, vjp = jax.vjp(fn, hq_ref[...], hf_ref[...], hi_ref[...], low_ref[0], st_ref[0, 0])
        dhq, dhf, dhi, dlow, dst = vjp((do_ref[...], dst_ref[...]))
        dst_ref[...] = dst
        if has_prev:
            dhq = dhq + pq_ref[...]
            dhi = dhi + pi_ref[...]
        dhq_ref[...] = dhq.astype(dhq_ref.dtype)
        dhi_ref[...] = dhi.astype(dhi_ref.dtype)
        dhf_ref[...] = dhf.astype(dhf_ref.dtype)

        @pl.when(n == 0)
        def _():
            dlow_ref[0] = dlow

        @pl.when(n > 0)
        def _():
            dlow_ref[0] += dlow

    def zspec(col):
        return pl.BlockSpec((rows, HG_DK), lambda h, n: (blk(n), col + h))

    hspec = pl.BlockSpec((rows, HG_DK), lambda h, n: (blk(n), h))
    in_specs = [zspec(col_q), zspec(col_f), zspec(col_v), pl.BlockSpec((1, 1, HG_DK), lambda h, n: (h, 0, 0)),
                pl.BlockSpec((1, 1, HG_DK, HG_DK), lambda h, n: (h, blk(n), 0, 0)), hspec]
    ins = [z, z, z, lower3, states, do]
    if has_prev:
        in_specs += [hspec, hspec]
        ins += list(prev)
    wide = HG_HEADS * HG_DK
    acc_dtype = BF16 if has_prev else F32
    return pl.pallas_call(
        body, name="gla_bwd_rev" if rev else "gla_bwd", grid=(HG_HEADS, nb),
        in_specs=in_specs,
        out_specs=[hspec, hspec, hspec, pl.BlockSpec((1, 1, HG_DK), lambda h, n: (h, 0, 0))],
        out_shape=[_sds((T, wide), acc_dtype), _sds((T, wide), acc_dtype), _sds((T, wide), BF16),
                   _sds((HG_HEADS, 1, HG_DK), F32)],
        scratch_shapes=[pltpu.VMEM((HG_DK, HG_DK), F32)],
        compiler_params=_params(("parallel", "arbitrary")),
    )(*ins)


def _lower_fn(lb):
    e = jnp.exp(lb - jnp.max(lb, axis=0, keepdims=True))
    return (e / jnp.sum(e, axis=0, keepdims=True))[0]


def _lower_bounds(lb):
    def body(lb_ref, o_ref):
        o_ref[...] = _lower_fn(lb_ref[...])
    return pl.pallas_call(body, name="lower_bounds", out_shape=_sds(lb.shape[1:], F32))(lb)


def _lower_bounds_bwd(lb, dlower):
    def body(lb_ref, d_ref, o_ref):
        _, vjp = jax.vjp(_lower_fn, lb_ref[...])
        o_ref[...] = vjp(d_ref[...])[0]
    return pl.pallas_call(body, name="lower_bounds_bwd", out_shape=_sds(lb.shape, F32))(lb, dlower)


def _row_tile(r):
    for t in (512, 256, 128, 64, 32, 16, 8):
        if r % t == 0:
            return t
    return r


def _sum4(shards, recv, chip, *, name):
    _, R, C = shards.shape
    tr = _row_tile(R)

    def body(chip_ref, o_ref, r_ref, out_ref):
        out_ref[...] = ((o_ref[0].astype(F32) + r_ref[0].astype(F32)) + r_ref[1].astype(F32)) + r_ref[2].astype(F32)

    grid_spec = pltpu.PrefetchScalarGridSpec(
        num_scalar_prefetch=1, grid=(R // tr,),
        in_specs=[pl.BlockSpec((1, tr, C), lambda i, chip_ref: (chip_ref[0], i, 0)),
                  pl.BlockSpec((3, tr, C), lambda i, chip_ref: (0, i, 0))],
        out_specs=pl.BlockSpec((tr, C), lambda i, chip_ref: (i, 0)))
    return pl.pallas_call(
        body, name=name, grid_spec=grid_spec, out_shape=_sds((R, C), F32), compiler_params=_params(("parallel",)),
    )(chip, shards, recv)


def _adamw_math(w, g, m, v):
    m = ADAM_B1 * m + (1.0 - ADAM_B1) * g
    v = ADAM_B2 * v + (1.0 - ADAM_B2) * (g * g)
    m_hat = m / (1.0 - ADAM_B1 ** ADAM_STEP)
    v_hat = v / (1.0 - ADAM_B2 ** ADAM_STEP)
    delta = -ADAM_LR * (m_hat / (jnp.sqrt(v_hat) + ADAM_EPS) + ADAM_WD * w)
    return delta, m, v


def _adamw(w, g_a, g_b, m, v, *, name):
    R, C = w.shape
    tr = _row_tile(R)
    two = g_b is not None

    def body(*refs):
        w_ref, ga_ref = refs[0], refs[1]
        rest = refs[2:]
        g = ga_ref[...]
        if two:
            g = g + rest[0][...]
            rest = rest[1:]
        m_ref, v_ref, g_out, d_out, m_out, v_out = rest
        delta, m_new, v_new = _adamw_math(w_ref[...], g, m_ref[...], v_ref[...])
        g_out[...] = g
        d_out[...] = delta
        m_out[...] = m_new
        v_out[...] = v_new

    spec = pl.BlockSpec((tr, C), lambda i: (i, 0))
    ins = [w, g_a] + ([g_b] if two else []) + [m, v]
    return pl.pallas_call(
        body, name=name, grid=(R // tr,), in_specs=[spec] * len(ins), out_specs=[spec] * 4,
        out_shape=[_sds((R, C), F32)] * 4, compiler_params=_params(("parallel",)),
    )(*ins)


def _chip_peers():
    x, y, c = lax.axis_index("x"), lax.axis_index("y"), lax.axis_index("c")
    return (x, y, c), 2 * x + y, [(1 - x, y), (x, 1 - y), (1 - x, 1 - y)]


_HBM = pl.BlockSpec(memory_space=pltpu.HBM)
_SEM = pl.BlockSpec(memory_space=pltpu.SEMAPHORE)
_EFFECT = pltpu.SideEffectType.DATAFLOW_SIDE_EFFECTING


def _chip_copies(srcs, lands, sems, gather):
    (x, y, c), me, chips = _chip_peers()
    copies = []
    for t, (src, land) in enumerate(zip(srcs, lands)):
        for k, (px, py) in enumerate(chips):
            copies.append(pltpu.make_async_remote_copy(
                src if gather else src.at[2 * px + py], land.at[me] if gather else land.at[k],
                sems[0].at[3 * t + k], sems[1].at[3 * t + k], device_id=(px, py, c), device_id_type=MESH))
        if gather:
            copies.append(pltpu.make_async_copy(src, land.at[me], sems[2].at[t]))
    return copies


def _exchange_start(srcs, *, gather, name):
    n = len(srcs)
    n_sem = 3 if gather else 2
    lands = [_sds(((N_CHIPS,) + s.shape) if gather else ((3,) + s.shape[1:]), s.dtype) for s in srcs]

    def body(*refs):
        for cp in _chip_copies(refs[:n], refs[n:2 * n], refs[2 * n:2 * n + n_sem], gather):
            cp.start()
        token = refs[-1]
        token[...] = jnp.zeros_like(token)

    sem_shapes = [pltpu.SemaphoreType.DMA((3 * n,)), pltpu.SemaphoreType.DMA((3 * n,))]
    sem_shapes += [pltpu.SemaphoreType.DMA((n,))] if gather else []
    thru = [pltpu.HBM(s.shape, s.dtype) for s in srcs] + [pltpu.HBM(l.shape, l.dtype) for l in lands]
    res = pl.pallas_call(
        body, name=name, in_specs=[_HBM] * (2 * n),
        out_specs=[_SEM] * n_sem + [_HBM] * (2 * n) + [pl.BlockSpec(memory_space=pltpu.VMEM)],
        out_shape=sem_shapes + thru + [_sds((8, LANE), F32)], input_output_aliases={i: n_sem + i for i in range(2 * n)},
        compiler_params=pltpu.CompilerParams(has_side_effects=_EFFECT),
    )(*[pltpu.with_memory_space_constraint(s, pltpu.HBM) for s in srcs],
      *[pltpu.with_memory_space_constraint(lax.empty(l.shape, l.dtype), pltpu.HBM) for l in lands])
    return (res[:n_sem], res[n_sem:n_sem + n], res[n_sem + n:n_sem + 2 * n]), res[-1]


def _exchange_wait(started, after, *, name):
    sems, srcs, lands = started
    n, n_sem = len(srcs), len(sems)

    def body(*refs):
        for cp in _chip_copies(refs[:n], refs[n:2 * n], refs[2 * n:2 * n + n_sem], n_sem == 3):
            cp.wait()

    res = pl.pallas_call(
        body, name=name, in_specs=[_HBM] * (2 * n) + [_SEM] * n_sem + [_ANY], out_specs=[_HBM] * (2 * n),
        out_shape=[pltpu.HBM(a.shape, a.dtype) for a in list(srcs) + list(lands)],
        input_output_aliases={i: i for i in range(2 * n)},
        compiler_params=pltpu.CompilerParams(has_side_effects=_EFFECT),
    )(*srcs, *lands, *sems, after)
    return res[:n], res[n:]


def _swap_sibling(parts):
    n = len(parts)

    def body(*refs):
        ins, outs = refs[:n], refs[n:2 * n]
        send_sems, recv_sems = refs[2 * n:]
        x, y, c = lax.axis_index("x"), lax.axis_index("y"), lax.axis_index("c")
        copies = []
        for t in range(n):
            cp = pltpu.make_async_remote_copy(ins[t], outs[t], send_sems.at[t], recv_sems.at[t],
                                              device_id=(x, y, 1 - c), device_id_type=MESH)
            cp.start()
            copies.append(cp)
        for cp in copies:
            cp.wait()

    return pl.pallas_call(
        body, name="swap_sibling", in_specs=[_ANY] * n, out_specs=[_ANY] * n,
        out_shape=[_sds(p.shape, p.dtype) for p in parts],
        scratch_shapes=[pltpu.SemaphoreType.DMA((n,)), pltpu.SemaphoreType.DMA((n,))],
        compiler_params=_params(),
    )(*parts)


def _allreduce_small(pack):
    R, C = pack.shape

    def body(in_ref, out_ref, slots, send_sems, recv_sems):
        x, y, c = lax.axis_index("x"), lax.axis_index("y"), lax.axis_index("c")
        me = 4 * x + 2 * y + c
        slots[me] = in_ref[...]
        copies = []
        for k in range(1, N_DEV):
            peer = (x ^ ((k >> 2) & 1), y ^ ((k >> 1) & 1), c ^ (k & 1))
            cp = pltpu.make_async_remote_copy(in_ref, slots.at[me], send_sems.at[k - 1], recv_sems.at[k - 1],
                                              device_id=peer, device_id_type=MESH)
            cp.start()
            copies.append(cp)
        for cp in copies:
            cp.wait()
        acc = slots[0]
        for d in range(1, N_DEV):
            acc = acc + slots[d]
        out_ref[...] = acc

    return pl.pallas_call(
        body, name="allreduce_small", out_shape=_sds((R, C), F32),
        in_specs=[pl.BlockSpec(memory_space=pltpu.VMEM)], out_specs=pl.BlockSpec(memory_space=pltpu.VMEM),
        scratch_shapes=[pltpu.VMEM((N_DEV, R, C), F32), pltpu.SemaphoreType.DMA((N_DEV - 1,)),
                        pltpu.SemaphoreType.DMA((N_DEV - 1,))],
        compiler_params=_params(),
    )(pack)


_Z_CQ, _Z_CKV, _Z_HQ, _Z_HFF, _Z_HFB, _Z_HI, _Z_HG, _Z_KR, _Z_END = 0, 256, 512, 1024, 1536, 2048, 2560, 3072, 3200


def _to_z_layout(w):
    pad = jnp.zeros((w.shape[0], _Z_END - _Z_KR - QK_ROPE), w.dtype)
    return jnp.concatenate([w[:, :512], w[:, 512 + QK_ROPE:], w[:, 512:512 + QK_ROPE], pad], axis=1)


def _from_z_layout(w):
    return jnp.concatenate([w[:, :512], w[:, _Z_KR:_Z_KR + QK_ROPE], w[:, 512:_Z_KR]], axis=1)


def _col_shards_to_full(g):
    return jnp.transpose(g, (1, 0, 2)).reshape(g.shape[1], -1)


def _full_to_col_shards(w):
    r, c = w.shape
    return jnp.transpose(w.reshape(r, N_CHIPS, c // N_CHIPS), (1, 0, 2))


def _full_to_row_shards(w):
    r, c = w.shape
    return w.reshape(N_CHIPS, r // N_CHIPS, c)


def kernel(x, p, positions, g_mix, w_in, g_qa, g_kva, w_qb, w_kvb, g_qn, g_kn, lb_param, g_hgo, w_o, g_ffn, w_gate, w_up, w_down, g_ple, w_ple_gate, w_ple_proj, loss_target, m_g_mix, m_w_in, m_g_qa, m_g_kva, m_w_qb, m_w_kvb, m_g_qn, m_g_kn, m_lb_param, m_g_hgo, m_w_o, m_g_ffn, m_w_gate, m_w_up, m_w_down, m_g_ple, m_w_ple_gate, m_w_ple_proj, v_g_mix, v_w_in, v_g_qa, v_g_kva, v_w_qb, v_w_kvb, v_g_qn, v_g_kn, v_lb_param, v_g_hgo, v_w_o, v_g_ffn, v_w_gate, v_w_up, v_w_down, v_g_ple, v_w_ple_gate, v_w_ple_proj):
    w_named = dict(g_mix=g_mix, w_in=w_in, g_qa=g_qa, g_kva=g_kva, w_qb=w_qb, w_kvb=w_kvb, g_qn=g_qn, g_kn=g_kn,
                   lb_param=lb_param, g_hgo=g_hgo, w_o=w_o, g_ffn=g_ffn, w_gate=w_gate, w_up=w_up, w_down=w_down,
                   g_ple=g_ple, w_ple_gate=w_ple_gate, w_ple_proj=w_ple_proj)
    m_named = dict(g_mix=m_g_mix, w_in=m_w_in, g_qa=m_g_qa, g_kva=m_g_kva, w_qb=m_w_qb, w_kvb=m_w_kvb, g_qn=m_g_qn,
                   g_kn=m_g_kn, lb_param=m_lb_param, g_hgo=m_g_hgo, w_o=m_w_o, g_ffn=m_g_ffn, w_gate=m_w_gate,
                   w_up=m_w_up, w_down=m_w_down, g_ple=m_g_ple, w_ple_gate=m_w_ple_gate, w_ple_proj=m_w_ple_proj)
    v_named = dict(g_mix=v_g_mix, w_in=v_w_in, g_qa=v_g_qa, g_kva=v_g_kva, w_qb=v_w_qb, w_kvb=v_w_kvb, g_qn=v_g_qn,
                   g_kn=v_g_kn, lb_param=v_lb_param, g_hgo=v_g_hgo, w_o=v_w_o, g_ffn=v_g_ffn, w_gate=v_w_gate,
                   w_up=v_w_up, w_down=v_w_down, g_ple=v_g_ple, w_ple_gate=v_w_ple_gate, w_ple_proj=v_w_ple_proj)
    order = list(w_named)
    col_sharded = ("w_in", "w_qb", "w_kvb", "w_gate", "w_up", "w_ple_proj")
    row_sharded = ("w_o", "w_down", "w_ple_gate")
    big = col_sharded + row_sharded

    x2d, p2d, tgt = x[0], p[0, 0], loss_target[0]
    T, D = x2d.shape

    lb_flat = lb_param.reshape(-1, lb_param.shape[-1])
    gather_groups = (("w_in", "w_qb", "w_kvb"), ("w_o", "w_gate", "w_up"), ("w_down", "w_ple_gate", "w_ple_proj"))
    gather_started = []
    for gi, names in enumerate(gather_groups):
        srcs = [w_named[n][0].astype(BF16) for n in names] + ([lb_flat] if gi == 0 else [])
        started, gather_token = _exchange_start(srcs, gather=True, name=f"gather_start_{gi}")
        gather_started.append(started)
    full = {}

    def gather_wait(gi, after):
        _, got = _exchange_wait(gather_started[gi], after, name=f"gather_wait_{gi}")
        for n, g in zip(gather_groups[gi], got):
            full[n] = _col_shards_to_full(g) if n in col_sharded else g.reshape(-1, g.shape[-1])
        return got

    inv_freq = ROPE_THETA ** (-jnp.arange(0, QK_ROPE, 2, dtype=F32) / QK_ROPE)
    ang = positions[0].astype(F32)[:, None] * inv_freq
    cos, sin = jnp.cos(ang), jnp.sin(ang)
    g_hgo_row = g_hgo.reshape(1, -1)

    h1 = _stage(_rms, [x2d], [g_mix], [_sds((T, D), BF16)], [], name="norm_mix", after=gather_token)[0]
    got = gather_wait(0, h1)
    lb_full = _col_shards_to_full(got[-1]).reshape(lb_param.shape[0], lb_param.shape[1], -1)
    w_in_z = _to_z_layout(full["w_in"])
    z = _mm(h1, w_in_z, name="in_proj")
    qn, kvn = _stage(_mla_a_fn, [_cols(z, 256, 0), _cols(z, 256, 1)], [g_qa, g_kva],
                     [_sds((T, 256), BF16), _sds((T, 256), BF16)], [], name="mla_latent_norm")
    q_raw = _mm(qn, full["w_qb"], name="q_up")
    kv_raw = _mm(kvn, full["w_kvb"], name="kv_up")
    kr = _cols(z, LANE, _Z_KR // LANE)
    q, k, v = _stage(_mla_b_fn, [q_raw, kv_raw, kr, cos, sin], [g_qn, g_kn],
                     [_sds((MLA_HEADS, T, QK_HEAD), BF16), _sds((MLA_HEADS, T, QK_HEAD), BF16),
                      _sds((MLA_HEADS, T, V_HEAD), BF16)], [], name="mla_qk_norm_rope")
    att, lse = _attention_fwd(q, k, v)

    lower = _lower_bounds(lb_full)
    lower3 = lower.reshape(2, HG_HEADS, 1, HG_DK)
    cq_blk, cf_blk, cb_blk, cv_blk = _Z_HQ // LANE, _Z_HFF // LANE, _Z_HFB // LANE, _Z_HI // LANE
    o_f, st_f = _gla_fwd(z, lower3[0], rev=False, col_q=cq_blk, col_f=cf_blk, col_v=cv_blk)
    o_b, st_b = _gla_fwd(z, lower3[1], rev=True, col_q=cq_blk, col_f=cb_blk, col_v=cv_blk)
    hg = _cols(z, 512, _Z_HG // 512)
    mix = _stage(_post_fn, [att, o_f, o_b, hg], [g_hgo_row], [_sds((T, att.shape[1] + o_f.shape[1]), BF16)], [],
                 name="mix_out")[0]
    gather_wait(1, mix)
    x1 = _mm(mix, full["w_o"], add=x2d, name="out_proj")
    h2 = _stage(_rms, [x1], [g_ffn], [_sds((T, D), BF16)], [], name="norm_ffn")[0]
    gt = _mm(h2, full["w_gate"], name="ffn_gate")
    up = _mm(h2, full["w_up"], name="ffn_up")
    act = _stage(_swiglu_fn, [gt, up], [], [_sds(gt.shape, BF16)], [], name="swiglu")[0]
    gather_wait(2, act)
    x2 = _mm(act, full["w_down"], add=x1, name="ffn_down")
    h3 = _stage(_rms, [x2], [g_ple], [_sds((T, D), BF16)], [], name="norm_ple")[0]
    pg = _mm(h3, full["w_ple_gate"], name="ple_gate")
    pp = _mm(p2d, full["w_ple_proj"], name="ple_proj")
    dx3, dpg, dpp, loss_part = _stage(_ple_loss_fn, [x2, pg, pp, tgt], [],
                                      [_sds((T, D), F32), _sds((T, D), BF16), _sds((T, D), BF16)],
                                      [_sds((1, 1), F32)], name="ple_loss")

    grads = {}
    scatter_groups = (("w_ple_proj", "w_ple_gate", "w_down"), ("w_gate", "w_up", "w_o"), ("w_qb", "w_kvb", "w_in"))
    scatter_started = []

    def scatter_start(gi):
        srcs = [_full_to_col_shards(grads[n]) if n in col_sharded else _full_to_row_shards(grads[n])
                for n in scatter_groups[gi]]
        started, token = _exchange_start(srcs, gather=False, name=f"scatter_start_{gi}")
        scatter_started.append(started)
        return token

    grads["w_ple_proj"] = _mm(p2d, dpp, ta=True, out_dtype=BF16, name="d_w_ple_proj")
    grads["w_ple_gate"] = _mm(h3, dpg, ta=True, out_dtype=BF16, name="d_w_ple_gate")
    dh3 = _mm(dpg, full["w_ple_gate"], tb=True, name="d_h3")
    dx2, grads["g_ple"] = _stage(_norm_bwd_fn, [x2, dh3, dx3], [g_ple], [_sds((T, D), F32)], [_sds((1, D), F32)],
                                 name="norm_ple_bwd")
    dact = _mm(dx2, full["w_down"], tb=True, name="d_act")
    grads["w_down"] = _mm(act, dx2, ta=True, out_dtype=BF16, name="d_w_down")
    token = scatter_start(0)
    dgt, dup = _stage(_swiglu_bwd_fn, [gt, up, dact], [], [_sds(gt.shape, BF16), _sds(gt.shape, BF16)], [],
                      name="swiglu_bwd", after=token)
    grads["w_gate"] = _mm(h2, dgt, ta=True, out_dtype=BF16, name="d_w_gate")
    grads["w_up"] = _mm(h2, dup, ta=True, out_dtype=BF16, name="d_w_up")
    dh2 = _mm(dgt, full["w_gate"], tb=True, name="d_h2_gate")
    dh2 = _mm(dup, full["w_up"], tb=True, add=dh2, name="d_h2_up")
    dx1, grads["g_ffn"] = _stage(_norm_bwd_fn, [x1, dh2, dx2], [g_ffn], [_sds((T, D), F32)], [_sds((1, D), F32)],
                                 name="norm_ffn_bwd")
    dmix = _mm(dx1, full["w_o"], tb=True, name="d_mix")
    grads["w_o"] = _mm(mix, dx1, ta=True, out_dtype=BF16, name="d_w_o")

    token = scatter_start(1)
    half = MLA_HEADS * V_HEAD
    do, dhg, dg_hgo = _stage(_post_bwd_fn, [o_f, o_b, hg, _cols(dmix, half, 1)], [g_hgo_row],
                             [_sds((T, half), F32), _sds((T, half), BF16)], [_sds((1, half), F32)], name="mix_out_bwd",
                             after=token)
    grads["g_hgo"] = dg_hgo
    dhq_f, dhi_f, dhf_f, dlow_f = _gla_bwd(z, lower3[0], st_f, do, None, rev=False,
                                           col_q=cq_blk, col_f=cf_blk, col_v=cv_blk)
    dhq, dhi, dhf_b, dlow_b = _gla_bwd(z, lower3[1], st_b, do, (dhq_f, dhi_f), rev=True,
                                       col_q=cq_blk, col_f=cb_blk, col_v=cv_blk)

    lse_row = lse.reshape(MLA_HEADS, 1, T)
    dq, dk, dv = _attention_bwd(q, k, v, att, lse_row, dmix)
    dq_raw, dkv_raw, dkr, grads["g_qn"], grads["g_kn"] = _stage(
        _mla_b_bwd_fn, [q_raw, kv_raw, kr, cos, sin, dq, dk, dv], [g_qn, g_kn],
        [_sds(q_raw.shape, BF16), _sds(kv_raw.shape, BF16), _sds((T, LANE), BF16)],
        [_sds(g_qn.shape, F32), _sds(g_kn.shape, F32)], name="mla_qk_norm_rope_bwd")
    grads["w_qb"] = _mm(qn, dq_raw, ta=True, out_dtype=BF16, name="d_w_qb")
    grads["w_kvb"] = _mm(kvn, dkv_raw, ta=True, out_dtype=BF16, name="d_w_kvb")
    dqn = _mm(dq_raw, full["w_qb"], tb=True, name="d_qn")
    dkvn = _mm(dkv_raw, full["w_kvb"], tb=True, name="d_kvn")
    dcq, dckv, grads["g_qa"], grads["g_kva"] = _stage(
        _mla_a_bwd_fn, [_cols(z, 256, 0), _cols(z, 256, 1), dqn, dkvn], [g_qa, g_kva],
        [_sds((T, 256), BF16), _sds((T, 256), BF16)], [_sds(g_qa.shape, F32), _sds(g_kva.shape, F32)],
        name="mla_latent_norm_bwd")
    dz = jnp.concatenate([dcq, dckv, dhq, dhf_f, dhf_b, dhi, dhg, dkr], axis=1)
    grads["w_in"] = _from_z_layout(_mm(h1, dz, ta=True, out_dtype=BF16, name="d_w_in"))
    token = scatter_start(2)
    dh1 = _mm(dz, w_in_z, tb=True, name="d_h1", after=token)
    grad_x, grads["g_mix"] = _stage(_norm_bwd_fn, [x2d, dh1, dx1], [g_mix], [_sds((T, D), F32)], [_sds((1, D), F32)],
                                    name="norm_mix_bwd")

    chip = 2 * lax.axis_index("x") + lax.axis_index("y")
    partial_sum = {}
    after = grad_x
    for gi, names in enumerate(scatter_groups):
        shards, recvs = _exchange_wait(scatter_started[gi], after, name=f"scatter_wait_{gi}")
        for n, s, r in zip(names, shards, recvs):
            partial_sum[n] = _sum4(s, r, chip.reshape(1), name="sum_" + n)
        after = partial_sum[names[-1]]
    partial_sums = [partial_sum[n] for n in big]
    sibling_sums = _swap_sibling(partial_sums)

    small = ("g_mix", "g_qa", "g_kva", "g_qn", "g_kn", "g_hgo", "g_ffn", "g_ple")
    small_all = small + ("lb_param",)
    width = -(-max(w_named[n].size for n in small_all) // LANE) * LANE

    def row(a):
        a = a.reshape(1, -1)
        return jnp.pad(a, ((0, 0), (0, width - a.shape[1])))

    dlower = jnp.concatenate([dlow_f.reshape(1, -1), dlow_b.reshape(1, -1)], axis=0)
    pack = jnp.concatenate([row(grads[n]) for n in small] + [row(dlower[0]), row(dlower[1]), row(loss_part)]
                           + [jnp.zeros((5, width), F32)], axis=0)
    red = _allreduce_small(pack)
    loss = red[10, 0]
    dlower_sum = red[8:10, :lb_full.shape[-1]]
    dlb_full = _lower_bounds_bwd(lb_full, dlower_sum)
    fshard = lb_param.shape[-1]
    dlb = lax.dynamic_slice_in_dim(dlb_full, chip * fshard, fshard, axis=2)

    out_g, out_d, out_m, out_v = {}, {}, {}, {}
    for n, mine, theirs in zip(big, partial_sums, sibling_sums):
        out_g[n], out_d[n], out_m[n], out_v[n] = (
            t[None] for t in _adamw(w_named[n][0], mine, theirs, m_named[n][0], v_named[n][0], name="adamw_" + n))
    g_rows = [red[i:i + 1] for i in range(len(small))] + [row(dlb)]
    g_pack = jnp.concatenate(g_rows + [jnp.zeros((16 - len(g_rows), width), F32)], axis=0)

    def packed(named):
        rows = [row(named[n]) for n in small_all]
        return jnp.concatenate(rows + [jnp.ones((16 - len(rows), width), F32)], axis=0)

    s_g, s_d, s_m, s_v = _adamw(packed(w_named), g_pack, None, packed(m_named), packed(v_named), name="adamw_small")
    for i, n in enumerate(small_all):
        size = w_named[n].size
        for src, dst in ((s_g, out_g), (s_d, out_d), (s_m, out_m), (s_v, out_v)):
            dst[n] = src[i, :size].reshape(w_named[n].shape)

    return (loss, grad_x[None], *[out_g[n] for n in order], *[out_d[n] for n in order],
            *[out_m[n] for n in order], *[out_v[n] for n in order])
```

```python
import functools

import jax
import jax.numpy as jnp
from jax import lax
from jax.experimental import pallas as pl
from jax.experimental.pallas import tpu as pltpu

F32 = jnp.float32
BF16 = jnp.bfloat16
MESH = pl.DeviceIdType.MESH

EPS = 1e-6
ROPE_THETA = 10000.0
MLA_HEADS = 4
QK_NOPE = 128
QK_ROPE = 64
QK_HEAD = QK_NOPE + QK_ROPE
V_HEAD = 128
HG_HEADS = 4
HG_DK = 128
CHUNK = 64
ADAM_LR = 0.001
ADAM_B1 = 0.9
ADAM_B2 = 0.999
ADAM_EPS = 1e-08
ADAM_WD = 0.01
ADAM_STEP = 10

LANE = 128
VMEM_LIMIT = 56 * 1024 * 1024
TOK_TILE = 256
GLA_GROUP = 8
ATT_TQ = 256
ATT_TK = 256
N_CHIPS = 4
N_DEV = 8


_ANY = pl.BlockSpec(memory_space=pl.ANY)


def _params(dims=None, **kw):
    return pltpu.CompilerParams(dimension_semantics=dims, vmem_limit_bytes=VMEM_LIMIT, **kw)


def _tile_candidates(n, cap):
    out = [d for d in range(LANE, min(n, cap) + 1, LANE) if n % d == 0]
    if n <= cap and n not in out:
        out.append(n)
    return out or [n]


MM_VMEM_BUDGET = 34 * 1024 * 1024
MM_MAX_ROWS = 1536
HBM_BYTES_PER_S = 2.8e12
MXU_FLOPS_PER_S = 8e14
STEP_OVERHEAD_S = 0.35e-6


def _mm_tiles(M, N, K, a_bytes, b_bytes, o_bytes, has_add):
    best = None
    for tm in _tile_candidates(M, MM_MAX_ROWS):
        for tn in _tile_candidates(N, N):
            for tk in _tile_candidates(K, K):
                ni, nj, nk = M // tm, N // tn, K // tk
                vmem = 2 * (tm * tk * a_bytes + tk * tn * b_bytes + tm * tn * o_bytes * (2 if has_add else 1))
                vmem += tm * tn * 4 * (2 if nk > 1 else 1)
                vmem += (tm * tk * 2 if a_bytes > 2 else 0) + (tk * tn * 2 if b_bytes > 2 else 0)
                if vmem > MM_VMEM_BUDGET:
                    continue
                moved = M * K * a_bytes * (nj if nk > 1 else 1) + K * N * b_bytes * (1 if nj == nk == 1 else ni)
                moved += M * N * o_bytes * (2 if has_add else 1)
                t = max(moved / HBM_BYTES_PER_S, 2 * M * N * K / MXU_FLOPS_PER_S) + ni * nj * nk * STEP_OVERHEAD_S
                if best is None or t < best[0]:
                    best = (t, tm, tn, tk)
    assert best is not None, (M, N, K)
    return best[1:]


def _dot_raw(a, b, kind):
    nb = a.ndim - 2
    batch = ((0,), (0,)) if nb else ((), ())
    ca = nb if kind == "tn" else nb + 1
    cb = nb + 1 if kind == "nt" else nb
    return lax.dot_general(a.astype(BF16), b.astype(BF16), (((ca,), (cb,)), batch), preferred_element_type=F32)


@functools.partial(jax.custom_vjp, nondiff_argnums=(2,))
def _bdot(a, b, kind):
    return _dot_raw(a, b, kind)


def _bdot_fwd(a, b, kind):
    return _dot_raw(a, b, kind), (a, b)


def _bdot_bwd(kind, res, g):
    a, b = res
    if kind == "nn":
        da, db = _bdot(g, b, "nt"), _bdot(a, g, "tn")
    elif kind == "nt":
        da, db = _bdot(g, b, "nn"), _bdot(g, a, "tn")
    else:
        da, db = _bdot(b, g, "nt"), _bdot(a, g, "nn")
    return da.astype(a.dtype), db.astype(b.dtype)


_bdot.defvjp(_bdot_fwd, _bdot_bwd)


def _mm(a, b, *, name, ta=False, tb=False, add=None, out_dtype=F32, after=None):
    K, M = a.shape if ta else a.shape[::-1]
    N, Kb = b.shape if tb else b.shape[::-1]
    assert K == Kb, (a.shape, b.shape, ta, tb)
    tm, tn, tk = _mm_tiles(M, N, K, a.dtype.itemsize, b.dtype.itemsize, jnp.dtype(out_dtype).itemsize, add is not None)
    nk = K // tk
    kind = "tn" if ta else ("nt" if tb else "nn")
    assert not (ta and tb)
    a_spec = pl.BlockSpec((tk, tm), lambda i, j, k: (k, i)) if ta else pl.BlockSpec((tm, tk), lambda i, j, k: (i, k))
    b_spec = pl.BlockSpec((tn, tk), lambda i, j, k: (j, k)) if tb else pl.BlockSpec((tk, tn), lambda i, j, k: (k, j))
    o_spec = pl.BlockSpec((tm, tn), lambda i, j, k: (i, j))
    has_add = add is not None

    def body(*refs):
        a_ref, b_ref = refs[0], refs[1]
        add_ref = refs[2] if has_add else None
        o_ref = refs[n_in]
        part = _dot_raw(a_ref[...], b_ref[...], kind)
        if nk == 1:
            if has_add:
                part = part + add_ref[...].astype(F32)
            o_ref[...] = part.astype(o_ref.dtype)
            return
        acc_ref = refs[-1]
        k = pl.program_id(2)

        @pl.when(k == 0)
        def _():
            acc_ref[...] = part

        @pl.when(k > 0)
        def _():
            acc_ref[...] += part

        @pl.when(k == nk - 1)
        def _():
            r = acc_ref[...]
            if has_add:
                r = r + add_ref[...].astype(F32)
            o_ref[...] = r.astype(o_ref.dtype)

    ins = [a, b] + ([add] if has_add else []) + ([after] if after is not None else [])
    in_specs = [a_spec, b_spec] + ([o_spec] if has_add else []) + ([_ANY] if after is not None else [])
    n_in = len(ins)
    return pl.pallas_call(
        body, name=name, grid=(M // tm, N // tn, nk), in_specs=in_specs, out_specs=o_spec,
        out_shape=jax.ShapeDtypeStruct((M, N), out_dtype),
        scratch_shapes=[pltpu.VMEM((tm, tn), F32)] if nk > 1 else [],
        compiler_params=_params(("parallel", "parallel", "arbitrary")),
    )(*ins)


def _cols(arr, width, block):
    return (arr, width, block)


def _stage(fn, tiles, params, out_tiles, out_sums, *, name, tile=TOK_TILE, after=None):
    def tok_spec(shape, width=None, block=0):
        if len(shape) == 2:
            w = shape[1] if width is None else width
            return pl.BlockSpec((tile, w), lambda i: (i, block))
        return pl.BlockSpec((shape[0], tile, shape[2]), lambda i: (0, i, 0))

    arrays, in_specs = [], []
    for t in tiles:
        if isinstance(t, tuple):
            arr, width, block = t
            arrays.append(arr)
            in_specs.append(tok_spec(arr.shape, width, block))
        else:
            arrays.append(t)
            in_specs.append(tok_spec(t.shape))
    n_tok = arrays[0].shape[0] if arrays[0].ndim == 2 else arrays[0].shape[1]
    for p in params:
        arrays.append(p)
        in_specs.append(pl.BlockSpec(p.shape, lambda i, nd=p.ndim: (0,) * nd))
    out_shape = list(out_tiles) + list(out_sums)
    out_specs = [tok_spec(o.shape) for o in out_tiles]
    out_specs += [pl.BlockSpec(o.shape, lambda i, nd=len(o.shape): (0,) * nd) for o in out_sums]
    n_fn, n_ot = len(arrays), len(out_tiles)
    if after is not None:
        arrays.append(after)
        in_specs.append(_ANY)
    n_in = len(arrays)

    def body(*refs):
        res = fn(*[r[...] for r in refs[:n_fn]])
        if not isinstance(res, (tuple, list)):
            res = (res,)
        outs = refs[n_in:]
        for o_ref, r in zip(outs[:n_ot], res[:n_ot]):
            o_ref[...] = r.astype(o_ref.dtype)
        i = pl.program_id(0)
        for o_ref, r in zip(outs[n_ot:], res[n_ot:]):
            @pl.when(i == 0)
            def _(o_ref=o_ref, r=r):
                o_ref[...] = r.astype(o_ref.dtype)

            @pl.when(i > 0)
            def _(o_ref=o_ref, r=r):
                o_ref[...] += r.astype(o_ref.dtype)

    res = pl.pallas_call(
        body, name=name, grid=(n_tok // tile,), in_specs=in_specs, out_specs=out_specs, out_shape=out_shape,
        compiler_params=_params(("arbitrary",)),
    )(*arrays)
    return res


def _sds(shape, dtype):
    return jax.ShapeDtypeStruct(tuple(shape), dtype)


def _sigmoid(x):
    return 1.0 / (1.0 + jnp.exp(-x))


def _rms(x, g):
    return x * lax.rsqrt(jnp.mean(x * x, axis=-1, keepdims=True) + EPS) * g


def _norm_bwd_fn(x, dh, dres, g):
    _, vjp = jax.vjp(_rms, x, g)
    dx, dg = vjp(dh)
    return dx + dres, dg


def _mla_a_fn(cq, ckv, g_qa, g_kva):
    return _rms(cq, g_qa), _rms(ckv, g_kva)


def _mla_a_bwd_fn(cq, ckv, dqn, dkvn, g_qa, g_kva):
    _, vjp = jax.vjp(_mla_a_fn, cq, ckv, g_qa, g_kva)
    return vjp((dqn, dkvn))


def _rope(t, cos, sin):
    t1, t2 = t[:, :QK_ROPE // 2], t[:, QK_ROPE // 2:]
    return jnp.concatenate([t1 * cos - t2 * sin, t1 * sin + t2 * cos], axis=-1)


def _mla_b_fn(q_raw, kv_raw, kr, cos, sin, g_qn, g_kn):
    krope = kr[:, :QK_ROPE]
    qs, ks, vs = [], [], []
    for h in range(MLA_HEADS):
        qh = _rms(q_raw[:, h * QK_HEAD:(h + 1) * QK_HEAD], g_qn)
        kvh = kv_raw[:, h * (QK_NOPE + V_HEAD):(h + 1) * (QK_NOPE + V_HEAD)]
        kh = _rms(jnp.concatenate([kvh[:, :QK_NOPE], krope], axis=-1), g_kn)
        qs.append(jnp.concatenate([qh[:, :QK_NOPE], _rope(qh[:, QK_NOPE:], cos, sin)], axis=-1))
        ks.append(jnp.concatenate([kh[:, :QK_NOPE], _rope(kh[:, QK_NOPE:], cos, sin)], axis=-1))
        vs.append(kvh[:, QK_NOPE:])
    return jnp.stack(qs), jnp.stack(ks), jnp.stack(vs)


def _mla_b_bwd_fn(q_raw, kv_raw, kr, cos, sin, dq, dk, dv, g_qn, g_kn):
    _, vjp = jax.vjp(lambda a, b, c, d, e: _mla_b_fn(a, b, c, cos, sin, d, e), q_raw, kv_raw, kr, g_qn, g_kn)
    return vjp((dq, dk, dv))


def _post_fn(a, o_f, o_b, hg, g_hgo):
    o = o_f + o_b
    parts = [a]
    for h in range(HG_HEADS):
        s = slice(h * HG_DK, (h + 1) * HG_DK)
        gate = hg[:, s]
        parts.append(_rms(o[:, s], g_hgo[:, s]) * (gate * _sigmoid(gate)))
    return jnp.concatenate(parts, axis=-1)


def _post_bwd_fn(o_f, o_b, hg, dr, g_hgo):
    def f(o, hg, g):
        return _post_fn(jnp.zeros_like(o), o, jnp.zeros_like(o), hg, g)[:, o.shape[1]:]
    _, vjp = jax.vjp(f, o_f + o_b, hg, g_hgo)
    return vjp(dr)


def _swiglu_fn(gt, up):
    return gt * _sigmoid(gt) * up


def _swiglu_bwd_fn(gt, up, dact):
    _, vjp = jax.vjp(_swiglu_fn, gt, up)
    return vjp(dact)


def _ple_loss_fn(x2, pg, pp, target):
    gate = _sigmoid(pg)
    err = x2 + gate * pp - target
    dx3 = err * (1.0 / err.shape[-1])
    loss = 0.5 * jnp.sum(jnp.mean(err * err, axis=-1, keepdims=True), axis=0, keepdims=True)
    return dx3, dx3 * pp * gate * (1.0 - gate), dx3 * gate, loss


def _attention_fwd(q, k, v):
    H, T, D = q.shape
    DV = v.shape[-1]
    tq = min(ATT_TQ, T)
    scale = D ** -0.5

    def body(q_ref, k_ref, v_ref, o_ref, lse_ref):
        s = _dot_raw(q_ref[0], k_ref[0], "nt") * scale
        m = jnp.max(s, axis=-1, keepdims=True)
        e = jnp.exp(s - m)
        l = jnp.sum(e, axis=-1, keepdims=True)
        o_ref[...] = _dot_raw(e / l, v_ref[0], "nn")
        lse_ref[0] = m + jnp.log(l)

    return pl.pallas_call(
        body, name="attention_fwd", grid=(H, T // tq),
        in_specs=[pl.BlockSpec((1, tq, D), lambda h, i: (h, i, 0)),
                  pl.BlockSpec((1, T, D), lambda h, i: (h, 0, 0)),
                  pl.BlockSpec((1, T, DV), lambda h, i: (h, 0, 0))],
        out_specs=[pl.BlockSpec((tq, DV), lambda h, i: (i, h)),
                   pl.BlockSpec((1, tq, 1), lambda h, i: (h, i, 0))],
        out_shape=[_sds((T, H * DV), F32), _sds((H, T, 1), F32)],
        compiler_params=_params(("parallel", "parallel")),
    )(q, k, v)


def _attention_bwd(q, k, v, o, lse_row, dmix):
    H, T, D = q.shape
    DV = v.shape[-1]
    tk = min(ATT_TK, T)
    scale = D ** -0.5

    def body(q_ref, k_ref, v_ref, o_ref, lse_ref, do_ref, dq_ref, dk_ref, dv_ref, delta_ref):
        j = pl.program_id(1)
        do = do_ref[...]

        @pl.when(j == 0)
        def _():
            delta_ref[...] = lax.dot_general(jnp.ones((8, DV), F32), do * o_ref[...], (((1,), (1,)), ((), ())),
                                             precision=lax.Precision.HIGHEST, preferred_element_type=F32)

        q_all, k_j, v_j = q_ref[0], k_ref[0], v_ref[0]
        st = _dot_raw(k_j, q_all, "nt") * scale
        pt = jnp.exp(st - lse_ref[0])
        dv_ref[0] = _dot_raw(pt, do, "nn")
        dpt = _dot_raw(v_j, do, "nt")
        dst = pt * (dpt - delta_ref[0:1, :]) * scale
        dk_ref[0] = _dot_raw(dst, q_all, "nn")
        dq_part = _dot_raw(dst, k_j, "tn")

        @pl.when(j == 0)
        def _():
            dq_ref[0] = dq_part

        @pl.when(j > 0)
        def _():
            dq_ref[0] += dq_part

    return pl.pallas_call(
        body, name="attention_bwd", grid=(H, T // tk),
        in_specs=[pl.BlockSpec((1, T, D), lambda h, j: (h, 0, 0)),
                  pl.BlockSpec((1, tk, D), lambda h, j: (h, j, 0)),
                  pl.BlockSpec((1, tk, DV), lambda h, j: (h, j, 0)),
                  pl.BlockSpec((T, DV), lambda h, j: (0, h)),
                  pl.BlockSpec((1, 1, T), lambda h, j: (h, 0, 0)),
                  pl.BlockSpec((T, DV), lambda h, j: (0, h))],
        out_specs=[pl.BlockSpec((1, T, D), lambda h, j: (h, 0, 0)),
                   pl.BlockSpec((1, tk, D), lambda h, j: (h, j, 0)),
                   pl.BlockSpec((1, tk, DV), lambda h, j: (h, j, 0))],
        out_shape=[_sds((H, T, D), F32), _sds((H, T, D), F32), _sds((H, T, DV), F32)],
        scratch_shapes=[pltpu.VMEM((8, T), F32)],
        compiler_params=_params(("parallel", "arbitrary")),
    )(q, k, v, o, lse_row, dmix)


def _gla_block(hq, hf, hi, lower, st_in, *, rev, dot):
    rows, dk = hq.shape
    G, C = rows // CHUNK, CHUNK
    q = hq * _sigmoid(hq)
    f = lower + (1.0 - lower) * _sigmoid(hf)
    k = 1.0 - f
    logf = jnp.log(f)
    q3, k3, v3, lf3 = (t.reshape(G, C, dk) for t in (q, k, hi, logf))
    r = lax.broadcasted_iota(jnp.int32, (C, C), 0)
    c = lax.broadcasted_iota(jnp.int32, (C, C), 1)
    tri = ((r <= c) if rev else (r >= c)).astype(F32)
    b = lax.dot_general(jnp.broadcast_to(tri, (G, C, C)), lf3, (((2,), (1,)), ((0,), (0,))),
                        precision=lax.Precision.HIGHEST, preferred_element_type=F32)
    tpos = lax.broadcasted_iota(jnp.int32, (1, C, 1), 1)
    first_half = (tpos >= C // 2) if rev else (tpos <= C // 2 - 1)
    b_mid = jnp.sum(jnp.where(first_half, lf3, 0.0), axis=1, keepdims=True)
    b_last = jnp.sum(lf3, axis=1, keepdims=True)
    a = dot(q3 * jnp.exp(b - b_mid), k3 * jnp.exp(b_mid - b), "nt") * tri
    o_intra = dot(a, v3, "nn")
    kv_t = dot(v3, k3 * jnp.exp(b_last - b), "tn")
    decay = jnp.exp(b_last)
    qd = q3 * jnp.exp(b)
    st = st_in
    o_inter = [None] * G
    for g in (reversed(range(G)) if rev else range(G)):
        o_inter[g] = dot(qd[g], st, "nt")
        st = st * decay[g] + kv_t[g]
    o = o_intra.reshape(rows, dk) + jnp.concatenate(o_inter, axis=0)
    return o, st


def _gla_fwd(z, lower3, *, rev, col_q, col_f, col_v):
    T = z.shape[0]
    rows = min(GLA_GROUP * CHUNK, T)
    nb = T // rows
    blk = (lambda n: nb - 1 - n) if rev else (lambda n: n)

    def body(hq_ref, hf_ref, hi_ref, low_ref, o_ref, st_out_ref, st_ref):
        @pl.when(pl.program_id(1) == 0)
        def _():
            st_ref[...] = jnp.zeros_like(st_ref)

        st_in = st_ref[...]
        st_out_ref[0, 0] = st_in
        o, st = _gla_block(hq_ref[...], hf_ref[...], hi_ref[...], low_ref[0], st_in, rev=rev, dot=_dot_raw)
        o_ref[...] = o
        st_ref[...] = st

    def zspec(col):
        return pl.BlockSpec((rows, HG_DK), lambda h, n: (blk(n), col + h))

    return pl.pallas_call(
        body, name="gla_fwd_rev" if rev else "gla_fwd", grid=(HG_HEADS, nb),
        in_specs=[zspec(col_q), zspec(col_f), zspec(col_v), pl.BlockSpec((1, 1, HG_DK), lambda h, n: (h, 0, 0))],
        out_specs=[pl.BlockSpec((rows, HG_DK), lambda h, n: (blk(n), h)),
                   pl.BlockSpec((1, 1, HG_DK, HG_DK), lambda h, n: (h, blk(n), 0, 0))],
        out_shape=[_sds((T, HG_HEADS * HG_DK), F32), _sds((HG_HEADS, nb, HG_DK, HG_DK), F32)],
        scratch_shapes=[pltpu.VMEM((HG_DK, HG_DK), F32)],
        compiler_params=_params(("parallel", "arbitrary")),
    )(z, z, z, lower3)


def _gla_bwd(z, lower3, states, do, prev, *, rev, col_q, col_f, col_v):
    T = z.shape[0]
    rows = min(GLA_GROUP * CHUNK, T)
    nb = T // rows
    blk = (lambda n: n) if rev else (lambda n: nb - 1 - n)
    has_prev = prev is not None
    fn = functools.partial(_gla_block, rev=rev, dot=_bdot)

    def body(*refs):
        hq_ref, hf_ref, hi_ref, low_ref, st_ref, do_ref = refs[:6]
        rest = refs[6:]
        if has_prev:
            pq_ref, pi_ref = rest[:2]
            rest = rest[2:]
        dhq_ref, dhi_ref, dhf_ref, dlow_ref, dst_ref = rest
        n = pl.program_id(1)

        @pl.when(n == 0)
        def _():
            dst_ref[...] = jnp.zeros_like(dst_ref)

        _, vjp = jax.vjp(fn, hq_ref[...], hf_ref[...], hi_ref[...], low_ref[0], st_ref[0, 0])
        dhq, dhf, dhi, dlow, dst = vjp((do_ref[...], dst_ref[...]))
        dst_ref[...] = dst
        if has_prev:
            dhq = dhq + pq_ref[...]
            dhi = dhi + pi_ref[...]
        dhq_ref[...] = dhq.astype(dhq_ref.dtype)
        dhi_ref[...] = dhi.astype(dhi_ref.dtype)
        dhf_ref[...] = dhf.astype(dhf_ref.dtype)

        @pl.when(n == 0)
        def _():
            dlow_ref[0] = dlow

        @pl.when(n > 0)
        def _():
            dlow_ref[0] += dlow

    def zspec(col):
        return pl.BlockSpec((rows, HG_DK), lambda h, n: (blk(n), col + h))

    hspec = pl.BlockSpec((rows, HG_DK), lambda h, n: (blk(n), h))
    in_specs = [zspec(col_q), zspec(col_f), zspec(col_v), pl.BlockSpec((1, 1, HG_DK), lambda h, n: (h, 0, 0)),
                pl.BlockSpec((1, 1, HG_DK, HG_DK), lambda h, n: (h, blk(n), 0, 0)), hspec]
    ins = [z, z, z, lower3, states, do]
    if has_prev:
        in_specs += [hspec, hspec]
        ins += list(prev)
    wide = HG_HEADS * HG_DK
    acc_dtype = BF16 if has_prev else F32
    return pl.pallas_call(
        body, name="gla_bwd_rev" if rev else "gla_bwd", grid=(HG_HEADS, nb),
        in_specs=in_specs,
        out_specs=[hspec, hspec, hspec, pl.BlockSpec((1, 1, HG_DK), lambda h, n: (h, 0, 0))],
        out_shape=[_sds((T, wide), acc_dtype), _sds((T, wide), acc_dtype), _sds((T, wide), BF16),
                   _sds((HG_HEADS, 1, HG_DK), F32)],
        scratch_shapes=[pltpu.VMEM((HG_DK, HG_DK), F32)],
        compiler_params=_params(("parallel", "arbitrary")),
    )(*ins)


def _lower_fn(lb):
    e = jnp.exp(lb - jnp.max(lb, axis=0, keepdims=True))
    return (e / jnp.sum(e, axis=0, keepdims=True))[0]


def _lower_bounds(lb):
    def body(lb_ref, o_ref):
        o_ref[...] = _lower_fn(lb_ref[...])
    return pl.pallas_call(body, name="lower_bounds", out_shape=_sds(lb.shape[1:], F32))(lb)


def _lower_bounds_bwd(lb, dlower):
    def body(lb_ref, d_ref, o_ref):
        _, vjp = jax.vjp(_lower_fn, lb_ref[...])
        o_ref[...] = vjp(d_ref[...])[0]
    return pl.pallas_call(body, name="lower_bounds_bwd", out_shape=_sds(lb.shape, F32))(lb, dlower)


def _row_tile(r):
    for t in (512, 256, 128, 64, 32, 16, 8):
        if r % t == 0:
            return t
    return r


def _sum4(shards, recv, chip, *, name):
    _, R, C = shards.shape
    tr = _row_tile(R)

    def body(chip_ref, o_ref, r_ref, out_ref):
        out_ref[...] = ((o_ref[0].astype(F32) + r_ref[0].astype(F32)) + r_ref[1].astype(F32)) + r_ref[2].astype(F32)

    grid_spec = pltpu.PrefetchScalarGridSpec(
        num_scalar_prefetch=1, grid=(R // tr,),
        in_specs=[pl.BlockSpec((1, tr, C), lambda i, chip_ref: (chip_ref[0], i, 0)),
                  pl.BlockSpec((3, tr, C), lambda i, chip_ref: (0, i, 0))],
        out_specs=pl.BlockSpec((tr, C), lambda i, chip_ref: (i, 0)))
    return pl.pallas_call(
        body, name=name, grid_spec=grid_spec, out_shape=_sds((R, C), F32), compiler_params=_params(("parallel",)),
    )(chip, shards, recv)


def _adamw_math(w, g, m, v):
    m = ADAM_B1 * m + (1.0 - ADAM_B1) * g
    v = ADAM_B2 * v + (1.0 - ADAM_B2) * (g * g)
    m_hat = m / (1.0 - ADAM_B1 ** ADAM_STEP)
    v_hat = v / (1.0 - ADAM_B2 ** ADAM_STEP)
    delta = -ADAM_LR * (m_hat / (jnp.sqrt(v_hat) + ADAM_EPS) + ADAM_WD * w)
    return delta, m, v


def _adamw(w, g_a, g_b, m, v, *, name):
    R, C = w.shape
    tr = _row_tile(R)
    two = g_b is not None

    def body(*refs):
        w_ref, ga_ref = refs[0], refs[1]
        rest = refs[2:]
        g = ga_ref[...]
        if two:
            g = g + rest[0][...]
            rest = rest[1:]
        m_ref, v_ref, g_out, d_out, m_out, v_out = rest
        delta, m_new, v_new = _adamw_math(w_ref[...], g, m_ref[...], v_ref[...])
        g_out[...] = g
        d_out[...] = delta
        m_out[...] = m_new
        v_out[...] = v_new

    spec = pl.BlockSpec((tr, C), lambda i: (i, 0))
    ins = [w, g_a] + ([g_b] if two else []) + [m, v]
    return pl.pallas_call(
        body, name=name, grid=(R // tr,), in_specs=[spec] * len(ins), out_specs=[spec] * 4,
        out_shape=[_sds((R, C), F32)] * 4, compiler_params=_params(("parallel",)),
    )(*ins)


def _chip_peers():
    x, y, c = lax.axis_index("x"), lax.axis_index("y"), lax.axis_index("c")
    return (x, y, c), 2 * x + y, [(1 - x, y), (x, 1 - y), (1 - x, 1 - y)]


_HBM = pl.BlockSpec(memory_space=pltpu.HBM)
_SEM = pl.BlockSpec(memory_space=pltpu.SEMAPHORE)
_EFFECT = pltpu.SideEffectType.DATAFLOW_SIDE_EFFECTING


def _chip_copies(srcs, lands, sems, gather):
    (x, y, c), me, chips = _chip_peers()
    copies = []
    for t, (src, land) in enumerate(zip(srcs, lands)):
        for k, (px, py) in enumerate(chips):
            copies.append(pltpu.make_async_remote_copy(
                src if gather else src.at[2 * px + py], land.at[me] if gather else land.at[k],
                sems[0].at[3 * t + k], sems[1].at[3 * t + k], device_id=(px, py, c), device_id_type=MESH))
        if gather:
            copies.append(pltpu.make_async_copy(src, land.at[me], sems[2].at[t]))
    return copies


def _exchange_start(srcs, *, gather, name, after=None):
    n = len(srcs)
    n_sem = 3 if gather else 2
    n_in = 2 * n + (after is not None)
    lands = [_sds(((N_CHIPS,) + s.shape) if gather else ((3,) + s.shape[1:]), s.dtype) for s in srcs]

    def body(*refs):
        for cp in _chip_copies(refs[:n], refs[n:2 * n], refs[n_in:n_in + n_sem], gather):
            cp.start()
        token = refs[-1]
        token[...] = jnp.zeros_like(token)

    sem_shapes = [pltpu.SemaphoreType.DMA((3 * n,)), pltpu.SemaphoreType.DMA((3 * n,))]
    sem_shapes += [pltpu.SemaphoreType.DMA((n,))] if gather else []
    thru = [pltpu.HBM(s.shape, s.dtype) for s in srcs] + [pltpu.HBM(l.shape, l.dtype) for l in lands]
    res = pl.pallas_call(
        body, name=name, in_specs=[_HBM] * (2 * n) + [_ANY] * (after is not None),
        out_specs=[_SEM] * n_sem + [_HBM] * (2 * n) + [pl.BlockSpec(memory_space=pltpu.VMEM)],
        out_shape=sem_shapes + thru + [_sds((8, LANE), F32)], input_output_aliases={i: n_sem + i for i in range(2 * n)},
        compiler_params=pltpu.CompilerParams(has_side_effects=_EFFECT),
    )(*[pltpu.with_memory_space_constraint(s, pltpu.HBM) for s in srcs],
      *[pltpu.with_memory_space_constraint(lax.empty(l.shape, l.dtype), pltpu.HBM) for l in lands],
      *([after] if after is not None else []))
    return (res[:n_sem], res[n_sem:n_sem + n], res[n_sem + n:n_sem + 2 * n]), res[-1]


def _exchange_wait(started, after, *, name):
    sems, srcs, lands = started
    n, n_sem = len(srcs), len(sems)

    def body(*refs):
        for cp in _chip_copies(refs[:n], refs[n:2 * n], refs[2 * n:2 * n + n_sem], n_sem == 3):
            cp.wait()

    res = pl.pallas_call(
        body, name=name, in_specs=[_HBM] * (2 * n) + [_SEM] * n_sem + [_ANY], out_specs=[_HBM] * (2 * n),
        out_shape=[pltpu.HBM(a.shape, a.dtype) for a in list(srcs) + list(lands)],
        input_output_aliases={i: i for i in range(2 * n)},
        compiler_params=pltpu.CompilerParams(has_side_effects=_EFFECT),
    )(*srcs, *lands, *sems, after)
    return res[:n], res[n:]


def _swap_sibling(parts):
    n = len(parts)

    def body(*refs):
        ins, outs = refs[:n], refs[n:2 * n]
        send_sems, recv_sems = refs[2 * n:]
        x, y, c = lax.axis_index("x"), lax.axis_index("y"), lax.axis_index("c")
        copies = []
        for t in range(n):
            cp = pltpu.make_async_remote_copy(ins[t], outs[t], send_sems.at[t], recv_sems.at[t],
                                              device_id=(x, y, 1 - c), device_id_type=MESH)
            cp.start()
            copies.append(cp)
        for cp in copies:
            cp.wait()

    return pl.pallas_call(
        body, name="swap_sibling", in_specs=[_ANY] * n, out_specs=[_ANY] * n,
        out_shape=[_sds(p.shape, p.dtype) for p in parts],
        scratch_shapes=[pltpu.SemaphoreType.DMA((n,)), pltpu.SemaphoreType.DMA((n,))],
        compiler_params=_params(),
    )(*parts)


def _allreduce_small(pack):
    R, C = pack.shape

    def body(in_ref, out_ref, slots, send_sems, recv_sems):
        x, y, c = lax.axis_index("x"), lax.axis_index("y"), lax.axis_index("c")
        me = 4 * x + 2 * y + c
        slots[me] = in_ref[...]
        copies = []
        for k in range(1, N_DEV):
            peer = (x ^ ((k >> 2) & 1), y ^ ((k >> 1) & 1), c ^ (k & 1))
            cp = pltpu.make_async_remote_copy(in_ref, slots.at[me], send_sems.at[k - 1], recv_sems.at[k - 1],
                                              device_id=peer, device_id_type=MESH)
            cp.start()
            copies.append(cp)
        for cp in copies:
            cp.wait()
        acc = slots[0]
        for d in range(1, N_DEV):
            acc = acc + slots[d]
        out_ref[...] = acc

    return pl.pallas_call(
        body, name="allreduce_small", out_shape=_sds((R, C), F32),
        in_specs=[pl.BlockSpec(memory_space=pltpu.VMEM)], out_specs=pl.BlockSpec(memory_space=pltpu.VMEM),
        scratch_shapes=[pltpu.VMEM((N_DEV, R, C), F32), pltpu.SemaphoreType.DMA((N_DEV - 1,)),
                        pltpu.SemaphoreType.DMA((N_DEV - 1,))],
        compiler_params=_params(),
    )(pack)


_Z_CQ, _Z_CKV, _Z_HQ, _Z_HFF, _Z_HFB, _Z_HI, _Z_HG, _Z_KR, _Z_END = 0, 256, 512, 1024, 1536, 2048, 2560, 3072, 3200


def _to_z_layout(w):
    pad = jnp.zeros((w.shape[0], _Z_END - _Z_KR - QK_ROPE), w.dtype)
    return jnp.concatenate([w[:, :512], w[:, 512 + QK_ROPE:], w[:, 512:512 + QK_ROPE], pad], axis=1)


def _from_z_layout(w):
    return jnp.concatenate([w[:, :512], w[:, _Z_KR:_Z_KR + QK_ROPE], w[:, 512:_Z_KR]], axis=1)


def _col_shards_to_full(g):
    return jnp.transpose(g, (1, 0, 2)).reshape(g.shape[1], -1)


def _full_to_col_shards(w):
    r, c = w.shape
    return jnp.transpose(w.reshape(r, N_CHIPS, c // N_CHIPS), (1, 0, 2))


def _full_to_row_shards(w):
    r, c = w.shape
    return w.reshape(N_CHIPS, r // N_CHIPS, c)


def kernel(x, p, positions, g_mix, w_in, g_qa, g_kva, w_qb, w_kvb, g_qn, g_kn, lb_param, g_hgo, w_o, g_ffn, w_gate, w_up, w_down, g_ple, w_ple_gate, w_ple_proj, loss_target, m_g_mix, m_w_in, m_g_qa, m_g_kva, m_w_qb, m_w_kvb, m_g_qn, m_g_kn, m_lb_param, m_g_hgo, m_w_o, m_g_ffn, m_w_gate, m_w_up, m_w_down, m_g_ple, m_w_ple_gate, m_w_ple_proj, v_g_mix, v_w_in, v_g_qa, v_g_kva, v_w_qb, v_w_kvb, v_g_qn, v_g_kn, v_lb_param, v_g_hgo, v_w_o, v_g_ffn, v_w_gate, v_w_up, v_w_down, v_g_ple, v_w_ple_gate, v_w_ple_proj):
    w_named = dict(g_mix=g_mix, w_in=w_in, g_qa=g_qa, g_kva=g_kva, w_qb=w_qb, w_kvb=w_kvb, g_qn=g_qn, g_kn=g_kn,
                   lb_param=lb_param, g_hgo=g_hgo, w_o=w_o, g_ffn=g_ffn, w_gate=w_gate, w_up=w_up, w_down=w_down,
                   g_ple=g_ple, w_ple_gate=w_ple_gate, w_ple_proj=w_ple_proj)
    m_named = dict(g_mix=m_g_mix, w_in=m_w_in, g_qa=m_g_qa, g_kva=m_g_kva, w_qb=m_w_qb, w_kvb=m_w_kvb, g_qn=m_g_qn,
                   g_kn=m_g_kn, lb_param=m_lb_param, g_hgo=m_g_hgo, w_o=m_w_o, g_ffn=m_g_ffn, w_gate=m_w_gate,
                   w_up=m_w_up, w_down=m_w_down, g_ple=m_g_ple, w_ple_gate=m_w_ple_gate, w_ple_proj=m_w_ple_proj)
    v_named = dict(g_mix=v_g_mix, w_in=v_w_in, g_qa=v_g_qa, g_kva=v_g_kva, w_qb=v_w_qb, w_kvb=v_w_kvb, g_qn=v_g_qn,
                   g_kn=v_g_kn, lb_param=v_lb_param, g_hgo=v_g_hgo, w_o=v_w_o, g_ffn=v_g_ffn, w_gate=v_w_gate,
                   w_up=v_w_up, w_down=v_w_down, g_ple=v_g_ple, w_ple_gate=v_w_ple_gate, w_ple_proj=v_w_ple_proj)
    order = list(w_named)
    col_sharded = ("w_in", "w_qb", "w_kvb", "w_gate", "w_up", "w_ple_proj")
    row_sharded = ("w_o", "w_down", "w_ple_gate")
    big = col_sharded + row_sharded

    x2d, p2d, tgt = x[0], p[0, 0], loss_target[0]
    T, D = x2d.shape

    lb_flat = lb_param.reshape(-1, lb_param.shape[-1])
    gather_groups = (("w_in",), ("w_qb", "w_kvb"), ("w_o", "w_gate", "w_up"), ("w_down", "w_ple_gate", "w_ple_proj"))
    gather_started = []

    def gather_start(gi, after):
        srcs = [w_named[n][0].astype(BF16) for n in gather_groups[gi]] + ([lb_flat] if gi == 0 else [])
        started, token = _exchange_start(srcs, gather=True, name=f"gather_start_{gi}", after=after)
        gather_started.append(started)
        return token

    full = {}

    def gather_wait(gi, after):
        _, got = _exchange_wait(gather_started[gi], after, name=f"gather_wait_{gi}")
        for n, g in zip(gather_groups[gi], got):
            full[n] = _col_shards_to_full(g) if n in col_sharded else g.reshape(-1, g.shape[-1])
        return got

    inv_freq = ROPE_THETA ** (-jnp.arange(0, QK_ROPE, 2, dtype=F32) / QK_ROPE)
    ang = positions[0].astype(F32)[:, None] * inv_freq
    cos, sin = jnp.cos(ang), jnp.sin(ang)
    g_hgo_row = g_hgo.reshape(1, -1)

    token = gather_start(0, None)
    h1 = _stage(_rms, [x2d], [g_mix], [_sds((T, D), BF16)], [], name="norm_mix", after=token)[0]
    got = gather_wait(0, h1)
    token = got[0]
    for gi in range(1, len(gather_groups)):
        token = gather_start(gi, token)
    lb_full = _col_shards_to_full(got[-1]).reshape(lb_param.shape[0], lb_param.shape[1], -1)
    w_in_z = _to_z_layout(full["w_in"])
    z = _mm(h1, w_in_z, name="in_proj", after=token)
    qn, kvn = _stage(_mla_a_fn, [_cols(z, 256, 0), _cols(z, 256, 1)], [g_qa, g_kva],
                     [_sds((T, 256), BF16), _sds((T, 256), BF16)], [], name="mla_latent_norm")
    gather_wait(1, qn)
    q_raw = _mm(qn, full["w_qb"], name="q_up")
    kv_raw = _mm(kvn, full["w_kvb"], name="kv_up")
    kr = _cols(z, LANE, _Z_KR // LANE)
    q, k, v = _stage(_mla_b_fn, [q_raw, kv_raw, kr, cos, sin], [g_qn, g_kn],
                     [_sds((MLA_HEADS, T, QK_HEAD), BF16), _sds((MLA_HEADS, T, QK_HEAD), BF16),
                      _sds((MLA_HEADS, T, V_HEAD), BF16)], [], name="mla_qk_norm_rope")
    att, lse = _attention_fwd(q, k, v)

    lower = _lower_bounds(lb_full)
    lower3 = lower.reshape(2, HG_HEADS, 1, HG_DK)
    cq_blk, cf_blk, cb_blk, cv_blk = _Z_HQ // LANE, _Z_HFF // LANE, _Z_HFB // LANE, _Z_HI // LANE
    o_f, st_f = _gla_fwd(z, lower3[0], rev=False, col_q=cq_blk, col_f=cf_blk, col_v=cv_blk)
    o_b, st_b = _gla_fwd(z, lower3[1], rev=True, col_q=cq_blk, col_f=cb_blk, col_v=cv_blk)
    hg = _cols(z, 512, _Z_HG // 512)
    mix = _stage(_post_fn, [att, o_f, o_b, hg], [g_hgo_row], [_sds((T, att.shape[1] + o_f.shape[1]), BF16)], [],
                 name="mix_out")[0]
    gather_wait(2, mix)
    x1 = _mm(mix, full["w_o"], add=x2d, name="out_proj")
    h2 = _stage(_rms, [x1], [g_ffn], [_sds((T, D), BF16)], [], name="norm_ffn")[0]
    gt = _mm(h2, full["w_gate"], name="ffn_gate")
    up = _mm(h2, full["w_up"], name="ffn_up")
    act = _stage(_swiglu_fn, [gt, up], [], [_sds(gt.shape, BF16)], [], name="swiglu")[0]
    gather_wait(3, act)
    x2 = _mm(act, full["w_down"], add=x1, name="ffn_down")
    h3 = _stage(_rms, [x2], [g_ple], [_sds((T, D), BF16)], [], name="norm_ple")[0]
    pg = _mm(h3, full["w_ple_gate"], name="ple_gate")
    pp = _mm(p2d, full["w_ple_proj"], name="ple_proj")
    dx3, dpg, dpp, loss_part = _stage(_ple_loss_fn, [x2, pg, pp, tgt], [],
                                      [_sds((T, D), F32), _sds((T, D), BF16), _sds((T, D), BF16)],
                                      [_sds((1, 1), F32)], name="ple_loss")

    grads = {}
    scatter_groups = (("w_ple_proj", "w_ple_gate", "w_down"), ("w_gate", "w_up", "w_o"), ("w_qb", "w_kvb", "w_in"))
    scatter_started = []

    def scatter_start(gi):
        srcs = [_full_to_col_shards(grads[n]) if n in col_sharded else _full_to_row_shards(grads[n])
                for n in scatter_groups[gi]]
        started, token = _exchange_start(srcs, gather=False, name=f"scatter_start_{gi}")
        scatter_started.append(started)
        return token

    grads["w_ple_proj"] = _mm(p2d, dpp, ta=True, out_dtype=BF16, name="d_w_ple_proj")
    grads["w_ple_gate"] = _mm(h3, dpg, ta=True, out_dtype=BF16, name="d_w_ple_gate")
    dh3 = _mm(dpg, full["w_ple_gate"], tb=True, name="d_h3")
    dx2, grads["g_ple"] = _stage(_norm_bwd_fn, [x2, dh3, dx3], [g_ple], [_sds((T, D), F32)], [_sds((1, D), F32)],
                                 name="norm_ple_bwd")
    dact = _mm(dx2, full["w_down"], tb=True, name="d_act")
    grads["w_down"] = _mm(act, dx2, ta=True, out_dtype=BF16, name="d_w_down")
    token = scatter_start(0)
    dgt, dup = _stage(_swiglu_bwd_fn, [gt, up, dact], [], [_sds(gt.shape, BF16), _sds(gt.shape, BF16)], [],
                      name="swiglu_bwd", after=token)
    grads["w_gate"] = _mm(h2, dgt, ta=True, out_dtype=BF16, name="d_w_gate")
    grads["w_up"] = _mm(h2, dup, ta=True, out_dtype=BF16, name="d_w_up")
    dh2 = _mm(dgt, full["w_gate"], tb=True, name="d_h2_gate")
    dh2 = _mm(dup, full["w_up"], tb=True, add=dh2, name="d_h2_up")
    dx1, grads["g_ffn"] = _stage(_norm_bwd_fn, [x1, dh2, dx2], [g_ffn], [_sds((T, D), F32)], [_sds((1, D), F32)],
                                 name="norm_ffn_bwd")
    dmix = _mm(dx1, full["w_o"], tb=True, name="d_mix")
    grads["w_o"] = _mm(mix, dx1, ta=True, out_dtype=BF16, name="d_w_o")

    token = scatter_start(1)
    half = MLA_HEADS * V_HEAD
    do, dhg, dg_hgo = _stage(_post_bwd_fn, [o_f, o_b, hg, _cols(dmix, half, 1)], [g_hgo_row],
                             [_sds((T, half), F32), _sds((T, half), BF16)], [_sds((1, half), F32)], name="mix_out_bwd",
                             after=token)
    grads["g_hgo"] = dg_hgo
    dhq_f, dhi_f, dhf_f, dlow_f = _gla_bwd(z, lower3[0], st_f, do, None, rev=False,
                                           col_q=cq_blk, col_f=cf_blk, col_v=cv_blk)
    dhq, dhi, dhf_b, dlow_b = _gla_bwd(z, lower3[1], st_b, do, (dhq_f, dhi_f), rev=True,
                                       col_q=cq_blk, col_f=cb_blk, col_v=cv_blk)

    lse_row = lse.reshape(MLA_HEADS, 1, T)
    dq, dk, dv = _attention_bwd(q, k, v, att, lse_row, dmix)
    dq_raw, dkv_raw, dkr, grads["g_qn"], grads["g_kn"] = _stage(
        _mla_b_bwd_fn, [q_raw, kv_raw, kr, cos, sin, dq, dk, dv], [g_qn, g_kn],
        [_sds(q_raw.shape, BF16), _sds(kv_raw.shape, BF16), _sds((T, LANE), BF16)],
        [_sds(g_qn.shape, F32), _sds(g_kn.shape, F32)], name="mla_qk_norm_rope_bwd")
    grads["w_qb"] = _mm(qn, dq_raw, ta=True, out_dtype=BF16, name="d_w_qb")
    grads["w_kvb"] = _mm(kvn, dkv_raw, ta=True, out_dtype=BF16, name="d_w_kvb")
    dqn = _mm(dq_raw, full["w_qb"], tb=True, name="d_qn")
    dkvn = _mm(dkv_raw, full["w_kvb"], tb=True, name="d_kvn")
    dcq, dckv, grads["g_qa"], grads["g_kva"] = _stage(
        _mla_a_bwd_fn, [_cols(z, 256, 0), _cols(z, 256, 1), dqn, dkvn], [g_qa, g_kva],
        [_sds((T, 256), BF16), _sds((T, 256), BF16)], [_sds(g_qa.shape, F32), _sds(g_kva.shape, F32)],
        name="mla_latent_norm_bwd")
    dz = jnp.concatenate([dcq, dckv, dhq, dhf_f, dhf_b, dhi, dhg, dkr], axis=1)
    grads["w_in"] = _from_z_layout(_mm(h1, dz, ta=True, out_dtype=BF16, name="d_w_in"))
    token = scatter_start(2)
    dh1 = _mm(dz, w_in_z, tb=True, name="d_h1", after=token)
    grad_x, grads["g_mix"] = _stage(_norm_bwd_fn, [x2d, dh1, dx1], [g_mix], [_sds((T, D), F32)], [_sds((1, D), F32)],
                                    name="norm_mix_bwd")

    chip = 2 * lax.axis_index("x") + lax.axis_index("y")
    partial_sum = {}
    after = grad_x
    for gi, names in enumerate(scatter_groups):
        shards, recvs = _exchange_wait(scatter_started[gi], after, name=f"scatter_wait_{gi}")
        for n, s, r in zip(names, shards, recvs):
            partial_sum[n] = _sum4(s, r, chip.reshape(1), name="sum_" + n)
        after = partial_sum[names[-1]]
    partial_sums = [partial_sum[n] for n in big]
    sibling_sums = _swap_sibling(partial_sums)

    small = ("g_mix", "g_qa", "g_kva", "g_qn", "g_kn", "g_hgo", "g_ffn", "g_ple")
    small_all = small + ("lb_param",)
    width = -(-max(w_named[n].size for n in small_all) // LANE) * LANE

    def row(a):
        a = a.reshape(1, -1)
        return jnp.pad(a, ((0, 0), (0, width - a.shape[1])))

    dlower = jnp.concatenate([dlow_f.reshape(1, -1), dlow_b.reshape(1, -1)], axis=0)
    pack = jnp.concatenate([row(grads[n]) for n in small] + [row(dlower[0]), row(dlower[1]), row(loss_part)]
                           + [jnp.zeros((5, width), F32)], axis=0)
    red = _allreduce_small(pack)
    loss = red[10, 0]
    dlower_sum = red[8:10, :lb_full.shape[-1]]
    dlb_full = _lower_bounds_bwd(lb_full, dlower_sum)
    fshard = lb_param.shape[-1]
    dlb = lax.dynamic_slice_in_dim(dlb_full, chip * fshard, fshard, axis=2)

    out_g, out_d, out_m, out_v = {}, {}, {}, {}
    for n, mine, theirs in zip(big, partial_sums, sibling_sums):
        out_g[n], out_d[n], out_m[n], out_v[n] = (
            t[None] for t in _adamw(w_named[n][0], mine, theirs, m_named[n][0], v_named[n][0], name="adamw_" + n))
    g_rows = [red[i:i + 1] for i in range(len(small))] + [row(dlb)]
    g_pack = jnp.concatenate(g_rows + [jnp.zeros((16 - len(g_rows), width), F32)], axis=0)

    def packed(named):
        rows = [row(named[n]) for n in small_all]
        return jnp.concatenate(rows + [jnp.ones((16 - len(rows), width), F32)], axis=0)

    s_g, s_d, s_m, s_v = _adamw(packed(w_named), g_pack, None, packed(m_named), packed(v_named), name="adamw_small")
    for i, n in enumerate(small_all):
        size = w_named[n].size
        for src, dst in ((s_g, out_g), (s_d, out_d), (s_m, out_m), (s_v, out_v)):
            dst[n] = src[i, :size].reshape(w_named[n].shape)

    return (loss, grad_x[None], *[out_g[n] for n in order], *[out_d[n] for n in order],
            *[out_m[n] for n in order], *[out_v[n] for n in order])
```

```python
import functools

import jax
import jax.numpy as jnp
from jax import lax
from jax.experimental import pallas as pl
from jax.experimental.pallas import tpu as pltpu

F32 = jnp.float32
BF16 = jnp.bfloat16
MESH = pl.DeviceIdType.MESH

EPS = 1e-6
ROPE_THETA = 10000.0
MLA_HEADS = 4
QK_NOPE = 128
QK_ROPE = 64
QK_HEAD = QK_NOPE + QK_ROPE
V_HEAD = 128
HG_HEADS = 4
HG_DK = 128
CHUNK = 64
ADAM_LR = 0.001
ADAM_B1 = 0.9
ADAM_B2 = 0.999
ADAM_EPS = 1e-08
ADAM_WD = 0.01
ADAM_STEP = 10

LANE = 128
VMEM_LIMIT = 56 * 1024 * 1024
TOK_TILE = 256
GLA_GROUP = 8
ATT_TQ = 512
ATT_TK = 512
ATT_CHUNK = 512
LOG2_E = 1.4426950408889634
N_CHIPS = 4
N_DEV = 8


_ANY = pl.BlockSpec(memory_space=pl.ANY)


def _params(dims=None, **kw):
    return pltpu.CompilerParams(dimension_semantics=dims, vmem_limit_bytes=VMEM_LIMIT, **kw)


def _tile_candidates(n, cap):
    out = [d for d in range(LANE, min(n, cap) + 1, LANE) if n % d == 0]
    if n <= cap and n not in out:
        out.append(n)
    return out or [n]


MM_VMEM_BUDGET = 34 * 1024 * 1024
MM_MAX_ROWS = 1536
HBM_BYTES_PER_S = 2.8e12
MXU_FLOPS_PER_S = 8e14
STEP_OVERHEAD_S = 0.35e-6


def _mm_tiles(M, N, K, a_bytes, b_bytes, o_bytes, has_add):
    best = None
    for tm in _tile_candidates(M, MM_MAX_ROWS):
        for tn in _tile_candidates(N, N):
            for tk in _tile_candidates(K, K):
                ni, nj, nk = M // tm, N // tn, K // tk
                vmem = 2 * (tm * tk * a_bytes + tk * tn * b_bytes + tm * tn * o_bytes * (2 if has_add else 1))
                vmem += tm * tn * 4 * (2 if nk > 1 else 1)
                vmem += (tm * tk * 2 if a_bytes > 2 else 0) + (tk * tn * 2 if b_bytes > 2 else 0)
                if vmem > MM_VMEM_BUDGET:
                    continue
                moved = M * K * a_bytes * (nj if nk > 1 else 1) + K * N * b_bytes * (1 if nj == nk == 1 else ni)
                moved += M * N * o_bytes * (2 if has_add else 1)
                t = max(moved / HBM_BYTES_PER_S, 2 * M * N * K / MXU_FLOPS_PER_S) + ni * nj * nk * STEP_OVERHEAD_S
                if best is None or t < best[0]:
                    best = (t, tm, tn, tk)
    assert best is not None, (M, N, K)
    return best[1:]


def _dot_raw(a, b, kind):
    nb = a.ndim - 2
    batch = ((0,), (0,)) if nb else ((), ())
    ca = nb if kind == "tn" else nb + 1
    cb = nb + 1 if kind == "nt" else nb
    return lax.dot_general(a.astype(BF16), b.astype(BF16), (((ca,), (cb,)), batch), preferred_element_type=F32)


@functools.partial(jax.custom_vjp, nondiff_argnums=(2,))
def _bdot(a, b, kind):
    return _dot_raw(a, b, kind)


def _bdot_fwd(a, b, kind):
    return _dot_raw(a, b, kind), (a, b)


def _bdot_bwd(kind, res, g):
    a, b = res
    if kind == "nn":
        da, db = _bdot(g, b, "nt"), _bdot(a, g, "tn")
    elif kind == "nt":
        da, db = _bdot(g, b, "nn"), _bdot(g, a, "tn")
    else:
        da, db = _bdot(b, g, "nt"), _bdot(a, g, "nn")
    return da.astype(a.dtype), db.astype(b.dtype)


_bdot.defvjp(_bdot_fwd, _bdot_bwd)


def _mm(a, b, *, name, ta=False, tb=False, add=None, out_dtype=F32, after=None):
    K, M = a.shape if ta else a.shape[::-1]
    N, Kb = b.shape if tb else b.shape[::-1]
    assert K == Kb, (a.shape, b.shape, ta, tb)
    tm, tn, tk = _mm_tiles(M, N, K, a.dtype.itemsize, b.dtype.itemsize, jnp.dtype(out_dtype).itemsize, add is not None)
    nk = K // tk
    kind = "tn" if ta else ("nt" if tb else "nn")
    assert not (ta and tb)
    a_spec = pl.BlockSpec((tk, tm), lambda i, j, k: (k, i)) if ta else pl.BlockSpec((tm, tk), lambda i, j, k: (i, k))
    b_spec = pl.BlockSpec((tn, tk), lambda i, j, k: (j, k)) if tb else pl.BlockSpec((tk, tn), lambda i, j, k: (k, j))
    o_spec = pl.BlockSpec((tm, tn), lambda i, j, k: (i, j))
    has_add = add is not None

    def body(*refs):
        a_ref, b_ref = refs[0], refs[1]
        add_ref = refs[2] if has_add else None
        o_ref = refs[n_in]
        part = _dot_raw(a_ref[...], b_ref[...], kind)
        if nk == 1:
            if has_add:
                part = part + add_ref[...].astype(F32)
            o_ref[...] = part.astype(o_ref.dtype)
            return
        acc_ref = refs[-1]
        k = pl.program_id(2)

        @pl.when(k == 0)
        def _():
            acc_ref[...] = part

        @pl.when(k > 0)
        def _():
            acc_ref[...] += part

        @pl.when(k == nk - 1)
        def _():
            r = acc_ref[...]
            if has_add:
                r = r + add_ref[...].astype(F32)
            o_ref[...] = r.astype(o_ref.dtype)

    ins = [a, b] + ([add] if has_add else []) + ([after] if after is not None else [])
    in_specs = [a_spec, b_spec] + ([o_spec] if has_add else []) + ([_ANY] if after is not None else [])
    n_in = len(ins)
    return pl.pallas_call(
        body, name=name, grid=(M // tm, N // tn, nk), in_specs=in_specs, out_specs=o_spec,
        out_shape=jax.ShapeDtypeStruct((M, N), out_dtype),
        scratch_shapes=[pltpu.VMEM((tm, tn), F32)] if nk > 1 else [],
        compiler_params=_params(("parallel", "parallel", "arbitrary")),
    )(*ins)


def _cols(arr, width, block):
    return (arr, width, block)


def _stage(fn, tiles, params, out_tiles, out_sums, *, name, tile=TOK_TILE, after=None):
    def tok_spec(shape, width=None, block=0):
        if len(shape) == 2:
            w = shape[1] if width is None else width
            return pl.BlockSpec((tile, w), lambda i: (i, block))
        return pl.BlockSpec((shape[0], tile, shape[2]), lambda i: (0, i, 0))

    arrays, in_specs = [], []
    for t in tiles:
        if isinstance(t, tuple):
            arr, width, block = t
            arrays.append(arr)
            in_specs.append(tok_spec(arr.shape, width, block))
        else:
            arrays.append(t)
            in_specs.append(tok_spec(t.shape))
    n_tok = arrays[0].shape[0] if arrays[0].ndim == 2 else arrays[0].shape[1]
    for p in params:
        arrays.append(p)
        in_specs.append(pl.BlockSpec(p.shape, lambda i, nd=p.ndim: (0,) * nd))
    out_shape = list(out_tiles) + list(out_sums)
    out_specs = [tok_spec(o.shape) for o in out_tiles]
    out_specs += [pl.BlockSpec(o.shape, lambda i, nd=len(o.shape): (0,) * nd) for o in out_sums]
    n_fn, n_ot = len(arrays), len(out_tiles)
    if after is not None:
        arrays.append(after)
        in_specs.append(_ANY)
    n_in = len(arrays)

    def body(*refs):
        res = fn(*[r[...] for r in refs[:n_fn]])
        if not isinstance(res, (tuple, list)):
            res = (res,)
        outs = refs[n_in:]
        for o_ref, r in zip(outs[:n_ot], res[:n_ot]):
            o_ref[...] = r.astype(o_ref.dtype)
        i = pl.program_id(0)
        for o_ref, r in zip(outs[n_ot:], res[n_ot:]):
            @pl.when(i == 0)
            def _(o_ref=o_ref, r=r):
                o_ref[...] = r.astype(o_ref.dtype)

            @pl.when(i > 0)
            def _(o_ref=o_ref, r=r):
                o_ref[...] += r.astype(o_ref.dtype)

    res = pl.pallas_call(
        body, name=name, grid=(n_tok // tile,), in_specs=in_specs, out_specs=out_specs, out_shape=out_shape,
        compiler_params=_params(("arbitrary",)),
    )(*arrays)
    return res


def _sds(shape, dtype):
    return jax.ShapeDtypeStruct(tuple(shape), dtype)


def _sigmoid(x):
    return 1.0 / (1.0 + jnp.exp(-x))


def _rms(x, g):
    return x * lax.rsqrt(jnp.mean(x * x, axis=-1, keepdims=True) + EPS) * g


def _norm_bwd_fn(x, dh, dres, g):
    _, vjp = jax.vjp(_rms, x, g)
    dx, dg = vjp(dh)
    return dx + dres, dg


def _mla_a_fn(cq, ckv, g_qa, g_kva):
    return _rms(cq, g_qa), _rms(ckv, g_kva)


def _mla_a_bwd_fn(cq, ckv, dqn, dkvn, g_qa, g_kva):
    _, vjp = jax.vjp(_mla_a_fn, cq, ckv, g_qa, g_kva)
    return vjp((dqn, dkvn))


def _rope(t, cos, sin):
    t1, t2 = t[:, :QK_ROPE // 2], t[:, QK_ROPE // 2:]
    return jnp.concatenate([t1 * cos - t2 * sin, t1 * sin + t2 * cos], axis=-1)


def _mla_b_fn(q_raw, kv_raw, kr, cos, sin, g_qn, g_kn):
    krope = kr[:, :QK_ROPE]
    qs, ks, vs = [], [], []
    for h in range(MLA_HEADS):
        qh = _rms(q_raw[:, h * QK_HEAD:(h + 1) * QK_HEAD], g_qn)
        kvh = kv_raw[:, h * (QK_NOPE + V_HEAD):(h + 1) * (QK_NOPE + V_HEAD)]
        kh = _rms(jnp.concatenate([kvh[:, :QK_NOPE], krope], axis=-1), g_kn)
        qs.append(jnp.concatenate([qh[:, :QK_NOPE], _rope(qh[:, QK_NOPE:], cos, sin)], axis=-1))
        ks.append(jnp.concatenate([kh[:, :QK_NOPE], _rope(kh[:, QK_NOPE:], cos, sin)], axis=-1))
        vs.append(kvh[:, QK_NOPE:])
    return jnp.stack(qs), jnp.stack(ks), jnp.stack(vs)


def _mla_b_bwd_fn(q_raw, kv_raw, kr, cos, sin, dq, dk, dv, g_qn, g_kn):
    _, vjp = jax.vjp(lambda a, b, c, d, e: _mla_b_fn(a, b, c, cos, sin, d, e), q_raw, kv_raw, kr, g_qn, g_kn)
    return vjp((dq, dk, dv))


def _post_fn(a, o_f, o_b, hg, g_hgo):
    o = o_f + o_b
    parts = [a]
    for h in range(HG_HEADS):
        s = slice(h * HG_DK, (h + 1) * HG_DK)
        gate = hg[:, s]
        parts.append(_rms(o[:, s], g_hgo[:, s]) * (gate * _sigmoid(gate)))
    return jnp.concatenate(parts, axis=-1)


def _post_bwd_fn(o_f, o_b, hg, dr, g_hgo):
    def f(o, hg, g):
        return _post_fn(jnp.zeros_like(o), o, jnp.zeros_like(o), hg, g)[:, o.shape[1]:]
    _, vjp = jax.vjp(f, o_f + o_b, hg, g_hgo)
    return vjp(dr)


def _swiglu_fn(gt, up):
    return gt * _sigmoid(gt) * up


def _swiglu_bwd_fn(gt, up, dact):
    _, vjp = jax.vjp(_swiglu_fn, gt, up)
    return vjp(dact)


def _ple_loss_fn(x2, pg, pp, target):
    gate = _sigmoid(pg)
    err = x2 + gate * pp - target
    dx3 = err * (1.0 / err.shape[-1])
    loss = 0.5 * jnp.sum(jnp.mean(err * err, axis=-1, keepdims=True), axis=0, keepdims=True)
    return dx3, dx3 * pp * gate * (1.0 - gate), dx3 * gate, loss


def _attention_fwd(q, k, v):
    H, T, D = q.shape
    DV = v.shape[-1]
    tq, ck = min(ATT_TQ, T), min(ATT_CHUNK, T)
    c2 = (D ** -0.5) * LOG2_E

    def body(q_ref, k_ref, v_ref, o_ref, lse_ref):
        q_i = q_ref[0]

        def chunk(c, carry):
            m, l, acc = carry
            rows = pl.ds(pl.multiple_of(c * ck, ck), ck)
            s = _dot_raw(q_i, k_ref[0, rows, :], "nt")
            m_new = jnp.maximum(m, jnp.max(s, axis=-1, keepdims=True))
            p = jnp.exp2((s - m_new) * c2)
            alpha = jnp.exp2((m - m_new) * c2)
            l = l * alpha + jnp.sum(p, axis=-1, keepdims=True)
            acc = acc * alpha + _dot_raw(p, v_ref[0, rows, :], "nn")
            return m_new, l, acc

        init = (jnp.full((tq, 1), -jnp.inf, F32), jnp.zeros((tq, 1), F32), jnp.zeros((tq, DV), F32))
        m, l, acc = lax.fori_loop(0, T // ck, chunk, init)
        o_ref[...] = acc / l
        lse_ref[0] = m * c2 + jnp.log2(l)

    return pl.pallas_call(
        body, name="attention_fwd", grid=(H, T // tq),
        in_specs=[pl.BlockSpec((1, tq, D), lambda h, i: (h, i, 0)),
                  pl.BlockSpec((1, T, D), lambda h, i: (h, 0, 0)),
                  pl.BlockSpec((1, T, DV), lambda h, i: (h, 0, 0))],
        out_specs=[pl.BlockSpec((tq, DV), lambda h, i: (i, h)),
                   pl.BlockSpec((1, tq, 1), lambda h, i: (h, i, 0))],
        out_shape=[_sds((T, H * DV), F32), _sds((H, T, 1), F32)],
        compiler_params=_params(("parallel", "parallel")),
    )(q, k, v)


def _attention_bwd(q, k, v, o, lse2, dmix):
    H, T, D = q.shape
    DV = v.shape[-1]
    tk, cq = min(ATT_TK, T), min(ATT_CHUNK, T)
    scale = D ** -0.5
    c2 = scale * LOG2_E

    def body(q_ref, k_ref, v_ref, o_ref, lse_ref, do_ref, dq_ref, dk_ref, dv_ref, delta_ref):
        j = pl.program_id(1)

        @pl.when(j == 0)
        def _():
            delta = lax.dot_general(jnp.ones((8, DV), F32), do_ref[...] * o_ref[...], (((1,), (1,)), ((), ())),
                                    precision=lax.Precision.HIGHEST, preferred_element_type=F32)
            for i in range(T // cq):
                delta_ref[i] = delta[:, i * cq:(i + 1) * cq]

        k_j, v_j = k_ref[0], v_ref[0]
        dk_ref[0] = jnp.zeros((tk, D), F32)
        dv_ref[0] = jnp.zeros((tk, DV), F32)

        def chunk(c, carry):
            rows = pl.ds(pl.multiple_of(c * cq, cq), cq)
            q_c = q_ref[0, rows, :]
            do_c = do_ref[rows, :].astype(BF16)
            st = _dot_raw(k_j, q_c, "nt")
            pt = jnp.exp2(st * c2 - lse_ref[0, c])
            dv_ref[0] += _dot_raw(pt, do_c, "nn")
            dpt = _dot_raw(v_j, do_c, "nt")
            dst = pt * (dpt - delta_ref[c, 0:1, :]) * scale
            dk_ref[0] += _dot_raw(dst, q_c, "nn")
            dq_part = _dot_raw(dst, k_j, "tn")

            @pl.when(j == 0)
            def _():
                dq_ref[0, rows, :] = dq_part

            @pl.when(j > 0)
            def _():
                dq_ref[0, rows, :] += dq_part

            return carry

        lax.fori_loop(0, T // cq, chunk, 0)

    return pl.pallas_call(
        body, name="attention_bwd", grid=(H, T // tk),
        in_specs=[pl.BlockSpec((1, T, D), lambda h, j: (h, 0, 0)),
                  pl.BlockSpec((1, tk, D), lambda h, j: (h, j, 0)),
                  pl.BlockSpec((1, tk, DV), lambda h, j: (h, j, 0)),
                  pl.BlockSpec((T, DV), lambda h, j: (0, h)),
                  pl.BlockSpec((1, T // cq, 1, cq), lambda h, j: (h, 0, 0, 0)),
                  pl.BlockSpec((T, DV), lambda h, j: (0, h))],
        out_specs=[pl.BlockSpec((1, T, D), lambda h, j: (h, 0, 0)),
                   pl.BlockSpec((1, tk, D), lambda h, j: (h, j, 0)),
                   pl.BlockSpec((1, tk, DV), lambda h, j: (h, j, 0))],
        out_shape=[_sds((H, T, D), F32), _sds((H, T, D), F32), _sds((H, T, DV), F32)],
        scratch_shapes=[pltpu.VMEM((T // cq, 8, cq), F32)],
        compiler_params=_params(("parallel", "arbitrary")),
    )(q, k, v, o, lse2.reshape(H, T // cq, 1, cq), dmix)


def _gla_block(hq, hf, hi, lower, st_in, *, rev, dot):
    rows, dk = hq.shape
    G, C = rows // CHUNK, CHUNK
    q = hq * _sigmoid(hq)
    f = lower + (1.0 - lower) * _sigmoid(hf)
    k = 1.0 - f
    logf = jnp.log(f)
    q3, k3, v3, lf3 = (t.reshape(G, C, dk) for t in (q, k, hi, logf))
    r = lax.broadcasted_iota(jnp.int32, (C, C), 0)
    c = lax.broadcasted_iota(jnp.int32, (C, C), 1)
    tri = ((r <= c) if rev else (r >= c)).astype(F32)
    b = lax.dot_general(jnp.broadcast_to(tri, (G, C, C)), lf3, (((2,), (1,)), ((0,), (0,))),
                        precision=lax.Precision.HIGHEST, preferred_element_type=F32)
    tpos = lax.broadcasted_iota(jnp.int32, (1, C, 1), 1)
    first_half = (tpos >= C // 2) if rev else (tpos <= C // 2 - 1)
    b_mid = jnp.sum(jnp.where(first_half, lf3, 0.0), axis=1, keepdims=True)
    b_last = jnp.sum(lf3, axis=1, keepdims=True)
    a = dot(q3 * jnp.exp(b - b_mid), k3 * jnp.exp(b_mid - b), "nt") * tri
    o_intra = dot(a, v3, "nn")
    kv_t = dot(v3, k3 * jnp.exp(b_last - b), "tn")
    decay = jnp.exp(b_last)
    qd = q3 * jnp.exp(b)
    st = st_in
    o_inter = [None] * G
    for g in (reversed(range(G)) if rev else range(G)):
        o_inter[g] = dot(qd[g], st, "nt")
        st = st * decay[g] + kv_t[g]
    o = o_intra.reshape(rows, dk) + jnp.concatenate(o_inter, axis=0)
    return o, st


def _gla_fwd(z, lower3, *, rev, col_q, col_f, col_v):
    T = z.shape[0]
    rows = min(GLA_GROUP * CHUNK, T)
    nb = T // rows
    blk = (lambda n: nb - 1 - n) if rev else (lambda n: n)

    def body(hq_ref, hf_ref, hi_ref, low_ref, o_ref, st_out_ref, st_ref):
        @pl.when(pl.program_id(1) == 0)
        def _():
            st_ref[...] = jnp.zeros_like(st_ref)

        st_in = st_ref[...]
        st_out_ref[0, 0] = st_in
        o, st = _gla_block(hq_ref[...], hf_ref[...], hi_ref[...], low_ref[0], st_in, rev=rev, dot=_dot_raw)
        o_ref[...] = o
        st_ref[...] = st

    def zspec(col):
        return pl.BlockSpec((rows, HG_DK), lambda h, n: (blk(n), col + h))

    return pl.pallas_call(
        body, name="gla_fwd_rev" if rev else "gla_fwd", grid=(HG_HEADS, nb),
        in_specs=[zspec(col_q), zspec(col_f), zspec(col_v), pl.BlockSpec((1, 1, HG_DK), lambda h, n: (h, 0, 0))],
        out_specs=[pl.BlockSpec((rows, HG_DK), lambda h, n: (blk(n), h)),
                   pl.BlockSpec((1, 1, HG_DK, HG_DK), lambda h, n: (h, blk(n), 0, 0))],
        out_shape=[_sds((T, HG_HEADS * HG_DK), F32), _sds((HG_HEADS, nb, HG_DK, HG_DK), F32)],
        scratch_shapes=[pltpu.VMEM((HG_DK, HG_DK), F32)],
        compiler_params=_params(("parallel", "arbitrary")),
    )(z, z, z, lower3)


def _gla_bwd(z, lower3, states, do, prev, *, rev, col_q, col_f, col_v):
    T = z.shape[0]
    rows = min(GLA_GROUP * CHUNK, T)
    nb = T // rows
    blk = (lambda n: n) if rev else (lambda n: nb - 1 - n)
    has_prev = prev is not None
    fn = functools.partial(_gla_block, rev=rev, dot=_bdot)

    def body(*refs):
        hq_ref, hf_ref, hi_ref, low_ref, st_ref, do_ref = refs[:6]
        rest = refs[6:]
        if has_prev:
            pq_ref, pi_ref = rest[:2]
            rest = rest[2:]
        dhq_ref, dhi_ref, dhf_ref, dlow_ref, dst_ref = rest
        n = pl.program_id(1)

        @pl.when(n == 0)
        def _():
            dst_ref[...] = jnp.zeros_like(dst_ref)

        _, vjp = jax.vjp(fn, hq_ref[...], hf_ref[...], hi_ref[...], low_ref[0], st_ref[0, 0])
        dhq, dhf, dhi, dlow, dst = vjp((do_ref[...], dst_ref[...]))
        dst_ref[...] = dst
        if has_prev:
            dhq = dhq + pq_ref[...]
            dhi = dhi + pi_ref[...]
        dhq_ref[...] = dhq.astype(dhq_ref.dtype)
        dhi_ref[...] = dhi.astype(dhi_ref.dtype)
        dhf_ref[...] = dhf.astype(dhf_ref.dtype)

        @pl.when(n == 0)
        def _():
            dlow_ref[0] = dlow

        @pl.when(n > 0)
        def _():
            dlow_ref[0] += dlow

    def zspec(col):
        return pl.BlockSpec((rows, HG_DK), lambda h, n: (blk(n), col + h))

    hspec = pl.BlockSpec((rows, HG_DK), lambda h, n: (blk(n), h))
    in_specs = [zspec(col_q), zspec(col_f), zspec(col_v), pl.BlockSpec((1, 1, HG_DK), lambda h, n: (h, 0, 0)),
                pl.BlockSpec((1, 1, HG_DK, HG_DK), lambda h, n: (h, blk(n), 0, 0)), hspec]
    ins = [z, z, z, lower3, states, do]
    if has_prev:
        in_specs += [hspec, hspec]
        ins += list(prev)
    wide = HG_HEADS * HG_DK
    acc_dtype = BF16 if has_prev else F32
    return pl.pallas_call(
        body, name="gla_bwd_rev" if rev else "gla_bwd", grid=(HG_HEADS, nb),
        in_specs=in_specs,
        out_specs=[hspec, hspec, hspec, pl.BlockSpec((1, 1, HG_DK), lambda h, n: (h, 0, 0))],
        out_shape=[_sds((T, wide), acc_dtype), _sds((T, wide), acc_dtype), _sds((T, wide), BF16),
                   _sds((HG_HEADS, 1, HG_DK), F32)],
        scratch_shapes=[pltpu.VMEM((HG_DK, HG_DK), F32)],
        compiler_params=_params(("parallel", "arbitrary")),
    )(*ins)


def _lower_fn(lb):
    e = jnp.exp(lb - jnp.max(lb, axis=0, keepdims=True))
    return (e / jnp.sum(e, axis=0, keepdims=True))[0]


def _lower_bounds(lb):
    def body(lb_ref, o_ref):
        o_ref[...] = _lower_fn(lb_ref[...])
    return pl.pallas_call(body, name="lower_bounds", out_shape=_sds(lb.shape[1:], F32))(lb)


def _lower_bounds_bwd(lb, dlower):
    def body(lb_ref, d_ref, o_ref):
        _, vjp = jax.vjp(_lower_fn, lb_ref[...])
        o_ref[...] = vjp(d_ref[...])[0]
    return pl.pallas_call(body, name="lower_bounds_bwd", out_shape=_sds(lb.shape, F32))(lb, dlower)


def _row_tile(r):
    for t in (512, 256, 128, 64, 32, 16, 8):
        if r % t == 0:
            return t
    return r


def _sum4(shards, recv, chip, *, name):
    _, R, C = shards.shape
    tr = _row_tile(R)

    def body(chip_ref, o_ref, r_ref, out_ref):
        out_ref[...] = ((o_ref[0].astype(F32) + r_ref[0].astype(F32)) + r_ref[1].astype(F32)) + r_ref[2].astype(F32)

    grid_spec = pltpu.PrefetchScalarGridSpec(
        num_scalar_prefetch=1, grid=(R // tr,),
        in_specs=[pl.BlockSpec((1, tr, C), lambda i, chip_ref: (chip_ref[0], i, 0)),
                  pl.BlockSpec((3, tr, C), lambda i, chip_ref: (0, i, 0))],
        out_specs=pl.BlockSpec((tr, C), lambda i, chip_ref: (i, 0)))
    return pl.pallas_call(
        body, name=name, grid_spec=grid_spec, out_shape=_sds((R, C), F32), compiler_params=_params(("parallel",)),
    )(chip, shards, recv)


def _adamw_math(w, g, m, v):
    m = ADAM_B1 * m + (1.0 - ADAM_B1) * g
    v = ADAM_B2 * v + (1.0 - ADAM_B2) * (g * g)
    m_hat = m / (1.0 - ADAM_B1 ** ADAM_STEP)
    v_hat = v / (1.0 - ADAM_B2 ** ADAM_STEP)
    delta = -ADAM_LR * (m_hat / (jnp.sqrt(v_hat) + ADAM_EPS) + ADAM_WD * w)
    return delta, m, v


def _adamw(w, g_a, g_b, m, v, *, name):
    R, C = w.shape
    tr = _row_tile(R)
    two = g_b is not None

    def body(*refs):
        w_ref, ga_ref = refs[0], refs[1]
        rest = refs[2:]
        g = ga_ref[...]
        if two:
            g = g + rest[0][...]
            rest = rest[1:]
        m_ref, v_ref, g_out, d_out, m_out, v_out = rest
        delta, m_new, v_new = _adamw_math(w_ref[...], g, m_ref[...], v_ref[...])
        g_out[...] = g
        d_out[...] = delta
        m_out[...] = m_new
        v_out[...] = v_new

    spec = pl.BlockSpec((tr, C), lambda i: (i, 0))
    ins = [w, g_a] + ([g_b] if two else []) + [m, v]
    return pl.pallas_call(
        body, name=name, grid=(R // tr,), in_specs=[spec] * len(ins), out_specs=[spec] * 4,
        out_shape=[_sds((R, C), F32)] * 4, compiler_params=_params(("parallel",)),
    )(*ins)


def _chip_peers():
    x, y, c = lax.axis_index("x"), lax.axis_index("y"), lax.axis_index("c")
    return (x, y, c), 2 * x + y, [(1 - x, y), (x, 1 - y), (1 - x, 1 - y)]


_HBM = pl.BlockSpec(memory_space=pltpu.HBM)
_SEM = pl.BlockSpec(memory_space=pltpu.SEMAPHORE)
_EFFECT = pltpu.SideEffectType.DATAFLOW_SIDE_EFFECTING


def _chip_copies(srcs, lands, sems, gather):
    (x, y, c), me, chips = _chip_peers()
    copies = []
    for t, (src, land) in enumerate(zip(srcs, lands)):
        for k, (px, py) in enumerate(chips):
            copies.append(pltpu.make_async_remote_copy(
                src if gather else src.at[2 * px + py], land.at[me] if gather else land.at[k],
                sems[0].at[3 * t + k], sems[1].at[3 * t + k], device_id=(px, py, c), device_id_type=MESH))
        if gather:
            copies.append(pltpu.make_async_copy(src, land.at[me], sems[2].at[t]))
    return copies


def _exchange_start(srcs, *, gather, name, after=None):
    n = len(srcs)
    n_sem = 3 if gather else 2
    n_in = 2 * n + (after is not None)
    lands = [_sds(((N_CHIPS,) + s.shape) if gather else ((3,) + s.shape[1:]), s.dtype) for s in srcs]

    def body(*refs):
        for cp in _chip_copies(refs[:n], refs[n:2 * n], refs[n_in:n_in + n_sem], gather):
            cp.start()
        token = refs[-1]
        token[...] = jnp.zeros_like(token)

    sem_shapes = [pltpu.SemaphoreType.DMA((3 * n,)), pltpu.SemaphoreType.DMA((3 * n,))]
    sem_shapes += [pltpu.SemaphoreType.DMA((n,))] if gather else []
    thru = [pltpu.HBM(s.shape, s.dtype) for s in srcs] + [pltpu.HBM(l.shape, l.dtype) for l in lands]
    res = pl.pallas_call(
        body, name=name, in_specs=[_HBM] * (2 * n) + [_ANY] * (after is not None),
        out_specs=[_SEM] * n_sem + [_HBM] * (2 * n) + [pl.BlockSpec(memory_space=pltpu.VMEM)],
        out_shape=sem_shapes + thru + [_sds((8, LANE), F32)], input_output_aliases={i: n_sem + i for i in range(2 * n)},
        compiler_params=pltpu.CompilerParams(has_side_effects=_EFFECT),
    )(*[pltpu.with_memory_space_constraint(s, pltpu.HBM) for s in srcs],
      *[pltpu.with_memory_space_constraint(lax.empty(l.shape, l.dtype), pltpu.HBM) for l in lands],
      *([after] if after is not None else []))
    return (res[:n_sem], res[n_sem:n_sem + n], res[n_sem + n:n_sem + 2 * n]), res[-1]


def _exchange_wait(started, after, *, name):
    sems, srcs, lands = started
    n, n_sem = len(srcs), len(sems)

    def body(*refs):
        for cp in _chip_copies(refs[:n], refs[n:2 * n], refs[2 * n:2 * n + n_sem], n_sem == 3):
            cp.wait()

    res = pl.pallas_call(
        body, name=name, in_specs=[_HBM] * (2 * n) + [_SEM] * n_sem + [_ANY], out_specs=[_HBM] * (2 * n),
        out_shape=[pltpu.HBM(a.shape, a.dtype) for a in list(srcs) + list(lands)],
        input_output_aliases={i: i for i in range(2 * n)},
        compiler_params=pltpu.CompilerParams(has_side_effects=_EFFECT),
    )(*srcs, *lands, *sems, after)
    return res[:n], res[n:]


def _swap_sibling(parts):
    n = len(parts)

    def body(*refs):
        ins, outs = refs[:n], refs[n:2 * n]
        send_sems, recv_sems = refs[2 * n:]
        x, y, c = lax.axis_index("x"), lax.axis_index("y"), lax.axis_index("c")
        copies = []
        for t in range(n):
            cp = pltpu.make_async_remote_copy(ins[t], outs[t], send_sems.at[t], recv_sems.at[t],
                                              device_id=(x, y, 1 - c), device_id_type=MESH)
            cp.start()
            copies.append(cp)
        for cp in copies:
            cp.wait()

    return pl.pallas_call(
        body, name="swap_sibling", in_specs=[_ANY] * n, out_specs=[_ANY] * n,
        out_shape=[_sds(p.shape, p.dtype) for p in parts],
        scratch_shapes=[pltpu.SemaphoreType.DMA((n,)), pltpu.SemaphoreType.DMA((n,))],
        compiler_params=_params(),
    )(*parts)


def _allreduce_small(pack):
    R, C = pack.shape

    def body(in_ref, out_ref, slots, send_sems, recv_sems):
        x, y, c = lax.axis_index("x"), lax.axis_index("y"), lax.axis_index("c")
        me = 4 * x + 2 * y + c
        slots[me] = in_ref[...]
        copies = []
        for k in range(1, N_DEV):
            peer = (x ^ ((k >> 2) & 1), y ^ ((k >> 1) & 1), c ^ (k & 1))
            cp = pltpu.make_async_remote_copy(in_ref, slots.at[me], send_sems.at[k - 1], recv_sems.at[k - 1],
                                              device_id=peer, device_id_type=MESH)
            cp.start()
            copies.append(cp)
        for cp in copies:
            cp.wait()
        acc = slots[0]
        for d in range(1, N_DEV):
            acc = acc + slots[d]
        out_ref[...] = acc

    return pl.pallas_call(
        body, name="allreduce_small", out_shape=_sds((R, C), F32),
        in_specs=[pl.BlockSpec(memory_space=pltpu.VMEM)], out_specs=pl.BlockSpec(memory_space=pltpu.VMEM),
        scratch_shapes=[pltpu.VMEM((N_DEV, R, C), F32), pltpu.SemaphoreType.DMA((N_DEV - 1,)),
                        pltpu.SemaphoreType.DMA((N_DEV - 1,))],
        compiler_params=_params(),
    )(pack)


_Z_CQ, _Z_CKV, _Z_HQ, _Z_HFF, _Z_HFB, _Z_HI, _Z_HG, _Z_KR, _Z_END = 0, 256, 512, 1024, 1536, 2048, 2560, 3072, 3200


def _to_z_layout(w):
    pad = jnp.zeros((w.shape[0], _Z_END - _Z_KR - QK_ROPE), w.dtype)
    return jnp.concatenate([w[:, :512], w[:, 512 + QK_ROPE:], w[:, 512:512 + QK_ROPE], pad], axis=1)


def _from_z_layout(w):
    return jnp.concatenate([w[:, :512], w[:, _Z_KR:_Z_KR + QK_ROPE], w[:, 512:_Z_KR]], axis=1)


def _col_shards_to_full(g):
    return jnp.transpose(g, (1, 0, 2)).reshape(g.shape[1], -1)


def _full_to_col_shards(w):
    r, c = w.shape
    return jnp.transpose(w.reshape(r, N_CHIPS, c // N_CHIPS), (1, 0, 2))


def _full_to_row_shards(w):
    r, c = w.shape
    return w.reshape(N_CHIPS, r // N_CHIPS, c)


def kernel(x, p, positions, g_mix, w_in, g_qa, g_kva, w_qb, w_kvb, g_qn, g_kn, lb_param, g_hgo, w_o, g_ffn, w_gate, w_up, w_down, g_ple, w_ple_gate, w_ple_proj, loss_target, m_g_mix, m_w_in, m_g_qa, m_g_kva, m_w_qb, m_w_kvb, m_g_qn, m_g_kn, m_lb_param, m_g_hgo, m_w_o, m_g_ffn, m_w_gate, m_w_up, m_w_down, m_g_ple, m_w_ple_gate, m_w_ple_proj, v_g_mix, v_w_in, v_g_qa, v_g_kva, v_w_qb, v_w_kvb, v_g_qn, v_g_kn, v_lb_param, v_g_hgo, v_w_o, v_g_ffn, v_w_gate, v_w_up, v_w_down, v_g_ple, v_w_ple_gate, v_w_ple_proj):
    w_named = dict(g_mix=g_mix, w_in=w_in, g_qa=g_qa, g_kva=g_kva, w_qb=w_qb, w_kvb=w_kvb, g_qn=g_qn, g_kn=g_kn,
                   lb_param=lb_param, g_hgo=g_hgo, w_o=w_o, g_ffn=g_ffn, w_gate=w_gate, w_up=w_up, w_down=w_down,
                   g_ple=g_ple, w_ple_gate=w_ple_gate, w_ple_proj=w_ple_proj)
    m_named = dict(g_mix=m_g_mix, w_in=m_w_in, g_qa=m_g_qa, g_kva=m_g_kva, w_qb=m_w_qb, w_kvb=m_w_kvb, g_qn=m_g_qn,
                   g_kn=m_g_kn, lb_param=m_lb_param, g_hgo=m_g_hgo, w_o=m_w_o, g_ffn=m_g_ffn, w_gate=m_w_gate,
                   w_up=m_w_up, w_down=m_w_down, g_ple=m_g_ple, w_ple_gate=m_w_ple_gate, w_ple_proj=m_w_ple_proj)
    v_named = dict(g_mix=v_g_mix, w_in=v_w_in, g_qa=v_g_qa, g_kva=v_g_kva, w_qb=v_w_qb, w_kvb=v_w_kvb, g_qn=v_g_qn,
                   g_kn=v_g_kn, lb_param=v_lb_param, g_hgo=v_g_hgo, w_o=v_w_o, g_ffn=v_g_ffn, w_gate=v_w_gate,
                   w_up=v_w_up, w_down=v_w_down, g_ple=v_g_ple, w_ple_gate=v_w_ple_gate, w_ple_proj=v_w_ple_proj)
    order = list(w_named)
    col_sharded = ("w_in", "w_qb", "w_kvb", "w_gate", "w_up", "w_ple_proj")
    row_sharded = ("w_o", "w_down", "w_ple_gate")
    big = col_sharded + row_sharded

    x2d, p2d, tgt = x[0], p[0, 0], loss_target[0]
    T, D = x2d.shape

    lb_flat = lb_param.reshape(-1, lb_param.shape[-1])
    gather_groups = (("w_in",), ("w_qb", "w_kvb"), ("w_o", "w_gate", "w_up"), ("w_down", "w_ple_gate", "w_ple_proj"))
    gather_started = []

    def gather_start(gi, after):
        srcs = [w_named[n][0].astype(BF16) for n in gather_groups[gi]] + ([lb_flat] if gi == 0 else [])
        started, token = _exchange_start(srcs, gather=True, name=f"gather_start_{gi}", after=after)
        gather_started.append(started)
        return token

    full = {}

    def gather_wait(gi, after):
        _, got = _exchange_wait(gather_started[gi], after, name=f"gather_wait_{gi}")
        for n, g in zip(gather_groups[gi], got):
            full[n] = _col_shards_to_full(g) if n in col_sharded else g.reshape(-1, g.shape[-1])
        return got

    inv_freq = ROPE_THETA ** (-jnp.arange(0, QK_ROPE, 2, dtype=F32) / QK_ROPE)
    ang = positions[0].astype(F32)[:, None] * inv_freq
    cos, sin = jnp.cos(ang), jnp.sin(ang)
    g_hgo_row = g_hgo.reshape(1, -1)

    token = gather_start(0, None)
    h1 = _stage(_rms, [x2d], [g_mix], [_sds((T, D), BF16)], [], name="norm_mix", after=token)[0]
    got = gather_wait(0, h1)
    token = got[0]
    for gi in range(1, len(gather_groups)):
        token = gather_start(gi, token)
    lb_full = _col_shards_to_full(got[-1]).reshape(lb_param.shape[0], lb_param.shape[1], -1)
    w_in_z = _to_z_layout(full["w_in"])
    z = _mm(h1, w_in_z, name="in_proj", after=token)
    qn, kvn = _stage(_mla_a_fn, [_cols(z, 256, 0), _cols(z, 256, 1)], [g_qa, g_kva],
                     [_sds((T, 256), BF16), _sds((T, 256), BF16)], [], name="mla_latent_norm")
    gather_wait(1, qn)
    q_raw = _mm(qn, full["w_qb"], name="q_up")
    kv_raw = _mm(kvn, full["w_kvb"], name="kv_up")
    kr = _cols(z, LANE, _Z_KR // LANE)
    q, k, v = _stage(_mla_b_fn, [q_raw, kv_raw, kr, cos, sin], [g_qn, g_kn],
                     [_sds((MLA_HEADS, T, QK_HEAD), BF16), _sds((MLA_HEADS, T, QK_HEAD), BF16),
                      _sds((MLA_HEADS, T, V_HEAD), BF16)], [], name="mla_qk_norm_rope")
    att, lse = _attention_fwd(q, k, v)

    lower = _lower_bounds(lb_full)
    lower3 = lower.reshape(2, HG_HEADS, 1, HG_DK)
    cq_blk, cf_blk, cb_blk, cv_blk = _Z_HQ // LANE, _Z_HFF // LANE, _Z_HFB // LANE, _Z_HI // LANE
    o_f, st_f = _gla_fwd(z, lower3[0], rev=False, col_q=cq_blk, col_f=cf_blk, col_v=cv_blk)
    o_b, st_b = _gla_fwd(z, lower3[1], rev=True, col_q=cq_blk, col_f=cb_blk, col_v=cv_blk)
    hg = _cols(z, 512, _Z_HG // 512)
    mix = _stage(_post_fn, [att, o_f, o_b, hg], [g_hgo_row], [_sds((T, att.shape[1] + o_f.shape[1]), BF16)], [],
                 name="mix_out")[0]
    gather_wait(2, mix)
    x1 = _mm(mix, full["w_o"], add=x2d, name="out_proj")
    h2 = _stage(_rms, [x1], [g_ffn], [_sds((T, D), BF16)], [], name="norm_ffn")[0]
    gt = _mm(h2, full["w_gate"], name="ffn_gate")
    up = _mm(h2, full["w_up"], name="ffn_up")
    act = _stage(_swiglu_fn, [gt, up], [], [_sds(gt.shape, BF16)], [], name="swiglu")[0]
    gather_wait(3, act)
    x2 = _mm(act, full["w_down"], add=x1, name="ffn_down")
    h3 = _stage(_rms, [x2], [g_ple], [_sds((T, D), BF16)], [], name="norm_ple")[0]
    pg = _mm(h3, full["w_ple_gate"], name="ple_gate")
    pp = _mm(p2d, full["w_ple_proj"], name="ple_proj")
    dx3, dpg, dpp, loss_part = _stage(_ple_loss_fn, [x2, pg, pp, tgt], [],
                                      [_sds((T, D), F32), _sds((T, D), BF16), _sds((T, D), BF16)],
                                      [_sds((1, 1), F32)], name="ple_loss")

    grads = {}
    scatter_groups = (("w_ple_proj", "w_ple_gate", "w_down"), ("w_gate", "w_up", "w_o"), ("w_qb", "w_kvb", "w_in"))
    scatter_started = []

    def scatter_start(gi):
        srcs = [_full_to_col_shards(grads[n]) if n in col_sharded else _full_to_row_shards(grads[n])
                for n in scatter_groups[gi]]
        started, token = _exchange_start(srcs, gather=False, name=f"scatter_start_{gi}")
        scatter_started.append(started)
        return token

    grads["w_ple_proj"] = _mm(p2d, dpp, ta=True, out_dtype=BF16, name="d_w_ple_proj")
    grads["w_ple_gate"] = _mm(h3, dpg, ta=True, out_dtype=BF16, name="d_w_ple_gate")
    dh3 = _mm(dpg, full["w_ple_gate"], tb=True, name="d_h3")
    dx2, grads["g_ple"] = _stage(_norm_bwd_fn, [x2, dh3, dx3], [g_ple], [_sds((T, D), F32)], [_sds((1, D), F32)],
                                 name="norm_ple_bwd")
    dact = _mm(dx2, full["w_down"], tb=True, name="d_act")
    grads["w_down"] = _mm(act, dx2, ta=True, out_dtype=BF16, name="d_w_down")
    token = scatter_start(0)
    dgt, dup = _stage(_swiglu_bwd_fn, [gt, up, dact], [], [_sds(gt.shape, BF16), _sds(gt.shape, BF16)], [],
                      name="swiglu_bwd", after=token)
    grads["w_gate"] = _mm(h2, dgt, ta=True, out_dtype=BF16, name="d_w_gate")
    grads["w_up"] = _mm(h2, dup, ta=True, out_dtype=BF16, name="d_w_up")
    dh2 = _mm(dgt, full["w_gate"], tb=True, name="d_h2_gate")
    dh2 = _mm(dup, full["w_up"], tb=True, add=dh2, name="d_h2_up")
    dx1, grads["g_ffn"] = _stage(_norm_bwd_fn, [x1, dh2, dx2], [g_ffn], [_sds((T, D), F32)], [_sds((1, D), F32)],
                                 name="norm_ffn_bwd")
    dmix = _mm(dx1, full["w_o"], tb=True, name="d_mix")
    grads["w_o"] = _mm(mix, dx1, ta=True, out_dtype=BF16, name="d_w_o")

    token = scatter_start(1)
    half = MLA_HEADS * V_HEAD
    do, dhg, dg_hgo = _stage(_post_bwd_fn, [o_f, o_b, hg, _cols(dmix, half, 1)], [g_hgo_row],
                             [_sds((T, half), F32), _sds((T, half), BF16)], [_sds((1, half), F32)], name="mix_out_bwd",
                             after=token)
    grads["g_hgo"] = dg_hgo
    dhq_f, dhi_f, dhf_f, dlow_f = _gla_bwd(z, lower3[0], st_f, do, None, rev=False,
                                           col_q=cq_blk, col_f=cf_blk, col_v=cv_blk)
    dhq, dhi, dhf_b, dlow_b = _gla_bwd(z, lower3[1], st_b, do, (dhq_f, dhi_f), rev=True,
                                       col_q=cq_blk, col_f=cb_blk, col_v=cv_blk)

    dq, dk, dv = _attention_bwd(q, k, v, att, lse, dmix)
    dq_raw, dkv_raw, dkr, grads["g_qn"], grads["g_kn"] = _stage(
        _mla_b_bwd_fn, [q_raw, kv_raw, kr, cos, sin, dq, dk, dv], [g_qn, g_kn],
        [_sds(q_raw.shape, BF16), _sds(kv_raw.shape, BF16), _sds((T, LANE), BF16)],
        [_sds(g_qn.shape, F32), _sds(g_kn.shape, F32)], name="mla_qk_norm_rope_bwd")
    grads["w_qb"] = _mm(qn, dq_raw, ta=True, out_dtype=BF16, name="d_w_qb")
    grads["w_kvb"] = _mm(kvn, dkv_raw, ta=True, out_dtype=BF16, name="d_w_kvb")
    dqn = _mm(dq_raw, full["w_qb"], tb=True, name="d_qn")
    dkvn = _mm(dkv_raw, full["w_kvb"], tb=True, name="d_kvn")
    dcq, dckv, grads["g_qa"], grads["g_kva"] = _stage(
        _mla_a_bwd_fn, [_cols(z, 256, 0), _cols(z, 256, 1), dqn, dkvn], [g_qa, g_kva],
        [_sds((T, 256), BF16), _sds((T, 256), BF16)], [_sds(g_qa.shape, F32), _sds(g_kva.shape, F32)],
        name="mla_latent_norm_bwd")
    dz = jnp.concatenate([dcq, dckv, dhq, dhf_f, dhf_b, dhi, dhg, dkr], axis=1)
    grads["w_in"] = _from_z_layout(_mm(h1, dz, ta=True, out_dtype=BF16, name="d_w_in"))
    token = scatter_start(2)
    dh1 = _mm(dz, w_in_z, tb=True, name="d_h1", after=token)
    grad_x, grads["g_mix"] = _stage(_norm_bwd_fn, [x2d, dh1, dx1], [g_mix], [_sds((T, D), F32)], [_sds((1, D), F32)],
                                    name="norm_mix_bwd")

    chip = 2 * lax.axis_index("x") + lax.axis_index("y")
    partial_sum = {}
    after = grad_x
    for gi, names in enumerate(scatter_groups):
        shards, recvs = _exchange_wait(scatter_started[gi], after, name=f"scatter_wait_{gi}")
        for n, s, r in zip(names, shards, recvs):
            partial_sum[n] = _sum4(s, r, chip.reshape(1), name="sum_" + n)
        after = partial_sum[names[-1]]
    partial_sums = [partial_sum[n] for n in big]
    sibling_sums = _swap_sibling(partial_sums)

    small = ("g_mix", "g_qa", "g_kva", "g_qn", "g_kn", "g_hgo", "g_ffn", "g_ple")
    small_all = small + ("lb_param",)
    width = -(-max(w_named[n].size for n in small_all) // LANE) * LANE

    def row(a):
        a = a.reshape(1, -1)
        return jnp.pad(a, ((0, 0), (0, width - a.shape[1])))

    dlower = jnp.concatenate([dlow_f.reshape(1, -1), dlow_b.reshape(1, -1)], axis=0)
    pack = jnp.concatenate([row(grads[n]) for n in small] + [row(dlower[0]), row(dlower[1]), row(loss_part)]
                           + [jnp.zeros((5, width), F32)], axis=0)
    red = _allreduce_small(pack)
    loss = red[10, 0]
    dlower_sum = red[8:10, :lb_full.shape[-1]]
    dlb_full = _lower_bounds_bwd(lb_full, dlower_sum)
    fshard = lb_param.shape[-1]
    dlb = lax.dynamic_slice_in_dim(dlb_full, chip * fshard, fshard, axis=2)

    out_g, out_d, out_m, out_v = {}, {}, {}, {}
    for n, mine, theirs in zip(big, partial_sums, sibling_sums):
        out_g[n], out_d[n], out_m[n], out_v[n] = (
            t[None] for t in _adamw(w_named[n][0], mine, theirs, m_named[n][0], v_named[n][0], name="adamw_" + n))
    g_rows = [red[i:i + 1] for i in range(len(small))] + [row(dlb)]
    g_pack = jnp.concatenate(g_rows + [jnp.zeros((16 - len(g_rows), width), F32)], axis=0)

    def packed(named):
        rows = [row(named[n]) for n in small_all]
        return jnp.concatenate(rows + [jnp.ones((16 - len(rows), width), F32)], axis=0)

    s_g, s_d, s_m, s_v = _adamw(packed(w_named), g_pack, None, packed(m_named), packed(v_named), name="adamw_small")
    for i, n in enumerate(small_all):
        size = w_named[n].size
        for src, dst in ((s_g, out_g), (s_d, out_d), (s_m, out_m), (s_v, out_v)):
            dst[n] = src[i, :size].reshape(w_named[n].shape)

    return (loss, grad_x[None], *[out_g[n] for n in order], *[out_d[n] for n in order],
            *[out_m[n] for n in order], *[out_v[n] for n in order])
```

```python
import functools

import jax
import jax.numpy as jnp
from jax import lax
from jax.experimental import pallas as pl
from jax.experimental.pallas import tpu as pltpu

F32 = jnp.float32
BF16 = jnp.bfloat16
MESH = pl.DeviceIdType.MESH

EPS = 1e-6
ROPE_THETA = 10000.0
MLA_HEADS = 4
QK_NOPE = 128
QK_ROPE = 64
QK_HEAD = QK_NOPE + QK_ROPE
V_HEAD = 128
HG_HEADS = 4
HG_DK = 128
CHUNK = 64
ADAM_LR = 0.001
ADAM_B1 = 0.9
ADAM_B2 = 0.999
ADAM_EPS = 1e-08
ADAM_WD = 0.01
ADAM_STEP = 10

LANE = 128
VMEM_LIMIT = 56 * 1024 * 1024
TOK_TILE = 256
GLA_GROUP = 8
ATT_TQ = 1024
ATT_TK = 1024
ATT_CHUNK = 512
LOG2_E = 1.4426950408889634
N_CHIPS = 4
N_DEV = 8


_ANY = pl.BlockSpec(memory_space=pl.ANY)


def _params(dims=None, **kw):
    return pltpu.CompilerParams(dimension_semantics=dims, vmem_limit_bytes=VMEM_LIMIT, **kw)


def _tile_candidates(n, cap):
    out = [d for d in range(LANE, min(n, cap) + 1, LANE) if n % d == 0]
    if n <= cap and n not in out:
        out.append(n)
    return out or [n]


MM_VMEM_BUDGET = 34 * 1024 * 1024
MM_MAX_ROWS = 1536
HBM_BYTES_PER_S = 2.8e12
MXU_FLOPS_PER_S = 8e14
STEP_OVERHEAD_S = 0.35e-6


def _mm_tiles(M, N, K, a_bytes, b_bytes, o_bytes, has_add):
    best = None
    for tm in _tile_candidates(M, MM_MAX_ROWS):
        for tn in _tile_candidates(N, N):
            for tk in _tile_candidates(K, K):
                ni, nj, nk = M // tm, N // tn, K // tk
                vmem = 2 * (tm * tk * a_bytes + tk * tn * b_bytes + tm * tn * o_bytes * (2 if has_add else 1))
                vmem += tm * tn * 4 * (2 if nk > 1 else 1)
                vmem += (tm * tk * 2 if a_bytes > 2 else 0) + (tk * tn * 2 if b_bytes > 2 else 0)
                if vmem > MM_VMEM_BUDGET:
                    continue
                moved = M * K * a_bytes * (nj if nk > 1 else 1) + K * N * b_bytes * (1 if nj == nk == 1 else ni)
                moved += M * N * o_bytes * (2 if has_add else 1)
                t = max(moved / HBM_BYTES_PER_S, 2 * M * N * K / MXU_FLOPS_PER_S) + ni * nj * nk * STEP_OVERHEAD_S
                if best is None or t < best[0]:
                    best = (t, tm, tn, tk)
    assert best is not None, (M, N, K)
    return best[1:]


def _dot_raw(a, b, kind):
    nb = a.ndim - 2
    batch = ((0,), (0,)) if nb else ((), ())
    ca = nb if kind == "tn" else nb + 1
    cb = nb + 1 if kind == "nt" else nb
    return lax.dot_general(a.astype(BF16), b.astype(BF16), (((ca,), (cb,)), batch), preferred_element_type=F32)


@functools.partial(jax.custom_vjp, nondiff_argnums=(2,))
def _bdot(a, b, kind):
    return _dot_raw(a, b, kind)


def _bdot_fwd(a, b, kind):
    return _dot_raw(a, b, kind), (a, b)


def _bdot_bwd(kind, res, g):
    a, b = res
    if kind == "nn":
        da, db = _bdot(g, b, "nt"), _bdot(a, g, "tn")
    elif kind == "nt":
        da, db = _bdot(g, b, "nn"), _bdot(g, a, "tn")
    else:
        da, db = _bdot(b, g, "nt"), _bdot(a, g, "nn")
    return da.astype(a.dtype), db.astype(b.dtype)


_bdot.defvjp(_bdot_fwd, _bdot_bwd)


def _mm(a, b, *, name, ta=False, tb=False, add=None, out_dtype=F32, after=None):
    K, M = a.shape if ta else a.shape[::-1]
    N, Kb = b.shape if tb else b.shape[::-1]
    assert K == Kb, (a.shape, b.shape, ta, tb)
    tm, tn, tk = _mm_tiles(M, N, K, a.dtype.itemsize, b.dtype.itemsize, jnp.dtype(out_dtype).itemsize, add is not None)
    nk = K // tk
    kind = "tn" if ta else ("nt" if tb else "nn")
    assert not (ta and tb)
    a_spec = pl.BlockSpec((tk, tm), lambda i, j, k: (k, i)) if ta else pl.BlockSpec((tm, tk), lambda i, j, k: (i, k))
    b_spec = pl.BlockSpec((tn, tk), lambda i, j, k: (j, k)) if tb else pl.BlockSpec((tk, tn), lambda i, j, k: (k, j))
    o_spec = pl.BlockSpec((tm, tn), lambda i, j, k: (i, j))
    has_add = add is not None

    def body(*refs):
        a_ref, b_ref = refs[0], refs[1]
        add_ref = refs[2] if has_add else None
        o_ref = refs[n_in]
        part = _dot_raw(a_ref[...], b_ref[...], kind)
        if nk == 1:
            if has_add:
                part = part + add_ref[...].astype(F32)
            o_ref[...] = part.astype(o_ref.dtype)
            return
        acc_ref = refs[-1]
        k = pl.program_id(2)

        @pl.when(k == 0)
        def _():
            acc_ref[...] = part

        @pl.when(k > 0)
        def _():
            acc_ref[...] += part

        @pl.when(k == nk - 1)
        def _():
            r = acc_ref[...]
            if has_add:
                r = r + add_ref[...].astype(F32)
            o_ref[...] = r.astype(o_ref.dtype)

    ins = [a, b] + ([add] if has_add else []) + ([after] if after is not None else [])
    in_specs = [a_spec, b_spec] + ([o_spec] if has_add else []) + ([_ANY] if after is not None else [])
    n_in = len(ins)
    return pl.pallas_call(
        body, name=name, grid=(M // tm, N // tn, nk), in_specs=in_specs, out_specs=o_spec,
        out_shape=jax.ShapeDtypeStruct((M, N), out_dtype),
        scratch_shapes=[pltpu.VMEM((tm, tn), F32)] if nk > 1 else [],
        compiler_params=_params(("parallel", "parallel", "arbitrary")),
    )(*ins)


def _cols(arr, width, block):
    return (arr, width, block)


def _stage(fn, tiles, params, out_tiles, out_sums, *, name, tile=TOK_TILE, after=None):
    def tok_spec(shape, width=None, block=0):
        if len(shape) == 2:
            w = shape[1] if width is None else width
            return pl.BlockSpec((tile, w), lambda i: (i, block))
        return pl.BlockSpec((shape[0], tile, shape[2]), lambda i: (0, i, 0))

    arrays, in_specs = [], []
    for t in tiles:
        if isinstance(t, tuple):
            arr, width, block = t
            arrays.append(arr)
            in_specs.append(tok_spec(arr.shape, width, block))
        else:
            arrays.append(t)
            in_specs.append(tok_spec(t.shape))
    n_tok = arrays[0].shape[0] if arrays[0].ndim == 2 else arrays[0].shape[1]
    for p in params:
        arrays.append(p)
        in_specs.append(pl.BlockSpec(p.shape, lambda i, nd=p.ndim: (0,) * nd))
    out_shape = list(out_tiles) + list(out_sums)
    out_specs = [tok_spec(o.shape) for o in out_tiles]
    out_specs += [pl.BlockSpec(o.shape, lambda i, nd=len(o.shape): (0,) * nd) for o in out_sums]
    n_fn, n_ot = len(arrays), len(out_tiles)
    if after is not None:
        arrays.append(after)
        in_specs.append(_ANY)
    n_in = len(arrays)

    def body(*refs):
        res = fn(*[r[...] for r in refs[:n_fn]])
        if not isinstance(res, (tuple, list)):
            res = (res,)
        outs = refs[n_in:]
        for o_ref, r in zip(outs[:n_ot], res[:n_ot]):
            o_ref[...] = r.astype(o_ref.dtype)
        i = pl.program_id(0)
        for o_ref, r in zip(outs[n_ot:], res[n_ot:]):
            @pl.when(i == 0)
            def _(o_ref=o_ref, r=r):
                o_ref[...] = r.astype(o_ref.dtype)

            @pl.when(i > 0)
            def _(o_ref=o_ref, r=r):
                o_ref[...] += r.astype(o_ref.dtype)

    res = pl.pallas_call(
        body, name=name, grid=(n_tok // tile,), in_specs=in_specs, out_specs=out_specs, out_shape=out_shape,
        compiler_params=_params(("arbitrary",)),
    )(*arrays)
    return res


def _sds(shape, dtype):
    return jax.ShapeDtypeStruct(tuple(shape), dtype)


def _sigmoid(x):
    return 1.0 / (1.0 + jnp.exp(-x))


def _rms(x, g):
    return x * lax.rsqrt(jnp.mean(x * x, axis=-1, keepdims=True) + EPS) * g


def _norm_bwd_fn(x, dh, dres, g):
    _, vjp = jax.vjp(_rms, x, g)
    dx, dg = vjp(dh)
    return dx + dres, dg


def _mla_a_fn(cq, ckv, g_qa, g_kva):
    return _rms(cq, g_qa), _rms(ckv, g_kva)


def _mla_a_bwd_fn(cq, ckv, dqn, dkvn, g_qa, g_kva):
    _, vjp = jax.vjp(_mla_a_fn, cq, ckv, g_qa, g_kva)
    return vjp((dqn, dkvn))


def _rope(t, cos, sin):
    t1, t2 = t[:, :QK_ROPE // 2], t[:, QK_ROPE // 2:]
    return jnp.concatenate([t1 * cos - t2 * sin, t1 * sin + t2 * cos], axis=-1)


def _mla_b_fn(q_raw, kv_raw, kr, cos, sin, g_qn, g_kn):
    krope = kr[:, :QK_ROPE]
    qs, ks, vs = [], [], []
    for h in range(MLA_HEADS):
        qh = _rms(q_raw[:, h * QK_HEAD:(h + 1) * QK_HEAD], g_qn)
        kvh = kv_raw[:, h * (QK_NOPE + V_HEAD):(h + 1) * (QK_NOPE + V_HEAD)]
        kh = _rms(jnp.concatenate([kvh[:, :QK_NOPE], krope], axis=-1), g_kn)
        qs.append(jnp.concatenate([qh[:, :QK_NOPE], _rope(qh[:, QK_NOPE:], cos, sin)], axis=-1))
        ks.append(jnp.concatenate([kh[:, :QK_NOPE], _rope(kh[:, QK_NOPE:], cos, sin)], axis=-1))
        vs.append(kvh[:, QK_NOPE:])
    return jnp.stack(qs), jnp.stack(ks), jnp.stack(vs)


def _mla_b_bwd_fn(q_raw, kv_raw, kr, cos, sin, dq, dk, dv, g_qn, g_kn):
    _, vjp = jax.vjp(lambda a, b, c, d, e: _mla_b_fn(a, b, c, cos, sin, d, e), q_raw, kv_raw, kr, g_qn, g_kn)
    return vjp((dq, dk, dv))


def _post_fn(a, o_f, o_b, hg, g_hgo):
    o = o_f + o_b
    parts = [a]
    for h in range(HG_HEADS):
        s = slice(h * HG_DK, (h + 1) * HG_DK)
        gate = hg[:, s]
        parts.append(_rms(o[:, s], g_hgo[:, s]) * (gate * _sigmoid(gate)))
    return jnp.concatenate(parts, axis=-1)


def _post_bwd_fn(o_f, o_b, hg, dr, g_hgo):
    def f(o, hg, g):
        return _post_fn(jnp.zeros_like(o), o, jnp.zeros_like(o), hg, g)[:, o.shape[1]:]
    _, vjp = jax.vjp(f, o_f + o_b, hg, g_hgo)
    return vjp(dr)


def _swiglu_fn(gt, up):
    return gt * _sigmoid(gt) * up


def _swiglu_bwd_fn(gt, up, dact):
    _, vjp = jax.vjp(_swiglu_fn, gt, up)
    return vjp(dact)


def _ple_loss_fn(x2, pg, pp, target):
    gate = _sigmoid(pg)
    err = x2 + gate * pp - target
    dx3 = err * (1.0 / err.shape[-1])
    loss = 0.5 * jnp.sum(jnp.mean(err * err, axis=-1, keepdims=True), axis=0, keepdims=True)
    return dx3, dx3 * pp * gate * (1.0 - gate), dx3 * gate, loss


def _attention_fwd(q, k, v):
    H, T, D = q.shape
    DV = v.shape[-1]
    tq, ck = min(ATT_TQ, T), min(ATT_CHUNK, T)
    c2 = (D ** -0.5) * LOG2_E

    def body(q_ref, k_ref, v_ref, o_ref, lse_ref):
        q_i = q_ref[0]

        def chunk(c, carry):
            m, l, acc = carry
            rows = pl.ds(pl.multiple_of(c * ck, ck), ck)
            s = _dot_raw(q_i, k_ref[0, rows, :], "nt")
            m_new = jnp.maximum(m, jnp.max(s, axis=-1, keepdims=True))
            p = jnp.exp2((s - m_new) * c2)
            alpha = jnp.exp2((m - m_new) * c2)
            l = l * alpha + jnp.sum(p, axis=-1, keepdims=True)
            acc = acc * alpha + _dot_raw(p, v_ref[0, rows, :], "nn")
            return m_new, l, acc

        init = (jnp.full((tq, 1), -jnp.inf, F32), jnp.zeros((tq, 1), F32), jnp.zeros((tq, DV), F32))
        m, l, acc = lax.fori_loop(0, T // ck, chunk, init, unroll=True)
        o_ref[...] = acc / l
        lse_ref[0] = m * c2 + jnp.log2(l)

    return pl.pallas_call(
        body, name="attention_fwd", grid=(H, T // tq),
        in_specs=[pl.BlockSpec((1, tq, D), lambda h, i: (h, i, 0)),
                  pl.BlockSpec((1, T, D), lambda h, i: (h, 0, 0)),
                  pl.BlockSpec((1, T, DV), lambda h, i: (h, 0, 0))],
        out_specs=[pl.BlockSpec((tq, DV), lambda h, i: (i, h)),
                   pl.BlockSpec((1, tq, 1), lambda h, i: (h, i, 0))],
        out_shape=[_sds((T, H * DV), F32), _sds((H, T, 1), F32)],
        compiler_params=_params(("parallel", "parallel")),
    )(q, k, v)


def _attention_bwd(q, k, v, o, lse2, dmix):
    H, T, D = q.shape
    DV = v.shape[-1]
    tk, cq = min(ATT_TK, T), min(ATT_CHUNK, T)
    scale = D ** -0.5
    c2 = scale * LOG2_E

    def body(q_ref, k_ref, v_ref, o_ref, lse_ref, do_ref, dq_ref, dk_ref, dv_ref, delta_ref):
        j = pl.program_id(1)

        @pl.when(j == 0)
        def _():
            delta = lax.dot_general(jnp.ones((8, DV), F32), do_ref[...] * o_ref[...], (((1,), (1,)), ((), ())),
                                    precision=lax.Precision.HIGHEST, preferred_element_type=F32)
            for i in range(T // cq):
                delta_ref[i] = delta[:, i * cq:(i + 1) * cq]
            dq_ref[0] = jnp.zeros((T, D), F32)

        k_j, v_j = k_ref[0], v_ref[0]
        dk_ref[0] = jnp.zeros((tk, D), F32)
        dv_ref[0] = jnp.zeros((tk, DV), F32)

        def chunk(c, carry):
            rows = pl.ds(pl.multiple_of(c * cq, cq), cq)
            q_c = q_ref[0, rows, :]
            do_c = do_ref[rows, :].astype(BF16)
            st = _dot_raw(k_j, q_c, "nt")
            pt = jnp.exp2(st * c2 - lse_ref[0, c])
            dv_ref[0] += _dot_raw(pt, do_c, "nn")
            dpt = _dot_raw(v_j, do_c, "nt")
            dst = pt * (dpt - delta_ref[c, 0:1, :]) * scale
            dk_ref[0] += _dot_raw(dst, q_c, "nn")
            dq_ref[0, rows, :] += _dot_raw(dst, k_j, "tn")
            return carry

        lax.fori_loop(0, T // cq, chunk, 0, unroll=True)

    return pl.pallas_call(
        body, name="attention_bwd", grid=(H, T // tk),
        in_specs=[pl.BlockSpec((1, T, D), lambda h, j: (h, 0, 0)),
                  pl.BlockSpec((1, tk, D), lambda h, j: (h, j, 0)),
                  pl.BlockSpec((1, tk, DV), lambda h, j: (h, j, 0)),
                  pl.BlockSpec((T, DV), lambda h, j: (0, h)),
                  pl.BlockSpec((1, T // cq, 1, cq), lambda h, j: (h, 0, 0, 0)),
                  pl.BlockSpec((T, DV), lambda h, j: (0, h))],
        out_specs=[pl.BlockSpec((1, T, D), lambda h, j: (h, 0, 0)),
                   pl.BlockSpec((1, tk, D), lambda h, j: (h, j, 0)),
                   pl.BlockSpec((1, tk, DV), lambda h, j: (h, j, 0))],
        out_shape=[_sds((H, T, D), F32), _sds((H, T, D), F32), _sds((H, T, DV), F32)],
        scratch_shapes=[pltpu.VMEM((T // cq, 8, cq), F32)],
        compiler_params=_params(("parallel", "arbitrary")),
    )(q, k, v, o, lse2.reshape(H, T // cq, 1, cq), dmix)


def _gla_block(hq, hf, hi, lower, st_in, *, rev, dot):
    rows, dk = hq.shape
    G, C = rows // CHUNK, CHUNK
    q = hq * _sigmoid(hq)
    f = lower + (1.0 - lower) * _sigmoid(hf)
    k = 1.0 - f
    logf = jnp.log(f)
    q3, k3, v3, lf3 = (t.reshape(G, C, dk) for t in (q, k, hi, logf))
    r = lax.broadcasted_iota(jnp.int32, (C, C), 0)
    c = lax.broadcasted_iota(jnp.int32, (C, C), 1)
    tri = ((r <= c) if rev else (r >= c)).astype(F32)
    b = lax.dot_general(jnp.broadcast_to(tri, (G, C, C)), lf3, (((2,), (1,)), ((0,), (0,))),
                        precision=lax.Precision.HIGHEST, preferred_element_type=F32)
    tpos = lax.broadcasted_iota(jnp.int32, (1, C, 1), 1)
    first_half = (tpos >= C // 2) if rev else (tpos <= C // 2 - 1)
    b_mid = jnp.sum(jnp.where(first_half, lf3, 0.0), axis=1, keepdims=True)
    b_last = jnp.sum(lf3, axis=1, keepdims=True)
    a = dot(q3 * jnp.exp(b - b_mid), k3 * jnp.exp(b_mid - b), "nt") * tri
    o_intra = dot(a, v3, "nn")
    kv_t = dot(v3, k3 * jnp.exp(b_last - b), "tn")
    decay = jnp.exp(b_last)
    qd = q3 * jnp.exp(b)
    st = st_in
    o_inter = [None] * G
    for g in (reversed(range(G)) if rev else range(G)):
        o_inter[g] = dot(qd[g], st, "nt")
        st = st * decay[g] + kv_t[g]
    o = o_intra.reshape(rows, dk) + jnp.concatenate(o_inter, axis=0)
    return o, st


def _gla_fwd(z, lower3, *, rev, col_q, col_f, col_v):
    T = z.shape[0]
    rows = min(GLA_GROUP * CHUNK, T)
    nb = T // rows
    blk = (lambda n: nb - 1 - n) if rev else (lambda n: n)

    def body(hq_ref, hf_ref, hi_ref, low_ref, o_ref, st_out_ref, st_ref):
        @pl.when(pl.program_id(1) == 0)
        def _():
            st_ref[...] = jnp.zeros_like(st_ref)

        st_in = st_ref[...]
        st_out_ref[0, 0] = st_in
        o, st = _gla_block(hq_ref[...], hf_ref[...], hi_ref[...], low_ref[0], st_in, rev=rev, dot=_dot_raw)
        o_ref[...] = o
        st_ref[...] = st

    def zspec(col):
        return pl.BlockSpec((rows, HG_DK), lambda h, n: (blk(n), col + h))

    return pl.pallas_call(
        body, name="gla_fwd_rev" if rev else "gla_fwd", grid=(HG_HEADS, nb),
        in_specs=[zspec(col_q), zspec(col_f), zspec(col_v), pl.BlockSpec((1, 1, HG_DK), lambda h, n: (h, 0, 0))],
        out_specs=[pl.BlockSpec((rows, HG_DK), lambda h, n: (blk(n), h)),
                   pl.BlockSpec((1, 1, HG_DK, HG_DK), lambda h, n: (h, blk(n), 0, 0))],
        out_shape=[_sds((T, HG_HEADS * HG_DK), F32), _sds((HG_HEADS, nb, HG_DK, HG_DK), F32)],
        scratch_shapes=[pltpu.VMEM((HG_DK, HG_DK), F32)],
        compiler_params=_params(("parallel", "arbitrary")),
    )(z, z, z, lower3)


def _gla_bwd(z, lower3, states, do, prev, *, rev, col_q, col_f, col_v):
    T = z.shape[0]
    rows = min(GLA_GROUP * CHUNK, T)
    nb = T // rows
    blk = (lambda n: n) if rev else (lambda n: nb - 1 - n)
    has_prev = prev is not None
    fn = functools.partial(_gla_block, rev=rev, dot=_bdot)

    def body(*refs):
        hq_ref, hf_ref, hi_ref, low_ref, st_ref, do_ref = refs[:6]
        rest = refs[6:]
        if has_prev:
            pq_ref, pi_ref = rest[:2]
            rest = rest[2:]
        dhq_ref, dhi_ref, dhf_ref, dlow_ref, dst_ref = rest
        n = pl.program_id(1)

        @pl.when(n == 0)
        def _():
            dst_ref[...] = jnp.zeros_like(dst_ref)

        _, vjp = jax.vjp(fn, hq_ref[...], hf_ref[...], hi_ref[...], low_ref[0], st_ref[0, 0])
        dhq, dhf, dhi, dlow, dst = vjp((do_ref[...], dst_ref[...]))
        dst_ref[...] = dst
        if has_prev:
            dhq = dhq + pq_ref[...]
            dhi = dhi + pi_ref[...]
        dhq_ref[...] = dhq.astype(dhq_ref.dtype)
        dhi_ref[...] = dhi.astype(dhi_ref.dtype)
        dhf_ref[...] = dhf.astype(dhf_ref.dtype)

        @pl.when(n == 0)
        def _():
            dlow_ref[0] = dlow

        @pl.when(n > 0)
        def _():
            dlow_ref[0] += dlow

    def zspec(col):
        return pl.BlockSpec((rows, HG_DK), lambda h, n: (blk(n), col + h))

    hspec = pl.BlockSpec((rows, HG_DK), lambda h, n: (blk(n), h))
    in_specs = [zspec(col_q), zspec(col_f), zspec(col_v), pl.BlockSpec((1, 1, HG_DK), lambda h, n: (h, 0, 0)),
                pl.BlockSpec((1, 1, HG_DK, HG_DK), lambda h, n: (h, blk(n), 0, 0)), hspec]
    ins = [z, z, z, lower3, states, do]
    if has_prev:
        in_specs += [hspec, hspec]
        ins += list(prev)
    wide = HG_HEADS * HG_DK
    acc_dtype = BF16 if has_prev else F32
    return pl.pallas_call(
        body, name="gla_bwd_rev" if rev else "gla_bwd", grid=(HG_HEADS, nb),
        in_specs=in_specs,
        out_specs=[hspec, hspec, hspec, pl.BlockSpec((1, 1, HG_DK), lambda h, n: (h, 0, 0))],
        out_shape=[_sds((T, wide), acc_dtype), _sds((T, wide), acc_dtype), _sds((T, wide), BF16),
                   _sds((HG_HEADS, 1, HG_DK), F32)],
        scratch_shapes=[pltpu.VMEM((HG_DK, HG_DK), F32)],
        compiler_params=_params(("parallel", "arbitrary")),
    )(*ins)


def _lower_fn(lb):
    e = jnp.exp(lb - jnp.max(lb, axis=0, keepdims=True))
    return (e / jnp.sum(e, axis=0, keepdims=True))[0]


def _lower_bounds(lb):
    def body(lb_ref, o_ref):
        o_ref[...] = _lower_fn(lb_ref[...])
    return pl.pallas_call(body, name="lower_bounds", out_shape=_sds(lb.shape[1:], F32))(lb)


def _lower_bounds_bwd(lb, dlower):
    def body(lb_ref, d_ref, o_ref):
        _, vjp = jax.vjp(_lower_fn, lb_ref[...])
        o_ref[...] = vjp(d_ref[...])[0]
    return pl.pallas_call(body, name="lower_bounds_bwd", out_shape=_sds(lb.shape, F32))(lb, dlower)


def _row_tile(r):
    for t in (512, 256, 128, 64, 32, 16, 8):
        if r % t == 0:
            return t
    return r


def _sum4(shards, recv, chip, *, name):
    _, R, C = shards.shape
    tr = _row_tile(R)

    def body(chip_ref, o_ref, r_ref, out_ref):
        out_ref[...] = ((o_ref[0].astype(F32) + r_ref[0].astype(F32)) + r_ref[1].astype(F32)) + r_ref[2].astype(F32)

    grid_spec = pltpu.PrefetchScalarGridSpec(
        num_scalar_prefetch=1, grid=(R // tr,),
        in_specs=[pl.BlockSpec((1, tr, C), lambda i, chip_ref: (chip_ref[0], i, 0)),
                  pl.BlockSpec((3, tr, C), lambda i, chip_ref: (0, i, 0))],
        out_specs=pl.BlockSpec((tr, C), lambda i, chip_ref: (i, 0)))
    return pl.pallas_call(
        body, name=name, grid_spec=grid_spec, out_shape=_sds((R, C), F32), compiler_params=_params(("parallel",)),
    )(chip, shards, recv)


def _adamw_math(w, g, m, v):
    m = ADAM_B1 * m + (1.0 - ADAM_B1) * g
    v = ADAM_B2 * v + (1.0 - ADAM_B2) * (g * g)
    m_hat = m / (1.0 - ADAM_B1 ** ADAM_STEP)
    v_hat = v / (1.0 - ADAM_B2 ** ADAM_STEP)
    delta = -ADAM_LR * (m_hat / (jnp.sqrt(v_hat) + ADAM_EPS) + ADAM_WD * w)
    return delta, m, v


def _adamw(w, g_a, g_b, m, v, *, name):
    R, C = w.shape
    tr = _row_tile(R)
    two = g_b is not None

    def body(*refs):
        w_ref, ga_ref = refs[0], refs[1]
        rest = refs[2:]
        g = ga_ref[...]
        if two:
            g = g + rest[0][...]
            rest = rest[1:]
        m_ref, v_ref, g_out, d_out, m_out, v_out = rest
        delta, m_new, v_new = _adamw_math(w_ref[...], g, m_ref[...], v_ref[...])
        g_out[...] = g
        d_out[...] = delta
        m_out[...] = m_new
        v_out[...] = v_new

    spec = pl.BlockSpec((tr, C), lambda i: (i, 0))
    ins = [w, g_a] + ([g_b] if two else []) + [m, v]
    return pl.pallas_call(
        body, name=name, grid=(R // tr,), in_specs=[spec] * len(ins), out_specs=[spec] * 4,
        out_shape=[_sds((R, C), F32)] * 4, compiler_params=_params(("parallel",)),
    )(*ins)


def _chip_peers():
    x, y, c = lax.axis_index("x"), lax.axis_index("y"), lax.axis_index("c")
    return (x, y, c), 2 * x + y, [(1 - x, y), (x, 1 - y), (1 - x, 1 - y)]


_HBM = pl.BlockSpec(memory_space=pltpu.HBM)
_SEM = pl.BlockSpec(memory_space=pltpu.SEMAPHORE)
_EFFECT = pltpu.SideEffectType.DATAFLOW_SIDE_EFFECTING


def _chip_copies(srcs, lands, sems, gather):
    (x, y, c), me, chips = _chip_peers()
    copies = []
    for t, (src, land) in enumerate(zip(srcs, lands)):
        for k, (px, py) in enumerate(chips):
            copies.append(pltpu.make_async_remote_copy(
                src if gather else src.at[2 * px + py], land.at[me] if gather else land.at[k],
                sems[0].at[3 * t + k], sems[1].at[3 * t + k], device_id=(px, py, c), device_id_type=MESH))
        if gather:
            copies.append(pltpu.make_async_copy(src, land.at[me], sems[2].at[t]))
    return copies


def _exchange_start(srcs, *, gather, name, after=None):
    n = len(srcs)
    n_sem = 3 if gather else 2
    n_in = 2 * n + (after is not None)
    lands = [_sds(((N_CHIPS,) + s.shape) if gather else ((3,) + s.shape[1:]), s.dtype) for s in srcs]

    def body(*refs):
        for cp in _chip_copies(refs[:n], refs[n:2 * n], refs[n_in:n_in + n_sem], gather):
            cp.start()
        token = refs[-1]
        token[...] = jnp.zeros_like(token)

    sem_shapes = [pltpu.SemaphoreType.DMA((3 * n,)), pltpu.SemaphoreType.DMA((3 * n,))]
    sem_shapes += [pltpu.SemaphoreType.DMA((n,))] if gather else []
    thru = [pltpu.HBM(s.shape, s.dtype) for s in srcs] + [pltpu.HBM(l.shape, l.dtype) for l in lands]
    res = pl.pallas_call(
        body, name=name, in_specs=[_HBM] * (2 * n) + [_ANY] * (after is not None),
        out_specs=[_SEM] * n_sem + [_HBM] * (2 * n) + [pl.BlockSpec(memory_space=pltpu.VMEM)],
        out_shape=sem_shapes + thru + [_sds((8, LANE), F32)], input_output_aliases={i: n_sem + i for i in range(2 * n)},
        compiler_params=pltpu.CompilerParams(has_side_effects=_EFFECT),
    )(*[pltpu.with_memory_space_constraint(s, pltpu.HBM) for s in srcs],
      *[pltpu.with_memory_space_constraint(lax.empty(l.shape, l.dtype), pltpu.HBM) for l in lands],
      *([after] if after is not None else []))
    return (res[:n_sem], res[n_sem:n_sem + n], res[n_sem + n:n_sem + 2 * n]), res[-1]


def _exchange_wait(started, after, *, name):
    sems, srcs, lands = started
    n, n_sem = len(srcs), len(sems)

    def body(*refs):
        for cp in _chip_copies(refs[:n], refs[n:2 * n], refs[2 * n:2 * n + n_sem], n_sem == 3):
            cp.wait()

    res = pl.pallas_call(
        body, name=name, in_specs=[_HBM] * (2 * n) + [_SEM] * n_sem + [_ANY], out_specs=[_HBM] * (2 * n),
        out_shape=[pltpu.HBM(a.shape, a.dtype) for a in list(srcs) + list(lands)],
        input_output_aliases={i: i for i in range(2 * n)},
        compiler_params=pltpu.CompilerParams(has_side_effects=_EFFECT),
    )(*srcs, *lands, *sems, after)
    return res[:n], res[n:]


def _swap_sibling(parts):
    n = len(parts)

    def body(*refs):
        ins, outs = refs[:n], refs[n:2 * n]
        send_sems, recv_sems = refs[2 * n:]
        x, y, c = lax.axis_index("x"), lax.axis_index("y"), lax.axis_index("c")
        copies = []
        for t in range(n):
            cp = pltpu.make_async_remote_copy(ins[t], outs[t], send_sems.at[t], recv_sems.at[t],
                                              device_id=(x, y, 1 - c), device_id_type=MESH)
            cp.start()
            copies.append(cp)
        for cp in copies:
            cp.wait()

    return pl.pallas_call(
        body, name="swap_sibling", in_specs=[_ANY] * n, out_specs=[_ANY] * n,
        out_shape=[_sds(p.shape, p.dtype) for p in parts],
        scratch_shapes=[pltpu.SemaphoreType.DMA((n,)), pltpu.SemaphoreType.DMA((n,))],
        compiler_params=_params(),
    )(*parts)


def _allreduce_small(pack):
    R, C = pack.shape

    def body(in_ref, out_ref, slots, send_sems, recv_sems):
        x, y, c = lax.axis_index("x"), lax.axis_index("y"), lax.axis_index("c")
        me = 4 * x + 2 * y + c
        slots[me] = in_ref[...]
        copies = []
        for k in range(1, N_DEV):
            peer = (x ^ ((k >> 2) & 1), y ^ ((k >> 1) & 1), c ^ (k & 1))
            cp = pltpu.make_async_remote_copy(in_ref, slots.at[me], send_sems.at[k - 1], recv_sems.at[k - 1],
                                              device_id=peer, device_id_type=MESH)
            cp.start()
            copies.append(cp)
        for cp in copies:
            cp.wait()
        acc = slots[0]
        for d in range(1, N_DEV):
            acc = acc + slots[d]
        out_ref[...] = acc

    return pl.pallas_call(
        body, name="allreduce_small", out_shape=_sds((R, C), F32),
        in_specs=[pl.BlockSpec(memory_space=pltpu.VMEM)], out_specs=pl.BlockSpec(memory_space=pltpu.VMEM),
        scratch_shapes=[pltpu.VMEM((N_DEV, R, C), F32), pltpu.SemaphoreType.DMA((N_DEV - 1,)),
                        pltpu.SemaphoreType.DMA((N_DEV - 1,))],
        compiler_params=_params(),
    )(pack)


_Z_CQ, _Z_CKV, _Z_HQ, _Z_HFF, _Z_HFB, _Z_HI, _Z_HG, _Z_KR, _Z_END = 0, 256, 512, 1024, 1536, 2048, 2560, 3072, 3200


def _to_z_layout(wt):
    pad = jnp.zeros((_Z_END - _Z_KR - QK_ROPE, wt.shape[1]), wt.dtype)
    return jnp.concatenate([wt[:512], wt[512 + QK_ROPE:], wt[512:512 + QK_ROPE], pad], axis=0)


def _from_z_layout(wt):
    return jnp.concatenate([wt[:512], wt[_Z_KR:_Z_KR + QK_ROPE], wt[512:_Z_KR]], axis=0)


def _col_shards_to_full(g):
    return jnp.transpose(g, (1, 0, 2)).reshape(g.shape[1], -1)


def _full_to_col_shards(w):
    r, c = w.shape
    return jnp.transpose(w.reshape(r, N_CHIPS, c // N_CHIPS), (1, 0, 2))


def _full_to_row_shards(w):
    r, c = w.shape
    return w.reshape(N_CHIPS, r // N_CHIPS, c)


def kernel(x, p, positions, g_mix, w_in, g_qa, g_kva, w_qb, w_kvb, g_qn, g_kn, lb_param, g_hgo, w_o, g_ffn, w_gate, w_up, w_down, g_ple, w_ple_gate, w_ple_proj, loss_target, m_g_mix, m_w_in, m_g_qa, m_g_kva, m_w_qb, m_w_kvb, m_g_qn, m_g_kn, m_lb_param, m_g_hgo, m_w_o, m_g_ffn, m_w_gate, m_w_up, m_w_down, m_g_ple, m_w_ple_gate, m_w_ple_proj, v_g_mix, v_w_in, v_g_qa, v_g_kva, v_w_qb, v_w_kvb, v_g_qn, v_g_kn, v_lb_param, v_g_hgo, v_w_o, v_g_ffn, v_w_gate, v_w_up, v_w_down, v_g_ple, v_w_ple_gate, v_w_ple_proj):
    w_named = dict(g_mix=g_mix, w_in=w_in, g_qa=g_qa, g_kva=g_kva, w_qb=w_qb, w_kvb=w_kvb, g_qn=g_qn, g_kn=g_kn,
                   lb_param=lb_param, g_hgo=g_hgo, w_o=w_o, g_ffn=g_ffn, w_gate=w_gate, w_up=w_up, w_down=w_down,
                   g_ple=g_ple, w_ple_gate=w_ple_gate, w_ple_proj=w_ple_proj)
    m_named = dict(g_mix=m_g_mix, w_in=m_w_in, g_qa=m_g_qa, g_kva=m_g_kva, w_qb=m_w_qb, w_kvb=m_w_kvb, g_qn=m_g_qn,
                   g_kn=m_g_kn, lb_param=m_lb_param, g_hgo=m_g_hgo, w_o=m_w_o, g_ffn=m_g_ffn, w_gate=m_w_gate,
                   w_up=m_w_up, w_down=m_w_down, g_ple=m_g_ple, w_ple_gate=m_w_ple_gate, w_ple_proj=m_w_ple_proj)
    v_named = dict(g_mix=v_g_mix, w_in=v_w_in, g_qa=v_g_qa, g_kva=v_g_kva, w_qb=v_w_qb, w_kvb=v_w_kvb, g_qn=v_g_qn,
                   g_kn=v_g_kn, lb_param=v_lb_param, g_hgo=v_g_hgo, w_o=v_w_o, g_ffn=v_g_ffn, w_gate=v_w_gate,
                   w_up=v_w_up, w_down=v_w_down, g_ple=v_g_ple, w_ple_gate=v_w_ple_gate, w_ple_proj=v_w_ple_proj)
    order = list(w_named)
    transposed = ("w_in", "w_qb", "w_gate", "w_up")
    col_sharded = ("w_kvb", "w_ple_proj")
    row_sharded = ("w_o", "w_down", "w_ple_gate")
    big = transposed + col_sharded + row_sharded

    def view(n, a):
        return jnp.transpose(a[0]) if n in transposed else a[0]

    def unview(n, a):
        return (jnp.transpose(a) if n in transposed else a)[None]

    def to_shards(n, g):
        return _full_to_col_shards(g) if n in col_sharded else _full_to_row_shards(g)

    x2d, p2d, tgt = x[0], p[0, 0], loss_target[0]
    T, D = x2d.shape

    lb_flat = lb_param.reshape(-1, lb_param.shape[-1])
    gather_groups = (("w_in",), ("w_qb", "w_kvb"), ("w_o", "w_gate", "w_up"), ("w_down", "w_ple_gate", "w_ple_proj"))
    gather_started = []

    def gather_start(gi, after):
        srcs = [view(n, w_named[n]).astype(BF16) for n in gather_groups[gi]] + ([lb_flat] if gi == 0 else [])
        started, token = _exchange_start(srcs, gather=True, name=f"gather_start_{gi}", after=after)
        gather_started.append(started)
        return token

    full = {}

    def gather_wait(gi, after):
        _, got = _exchange_wait(gather_started[gi], after, name=f"gather_wait_{gi}")
        for n, g in zip(gather_groups[gi], got):
            full[n] = _col_shards_to_full(g) if n in col_sharded else g.reshape(-1, g.shape[-1])
        return got

    inv_freq = ROPE_THETA ** (-jnp.arange(0, QK_ROPE, 2, dtype=F32) / QK_ROPE)
    ang = positions[0].astype(F32)[:, None] * inv_freq
    cos, sin = jnp.cos(ang), jnp.sin(ang)
    g_hgo_row = g_hgo.reshape(1, -1)

    token = gather_start(0, None)
    h1 = _stage(_rms, [x2d], [g_mix], [_sds((T, D), BF16)], [], name="norm_mix", after=token)[0]
    got = gather_wait(0, h1)
    token = got[0]
    for gi in range(1, len(gather_groups)):
        token = gather_start(gi, token)
    lb_full = _col_shards_to_full(got[-1]).reshape(lb_param.shape[0], lb_param.shape[1], -1)
    w_in_zt = _to_z_layout(full["w_in"])
    z = _mm(h1, w_in_zt, tb=True, name="in_proj", after=token)
    qn, kvn = _stage(_mla_a_fn, [_cols(z, 256, 0), _cols(z, 256, 1)], [g_qa, g_kva],
                     [_sds((T, 256), BF16), _sds((T, 256), BF16)], [], name="mla_latent_norm")
    gather_wait(1, qn)
    q_raw = _mm(qn, full["w_qb"], tb=True, name="q_up")
    kv_raw = _mm(kvn, full["w_kvb"], name="kv_up")
    kr = _cols(z, LANE, _Z_KR // LANE)
    q, k, v = _stage(_mla_b_fn, [q_raw, kv_raw, kr, cos, sin], [g_qn, g_kn],
                     [_sds((MLA_HEADS, T, QK_HEAD), BF16), _sds((MLA_HEADS, T, QK_HEAD), BF16),
                      _sds((MLA_HEADS, T, V_HEAD), BF16)], [], name="mla_qk_norm_rope")
    att, lse = _attention_fwd(q, k, v)

    lower = _lower_bounds(lb_full)
    lower3 = lower.reshape(2, HG_HEADS, 1, HG_DK)
    cq_blk, cf_blk, cb_blk, cv_blk = _Z_HQ // LANE, _Z_HFF // LANE, _Z_HFB // LANE, _Z_HI // LANE
    o_f, st_f = _gla_fwd(z, lower3[0], rev=False, col_q=cq_blk, col_f=cf_blk, col_v=cv_blk)
    o_b, st_b = _gla_fwd(z, lower3[1], rev=True, col_q=cq_blk, col_f=cb_blk, col_v=cv_blk)
    hg = _cols(z, 512, _Z_HG // 512)
    mix = _stage(_post_fn, [att, o_f, o_b, hg], [g_hgo_row], [_sds((T, att.shape[1] + o_f.shape[1]), BF16)], [],
                 name="mix_out")[0]
    gather_wait(2, mix)
    x1 = _mm(mix, full["w_o"], add=x2d, name="out_proj")
    h2 = _stage(_rms, [x1], [g_ffn], [_sds((T, D), BF16)], [], name="norm_ffn")[0]
    gt = _mm(h2, full["w_gate"], tb=True, name="ffn_gate")
    up = _mm(h2, full["w_up"], tb=True, name="ffn_up")
    act = _stage(_swiglu_fn, [gt, up], [], [_sds(gt.shape, BF16)], [], name="swiglu")[0]
    gather_wait(3, act)
    x2 = _mm(act, full["w_down"], add=x1, name="ffn_down")
    h3 = _stage(_rms, [x2], [g_ple], [_sds((T, D), BF16)], [], name="norm_ple")[0]
    pg = _mm(h3, full["w_ple_gate"], name="ple_gate")
    pp = _mm(p2d, full["w_ple_proj"], name="ple_proj")
    dx3, dpg, dpp, loss_part = _stage(_ple_loss_fn, [x2, pg, pp, tgt], [],
                                      [_sds((T, D), F32), _sds((T, D), BF16), _sds((T, D), BF16)],
                                      [_sds((1, 1), F32)], name="ple_loss")

    grads = {}
    scatter_groups = (("w_ple_proj", "w_ple_gate", "w_down"), ("w_gate", "w_up", "w_o"), ("w_qb", "w_kvb", "w_in"))
    scatter_started = []

    def scatter_start(gi):
        srcs = [to_shards(n, grads[n]) for n in scatter_groups[gi]]
        started, token = _exchange_start(srcs, gather=False, name=f"scatter_start_{gi}")
        scatter_started.append(started)
        return token

    grads["w_ple_proj"] = _mm(p2d, dpp, ta=True, out_dtype=BF16, name="d_w_ple_proj")
    grads["w_ple_gate"] = _mm(h3, dpg, ta=True, out_dtype=BF16, name="d_w_ple_gate")
    dh3 = _mm(dpg, full["w_ple_gate"], tb=True, name="d_h3")
    dx2, grads["g_ple"] = _stage(_norm_bwd_fn, [x2, dh3, dx3], [g_ple], [_sds((T, D), F32)], [_sds((1, D), F32)],
                                 name="norm_ple_bwd")
    dact = _mm(dx2, full["w_down"], tb=True, name="d_act")
    grads["w_down"] = _mm(act, dx2, ta=True, out_dtype=BF16, name="d_w_down")
    token = scatter_start(0)
    dgt, dup = _stage(_swiglu_bwd_fn, [gt, up, dact], [], [_sds(gt.shape, BF16), _sds(gt.shape, BF16)], [],
                      name="swiglu_bwd", after=token)
    grads["w_gate"] = _mm(dgt, h2, ta=True, out_dtype=BF16, name="d_w_gate")
    grads["w_up"] = _mm(dup, h2, ta=True, out_dtype=BF16, name="d_w_up")
    dh2 = _mm(dgt, full["w_gate"], name="d_h2_gate")
    dh2 = _mm(dup, full["w_up"], add=dh2, name="d_h2_up")
    dx1, grads["g_ffn"] = _stage(_norm_bwd_fn, [x1, dh2, dx2], [g_ffn], [_sds((T, D), F32)], [_sds((1, D), F32)],
                                 name="norm_ffn_bwd")
    dmix = _mm(dx1, full["w_o"], tb=True, name="d_mix")
    grads["w_o"] = _mm(mix, dx1, ta=True, out_dtype=BF16, name="d_w_o")

    token = scatter_start(1)
    half = MLA_HEADS * V_HEAD
    do, dhg, dg_hgo = _stage(_post_bwd_fn, [o_f, o_b, hg, _cols(dmix, half, 1)], [g_hgo_row],
                             [_sds((T, half), F32), _sds((T, half), BF16)], [_sds((1, half), F32)], name="mix_out_bwd",
                             after=token)
    grads["g_hgo"] = dg_hgo
    dhq_f, dhi_f, dhf_f, dlow_f = _gla_bwd(z, lower3[0], st_f, do, None, rev=False,
                                           col_q=cq_blk, col_f=cf_blk, col_v=cv_blk)
    dhq, dhi, dhf_b, dlow_b = _gla_bwd(z, lower3[1], st_b, do, (dhq_f, dhi_f), rev=True,
                                       col_q=cq_blk, col_f=cb_blk, col_v=cv_blk)

    dq, dk, dv = _attention_bwd(q, k, v, att, lse, dmix)
    dq_raw, dkv_raw, dkr, grads["g_qn"], grads["g_kn"] = _stage(
        _mla_b_bwd_fn, [q_raw, kv_raw, kr, cos, sin, dq, dk, dv], [g_qn, g_kn],
        [_sds(q_raw.shape, BF16), _sds(kv_raw.shape, BF16), _sds((T, LANE), BF16)],
        [_sds(g_qn.shape, F32), _sds(g_kn.shape, F32)], name="mla_qk_norm_rope_bwd")
    grads["w_qb"] = _mm(dq_raw, qn, ta=True, out_dtype=BF16, name="d_w_qb")
    grads["w_kvb"] = _mm(kvn, dkv_raw, ta=True, out_dtype=BF16, name="d_w_kvb")
    dqn = _mm(dq_raw, full["w_qb"], name="d_qn")
    dkvn = _mm(dkv_raw, full["w_kvb"], tb=True, name="d_kvn")
    dcq, dckv, grads["g_qa"], grads["g_kva"] = _stage(
        _mla_a_bwd_fn, [_cols(z, 256, 0), _cols(z, 256, 1), dqn, dkvn], [g_qa, g_kva],
        [_sds((T, 256), BF16), _sds((T, 256), BF16)], [_sds(g_qa.shape, F32), _sds(g_kva.shape, F32)],
        name="mla_latent_norm_bwd")
    dz = jnp.concatenate([dcq, dckv, dhq, dhf_f, dhf_b, dhi, dhg, dkr], axis=1)
    grads["w_in"] = _from_z_layout(_mm(dz, h1, ta=True, out_dtype=BF16, name="d_w_in"))
    token = scatter_start(2)
    dh1 = _mm(dz, w_in_zt, name="d_h1", after=token)
    grad_x, grads["g_mix"] = _stage(_norm_bwd_fn, [x2d, dh1, dx1], [g_mix], [_sds((T, D), F32)], [_sds((1, D), F32)],
                                    name="norm_mix_bwd")

    chip = 2 * lax.axis_index("x") + lax.axis_index("y")
    partial_sum = {}
    after = grad_x
    for gi, names in enumerate(scatter_groups):
        shards, recvs = _exchange_wait(scatter_started[gi], after, name=f"scatter_wait_{gi}")
        for n, s, r in zip(names, shards, recvs):
            partial_sum[n] = _sum4(s, r, chip.reshape(1), name="sum_" + n)
        after = partial_sum[names[-1]]
    partial_sums = [partial_sum[n] for n in big]
    sibling_sums = _swap_sibling(partial_sums)

    small = ("g_mix", "g_qa", "g_kva", "g_qn", "g_kn", "g_hgo", "g_ffn", "g_ple")
    small_all = small + ("lb_param",)
    width = -(-max(w_named[n].size for n in small_all) // LANE) * LANE

    def row(a):
        a = a.reshape(1, -1)
        return jnp.pad(a, ((0, 0), (0, width - a.shape[1])))

    dlower = jnp.concatenate([dlow_f.reshape(1, -1), dlow_b.reshape(1, -1)], axis=0)
    pack = jnp.concatenate([row(grads[n]) for n in small] + [row(dlower[0]), row(dlower[1]), row(loss_part)]
                           + [jnp.zeros((5, width), F32)], axis=0)
    red = _allreduce_small(pack)
    loss = red[10, 0]
    dlower_sum = red[8:10, :lb_full.shape[-1]]
    dlb_full = _lower_bounds_bwd(lb_full, dlower_sum)
    fshard = lb_param.shape[-1]
    dlb = lax.dynamic_slice_in_dim(dlb_full, chip * fshard, fshard, axis=2)

    out_g, out_d, out_m, out_v = {}, {}, {}, {}
    for n, mine, theirs in zip(big, partial_sums, sibling_sums):
        out_g[n], out_d[n], out_m[n], out_v[n] = (
            unview(n, t) for t in _adamw(view(n, w_named[n]), mine, theirs, view(n, m_named[n]), view(n, v_named[n]),
                                         name="adamw_" + n))
    g_rows = [red[i:i + 1] for i in range(len(small))] + [row(dlb)]
    g_pack = jnp.concatenate(g_rows + [jnp.zeros((16 - len(g_rows), width), F32)], axis=0)

    def packed(named):
        rows = [row(named[n]) for n in small_all]
        return jnp.concatenate(rows + [jnp.ones((16 - len(rows), width), F32)], axis=0)

    s_g, s_d, s_m, s_v = _adamw(packed(w_named), g_pack, None, packed(m_named), packed(v_named), name="adamw_small")
    for i, n in enumerate(small_all):
        size = w_named[n].size
        for src, dst in ((s_g, out_g), (s_d, out_d), (s_m, out_m), (s_v, out_v)):
            dst[n] = src[i, :size].reshape(w_named[n].shape)

    return (loss, grad_x[None], *[out_g[n] for n in order], *[out_d[n] for n in order],
            *[out_m[n] for n in order], *[out_v[n] for n in order])
```

```python
import functools

import jax
import jax.numpy as jnp
from jax import lax
from jax.experimental import pallas as pl
from jax.experimental.pallas import tpu as pltpu

F32 = jnp.float32
BF16 = jnp.bfloat16
MESH = pl.DeviceIdType.MESH

EPS = 1e-6
ROPE_THETA = 10000.0
MLA_HEADS = 4
QK_NOPE = 128
QK_ROPE = 64
QK_HEAD = QK_NOPE + QK_ROPE
V_HEAD = 128
HG_HEADS = 4
HG_DK = 128
CHUNK = 64
ADAM_LR = 0.001
ADAM_B1 = 0.9
ADAM_B2 = 0.999
ADAM_EPS = 1e-08
ADAM_WD = 0.01
ADAM_STEP = 10

LANE = 128
VMEM_LIMIT = 56 * 1024 * 1024
TOK_TILE = 256
GLA_GROUP = 8
ATT_TQ = 1024
ATT_TK = 1024
ATT_CHUNK = 512
LOG2_E = 1.4426950408889634
N_CHIPS = 4
N_DEV = 8


_ANY = pl.BlockSpec(memory_space=pl.ANY)


def _params(dims=None, **kw):
    return pltpu.CompilerParams(dimension_semantics=dims, vmem_limit_bytes=VMEM_LIMIT, **kw)


def _tile_candidates(n, cap):
    out = [d for d in range(LANE, min(n, cap) + 1, LANE) if n % d == 0]
    if n <= cap and n not in out:
        out.append(n)
    return out or [n]


MM_VMEM_BUDGET = 34 * 1024 * 1024
MM_MAX_ROWS = 1536
HBM_BYTES_PER_S = 2.8e12
MXU_FLOPS_PER_S = 8e14
STEP_OVERHEAD_S = 0.35e-6


def _mm_tiles(M, N, K, a_bytes, b_bytes, o_bytes, has_add, full_rows=False):
    best = None
    for tm in _tile_candidates(M, MM_MAX_ROWS):
        for tn in ([N] if full_rows else _tile_candidates(N, N)):
            for tk in _tile_candidates(K, K):
                ni, nj, nk = M // tm, N // tn, K // tk
                vmem = 2 * (tm * tk * a_bytes + tk * tn * b_bytes + tm * tn * o_bytes * (2 if has_add else 1))
                vmem += tm * tn * 4 * (2 if nk > 1 else 1)
                vmem += (tm * tk * 2 if a_bytes > 2 else 0) + (tk * tn * 2 if b_bytes > 2 else 0)
                if vmem > MM_VMEM_BUDGET:
                    continue
                moved = M * K * a_bytes * (nj if nk > 1 else 1) + K * N * b_bytes * (1 if nj == nk == 1 else ni)
                moved += M * N * o_bytes * (2 if has_add else 1)
                t = max(moved / HBM_BYTES_PER_S, 2 * M * N * K / MXU_FLOPS_PER_S) + ni * nj * nk * STEP_OVERHEAD_S
                if best is None or t < best[0]:
                    best = (t, tm, tn, tk)
    assert best is not None, (M, N, K)
    return best[1:]


def _dot_raw(a, b, kind):
    nb = a.ndim - 2
    batch = ((0,), (0,)) if nb else ((), ())
    ca = nb if kind == "tn" else nb + 1
    cb = nb + 1 if kind == "nt" else nb
    return lax.dot_general(a.astype(BF16), b.astype(BF16), (((ca,), (cb,)), batch), preferred_element_type=F32)


@functools.partial(jax.custom_vjp, nondiff_argnums=(2,))
def _bdot(a, b, kind):
    return _dot_raw(a, b, kind)


def _bdot_fwd(a, b, kind):
    return _dot_raw(a, b, kind), (a, b)


def _bdot_bwd(kind, res, g):
    a, b = res
    if kind == "nn":
        da, db = _bdot(g, b, "nt"), _bdot(a, g, "tn")
    elif kind == "nt":
        da, db = _bdot(g, b, "nn"), _bdot(g, a, "tn")
    else:
        da, db = _bdot(b, g, "nt"), _bdot(a, g, "nn")
    return da.astype(a.dtype), db.astype(b.dtype)


_bdot.defvjp(_bdot_fwd, _bdot_bwd)


def _mm(a, b, *, name, ta=False, tb=False, add=None, out_dtype=F32, after=None):
    K, M = a.shape if ta else a.shape[::-1]
    N, Kb = b.shape if tb else b.shape[::-1]
    assert K == Kb, (a.shape, b.shape, ta, tb)
    tm, tn, tk = _mm_tiles(M, N, K, a.dtype.itemsize, b.dtype.itemsize, jnp.dtype(out_dtype).itemsize, add is not None)
    nk = K // tk
    kind = "tn" if ta else ("nt" if tb else "nn")
    assert not (ta and tb)
    a_spec = pl.BlockSpec((tk, tm), lambda i, j, k: (k, i)) if ta else pl.BlockSpec((tm, tk), lambda i, j, k: (i, k))
    b_spec = pl.BlockSpec((tn, tk), lambda i, j, k: (j, k)) if tb else pl.BlockSpec((tk, tn), lambda i, j, k: (k, j))
    o_spec = pl.BlockSpec((tm, tn), lambda i, j, k: (i, j))
    has_add = add is not None

    def body(*refs):
        a_ref, b_ref = refs[0], refs[1]
        add_ref = refs[2] if has_add else None
        o_ref = refs[n_in]
        part = _dot_raw(a_ref[...], b_ref[...], kind)
        if nk == 1:
            if has_add:
                part = part + add_ref[...].astype(F32)
            o_ref[...] = part.astype(o_ref.dtype)
            return
        acc_ref = refs[-1]
        k = pl.program_id(2)

        @pl.when(k == 0)
        def _():
            acc_ref[...] = part

        @pl.when(k > 0)
        def _():
            acc_ref[...] += part

        @pl.when(k == nk - 1)
        def _():
            r = acc_ref[...]
            if has_add:
                r = r + add_ref[...].astype(F32)
            o_ref[...] = r.astype(o_ref.dtype)

    ins = [a, b] + ([add] if has_add else []) + ([after] if after is not None else [])
    in_specs = [a_spec, b_spec] + ([o_spec] if has_add else []) + ([_ANY] if after is not None else [])
    n_in = len(ins)
    return pl.pallas_call(
        body, name=name, grid=(M // tm, N // tn, nk), in_specs=in_specs, out_specs=o_spec,
        out_shape=jax.ShapeDtypeStruct((M, N), out_dtype),
        scratch_shapes=[pltpu.VMEM((tm, tn), F32)] if nk > 1 else [],
        compiler_params=_params(("parallel", "parallel", "arbitrary")),
    )(*ins)


def _mm_fused(a, b, fn, tiles, params, out_dtypes, sums, *, name, ta=False, tb=False, full_rows=False, after=None):
    K, M = a.shape if ta else a.shape[::-1]
    N, Kb = b.shape if tb else b.shape[::-1]
    assert K == Kb and not (ta and tb), (a.shape, b.shape, ta, tb)
    per_elem = sum(t.dtype.itemsize for t in tiles) + sum(jnp.dtype(d).itemsize for d in out_dtypes)
    tm, tn, tk = _mm_tiles(M, N, K, a.dtype.itemsize, b.dtype.itemsize, per_elem, False, full_rows)
    nk = K // tk
    kind = "tn" if ta else ("nt" if tb else "nn")
    a_spec = pl.BlockSpec((tk, tm), lambda i, j, k: (k, i)) if ta else pl.BlockSpec((tm, tk), lambda i, j, k: (i, k))
    b_spec = pl.BlockSpec((tn, tk), lambda i, j, k: (j, k)) if tb else pl.BlockSpec((tk, tn), lambda i, j, k: (k, j))
    o_spec = pl.BlockSpec((tm, tn), lambda i, j, k: (i, j))
    ins = [a, b] + list(tiles) + list(params) + ([after] if after is not None else [])
    in_specs = [a_spec, b_spec] + [o_spec] * len(tiles)
    in_specs += [pl.BlockSpec(p.shape, lambda i, j, k, nd=p.ndim: (0,) * nd) for p in params]
    in_specs += [_ANY] if after is not None else []
    n_in, n_t, n_p, n_o = len(ins), len(tiles), len(params), len(out_dtypes)

    def body(*refs):
        outs, sum_refs = refs[n_in:n_in + n_o], refs[n_in + n_o:n_in + n_o + len(sums)]

        def finish(r):
            res = fn(r, *[t[...] for t in refs[2:2 + n_t + n_p]])
            for o_ref, v in zip(outs, res[:n_o]):
                o_ref[...] = v.astype(o_ref.dtype)
            first = jnp.logical_and(pl.program_id(0) == 0, pl.program_id(1) == 0)
            for s_ref, v in zip(sum_refs, res[n_o:]):
                @pl.when(first)
                def _(s_ref=s_ref, v=v):
                    s_ref[...] = v

                @pl.when(jnp.logical_not(first))
                def _(s_ref=s_ref, v=v):
                    s_ref[...] += v

        part = _dot_raw(refs[0][...], refs[1][...], kind)
        if nk == 1:
            finish(part)
            return
        acc_ref = refs[-1]
        k = pl.program_id(2)

        @pl.when(k == 0)
        def _():
            acc_ref[...] = part

        @pl.when(k > 0)
        def _():
            acc_ref[...] += part

        @pl.when(k == nk - 1)
        def _():
            finish(acc_ref[...])

    out_shape = [_sds((M, N), d) for d in out_dtypes] + list(sums)
    out_specs = [o_spec] * n_o + [pl.BlockSpec(s.shape, lambda i, j, k, nd=len(s.shape): (0,) * nd) for s in sums]
    order = ("arbitrary",) * 3 if sums else ("parallel", "parallel", "arbitrary")
    return pl.pallas_call(
        body, name=name, grid=(M // tm, N // tn, nk), in_specs=in_specs, out_specs=out_specs, out_shape=out_shape,
        scratch_shapes=[pltpu.VMEM((tm, tn), F32)] if nk > 1 else [], compiler_params=_params(order),
    )(*ins)


def _cols(arr, width, block):
    return (arr, width, block)


def _stage(fn, tiles, params, out_tiles, out_sums, *, name, tile=TOK_TILE, after=None):
    def tok_spec(shape, width=None, block=0):
        if len(shape) == 2:
            w = shape[1] if width is None else width
            return pl.BlockSpec((tile, w), lambda i: (i, block))
        return pl.BlockSpec((shape[0], tile, shape[2]), lambda i: (0, i, 0))

    arrays, in_specs = [], []
    for t in tiles:
        if isinstance(t, tuple):
            arr, width, block = t
            arrays.append(arr)
            in_specs.append(tok_spec(arr.shape, width, block))
        else:
            arrays.append(t)
            in_specs.append(tok_spec(t.shape))
    n_tok = arrays[0].shape[0] if arrays[0].ndim == 2 else arrays[0].shape[1]
    for p in params:
        arrays.append(p)
        in_specs.append(pl.BlockSpec(p.shape, lambda i, nd=p.ndim: (0,) * nd))
    out_shape = list(out_tiles) + list(out_sums)
    out_specs = [tok_spec(o.shape) for o in out_tiles]
    out_specs += [pl.BlockSpec(o.shape, lambda i, nd=len(o.shape): (0,) * nd) for o in out_sums]
    n_fn, n_ot = len(arrays), len(out_tiles)
    if after is not None:
        arrays.append(after)
        in_specs.append(_ANY)
    n_in = len(arrays)

    def body(*refs):
        res = fn(*[r[...] for r in refs[:n_fn]])
        if not isinstance(res, (tuple, list)):
            res = (res,)
        outs = refs[n_in:]
        for o_ref, r in zip(outs[:n_ot], res[:n_ot]):
            o_ref[...] = r.astype(o_ref.dtype)
        i = pl.program_id(0)
        for o_ref, r in zip(outs[n_ot:], res[n_ot:]):
            @pl.when(i == 0)
            def _(o_ref=o_ref, r=r):
                o_ref[...] = r.astype(o_ref.dtype)

            @pl.when(i > 0)
            def _(o_ref=o_ref, r=r):
                o_ref[...] += r.astype(o_ref.dtype)

    res = pl.pallas_call(
        body, name=name, grid=(n_tok // tile,), in_specs=in_specs, out_specs=out_specs, out_shape=out_shape,
        compiler_params=_params(("arbitrary",)),
    )(*arrays)
    return res


def _sds(shape, dtype):
    return jax.ShapeDtypeStruct(tuple(shape), dtype)


def _sigmoid(x):
    return 1.0 / (1.0 + jnp.exp(-x))


def _rms(x, g):
    return x * lax.rsqrt(jnp.mean(x * x, axis=-1, keepdims=True) + EPS) * g


def _norm_bwd_fn(x, dh, dres, g):
    _, vjp = jax.vjp(_rms, x, g)
    dx, dg = vjp(dh)
    return dx + dres, dg


def _mla_a_fn(cq, ckv, g_qa, g_kva):
    return _rms(cq, g_qa), _rms(ckv, g_kva)


def _mla_a_bwd_fn(cq, ckv, dqn, dkvn, g_qa, g_kva):
    _, vjp = jax.vjp(_mla_a_fn, cq, ckv, g_qa, g_kva)
    return vjp((dqn, dkvn))


def _rope(t, cos, sin):
    t1, t2 = t[:, :QK_ROPE // 2], t[:, QK_ROPE // 2:]
    return jnp.concatenate([t1 * cos - t2 * sin, t1 * sin + t2 * cos], axis=-1)


def _mla_b_fn(q_raw, kv_raw, kr, cos, sin, g_qn, g_kn):
    krope = kr[:, :QK_ROPE]
    qs, ks, vs = [], [], []
    for h in range(MLA_HEADS):
        qh = _rms(q_raw[:, h * QK_HEAD:(h + 1) * QK_HEAD], g_qn)
        kvh = kv_raw[:, h * (QK_NOPE + V_HEAD):(h + 1) * (QK_NOPE + V_HEAD)]
        kh = _rms(jnp.concatenate([kvh[:, :QK_NOPE], krope], axis=-1), g_kn)
        qs.append(jnp.concatenate([qh[:, :QK_NOPE], _rope(qh[:, QK_NOPE:], cos, sin)], axis=-1))
        ks.append(jnp.concatenate([kh[:, :QK_NOPE], _rope(kh[:, QK_NOPE:], cos, sin)], axis=-1))
        vs.append(kvh[:, QK_NOPE:])
    return jnp.stack(qs), jnp.stack(ks), jnp.stack(vs)


def _mla_b_bwd_fn(q_raw, kv_raw, kr, cos, sin, dq, dk, dv, g_qn, g_kn):
    _, vjp = jax.vjp(lambda a, b, c, d, e: _mla_b_fn(a, b, c, cos, sin, d, e), q_raw, kv_raw, kr, g_qn, g_kn)
    return vjp((dq, dk, dv))


def _post_fn(a, o_f, o_b, hg, g_hgo):
    o = o_f + o_b
    parts = [a]
    for h in range(HG_HEADS):
        s = slice(h * HG_DK, (h + 1) * HG_DK)
        gate = hg[:, s]
        parts.append(_rms(o[:, s], g_hgo[:, s]) * (gate * _sigmoid(gate)))
    return jnp.concatenate(parts, axis=-1)


def _post_bwd_fn(o_f, o_b, hg, dr, g_hgo):
    def f(o, hg, g):
        return _post_fn(jnp.zeros_like(o), o, jnp.zeros_like(o), hg, g)[:, o.shape[1]:]
    _, vjp = jax.vjp(f, o_f + o_b, hg, g_hgo)
    return vjp(dr)


def _swiglu_fn(gt, up):
    return gt * _sigmoid(gt) * up


def _resid_norm_fn(acc, x, g):
    x_new = acc + x
    return x_new, _rms(x_new, g)


def _swiglu_bwd_fn(gt, up, dact):
    _, vjp = jax.vjp(_swiglu_fn, gt, up)
    return vjp(dact)


def _ple_loss_fn(x2, pg, pp, target):
    gate = _sigmoid(pg)
    err = x2 + gate * pp - target
    dx3 = err * (1.0 / err.shape[-1])
    loss = 0.5 * jnp.sum(jnp.mean(err * err, axis=-1, keepdims=True), axis=0, keepdims=True)
    return dx3, dx3 * pp * gate * (1.0 - gate), dx3 * gate, loss


def _attention_fwd(q, k, v):
    H, T, D = q.shape
    DV = v.shape[-1]
    tq, ck = min(ATT_TQ, T), min(ATT_CHUNK, T)
    c2 = (D ** -0.5) * LOG2_E

    def body(q_ref, k_ref, v_ref, o_ref, lse_ref):
        q_i = q_ref[0]

        def chunk(c, carry):
            m, l, acc = carry
            rows = pl.ds(pl.multiple_of(c * ck, ck), ck)
            s = _dot_raw(q_i, k_ref[0, rows, :], "nt")
            m_new = jnp.maximum(m, jnp.max(s, axis=-1, keepdims=True))
            p = jnp.exp2((s - m_new) * c2)
            alpha = jnp.exp2((m - m_new) * c2)
            l = l * alpha + jnp.sum(p, axis=-1, keepdims=True)
            acc = acc * alpha + _dot_raw(p, v_ref[0, rows, :], "nn")
            return m_new, l, acc

        init = (jnp.full((tq, 1), -jnp.inf, F32), jnp.zeros((tq, 1), F32), jnp.zeros((tq, DV), F32))
        m, l, acc = lax.fori_loop(0, T // ck, chunk, init, unroll=True)
        o_ref[...] = acc / l
        lse_ref[0] = m * c2 + jnp.log2(l)

    return pl.pallas_call(
        body, name="attention_fwd", grid=(H, T // tq),
        in_specs=[pl.BlockSpec((1, tq, D), lambda h, i: (h, i, 0)),
                  pl.BlockSpec((1, T, D), lambda h, i: (h, 0, 0)),
                  pl.BlockSpec((1, T, DV), lambda h, i: (h, 0, 0))],
        out_specs=[pl.BlockSpec((tq, DV), lambda h, i: (i, h)),
                   pl.BlockSpec((1, tq, 1), lambda h, i: (h, i, 0))],
        out_shape=[_sds((T, H * DV), F32), _sds((H, T, 1), F32)],
        compiler_params=_params(("parallel", "parallel")),
    )(q, k, v)


def _attention_bwd(q, k, v, o, lse2, dmix):
    H, T, D = q.shape
    DV = v.shape[-1]
    tk, cq = min(ATT_TK, T), min(ATT_CHUNK, T)
    scale = D ** -0.5
    c2 = scale * LOG2_E

    def body(q_ref, k_ref, v_ref, o_ref, lse_ref, do_ref, dq_ref, dk_ref, dv_ref, delta_ref):
        j = pl.program_id(1)

        @pl.when(j == 0)
        def _():
            delta = lax.dot_general(jnp.ones((8, DV), F32), do_ref[...] * o_ref[...], (((1,), (1,)), ((), ())),
                                    precision=lax.Precision.HIGHEST, preferred_element_type=F32)
            for i in range(T // cq):
                delta_ref[i] = delta[:, i * cq:(i + 1) * cq]
            dq_ref[0] = jnp.zeros((T, D), F32)

        k_j, v_j = k_ref[0], v_ref[0]
        dk_ref[0] = jnp.zeros((tk, D), F32)
        dv_ref[0] = jnp.zeros((tk, DV), F32)

        def chunk(c, carry):
            rows = pl.ds(pl.multiple_of(c * cq, cq), cq)
            q_c = q_ref[0, rows, :]
            do_c = do_ref[rows, :].astype(BF16)
            st = _dot_raw(k_j, q_c, "nt")
            pt = jnp.exp2(st * c2 - lse_ref[0, c])
            dv_ref[0] += _dot_raw(pt, do_c, "nn")
            dpt = _dot_raw(v_j, do_c, "nt")
            dst = pt * (dpt - delta_ref[c, 0:1, :]) * scale
            dk_ref[0] += _dot_raw(dst, q_c, "nn")
            dq_ref[0, rows, :] += _dot_raw(dst, k_j, "tn")
            return carry

        lax.fori_loop(0, T // cq, chunk, 0, unroll=True)

    return pl.pallas_call(
        body, name="attention_bwd", grid=(H, T // tk),
        in_specs=[pl.BlockSpec((1, T, D), lambda h, j: (h, 0, 0)),
                  pl.BlockSpec((1, tk, D), lambda h, j: (h, j, 0)),
                  pl.BlockSpec((1, tk, DV), lambda h, j: (h, j, 0)),
                  pl.BlockSpec((T, DV), lambda h, j: (0, h)),
                  pl.BlockSpec((1, T // cq, 1, cq), lambda h, j: (h, 0, 0, 0)),
                  pl.BlockSpec((T, DV), lambda h, j: (0, h))],
        out_specs=[pl.BlockSpec((1, T, D), lambda h, j: (h, 0, 0)),
                   pl.BlockSpec((1, tk, D), lambda h, j: (h, j, 0)),
                   pl.BlockSpec((1, tk, DV), lambda h, j: (h, j, 0))],
        out_shape=[_sds((H, T, D), F32), _sds((H, T, D), F32), _sds((H, T, DV), F32)],
        scratch_shapes=[pltpu.VMEM((T // cq, 8, cq), F32)],
        compiler_params=_params(("parallel", "arbitrary")),
    )(q, k, v, o, lse2.reshape(H, T // cq, 1, cq), dmix)


def _gla_block(hq, hf, hi, lower, st_in, *, rev, dot):
    rows, dk = hq.shape
    G, C = rows // CHUNK, CHUNK
    q = hq * _sigmoid(hq)
    f = lower + (1.0 - lower) * _sigmoid(hf)
    k = 1.0 - f
    logf = jnp.log(f)
    q3, k3, v3, lf3 = (t.reshape(G, C, dk) for t in (q, k, hi, logf))
    r = lax.broadcasted_iota(jnp.int32, (C, C), 0)
    c = lax.broadcasted_iota(jnp.int32, (C, C), 1)
    tri = ((r <= c) if rev else (r >= c)).astype(F32)
    b = lax.dot_general(jnp.broadcast_to(tri, (G, C, C)), lf3, (((2,), (1,)), ((0,), (0,))),
                        precision=lax.Precision.HIGHEST, preferred_element_type=F32)
    tpos = lax.broadcasted_iota(jnp.int32, (1, C, 1), 1)
    first_half = (tpos >= C // 2) if rev else (tpos <= C // 2 - 1)
    b_mid = jnp.sum(jnp.where(first_half, lf3, 0.0), axis=1, keepdims=True)
    b_last = jnp.sum(lf3, axis=1, keepdims=True)
    a = dot(q3 * jnp.exp(b - b_mid), k3 * jnp.exp(b_mid - b), "nt") * tri
    o_intra = dot(a, v3, "nn")
    kv_t = dot(v3, k3 * jnp.exp(b_last - b), "tn")
    decay = jnp.exp(b_last)
    qd = q3 * jnp.exp(b)
    st = st_in
    o_inter = [None] * G
    for g in (reversed(range(G)) if rev else range(G)):
        o_inter[g] = dot(qd[g], st, "nt")
        st = st * decay[g] + kv_t[g]
    o = o_intra.reshape(rows, dk) + jnp.concatenate(o_inter, axis=0)
    return o, st


def _gla_fwd(z, lower3, *, rev, col_q, col_f, col_v):
    T = z.shape[0]
    rows = min(GLA_GROUP * CHUNK, T)
    nb = T // rows
    blk = (lambda n: nb - 1 - n) if rev else (lambda n: n)

    def body(hq_ref, hf_ref, hi_ref, low_ref, o_ref, st_out_ref, st_ref):
        @pl.when(pl.program_id(1) == 0)
        def _():
            st_ref[...] = jnp.zeros_like(st_ref)

        st_in = st_ref[...]
        st_out_ref[0, 0] = st_in
        o, st = _gla_block(hq_ref[...], hf_ref[...], hi_ref[...], low_ref[0], st_in, rev=rev, dot=_dot_raw)
        o_ref[...] = o
        st_ref[...] = st

    def zspec(col):
        return pl.BlockSpec((rows, HG_DK), lambda h, n: (blk(n), col + h))

    return pl.pallas_call(
        body, name="gla_fwd_rev" if rev else "gla_fwd", grid=(HG_HEADS, nb),
        in_specs=[zspec(col_q), zspec(col_f), zspec(col_v), pl.BlockSpec((1, 1, HG_DK), lambda h, n: (h, 0, 0))],
        out_specs=[pl.BlockSpec((rows, HG_DK), lambda h, n: (blk(n), h)),
                   pl.BlockSpec((1, 1, HG_DK, HG_DK), lambda h, n: (h, blk(n), 0, 0))],
        out_shape=[_sds((T, HG_HEADS * HG_DK), F32), _sds((HG_HEADS, nb, HG_DK, HG_DK), F32)],
        scratch_shapes=[pltpu.VMEM((HG_DK, HG_DK), F32)],
        compiler_params=_params(("parallel", "arbitrary")),
    )(z, z, z, lower3)


def _gla_bwd(z, lower3, states, do, prev, *, rev, col_q, col_f, col_v):
    T = z.shape[0]
    rows = min(GLA_GROUP * CHUNK, T)
    nb = T // rows
    blk = (lambda n: n) if rev else (lambda n: nb - 1 - n)
    has_prev = prev is not None
    fn = functools.partial(_gla_block, rev=rev, dot=_bdot)

    def body(*refs):
        hq_ref, hf_ref, hi_ref, low_ref, st_ref, do_ref = refs[:6]
        rest = refs[6:]
        if has_prev:
            pq_ref, pi_ref = rest[:2]
            rest = rest[2:]
        dhq_ref, dhi_ref, dhf_ref, dlow_ref, dst_ref = rest
        n = pl.program_id(1)

        @pl.when(n == 0)
        def _():
            dst_ref[...] = jnp.zeros_like(dst_ref)

        _, vjp = jax.vjp(fn, hq_ref[...], hf_ref[...], hi_ref[...], low_ref[0], st_ref[0, 0])
        dhq, dhf, dhi, dlow, dst = vjp((do_ref[...], dst_ref[...]))
        dst_ref[...] = dst
        if has_prev:
            dhq = dhq + pq_ref[...]
            dhi = dhi + pi_ref[...]
        dhq_ref[...] = dhq.astype(dhq_ref.dtype)
        dhi_ref[...] = dhi.astype(dhi_ref.dtype)
        dhf_ref[...] = dhf.astype(dhf_ref.dtype)

        @pl.when(n == 0)
        def _():
            dlow_ref[0] = dlow

        @pl.when(n > 0)
        def _():
            dlow_ref[0] += dlow

    def zspec(col):
        return pl.BlockSpec((rows, HG_DK), lambda h, n: (blk(n), col + h))

    hspec = pl.BlockSpec((rows, HG_DK), lambda h, n: (blk(n), h))
    in_specs = [zspec(col_q), zspec(col_f), zspec(col_v), pl.BlockSpec((1, 1, HG_DK), lambda h, n: (h, 0, 0)),
                pl.BlockSpec((1, 1, HG_DK, HG_DK), lambda h, n: (h, blk(n), 0, 0)), hspec]
    ins = [z, z, z, lower3, states, do]
    if has_prev:
        in_specs += [hspec, hspec]
        ins += list(prev)
    wide = HG_HEADS * HG_DK
    acc_dtype = BF16 if has_prev else F32
    return pl.pallas_call(
        body, name="gla_bwd_rev" if rev else "gla_bwd", grid=(HG_HEADS, nb),
        in_specs=in_specs,
        out_specs=[hspec, hspec, hspec, pl.BlockSpec((1, 1, HG_DK), lambda h, n: (h, 0, 0))],
        out_shape=[_sds((T, wide), acc_dtype), _sds((T, wide), acc_dtype), _sds((T, wide), BF16),
                   _sds((HG_HEADS, 1, HG_DK), F32)],
        scratch_shapes=[pltpu.VMEM((HG_DK, HG_DK), F32)],
        compiler_params=_params(("parallel", "arbitrary")),
    )(*ins)


def _lower_fn(lb):
    e = jnp.exp(lb - jnp.max(lb, axis=0, keepdims=True))
    return (e / jnp.sum(e, axis=0, keepdims=True))[0]


def _lower_bounds(lb):
    def body(lb_ref, o_ref):
        o_ref[...] = _lower_fn(lb_ref[...])
    return pl.pallas_call(body, name="lower_bounds", out_shape=_sds(lb.shape[1:], F32))(lb)


def _lower_bounds_bwd(lb, dlower):
    def body(lb_ref, d_ref, o_ref):
        _, vjp = jax.vjp(_lower_fn, lb_ref[...])
        o_ref[...] = vjp(d_ref[...])[0]
    return pl.pallas_call(body, name="lower_bounds_bwd", out_shape=_sds(lb.shape, F32))(lb, dlower)


def _row_tile(r, cap=1024):
    best = None
    for t in range(16, min(r, cap) + 1, 16):
        if r % t == 0:
            best = t
    return best if best is not None else r


def _sum4(shards, recv, chip, *, name):
    _, R, C = shards.shape
    tr = _row_tile(R)

    def body(chip_ref, o_ref, r_ref, out_ref):
        out_ref[...] = ((o_ref[0].astype(F32) + r_ref[0].astype(F32)) + r_ref[1].astype(F32)) + r_ref[2].astype(F32)

    grid_spec = pltpu.PrefetchScalarGridSpec(
        num_scalar_prefetch=1, grid=(R // tr,),
        in_specs=[pl.BlockSpec((1, tr, C), lambda i, chip_ref: (chip_ref[0], i, 0)),
                  pl.BlockSpec((3, tr, C), lambda i, chip_ref: (0, i, 0))],
        out_specs=pl.BlockSpec((tr, C), lambda i, chip_ref: (i, 0)))
    return pl.pallas_call(
        body, name=name, grid_spec=grid_spec, out_shape=_sds((R, C), F32), compiler_params=_params(("parallel",)),
    )(chip, shards, recv)


def _adamw_math(w, g, m, v):
    m = ADAM_B1 * m + (1.0 - ADAM_B1) * g
    v = ADAM_B2 * v + (1.0 - ADAM_B2) * (g * g)
    m_hat = m / (1.0 - ADAM_B1 ** ADAM_STEP)
    v_hat = v / (1.0 - ADAM_B2 ** ADAM_STEP)
    delta = -ADAM_LR * (m_hat / (jnp.sqrt(v_hat) + ADAM_EPS) + ADAM_WD * w)
    return delta, m, v


def _adamw(w, g_a, g_b, m, v, *, name):
    R, C = w.shape
    tr = _row_tile(R)
    two = g_b is not None

    def body(*refs):
        w_ref, ga_ref = refs[0], refs[1]
        rest = refs[2:]
        g = ga_ref[...]
        if two:
            g = g + rest[0][...]
            rest = rest[1:]
        m_ref, v_ref, g_out, d_out, m_out, v_out = rest
        delta, m_new, v_new = _adamw_math(w_ref[...], g, m_ref[...], v_ref[...])
        g_out[...] = g
        d_out[...] = delta
        m_out[...] = m_new
        v_out[...] = v_new

    spec = pl.BlockSpec((tr, C), lambda i: (i, 0))
    ins = [w, g_a] + ([g_b] if two else []) + [m, v]
    return pl.pallas_call(
        body, name=name, grid=(R // tr,), in_specs=[spec] * len(ins), out_specs=[spec] * 4,
        out_shape=[_sds((R, C), F32)] * 4, compiler_params=_params(("parallel",)),
    )(*ins)


def _chip_peers():
    x, y, c = lax.axis_index("x"), lax.axis_index("y"), lax.axis_index("c")
    return (x, y, c), 2 * x + y, [(1 - x, y), (x, 1 - y), (1 - x, 1 - y)]


_HBM = pl.BlockSpec(memory_space=pltpu.HBM)
_SEM = pl.BlockSpec(memory_space=pltpu.SEMAPHORE)
_EFFECT = pltpu.SideEffectType.DATAFLOW_SIDE_EFFECTING


def _chip_copies(srcs, lands, sems, gather):
    (x, y, c), me, chips = _chip_peers()
    copies = []
    for t, (src, land) in enumerate(zip(srcs, lands)):
        for k, (px, py) in enumerate(chips):
            copies.append(pltpu.make_async_remote_copy(
                src if gather else src.at[2 * px + py], land.at[me] if gather else land.at[k],
                sems[0].at[3 * t + k], sems[1].at[3 * t + k], device_id=(px, py, c), device_id_type=MESH))
        if gather:
            copies.append(pltpu.make_async_copy(src, land.at[me], sems[2].at[t]))
    return copies


def _exchange_start(srcs, *, gather, name, after=None):
    n = len(srcs)
    n_sem = 3 if gather else 2
    n_in = 2 * n + (after is not None)
    lands = [_sds(((N_CHIPS,) + s.shape) if gather else ((3,) + s.shape[1:]), s.dtype) for s in srcs]

    def body(*refs):
        for cp in _chip_copies(refs[:n], refs[n:2 * n], refs[n_in:n_in + n_sem], gather):
            cp.start()
        token = refs[-1]
        token[...] = jnp.zeros_like(token)

    sem_shapes = [pltpu.SemaphoreType.DMA((3 * n,)), pltpu.SemaphoreType.DMA((3 * n,))]
    sem_shapes += [pltpu.SemaphoreType.DMA((n,))] if gather else []
    thru = [pltpu.HBM(s.shape, s.dtype) for s in srcs] + [pltpu.HBM(l.shape, l.dtype) for l in lands]
    res = pl.pallas_call(
        body, name=name, in_specs=[_HBM] * (2 * n) + [_ANY] * (after is not None),
        out_specs=[_SEM] * n_sem + [_HBM] * (2 * n) + [pl.BlockSpec(memory_space=pltpu.VMEM)],
        out_shape=sem_shapes + thru + [_sds((8, LANE), F32)], input_output_aliases={i: n_sem + i for i in range(2 * n)},
        compiler_params=pltpu.CompilerParams(has_side_effects=_EFFECT),
    )(*[pltpu.with_memory_space_constraint(s, pltpu.HBM) for s in srcs],
      *[pltpu.with_memory_space_constraint(lax.empty(l.shape, l.dtype), pltpu.HBM) for l in lands],
      *([after] if after is not None else []))
    return (res[:n_sem], res[n_sem:n_sem + n], res[n_sem + n:n_sem + 2 * n]), res[-1]


def _exchange_wait(started, after, *, name):
    sems, srcs, lands = started
    n, n_sem = len(srcs), len(sems)

    def body(*refs):
        for cp in _chip_copies(refs[:n], refs[n:2 * n], refs[2 * n:2 * n + n_sem], n_sem == 3):
            cp.wait()

    res = pl.pallas_call(
        body, name=name, in_specs=[_HBM] * (2 * n) + [_SEM] * n_sem + [_ANY], out_specs=[_HBM] * (2 * n),
        out_shape=[pltpu.HBM(a.shape, a.dtype) for a in list(srcs) + list(lands)],
        input_output_aliases={i: i for i in range(2 * n)},
        compiler_params=pltpu.CompilerParams(has_side_effects=_EFFECT),
    )(*srcs, *lands, *sems, after)
    return res[:n], res[n:]


def _swap_sibling(parts):
    n = len(parts)

    def body(*refs):
        ins, outs = refs[:n], refs[n:2 * n]
        send_sems, recv_sems = refs[2 * n:]
        x, y, c = lax.axis_index("x"), lax.axis_index("y"), lax.axis_index("c")
        copies = []
        for t in range(n):
            cp = pltpu.make_async_remote_copy(ins[t], outs[t], send_sems.at[t], recv_sems.at[t],
                                              device_id=(x, y, 1 - c), device_id_type=MESH)
            cp.start()
            copies.append(cp)
        for cp in copies:
            cp.wait()

    return pl.pallas_call(
        body, name="swap_sibling", in_specs=[_ANY] * n, out_specs=[_ANY] * n,
        out_shape=[_sds(p.shape, p.dtype) for p in parts],
        scratch_shapes=[pltpu.SemaphoreType.DMA((n,)), pltpu.SemaphoreType.DMA((n,))],
        compiler_params=_params(),
    )(*parts)


def _allreduce_small(pack):
    R, C = pack.shape

    def body(in_ref, out_ref, slots, send_sems, recv_sems):
        x, y, c = lax.axis_index("x"), lax.axis_index("y"), lax.axis_index("c")
        me = 4 * x + 2 * y + c
        slots[me] = in_ref[...]
        copies = []
        for k in range(1, N_DEV):
            peer = (x ^ ((k >> 2) & 1), y ^ ((k >> 1) & 1), c ^ (k & 1))
            cp = pltpu.make_async_remote_copy(in_ref, slots.at[me], send_sems.at[k - 1], recv_sems.at[k - 1],
                                              device_id=peer, device_id_type=MESH)
            cp.start()
            copies.append(cp)
        for cp in copies:
            cp.wait()
        acc = slots[0]
        for d in range(1, N_DEV):
            acc = acc + slots[d]
        out_ref[...] = acc

    return pl.pallas_call(
        body, name="allreduce_small", out_shape=_sds((R, C), F32),
        in_specs=[pl.BlockSpec(memory_space=pltpu.VMEM)], out_specs=pl.BlockSpec(memory_space=pltpu.VMEM),
        scratch_shapes=[pltpu.VMEM((N_DEV, R, C), F32), pltpu.SemaphoreType.DMA((N_DEV - 1,)),
                        pltpu.SemaphoreType.DMA((N_DEV - 1,))],
        compiler_params=_params(),
    )(pack)


_Z_CQ, _Z_CKV, _Z_HQ, _Z_HFF, _Z_HFB, _Z_HI, _Z_HG, _Z_KR, _Z_END = 0, 256, 512, 1024, 1536, 2048, 2560, 3072, 3200


def _to_z_layout(wt):
    pad = jnp.zeros((_Z_END - _Z_KR - QK_ROPE, wt.shape[1]), wt.dtype)
    return jnp.concatenate([wt[:512], wt[512 + QK_ROPE:], wt[512:512 + QK_ROPE], pad], axis=0)


def _from_z_layout(wt):
    return jnp.concatenate([wt[:512], wt[_Z_KR:_Z_KR + QK_ROPE], wt[512:_Z_KR]], axis=0)


def _col_shards_to_full(g):
    return jnp.transpose(g, (1, 0, 2)).reshape(g.shape[1], -1)


def _full_to_col_shards(w):
    r, c = w.shape
    return jnp.transpose(w.reshape(r, N_CHIPS, c // N_CHIPS), (1, 0, 2))


def _full_to_row_shards(w):
    r, c = w.shape
    return w.reshape(N_CHIPS, r // N_CHIPS, c)


def kernel(x, p, positions, g_mix, w_in, g_qa, g_kva, w_qb, w_kvb, g_qn, g_kn, lb_param, g_hgo, w_o, g_ffn, w_gate, w_up, w_down, g_ple, w_ple_gate, w_ple_proj, loss_target, m_g_mix, m_w_in, m_g_qa, m_g_kva, m_w_qb, m_w_kvb, m_g_qn, m_g_kn, m_lb_param, m_g_hgo, m_w_o, m_g_ffn, m_w_gate, m_w_up, m_w_down, m_g_ple, m_w_ple_gate, m_w_ple_proj, v_g_mix, v_w_in, v_g_qa, v_g_kva, v_w_qb, v_w_kvb, v_g_qn, v_g_kn, v_lb_param, v_g_hgo, v_w_o, v_g_ffn, v_w_gate, v_w_up, v_w_down, v_g_ple, v_w_ple_gate, v_w_ple_proj):
    w_named = dict(g_mix=g_mix, w_in=w_in, g_qa=g_qa, g_kva=g_kva, w_qb=w_qb, w_kvb=w_kvb, g_qn=g_qn, g_kn=g_kn,
                   lb_param=lb_param, g_hgo=g_hgo, w_o=w_o, g_ffn=g_ffn, w_gate=w_gate, w_up=w_up, w_down=w_down,
                   g_ple=g_ple, w_ple_gate=w_ple_gate, w_ple_proj=w_ple_proj)
    m_named = dict(g_mix=m_g_mix, w_in=m_w_in, g_qa=m_g_qa, g_kva=m_g_kva, w_qb=m_w_qb, w_kvb=m_w_kvb, g_qn=m_g_qn,
                   g_kn=m_g_kn, lb_param=m_lb_param, g_hgo=m_g_hgo, w_o=m_w_o, g_ffn=m_g_ffn, w_gate=m_w_gate,
                   w_up=m_w_up, w_down=m_w_down, g_ple=m_g_ple, w_ple_gate=m_w_ple_gate, w_ple_proj=m_w_ple_proj)
    v_named = dict(g_mix=v_g_mix, w_in=v_w_in, g_qa=v_g_qa, g_kva=v_g_kva, w_qb=v_w_qb, w_kvb=v_w_kvb, g_qn=v_g_qn,
                   g_kn=v_g_kn, lb_param=v_lb_param, g_hgo=v_g_hgo, w_o=v_w_o, g_ffn=v_g_ffn, w_gate=v_w_gate,
                   w_up=v_w_up, w_down=v_w_down, g_ple=v_g_ple, w_ple_gate=v_w_ple_gate, w_ple_proj=v_w_ple_proj)
    order = list(w_named)
    transposed = ("w_in", "w_qb", "w_gate", "w_up")
    col_sharded = ("w_kvb", "w_ple_proj")
    row_sharded = ("w_o", "w_down", "w_ple_gate")
    big = transposed + col_sharded + row_sharded

    def view(n, a):
        return jnp.transpose(a[0]) if n in transposed else a[0]

    def unview(n, a):
        return (jnp.transpose(a) if n in transposed else a)[None]

    def to_shards(n, g):
        return _full_to_col_shards(g) if n in col_sharded else _full_to_row_shards(g)

    x2d, p2d, tgt = x[0], p[0, 0], loss_target[0]
    T, D = x2d.shape

    lb_flat = lb_param.reshape(-1, lb_param.shape[-1])
    gather_groups = (("w_in",), ("w_qb", "w_kvb"), ("w_o", "w_gate", "w_up"), ("w_down", "w_ple_gate", "w_ple_proj"))
    gather_started = []

    def gather_start(gi, after):
        srcs = [view(n, w_named[n]).astype(BF16) for n in gather_groups[gi]] + ([lb_flat] if gi == 0 else [])
        started, token = _exchange_start(srcs, gather=True, name=f"gather_start_{gi}", after=after)
        gather_started.append(started)
        return token

    full = {}

    def gather_wait(gi, after):
        _, got = _exchange_wait(gather_started[gi], after, name=f"gather_wait_{gi}")
        for n, g in zip(gather_groups[gi], got):
            full[n] = _col_shards_to_full(g) if n in col_sharded else g.reshape(-1, g.shape[-1])
        return got

    inv_freq = ROPE_THETA ** (-jnp.arange(0, QK_ROPE, 2, dtype=F32) / QK_ROPE)
    ang = positions[0].astype(F32)[:, None] * inv_freq
    cos, sin = jnp.cos(ang), jnp.sin(ang)
    g_hgo_row = g_hgo.reshape(1, -1)

    token = gather_start(0, None)
    h1 = _stage(_rms, [x2d], [g_mix], [_sds((T, D), BF16)], [], name="norm_mix", after=token)[0]
    got = gather_wait(0, h1)
    token = got[0]
    for gi in range(1, len(gather_groups)):
        token = gather_start(gi, token)
    lb_full = _col_shards_to_full(got[-1]).reshape(lb_param.shape[0], lb_param.shape[1], -1)
    w_in_zt = _to_z_layout(full["w_in"])
    z = _mm(h1, w_in_zt, tb=True, name="in_proj", after=token)
    qn, kvn = _stage(_mla_a_fn, [_cols(z, 256, 0), _cols(z, 256, 1)], [g_qa, g_kva],
                     [_sds((T, 256), BF16), _sds((T, 256), BF16)], [], name="mla_latent_norm")
    gather_wait(1, qn)
    q_raw = _mm(qn, full["w_qb"], tb=True, name="q_up")
    kv_raw = _mm(kvn, full["w_kvb"], name="kv_up")
    kr = _cols(z, LANE, _Z_KR // LANE)
    q, k, v = _stage(_mla_b_fn, [q_raw, kv_raw, kr, cos, sin], [g_qn, g_kn],
                     [_sds((MLA_HEADS, T, QK_HEAD), BF16), _sds((MLA_HEADS, T, QK_HEAD), BF16),
                      _sds((MLA_HEADS, T, V_HEAD), BF16)], [], name="mla_qk_norm_rope")
    att, lse = _attention_fwd(q, k, v)

    lower = _lower_bounds(lb_full)
    lower3 = lower.reshape(2, HG_HEADS, 1, HG_DK)
    cq_blk, cf_blk, cb_blk, cv_blk = _Z_HQ // LANE, _Z_HFF // LANE, _Z_HFB // LANE, _Z_HI // LANE
    o_f, st_f = _gla_fwd(z, lower3[0], rev=False, col_q=cq_blk, col_f=cf_blk, col_v=cv_blk)
    o_b, st_b = _gla_fwd(z, lower3[1], rev=True, col_q=cq_blk, col_f=cb_blk, col_v=cv_blk)
    hg = _cols(z, 512, _Z_HG // 512)
    mix = _stage(_post_fn, [att, o_f, o_b, hg], [g_hgo_row], [_sds((T, att.shape[1] + o_f.shape[1]), BF16)], [],
                 name="mix_out")[0]
    gather_wait(2, mix)
    x1, h2 = _mm_fused(mix, full["w_o"], _resid_norm_fn, [x2d], [g_ffn], [F32, BF16], [], full_rows=True,
                       name="out_proj")
    up = _mm(h2, full["w_up"], tb=True, name="ffn_up")
    gt, act = _mm_fused(h2, full["w_gate"], lambda acc, up: (acc, _swiglu_fn(acc, up)), [up], [], [F32, BF16], [],
                        tb=True, name="ffn_gate")
    gather_wait(3, act)
    x2, h3 = _mm_fused(act, full["w_down"], _resid_norm_fn, [x1], [g_ple], [F32, BF16], [], full_rows=True,
                       name="ffn_down")
    pp = _mm(p2d, full["w_ple_proj"], name="ple_proj")
    dx3, dpg, dpp, loss_part = _mm_fused(
        h3, full["w_ple_gate"], lambda acc, pp, x2, tgt: _ple_loss_fn(x2, acc, pp, tgt), [pp, x2, tgt], [],
        [F32, BF16, BF16], [_sds((1, 1), F32)], full_rows=True, name="ple_gate_loss")

    grads = {}
    scatter_groups = (("w_ple_proj", "w_ple_gate", "w_down"), ("w_gate", "w_up", "w_o"), ("w_qb", "w_kvb", "w_in"))
    scatter_started = []

    def scatter_start(gi):
        srcs = [to_shards(n, grads[n]) for n in scatter_groups[gi]]
        started, token = _exchange_start(srcs, gather=False, name=f"scatter_start_{gi}")
        scatter_started.append(started)
        return token

    grads["w_ple_proj"] = _mm(p2d, dpp, ta=True, out_dtype=BF16, name="d_w_ple_proj")
    grads["w_ple_gate"] = _mm(h3, dpg, ta=True, out_dtype=BF16, name="d_w_ple_gate")
    dx2, grads["g_ple"] = _mm_fused(
        dpg, full["w_ple_gate"], lambda acc, x2, dx3, g: _norm_bwd_fn(x2, acc, dx3, g), [x2, dx3], [g_ple],
        [F32], [_sds((1, D), F32)], tb=True, full_rows=True, name="d_h3_norm_ple_bwd")
    dgt, dup = _mm_fused(dx2, full["w_down"], lambda acc, gt, up: _swiglu_bwd_fn(gt, up, acc), [gt, up], [],
                         [BF16, BF16], [], tb=True, name="d_act_swiglu_bwd")
    grads["w_down"] = _mm(act, dx2, ta=True, out_dtype=BF16, name="d_w_down")
    token = scatter_start(0)
    grads["w_gate"] = _mm(dgt, h2, ta=True, out_dtype=BF16, name="d_w_gate")
    grads["w_up"] = _mm(dup, h2, ta=True, out_dtype=BF16, name="d_w_up")
    dh2 = _mm(dgt, full["w_gate"], name="d_h2_gate", after=token)
    dx1, grads["g_ffn"] = _mm_fused(
        dup, full["w_up"], lambda acc, dh2, x1, dx2, g: _norm_bwd_fn(x1, acc + dh2, dx2, g), [dh2, x1, dx2], [g_ffn],
        [F32], [_sds((1, D), F32)], full_rows=True, name="d_h2_norm_ffn_bwd")
    dmix = _mm(dx1, full["w_o"], tb=True, name="d_mix")
    grads["w_o"] = _mm(mix, dx1, ta=True, out_dtype=BF16, name="d_w_o")

    token = scatter_start(1)
    half = MLA_HEADS * V_HEAD
    do, dhg, dg_hgo = _stage(_post_bwd_fn, [o_f, o_b, hg, _cols(dmix, half, 1)], [g_hgo_row],
                             [_sds((T, half), F32), _sds((T, half), BF16)], [_sds((1, half), F32)], name="mix_out_bwd",
                             after=token)
    grads["g_hgo"] = dg_hgo
    dhq_f, dhi_f, dhf_f, dlow_f = _gla_bwd(z, lower3[0], st_f, do, None, rev=False,
                                           col_q=cq_blk, col_f=cf_blk, col_v=cv_blk)
    dhq, dhi, dhf_b, dlow_b = _gla_bwd(z, lower3[1], st_b, do, (dhq_f, dhi_f), rev=True,
                                       col_q=cq_blk, col_f=cb_blk, col_v=cv_blk)

    dq, dk, dv = _attention_bwd(q, k, v, att, lse, dmix)
    dq_raw, dkv_raw, dkr, grads["g_qn"], grads["g_kn"] = _stage(
        _mla_b_bwd_fn, [q_raw, kv_raw, kr, cos, sin, dq, dk, dv], [g_qn, g_kn],
        [_sds(q_raw.shape, BF16), _sds(kv_raw.shape, BF16), _sds((T, LANE), BF16)],
        [_sds(g_qn.shape, F32), _sds(g_kn.shape, F32)], name="mla_qk_norm_rope_bwd")
    grads["w_qb"] = _mm(dq_raw, qn, ta=True, out_dtype=BF16, name="d_w_qb")
    grads["w_kvb"] = _mm(kvn, dkv_raw, ta=True, out_dtype=BF16, name="d_w_kvb")
    dqn = _mm(dq_raw, full["w_qb"], name="d_qn")
    dkvn = _mm(dkv_raw, full["w_kvb"], tb=True, name="d_kvn")
    dcq, dckv, grads["g_qa"], grads["g_kva"] = _stage(
        _mla_a_bwd_fn, [_cols(z, 256, 0), _cols(z, 256, 1), dqn, dkvn], [g_qa, g_kva],
        [_sds((T, 256), BF16), _sds((T, 256), BF16)], [_sds(g_qa.shape, F32), _sds(g_kva.shape, F32)],
        name="mla_latent_norm_bwd")
    dz = jnp.concatenate([dcq, dckv, dhq, dhf_f, dhf_b, dhi, dhg, dkr], axis=1)
    grads["w_in"] = _from_z_layout(_mm(dz, h1, ta=True, out_dtype=BF16, name="d_w_in"))
    token = scatter_start(2)
    grad_x, grads["g_mix"] = _mm_fused(
        dz, w_in_zt, lambda acc, x, dx1, g: _norm_bwd_fn(x, acc, dx1, g), [x2d, dx1], [g_mix],
        [F32], [_sds((1, D), F32)], full_rows=True, name="d_h1_norm_mix_bwd", after=token)

    chip = 2 * lax.axis_index("x") + lax.axis_index("y")
    partial_sum = {}
    after = grad_x
    for gi, names in enumerate(scatter_groups):
        shards, recvs = _exchange_wait(scatter_started[gi], after, name=f"scatter_wait_{gi}")
        for n, s, r in zip(names, shards, recvs):
            partial_sum[n] = _sum4(s, r, chip.reshape(1), name="sum_" + n)
        after = partial_sum[names[-1]]
    partial_sums = [partial_sum[n] for n in big]
    sibling_sums = _swap_sibling(partial_sums)

    small = ("g_mix", "g_qa", "g_kva", "g_qn", "g_kn", "g_hgo", "g_ffn", "g_ple")
    small_all = small + ("lb_param",)
    width = -(-max(w_named[n].size for n in small_all) // LANE) * LANE

    def row(a):
        a = a.reshape(1, -1)
        return jnp.pad(a, ((0, 0), (0, width - a.shape[1])))

    dlower = jnp.concatenate([dlow_f.reshape(1, -1), dlow_b.reshape(1, -1)], axis=0)
    pack = jnp.concatenate([row(grads[n]) for n in small] + [row(dlower[0]), row(dlower[1]), row(loss_part)]
                           + [jnp.zeros((5, width), F32)], axis=0)
    red = _allreduce_small(pack)
    loss = red[10, 0]
    dlower_sum = red[8:10, :lb_full.shape[-1]]
    dlb_full = _lower_bounds_bwd(lb_full, dlower_sum)
    fshard = lb_param.shape[-1]
    dlb = lax.dynamic_slice_in_dim(dlb_full, chip * fshard, fshard, axis=2)

    out_g, out_d, out_m, out_v = {}, {}, {}, {}
    for n, mine, theirs in zip(big, partial_sums, sibling_sums):
        out_g[n], out_d[n], out_m[n], out_v[n] = (
            unview(n, t) for t in _adamw(view(n, w_named[n]), mine, theirs, view(n, m_named[n]), view(n, v_named[n]),
                                         name="adamw_" + n))
    g_rows = [red[i:i + 1] for i in range(len(small))] + [row(dlb)]
    g_pack = jnp.concatenate(g_rows + [jnp.zeros((16 - len(g_rows), width), F32)], axis=0)

    def packed(named):
        rows = [row(named[n]) for n in small_all]
        return jnp.concatenate(rows + [jnp.ones((16 - len(rows), width), F32)], axis=0)

    s_g, s_d, s_m, s_v = _adamw(packed(w_named), g_pack, None, packed(m_named), packed(v_named), name="adamw_small")
    for i, n in enumerate(small_all):
        size = w_named[n].size
        for src, dst in ((s_g, out_g), (s_d, out_d), (s_m, out_m), (s_v, out_v)):
            dst[n] = src[i, :size].reshape(w_named[n].shape)

    return (loss, grad_x[None], *[out_g[n] for n in order], *[out_d[n] for n in order],
            *[out_m[n] for n in order], *[out_v[n] for n in order])
```

```python
import functools

import jax
import jax.numpy as jnp
from jax import lax
from jax.experimental import pallas as pl
from jax.experimental.pallas import tpu as pltpu

F32 = jnp.float32
BF16 = jnp.bfloat16
MESH = pl.DeviceIdType.MESH

EPS = 1e-6
ROPE_THETA = 10000.0
MLA_HEADS = 4
QK_NOPE = 128
QK_ROPE = 64
QK_HEAD = QK_NOPE + QK_ROPE
V_HEAD = 128
HG_HEADS = 4
HG_DK = 128
CHUNK = 64
ADAM_LR = 0.001
ADAM_B1 = 0.9
ADAM_B2 = 0.999
ADAM_EPS = 1e-08
ADAM_WD = 0.01
ADAM_STEP = 10

LANE = 128
VMEM_LIMIT = 56 * 1024 * 1024
TOK_TILE = 256
GLA_GROUP = 16
GLA_FWD_HEADS = 4
GLA_BWD_HEADS = 2
ATT_TQ = 1024
ATT_TK = 1024
ATT_CHUNK = 512
LOG2_E = 1.4426950408889634
N_CHIPS = 4
N_DEV = 8


_ANY = pl.BlockSpec(memory_space=pl.ANY)


def _params(dims=None, **kw):
    return pltpu.CompilerParams(dimension_semantics=dims, vmem_limit_bytes=VMEM_LIMIT, **kw)


def _tile_candidates(n, cap):
    out = [d for d in range(LANE, min(n, cap) + 1, LANE) if n % d == 0]
    if n <= cap and n not in out:
        out.append(n)
    return out or [n]


MM_VMEM_BUDGET = 34 * 1024 * 1024
MM_MAX_ROWS = 1536
HBM_BYTES_PER_S = 2.8e12
MXU_FLOPS_PER_S = 8e14
STEP_OVERHEAD_S = 0.35e-6


def _mm_tiles(M, N, K, a_bytes, b_bytes, o_bytes, has_add, full_rows=False):
    best = None
    for tm in _tile_candidates(M, MM_MAX_ROWS):
        for tn in ([N] if full_rows else _tile_candidates(N, N)):
            for tk in _tile_candidates(K, K):
                ni, nj, nk = M // tm, N // tn, K // tk
                vmem = 2 * (tm * tk * a_bytes + tk * tn * b_bytes + tm * tn * o_bytes * (2 if has_add else 1))
                vmem += tm * tn * 4 * (2 if nk > 1 else 1)
                vmem += (tm * tk * 2 if a_bytes > 2 else 0) + (tk * tn * 2 if b_bytes > 2 else 0)
                if vmem > MM_VMEM_BUDGET:
                    continue
                moved = M * K * a_bytes * (nj if nk > 1 else 1) + K * N * b_bytes * (1 if nj == nk == 1 else ni)
                moved += M * N * o_bytes * (2 if has_add else 1)
                t = max(moved / HBM_BYTES_PER_S, 2 * M * N * K / MXU_FLOPS_PER_S) + ni * nj * nk * STEP_OVERHEAD_S
                if best is None or t < best[0]:
                    best = (t, tm, tn, tk)
    assert best is not None, (M, N, K)
    return best[1:]


def _dot_raw(a, b, kind):
    nb = a.ndim - 2
    batch = ((0,), (0,)) if nb else ((), ())
    ca = nb if kind == "tn" else nb + 1
    cb = nb + 1 if kind == "nt" else nb
    return lax.dot_general(a.astype(BF16), b.astype(BF16), (((ca,), (cb,)), batch), preferred_element_type=F32)


@functools.partial(jax.custom_vjp, nondiff_argnums=(2,))
def _bdot(a, b, kind):
    return _dot_raw(a, b, kind)


def _bdot_fwd(a, b, kind):
    return _dot_raw(a, b, kind), (a, b)


def _bdot_bwd(kind, res, g):
    a, b = res
    if kind == "nn":
        da, db = _bdot(g, b, "nt"), _bdot(a, g, "tn")
    elif kind == "nt":
        da, db = _bdot(g, b, "nn"), _bdot(g, a, "tn")
    else:
        da, db = _bdot(b, g, "nt"), _bdot(a, g, "nn")
    return da.astype(a.dtype), db.astype(b.dtype)


_bdot.defvjp(_bdot_fwd, _bdot_bwd)


def _mm(a, b, *, name, ta=False, tb=False, add=None, out_dtype=F32, after=None):
    K, M = a.shape if ta else a.shape[::-1]
    N, Kb = b.shape if tb else b.shape[::-1]
    assert K == Kb, (a.shape, b.shape, ta, tb)
    tm, tn, tk = _mm_tiles(M, N, K, a.dtype.itemsize, b.dtype.itemsize, jnp.dtype(out_dtype).itemsize, add is not None)
    nk = K // tk
    kind = "tn" if ta else ("nt" if tb else "nn")
    assert not (ta and tb)
    a_spec = pl.BlockSpec((tk, tm), lambda i, j, k: (k, i)) if ta else pl.BlockSpec((tm, tk), lambda i, j, k: (i, k))
    b_spec = pl.BlockSpec((tn, tk), lambda i, j, k: (j, k)) if tb else pl.BlockSpec((tk, tn), lambda i, j, k: (k, j))
    o_spec = pl.BlockSpec((tm, tn), lambda i, j, k: (i, j))
    has_add = add is not None

    def body(*refs):
        a_ref, b_ref = refs[0], refs[1]
        add_ref = refs[2] if has_add else None
        o_ref = refs[n_in]
        part = _dot_raw(a_ref[...], b_ref[...], kind)
        if nk == 1:
            if has_add:
                part = part + add_ref[...].astype(F32)
            o_ref[...] = part.astype(o_ref.dtype)
            return
        acc_ref = refs[-1]
        k = pl.program_id(2)

        @pl.when(k == 0)
        def _():
            acc_ref[...] = part

        @pl.when(k > 0)
        def _():
            acc_ref[...] += part

        @pl.when(k == nk - 1)
        def _():
            r = acc_ref[...]
            if has_add:
                r = r + add_ref[...].astype(F32)
            o_ref[...] = r.astype(o_ref.dtype)

    ins = [a, b] + ([add] if has_add else []) + ([after] if after is not None else [])
    in_specs = [a_spec, b_spec] + ([o_spec] if has_add else []) + ([_ANY] if after is not None else [])
    n_in = len(ins)
    return pl.pallas_call(
        body, name=name, grid=(M // tm, N // tn, nk), in_specs=in_specs, out_specs=o_spec,
        out_shape=jax.ShapeDtypeStruct((M, N), out_dtype),
        scratch_shapes=[pltpu.VMEM((tm, tn), F32)] if nk > 1 else [],
        compiler_params=_params(("parallel", "parallel", "arbitrary")),
    )(*ins)


def _mm_fused(a, b, fn, tiles, params, out_dtypes, sums, *, name, ta=False, tb=False, full_rows=False, after=None):
    K, M = a.shape if ta else a.shape[::-1]
    N, Kb = b.shape if tb else b.shape[::-1]
    assert K == Kb and not (ta and tb), (a.shape, b.shape, ta, tb)
    per_elem = sum(t.dtype.itemsize for t in tiles) + sum(jnp.dtype(d).itemsize for d in out_dtypes)
    tm, tn, tk = _mm_tiles(M, N, K, a.dtype.itemsize, b.dtype.itemsize, per_elem, False, full_rows)
    nk = K // tk
    kind = "tn" if ta else ("nt" if tb else "nn")
    a_spec = pl.BlockSpec((tk, tm), lambda i, j, k: (k, i)) if ta else pl.BlockSpec((tm, tk), lambda i, j, k: (i, k))
    b_spec = pl.BlockSpec((tn, tk), lambda i, j, k: (j, k)) if tb else pl.BlockSpec((tk, tn), lambda i, j, k: (k, j))
    o_spec = pl.BlockSpec((tm, tn), lambda i, j, k: (i, j))
    ins = [a, b] + list(tiles) + list(params) + ([after] if after is not None else [])
    in_specs = [a_spec, b_spec] + [o_spec] * len(tiles)
    in_specs += [pl.BlockSpec(p.shape, lambda i, j, k, nd=p.ndim: (0,) * nd) for p in params]
    in_specs += [_ANY] if after is not None else []
    n_in, n_t, n_p, n_o = len(ins), len(tiles), len(params), len(out_dtypes)

    def body(*refs):
        outs, sum_refs = refs[n_in:n_in + n_o], refs[n_in + n_o:n_in + n_o + len(sums)]

        def finish(r):
            res = fn(r, *[t[...] for t in refs[2:2 + n_t + n_p]])
            for o_ref, v in zip(outs, res[:n_o]):
                o_ref[...] = v.astype(o_ref.dtype)
            first = jnp.logical_and(pl.program_id(0) == 0, pl.program_id(1) == 0)
            for s_ref, v in zip(sum_refs, res[n_o:]):
                @pl.when(first)
                def _(s_ref=s_ref, v=v):
                    s_ref[...] = v

                @pl.when(jnp.logical_not(first))
                def _(s_ref=s_ref, v=v):
                    s_ref[...] += v

        part = _dot_raw(refs[0][...], refs[1][...], kind)
        if nk == 1:
            finish(part)
            return
        acc_ref = refs[-1]
        k = pl.program_id(2)

        @pl.when(k == 0)
        def _():
            acc_ref[...] = part

        @pl.when(k > 0)
        def _():
            acc_ref[...] += part

        @pl.when(k == nk - 1)
        def _():
            finish(acc_ref[...])

    out_shape = [_sds((M, N), d) for d in out_dtypes] + list(sums)
    out_specs = [o_spec] * n_o + [pl.BlockSpec(s.shape, lambda i, j, k, nd=len(s.shape): (0,) * nd) for s in sums]
    order = ("arbitrary",) * 3 if sums else ("parallel", "parallel", "arbitrary")
    return pl.pallas_call(
        body, name=name, grid=(M // tm, N // tn, nk), in_specs=in_specs, out_specs=out_specs, out_shape=out_shape,
        scratch_shapes=[pltpu.VMEM((tm, tn), F32)] if nk > 1 else [], compiler_params=_params(order),
    )(*ins)


def _cols(arr, width, block):
    return (arr, width, block)


def _stage(fn, tiles, params, out_tiles, out_sums, *, name, tile=TOK_TILE, after=None):
    def tok_spec(shape, width=None, block=0):
        if len(shape) == 2:
            w = shape[1] if width is None else width
            return pl.BlockSpec((tile, w), lambda i: (i, block))
        return pl.BlockSpec((shape[0], tile, shape[2]), lambda i: (0, i, 0))

    arrays, in_specs = [], []
    for t in tiles:
        if isinstance(t, tuple):
            arr, width, block = t
            arrays.append(arr)
            in_specs.append(tok_spec(arr.shape, width, block))
        else:
            arrays.append(t)
            in_specs.append(tok_spec(t.shape))
    n_tok = arrays[0].shape[0] if arrays[0].ndim == 2 else arrays[0].shape[1]
    for p in params:
        arrays.append(p)
        in_specs.append(pl.BlockSpec(p.shape, lambda i, nd=p.ndim: (0,) * nd))
    out_shape = list(out_tiles) + list(out_sums)
    out_specs = [tok_spec(o.shape) for o in out_tiles]
    out_specs += [pl.BlockSpec(o.shape, lambda i, nd=len(o.shape): (0,) * nd) for o in out_sums]
    n_fn, n_ot = len(arrays), len(out_tiles)
    if after is not None:
        arrays.append(after)
        in_specs.append(_ANY)
    n_in = len(arrays)

    def body(*refs):
        res = fn(*[r[...] for r in refs[:n_fn]])
        if not isinstance(res, (tuple, list)):
            res = (res,)
        outs = refs[n_in:]
        for o_ref, r in zip(outs[:n_ot], res[:n_ot]):
            o_ref[...] = r.astype(o_ref.dtype)
        i = pl.program_id(0)
        for o_ref, r in zip(outs[n_ot:], res[n_ot:]):
            @pl.when(i == 0)
            def _(o_ref=o_ref, r=r):
                o_ref[...] = r.astype(o_ref.dtype)

            @pl.when(i > 0)
            def _(o_ref=o_ref, r=r):
                o_ref[...] += r.astype(o_ref.dtype)

    res = pl.pallas_call(
        body, name=name, grid=(n_tok // tile,), in_specs=in_specs, out_specs=out_specs, out_shape=out_shape,
        compiler_params=_params(("arbitrary",)),
    )(*arrays)
    return res


def _sds(shape, dtype):
    return jax.ShapeDtypeStruct(tuple(shape), dtype)


def _sigmoid(x):
    return 1.0 / (1.0 + jnp.exp(-x))


def _sigmoid_t(x):
    return 0.5 * jnp.tanh(0.5 * x) + 0.5


def _rms(x, g):
    return x * lax.rsqrt(jnp.mean(x * x, axis=-1, keepdims=True) + EPS) * g


def _norm_bwd_fn(x, dh, dres, g):
    _, vjp = jax.vjp(_rms, x, g)
    dx, dg = vjp(dh)
    return dx + dres, dg


def _mla_a_fn(cq, ckv, g_qa, g_kva):
    return _rms(cq, g_qa), _rms(ckv, g_kva)


def _mla_a_bwd_fn(cq, ckv, dqn, dkvn, g_qa, g_kva):
    _, vjp = jax.vjp(_mla_a_fn, cq, ckv, g_qa, g_kva)
    return vjp((dqn, dkvn))


def _rope(t, cos, sin):
    t1, t2 = t[:, :QK_ROPE // 2], t[:, QK_ROPE // 2:]
    return jnp.concatenate([t1 * cos - t2 * sin, t1 * sin + t2 * cos], axis=-1)


def _mla_b_fn(q_raw, kv_raw, kr, cos, sin, g_qn, g_kn):
    krope = kr[:, :QK_ROPE]
    qs, ks, vs = [], [], []
    for h in range(MLA_HEADS):
        qh = _rms(q_raw[:, h * QK_HEAD:(h + 1) * QK_HEAD], g_qn)
        kvh = kv_raw[:, h * (QK_NOPE + V_HEAD):(h + 1) * (QK_NOPE + V_HEAD)]
        kh = _rms(jnp.concatenate([kvh[:, :QK_NOPE], krope], axis=-1), g_kn)
        qs.append(jnp.concatenate([qh[:, :QK_NOPE], _rope(qh[:, QK_NOPE:], cos, sin)], axis=-1))
        ks.append(jnp.concatenate([kh[:, :QK_NOPE], _rope(kh[:, QK_NOPE:], cos, sin)], axis=-1))
        vs.append(kvh[:, QK_NOPE:])
    return jnp.stack(qs), jnp.stack(ks), jnp.stack(vs)


def _mla_b_bwd_fn(q_raw, kv_raw, kr, cos, sin, dq, dk, dv, g_qn, g_kn):
    _, vjp = jax.vjp(lambda a, b, c, d, e: _mla_b_fn(a, b, c, cos, sin, d, e), q_raw, kv_raw, kr, g_qn, g_kn)
    return vjp((dq, dk, dv))


def _post_fn(a, o_f, o_b, hg, g_hgo):
    o = o_f + o_b
    parts = [a]
    for h in range(HG_HEADS):
        s = slice(h * HG_DK, (h + 1) * HG_DK)
        gate = hg[:, s]
        parts.append(_rms(o[:, s], g_hgo[:, s]) * (gate * _sigmoid(gate)))
    return jnp.concatenate(parts, axis=-1)


def _post_bwd_fn(o_f, o_b, hg, dr, g_hgo):
    def f(o, hg, g):
        return _post_fn(jnp.zeros_like(o), o, jnp.zeros_like(o), hg, g)[:, o.shape[1]:]
    _, vjp = jax.vjp(f, o_f + o_b, hg, g_hgo)
    return vjp(dr)


def _swiglu_fn(gt, up):
    return gt * _sigmoid(gt) * up


def _resid_norm_fn(acc, x, g):
    x_new = acc + x
    return x_new, _rms(x_new, g)


def _swiglu_bwd_fn(gt, up, dact):
    _, vjp = jax.vjp(_swiglu_fn, gt, up)
    return vjp(dact)


def _ple_loss_fn(x2, pg, pp, target):
    gate = _sigmoid(pg)
    err = x2 + gate * pp - target
    dx3 = err * (1.0 / err.shape[-1])
    loss = 0.5 * jnp.sum(jnp.mean(err * err, axis=-1, keepdims=True), axis=0, keepdims=True)
    return dx3, dx3 * pp * gate * (1.0 - gate), dx3 * gate, loss


def _attention_fwd(q, k, v):
    H, T, D = q.shape
    DV = v.shape[-1]
    tq, ck = min(ATT_TQ, T), min(ATT_CHUNK, T)
    c2 = (D ** -0.5) * LOG2_E

    def body(q_ref, k_ref, v_ref, o_ref, lse_ref):
        q_i = q_ref[0]

        def chunk(c, carry):
            m, l, acc = carry
            rows = pl.ds(pl.multiple_of(c * ck, ck), ck)
            s = _dot_raw(q_i, k_ref[0, rows, :], "nt")
            m_new = jnp.maximum(m, jnp.max(s, axis=-1, keepdims=True))
            p = jnp.exp2((s - m_new) * c2)
            alpha = jnp.exp2((m - m_new) * c2)
            l = l * alpha + jnp.sum(p, axis=-1, keepdims=True)
            acc = acc * alpha + _dot_raw(p, v_ref[0, rows, :], "nn")
            return m_new, l, acc

        init = (jnp.full((tq, 1), -jnp.inf, F32), jnp.zeros((tq, 1), F32), jnp.zeros((tq, DV), F32))
        m, l, acc = lax.fori_loop(0, T // ck, chunk, init, unroll=True)
        o_ref[...] = acc / l
        lse_ref[0] = m * c2 + jnp.log2(l)

    return pl.pallas_call(
        body, name="attention_fwd", grid=(H, T // tq),
        in_specs=[pl.BlockSpec((1, tq, D), lambda h, i: (h, i, 0)),
                  pl.BlockSpec((1, T, D), lambda h, i: (h, 0, 0)),
                  pl.BlockSpec((1, T, DV), lambda h, i: (h, 0, 0))],
        out_specs=[pl.BlockSpec((tq, DV), lambda h, i: (i, h)),
                   pl.BlockSpec((1, tq, 1), lambda h, i: (h, i, 0))],
        out_shape=[_sds((T, H * DV), F32), _sds((H, T, 1), F32)],
        compiler_params=_params(("parallel", "parallel")),
    )(q, k, v)


def _attention_bwd(q, k, v, o, lse2, dmix):
    H, T, D = q.shape
    DV = v.shape[-1]
    tk, cq = min(ATT_TK, T), min(ATT_CHUNK, T)
    scale = D ** -0.5
    c2 = scale * LOG2_E

    def body(q_ref, k_ref, v_ref, o_ref, lse_ref, do_ref, dq_ref, dk_ref, dv_ref, delta_ref):
        j = pl.program_id(1)

        @pl.when(j == 0)
        def _():
            delta = lax.dot_general(jnp.ones((8, DV), F32), do_ref[...] * o_ref[...], (((1,), (1,)), ((), ())),
                                    precision=lax.Precision.HIGHEST, preferred_element_type=F32)
            for i in range(T // cq):
                delta_ref[i] = delta[:, i * cq:(i + 1) * cq]
            dq_ref[0] = jnp.zeros((T, D), F32)

        k_j, v_j = k_ref[0], v_ref[0]
        dk_ref[0] = jnp.zeros((tk, D), F32)
        dv_ref[0] = jnp.zeros((tk, DV), F32)

        def chunk(c, carry):
            rows = pl.ds(pl.multiple_of(c * cq, cq), cq)
            q_c = q_ref[0, rows, :]
            do_c = do_ref[rows, :].astype(BF16)
            st = _dot_raw(k_j, q_c, "nt")
            pt = jnp.exp2(st * c2 - lse_ref[0, c])
            dv_ref[0] += _dot_raw(pt, do_c, "nn")
            dpt = _dot_raw(v_j, do_c, "nt")
            dst = pt * (dpt - delta_ref[c, 0:1, :]) * scale
            dk_ref[0] += _dot_raw(dst, q_c, "nn")
            dq_ref[0, rows, :] += _dot_raw(dst, k_j, "tn")
            return carry

        lax.fori_loop(0, T // cq, chunk, 0, unroll=True)

    return pl.pallas_call(
        body, name="attention_bwd", grid=(H, T // tk),
        in_specs=[pl.BlockSpec((1, T, D), lambda h, j: (h, 0, 0)),
                  pl.BlockSpec((1, tk, D), lambda h, j: (h, j, 0)),
                  pl.BlockSpec((1, tk, DV), lambda h, j: (h, j, 0)),
                  pl.BlockSpec((T, DV), lambda h, j: (0, h)),
                  pl.BlockSpec((1, T // cq, 1, cq), lambda h, j: (h, 0, 0, 0)),
                  pl.BlockSpec((T, DV), lambda h, j: (0, h))],
        out_specs=[pl.BlockSpec((1, T, D), lambda h, j: (h, 0, 0)),
                   pl.BlockSpec((1, tk, D), lambda h, j: (h, j, 0)),
                   pl.BlockSpec((1, tk, DV), lambda h, j: (h, j, 0))],
        out_shape=[_sds((H, T, D), F32), _sds((H, T, D), F32), _sds((H, T, DV), F32)],
        scratch_shapes=[pltpu.VMEM((T // cq, 8, cq), F32)],
        compiler_params=_params(("parallel", "arbitrary")),
    )(q, k, v, o, lse2.reshape(H, T // cq, 1, cq), dmix)


def _split3_dot(ones, x, kind):
    hi = x.astype(BF16)
    rest = x - hi.astype(F32)
    mid = rest.astype(BF16)
    lo = (rest - mid.astype(F32)).astype(BF16)
    return (_dot_raw(ones, hi, kind) + _dot_raw(ones, mid, kind)) + _dot_raw(ones, lo, kind)


@jax.custom_vjp
def _running_sum(x, tri):
    return _split3_dot(tri, x, "nn")


def _running_sum_fwd(x, tri):
    return _split3_dot(tri, x, "nn"), tri


def _running_sum_bwd(tri, g):
    return _split3_dot(tri, g, "tn"), jnp.zeros_like(tri)


_running_sum.defvjp(_running_sum_fwd, _running_sum_bwd)


def _gla_block(hq, hf, hi, lower, st_in, *, rev, dot):
    rows, dk = hq.shape
    G, C = rows // CHUNK, CHUNK
    q = hq * _sigmoid_t(hq)
    f = lower + (1.0 - lower) * _sigmoid_t(hf)
    k = 1.0 - f
    logf = jnp.log2(f)
    q3, k3, v3, lf3 = (t.reshape(G, C, dk) for t in (q, k, hi, logf))
    r = lax.broadcasted_iota(jnp.int32, (C, C), 0)
    c = lax.broadcasted_iota(jnp.int32, (C, C), 1)
    tri = ((r <= c) if rev else (r >= c)).astype(F32)
    b = _running_sum(lf3, jnp.broadcast_to(tri, (G, C, C)))
    tpos = lax.broadcasted_iota(jnp.int32, (1, C, 1), 1)
    first_half = (tpos >= C // 2) if rev else (tpos <= C // 2 - 1)
    b_mid = jnp.sum(jnp.where(first_half, lf3, 0.0), axis=1, keepdims=True)
    b_last = jnp.sum(lf3, axis=1, keepdims=True)
    a = dot(q3 * jnp.exp2(b - b_mid), k3 * jnp.exp2(b_mid - b), "nt") * tri
    o_intra = dot(a, v3, "nn")
    kv_t = dot(v3, k3 * jnp.exp2(b_last - b), "tn")
    decay = jnp.exp2(b_last)
    qd = q3 * jnp.exp2(b)
    st = st_in
    o_inter = [None] * G
    for g in (reversed(range(G)) if rev else range(G)):
        o_inter[g] = dot(qd[g], st, "nt")
        st = st * decay[g] + kv_t[g]
    o = o_intra.reshape(rows, dk) + jnp.concatenate(o_inter, axis=0)
    return o, st


def _gla_fwd(z, lower3, *, rev, col_q, col_f, col_v, hp):
    T = z.shape[0]
    rows = min(GLA_GROUP * CHUNK, T)
    nb = T // rows
    wide = hp * HG_DK
    blk = (lambda n: nb - 1 - n) if rev else (lambda n: n)

    def body(hq_ref, hf_ref, hi_ref, low_ref, o_ref, st_out_ref, st_ref):
        @pl.when(pl.program_id(1) == 0)
        def _():
            st_ref[...] = jnp.zeros_like(st_ref)

        st_in = [st_ref[i] for i in range(hp)]
        heads = []
        for i in range(hp):
            cols = slice(i * HG_DK, (i + 1) * HG_DK)
            heads.append(_gla_block(hq_ref[:, cols], hf_ref[:, cols], hi_ref[:, cols], low_ref[i], st_in[i], rev=rev,
                                    dot=_dot_raw))
        for i, (o, st) in enumerate(heads):
            st_out_ref[i, 0] = st_in[i]
            o_ref[:, i * HG_DK:(i + 1) * HG_DK] = o
            st_ref[i] = st

    def zspec(col):
        return pl.BlockSpec((rows, wide), lambda h, n: (blk(n), col // wide + h))

    return pl.pallas_call(
        body, name="gla_fwd_rev" if rev else "gla_fwd", grid=(HG_HEADS // hp, nb),
        in_specs=[zspec(col_q), zspec(col_f), zspec(col_v), pl.BlockSpec((hp, 1, HG_DK), lambda h, n: (h, 0, 0))],
        out_specs=[pl.BlockSpec((rows, wide), lambda h, n: (blk(n), h)),
                   pl.BlockSpec((hp, 1, HG_DK, HG_DK), lambda h, n: (h, blk(n), 0, 0))],
        out_shape=[_sds((T, HG_HEADS * HG_DK), F32), _sds((HG_HEADS, nb, HG_DK, HG_DK), F32)],
        scratch_shapes=[pltpu.VMEM((hp, HG_DK, HG_DK), F32)],
        compiler_params=_params(("parallel", "arbitrary")),
    )(z, z, z, lower3)


def _gla_bwd(z, lower3, states, do, prev, *, rev, col_q, col_f, col_v, hp):
    T = z.shape[0]
    rows = min(GLA_GROUP * CHUNK, T)
    nb = T // rows
    wide = hp * HG_DK
    blk = (lambda n: n) if rev else (lambda n: nb - 1 - n)
    has_prev = prev is not None
    fn = functools.partial(_gla_block, rev=rev, dot=_bdot)

    def body(*refs):
        hq_ref, hf_ref, hi_ref, low_ref, st_ref, do_ref = refs[:6]
        rest = refs[6:]
        if has_prev:
            pq_ref, pi_ref = rest[:2]
            rest = rest[2:]
        dhq_ref, dhi_ref, dhf_ref, dlow_ref, dst_ref = rest
        n = pl.program_id(1)

        @pl.when(n == 0)
        def _():
            dst_ref[...] = jnp.zeros_like(dst_ref)

        dst_in = [dst_ref[i] for i in range(hp)]
        heads = []
        for i in range(hp):
            cols = slice(i * HG_DK, (i + 1) * HG_DK)
            _, vjp = jax.vjp(fn, hq_ref[:, cols], hf_ref[:, cols], hi_ref[:, cols], low_ref[i], st_ref[i, 0])
            dhq, dhf, dhi, dlow, dst = vjp((do_ref[:, cols], dst_in[i]))
            if has_prev:
                dhq = dhq + pq_ref[:, cols]
                dhi = dhi + pi_ref[:, cols]
            heads.append((dhq, dhf, dhi, dlow, dst))
        for i, (dhq, dhf, dhi, dlow, dst) in enumerate(heads):
            cols = slice(i * HG_DK, (i + 1) * HG_DK)
            dst_ref[i] = dst
            dhq_ref[:, cols] = dhq.astype(dhq_ref.dtype)
            dhi_ref[:, cols] = dhi.astype(dhi_ref.dtype)
            dhf_ref[:, cols] = dhf.astype(dhf_ref.dtype)

        @pl.when(n == 0)
        def _():
            for i in range(hp):
                dlow_ref[i] = heads[i][3]

        @pl.when(n > 0)
        def _():
            for i in range(hp):
                dlow_ref[i] += heads[i][3]

    def zspec(col):
        return pl.BlockSpec((rows, wide), lambda h, n: (blk(n), col // wide + h))

    hspec = pl.BlockSpec((rows, wide), lambda h, n: (blk(n), h))
    in_specs = [zspec(col_q), zspec(col_f), zspec(col_v), pl.BlockSpec((hp, 1, HG_DK), lambda h, n: (h, 0, 0)),
                pl.BlockSpec((hp, 1, HG_DK, HG_DK), lambda h, n: (h, blk(n), 0, 0)), hspec]
    ins = [z, z, z, lower3, states, do]
    if has_prev:
        in_specs += [hspec, hspec]
        ins += list(prev)
    full_wide = HG_HEADS * HG_DK
    acc_dtype = BF16 if has_prev else F32
    return pl.pallas_call(
        body, name="gla_bwd_rev" if rev else "gla_bwd", grid=(HG_HEADS // hp, nb),
        in_specs=in_specs,
        out_specs=[hspec, hspec, hspec, pl.BlockSpec((hp, 1, HG_DK), lambda h, n: (h, 0, 0))],
        out_shape=[_sds((T, full_wide), acc_dtype), _sds((T, full_wide), acc_dtype), _sds((T, full_wide), BF16),
                   _sds((HG_HEADS, 1, HG_DK), F32)],
        scratch_shapes=[pltpu.VMEM((hp, HG_DK, HG_DK), F32)],
        compiler_params=_params(("parallel", "arbitrary")),
    )(*ins)


def _lower_fn(lb):
    e = jnp.exp(lb - jnp.max(lb, axis=0, keepdims=True))
    return (e / jnp.sum(e, axis=0, keepdims=True))[0]


def _lower_bounds(lb):
    def body(lb_ref, o_ref):
        o_ref[...] = _lower_fn(lb_ref[...])
    return pl.pallas_call(body, name="lower_bounds", out_shape=_sds(lb.shape[1:], F32))(lb)


def _lower_bounds_bwd(lb, dlower):
    def body(lb_ref, d_ref, o_ref):
        _, vjp = jax.vjp(_lower_fn, lb_ref[...])
        o_ref[...] = vjp(d_ref[...])[0]
    return pl.pallas_call(body, name="lower_bounds_bwd", out_shape=_sds(lb.shape, F32))(lb, dlower)


def _row_tile(r, cap=1024):
    best = None
    for t in range(16, min(r, cap) + 1, 16):
        if r % t == 0:
            best = t
    return best if best is not None else r


def _sum4(shards, recv, chip, *, name):
    _, R, C = shards.shape
    tr = _row_tile(R)

    def body(chip_ref, o_ref, r_ref, out_ref):
        out_ref[...] = ((o_ref[0].astype(F32) + r_ref[0].astype(F32)) + r_ref[1].astype(F32)) + r_ref[2].astype(F32)

    grid_spec = pltpu.PrefetchScalarGridSpec(
        num_scalar_prefetch=1, grid=(R // tr,),
        in_specs=[pl.BlockSpec((1, tr, C), lambda i, chip_ref: (chip_ref[0], i, 0)),
                  pl.BlockSpec((3, tr, C), lambda i, chip_ref: (0, i, 0))],
        out_specs=pl.BlockSpec((tr, C), lambda i, chip_ref: (i, 0)))
    return pl.pallas_call(
        body, name=name, grid_spec=grid_spec, out_shape=_sds((R, C), F32), compiler_params=_params(("parallel",)),
    )(chip, shards, recv)


def _adamw_math(w, g, m, v):
    m = ADAM_B1 * m + (1.0 - ADAM_B1) * g
    v = ADAM_B2 * v + (1.0 - ADAM_B2) * (g * g)
    m_hat = m / (1.0 - ADAM_B1 ** ADAM_STEP)
    v_hat = v / (1.0 - ADAM_B2 ** ADAM_STEP)
    delta = -ADAM_LR * (m_hat / (jnp.sqrt(v_hat) + ADAM_EPS) + ADAM_WD * w)
    return delta, m, v


def _adamw(w, g_a, g_b, m, v, *, name):
    R, C = w.shape
    tr = _row_tile(R)
    two = g_b is not None

    def body(*refs):
        w_ref, ga_ref = refs[0], refs[1]
        rest = refs[2:]
        g = ga_ref[...]
        if two:
            g = g + rest[0][...]
            rest = rest[1:]
        m_ref, v_ref, g_out, d_out, m_out, v_out = rest
        delta, m_new, v_new = _adamw_math(w_ref[...], g, m_ref[...], v_ref[...])
        g_out[...] = g
        d_out[...] = delta
        m_out[...] = m_new
        v_out[...] = v_new

    spec = pl.BlockSpec((tr, C), lambda i: (i, 0))
    ins = [w, g_a] + ([g_b] if two else []) + [m, v]
    return pl.pallas_call(
        body, name=name, grid=(R // tr,), in_specs=[spec] * len(ins), out_specs=[spec] * 4,
        out_shape=[_sds((R, C), F32)] * 4, compiler_params=_params(("parallel",)),
    )(*ins)


def _chip_peers():
    x, y, c = lax.axis_index("x"), lax.axis_index("y"), lax.axis_index("c")
    return (x, y, c), 2 * x + y, [(1 - x, y), (x, 1 - y), (1 - x, 1 - y)]


_HBM = pl.BlockSpec(memory_space=pltpu.HBM)
_SEM = pl.BlockSpec(memory_space=pltpu.SEMAPHORE)
_EFFECT = pltpu.SideEffectType.DATAFLOW_SIDE_EFFECTING


def _chip_copies(srcs, lands, sems, gather):
    (x, y, c), me, chips = _chip_peers()
    copies = []
    for t, (src, land) in enumerate(zip(srcs, lands)):
        for k, (px, py) in enumerate(chips):
            copies.append(pltpu.make_async_remote_copy(
                src if gather else src.at[2 * px + py], land.at[me] if gather else land.at[k],
                sems[0].at[3 * t + k], sems[1].at[3 * t + k], device_id=(px, py, c), device_id_type=MESH))
        if gather:
            copies.append(pltpu.make_async_copy(src, land.at[me], sems[2].at[t]))
    return copies


def _exchange_start(srcs, *, gather, name, after=None):
    n = len(srcs)
    n_sem = 3 if gather else 2
    n_in = 2 * n + (after is not None)
    lands = [_sds(((N_CHIPS,) + s.shape) if gather else ((3,) + s.shape[1:]), s.dtype) for s in srcs]

    def body(*refs):
        for cp in _chip_copies(refs[:n], refs[n:2 * n], refs[n_in:n_in + n_sem], gather):
            cp.start()
        token = refs[-1]
        token[...] = jnp.zeros_like(token)

    sem_shapes = [pltpu.SemaphoreType.DMA((3 * n,)), pltpu.SemaphoreType.DMA((3 * n,))]
    sem_shapes += [pltpu.SemaphoreType.DMA((n,))] if gather else []
    thru = [pltpu.HBM(s.shape, s.dtype) for s in srcs] + [pltpu.HBM(l.shape, l.dtype) for l in lands]
    res = pl.pallas_call(
        body, name=name, in_specs=[_HBM] * (2 * n) + [_ANY] * (after is not None),
        out_specs=[_SEM] * n_sem + [_HBM] * (2 * n) + [pl.BlockSpec(memory_space=pltpu.VMEM)],
        out_shape=sem_shapes + thru + [_sds((8, LANE), F32)], input_output_aliases={i: n_sem + i for i in range(2 * n)},
        compiler_params=pltpu.CompilerParams(has_side_effects=_EFFECT),
    )(*[pltpu.with_memory_space_constraint(s, pltpu.HBM) for s in srcs],
      *[pltpu.with_memory_space_constraint(lax.empty(l.shape, l.dtype), pltpu.HBM) for l in lands],
      *([after] if after is not None else []))
    return (res[:n_sem], res[n_sem:n_sem + n], res[n_sem + n:n_sem + 2 * n]), res[-1]


def _exchange_wait(started, after, *, name):
    sems, srcs, lands = started
    n, n_sem = len(srcs), len(sems)

    def body(*refs):
        for cp in _chip_copies(refs[:n], refs[n:2 * n], refs[2 * n:2 * n + n_sem], n_sem == 3):
            cp.wait()

    res = pl.pallas_call(
        body, name=name, in_specs=[_HBM] * (2 * n) + [_SEM] * n_sem + [_ANY], out_specs=[_HBM] * (2 * n),
        out_shape=[pltpu.HBM(a.shape, a.dtype) for a in list(srcs) + list(lands)],
        input_output_aliases={i: i for i in range(2 * n)},
        compiler_params=pltpu.CompilerParams(has_side_effects=_EFFECT),
    )(*srcs, *lands, *sems, after)
    return res[:n], res[n:]


def _swap_sibling(parts):
    n = len(parts)

    def body(*refs):
        ins, outs = refs[:n], refs[n:2 * n]
        send_sems, recv_sems = refs[2 * n:]
        x, y, c = lax.axis_index("x"), lax.axis_index("y"), lax.axis_index("c")
        copies = []
        for t in range(n):
            cp = pltpu.make_async_remote_copy(ins[t], outs[t], send_sems.at[t], recv_sems.at[t],
                                              device_id=(x, y, 1 - c), device_id_type=MESH)
            cp.start()
            copies.append(cp)
        for cp in copies:
            cp.wait()

    return pl.pallas_call(
        body, name="swap_sibling", in_specs=[_ANY] * n, out_specs=[_ANY] * n,
        out_shape=[_sds(p.shape, p.dtype) for p in parts],
        scratch_shapes=[pltpu.SemaphoreType.DMA((n,)), pltpu.SemaphoreType.DMA((n,))],
        compiler_params=_params(),
    )(*parts)


def _allreduce_small(pack):
    R, C = pack.shape

    def body(in_ref, out_ref, slots, send_sems, recv_sems):
        x, y, c = lax.axis_index("x"), lax.axis_index("y"), lax.axis_index("c")
        me = 4 * x + 2 * y + c
        slots[me] = in_ref[...]
        copies = []
        for k in range(1, N_DEV):
            peer = (x ^ ((k >> 2) & 1), y ^ ((k >> 1) & 1), c ^ (k & 1))
            cp = pltpu.make_async_remote_copy(in_ref, slots.at[me], send_sems.at[k - 1], recv_sems.at[k - 1],
                                              device_id=peer, device_id_type=MESH)
            cp.start()
            copies.append(cp)
        for cp in copies:
            cp.wait()
        acc = slots[0]
        for d in range(1, N_DEV):
            acc = acc + slots[d]
        out_ref[...] = acc

    return pl.pallas_call(
        body, name="allreduce_small", out_shape=_sds((R, C), F32),
        in_specs=[pl.BlockSpec(memory_space=pltpu.VMEM)], out_specs=pl.BlockSpec(memory_space=pltpu.VMEM),
        scratch_shapes=[pltpu.VMEM((N_DEV, R, C), F32), pltpu.SemaphoreType.DMA((N_DEV - 1,)),
                        pltpu.SemaphoreType.DMA((N_DEV - 1,))],
        compiler_params=_params(),
    )(pack)


_Z_CQ, _Z_CKV, _Z_HQ, _Z_HFF, _Z_HFB, _Z_HI, _Z_HG, _Z_KR, _Z_END = 0, 256, 512, 1024, 1536, 2048, 2560, 3072, 3200


def _to_z_layout(wt):
    pad = jnp.zeros((_Z_END - _Z_KR - QK_ROPE, wt.shape[1]), wt.dtype)
    return jnp.concatenate([wt[:512], wt[512 + QK_ROPE:], wt[512:512 + QK_ROPE], pad], axis=0)


def _from_z_layout(wt):
    return jnp.concatenate([wt[:512], wt[_Z_KR:_Z_KR + QK_ROPE], wt[512:_Z_KR]], axis=0)


def _col_shards_to_full(g):
    return jnp.transpose(g, (1, 0, 2)).reshape(g.shape[1], -1)


def _full_to_col_shards(w):
    r, c = w.shape
    return jnp.transpose(w.reshape(r, N_CHIPS, c // N_CHIPS), (1, 0, 2))


def _full_to_row_shards(w):
    r, c = w.shape
    return w.reshape(N_CHIPS, r // N_CHIPS, c)


def kernel(x, p, positions, g_mix, w_in, g_qa, g_kva, w_qb, w_kvb, g_qn, g_kn, lb_param, g_hgo, w_o, g_ffn, w_gate, w_up, w_down, g_ple, w_ple_gate, w_ple_proj, loss_target, m_g_mix, m_w_in, m_g_qa, m_g_kva, m_w_qb, m_w_kvb, m_g_qn, m_g_kn, m_lb_param, m_g_hgo, m_w_o, m_g_ffn, m_w_gate, m_w_up, m_w_down, m_g_ple, m_w_ple_gate, m_w_ple_proj, v_g_mix, v_w_in, v_g_qa, v_g_kva, v_w_qb, v_w_kvb, v_g_qn, v_g_kn, v_lb_param, v_g_hgo, v_w_o, v_g_ffn, v_w_gate, v_w_up, v_w_down, v_g_ple, v_w_ple_gate, v_w_ple_proj):
    w_named = dict(g_mix=g_mix, w_in=w_in, g_qa=g_qa, g_kva=g_kva, w_qb=w_qb, w_kvb=w_kvb, g_qn=g_qn, g_kn=g_kn,
                   lb_param=lb_param, g_hgo=g_hgo, w_o=w_o, g_ffn=g_ffn, w_gate=w_gate, w_up=w_up, w_down=w_down,
                   g_ple=g_ple, w_ple_gate=w_ple_gate, w_ple_proj=w_ple_proj)
    m_named = dict(g_mix=m_g_mix, w_in=m_w_in, g_qa=m_g_qa, g_kva=m_g_kva, w_qb=m_w_qb, w_kvb=m_w_kvb, g_qn=m_g_qn,
                   g_kn=m_g_kn, lb_param=m_lb_param, g_hgo=m_g_hgo, w_o=m_w_o, g_ffn=m_g_ffn, w_gate=m_w_gate,
                   w_up=m_w_up, w_down=m_w_down, g_ple=m_g_ple, w_ple_gate=m_w_ple_gate, w_ple_proj=m_w_ple_proj)
    v_named = dict(g_mix=v_g_mix, w_in=v_w_in, g_qa=v_g_qa, g_kva=v_g_kva, w_qb=v_w_qb, w_kvb=v_w_kvb, g_qn=v_g_qn,
                   g_kn=v_g_kn, lb_param=v_lb_param, g_hgo=v_g_hgo, w_o=v_w_o, g_ffn=v_g_ffn, w_gate=v_w_gate,
                   w_up=v_w_up, w_down=v_w_down, g_ple=v_g_ple, w_ple_gate=v_w_ple_gate, w_ple_proj=v_w_ple_proj)
    order = list(w_named)
    transposed = ("w_in", "w_qb", "w_gate", "w_up")
    col_sharded = ("w_kvb", "w_ple_proj")
    row_sharded = ("w_o", "w_down", "w_ple_gate")
    big = transposed + col_sharded + row_sharded

    def view(n, a):
        return jnp.transpose(a[0]) if n in transposed else a[0]

    def unview(n, a):
        return (jnp.transpose(a) if n in transposed else a)[None]

    def to_shards(n, g):
        return _full_to_col_shards(g) if n in col_sharded else _full_to_row_shards(g)

    x2d, p2d, tgt = x[0], p[0, 0], loss_target[0]
    T, D = x2d.shape

    lb_flat = lb_param.reshape(-1, lb_param.shape[-1])
    gather_groups = (("w_in",), ("w_qb", "w_kvb"), ("w_o", "w_gate", "w_up"), ("w_down", "w_ple_gate", "w_ple_proj"))
    gather_started = []

    def gather_start(gi, after):
        srcs = [view(n, w_named[n]).astype(BF16) for n in gather_groups[gi]] + ([lb_flat] if gi == 0 else [])
        started, token = _exchange_start(srcs, gather=True, name=f"gather_start_{gi}", after=after)
        gather_started.append(started)
        return token

    full = {}

    def gather_wait(gi, after):
        _, got = _exchange_wait(gather_started[gi], after, name=f"gather_wait_{gi}")
        for n, g in zip(gather_groups[gi], got):
            full[n] = _col_shards_to_full(g) if n in col_sharded else g.reshape(-1, g.shape[-1])
        return got

    inv_freq = ROPE_THETA ** (-jnp.arange(0, QK_ROPE, 2, dtype=F32) / QK_ROPE)
    ang = positions[0].astype(F32)[:, None] * inv_freq
    cos, sin = jnp.cos(ang), jnp.sin(ang)
    g_hgo_row = g_hgo.reshape(1, -1)

    token = gather_start(0, None)
    h1 = _stage(_rms, [x2d], [g_mix], [_sds((T, D), BF16)], [], name="norm_mix", after=token)[0]
    got = gather_wait(0, h1)
    token = got[0]
    for gi in range(1, len(gather_groups)):
        token = gather_start(gi, token)
    lb_full = _col_shards_to_full(got[-1]).reshape(lb_param.shape[0], lb_param.shape[1], -1)
    w_in_zt = _to_z_layout(full["w_in"])
    z = _mm(h1, w_in_zt, tb=True, name="in_proj", after=token)
    qn, kvn = _stage(_mla_a_fn, [_cols(z, 256, 0), _cols(z, 256, 1)], [g_qa, g_kva],
                     [_sds((T, 256), BF16), _sds((T, 256), BF16)], [], name="mla_latent_norm")
    gather_wait(1, qn)
    q_raw = _mm(qn, full["w_qb"], tb=True, name="q_up")
    kv_raw = _mm(kvn, full["w_kvb"], name="kv_up")
    kr = _cols(z, LANE, _Z_KR // LANE)
    q, k, v = _stage(_mla_b_fn, [q_raw, kv_raw, kr, cos, sin], [g_qn, g_kn],
                     [_sds((MLA_HEADS, T, QK_HEAD), BF16), _sds((MLA_HEADS, T, QK_HEAD), BF16),
                      _sds((MLA_HEADS, T, V_HEAD), BF16)], [], name="mla_qk_norm_rope")
    att, lse = _attention_fwd(q, k, v)

    lower = _lower_bounds(lb_full)
    lower3 = lower.reshape(2, HG_HEADS, 1, HG_DK)
    o_f, st_f = _gla_fwd(z, lower3[0], rev=False, col_q=_Z_HQ, col_f=_Z_HFF, col_v=_Z_HI, hp=GLA_FWD_HEADS)
    o_b, st_b = _gla_fwd(z, lower3[1], rev=True, col_q=_Z_HQ, col_f=_Z_HFB, col_v=_Z_HI, hp=GLA_FWD_HEADS)
    hg = _cols(z, 512, _Z_HG // 512)
    mix = _stage(_post_fn, [att, o_f, o_b, hg], [g_hgo_row], [_sds((T, att.shape[1] + o_f.shape[1]), BF16)], [],
                 name="mix_out")[0]
    gather_wait(2, mix)
    x1, h2 = _mm_fused(mix, full["w_o"], _resid_norm_fn, [x2d], [g_ffn], [F32, BF16], [], full_rows=True,
                       name="out_proj")
    up = _mm(h2, full["w_up"], tb=True, name="ffn_up")
    gt, act = _mm_fused(h2, full["w_gate"], lambda acc, up: (acc, _swiglu_fn(acc, up)), [up], [], [F32, BF16], [],
                        tb=True, name="ffn_gate")
    gather_wait(3, act)
    x2, h3 = _mm_fused(act, full["w_down"], _resid_norm_fn, [x1], [g_ple], [F32, BF16], [], full_rows=True,
                       name="ffn_down")
    pp = _mm(p2d, full["w_ple_proj"], name="ple_proj")
    dx3, dpg, dpp, loss_part = _mm_fused(
        h3, full["w_ple_gate"], lambda acc, pp, x2, tgt: _ple_loss_fn(x2, acc, pp, tgt), [pp, x2, tgt], [],
        [F32, BF16, BF16], [_sds((1, 1), F32)], full_rows=True, name="ple_gate_loss")

    grads = {}
    scatter_groups = (("w_ple_proj", "w_ple_gate", "w_down"), ("w_gate", "w_up", "w_o"), ("w_qb", "w_kvb", "w_in"))
    scatter_started = []

    def scatter_start(gi):
        srcs = [to_shards(n, grads[n]) for n in scatter_groups[gi]]
        started, token = _exchange_start(srcs, gather=False, name=f"scatter_start_{gi}")
        scatter_started.append(started)
        return token

    grads["w_ple_proj"] = _mm(p2d, dpp, ta=True, out_dtype=BF16, name="d_w_ple_proj")
    grads["w_ple_gate"] = _mm(h3, dpg, ta=True, out_dtype=BF16, name="d_w_ple_gate")
    dx2, grads["g_ple"] = _mm_fused(
        dpg, full["w_ple_gate"], lambda acc, x2, dx3, g: _norm_bwd_fn(x2, acc, dx3, g), [x2, dx3], [g_ple],
        [F32], [_sds((1, D), F32)], tb=True, full_rows=True, name="d_h3_norm_ple_bwd")
    dgt, dup = _mm_fused(dx2, full["w_down"], lambda acc, gt, up: _swiglu_bwd_fn(gt, up, acc), [gt, up], [],
                         [BF16, BF16], [], tb=True, name="d_act_swiglu_bwd")
    grads["w_down"] = _mm(act, dx2, ta=True, out_dtype=BF16, name="d_w_down")
    token = scatter_start(0)
    grads["w_gate"] = _mm(dgt, h2, ta=True, out_dtype=BF16, name="d_w_gate")
    grads["w_up"] = _mm(dup, h2, ta=True, out_dtype=BF16, name="d_w_up")
    dh2 = _mm(dgt, full["w_gate"], name="d_h2_gate", after=token)
    dx1, grads["g_ffn"] = _mm_fused(
        dup, full["w_up"], lambda acc, dh2, x1, dx2, g: _norm_bwd_fn(x1, acc + dh2, dx2, g), [dh2, x1, dx2], [g_ffn],
        [F32], [_sds((1, D), F32)], full_rows=True, name="d_h2_norm_ffn_bwd")
    dmix = _mm(dx1, full["w_o"], tb=True, name="d_mix")
    grads["w_o"] = _mm(mix, dx1, ta=True, out_dtype=BF16, name="d_w_o")

    token = scatter_start(1)
    half = MLA_HEADS * V_HEAD
    do, dhg, dg_hgo = _stage(_post_bwd_fn, [o_f, o_b, hg, _cols(dmix, half, 1)], [g_hgo_row],
                             [_sds((T, half), F32), _sds((T, half), BF16)], [_sds((1, half), F32)], name="mix_out_bwd",
                             after=token)
    grads["g_hgo"] = dg_hgo
    dhq_f, dhi_f, dhf_f, dlow_f = _gla_bwd(z, lower3[0], st_f, do, None, rev=False,
                                           col_q=_Z_HQ, col_f=_Z_HFF, col_v=_Z_HI, hp=GLA_BWD_HEADS)
    dhq, dhi, dhf_b, dlow_b = _gla_bwd(z, lower3[1], st_b, do, (dhq_f, dhi_f), rev=True,
                                       col_q=_Z_HQ, col_f=_Z_HFB, col_v=_Z_HI, hp=GLA_BWD_HEADS)

    dq, dk, dv = _attention_bwd(q, k, v, att, lse, dmix)
    dq_raw, dkv_raw, dkr, grads["g_qn"], grads["g_kn"] = _stage(
        _mla_b_bwd_fn, [q_raw, kv_raw, kr, cos, sin, dq, dk, dv], [g_qn, g_kn],
        [_sds(q_raw.shape, BF16), _sds(kv_raw.shape, BF16), _sds((T, LANE), BF16)],
        [_sds(g_qn.shape, F32), _sds(g_kn.shape, F32)], name="mla_qk_norm_rope_bwd")
    grads["w_qb"] = _mm(dq_raw, qn, ta=True, out_dtype=BF16, name="d_w_qb")
    grads["w_kvb"] = _mm(kvn, dkv_raw, ta=True, out_dtype=BF16, name="d_w_kvb")
    dqn = _mm(dq_raw, full["w_qb"], name="d_qn")
    dkvn = _mm(dkv_raw, full["w_kvb"], tb=True, name="d_kvn")
    dcq, dckv, grads["g_qa"], grads["g_kva"] = _stage(
        _mla_a_bwd_fn, [_cols(z, 256, 0), _cols(z, 256, 1), dqn, dkvn], [g_qa, g_kva],
        [_sds((T, 256), BF16), _sds((T, 256), BF16)], [_sds(g_qa.shape, F32), _sds(g_kva.shape, F32)],
        name="mla_latent_norm_bwd")
    dz = jnp.concatenate([dcq, dckv, dhq, dhf_f, dhf_b, dhi, dhg, dkr], axis=1)
    grads["w_in"] = _from_z_layout(_mm(dz, h1, ta=True, out_dtype=BF16, name="d_w_in"))
    token = scatter_start(2)
    grad_x, grads["g_mix"] = _mm_fused(
        dz, w_in_zt, lambda acc, x, dx1, g: _norm_bwd_fn(x, acc, dx1, g), [x2d, dx1], [g_mix],
        [F32], [_sds((1, D), F32)], full_rows=True, name="d_h1_norm_mix_bwd", after=token)

    chip = 2 * lax.axis_index("x") + lax.axis_index("y")
    partial_sum = {}
    after = grad_x
    for gi, names in enumerate(scatter_groups):
        shards, recvs = _exchange_wait(scatter_started[gi], after, name=f"scatter_wait_{gi}")
        for n, s, r in zip(names, shards, recvs):
            partial_sum[n] = _sum4(s, r, chip.reshape(1), name="sum_" + n)
        after = partial_sum[names[-1]]
    partial_sums = [partial_sum[n] for n in big]
    sibling_sums = _swap_sibling(partial_sums)

    small = ("g_mix", "g_qa", "g_kva", "g_qn", "g_kn", "g_hgo", "g_ffn", "g_ple")
    small_all = small + ("lb_param",)
    width = -(-max(w_named[n].size for n in small_all) // LANE) * LANE

    def row(a):
        a = a.reshape(1, -1)
        return jnp.pad(a, ((0, 0), (0, width - a.shape[1])))

    dlower = jnp.concatenate([dlow_f.reshape(1, -1), dlow_b.reshape(1, -1)], axis=0)
    pack = jnp.concatenate([row(grads[n]) for n in small] + [row(dlower[0]), row(dlower[1]), row(loss_part)]
                           + [jnp.zeros((5, width), F32)], axis=0)
    red = _allreduce_small(pack)
    loss = red[10, 0]
    dlower_sum = red[8:10, :lb_full.shape[-1]]
    dlb_full = _lower_bounds_bwd(lb_full, dlower_sum)
    fshard = lb_param.shape[-1]
    dlb = lax.dynamic_slice_in_dim(dlb_full, chip * fshard, fshard, axis=2)

    out_g, out_d, out_m, out_v = {}, {}, {}, {}
    for n, mine, theirs in zip(big, partial_sums, sibling_sums):
        out_g[n], out_d[n], out_m[n], out_v[n] = (
            unview(n, t) for t in _adamw(view(n, w_named[n]), mine, theirs, view(n, m_named[n]), view(n, v_named[n]),
                                         name="adamw_" + n))
    g_rows = [red[i:i + 1] for i in range(len(small))] + [row(dlb)]
    g_pack = jnp.concatenate(g_rows + [jnp.zeros((16 - len(g_rows), width), F32)], axis=0)

    def packed(named):
        rows = [row(named[n]) for n in small_all]
        return jnp.concatenate(rows + [jnp.ones((16 - len(rows), width), F32)], axis=0)

    s_g, s_d, s_m, s_v = _adamw(packed(w_named), g_pack, None, packed(m_named), packed(v_named), name="adamw_small")
    for i, n in enumerate(small_all):
        size = w_named[n].size
        for src, dst in ((s_g, out_g), (s_d, out_d), (s_m, out_m), (s_v, out_v)):
            dst[n] = src[i, :size].reshape(w_named[n].shape)

    return (loss, grad_x[None], *[out_g[n] for n in order], *[out_d[n] for n in order],
            *[out_m[n] for n in order], *[out_v[n] for n in order])
```

```python
import functools

import jax
import jax.numpy as jnp
from jax import lax
from jax.experimental import pallas as pl
from jax.experimental.pallas import tpu as pltpu

F32 = jnp.float32
BF16 = jnp.bfloat16
MESH = pl.DeviceIdType.MESH

EPS = 1e-6
ROPE_THETA = 10000.0
MLA_HEADS = 4
QK_NOPE = 128
QK_ROPE = 64
QK_HEAD = QK_NOPE + QK_ROPE
V_HEAD = 128
HG_HEADS = 4
HG_DK = 128
CHUNK = 64
ADAM_LR = 0.001
ADAM_B1 = 0.9
ADAM_B2 = 0.999
ADAM_EPS = 1e-08
ADAM_WD = 0.01
ADAM_STEP = 10

LANE = 128
VMEM_LIMIT = 56 * 1024 * 1024
TOK_TILE = 256
GLA_GROUP = 16
GLA_FWD_HEADS = 4
GLA_BWD_HEADS = 2
ATT_TQ = 1024
ATT_TK = 1024
ATT_CHUNK = 512
LOG2_E = 1.4426950408889634
N_CHIPS = 4
N_DEV = 8


_ANY = pl.BlockSpec(memory_space=pl.ANY)


def _params(dims=None, **kw):
    return pltpu.CompilerParams(dimension_semantics=dims, vmem_limit_bytes=VMEM_LIMIT, **kw)


def _tile_candidates(n, cap):
    out = [d for d in range(LANE, min(n, cap) + 1, LANE) if n % d == 0]
    if n <= cap and n not in out:
        out.append(n)
    return out or [n]


MM_VMEM_BUDGET = 34 * 1024 * 1024
MM_MAX_ROWS = 1536
HBM_BYTES_PER_S = 2.8e12
MXU_FLOPS_PER_S = 8e14
STEP_OVERHEAD_S = 0.35e-6


def _mm_tiles(M, N, K, a_bytes, b_bytes, o_bytes, has_add, full_rows=False):
    best = None
    for tm in _tile_candidates(M, MM_MAX_ROWS):
        for tn in ([N] if full_rows else _tile_candidates(N, N)):
            for tk in _tile_candidates(K, K):
                ni, nj, nk = M // tm, N // tn, K // tk
                vmem = 2 * (tm * tk * a_bytes + tk * tn * b_bytes + tm * tn * o_bytes * (2 if has_add else 1))
                vmem += tm * tn * 4 * (2 if nk > 1 else 1)
                vmem += (tm * tk * 2 if a_bytes > 2 else 0) + (tk * tn * 2 if b_bytes > 2 else 0)
                if vmem > MM_VMEM_BUDGET:
                    continue
                moved = M * K * a_bytes * (nj if nk > 1 else 1) + K * N * b_bytes * (1 if nj == nk == 1 else ni)
                moved += M * N * o_bytes * (2 if has_add else 1)
                t = max(moved / HBM_BYTES_PER_S, 2 * M * N * K / MXU_FLOPS_PER_S) + ni * nj * nk * STEP_OVERHEAD_S
                if best is None or t < best[0]:
                    best = (t, tm, tn, tk)
    assert best is not None, (M, N, K)
    return best[1:]


def _dot_raw(a, b, kind):
    nb = a.ndim - 2
    batch = ((0,), (0,)) if nb else ((), ())
    ca = nb if kind == "tn" else nb + 1
    cb = nb + 1 if kind == "nt" else nb
    return lax.dot_general(a.astype(BF16), b.astype(BF16), (((ca,), (cb,)), batch), preferred_element_type=F32)


@functools.partial(jax.custom_vjp, nondiff_argnums=(2,))
def _bdot(a, b, kind):
    return _dot_raw(a, b, kind)


def _bdot_fwd(a, b, kind):
    return _dot_raw(a, b, kind), (a, b)


def _bdot_bwd(kind, res, g):
    a, b = res
    if kind == "nn":
        da, db = _bdot(g, b, "nt"), _bdot(a, g, "tn")
    elif kind == "nt":
        da, db = _bdot(g, b, "nn"), _bdot(g, a, "tn")
    else:
        da, db = _bdot(b, g, "nt"), _bdot(a, g, "nn")
    return da.astype(a.dtype), db.astype(b.dtype)


_bdot.defvjp(_bdot_fwd, _bdot_bwd)


def _mm(a, b, *, name, ta=False, tb=False, add=None, out_dtype=F32, after=None):
    K, M = a.shape if ta else a.shape[::-1]
    N, Kb = b.shape if tb else b.shape[::-1]
    assert K == Kb, (a.shape, b.shape, ta, tb)
    tm, tn, tk = _mm_tiles(M, N, K, a.dtype.itemsize, b.dtype.itemsize, jnp.dtype(out_dtype).itemsize, add is not None)
    nk = K // tk
    kind = "tn" if ta else ("nt" if tb else "nn")
    assert not (ta and tb)
    a_spec = pl.BlockSpec((tk, tm), lambda i, j, k: (k, i)) if ta else pl.BlockSpec((tm, tk), lambda i, j, k: (i, k))
    b_spec = pl.BlockSpec((tn, tk), lambda i, j, k: (j, k)) if tb else pl.BlockSpec((tk, tn), lambda i, j, k: (k, j))
    o_spec = pl.BlockSpec((tm, tn), lambda i, j, k: (i, j))
    has_add = add is not None

    def body(*refs):
        a_ref, b_ref = refs[0], refs[1]
        add_ref = refs[2] if has_add else None
        o_ref = refs[n_in]
        part = _dot_raw(a_ref[...], b_ref[...], kind)
        if nk == 1:
            if has_add:
                part = part + add_ref[...].astype(F32)
            o_ref[...] = part.astype(o_ref.dtype)
            return
        acc_ref = refs[-1]
        k = pl.program_id(2)

        @pl.when(k == 0)
        def _():
            acc_ref[...] = part

        @pl.when(k > 0)
        def _():
            acc_ref[...] += part

        @pl.when(k == nk - 1)
        def _():
            r = acc_ref[...]
            if has_add:
                r = r + add_ref[...].astype(F32)
            o_ref[...] = r.astype(o_ref.dtype)

    ins = [a, b] + ([add] if has_add else []) + ([after] if after is not None else [])
    in_specs = [a_spec, b_spec] + ([o_spec] if has_add else []) + ([_ANY] if after is not None else [])
    n_in = len(ins)
    return pl.pallas_call(
        body, name=name, grid=(M // tm, N // tn, nk), in_specs=in_specs, out_specs=o_spec,
        out_shape=jax.ShapeDtypeStruct((M, N), out_dtype),
        scratch_shapes=[pltpu.VMEM((tm, tn), F32)] if nk > 1 else [],
        compiler_params=_params(("parallel", "parallel", "arbitrary")),
    )(*ins)


def _mm_fused(a, b, fn, tiles, params, out_dtypes, sums, *, name, ta=False, tb=False, full_rows=False, after=None):
    K, M = a.shape if ta else a.shape[::-1]
    N, Kb = b.shape if tb else b.shape[::-1]
    assert K == Kb and not (ta and tb), (a.shape, b.shape, ta, tb)
    per_elem = sum(t.dtype.itemsize for t in tiles) + sum(jnp.dtype(d).itemsize for d in out_dtypes)
    tm, tn, tk = _mm_tiles(M, N, K, a.dtype.itemsize, b.dtype.itemsize, per_elem, False, full_rows)
    nk = K // tk
    kind = "tn" if ta else ("nt" if tb else "nn")
    a_spec = pl.BlockSpec((tk, tm), lambda i, j, k: (k, i)) if ta else pl.BlockSpec((tm, tk), lambda i, j, k: (i, k))
    b_spec = pl.BlockSpec((tn, tk), lambda i, j, k: (j, k)) if tb else pl.BlockSpec((tk, tn), lambda i, j, k: (k, j))
    o_spec = pl.BlockSpec((tm, tn), lambda i, j, k: (i, j))
    ins = [a, b] + list(tiles) + list(params) + ([after] if after is not None else [])
    in_specs = [a_spec, b_spec] + [o_spec] * len(tiles)
    in_specs += [pl.BlockSpec(p.shape, lambda i, j, k, nd=p.ndim: (0,) * nd) for p in params]
    in_specs += [_ANY] if after is not None else []
    n_in, n_t, n_p, n_o = len(ins), len(tiles), len(params), len(out_dtypes)

    def body(*refs):
        outs, sum_refs = refs[n_in:n_in + n_o], refs[n_in + n_o:n_in + n_o + len(sums)]

        def finish(r):
            res = fn(r, *[t[...] for t in refs[2:2 + n_t + n_p]])
            for o_ref, v in zip(outs, res[:n_o]):
                o_ref[...] = v.astype(o_ref.dtype)
            first = jnp.logical_and(pl.program_id(0) == 0, pl.program_id(1) == 0)
            for s_ref, v in zip(sum_refs, res[n_o:]):
                @pl.when(first)
                def _(s_ref=s_ref, v=v):
                    s_ref[...] = v

                @pl.when(jnp.logical_not(first))
                def _(s_ref=s_ref, v=v):
                    s_ref[...] += v

        part = _dot_raw(refs[0][...], refs[1][...], kind)
        if nk == 1:
            finish(part)
            return
        acc_ref = refs[-1]
        k = pl.program_id(2)

        @pl.when(k == 0)
        def _():
            acc_ref[...] = part

        @pl.when(k > 0)
        def _():
            acc_ref[...] += part

        @pl.when(k == nk - 1)
        def _():
            finish(acc_ref[...])

    out_shape = [_sds((M, N), d) for d in out_dtypes] + list(sums)
    out_specs = [o_spec] * n_o + [pl.BlockSpec(s.shape, lambda i, j, k, nd=len(s.shape): (0,) * nd) for s in sums]
    order = ("arbitrary",) * 3 if sums else ("parallel", "parallel", "arbitrary")
    return pl.pallas_call(
        body, name=name, grid=(M // tm, N // tn, nk), in_specs=in_specs, out_specs=out_specs, out_shape=out_shape,
        scratch_shapes=[pltpu.VMEM((tm, tn), F32)] if nk > 1 else [], compiler_params=_params(order),
    )(*ins)


def _cols(arr, width, block):
    return (arr, width, block)


def _stage(fn, tiles, params, out_tiles, out_sums, *, name, tile=TOK_TILE, after=None):
    def tok_spec(shape, width=None, block=0):
        if len(shape) == 2:
            w = shape[1] if width is None else width
            return pl.BlockSpec((tile, w), lambda i: (i, block))
        return pl.BlockSpec((shape[0], tile, shape[2]), lambda i: (0, i, 0))

    arrays, in_specs = [], []
    for t in tiles:
        if isinstance(t, tuple):
            arr, width, block = t
            arrays.append(arr)
            in_specs.append(tok_spec(arr.shape, width, block))
        else:
            arrays.append(t)
            in_specs.append(tok_spec(t.shape))
    n_tok = arrays[0].shape[0] if arrays[0].ndim == 2 else arrays[0].shape[1]
    for p in params:
        arrays.append(p)
        in_specs.append(pl.BlockSpec(p.shape, lambda i, nd=p.ndim: (0,) * nd))
    out_shape = list(out_tiles) + list(out_sums)
    out_specs = [tok_spec(o.shape) for o in out_tiles]
    out_specs += [pl.BlockSpec(o.shape, lambda i, nd=len(o.shape): (0,) * nd) for o in out_sums]
    n_fn, n_ot = len(arrays), len(out_tiles)
    if after is not None:
        arrays.append(after)
        in_specs.append(_ANY)
    n_in = len(arrays)

    def body(*refs):
        res = fn(*[r[...] for r in refs[:n_fn]])
        if not isinstance(res, (tuple, list)):
            res = (res,)
        outs = refs[n_in:]
        for o_ref, r in zip(outs[:n_ot], res[:n_ot]):
            o_ref[...] = r.astype(o_ref.dtype)
        i = pl.program_id(0)
        for o_ref, r in zip(outs[n_ot:], res[n_ot:]):
            @pl.when(i == 0)
            def _(o_ref=o_ref, r=r):
                o_ref[...] = r.astype(o_ref.dtype)

            @pl.when(i > 0)
            def _(o_ref=o_ref, r=r):
                o_ref[...] += r.astype(o_ref.dtype)

    res = pl.pallas_call(
        body, name=name, grid=(n_tok // tile,), in_specs=in_specs, out_specs=out_specs, out_shape=out_shape,
        compiler_params=_params(("arbitrary",)),
    )(*arrays)
    return res


def _sds(shape, dtype):
    return jax.ShapeDtypeStruct(tuple(shape), dtype)


def _sigmoid(x):
    return 1.0 / (1.0 + jnp.exp(-x))


def _sigmoid_t(x):
    return 0.5 * jnp.tanh(0.5 * x) + 0.5


def _rms(x, g):
    return x * lax.rsqrt(jnp.mean(x * x, axis=-1, keepdims=True) + EPS) * g


def _norm_bwd_fn(x, dh, dres, g):
    _, vjp = jax.vjp(_rms, x, g)
    dx, dg = vjp(dh)
    return dx + dres, dg


def _mla_a_fn(cq, ckv, g_qa, g_kva):
    return _rms(cq, g_qa), _rms(ckv, g_kva)


def _mla_a_bwd_fn(cq, ckv, dqn, dkvn, g_qa, g_kva):
    _, vjp = jax.vjp(_mla_a_fn, cq, ckv, g_qa, g_kva)
    return vjp((dqn, dkvn))


def _rope(t, cos, sin):
    t1, t2 = t[:, :QK_ROPE // 2], t[:, QK_ROPE // 2:]
    return jnp.concatenate([t1 * cos - t2 * sin, t1 * sin + t2 * cos], axis=-1)


def _mla_b_fn(q_raw, kv_raw, kr, cos, sin, g_qn, g_kn):
    krope = kr[:, :QK_ROPE]
    qs, ks, vs = [], [], []
    for h in range(MLA_HEADS):
        qh = _rms(q_raw[:, h * QK_HEAD:(h + 1) * QK_HEAD], g_qn)
        kvh = kv_raw[:, h * (QK_NOPE + V_HEAD):(h + 1) * (QK_NOPE + V_HEAD)]
        kh = _rms(jnp.concatenate([kvh[:, :QK_NOPE], krope], axis=-1), g_kn)
        qs.append(jnp.concatenate([qh[:, :QK_NOPE], _rope(qh[:, QK_NOPE:], cos, sin)], axis=-1))
        ks.append(jnp.concatenate([kh[:, :QK_NOPE], _rope(kh[:, QK_NOPE:], cos, sin)], axis=-1))
        vs.append(kvh[:, QK_NOPE:])
    return jnp.stack(qs), jnp.stack(ks), jnp.stack(vs)


def _mla_b_bwd_fn(q_raw, kv_raw, kr, cos, sin, dq, dk, dv, g_qn, g_kn):
    _, vjp = jax.vjp(lambda a, b, c, d, e: _mla_b_fn(a, b, c, cos, sin, d, e), q_raw, kv_raw, kr, g_qn, g_kn)
    return vjp((dq, dk, dv))


def _post_fn(a, o_f, o_b, hg, g_hgo):
    o = o_f + o_b
    parts = [a]
    for h in range(HG_HEADS):
        s = slice(h * HG_DK, (h + 1) * HG_DK)
        gate = hg[:, s]
        parts.append(_rms(o[:, s], g_hgo[:, s]) * (gate * _sigmoid(gate)))
    return jnp.concatenate(parts, axis=-1)


def _post_bwd_fn(o_f, o_b, hg, dr, g_hgo):
    def f(o, hg, g):
        return _post_fn(jnp.zeros_like(o), o, jnp.zeros_like(o), hg, g)[:, o.shape[1]:]
    _, vjp = jax.vjp(f, o_f + o_b, hg, g_hgo)
    return vjp(dr)


def _swiglu_fn(gt, up):
    return gt * _sigmoid(gt) * up


def _resid_norm_fn(acc, x, g):
    x_new = acc + x
    return x_new, _rms(x_new, g)


def _swiglu_bwd_fn(gt, up, dact):
    _, vjp = jax.vjp(_swiglu_fn, gt, up)
    return vjp(dact)


def _ple_loss_fn(x2, pg, pp, target):
    gate = _sigmoid(pg)
    err = x2 + gate * pp - target
    dx3 = err * (1.0 / err.shape[-1])
    loss = 0.5 * jnp.sum(jnp.mean(err * err, axis=-1, keepdims=True), axis=0, keepdims=True)
    return dx3, dx3 * pp * gate * (1.0 - gate), dx3 * gate, loss


def _attention_fwd(q, k, v):
    H, T, D = q.shape
    DV = v.shape[-1]
    tq, ck = min(ATT_TQ, T), min(ATT_CHUNK, T)
    c2 = (D ** -0.5) * LOG2_E

    def body(q_ref, k_ref, v_ref, o_ref, lse_ref):
        q_i = q_ref[0]

        def chunk(c, carry):
            m, l, acc = carry
            rows = pl.ds(pl.multiple_of(c * ck, ck), ck)
            s = _dot_raw(q_i, k_ref[0, rows, :], "nt")
            m_new = jnp.maximum(m, jnp.max(s, axis=-1, keepdims=True))
            p = jnp.exp2((s - m_new) * c2)
            alpha = jnp.exp2((m - m_new) * c2)
            l = l * alpha + jnp.sum(p, axis=-1, keepdims=True)
            acc = acc * alpha + _dot_raw(p, v_ref[0, rows, :], "nn")
            return m_new, l, acc

        init = (jnp.full((tq, 1), -jnp.inf, F32), jnp.zeros((tq, 1), F32), jnp.zeros((tq, DV), F32))
        m, l, acc = lax.fori_loop(0, T // ck, chunk, init, unroll=True)
        o_ref[...] = acc / l
        lse_ref[0] = m * c2 + jnp.log2(l)

    return pl.pallas_call(
        body, name="attention_fwd", grid=(H, T // tq),
        in_specs=[pl.BlockSpec((1, tq, D), lambda h, i: (h, i, 0)),
                  pl.BlockSpec((1, T, D), lambda h, i: (h, 0, 0)),
                  pl.BlockSpec((1, T, DV), lambda h, i: (h, 0, 0))],
        out_specs=[pl.BlockSpec((tq, DV), lambda h, i: (i, h)),
                   pl.BlockSpec((1, tq, 1), lambda h, i: (h, i, 0))],
        out_shape=[_sds((T, H * DV), F32), _sds((H, T, 1), F32)],
        compiler_params=_params(("parallel", "parallel")),
    )(q, k, v)


def _attention_bwd(q, k, v, o, lse2, dmix):
    H, T, D = q.shape
    DV = v.shape[-1]
    tk, cq = min(ATT_TK, T), min(ATT_CHUNK, T)
    scale = D ** -0.5
    c2 = scale * LOG2_E

    def body(q_ref, k_ref, v_ref, o_ref, lse_ref, do_ref, dq_ref, dk_ref, dv_ref, delta_ref):
        j = pl.program_id(1)

        @pl.when(j == 0)
        def _():
            delta = lax.dot_general(jnp.ones((8, DV), F32), do_ref[...] * o_ref[...], (((1,), (1,)), ((), ())),
                                    precision=lax.Precision.HIGHEST, preferred_element_type=F32)
            for i in range(T // cq):
                delta_ref[i] = delta[:, i * cq:(i + 1) * cq]
            dq_ref[0] = jnp.zeros((T, D), F32)

        k_j, v_j = k_ref[0], v_ref[0]
        dk_ref[0] = jnp.zeros((tk, D), F32)
        dv_ref[0] = jnp.zeros((tk, DV), F32)

        def chunk(c, carry):
            rows = pl.ds(pl.multiple_of(c * cq, cq), cq)
            q_c = q_ref[0, rows, :]
            do_c = do_ref[rows, :].astype(BF16)
            st = _dot_raw(k_j, q_c, "nt")
            pt = jnp.exp2(st * c2 - lse_ref[0, c])
            dv_ref[0] += _dot_raw(pt, do_c, "nn")
            dpt = _dot_raw(v_j, do_c, "nt")
            dst = pt * (dpt - delta_ref[c, 0:1, :]) * scale
            dk_ref[0] += _dot_raw(dst, q_c, "nn")
            dq_ref[0, rows, :] += _dot_raw(dst, k_j, "tn")
            return carry

        lax.fori_loop(0, T // cq, chunk, 0, unroll=True)

    return pl.pallas_call(
        body, name="attention_bwd", grid=(H, T // tk),
        in_specs=[pl.BlockSpec((1, T, D), lambda h, j: (h, 0, 0)),
                  pl.BlockSpec((1, tk, D), lambda h, j: (h, j, 0)),
                  pl.BlockSpec((1, tk, DV), lambda h, j: (h, j, 0)),
                  pl.BlockSpec((T, DV), lambda h, j: (0, h)),
                  pl.BlockSpec((1, T // cq, 1, cq), lambda h, j: (h, 0, 0, 0)),
                  pl.BlockSpec((T, DV), lambda h, j: (0, h))],
        out_specs=[pl.BlockSpec((1, T, D), lambda h, j: (h, 0, 0)),
                   pl.BlockSpec((1, tk, D), lambda h, j: (h, j, 0)),
                   pl.BlockSpec((1, tk, DV), lambda h, j: (h, j, 0))],
        out_shape=[_sds((H, T, D), F32), _sds((H, T, D), F32), _sds((H, T, DV), F32)],
        scratch_shapes=[pltpu.VMEM((T // cq, 8, cq), F32)],
        compiler_params=_params(("parallel", "arbitrary")),
    )(q, k, v, o, lse2.reshape(H, T // cq, 1, cq), dmix)


def _split3_dot(ones, x, kind):
    hi = x.astype(BF16)
    rest = x - hi.astype(F32)
    mid = rest.astype(BF16)
    lo = (rest - mid.astype(F32)).astype(BF16)
    return (_dot_raw(ones, hi, kind) + _dot_raw(ones, mid, kind)) + _dot_raw(ones, lo, kind)


@jax.custom_vjp
def _running_sum(x, tri):
    return _split3_dot(tri, x, "nn")


def _running_sum_fwd(x, tri):
    return _split3_dot(tri, x, "nn"), tri


def _running_sum_bwd(tri, g):
    return _split3_dot(tri, g, "tn"), jnp.zeros_like(tri)


_running_sum.defvjp(_running_sum_fwd, _running_sum_bwd)


def _gla_block(hq, hf, hi, lower, st_in, *, rev, dot):
    rows, dk = hq.shape
    G, C = rows // CHUNK, CHUNK
    q = hq * _sigmoid_t(hq)
    f = lower + (1.0 - lower) * _sigmoid_t(hf)
    k = 1.0 - f
    logf = jnp.log2(f)
    q3, k3, v3, lf3 = (t.reshape(G, C, dk) for t in (q, k, hi, logf))
    r = lax.broadcasted_iota(jnp.int32, (C, C), 0)
    c = lax.broadcasted_iota(jnp.int32, (C, C), 1)
    tri = ((r <= c) if rev else (r >= c)).astype(F32)
    b = _running_sum(lf3, jnp.broadcast_to(tri, (G, C, C)))
    tpos = lax.broadcasted_iota(jnp.int32, (1, C, 1), 1)
    first_half = (tpos >= C // 2) if rev else (tpos <= C // 2 - 1)
    b_mid = jnp.sum(jnp.where(first_half, lf3, 0.0), axis=1, keepdims=True)
    b_last = jnp.sum(lf3, axis=1, keepdims=True)
    a = dot(q3 * jnp.exp2(b - b_mid), k3 * jnp.exp2(b_mid - b), "nt") * tri
    o_intra = dot(a, v3, "nn")
    kv_t = dot(v3, k3 * jnp.exp2(b_last - b), "tn")
    decay = jnp.exp2(b_last)
    qd = q3 * jnp.exp2(b)
    st = st_in
    o_inter = [None] * G
    for g in (reversed(range(G)) if rev else range(G)):
        o_inter[g] = dot(qd[g], st, "nt")
        st = st * decay[g] + kv_t[g]
    o = o_intra.reshape(rows, dk) + jnp.concatenate(o_inter, axis=0)
    return o, st


def _gla_fwd(z, lower3, *, rev, col_q, col_f, col_v, hp):
    T = z.shape[0]
    rows = min(GLA_GROUP * CHUNK, T)
    nb = T // rows
    wide = hp * HG_DK
    blk = (lambda n: nb - 1 - n) if rev else (lambda n: n)

    def body(hq_ref, hf_ref, hi_ref, low_ref, o_ref, st_out_ref, st_ref):
        @pl.when(pl.program_id(1) == 0)
        def _():
            st_ref[...] = jnp.zeros_like(st_ref)

        st_in = [st_ref[i] for i in range(hp)]
        heads = []
        for i in range(hp):
            cols = slice(i * HG_DK, (i + 1) * HG_DK)
            heads.append(_gla_block(hq_ref[:, cols], hf_ref[:, cols], hi_ref[:, cols], low_ref[i], st_in[i], rev=rev,
                                    dot=_dot_raw))
        for i, (o, st) in enumerate(heads):
            st_out_ref[i, 0] = st_in[i]
            o_ref[:, i * HG_DK:(i + 1) * HG_DK] = o
            st_ref[i] = st

    def zspec(col):
        return pl.BlockSpec((rows, wide), lambda h, n: (blk(n), col // wide + h))

    return pl.pallas_call(
        body, name="gla_fwd_rev" if rev else "gla_fwd", grid=(HG_HEADS // hp, nb),
        in_specs=[zspec(col_q), zspec(col_f), zspec(col_v), pl.BlockSpec((hp, 1, HG_DK), lambda h, n: (h, 0, 0))],
        out_specs=[pl.BlockSpec((rows, wide), lambda h, n: (blk(n), h)),
                   pl.BlockSpec((hp, 1, HG_DK, HG_DK), lambda h, n: (h, blk(n), 0, 0))],
        out_shape=[_sds((T, HG_HEADS * HG_DK), F32), _sds((HG_HEADS, nb, HG_DK, HG_DK), F32)],
        scratch_shapes=[pltpu.VMEM((hp, HG_DK, HG_DK), F32)],
        compiler_params=_params(("parallel", "arbitrary")),
    )(z, z, z, lower3)


def _gla_bwd(z, lower3, states, do, prev, *, rev, col_q, col_f, col_v, hp):
    T = z.shape[0]
    rows = min(GLA_GROUP * CHUNK, T)
    nb = T // rows
    wide = hp * HG_DK
    blk = (lambda n: n) if rev else (lambda n: nb - 1 - n)
    has_prev = prev is not None
    fn = functools.partial(_gla_block, rev=rev, dot=_bdot)

    def body(*refs):
        hq_ref, hf_ref, hi_ref, low_ref, st_ref, do_ref = refs[:6]
        rest = refs[6:]
        if has_prev:
            pq_ref, pi_ref = rest[:2]
            rest = rest[2:]
        dhq_ref, dhi_ref, dhf_ref, dlow_ref, dst_ref = rest
        n = pl.program_id(1)

        @pl.when(n == 0)
        def _():
            dst_ref[...] = jnp.zeros_like(dst_ref)

        dst_in = [dst_ref[i] for i in range(hp)]
        heads = []
        for i in range(hp):
            cols = slice(i * HG_DK, (i + 1) * HG_DK)
            _, vjp = jax.vjp(fn, hq_ref[:, cols], hf_ref[:, cols], hi_ref[:, cols], low_ref[i], st_ref[i, 0])
            dhq, dhf, dhi, dlow, dst = vjp((do_ref[:, cols], dst_in[i]))
            if has_prev:
                dhq = dhq + pq_ref[:, cols]
                dhi = dhi + pi_ref[:, cols]
            heads.append((dhq, dhf, dhi, dlow, dst))
        for i, (dhq, dhf, dhi, dlow, dst) in enumerate(heads):
            cols = slice(i * HG_DK, (i + 1) * HG_DK)
            dst_ref[i] = dst
            dhq_ref[:, cols] = dhq.astype(dhq_ref.dtype)
            dhi_ref[:, cols] = dhi.astype(dhi_ref.dtype)
            dhf_ref[:, cols] = dhf.astype(dhf_ref.dtype)

        @pl.when(n == 0)
        def _():
            for i in range(hp):
                dlow_ref[i] = heads[i][3]

        @pl.when(n > 0)
        def _():
            for i in range(hp):
                dlow_ref[i] += heads[i][3]

    def zspec(col):
        return pl.BlockSpec((rows, wide), lambda h, n: (blk(n), col // wide + h))

    hspec = pl.BlockSpec((rows, wide), lambda h, n: (blk(n), h))
    in_specs = [zspec(col_q), zspec(col_f), zspec(col_v), pl.BlockSpec((hp, 1, HG_DK), lambda h, n: (h, 0, 0)),
                pl.BlockSpec((hp, 1, HG_DK, HG_DK), lambda h, n: (h, blk(n), 0, 0)), hspec]
    ins = [z, z, z, lower3, states, do]
    if has_prev:
        in_specs += [hspec, hspec]
        ins += list(prev)
    full_wide = HG_HEADS * HG_DK
    acc_dtype = BF16 if has_prev else F32
    return pl.pallas_call(
        body, name="gla_bwd_rev" if rev else "gla_bwd", grid=(HG_HEADS // hp, nb),
        in_specs=in_specs,
        out_specs=[hspec, hspec, hspec, pl.BlockSpec((hp, 1, HG_DK), lambda h, n: (h, 0, 0))],
        out_shape=[_sds((T, full_wide), acc_dtype), _sds((T, full_wide), acc_dtype), _sds((T, full_wide), BF16),
                   _sds((HG_HEADS, 1, HG_DK), F32)],
        scratch_shapes=[pltpu.VMEM((hp, HG_DK, HG_DK), F32)],
        compiler_params=_params(("parallel", "arbitrary")),
    )(*ins)


def _lower_fn(lb):
    e = jnp.exp(lb - jnp.max(lb, axis=0, keepdims=True))
    return (e / jnp.sum(e, axis=0, keepdims=True))[0]


def _lower_bounds(lb):
    def body(lb_ref, o_ref):
        o_ref[...] = _lower_fn(lb_ref[...])
    return pl.pallas_call(body, name="lower_bounds", out_shape=_sds(lb.shape[1:], F32))(lb)


def _lower_bounds_bwd(lb, dlower):
    def body(lb_ref, d_ref, o_ref):
        _, vjp = jax.vjp(_lower_fn, lb_ref[...])
        o_ref[...] = vjp(d_ref[...])[0]
    return pl.pallas_call(body, name="lower_bounds_bwd", out_shape=_sds(lb.shape, F32))(lb, dlower)


def _row_tile(r, cap=1024):
    best = None
    for t in range(16, min(r, cap) + 1, 16):
        if r % t == 0:
            best = t
    return best if best is not None else r


def _sum4(shards, recv, chip, *, name):
    _, R, C = shards.shape
    tr = _row_tile(R)

    def body(chip_ref, o_ref, r_ref, out_ref):
        out_ref[...] = ((o_ref[0].astype(F32) + r_ref[0].astype(F32)) + r_ref[1].astype(F32)) + r_ref[2].astype(F32)

    grid_spec = pltpu.PrefetchScalarGridSpec(
        num_scalar_prefetch=1, grid=(R // tr,),
        in_specs=[pl.BlockSpec((1, tr, C), lambda i, chip_ref: (chip_ref[0], i, 0)),
                  pl.BlockSpec((3, tr, C), lambda i, chip_ref: (0, i, 0))],
        out_specs=pl.BlockSpec((tr, C), lambda i, chip_ref: (i, 0)))
    return pl.pallas_call(
        body, name=name, grid_spec=grid_spec, out_shape=_sds((R, C), F32), compiler_params=_params(("parallel",)),
    )(chip, shards, recv)


def _adamw_math(w, g, m, v):
    m = ADAM_B1 * m + (1.0 - ADAM_B1) * g
    v = ADAM_B2 * v + (1.0 - ADAM_B2) * (g * g)
    m_hat = m / (1.0 - ADAM_B1 ** ADAM_STEP)
    v_hat = v / (1.0 - ADAM_B2 ** ADAM_STEP)
    delta = -ADAM_LR * (m_hat / (jnp.sqrt(v_hat) + ADAM_EPS) + ADAM_WD * w)
    return delta, m, v


def _adamw(w, g_a, g_b, m, v, *, name):
    R, C = w.shape
    tr = _row_tile(R)
    two = g_b is not None

    def body(*refs):
        w_ref, ga_ref = refs[0], refs[1]
        rest = refs[2:]
        g = ga_ref[...]
        if two:
            g = g + rest[0][...]
            rest = rest[1:]
        m_ref, v_ref, g_out, d_out, m_out, v_out = rest
        delta, m_new, v_new = _adamw_math(w_ref[...], g, m_ref[...], v_ref[...])
        g_out[...] = g
        d_out[...] = delta
        m_out[...] = m_new
        v_out[...] = v_new

    spec = pl.BlockSpec((tr, C), lambda i: (i, 0))
    ins = [w, g_a] + ([g_b] if two else []) + [m, v]
    return pl.pallas_call(
        body, name=name, grid=(R // tr,), in_specs=[spec] * len(ins), out_specs=[spec] * 4,
        out_shape=[_sds((R, C), F32)] * 4, compiler_params=_params(("parallel",)),
    )(*ins)


def _chip_peers():
    x, y, c = lax.axis_index("x"), lax.axis_index("y"), lax.axis_index("c")
    return (x, y, c), 2 * x + y, [(1 - x, y), (x, 1 - y), (1 - x, 1 - y)]


_HBM = pl.BlockSpec(memory_space=pltpu.HBM)
_SEM = pl.BlockSpec(memory_space=pltpu.SEMAPHORE)
_EFFECT = pltpu.SideEffectType.DATAFLOW_SIDE_EFFECTING


def _exchange_copies(srcs, lands, sems, mode):
    (x, y, c), me, chips = _chip_peers()
    copies = []
    for t, (src, land) in enumerate(zip(srcs, lands)):
        if mode == "swap":
            copies.append(pltpu.make_async_remote_copy(src, land, sems[0].at[3 * t], sems[1].at[3 * t],
                                                       device_id=(x, y, 1 - c), device_id_type=MESH))
            continue
        for k, (px, py) in enumerate(chips):
            gather = mode == "gather"
            copies.append(pltpu.make_async_remote_copy(
                src if gather else src.at[2 * px + py], land.at[me] if gather else land.at[k],
                sems[0].at[3 * t + k], sems[1].at[3 * t + k], device_id=(px, py, c), device_id_type=MESH))
        if mode == "gather":
            copies.append(pltpu.make_async_copy(src, land.at[me], sems[2].at[t]))
    return copies


def _exchange_start(srcs, *, mode, name, after=None):
    n = len(srcs)
    n_sem = 3 if mode == "gather" else 2
    n_in = 2 * n + (after is not None)
    land_shape = {"gather": lambda s: (N_CHIPS,) + s.shape, "scatter": lambda s: (3,) + s.shape[1:], "swap": lambda s: s.shape}
    lands = [_sds(land_shape[mode](s), s.dtype) for s in srcs]

    def body(*refs):
        for cp in _exchange_copies(refs[:n], refs[n:2 * n], refs[n_in:n_in + n_sem], mode):
            cp.start()
        token = refs[-1]
        token[...] = jnp.zeros_like(token)

    sem_shapes = [pltpu.SemaphoreType.DMA((3 * n,)), pltpu.SemaphoreType.DMA((3 * n,))]
    sem_shapes += [pltpu.SemaphoreType.DMA((n,))] if mode == "gather" else []
    thru = [pltpu.HBM(s.shape, s.dtype) for s in srcs] + [pltpu.HBM(l.shape, l.dtype) for l in lands]
    res = pl.pallas_call(
        body, name=name, in_specs=[_HBM] * (2 * n) + [_ANY] * (after is not None),
        out_specs=[_SEM] * n_sem + [_HBM] * (2 * n) + [pl.BlockSpec(memory_space=pltpu.VMEM)],
        out_shape=sem_shapes + thru + [_sds((8, LANE), F32)], input_output_aliases={i: n_sem + i for i in range(2 * n)},
        compiler_params=pltpu.CompilerParams(has_side_effects=_EFFECT),
    )(*[pltpu.with_memory_space_constraint(s, pltpu.HBM) for s in srcs],
      *[pltpu.with_memory_space_constraint(lax.empty(l.shape, l.dtype), pltpu.HBM) for l in lands],
      *([after] if after is not None else []))
    return (res[:n_sem], res[n_sem:n_sem + n], res[n_sem + n:n_sem + 2 * n], mode), res[-1]


def _exchange_wait(started, after, *, name):
    sems, srcs, lands, mode = started
    n, n_sem = len(srcs), len(sems)

    def body(*refs):
        for cp in _exchange_copies(refs[:n], refs[n:2 * n], refs[2 * n:2 * n + n_sem], mode):
            cp.wait()

    res = pl.pallas_call(
        body, name=name, in_specs=[_HBM] * (2 * n) + [_SEM] * n_sem + [_ANY], out_specs=[_HBM] * (2 * n),
        out_shape=[pltpu.HBM(a.shape, a.dtype) for a in list(srcs) + list(lands)],
        input_output_aliases={i: i for i in range(2 * n)},
        compiler_params=pltpu.CompilerParams(has_side_effects=_EFFECT),
    )(*srcs, *lands, *sems, after)
    return res[:n], res[n:]


def _allreduce_small(pack):
    R, C = pack.shape

    def body(in_ref, out_ref, slots, send_sems, recv_sems):
        x, y, c = lax.axis_index("x"), lax.axis_index("y"), lax.axis_index("c")
        me = 4 * x + 2 * y + c
        slots[me] = in_ref[...]
        copies = []
        for k in range(1, N_DEV):
            peer = (x ^ ((k >> 2) & 1), y ^ ((k >> 1) & 1), c ^ (k & 1))
            cp = pltpu.make_async_remote_copy(in_ref, slots.at[me], send_sems.at[k - 1], recv_sems.at[k - 1],
                                              device_id=peer, device_id_type=MESH)
            cp.start()
            copies.append(cp)
        for cp in copies:
            cp.wait()
        acc = slots[0]
        for d in range(1, N_DEV):
            acc = acc + slots[d]
        out_ref[...] = acc

    return pl.pallas_call(
        body, name="allreduce_small", out_shape=_sds((R, C), F32),
        in_specs=[pl.BlockSpec(memory_space=pltpu.VMEM)], out_specs=pl.BlockSpec(memory_space=pltpu.VMEM),
        scratch_shapes=[pltpu.VMEM((N_DEV, R, C), F32), pltpu.SemaphoreType.DMA((N_DEV - 1,)),
                        pltpu.SemaphoreType.DMA((N_DEV - 1,))],
        compiler_params=_params(),
    )(pack)


_Z_CQ, _Z_CKV, _Z_HQ, _Z_HFF, _Z_HFB, _Z_HI, _Z_HG, _Z_KR, _Z_END = 0, 256, 512, 1024, 1536, 2048, 2560, 3072, 3200


def _to_z_layout(wt):
    pad = jnp.zeros((_Z_END - _Z_KR - QK_ROPE, wt.shape[1]), wt.dtype)
    return jnp.concatenate([wt[:512], wt[512 + QK_ROPE:], wt[512:512 + QK_ROPE], pad], axis=0)


def _from_z_layout(wt):
    return jnp.concatenate([wt[:512], wt[_Z_KR:_Z_KR + QK_ROPE], wt[512:_Z_KR]], axis=0)


def _col_shards_to_full(g):
    return jnp.transpose(g, (1, 0, 2)).reshape(g.shape[1], -1)


def _full_to_col_shards(w):
    r, c = w.shape
    return jnp.transpose(w.reshape(r, N_CHIPS, c // N_CHIPS), (1, 0, 2))


def _full_to_row_shards(w):
    r, c = w.shape
    return w.reshape(N_CHIPS, r // N_CHIPS, c)


def kernel(x, p, positions, g_mix, w_in, g_qa, g_kva, w_qb, w_kvb, g_qn, g_kn, lb_param, g_hgo, w_o, g_ffn, w_gate, w_up, w_down, g_ple, w_ple_gate, w_ple_proj, loss_target, m_g_mix, m_w_in, m_g_qa, m_g_kva, m_w_qb, m_w_kvb, m_g_qn, m_g_kn, m_lb_param, m_g_hgo, m_w_o, m_g_ffn, m_w_gate, m_w_up, m_w_down, m_g_ple, m_w_ple_gate, m_w_ple_proj, v_g_mix, v_w_in, v_g_qa, v_g_kva, v_w_qb, v_w_kvb, v_g_qn, v_g_kn, v_lb_param, v_g_hgo, v_w_o, v_g_ffn, v_w_gate, v_w_up, v_w_down, v_g_ple, v_w_ple_gate, v_w_ple_proj):
    w_named = dict(g_mix=g_mix, w_in=w_in, g_qa=g_qa, g_kva=g_kva, w_qb=w_qb, w_kvb=w_kvb, g_qn=g_qn, g_kn=g_kn,
                   lb_param=lb_param, g_hgo=g_hgo, w_o=w_o, g_ffn=g_ffn, w_gate=w_gate, w_up=w_up, w_down=w_down,
                   g_ple=g_ple, w_ple_gate=w_ple_gate, w_ple_proj=w_ple_proj)
    m_named = dict(g_mix=m_g_mix, w_in=m_w_in, g_qa=m_g_qa, g_kva=m_g_kva, w_qb=m_w_qb, w_kvb=m_w_kvb, g_qn=m_g_qn,
                   g_kn=m_g_kn, lb_param=m_lb_param, g_hgo=m_g_hgo, w_o=m_w_o, g_ffn=m_g_ffn, w_gate=m_w_gate,
                   w_up=m_w_up, w_down=m_w_down, g_ple=m_g_ple, w_ple_gate=m_w_ple_gate, w_ple_proj=m_w_ple_proj)
    v_named = dict(g_mix=v_g_mix, w_in=v_w_in, g_qa=v_g_qa, g_kva=v_g_kva, w_qb=v_w_qb, w_kvb=v_w_kvb, g_qn=v_g_qn,
                   g_kn=v_g_kn, lb_param=v_lb_param, g_hgo=v_g_hgo, w_o=v_w_o, g_ffn=v_g_ffn, w_gate=v_w_gate,
                   w_up=v_w_up, w_down=v_w_down, g_ple=v_g_ple, w_ple_gate=v_w_ple_gate, w_ple_proj=v_w_ple_proj)
    order = list(w_named)
    transposed = ("w_in", "w_qb", "w_gate", "w_up")
    col_sharded = ("w_kvb", "w_ple_proj")
    row_sharded = ("w_o", "w_down", "w_ple_gate")
    big = transposed + col_sharded + row_sharded

    def view(n, a):
        return jnp.transpose(a[0]) if n in transposed else a[0]

    def unview(n, a):
        return (jnp.transpose(a) if n in transposed else a)[None]

    def to_shards(n, g):
        return _full_to_col_shards(g) if n in col_sharded else _full_to_row_shards(g)

    x2d, p2d, tgt = x[0], p[0, 0], loss_target[0]
    T, D = x2d.shape

    lb_flat = lb_param.reshape(-1, lb_param.shape[-1])
    gather_groups = (("w_in",), ("w_qb", "w_kvb"), ("w_o", "w_gate", "w_up"), ("w_down", "w_ple_gate", "w_ple_proj"))
    gather_started = []

    def gather_start(gi, after):
        srcs = [view(n, w_named[n]).astype(BF16) for n in gather_groups[gi]] + ([lb_flat] if gi == 0 else [])
        started, token = _exchange_start(srcs, mode="gather", name=f"gather_start_{gi}", after=after)
        gather_started.append(started)
        return token

    full = {}

    def gather_wait(gi, after):
        _, got = _exchange_wait(gather_started[gi], after, name=f"gather_wait_{gi}")
        for n, g in zip(gather_groups[gi], got):
            full[n] = _col_shards_to_full(g) if n in col_sharded else g.reshape(-1, g.shape[-1])
        return got

    inv_freq = ROPE_THETA ** (-jnp.arange(0, QK_ROPE, 2, dtype=F32) / QK_ROPE)
    ang = positions[0].astype(F32)[:, None] * inv_freq
    cos, sin = jnp.cos(ang), jnp.sin(ang)
    g_hgo_row = g_hgo.reshape(1, -1)

    token = gather_start(0, None)
    h1 = _stage(_rms, [x2d], [g_mix], [_sds((T, D), BF16)], [], name="norm_mix", after=token)[0]
    got = gather_wait(0, h1)
    token = got[0]
    for gi in range(1, len(gather_groups)):
        token = gather_start(gi, token)
    lb_full = _col_shards_to_full(got[-1]).reshape(lb_param.shape[0], lb_param.shape[1], -1)
    w_in_zt = _to_z_layout(full["w_in"])
    z = _mm(h1, w_in_zt, tb=True, name="in_proj", after=token)
    qn, kvn = _stage(_mla_a_fn, [_cols(z, 256, 0), _cols(z, 256, 1)], [g_qa, g_kva],
                     [_sds((T, 256), BF16), _sds((T, 256), BF16)], [], name="mla_latent_norm")
    gather_wait(1, qn)
    q_raw = _mm(qn, full["w_qb"], tb=True, name="q_up")
    kv_raw = _mm(kvn, full["w_kvb"], name="kv_up")
    kr = _cols(z, LANE, _Z_KR // LANE)
    q, k, v = _stage(_mla_b_fn, [q_raw, kv_raw, kr, cos, sin], [g_qn, g_kn],
                     [_sds((MLA_HEADS, T, QK_HEAD), BF16), _sds((MLA_HEADS, T, QK_HEAD), BF16),
                      _sds((MLA_HEADS, T, V_HEAD), BF16)], [], name="mla_qk_norm_rope")
    att, lse = _attention_fwd(q, k, v)

    lower = _lower_bounds(lb_full)
    lower3 = lower.reshape(2, HG_HEADS, 1, HG_DK)
    o_f, st_f = _gla_fwd(z, lower3[0], rev=False, col_q=_Z_HQ, col_f=_Z_HFF, col_v=_Z_HI, hp=GLA_FWD_HEADS)
    o_b, st_b = _gla_fwd(z, lower3[1], rev=True, col_q=_Z_HQ, col_f=_Z_HFB, col_v=_Z_HI, hp=GLA_FWD_HEADS)
    hg = _cols(z, 512, _Z_HG // 512)
    mix = _stage(_post_fn, [att, o_f, o_b, hg], [g_hgo_row], [_sds((T, att.shape[1] + o_f.shape[1]), BF16)], [],
                 name="mix_out")[0]
    gather_wait(2, mix)
    x1, h2 = _mm_fused(mix, full["w_o"], _resid_norm_fn, [x2d], [g_ffn], [F32, BF16], [], full_rows=True,
                       name="out_proj")
    up = _mm(h2, full["w_up"], tb=True, name="ffn_up")
    gt, act = _mm_fused(h2, full["w_gate"], lambda acc, up: (acc, _swiglu_fn(acc, up)), [up], [], [F32, BF16], [],
                        tb=True, name="ffn_gate")
    gather_wait(3, act)
    x2, h3 = _mm_fused(act, full["w_down"], _resid_norm_fn, [x1], [g_ple], [F32, BF16], [], full_rows=True,
                       name="ffn_down")
    pp = _mm(p2d, full["w_ple_proj"], name="ple_proj")
    dx3, dpg, dpp, loss_part = _mm_fused(
        h3, full["w_ple_gate"], lambda acc, pp, x2, tgt: _ple_loss_fn(x2, acc, pp, tgt), [pp, x2, tgt], [],
        [F32, BF16, BF16], [_sds((1, 1), F32)], full_rows=True, name="ple_gate_loss")

    grads = {}
    scatter_groups = (("w_ple_proj", "w_ple_gate", "w_down"), ("w_gate", "w_up", "w_o"), ("w_qb", "w_kvb", "w_in"))
    scatter_started = []

    def scatter_start(gi):
        srcs = [to_shards(n, grads[n]) for n in scatter_groups[gi]]
        started, token = _exchange_start(srcs, mode="scatter", name=f"scatter_start_{gi}")
        scatter_started.append(started)
        return token

    chip = 2 * lax.axis_index("x") + lax.axis_index("y")
    swap_started = []

    def reduce_group(gi, after):
        shards, recvs = _exchange_wait(scatter_started[gi], after, name=f"scatter_wait_{gi}")
        sums = [_sum4(s, r, chip.reshape(1), name="sum_" + n) for n, s, r in zip(scatter_groups[gi], shards, recvs)]
        started, token = _exchange_start(sums, mode="swap", name=f"swap_start_{gi}")
        swap_started.append(started)
        return token

    grads["w_ple_proj"] = _mm(p2d, dpp, ta=True, out_dtype=BF16, name="d_w_ple_proj")
    grads["w_ple_gate"] = _mm(h3, dpg, ta=True, out_dtype=BF16, name="d_w_ple_gate")
    dx2, grads["g_ple"] = _mm_fused(
        dpg, full["w_ple_gate"], lambda acc, x2, dx3, g: _norm_bwd_fn(x2, acc, dx3, g), [x2, dx3], [g_ple],
        [F32], [_sds((1, D), F32)], tb=True, full_rows=True, name="d_h3_norm_ple_bwd")
    dgt, dup = _mm_fused(dx2, full["w_down"], lambda acc, gt, up: _swiglu_bwd_fn(gt, up, acc), [gt, up], [],
                         [BF16, BF16], [], tb=True, name="d_act_swiglu_bwd")
    grads["w_down"] = _mm(act, dx2, ta=True, out_dtype=BF16, name="d_w_down")
    token = scatter_start(0)
    grads["w_gate"] = _mm(dgt, h2, ta=True, out_dtype=BF16, name="d_w_gate")
    grads["w_up"] = _mm(dup, h2, ta=True, out_dtype=BF16, name="d_w_up")
    dh2 = _mm(dgt, full["w_gate"], name="d_h2_gate", after=token)
    dx1, grads["g_ffn"] = _mm_fused(
        dup, full["w_up"], lambda acc, dh2, x1, dx2, g: _norm_bwd_fn(x1, acc + dh2, dx2, g), [dh2, x1, dx2], [g_ffn],
        [F32], [_sds((1, D), F32)], full_rows=True, name="d_h2_norm_ffn_bwd")
    dmix = _mm(dx1, full["w_o"], tb=True, name="d_mix")
    grads["w_o"] = _mm(mix, dx1, ta=True, out_dtype=BF16, name="d_w_o")

    token = scatter_start(1)
    half = MLA_HEADS * V_HEAD
    do, dhg, dg_hgo = _stage(_post_bwd_fn, [o_f, o_b, hg, _cols(dmix, half, 1)], [g_hgo_row],
                             [_sds((T, half), F32), _sds((T, half), BF16)], [_sds((1, half), F32)], name="mix_out_bwd",
                             after=token)
    grads["g_hgo"] = dg_hgo
    dhq_f, dhi_f, dhf_f, dlow_f = _gla_bwd(z, lower3[0], st_f, do, None, rev=False,
                                           col_q=_Z_HQ, col_f=_Z_HFF, col_v=_Z_HI, hp=GLA_BWD_HEADS)
    dhq, dhi, dhf_b, dlow_b = _gla_bwd(z, lower3[1], st_b, do, (dhq_f, dhi_f), rev=True,
                                       col_q=_Z_HQ, col_f=_Z_HFB, col_v=_Z_HI, hp=GLA_BWD_HEADS)

    dq, dk, dv = _attention_bwd(q, k, v, att, lse, dmix)
    token = reduce_group(0, dq)
    dq_raw, dkv_raw, dkr, grads["g_qn"], grads["g_kn"] = _stage(
        _mla_b_bwd_fn, [q_raw, kv_raw, kr, cos, sin, dq, dk, dv], [g_qn, g_kn],
        [_sds(q_raw.shape, BF16), _sds(kv_raw.shape, BF16), _sds((T, LANE), BF16)],
        [_sds(g_qn.shape, F32), _sds(g_kn.shape, F32)], name="mla_qk_norm_rope_bwd", after=token)
    grads["w_qb"] = _mm(dq_raw, qn, ta=True, out_dtype=BF16, name="d_w_qb")
    grads["w_kvb"] = _mm(kvn, dkv_raw, ta=True, out_dtype=BF16, name="d_w_kvb")
    dqn = _mm(dq_raw, full["w_qb"], name="d_qn")
    dkvn = _mm(dkv_raw, full["w_kvb"], tb=True, name="d_kvn")
    dcq, dckv, grads["g_qa"], grads["g_kva"] = _stage(
        _mla_a_bwd_fn, [_cols(z, 256, 0), _cols(z, 256, 1), dqn, dkvn], [g_qa, g_kva],
        [_sds((T, 256), BF16), _sds((T, 256), BF16)], [_sds(g_qa.shape, F32), _sds(g_kva.shape, F32)],
        name="mla_latent_norm_bwd")
    token = reduce_group(1, dcq)
    dz = jnp.concatenate([dcq, dckv, dhq, dhf_f, dhf_b, dhi, dhg, dkr], axis=1)
    grads["w_in"] = _from_z_layout(_mm(dz, h1, ta=True, out_dtype=BF16, name="d_w_in", after=token))
    token = scatter_start(2)
    grad_x, grads["g_mix"] = _mm_fused(
        dz, w_in_zt, lambda acc, x, dx1, g: _norm_bwd_fn(x, acc, dx1, g), [x2d, dx1], [g_mix],
        [F32], [_sds((1, D), F32)], full_rows=True, name="d_h1_norm_mix_bwd", after=token)

    out_g, out_d, out_m, out_v = {}, {}, {}, {}

    def update_group(gi, after):
        mine, theirs = _exchange_wait(swap_started[gi], after, name=f"swap_wait_{gi}")
        for n, a, b in zip(scatter_groups[gi], mine, theirs):
            out_g[n], out_d[n], out_m[n], out_v[n] = (
                unview(n, t) for t in _adamw(view(n, w_named[n]), a, b, view(n, m_named[n]), view(n, v_named[n]),
                                             name="adamw_" + n))
        return out_v[scatter_groups[gi][-1]]

    done = update_group(0, grad_x)
    done = update_group(1, done)
    token = reduce_group(2, done)
    update_group(2, token)

    small = ("g_mix", "g_qa", "g_kva", "g_qn", "g_kn", "g_hgo", "g_ffn", "g_ple")
    small_all = small + ("lb_param",)
    width = -(-max(w_named[n].size for n in small_all) // LANE) * LANE

    def row(a):
        a = a.reshape(1, -1)
        return jnp.pad(a, ((0, 0), (0, width - a.shape[1])))

    dlower = jnp.concatenate([dlow_f.reshape(1, -1), dlow_b.reshape(1, -1)], axis=0)
    pack = jnp.concatenate([row(grads[n]) for n in small] + [row(dlower[0]), row(dlower[1]), row(loss_part)]
                           + [jnp.zeros((5, width), F32)], axis=0)
    red = _allreduce_small(pack)
    loss = red[10, 0]
    dlower_sum = red[8:10, :lb_full.shape[-1]]
    dlb_full = _lower_bounds_bwd(lb_full, dlower_sum)
    fshard = lb_param.shape[-1]
    dlb = lax.dynamic_slice_in_dim(dlb_full, chip * fshard, fshard, axis=2)

    g_rows = [red[i:i + 1] for i in range(len(small))] + [row(dlb)]
    g_pack = jnp.concatenate(g_rows + [jnp.zeros((16 - len(g_rows), width), F32)], axis=0)

    def packed(named):
        rows = [row(named[n]) for n in small_all]
        return jnp.concatenate(rows + [jnp.ones((16 - len(rows), width), F32)], axis=0)

    s_g, s_d, s_m, s_v = _adamw(packed(w_named), g_pack, None, packed(m_named), packed(v_named), name="adamw_small")
    for i, n in enumerate(small_all):
        size = w_named[n].size
        for src, dst in ((s_g, out_g), (s_d, out_d), (s_m, out_m), (s_v, out_v)):
            dst[n] = src[i, :size].reshape(w_named[n].shape)

    return (loss, grad_x[None], *[out_g[n] for n in order], *[out_d[n] for n in order],
            *[out_m[n] for n in order], *[out_v[n] for n in order])
```

```python
import functools
import itertools

import jax
import jax.numpy as jnp
from jax import lax
from jax.experimental import pallas as pl
from jax.experimental.pallas import tpu as pltpu

F32 = jnp.float32
BF16 = jnp.bfloat16
MESH = pl.DeviceIdType.MESH

EPS = 1e-6
ROPE_THETA = 10000.0
MLA_HEADS = 4
QK_NOPE = 128
QK_ROPE = 64
QK_HEAD = QK_NOPE + QK_ROPE
V_HEAD = 128
HG_HEADS = 4
HG_DK = 128
CHUNK = 64
ADAM_LR = 0.001
ADAM_B1 = 0.9
ADAM_B2 = 0.999
ADAM_EPS = 1e-08
ADAM_WD = 0.01
ADAM_STEP = 10

LANE = 128
VMEM_LIMIT = 56 * 1024 * 1024
TOK_TILE = 256
GLA_GROUP = 16
GLA_FWD_HEADS = 4
GLA_BWD_HEADS = 2
ATT_TQ = 1024
ATT_TK = 1024
ATT_CHUNK = 512
LOG2_E = 1.4426950408889634
N_CHIPS = 4
N_DEV = 8


_ANY = pl.BlockSpec(memory_space=pl.ANY)


def _params(dims=None, **kw):
    return pltpu.CompilerParams(dimension_semantics=dims, vmem_limit_bytes=VMEM_LIMIT, **kw)


def _tile_candidates(n, cap):
    out = [d for d in range(LANE, min(n, cap) + 1, LANE) if n % d == 0]
    if n <= cap and n not in out:
        out.append(n)
    return out or [n]


MM_VMEM_BUDGET = 34 * 1024 * 1024
MM_MAX_ROWS = 1536
HBM_BYTES_PER_S = 2.8e12
MXU_FLOPS_PER_S = 8e14
STEP_OVERHEAD_S = 0.35e-6


def _mm_tiles(M, N, K, a_bytes, b_bytes, o_bytes, has_add, full_rows=False):
    best = None
    for tm in _tile_candidates(M, MM_MAX_ROWS):
        for tn in ([N] if full_rows else _tile_candidates(N, N)):
            for tk in _tile_candidates(K, K):
                ni, nj, nk = M // tm, N // tn, K // tk
                vmem = 2 * (tm * tk * a_bytes + tk * tn * b_bytes + tm * tn * o_bytes * (2 if has_add else 1))
                vmem += tm * tn * 4 * (2 if nk > 1 else 1)
                vmem += (tm * tk * 2 if a_bytes > 2 else 0) + (tk * tn * 2 if b_bytes > 2 else 0)
                if vmem > MM_VMEM_BUDGET:
                    continue
                moved = M * K * a_bytes * (nj if nk > 1 else 1) + K * N * b_bytes * (1 if nj == nk == 1 else ni)
                moved += M * N * o_bytes * (2 if has_add else 1)
                t = max(moved / HBM_BYTES_PER_S, 2 * M * N * K / MXU_FLOPS_PER_S) + ni * nj * nk * STEP_OVERHEAD_S
                if best is None or t < best[0]:
                    best = (t, tm, tn, tk)
    assert best is not None, (M, N, K)
    return best[1:]


def _dot_raw(a, b, kind):
    nb = a.ndim - 2
    batch = ((0,), (0,)) if nb else ((), ())
    ca = nb if kind == "tn" else nb + 1
    cb = nb + 1 if kind == "nt" else nb
    return lax.dot_general(a.astype(BF16), b.astype(BF16), (((ca,), (cb,)), batch), preferred_element_type=F32)


@functools.partial(jax.custom_vjp, nondiff_argnums=(2,))
def _bdot(a, b, kind):
    return _dot_raw(a, b, kind)


def _bdot_fwd(a, b, kind):
    return _dot_raw(a, b, kind), (a, b)


def _bdot_bwd(kind, res, g):
    a, b = res
    if kind == "nn":
        da, db = _bdot(g, b, "nt"), _bdot(a, g, "tn")
    elif kind == "nt":
        da, db = _bdot(g, b, "nn"), _bdot(g, a, "tn")
    else:
        da, db = _bdot(b, g, "nt"), _bdot(a, g, "nn")
    return da.astype(a.dtype), db.astype(b.dtype)


_bdot.defvjp(_bdot_fwd, _bdot_bwd)


def _mm(a, b, *, name, ta=False, tb=False, add=None, out_dtype=F32, after=None):
    K, M = a.shape if ta else a.shape[::-1]
    N, Kb = b.shape if tb else b.shape[::-1]
    assert K == Kb, (a.shape, b.shape, ta, tb)
    tm, tn, tk = _mm_tiles(M, N, K, a.dtype.itemsize, b.dtype.itemsize, jnp.dtype(out_dtype).itemsize, add is not None)
    nk = K // tk
    kind = "tn" if ta else ("nt" if tb else "nn")
    assert not (ta and tb)
    a_spec = pl.BlockSpec((tk, tm), lambda i, j, k: (k, i)) if ta else pl.BlockSpec((tm, tk), lambda i, j, k: (i, k))
    b_spec = pl.BlockSpec((tn, tk), lambda i, j, k: (j, k)) if tb else pl.BlockSpec((tk, tn), lambda i, j, k: (k, j))
    o_spec = pl.BlockSpec((tm, tn), lambda i, j, k: (i, j))
    has_add = add is not None

    def body(*refs):
        a_ref, b_ref = refs[0], refs[1]
        add_ref = refs[2] if has_add else None
        o_ref = refs[n_in]
        part = _dot_raw(a_ref[...], b_ref[...], kind)
        if nk == 1:
            if has_add:
                part = part + add_ref[...].astype(F32)
            o_ref[...] = part.astype(o_ref.dtype)
            return
        acc_ref = refs[-1]
        k = pl.program_id(2)

        @pl.when(k == 0)
        def _():
            acc_ref[...] = part

        @pl.when(k > 0)
        def _():
            acc_ref[...] += part

        @pl.when(k == nk - 1)
        def _():
            r = acc_ref[...]
            if has_add:
                r = r + add_ref[...].astype(F32)
            o_ref[...] = r.astype(o_ref.dtype)

    ins = [a, b] + ([add] if has_add else []) + ([after] if after is not None else [])
    in_specs = [a_spec, b_spec] + ([o_spec] if has_add else []) + ([_ANY] if after is not None else [])
    n_in = len(ins)
    return pl.pallas_call(
        body, name=name, grid=(M // tm, N // tn, nk), in_specs=in_specs, out_specs=o_spec,
        out_shape=jax.ShapeDtypeStruct((M, N), out_dtype),
        scratch_shapes=[pltpu.VMEM((tm, tn), F32)] if nk > 1 else [],
        compiler_params=_params(("parallel", "parallel", "arbitrary")),
    )(*ins)


def _mm_fused(a, b, fn, tiles, params, out_dtypes, sums, *, name, ta=False, tb=False, full_rows=False, after=None):
    K, M = a.shape if ta else a.shape[::-1]
    N, Kb = b.shape if tb else b.shape[::-1]
    assert K == Kb and not (ta and tb), (a.shape, b.shape, ta, tb)
    per_elem = sum(t.dtype.itemsize for t in tiles) + sum(jnp.dtype(d).itemsize for d in out_dtypes)
    tm, tn, tk = _mm_tiles(M, N, K, a.dtype.itemsize, b.dtype.itemsize, per_elem, False, full_rows)
    nk = K // tk
    kind = "tn" if ta else ("nt" if tb else "nn")
    a_spec = pl.BlockSpec((tk, tm), lambda i, j, k: (k, i)) if ta else pl.BlockSpec((tm, tk), lambda i, j, k: (i, k))
    b_spec = pl.BlockSpec((tn, tk), lambda i, j, k: (j, k)) if tb else pl.BlockSpec((tk, tn), lambda i, j, k: (k, j))
    o_spec = pl.BlockSpec((tm, tn), lambda i, j, k: (i, j))
    ins = [a, b] + list(tiles) + list(params) + ([after] if after is not None else [])
    in_specs = [a_spec, b_spec] + [o_spec] * len(tiles)
    in_specs += [pl.BlockSpec(p.shape, lambda i, j, k, nd=p.ndim: (0,) * nd) for p in params]
    in_specs += [_ANY] if after is not None else []
    n_in, n_t, n_p, n_o = len(ins), len(tiles), len(params), len(out_dtypes)

    def body(*refs):
        outs, sum_refs = refs[n_in:n_in + n_o], refs[n_in + n_o:n_in + n_o + len(sums)]

        def finish(r):
            res = fn(r, *[t[...] for t in refs[2:2 + n_t + n_p]])
            for o_ref, v in zip(outs, res[:n_o]):
                o_ref[...] = v.astype(o_ref.dtype)
            first = jnp.logical_and(pl.program_id(0) == 0, pl.program_id(1) == 0)
            for s_ref, v in zip(sum_refs, res[n_o:]):
                @pl.when(first)
                def _(s_ref=s_ref, v=v):
                    s_ref[...] = v

                @pl.when(jnp.logical_not(first))
                def _(s_ref=s_ref, v=v):
                    s_ref[...] += v

        part = _dot_raw(refs[0][...], refs[1][...], kind)
        if nk == 1:
            finish(part)
            return
        acc_ref = refs[-1]
        k = pl.program_id(2)

        @pl.when(k == 0)
        def _():
            acc_ref[...] = part

        @pl.when(k > 0)
        def _():
            acc_ref[...] += part

        @pl.when(k == nk - 1)
        def _():
            finish(acc_ref[...])

    out_shape = [_sds((M, N), d) for d in out_dtypes] + list(sums)
    out_specs = [o_spec] * n_o + [pl.BlockSpec(s.shape, lambda i, j, k, nd=len(s.shape): (0,) * nd) for s in sums]
    order = ("arbitrary",) * 3 if sums else ("parallel", "parallel", "arbitrary")
    return pl.pallas_call(
        body, name=name, grid=(M // tm, N // tn, nk), in_specs=in_specs, out_specs=out_specs, out_shape=out_shape,
        scratch_shapes=[pltpu.VMEM((tm, tn), F32)] if nk > 1 else [], compiler_params=_params(order),
    )(*ins)


def _cols(arr, width, block):
    return (arr, width, block)


def _stage(fn, tiles, params, out_tiles, out_sums, *, name, tile=TOK_TILE, after=None):
    def tok_spec(shape, width=None, block=0):
        if len(shape) == 2:
            w = shape[1] if width is None else width
            return pl.BlockSpec((tile, w), lambda i: (i, block))
        return pl.BlockSpec((shape[0], tile, shape[2]), lambda i: (0, i, 0))

    arrays, in_specs = [], []
    for t in tiles:
        if isinstance(t, tuple):
            arr, width, block = t
            arrays.append(arr)
            in_specs.append(tok_spec(arr.shape, width, block))
        else:
            arrays.append(t)
            in_specs.append(tok_spec(t.shape))
    n_tok = arrays[0].shape[0] if arrays[0].ndim == 2 else arrays[0].shape[1]
    for p in params:
        arrays.append(p)
        in_specs.append(pl.BlockSpec(p.shape, lambda i, nd=p.ndim: (0,) * nd))
    out_shape = list(out_tiles) + list(out_sums)
    out_specs = [tok_spec(o.shape) for o in out_tiles]
    out_specs += [pl.BlockSpec(o.shape, lambda i, nd=len(o.shape): (0,) * nd) for o in out_sums]
    n_fn, n_ot = len(arrays), len(out_tiles)
    if after is not None:
        arrays.append(after)
        in_specs.append(_ANY)
    n_in = len(arrays)

    def body(*refs):
        res = fn(*[r[...] for r in refs[:n_fn]])
        if not isinstance(res, (tuple, list)):
            res = (res,)
        outs = refs[n_in:]
        for o_ref, r in zip(outs[:n_ot], res[:n_ot]):
            o_ref[...] = r.astype(o_ref.dtype)
        i = pl.program_id(0)
        for o_ref, r in zip(outs[n_ot:], res[n_ot:]):
            @pl.when(i == 0)
            def _(o_ref=o_ref, r=r):
                o_ref[...] = r.astype(o_ref.dtype)

            @pl.when(i > 0)
            def _(o_ref=o_ref, r=r):
                o_ref[...] += r.astype(o_ref.dtype)

    res = pl.pallas_call(
        body, name=name, grid=(n_tok // tile,), in_specs=in_specs, out_specs=out_specs, out_shape=out_shape,
        compiler_params=_params(("arbitrary",)),
    )(*arrays)
    return res


def _sds(shape, dtype):
    return jax.ShapeDtypeStruct(tuple(shape), dtype)


def _sigmoid(x):
    return 1.0 / (1.0 + jnp.exp(-x))


def _sigmoid_t(x):
    return 0.5 * jnp.tanh(0.5 * x) + 0.5


def _rms(x, g):
    return x * lax.rsqrt(jnp.mean(x * x, axis=-1, keepdims=True) + EPS) * g


def _norm_bwd_fn(x, dh, dres, g):
    _, vjp = jax.vjp(_rms, x, g)
    dx, dg = vjp(dh)
    return dx + dres, dg


def _mla_a_fn(cq, ckv, g_qa, g_kva):
    return _rms(cq, g_qa), _rms(ckv, g_kva)


def _mla_a_bwd_fn(cq, ckv, dqn, dkvn, g_qa, g_kva):
    _, vjp = jax.vjp(_mla_a_fn, cq, ckv, g_qa, g_kva)
    return vjp((dqn, dkvn))


def _rope(t, cos, sin):
    t1, t2 = t[:, :QK_ROPE // 2], t[:, QK_ROPE // 2:]
    return jnp.concatenate([t1 * cos - t2 * sin, t1 * sin + t2 * cos], axis=-1)


def _mla_b_fn(q_raw, kv_raw, kr, cos, sin, g_qn, g_kn):
    krope = kr[:, :QK_ROPE]
    qs, ks, vs = [], [], []
    for h in range(MLA_HEADS):
        qh = _rms(q_raw[:, h * QK_HEAD:(h + 1) * QK_HEAD], g_qn)
        kvh = kv_raw[:, h * (QK_NOPE + V_HEAD):(h + 1) * (QK_NOPE + V_HEAD)]
        kh = _rms(jnp.concatenate([kvh[:, :QK_NOPE], krope], axis=-1), g_kn)
        qs.append(jnp.concatenate([qh[:, :QK_NOPE], _rope(qh[:, QK_NOPE:], cos, sin)], axis=-1))
        ks.append(jnp.concatenate([kh[:, :QK_NOPE], _rope(kh[:, QK_NOPE:], cos, sin)], axis=-1))
        vs.append(kvh[:, QK_NOPE:])
    return jnp.stack(qs), jnp.stack(ks), jnp.stack(vs)


def _mla_b_bwd_fn(q_raw, kv_raw, kr, cos, sin, dq, dk, dv, g_qn, g_kn):
    _, vjp = jax.vjp(lambda a, b, c, d, e: _mla_b_fn(a, b, c, cos, sin, d, e), q_raw, kv_raw, kr, g_qn, g_kn)
    return vjp((dq, dk, dv))


def _post_fn(a, o_f, o_b, hg, g_hgo):
    o = o_f + o_b
    parts = [a]
    for h in range(HG_HEADS):
        s = slice(h * HG_DK, (h + 1) * HG_DK)
        gate = hg[:, s]
        parts.append(_rms(o[:, s], g_hgo[:, s]) * (gate * _sigmoid(gate)))
    return jnp.concatenate(parts, axis=-1)


def _post_bwd_fn(o_f, o_b, hg, dr, g_hgo):
    def f(o, hg, g):
        return _post_fn(jnp.zeros_like(o), o, jnp.zeros_like(o), hg, g)[:, o.shape[1]:]
    _, vjp = jax.vjp(f, o_f + o_b, hg, g_hgo)
    return vjp(dr)


def _swiglu_fn(gt, up):
    return gt * _sigmoid(gt) * up


def _resid_norm_fn(acc, x, g):
    x_new = acc + x
    return x_new, _rms(x_new, g)


def _swiglu_bwd_fn(gt, up, dact):
    _, vjp = jax.vjp(_swiglu_fn, gt, up)
    return vjp(dact)


def _ple_loss_fn(x2, pg, pp, target):
    gate = _sigmoid(pg)
    err = x2 + gate * pp - target
    dx3 = err * (1.0 / err.shape[-1])
    loss = 0.5 * jnp.sum(jnp.mean(err * err, axis=-1, keepdims=True), axis=0, keepdims=True)
    return dx3, dx3 * pp * gate * (1.0 - gate), dx3 * gate, loss


def _attention_fwd(q, k, v):
    H, T, D = q.shape
    DV = v.shape[-1]
    tq, ck = min(ATT_TQ, T), min(ATT_CHUNK, T)
    c2 = (D ** -0.5) * LOG2_E

    def body(q_ref, k_ref, v_ref, o_ref, lse_ref):
        q_i = q_ref[0]

        def chunk(c, carry):
            m, l, acc = carry
            rows = pl.ds(pl.multiple_of(c * ck, ck), ck)
            s = _dot_raw(q_i, k_ref[0, rows, :], "nt")
            m_new = jnp.maximum(m, jnp.max(s, axis=-1, keepdims=True))
            p = jnp.exp2((s - m_new) * c2)
            alpha = jnp.exp2((m - m_new) * c2)
            l = l * alpha + jnp.sum(p, axis=-1, keepdims=True)
            acc = acc * alpha + _dot_raw(p, v_ref[0, rows, :], "nn")
            return m_new, l, acc

        init = (jnp.full((tq, 1), -jnp.inf, F32), jnp.zeros((tq, 1), F32), jnp.zeros((tq, DV), F32))
        m, l, acc = lax.fori_loop(0, T // ck, chunk, init, unroll=True)
        o_ref[...] = acc / l
        lse_ref[0] = m * c2 + jnp.log2(l)

    return pl.pallas_call(
        body, name="attention_fwd", grid=(H, T // tq),
        in_specs=[pl.BlockSpec((1, tq, D), lambda h, i: (h, i, 0)),
                  pl.BlockSpec((1, T, D), lambda h, i: (h, 0, 0)),
                  pl.BlockSpec((1, T, DV), lambda h, i: (h, 0, 0))],
        out_specs=[pl.BlockSpec((tq, DV), lambda h, i: (i, h)),
                   pl.BlockSpec((1, tq, 1), lambda h, i: (h, i, 0))],
        out_shape=[_sds((T, H * DV), F32), _sds((H, T, 1), F32)],
        compiler_params=_params(("parallel", "parallel")),
    )(q, k, v)


def _attention_bwd(q, k, v, o, lse2, dmix):
    H, T, D = q.shape
    DV = v.shape[-1]
    tk, cq = min(ATT_TK, T), min(ATT_CHUNK, T)
    scale = D ** -0.5
    c2 = scale * LOG2_E

    def body(q_ref, k_ref, v_ref, o_ref, lse_ref, do_ref, dq_ref, dk_ref, dv_ref, delta_ref):
        j = pl.program_id(1)

        @pl.when(j == 0)
        def _():
            delta = lax.dot_general(jnp.ones((8, DV), F32), do_ref[...] * o_ref[...], (((1,), (1,)), ((), ())),
                                    precision=lax.Precision.HIGHEST, preferred_element_type=F32)
            for i in range(T // cq):
                delta_ref[i] = delta[:, i * cq:(i + 1) * cq]
            dq_ref[0] = jnp.zeros((T, D), F32)

        k_j, v_j = k_ref[0], v_ref[0]
        dk_ref[0] = jnp.zeros((tk, D), F32)
        dv_ref[0] = jnp.zeros((tk, DV), F32)

        def chunk(c, carry):
            rows = pl.ds(pl.multiple_of(c * cq, cq), cq)
            q_c = q_ref[0, rows, :]
            do_c = do_ref[rows, :].astype(BF16)
            st = _dot_raw(k_j, q_c, "nt")
            pt = jnp.exp2(st * c2 - lse_ref[0, c])
            dv_ref[0] += _dot_raw(pt, do_c, "nn")
            dpt = _dot_raw(v_j, do_c, "nt")
            dst = pt * (dpt - delta_ref[c, 0:1, :]) * scale
            dk_ref[0] += _dot_raw(dst, q_c, "nn")
            dq_ref[0, rows, :] += _dot_raw(dst, k_j, "tn")
            return carry

        lax.fori_loop(0, T // cq, chunk, 0, unroll=True)

    return pl.pallas_call(
        body, name="attention_bwd", grid=(H, T // tk),
        in_specs=[pl.BlockSpec((1, T, D), lambda h, j: (h, 0, 0)),
                  pl.BlockSpec((1, tk, D), lambda h, j: (h, j, 0)),
                  pl.BlockSpec((1, tk, DV), lambda h, j: (h, j, 0)),
                  pl.BlockSpec((T, DV), lambda h, j: (0, h)),
                  pl.BlockSpec((1, T // cq, 1, cq), lambda h, j: (h, 0, 0, 0)),
                  pl.BlockSpec((T, DV), lambda h, j: (0, h))],
        out_specs=[pl.BlockSpec((1, T, D), lambda h, j: (h, 0, 0)),
                   pl.BlockSpec((1, tk, D), lambda h, j: (h, j, 0)),
                   pl.BlockSpec((1, tk, DV), lambda h, j: (h, j, 0))],
        out_shape=[_sds((H, T, D), F32), _sds((H, T, D), F32), _sds((H, T, DV), F32)],
        scratch_shapes=[pltpu.VMEM((T // cq, 8, cq), F32)],
        compiler_params=_params(("parallel", "arbitrary")),
    )(q, k, v, o, lse2.reshape(H, T // cq, 1, cq), dmix)


def _split3_dot(ones, x, kind):
    hi = x.astype(BF16)
    rest = x - hi.astype(F32)
    mid = rest.astype(BF16)
    lo = (rest - mid.astype(F32)).astype(BF16)
    return (_dot_raw(ones, hi, kind) + _dot_raw(ones, mid, kind)) + _dot_raw(ones, lo, kind)


@jax.custom_vjp
def _running_sum(x, tri):
    return _split3_dot(tri, x, "nn")


def _running_sum_fwd(x, tri):
    return _split3_dot(tri, x, "nn"), tri


def _running_sum_bwd(tri, g):
    return _split3_dot(tri, g, "tn"), jnp.zeros_like(tri)


_running_sum.defvjp(_running_sum_fwd, _running_sum_bwd)


def _gla_block(hq, hf, hi, lower, st_in, *, rev, dot):
    rows, dk = hq.shape
    G, C = rows // CHUNK, CHUNK
    q = hq * _sigmoid_t(hq)
    f = lower + (1.0 - lower) * _sigmoid_t(hf)
    k = 1.0 - f
    logf = jnp.log2(f)
    q3, k3, v3, lf3 = (t.reshape(G, C, dk) for t in (q, k, hi, logf))
    r = lax.broadcasted_iota(jnp.int32, (C, C), 0)
    c = lax.broadcasted_iota(jnp.int32, (C, C), 1)
    tri = ((r <= c) if rev else (r >= c)).astype(F32)
    b = _running_sum(lf3, jnp.broadcast_to(tri, (G, C, C)))
    tpos = lax.broadcasted_iota(jnp.int32, (1, C, 1), 1)
    first_half = (tpos >= C // 2) if rev else (tpos <= C // 2 - 1)
    b_mid = jnp.sum(jnp.where(first_half, lf3, 0.0), axis=1, keepdims=True)
    b_last = jnp.sum(lf3, axis=1, keepdims=True)
    a = dot(q3 * jnp.exp2(b - b_mid), k3 * jnp.exp2(b_mid - b), "nt") * tri
    o_intra = dot(a, v3, "nn")
    kv_t = dot(v3, k3 * jnp.exp2(b_last - b), "tn")
    decay = jnp.exp2(b_last)
    qd = q3 * jnp.exp2(b)
    st = st_in
    o_inter = [None] * G
    for g in (reversed(range(G)) if rev else range(G)):
        o_inter[g] = dot(qd[g], st, "nt")
        st = st * decay[g] + kv_t[g]
    o = o_intra.reshape(rows, dk) + jnp.concatenate(o_inter, axis=0)
    return o, st


def _gla_fwd(z, lower3, *, rev, col_q, col_f, col_v, hp):
    T = z.shape[0]
    rows = min(GLA_GROUP * CHUNK, T)
    nb = T // rows
    wide = hp * HG_DK
    blk = (lambda n: nb - 1 - n) if rev else (lambda n: n)

    def body(hq_ref, hf_ref, hi_ref, low_ref, o_ref, st_out_ref, st_ref):
        @pl.when(pl.program_id(1) == 0)
        def _():
            st_ref[...] = jnp.zeros_like(st_ref)

        st_in = [st_ref[i] for i in range(hp)]
        heads = []
        for i in range(hp):
            cols = slice(i * HG_DK, (i + 1) * HG_DK)
            heads.append(_gla_block(hq_ref[:, cols], hf_ref[:, cols], hi_ref[:, cols], low_ref[i], st_in[i], rev=rev,
                                    dot=_dot_raw))
        for i, (o, st) in enumerate(heads):
            st_out_ref[i, 0] = st_in[i]
            o_ref[:, i * HG_DK:(i + 1) * HG_DK] = o
            st_ref[i] = st

    def zspec(col):
        return pl.BlockSpec((rows, wide), lambda h, n: (blk(n), col // wide + h))

    return pl.pallas_call(
        body, name="gla_fwd_rev" if rev else "gla_fwd", grid=(HG_HEADS // hp, nb),
        in_specs=[zspec(col_q), zspec(col_f), zspec(col_v), pl.BlockSpec((hp, 1, HG_DK), lambda h, n: (h, 0, 0))],
        out_specs=[pl.BlockSpec((rows, wide), lambda h, n: (blk(n), h)),
                   pl.BlockSpec((hp, 1, HG_DK, HG_DK), lambda h, n: (h, blk(n), 0, 0))],
        out_shape=[_sds((T, HG_HEADS * HG_DK), F32), _sds((HG_HEADS, nb, HG_DK, HG_DK), F32)],
        scratch_shapes=[pltpu.VMEM((hp, HG_DK, HG_DK), F32)],
        compiler_params=_params(("parallel", "arbitrary")),
    )(z, z, z, lower3)


def _gla_bwd(z, lower3, states, do, prev, *, rev, col_q, col_f, col_v, hp):
    T = z.shape[0]
    rows = min(GLA_GROUP * CHUNK, T)
    nb = T // rows
    wide = hp * HG_DK
    blk = (lambda n: n) if rev else (lambda n: nb - 1 - n)
    has_prev = prev is not None
    fn = functools.partial(_gla_block, rev=rev, dot=_bdot)

    def body(*refs):
        hq_ref, hf_ref, hi_ref, low_ref, st_ref, do_ref = refs[:6]
        rest = refs[6:]
        if has_prev:
            pq_ref, pi_ref = rest[:2]
            rest = rest[2:]
        dhq_ref, dhi_ref, dhf_ref, dlow_ref, dst_ref = rest
        n = pl.program_id(1)

        @pl.when(n == 0)
        def _():
            dst_ref[...] = jnp.zeros_like(dst_ref)

        dst_in = [dst_ref[i] for i in range(hp)]
        heads = []
        for i in range(hp):
            cols = slice(i * HG_DK, (i + 1) * HG_DK)
            _, vjp = jax.vjp(fn, hq_ref[:, cols], hf_ref[:, cols], hi_ref[:, cols], low_ref[i], st_ref[i, 0])
            dhq, dhf, dhi, dlow, dst = vjp((do_ref[:, cols], dst_in[i]))
            if has_prev:
                dhq = dhq + pq_ref[:, cols]
                dhi = dhi + pi_ref[:, cols]
            heads.append((dhq, dhf, dhi, dlow, dst))
        for i, (dhq, dhf, dhi, dlow, dst) in enumerate(heads):
            cols = slice(i * HG_DK, (i + 1) * HG_DK)
            dst_ref[i] = dst
            dhq_ref[:, cols] = dhq.astype(dhq_ref.dtype)
            dhi_ref[:, cols] = dhi.astype(dhi_ref.dtype)
            dhf_ref[:, cols] = dhf.astype(dhf_ref.dtype)

        @pl.when(n == 0)
        def _():
            for i in range(hp):
                dlow_ref[i] = heads[i][3]

        @pl.when(n > 0)
        def _():
            for i in range(hp):
                dlow_ref[i] += heads[i][3]

    def zspec(col):
        return pl.BlockSpec((rows, wide), lambda h, n: (blk(n), col // wide + h))

    hspec = pl.BlockSpec((rows, wide), lambda h, n: (blk(n), h))
    in_specs = [zspec(col_q), zspec(col_f), zspec(col_v), pl.BlockSpec((hp, 1, HG_DK), lambda h, n: (h, 0, 0)),
                pl.BlockSpec((hp, 1, HG_DK, HG_DK), lambda h, n: (h, blk(n), 0, 0)), hspec]
    ins = [z, z, z, lower3, states, do]
    if has_prev:
        in_specs += [hspec, hspec]
        ins += list(prev)
    full_wide = HG_HEADS * HG_DK
    acc_dtype = BF16 if has_prev else F32
    return pl.pallas_call(
        body, name="gla_bwd_rev" if rev else "gla_bwd", grid=(HG_HEADS // hp, nb),
        in_specs=in_specs,
        out_specs=[hspec, hspec, hspec, pl.BlockSpec((hp, 1, HG_DK), lambda h, n: (h, 0, 0))],
        out_shape=[_sds((T, full_wide), acc_dtype), _sds((T, full_wide), acc_dtype), _sds((T, full_wide), BF16),
                   _sds((HG_HEADS, 1, HG_DK), F32)],
        scratch_shapes=[pltpu.VMEM((hp, HG_DK, HG_DK), F32)],
        compiler_params=_params(("parallel", "arbitrary")),
    )(*ins)


def _lower_fn(lb):
    e = jnp.exp(lb - jnp.max(lb, axis=0, keepdims=True))
    return (e / jnp.sum(e, axis=0, keepdims=True))[0]


def _lower_bounds(lb):
    def body(lb_ref, o_ref):
        o_ref[...] = _lower_fn(lb_ref[...])
    return pl.pallas_call(body, name="lower_bounds", out_shape=_sds(lb.shape[1:], F32))(lb)


def _row_tile(r, cap=1024):
    best = None
    for t in range(16, min(r, cap) + 1, 16):
        if r % t == 0:
            best = t
    return best if best is not None else r


def _sum4(shards, recv, chip, *, name):
    _, R, C = shards.shape
    tr = _row_tile(R)

    def body(chip_ref, o_ref, r_ref, out_ref):
        out_ref[...] = ((o_ref[0].astype(F32) + r_ref[0].astype(F32)) + r_ref[1].astype(F32)) + r_ref[2].astype(F32)

    grid_spec = pltpu.PrefetchScalarGridSpec(
        num_scalar_prefetch=1, grid=(R // tr,),
        in_specs=[pl.BlockSpec((1, tr, C), lambda i, chip_ref: (chip_ref[0], i, 0)),
                  pl.BlockSpec((3, tr, C), lambda i, chip_ref: (0, i, 0))],
        out_specs=pl.BlockSpec((tr, C), lambda i, chip_ref: (i, 0)))
    return pl.pallas_call(
        body, name=name, grid_spec=grid_spec, out_shape=_sds((R, C), F32), compiler_params=_params(("parallel",)),
    )(chip, shards, recv)


def _adamw_math(w, g, m, v):
    m = ADAM_B1 * m + (1.0 - ADAM_B1) * g
    v = ADAM_B2 * v + (1.0 - ADAM_B2) * (g * g)
    m_hat = m / (1.0 - ADAM_B1 ** ADAM_STEP)
    v_hat = v / (1.0 - ADAM_B2 ** ADAM_STEP)
    delta = -ADAM_LR * (m_hat / (jnp.sqrt(v_hat) + ADAM_EPS) + ADAM_WD * w)
    return delta, m, v


def _adamw(w, g_a, g_b, m, v, *, name):
    R, C = w.shape
    tr = _row_tile(R)
    two = g_b is not None

    def body(*refs):
        w_ref, ga_ref = refs[0], refs[1]
        rest = refs[2:]
        g = ga_ref[...]
        if two:
            g = g + rest[0][...]
            rest = rest[1:]
        m_ref, v_ref, g_out, d_out, m_out, v_out = rest
        delta, m_new, v_new = _adamw_math(w_ref[...], g, m_ref[...], v_ref[...])
        g_out[...] = g
        d_out[...] = delta
        m_out[...] = m_new
        v_out[...] = v_new

    spec = pl.BlockSpec((tr, C), lambda i: (i, 0))
    ins = [w, g_a] + ([g_b] if two else []) + [m, v]
    return pl.pallas_call(
        body, name=name, grid=(R // tr,), in_specs=[spec] * len(ins), out_specs=[spec] * 4,
        out_shape=[_sds((R, C), F32)] * 4, compiler_params=_params(("parallel",)),
    )(*ins)


def _adamw_small(red, lb_full, ws, ms, vs):
    n = len(ws)

    def pieces(shape):
        out = []
        for j, idx in enumerate(itertools.product(*[range(d) for d in shape[:-1]])):
            out.append((idx[:-1] + (slice(idx[-1], idx[-1] + 1), slice(None)), j * shape[-1]))
        return out

    def body(*refs):
        red_ref, lb_ref = refs[0], refs[1]
        w_refs, m_refs, v_refs = refs[2:2 + n], refs[2 + n:2 + 2 * n], refs[2 + 2 * n:2 + 3 * n]
        out_refs = refs[2 + 3 * n:]
        chip = 2 * lax.axis_index("x") + lax.axis_index("y")
        n_f, shard = lb_ref.shape[-1], w_refs[n - 1].shape[-1]
        _, vjp = jax.vjp(_lower_fn, lb_ref[...])
        dlb = vjp(red_ref[8:10, 0:n_f])[0]
        for i in range(n):
            width = w_refs[i].shape[-1]
            for j, (at, lane) in enumerate(pieces(w_refs[i].shape)):
                if i < n - 1:
                    g = red_ref[i:i + 1, lane:lane + width]
                else:
                    row = dlb[j // 2][j % 2:j % 2 + 1]
                    g = sum(jnp.where(chip == q, row[:, q * shard:(q + 1) * shard], 0.0) for q in range(N_CHIPS))
                delta, m_new, v_new = _adamw_math(w_refs[i][at], g, m_refs[i][at], v_refs[i][at])
                for o_ref, val in zip(out_refs[4 * i:4 * i + 4], (g, delta, m_new, v_new)):
                    o_ref[at] = val

    return pl.pallas_call(
        body, name="adamw_small", out_shape=[_sds(w.shape, F32) for w in ws for _ in range(4)],
    )(red, lb_full, *ws, *ms, *vs)


def _chip_peers():
    x, y, c = lax.axis_index("x"), lax.axis_index("y"), lax.axis_index("c")
    return (x, y, c), 2 * x + y, [(1 - x, y), (x, 1 - y), (1 - x, 1 - y)]


_HBM = pl.BlockSpec(memory_space=pltpu.HBM)
_SEM = pl.BlockSpec(memory_space=pltpu.SEMAPHORE)
_EFFECT = pltpu.SideEffectType.DATAFLOW_SIDE_EFFECTING


def _exchange_copies(srcs, lands, sems, mode):
    (x, y, c), me, chips = _chip_peers()
    copies = []
    for t, (src, land) in enumerate(zip(srcs, lands)):
        if mode == "swap":
            copies.append(pltpu.make_async_remote_copy(src, land, sems[0].at[3 * t], sems[1].at[3 * t],
                                                       device_id=(x, y, 1 - c), device_id_type=MESH))
            continue
        for k, (px, py) in enumerate(chips):
            gather = mode == "gather"
            copies.append(pltpu.make_async_remote_copy(
                src if gather else src.at[2 * px + py], land.at[me] if gather else land.at[k],
                sems[0].at[3 * t + k], sems[1].at[3 * t + k], device_id=(px, py, c), device_id_type=MESH))
        if mode == "gather":
            copies.append(pltpu.make_async_copy(src, land.at[me], sems[2].at[t]))
    return copies


def _exchange_start(srcs, *, mode, name, after=None):
    n = len(srcs)
    n_sem = 3 if mode == "gather" else 2
    n_in = 2 * n + (after is not None)
    land_shape = {"gather": lambda s: (N_CHIPS,) + s.shape, "scatter": lambda s: (3,) + s.shape[1:], "swap": lambda s: s.shape}
    lands = [_sds(land_shape[mode](s), s.dtype) for s in srcs]

    def body(*refs):
        for cp in _exchange_copies(refs[:n], refs[n:2 * n], refs[n_in:n_in + n_sem], mode):
            cp.start()
        token = refs[-1]
        token[...] = jnp.zeros_like(token)

    sem_shapes = [pltpu.SemaphoreType.DMA((3 * n,)), pltpu.SemaphoreType.DMA((3 * n,))]
    sem_shapes += [pltpu.SemaphoreType.DMA((n,))] if mode == "gather" else []
    thru = [pltpu.HBM(s.shape, s.dtype) for s in srcs] + [pltpu.HBM(l.shape, l.dtype) for l in lands]
    res = pl.pallas_call(
        body, name=name, in_specs=[_HBM] * (2 * n) + [_ANY] * (after is not None),
        out_specs=[_SEM] * n_sem + [_HBM] * (2 * n) + [pl.BlockSpec(memory_space=pltpu.VMEM)],
        out_shape=sem_shapes + thru + [_sds((8, LANE), F32)], input_output_aliases={i: n_sem + i for i in range(2 * n)},
        compiler_params=pltpu.CompilerParams(has_side_effects=_EFFECT),
    )(*[pltpu.with_memory_space_constraint(s, pltpu.HBM) for s in srcs],
      *[pltpu.with_memory_space_constraint(lax.empty(l.shape, l.dtype), pltpu.HBM) for l in lands],
      *([after] if after is not None else []))
    return (res[:n_sem], res[n_sem:n_sem + n], res[n_sem + n:n_sem + 2 * n], mode), res[-1]


def _exchange_wait(started, after, *, name):
    sems, srcs, lands, mode = started
    n, n_sem = len(srcs), len(sems)
    after = list(after) if isinstance(after, (list, tuple)) else [after]

    def body(*refs):
        for cp in _exchange_copies(refs[:n], refs[n:2 * n], refs[2 * n:2 * n + n_sem], mode):
            cp.wait()

    res = pl.pallas_call(
        body, name=name, in_specs=[_HBM] * (2 * n) + [_SEM] * n_sem + [_ANY] * len(after), out_specs=[_HBM] * (2 * n),
        out_shape=[pltpu.HBM(a.shape, a.dtype) for a in list(srcs) + list(lands)],
        input_output_aliases={i: i for i in range(2 * n)},
        compiler_params=pltpu.CompilerParams(has_side_effects=_EFFECT),
    )(*srcs, *lands, *sems, *after)
    return res[:n], res[n:]


def _allreduce_small(pack):
    R, C = pack.shape

    def body(in_ref, out_ref, slots, send_sems, recv_sems):
        x, y, c = lax.axis_index("x"), lax.axis_index("y"), lax.axis_index("c")
        me = 4 * x + 2 * y + c
        slots[me] = in_ref[...]
        copies = []
        for k in range(1, N_DEV):
            peer = (x ^ ((k >> 2) & 1), y ^ ((k >> 1) & 1), c ^ (k & 1))
            cp = pltpu.make_async_remote_copy(in_ref, slots.at[me], send_sems.at[k - 1], recv_sems.at[k - 1],
                                              device_id=peer, device_id_type=MESH)
            cp.start()
            copies.append(cp)
        for cp in copies:
            cp.wait()
        acc = slots[0]
        for d in range(1, N_DEV):
            acc = acc + slots[d]
        out_ref[...] = acc

    return pl.pallas_call(
        body, name="allreduce_small", out_shape=_sds((R, C), F32),
        in_specs=[pl.BlockSpec(memory_space=pltpu.VMEM)], out_specs=pl.BlockSpec(memory_space=pltpu.VMEM),
        scratch_shapes=[pltpu.VMEM((N_DEV, R, C), F32), pltpu.SemaphoreType.DMA((N_DEV - 1,)),
                        pltpu.SemaphoreType.DMA((N_DEV - 1,))],
        compiler_params=_params(),
    )(pack)


_Z_CQ, _Z_CKV, _Z_HQ, _Z_HFF, _Z_HFB, _Z_HI, _Z_HG, _Z_KR, _Z_END = 0, 256, 512, 1024, 1536, 2048, 2560, 3072, 3200


def _to_z_layout(wt):
    pad = jnp.zeros((_Z_END - _Z_KR - QK_ROPE, wt.shape[1]), wt.dtype)
    return jnp.concatenate([wt[:512], wt[512 + QK_ROPE:], wt[512:512 + QK_ROPE], pad], axis=0)


def _from_z_layout(wt):
    return jnp.concatenate([wt[:512], wt[_Z_KR:_Z_KR + QK_ROPE], wt[512:_Z_KR]], axis=0)


def _col_shards_to_full(g):
    return jnp.transpose(g, (1, 0, 2)).reshape(g.shape[1], -1)


def _full_to_col_shards(w):
    r, c = w.shape
    return jnp.transpose(w.reshape(r, N_CHIPS, c // N_CHIPS), (1, 0, 2))


def _full_to_row_shards(w):
    r, c = w.shape
    return w.reshape(N_CHIPS, r // N_CHIPS, c)


def kernel(x, p, positions, g_mix, w_in, g_qa, g_kva, w_qb, w_kvb, g_qn, g_kn, lb_param, g_hgo, w_o, g_ffn, w_gate, w_up, w_down, g_ple, w_ple_gate, w_ple_proj, loss_target, m_g_mix, m_w_in, m_g_qa, m_g_kva, m_w_qb, m_w_kvb, m_g_qn, m_g_kn, m_lb_param, m_g_hgo, m_w_o, m_g_ffn, m_w_gate, m_w_up, m_w_down, m_g_ple, m_w_ple_gate, m_w_ple_proj, v_g_mix, v_w_in, v_g_qa, v_g_kva, v_w_qb, v_w_kvb, v_g_qn, v_g_kn, v_lb_param, v_g_hgo, v_w_o, v_g_ffn, v_w_gate, v_w_up, v_w_down, v_g_ple, v_w_ple_gate, v_w_ple_proj):
    w_named = dict(g_mix=g_mix, w_in=w_in, g_qa=g_qa, g_kva=g_kva, w_qb=w_qb, w_kvb=w_kvb, g_qn=g_qn, g_kn=g_kn,
                   lb_param=lb_param, g_hgo=g_hgo, w_o=w_o, g_ffn=g_ffn, w_gate=w_gate, w_up=w_up, w_down=w_down,
                   g_ple=g_ple, w_ple_gate=w_ple_gate, w_ple_proj=w_ple_proj)
    m_named = dict(g_mix=m_g_mix, w_in=m_w_in, g_qa=m_g_qa, g_kva=m_g_kva, w_qb=m_w_qb, w_kvb=m_w_kvb, g_qn=m_g_qn,
                   g_kn=m_g_kn, lb_param=m_lb_param, g_hgo=m_g_hgo, w_o=m_w_o, g_ffn=m_g_ffn, w_gate=m_w_gate,
                   w_up=m_w_up, w_down=m_w_down, g_ple=m_g_ple, w_ple_gate=m_w_ple_gate, w_ple_proj=m_w_ple_proj)
    v_named = dict(g_mix=v_g_mix, w_in=v_w_in, g_qa=v_g_qa, g_kva=v_g_kva, w_qb=v_w_qb, w_kvb=v_w_kvb, g_qn=v_g_qn,
                   g_kn=v_g_kn, lb_param=v_lb_param, g_hgo=v_g_hgo, w_o=v_w_o, g_ffn=v_g_ffn, w_gate=v_w_gate,
                   w_up=v_w_up, w_down=v_w_down, g_ple=v_g_ple, w_ple_gate=v_w_ple_gate, w_ple_proj=v_w_ple_proj)
    order = list(w_named)
    transposed = ("w_in", "w_qb", "w_gate", "w_up")
    col_sharded = ("w_kvb", "w_ple_proj")
    row_sharded = ("w_o", "w_down", "w_ple_gate")
    big = transposed + col_sharded + row_sharded

    def view(n, a):
        return jnp.transpose(a[0]) if n in transposed else a[0]

    def unview(n, a):
        return (jnp.transpose(a) if n in transposed else a)[None]

    def to_shards(n, g):
        return _full_to_col_shards(g) if n in col_sharded else _full_to_row_shards(g)

    x2d, p2d, tgt = x[0], p[0, 0], loss_target[0]
    T, D = x2d.shape

    lb_flat = lb_param.reshape(-1, lb_param.shape[-1])
    gather_groups = (("w_in",), ("w_qb", "w_kvb"), ("w_o", "w_gate", "w_up"), ("w_down", "w_ple_gate", "w_ple_proj"))
    gather_started = []

    casts = {n: view(n, w_named[n]).astype(BF16) for n in big}

    def gather_start(gi, after):
        srcs = [casts[n] for n in gather_groups[gi]] + ([lb_flat] if gi == 0 else [])
        started, token = _exchange_start(srcs, mode="gather", name=f"gather_start_{gi}", after=after)
        gather_started.append(started)
        return token

    full = {}

    def gather_wait(gi, after):
        _, got = _exchange_wait(gather_started[gi], after, name=f"gather_wait_{gi}")
        for n, g in zip(gather_groups[gi], got):
            full[n] = _col_shards_to_full(g) if n in col_sharded else g.reshape(-1, g.shape[-1])
        return got

    g_hgo_row = g_hgo.reshape(1, -1)

    inv_freq = ROPE_THETA ** (-jnp.arange(0, QK_ROPE, 2, dtype=F32) / QK_ROPE)
    ang = positions[0].astype(F32)[:, None] * inv_freq
    cos, sin = jnp.cos(ang), jnp.sin(ang)
    token = gather_start(0, None)
    h1 = _stage(_rms, [x2d], [g_mix], [_sds((T, D), BF16)], [], name="norm_mix", after=token)[0]
    got = gather_wait(0, [h1, cos, sin] + [casts[n] for g in gather_groups[1:] for n in g])
    token = got[0]
    for gi in range(1, len(gather_groups)):
        token = gather_start(gi, token)
    lb_full = _col_shards_to_full(got[-1]).reshape(lb_param.shape[0], lb_param.shape[1], -1)
    w_in_zt = _to_z_layout(full["w_in"])
    z = _mm(h1, w_in_zt, tb=True, name="in_proj", after=token)
    qn, kvn = _stage(_mla_a_fn, [_cols(z, 256, 0), _cols(z, 256, 1)], [g_qa, g_kva],
                     [_sds((T, 256), BF16), _sds((T, 256), BF16)], [], name="mla_latent_norm")
    gather_wait(1, qn)
    q_raw = _mm(qn, full["w_qb"], tb=True, name="q_up")
    kv_raw = _mm(kvn, full["w_kvb"], name="kv_up")
    kr = _cols(z, LANE, _Z_KR // LANE)
    q, k, v = _stage(_mla_b_fn, [q_raw, kv_raw, kr, cos, sin], [g_qn, g_kn],
                     [_sds((MLA_HEADS, T, QK_HEAD), BF16), _sds((MLA_HEADS, T, QK_HEAD), BF16),
                      _sds((MLA_HEADS, T, V_HEAD), BF16)], [], name="mla_qk_norm_rope")
    att, lse = _attention_fwd(q, k, v)

    lower = _lower_bounds(lb_full)
    lower3 = lower.reshape(2, HG_HEADS, 1, HG_DK)
    o_f, st_f = _gla_fwd(z, lower3[0], rev=False, col_q=_Z_HQ, col_f=_Z_HFF, col_v=_Z_HI, hp=GLA_FWD_HEADS)
    o_b, st_b = _gla_fwd(z, lower3[1], rev=True, col_q=_Z_HQ, col_f=_Z_HFB, col_v=_Z_HI, hp=GLA_FWD_HEADS)
    hg = _cols(z, 512, _Z_HG // 512)
    mix = _stage(_post_fn, [att, o_f, o_b, hg], [g_hgo_row], [_sds((T, att.shape[1] + o_f.shape[1]), BF16)], [],
                 name="mix_out")[0]
    gather_wait(2, mix)
    x1, h2 = _mm_fused(mix, full["w_o"], _resid_norm_fn, [x2d], [g_ffn], [F32, BF16], [], full_rows=True,
                       name="out_proj")
    up = _mm(h2, full["w_up"], tb=True, name="ffn_up")
    gt, act = _mm_fused(h2, full["w_gate"], lambda acc, up: (acc, _swiglu_fn(acc, up)), [up], [], [F32, BF16], [],
                        tb=True, name="ffn_gate")
    gather_wait(3, act)
    x2, h3 = _mm_fused(act, full["w_down"], _resid_norm_fn, [x1], [g_ple], [F32, BF16], [], full_rows=True,
                       name="ffn_down")
    pp = _mm(p2d, full["w_ple_proj"], name="ple_proj")
    dx3, dpg, dpp, loss_part = _mm_fused(
        h3, full["w_ple_gate"], lambda acc, pp, x2, tgt: _ple_loss_fn(x2, acc, pp, tgt), [pp, x2, tgt], [],
        [F32, BF16, BF16], [_sds((1, 1), F32)], full_rows=True, name="ple_gate_loss")

    grads = {}
    scatter_groups = (("w_ple_proj", "w_ple_gate", "w_down"), ("w_gate", "w_up", "w_o"), ("w_qb", "w_kvb", "w_in"))
    scatter_started = []

    def scatter_start(gi):
        srcs = [to_shards(n, grads[n]) for n in scatter_groups[gi]]
        started, token = _exchange_start(srcs, mode="scatter", name=f"scatter_start_{gi}")
        scatter_started.append(started)
        return token

    chip = 2 * lax.axis_index("x") + lax.axis_index("y")
    swap_started = []

    def reduce_group(gi, after):
        shards, recvs = _exchange_wait(scatter_started[gi], after, name=f"scatter_wait_{gi}")
        sums = [_sum4(s, r, chip.reshape(1), name="sum_" + n) for n, s, r in zip(scatter_groups[gi], shards, recvs)]
        started, token = _exchange_start(sums, mode="swap", name=f"swap_start_{gi}")
        swap_started.append(started)
        return token

    grads["w_ple_proj"] = _mm(p2d, dpp, ta=True, out_dtype=BF16, name="d_w_ple_proj")
    grads["w_ple_gate"] = _mm(h3, dpg, ta=True, out_dtype=BF16, name="d_w_ple_gate")
    dx2, grads["g_ple"] = _mm_fused(
        dpg, full["w_ple_gate"], lambda acc, x2, dx3, g: _norm_bwd_fn(x2, acc, dx3, g), [x2, dx3], [g_ple],
        [F32], [_sds((1, D), F32)], tb=True, full_rows=True, name="d_h3_norm_ple_bwd")
    dgt, dup = _mm_fused(dx2, full["w_down"], lambda acc, gt, up: _swiglu_bwd_fn(gt, up, acc), [gt, up], [],
                         [BF16, BF16], [], tb=True, name="d_act_swiglu_bwd")
    grads["w_down"] = _mm(act, dx2, ta=True, out_dtype=BF16, name="d_w_down")
    token = scatter_start(0)
    grads["w_gate"] = _mm(dgt, h2, ta=True, out_dtype=BF16, name="d_w_gate")
    grads["w_up"] = _mm(dup, h2, ta=True, out_dtype=BF16, name="d_w_up")
    dh2 = _mm(dgt, full["w_gate"], name="d_h2_gate", after=token)
    dx1, grads["g_ffn"] = _mm_fused(
        dup, full["w_up"], lambda acc, dh2, x1, dx2, g: _norm_bwd_fn(x1, acc + dh2, dx2, g), [dh2, x1, dx2], [g_ffn],
        [F32], [_sds((1, D), F32)], full_rows=True, name="d_h2_norm_ffn_bwd")
    dmix = _mm(dx1, full["w_o"], tb=True, name="d_mix")
    grads["w_o"] = _mm(mix, dx1, ta=True, out_dtype=BF16, name="d_w_o")

    token = scatter_start(1)
    half = MLA_HEADS * V_HEAD
    do, dhg, dg_hgo = _stage(_post_bwd_fn, [o_f, o_b, hg, _cols(dmix, half, 1)], [g_hgo_row],
                             [_sds((T, half), F32), _sds((T, half), BF16)], [_sds((1, half), F32)], name="mix_out_bwd",
                             after=token)
    grads["g_hgo"] = dg_hgo
    dhq_f, dhi_f, dhf_f, dlow_f = _gla_bwd(z, lower3[0], st_f, do, None, rev=False,
                                           col_q=_Z_HQ, col_f=_Z_HFF, col_v=_Z_HI, hp=GLA_BWD_HEADS)
    dhq, dhi, dhf_b, dlow_b = _gla_bwd(z, lower3[1], st_b, do, (dhq_f, dhi_f), rev=True,
                                       col_q=_Z_HQ, col_f=_Z_HFB, col_v=_Z_HI, hp=GLA_BWD_HEADS)

    dq, dk, dv = _attention_bwd(q, k, v, att, lse, dmix)
    token = reduce_group(0, dq)
    dq_raw, dkv_raw, dkr, grads["g_qn"], grads["g_kn"] = _stage(
        _mla_b_bwd_fn, [q_raw, kv_raw, kr, cos, sin, dq, dk, dv], [g_qn, g_kn],
        [_sds(q_raw.shape, BF16), _sds(kv_raw.shape, BF16), _sds((T, LANE), BF16)],
        [_sds(g_qn.shape, F32), _sds(g_kn.shape, F32)], name="mla_qk_norm_rope_bwd", after=token)
    grads["w_qb"] = _mm(dq_raw, qn, ta=True, out_dtype=BF16, name="d_w_qb")
    grads["w_kvb"] = _mm(kvn, dkv_raw, ta=True, out_dtype=BF16, name="d_w_kvb")
    dqn = _mm(dq_raw, full["w_qb"], name="d_qn")
    dkvn = _mm(dkv_raw, full["w_kvb"], tb=True, name="d_kvn")
    dcq, dckv, grads["g_qa"], grads["g_kva"] = _stage(
        _mla_a_bwd_fn, [_cols(z, 256, 0), _cols(z, 256, 1), dqn, dkvn], [g_qa, g_kva],
        [_sds((T, 256), BF16), _sds((T, 256), BF16)], [_sds(g_qa.shape, F32), _sds(g_kva.shape, F32)],
        name="mla_latent_norm_bwd")
    token = reduce_group(1, dcq)
    dz = jnp.concatenate([dcq, dckv, dhq, dhf_f, dhf_b, dhi, dhg, dkr], axis=1)
    grads["w_in"] = _from_z_layout(_mm(dz, h1, ta=True, out_dtype=BF16, name="d_w_in", after=token))
    token = scatter_start(2)
    grad_x, grads["g_mix"] = _mm_fused(
        dz, w_in_zt, lambda acc, x, dx1, g: _norm_bwd_fn(x, acc, dx1, g), [x2d, dx1], [g_mix],
        [F32], [_sds((1, D), F32)], full_rows=True, name="d_h1_norm_mix_bwd", after=token)

    out_g, out_d, out_m, out_v = {}, {}, {}, {}

    def update_group(gi, after):
        mine, theirs = _exchange_wait(swap_started[gi], after, name=f"swap_wait_{gi}")
        for n, a, b in zip(scatter_groups[gi], mine, theirs):
            out_g[n], out_d[n], out_m[n], out_v[n] = (
                unview(n, t) for t in _adamw(view(n, w_named[n]), a, b, view(n, m_named[n]), view(n, v_named[n]),
                                             name="adamw_" + n))
        return out_v[scatter_groups[gi][-1]]

    done = update_group(0, grad_x)
    done = update_group(1, done)
    token = reduce_group(2, done)
    update_group(2, token)

    small = ("g_mix", "g_qa", "g_kva", "g_qn", "g_kn", "g_hgo", "g_ffn", "g_ple")
    small_all = small + ("lb_param",)
    width = -(-max(w_named[n].size for n in small_all) // LANE) * LANE

    def row(a):
        a = a.reshape(1, -1)
        return jnp.pad(a, ((0, 0), (0, width - a.shape[1])))

    dlower = jnp.concatenate([dlow_f.reshape(1, -1), dlow_b.reshape(1, -1)], axis=0)
    pack = jnp.concatenate([row(grads[n]) for n in small] + [row(dlower[0]), row(dlower[1]), row(loss_part)]
                           + [jnp.zeros((5, width), F32)], axis=0)
    red = _allreduce_small(pack)
    loss = red[10, 0]

    outs = _adamw_small(red, lb_full, [w_named[n] for n in small_all], [m_named[n] for n in small_all],
                        [v_named[n] for n in small_all])
    for i, n in enumerate(small_all):
        out_g[n], out_d[n], out_m[n], out_v[n] = outs[4 * i:4 * i + 4]

    return (loss, grad_x[None], *[out_g[n] for n in order], *[out_d[n] for n in order],
            *[out_m[n] for n in order], *[out_v[n] for n in order])
```

```python
import functools
import itertools

import jax
import jax.numpy as jnp
from jax import lax
from jax.experimental import pallas as pl
from jax.experimental.pallas import tpu as pltpu

F32 = jnp.float32
BF16 = jnp.bfloat16
MESH = pl.DeviceIdType.MESH

EPS = 1e-6
ROPE_THETA = 10000.0
MLA_HEADS = 4
QK_NOPE = 128
QK_ROPE = 64
QK_HEAD = QK_NOPE + QK_ROPE
V_HEAD = 128
HG_HEADS = 4
HG_DK = 128
CHUNK = 64
ADAM_LR = 0.001
ADAM_B1 = 0.9
ADAM_B2 = 0.999
ADAM_EPS = 1e-08
ADAM_WD = 0.01
ADAM_STEP = 10

LANE = 128
VMEM_LIMIT = 56 * 1024 * 1024
TOK_TILE = 256
GLA_GROUP = 16
GLA_FWD_HEADS = 4
GLA_BWD_HEADS = 2
ATT_TQ = 1024
ATT_TK = 1024
ATT_CHUNK = 512
LOG2_E = 1.4426950408889634
N_CHIPS = 4
N_DEV = 8


_ANY = pl.BlockSpec(memory_space=pl.ANY)


def _params(dims=None, **kw):
    return pltpu.CompilerParams(dimension_semantics=dims, vmem_limit_bytes=VMEM_LIMIT, **kw)


def _tile_candidates(n, cap):
    out = [d for d in range(LANE, min(n, cap) + 1, LANE) if n % d == 0]
    if n <= cap and n not in out:
        out.append(n)
    return out or [n]


MM_VMEM_BUDGET = 34 * 1024 * 1024
MM_MAX_ROWS = 1536
HBM_BYTES_PER_S = 2.8e12
MXU_FLOPS_PER_S = 8e14
STEP_OVERHEAD_S = 0.35e-6


def _mm_tiles(M, N, K, a_bytes, b_bytes, o_bytes, has_add, full_rows=False, full_k=False):
    best = None
    for tm in _tile_candidates(M, MM_MAX_ROWS):
        for tn in ([N] if full_rows else _tile_candidates(N, N)):
            for tk in ([K] if full_k else _tile_candidates(K, K)):
                ni, nj, nk = M // tm, N // tn, K // tk
                vmem = 2 * (tm * tk * a_bytes + tk * tn * b_bytes + tm * tn * o_bytes * (2 if has_add else 1))
                vmem += tm * tn * 4 * (2 if nk > 1 else 1)
                vmem += (tm * tk * 2 if a_bytes > 2 else 0) + (tk * tn * 2 if b_bytes > 2 else 0)
                if vmem > MM_VMEM_BUDGET:
                    continue
                moved = M * K * a_bytes * (nj if nk > 1 else 1) + K * N * b_bytes * (1 if nj == nk == 1 else ni)
                moved += M * N * o_bytes * (2 if has_add else 1)
                t = max(moved / HBM_BYTES_PER_S, 2 * M * N * K / MXU_FLOPS_PER_S) + ni * nj * nk * STEP_OVERHEAD_S
                if best is None or t < best[0]:
                    best = (t, tm, tn, tk)
    assert best is not None, (M, N, K)
    return best[1:]


def _dot_raw(a, b, kind):
    nb = a.ndim - 2
    batch = ((0,), (0,)) if nb else ((), ())
    ca = nb if kind == "tn" else nb + 1
    cb = nb + 1 if kind == "nt" else nb
    return lax.dot_general(a.astype(BF16), b.astype(BF16), (((ca,), (cb,)), batch), preferred_element_type=F32)


@functools.partial(jax.custom_vjp, nondiff_argnums=(2,))
def _bdot(a, b, kind):
    return _dot_raw(a, b, kind)


def _bdot_fwd(a, b, kind):
    return _dot_raw(a, b, kind), (a, b)


def _bdot_bwd(kind, res, g):
    a, b = res
    if kind == "nn":
        da, db = _bdot(g, b, "nt"), _bdot(a, g, "tn")
    elif kind == "nt":
        da, db = _bdot(g, b, "nn"), _bdot(g, a, "tn")
    else:
        da, db = _bdot(b, g, "nt"), _bdot(a, g, "nn")
    return da.astype(a.dtype), db.astype(b.dtype)


_bdot.defvjp(_bdot_fwd, _bdot_bwd)


def _mm(a, b, *, name, ta=False, tb=False, add=None, out_dtype=F32, after=None):
    K, M = a.shape if ta else a.shape[::-1]
    N, Kb = b.shape if tb else b.shape[::-1]
    assert K == Kb, (a.shape, b.shape, ta, tb)
    tm, tn, tk = _mm_tiles(M, N, K, a.dtype.itemsize, b.dtype.itemsize, jnp.dtype(out_dtype).itemsize, add is not None)
    nk = K // tk
    kind = "tn" if ta else ("nt" if tb else "nn")
    assert not (ta and tb)
    a_spec = pl.BlockSpec((tk, tm), lambda i, j, k: (k, i)) if ta else pl.BlockSpec((tm, tk), lambda i, j, k: (i, k))
    b_spec = pl.BlockSpec((tn, tk), lambda i, j, k: (j, k)) if tb else pl.BlockSpec((tk, tn), lambda i, j, k: (k, j))
    o_spec = pl.BlockSpec((tm, tn), lambda i, j, k: (i, j))
    has_add = add is not None

    def body(*refs):
        a_ref, b_ref = refs[0], refs[1]
        add_ref = refs[2] if has_add else None
        o_ref = refs[n_in]
        part = _dot_raw(a_ref[...], b_ref[...], kind)
        if nk == 1:
            if has_add:
                part = part + add_ref[...].astype(F32)
            o_ref[...] = part.astype(o_ref.dtype)
            return
        acc_ref = refs[-1]
        k = pl.program_id(2)

        @pl.when(k == 0)
        def _():
            acc_ref[...] = part

        @pl.when(k > 0)
        def _():
            acc_ref[...] += part

        @pl.when(k == nk - 1)
        def _():
            r = acc_ref[...]
            if has_add:
                r = r + add_ref[...].astype(F32)
            o_ref[...] = r.astype(o_ref.dtype)

    ins = [a, b] + ([add] if has_add else []) + ([after] if after is not None else [])
    in_specs = [a_spec, b_spec] + ([o_spec] if has_add else []) + ([_ANY] if after is not None else [])
    n_in = len(ins)
    return pl.pallas_call(
        body, name=name, grid=(M // tm, N // tn, nk), in_specs=in_specs, out_specs=o_spec,
        out_shape=jax.ShapeDtypeStruct((M, N), out_dtype),
        scratch_shapes=[pltpu.VMEM((tm, tn), F32)] if nk > 1 else [],
        compiler_params=_params(("parallel", "parallel", "arbitrary")),
    )(*ins)


def _mm_fused(a, b, fn, tiles, params, out_dtypes, sums, *, name, ta=False, tb=False, full_rows=False, after=None,
              b2=None):
    K, M = a.shape if ta else a.shape[::-1]
    N, Kb = b.shape if tb else b.shape[::-1]
    assert K == Kb and not (ta and tb), (a.shape, b.shape, ta, tb)
    per_elem = sum(t.dtype.itemsize for t in tiles) + sum(jnp.dtype(d).itemsize for d in out_dtypes)
    n_b = 1 if b2 is None else 2
    tm, tn, tk = _mm_tiles(M, N, K, a.dtype.itemsize, n_b * b.dtype.itemsize, per_elem, False, full_rows, b2 is not None)
    nk = K // tk
    kind = "tn" if ta else ("nt" if tb else "nn")
    a_spec = pl.BlockSpec((tk, tm), lambda i, j, k: (k, i)) if ta else pl.BlockSpec((tm, tk), lambda i, j, k: (i, k))
    b_spec = pl.BlockSpec((tn, tk), lambda i, j, k: (j, k)) if tb else pl.BlockSpec((tk, tn), lambda i, j, k: (k, j))
    o_spec = pl.BlockSpec((tm, tn), lambda i, j, k: (i, j))
    ins = [a, b] + ([b2] if b2 is not None else []) + list(tiles) + list(params) + ([after] if after is not None else [])
    in_specs = [a_spec] + [b_spec] * n_b + [o_spec] * len(tiles)
    in_specs += [pl.BlockSpec(p.shape, lambda i, j, k, nd=p.ndim: (0,) * nd) for p in params]
    in_specs += [_ANY] if after is not None else []
    n_in, n_t, n_p, n_o = len(ins), len(tiles), len(params), len(out_dtypes)

    def body(*refs):
        outs, sum_refs = refs[n_in:n_in + n_o], refs[n_in + n_o:n_in + n_o + len(sums)]

        def finish(*products):
            res = fn(*products, *[t[...] for t in refs[1 + n_b:1 + n_b + n_t + n_p]])
            for o_ref, v in zip(outs, res[:n_o]):
                o_ref[...] = v.astype(o_ref.dtype)
            first = jnp.logical_and(pl.program_id(0) == 0, pl.program_id(1) == 0)
            for s_ref, v in zip(sum_refs, res[n_o:]):
                @pl.when(first)
                def _(s_ref=s_ref, v=v):
                    s_ref[...] = v

                @pl.when(jnp.logical_not(first))
                def _(s_ref=s_ref, v=v):
                    s_ref[...] += v

        part = _dot_raw(refs[0][...], refs[1][...], kind)
        if nk == 1:
            finish(part, *([_dot_raw(refs[0][...], refs[2][...], kind)] if b2 is not None else []))
            return
        acc_ref = refs[-1]
        k = pl.program_id(2)

        @pl.when(k == 0)
        def _():
            acc_ref[...] = part

        @pl.when(k > 0)
        def _():
            acc_ref[...] += part

        @pl.when(k == nk - 1)
        def _():
            finish(acc_ref[...])

    out_shape = [_sds((M, N), d) for d in out_dtypes] + list(sums)
    out_specs = [o_spec] * n_o + [pl.BlockSpec(s.shape, lambda i, j, k, nd=len(s.shape): (0,) * nd) for s in sums]
    order = ("arbitrary",) * 3 if sums else ("parallel", "parallel", "arbitrary")
    return pl.pallas_call(
        body, name=name, grid=(M // tm, N // tn, nk), in_specs=in_specs, out_specs=out_specs, out_shape=out_shape,
        scratch_shapes=[pltpu.VMEM((tm, tn), F32)] if nk > 1 else [], compiler_params=_params(order),
    )(*ins)


def _cols(arr, width, block):
    return (arr, width, block)


def _stage(fn, tiles, params, out_tiles, out_sums, *, name, tile=TOK_TILE, after=None):
    def tok_spec(shape, width=None, block=0):
        if len(shape) == 2:
            w = shape[1] if width is None else width
            return pl.BlockSpec((tile, w), lambda i: (i, block))
        return pl.BlockSpec((shape[0], tile, shape[2]), lambda i: (0, i, 0))

    arrays, in_specs = [], []
    for t in tiles:
        if isinstance(t, tuple):
            arr, width, block = t
            arrays.append(arr)
            in_specs.append(tok_spec(arr.shape, width, block))
        else:
            arrays.append(t)
            in_specs.append(tok_spec(t.shape))
    n_tok = arrays[0].shape[0] if arrays[0].ndim == 2 else arrays[0].shape[1]
    for p in params:
        arrays.append(p)
        in_specs.append(pl.BlockSpec(p.shape, lambda i, nd=p.ndim: (0,) * nd))
    out_shape = list(out_tiles) + list(out_sums)
    out_specs = [tok_spec(o.shape) for o in out_tiles]
    out_specs += [pl.BlockSpec(o.shape, lambda i, nd=len(o.shape): (0,) * nd) for o in out_sums]
    n_fn, n_ot = len(arrays), len(out_tiles)
    if after is not None:
        arrays.append(after)
        in_specs.append(_ANY)
    n_in = len(arrays)

    def body(*refs):
        res = fn(*[r[...] for r in refs[:n_fn]])
        if not isinstance(res, (tuple, list)):
            res = (res,)
        outs = refs[n_in:]
        for o_ref, r in zip(outs[:n_ot], res[:n_ot]):
            o_ref[...] = r.astype(o_ref.dtype)
        i = pl.program_id(0)
        for o_ref, r in zip(outs[n_ot:], res[n_ot:]):
            @pl.when(i == 0)
            def _(o_ref=o_ref, r=r):
                o_ref[...] = r.astype(o_ref.dtype)

            @pl.when(i > 0)
            def _(o_ref=o_ref, r=r):
                o_ref[...] += r.astype(o_ref.dtype)

    res = pl.pallas_call(
        body, name=name, grid=(n_tok // tile,), in_specs=in_specs, out_specs=out_specs, out_shape=out_shape,
        compiler_params=_params(("arbitrary",)),
    )(*arrays)
    return res


def _sds(shape, dtype):
    return jax.ShapeDtypeStruct(tuple(shape), dtype)


def _sigmoid(x):
    return 1.0 / (1.0 + jnp.exp(-x))


def _sigmoid_t(x):
    return 0.5 * jnp.tanh(0.5 * x) + 0.5


def _rms(x, g):
    return x * lax.rsqrt(jnp.mean(x * x, axis=-1, keepdims=True) + EPS) * g


def _norm_bwd_fn(x, dh, dres, g):
    _, vjp = jax.vjp(_rms, x, g)
    dx, dg = vjp(dh)
    return dx + dres, dg


def _mla_a_fn(cq, ckv, g_qa, g_kva):
    return _rms(cq, g_qa), _rms(ckv, g_kva)


def _mla_a_bwd_fn(cq, ckv, dqn, dkvn, g_qa, g_kva):
    _, vjp = jax.vjp(_mla_a_fn, cq, ckv, g_qa, g_kva)
    return vjp((dqn, dkvn))


def _rope(t, cos, sin):
    t1, t2 = t[:, :QK_ROPE // 2], t[:, QK_ROPE // 2:]
    return jnp.concatenate([t1 * cos - t2 * sin, t1 * sin + t2 * cos], axis=-1)


def _mla_b_fn(q_raw, kv_raw, kr, cos, sin, g_qn, g_kn):
    krope = kr[:, :QK_ROPE]
    qs, ks, vs = [], [], []
    for h in range(MLA_HEADS):
        qh = _rms(q_raw[:, h * QK_HEAD:(h + 1) * QK_HEAD], g_qn)
        kvh = kv_raw[:, h * (QK_NOPE + V_HEAD):(h + 1) * (QK_NOPE + V_HEAD)]
        kh = _rms(jnp.concatenate([kvh[:, :QK_NOPE], krope], axis=-1), g_kn)
        qs.append(jnp.concatenate([qh[:, :QK_NOPE], _rope(qh[:, QK_NOPE:], cos, sin)], axis=-1))
        ks.append(jnp.concatenate([kh[:, :QK_NOPE], _rope(kh[:, QK_NOPE:], cos, sin)], axis=-1))
        vs.append(kvh[:, QK_NOPE:])
    return jnp.stack(qs), jnp.stack(ks), jnp.stack(vs)


def _mla_b_bwd_fn(q_raw, kv_raw, kr, cos, sin, dq, dk, dv, g_qn, g_kn):
    _, vjp = jax.vjp(lambda a, b, c, d, e: _mla_b_fn(a, b, c, cos, sin, d, e), q_raw, kv_raw, kr, g_qn, g_kn)
    return vjp((dq, dk, dv))


def _post_fn(a, o_f, o_b, hg, g_hgo):
    o = o_f + o_b
    parts = [a]
    for h in range(HG_HEADS):
        s = slice(h * HG_DK, (h + 1) * HG_DK)
        gate = hg[:, s]
        parts.append(_rms(o[:, s], g_hgo[:, s]) * (gate * _sigmoid(gate)))
    return jnp.concatenate(parts, axis=-1)


def _post_bwd_fn(o_f, o_b, hg, dr, g_hgo):
    def f(o, hg, g):
        return _post_fn(jnp.zeros_like(o), o, jnp.zeros_like(o), hg, g)[:, o.shape[1]:]
    _, vjp = jax.vjp(f, o_f + o_b, hg, g_hgo)
    return vjp(dr)


def _swiglu_fn(gt, up):
    return gt * _sigmoid(gt) * up


def _resid_norm_fn(acc, x, g):
    x_new = acc + x
    return x_new, _rms(x_new, g)


def _swiglu_bwd_fn(gt, up, dact):
    _, vjp = jax.vjp(_swiglu_fn, gt, up)
    return vjp(dact)


def _ple_loss_fn(x2, pg, pp, target):
    gate = _sigmoid(pg)
    err = x2 + gate * pp - target
    dx3 = err * (1.0 / err.shape[-1])
    loss = 0.5 * jnp.sum(jnp.mean(err * err, axis=-1, keepdims=True), axis=0, keepdims=True)
    return dx3, dx3 * pp * gate * (1.0 - gate), dx3 * gate, loss


def _attention_fwd(q, k, v):
    H, T, D = q.shape
    DV = v.shape[-1]
    tq, ck = min(ATT_TQ, T), min(ATT_CHUNK, T)
    c2 = (D ** -0.5) * LOG2_E

    def body(q_ref, k_ref, v_ref, o_ref, lse_ref):
        q_i = q_ref[0]

        def chunk(c, carry):
            m, l, acc = carry
            rows = pl.ds(pl.multiple_of(c * ck, ck), ck)
            s = _dot_raw(q_i, k_ref[0, rows, :], "nt")
            m_new = jnp.maximum(m, jnp.max(s, axis=-1, keepdims=True))
            p = jnp.exp2((s - m_new) * c2)
            alpha = jnp.exp2((m - m_new) * c2)
            l = l * alpha + jnp.sum(p, axis=-1, keepdims=True)
            acc = acc * alpha + _dot_raw(p, v_ref[0, rows, :], "nn")
            return m_new, l, acc

        init = (jnp.full((tq, 1), -jnp.inf, F32), jnp.zeros((tq, 1), F32), jnp.zeros((tq, DV), F32))
        m, l, acc = lax.fori_loop(0, T // ck, chunk, init, unroll=True)
        o_ref[...] = acc / l
        lse_ref[0] = m * c2 + jnp.log2(l)

    return pl.pallas_call(
        body, name="attention_fwd", grid=(H, T // tq),
        in_specs=[pl.BlockSpec((1, tq, D), lambda h, i: (h, i, 0)),
                  pl.BlockSpec((1, T, D), lambda h, i: (h, 0, 0)),
                  pl.BlockSpec((1, T, DV), lambda h, i: (h, 0, 0))],
        out_specs=[pl.BlockSpec((tq, DV), lambda h, i: (i, h)),
                   pl.BlockSpec((1, tq, 1), lambda h, i: (h, i, 0))],
        out_shape=[_sds((T, H * DV), F32), _sds((H, T, 1), F32)],
        compiler_params=_params(("parallel", "parallel")),
    )(q, k, v)


def _attention_bwd(q, k, v, o, lse2, dmix):
    H, T, D = q.shape
    DV = v.shape[-1]
    tk, cq = min(ATT_TK, T), min(ATT_CHUNK, T)
    scale = D ** -0.5
    c2 = scale * LOG2_E

    def body(q_ref, k_ref, v_ref, o_ref, lse_ref, do_ref, dq_ref, dk_ref, dv_ref, delta_ref):
        j = pl.program_id(1)

        @pl.when(j == 0)
        def _():
            delta = lax.dot_general(jnp.ones((8, DV), F32), do_ref[...] * o_ref[...], (((1,), (1,)), ((), ())),
                                    precision=lax.Precision.HIGHEST, preferred_element_type=F32)
            for i in range(T // cq):
                delta_ref[i] = delta[:, i * cq:(i + 1) * cq]
            dq_ref[0] = jnp.zeros((T, D), F32)

        k_j, v_j = k_ref[0], v_ref[0]
        dk_ref[0] = jnp.zeros((tk, D), F32)
        dv_ref[0] = jnp.zeros((tk, DV), F32)

        def chunk(c, carry):
            rows = pl.ds(pl.multiple_of(c * cq, cq), cq)
            q_c = q_ref[0, rows, :]
            do_c = do_ref[rows, :].astype(BF16)
            st = _dot_raw(k_j, q_c, "nt")
            pt = jnp.exp2(st * c2 - lse_ref[0, c])
            dv_ref[0] += _dot_raw(pt, do_c, "nn")
            dpt = _dot_raw(v_j, do_c, "nt")
            dst = pt * (dpt - delta_ref[c, 0:1, :]) * scale
            dk_ref[0] += _dot_raw(dst, q_c, "nn")
            dq_ref[0, rows, :] += _dot_raw(dst, k_j, "tn")
            return carry

        lax.fori_loop(0, T // cq, chunk, 0, unroll=True)

    return pl.pallas_call(
        body, name="attention_bwd", grid=(H, T // tk),
        in_specs=[pl.BlockSpec((1, T, D), lambda h, j: (h, 0, 0)),
                  pl.BlockSpec((1, tk, D), lambda h, j: (h, j, 0)),
                  pl.BlockSpec((1, tk, DV), lambda h, j: (h, j, 0)),
                  pl.BlockSpec((T, DV), lambda h, j: (0, h)),
                  pl.BlockSpec((1, T // cq, 1, cq), lambda h, j: (h, 0, 0, 0)),
                  pl.BlockSpec((T, DV), lambda h, j: (0, h))],
        out_specs=[pl.BlockSpec((1, T, D), lambda h, j: (h, 0, 0)),
                   pl.BlockSpec((1, tk, D), lambda h, j: (h, j, 0)),
                   pl.BlockSpec((1, tk, DV), lambda h, j: (h, j, 0))],
        out_shape=[_sds((H, T, D), F32), _sds((H, T, D), F32), _sds((H, T, DV), F32)],
        scratch_shapes=[pltpu.VMEM((T // cq, 8, cq), F32)],
        compiler_params=_params(("parallel", "arbitrary")),
    )(q, k, v, o, lse2.reshape(H, T // cq, 1, cq), dmix)


def _split3_dot(ones, x, kind):
    hi = x.astype(BF16)
    rest = x - hi.astype(F32)
    mid = rest.astype(BF16)
    lo = (rest - mid.astype(F32)).astype(BF16)
    return (_dot_raw(ones, hi, kind) + _dot_raw(ones, mid, kind)) + _dot_raw(ones, lo, kind)


@jax.custom_vjp
def _running_sum(x, tri):
    return _split3_dot(tri, x, "nn")


def _running_sum_fwd(x, tri):
    return _split3_dot(tri, x, "nn"), tri


def _running_sum_bwd(tri, g):
    return _split3_dot(tri, g, "tn"), jnp.zeros_like(tri)


_running_sum.defvjp(_running_sum_fwd, _running_sum_bwd)


def _gla_block(hq, hf, hi, lower, st_in, *, rev, dot):
    rows, dk = hq.shape
    G, C = rows // CHUNK, CHUNK
    q = hq * _sigmoid_t(hq)
    f = lower + (1.0 - lower) * _sigmoid_t(hf)
    k = 1.0 - f
    logf = jnp.log2(f)
    q3, k3, v3, lf3 = (t.reshape(G, C, dk) for t in (q, k, hi, logf))
    r = lax.broadcasted_iota(jnp.int32, (C, C), 0)
    c = lax.broadcasted_iota(jnp.int32, (C, C), 1)
    tri = ((r <= c) if rev else (r >= c)).astype(F32)
    b = _running_sum(lf3, jnp.broadcast_to(tri, (G, C, C)))
    tpos = lax.broadcasted_iota(jnp.int32, (1, C, 1), 1)
    first_half = (tpos >= C // 2) if rev else (tpos <= C // 2 - 1)
    b_mid = jnp.sum(jnp.where(first_half, lf3, 0.0), axis=1, keepdims=True)
    b_last = jnp.sum(lf3, axis=1, keepdims=True)
    a = dot(q3 * jnp.exp2(b - b_mid), k3 * jnp.exp2(b_mid - b), "nt") * tri
    o_intra = dot(a, v3, "nn")
    kv_t = dot(v3, k3 * jnp.exp2(b_last - b), "tn")
    decay = jnp.exp2(b_last)
    qd = q3 * jnp.exp2(b)
    st = st_in
    o_inter = [None] * G
    for g in (reversed(range(G)) if rev else range(G)):
        o_inter[g] = dot(qd[g], st, "nt")
        st = st * decay[g] + kv_t[g]
    o = o_intra.reshape(rows, dk) + jnp.concatenate(o_inter, axis=0)
    return o, st


def _gla_fwd(z, lower3, *, rev, col_q, col_f, col_v, hp):
    T = z.shape[0]
    rows = min(GLA_GROUP * CHUNK, T)
    nb = T // rows
    wide = hp * HG_DK
    blk = (lambda n: nb - 1 - n) if rev else (lambda n: n)

    def body(hq_ref, hf_ref, hi_ref, low_ref, o_ref, st_out_ref, st_ref):
        @pl.when(pl.program_id(1) == 0)
        def _():
            st_ref[...] = jnp.zeros_like(st_ref)

        st_in = [st_ref[i] for i in range(hp)]
        heads = []
        for i in range(hp):
            cols = slice(i * HG_DK, (i + 1) * HG_DK)
            heads.append(_gla_block(hq_ref[:, cols], hf_ref[:, cols], hi_ref[:, cols], low_ref[i], st_in[i], rev=rev,
                                    dot=_dot_raw))
        for i, (o, st) in enumerate(heads):
            st_out_ref[i, 0] = st_in[i]
            o_ref[:, i * HG_DK:(i + 1) * HG_DK] = o
            st_ref[i] = st

    def zspec(col):
        return pl.BlockSpec((rows, wide), lambda h, n: (blk(n), col // wide + h))

    return pl.pallas_call(
        body, name="gla_fwd_rev" if rev else "gla_fwd", grid=(HG_HEADS // hp, nb),
        in_specs=[zspec(col_q), zspec(col_f), zspec(col_v), pl.BlockSpec((hp, 1, HG_DK), lambda h, n: (h, 0, 0))],
        out_specs=[pl.BlockSpec((rows, wide), lambda h, n: (blk(n), h)),
                   pl.BlockSpec((hp, 1, HG_DK, HG_DK), lambda h, n: (h, blk(n), 0, 0))],
        out_shape=[_sds((T, HG_HEADS * HG_DK), F32), _sds((HG_HEADS, nb, HG_DK, HG_DK), F32)],
        scratch_shapes=[pltpu.VMEM((hp, HG_DK, HG_DK), F32)],
        compiler_params=_params(("parallel", "arbitrary")),
    )(z, z, z, lower3)


def _gla_bwd(z, lower3, states, do, prev, *, rev, col_q, col_f, col_v, hp):
    T = z.shape[0]
    rows = min(GLA_GROUP * CHUNK, T)
    nb = T // rows
    wide = hp * HG_DK
    blk = (lambda n: n) if rev else (lambda n: nb - 1 - n)
    has_prev = prev is not None
    fn = functools.partial(_gla_block, rev=rev, dot=_bdot)

    def body(*refs):
        hq_ref, hf_ref, hi_ref, low_ref, st_ref, do_ref = refs[:6]
        rest = refs[6:]
        if has_prev:
            pq_ref, pi_ref = rest[:2]
            rest = rest[2:]
        dhq_ref, dhi_ref, dhf_ref, dlow_ref, dst_ref = rest
        n = pl.program_id(1)

        @pl.when(n == 0)
        def _():
            dst_ref[...] = jnp.zeros_like(dst_ref)

        dst_in = [dst_ref[i] for i in range(hp)]
        heads = []
        for i in range(hp):
            cols = slice(i * HG_DK, (i + 1) * HG_DK)
            _, vjp = jax.vjp(fn, hq_ref[:, cols], hf_ref[:, cols], hi_ref[:, cols], low_ref[i], st_ref[i, 0])
            dhq, dhf, dhi, dlow, dst = vjp((do_ref[:, cols], dst_in[i]))
            if has_prev:
                dhq = dhq + pq_ref[:, cols]
                dhi = dhi + pi_ref[:, cols]
            heads.append((dhq, dhf, dhi, dlow, dst))
        for i, (dhq, dhf, dhi, dlow, dst) in enumerate(heads):
            cols = slice(i * HG_DK, (i + 1) * HG_DK)
            dst_ref[i] = dst
            dhq_ref[:, cols] = dhq.astype(dhq_ref.dtype)
            dhi_ref[:, cols] = dhi.astype(dhi_ref.dtype)
            dhf_ref[:, cols] = dhf.astype(dhf_ref.dtype)

        @pl.when(n == 0)
        def _():
            for i in range(hp):
                dlow_ref[i] = heads[i][3]

        @pl.when(n > 0)
        def _():
            for i in range(hp):
                dlow_ref[i] += heads[i][3]

    def zspec(col):
        return pl.BlockSpec((rows, wide), lambda h, n: (blk(n), col // wide + h))

    hspec = pl.BlockSpec((rows, wide), lambda h, n: (blk(n), h))
    in_specs = [zspec(col_q), zspec(col_f), zspec(col_v), pl.BlockSpec((hp, 1, HG_DK), lambda h, n: (h, 0, 0)),
                pl.BlockSpec((hp, 1, HG_DK, HG_DK), lambda h, n: (h, blk(n), 0, 0)), hspec]
    ins = [z, z, z, lower3, states, do]
    if has_prev:
        in_specs += [hspec, hspec]
        ins += list(prev)
    full_wide = HG_HEADS * HG_DK
    acc_dtype = BF16 if has_prev else F32
    return pl.pallas_call(
        body, name="gla_bwd_rev" if rev else "gla_bwd", grid=(HG_HEADS // hp, nb),
        in_specs=in_specs,
        out_specs=[hspec, hspec, hspec, pl.BlockSpec((hp, 1, HG_DK), lambda h, n: (h, 0, 0))],
        out_shape=[_sds((T, full_wide), acc_dtype), _sds((T, full_wide), acc_dtype), _sds((T, full_wide), BF16),
                   _sds((HG_HEADS, 1, HG_DK), F32)],
        scratch_shapes=[pltpu.VMEM((hp, HG_DK, HG_DK), F32)],
        compiler_params=_params(("parallel", "arbitrary")),
    )(*ins)


def _lower_fn(lb):
    e = jnp.exp(lb - jnp.max(lb, axis=0, keepdims=True))
    return (e / jnp.sum(e, axis=0, keepdims=True))[0]


def _lower_bounds(lb):
    def body(lb_ref, o_ref):
        o_ref[...] = _lower_fn(lb_ref[...])
    return pl.pallas_call(body, name="lower_bounds", out_shape=_sds(lb.shape[1:], F32))(lb)


def _row_tile(r, cap=1024):
    best = None
    for t in range(16, min(r, cap) + 1, 16):
        if r % t == 0:
            best = t
    return best if best is not None else r


def _sum4(shards, recv, chip, *, name):
    _, R, C = shards.shape
    tr = _row_tile(R)

    def body(chip_ref, o_ref, r_ref, out_ref):
        out_ref[...] = ((o_ref[0].astype(F32) + r_ref[0].astype(F32)) + r_ref[1].astype(F32)) + r_ref[2].astype(F32)

    grid_spec = pltpu.PrefetchScalarGridSpec(
        num_scalar_prefetch=1, grid=(R // tr,),
        in_specs=[pl.BlockSpec((1, tr, C), lambda i, chip_ref: (chip_ref[0], i, 0)),
                  pl.BlockSpec((3, tr, C), lambda i, chip_ref: (0, i, 0))],
        out_specs=pl.BlockSpec((tr, C), lambda i, chip_ref: (i, 0)))
    return pl.pallas_call(
        body, name=name, grid_spec=grid_spec, out_shape=_sds((R, C), F32), compiler_params=_params(("parallel",)),
    )(chip, shards, recv)


def _adamw_math(w, g, m, v):
    m = ADAM_B1 * m + (1.0 - ADAM_B1) * g
    v = ADAM_B2 * v + (1.0 - ADAM_B2) * (g * g)
    m_hat = m / (1.0 - ADAM_B1 ** ADAM_STEP)
    v_hat = v / (1.0 - ADAM_B2 ** ADAM_STEP)
    delta = -ADAM_LR * (m_hat / (jnp.sqrt(v_hat) + ADAM_EPS) + ADAM_WD * w)
    return delta, m, v


def _adamw(w, g_a, g_b, m, v, *, name):
    R, C = w.shape
    tr = _row_tile(R)
    two = g_b is not None

    def body(*refs):
        w_ref, ga_ref = refs[0], refs[1]
        rest = refs[2:]
        g = ga_ref[...]
        if two:
            g = g + rest[0][...]
            rest = rest[1:]
        m_ref, v_ref, g_out, d_out, m_out, v_out = rest
        delta, m_new, v_new = _adamw_math(w_ref[...], g, m_ref[...], v_ref[...])
        g_out[...] = g
        d_out[...] = delta
        m_out[...] = m_new
        v_out[...] = v_new

    spec = pl.BlockSpec((tr, C), lambda i: (i, 0))
    ins = [w, g_a] + ([g_b] if two else []) + [m, v]
    return pl.pallas_call(
        body, name=name, grid=(R // tr,), in_specs=[spec] * len(ins), out_specs=[spec] * 4,
        out_shape=[_sds((R, C), F32)] * 4, compiler_params=_params(("parallel",)),
    )(*ins)


def _adamw_small(red, lb_full, ws, ms, vs):
    n = len(ws)

    def pieces(shape):
        out = []
        for j, idx in enumerate(itertools.product(*[range(d) for d in shape[:-1]])):
            out.append((idx[:-1] + (slice(idx[-1], idx[-1] + 1), slice(None)), j * shape[-1]))
        return out

    def body(*refs):
        red_ref, lb_ref = refs[0], refs[1]
        w_refs, m_refs, v_refs = refs[2:2 + n], refs[2 + n:2 + 2 * n], refs[2 + 2 * n:2 + 3 * n]
        out_refs = refs[2 + 3 * n:]
        chip = 2 * lax.axis_index("x") + lax.axis_index("y")
        n_f, shard = lb_ref.shape[-1], w_refs[n - 1].shape[-1]
        _, vjp = jax.vjp(_lower_fn, lb_ref[...])
        dlb = vjp(red_ref[8:10, 0:n_f])[0]
        for i in range(n):
            width = w_refs[i].shape[-1]
            for j, (at, lane) in enumerate(pieces(w_refs[i].shape)):
                if i < n - 1:
                    g = red_ref[i:i + 1, lane:lane + width]
                else:
                    row = dlb[j // 2][j % 2:j % 2 + 1]
                    g = sum(jnp.where(chip == q, row[:, q * shard:(q + 1) * shard], 0.0) for q in range(N_CHIPS))
                delta, m_new, v_new = _adamw_math(w_refs[i][at], g, m_refs[i][at], v_refs[i][at])
                for o_ref, val in zip(out_refs[4 * i:4 * i + 4], (g, delta, m_new, v_new)):
                    o_ref[at] = val

    return pl.pallas_call(
        body, name="adamw_small", out_shape=[_sds(w.shape, F32) for w in ws for _ in range(4)],
    )(red, lb_full, *ws, *ms, *vs)


def _chip_peers():
    x, y, c = lax.axis_index("x"), lax.axis_index("y"), lax.axis_index("c")
    return (x, y, c), 2 * x + y, [(1 - x, y), (x, 1 - y), (1 - x, 1 - y)]


_HBM = pl.BlockSpec(memory_space=pltpu.HBM)
_SEM = pl.BlockSpec(memory_space=pltpu.SEMAPHORE)
_EFFECT = pltpu.SideEffectType.DATAFLOW_SIDE_EFFECTING


def _exchange_copies(srcs, lands, sems, mode):
    (x, y, c), me, chips = _chip_peers()
    copies = []
    for t, (src, land) in enumerate(zip(srcs, lands)):
        if mode == "swap":
            copies.append(pltpu.make_async_remote_copy(src, land, sems[0].at[3 * t], sems[1].at[3 * t],
                                                       device_id=(x, y, 1 - c), device_id_type=MESH))
            continue
        for k, (px, py) in enumerate(chips):
            gather = mode == "gather"
            copies.append(pltpu.make_async_remote_copy(
                src if gather else src.at[2 * px + py], land.at[me] if gather else land.at[k],
                sems[0].at[3 * t + k], sems[1].at[3 * t + k], device_id=(px, py, c), device_id_type=MESH))
        if mode == "gather":
            copies.append(pltpu.make_async_copy(src, land.at[me], sems[2].at[t]))
    return copies


def _exchange_start(srcs, *, mode, name, after=None):
    n = len(srcs)
    n_sem = 3 if mode == "gather" else 2
    n_in = 2 * n + (after is not None)
    land_shape = {"gather": lambda s: (N_CHIPS,) + s.shape, "scatter": lambda s: (3,) + s.shape[1:], "swap": lambda s: s.shape}
    lands = [_sds(land_shape[mode](s), s.dtype) for s in srcs]

    def body(*refs):
        for cp in _exchange_copies(refs[:n], refs[n:2 * n], refs[n_in:n_in + n_sem], mode):
            cp.start()
        token = refs[-1]
        token[...] = jnp.zeros_like(token)

    sem_shapes = [pltpu.SemaphoreType.DMA((3 * n,)), pltpu.SemaphoreType.DMA((3 * n,))]
    sem_shapes += [pltpu.SemaphoreType.DMA((n,))] if mode == "gather" else []
    thru = [pltpu.HBM(s.shape, s.dtype) for s in srcs] + [pltpu.HBM(l.shape, l.dtype) for l in lands]
    res = pl.pallas_call(
        body, name=name, in_specs=[_HBM] * (2 * n) + [_ANY] * (after is not None),
        out_specs=[_SEM] * n_sem + [_HBM] * (2 * n) + [pl.BlockSpec(memory_space=pltpu.VMEM)],
        out_shape=sem_shapes + thru + [_sds((8, LANE), F32)], input_output_aliases={i: n_sem + i for i in range(2 * n)},
        compiler_params=pltpu.CompilerParams(has_side_effects=_EFFECT),
    )(*[pltpu.with_memory_space_constraint(s, pltpu.HBM) for s in srcs],
      *[pltpu.with_memory_space_constraint(lax.empty(l.shape, l.dtype), pltpu.HBM) for l in lands],
      *([after] if after is not None else []))
    return (res[:n_sem], res[n_sem:n_sem + n], res[n_sem + n:n_sem + 2 * n], mode), res[-1]


def _exchange_wait(started, after, *, name):
    sems, srcs, lands, mode = started
    n, n_sem = len(srcs), len(sems)
    after = list(after) if isinstance(after, (list, tuple)) else [after]

    def body(*refs):
        for cp in _exchange_copies(refs[:n], refs[n:2 * n], refs[2 * n:2 * n + n_sem], mode):
            cp.wait()

    res = pl.pallas_call(
        body, name=name, in_specs=[_HBM] * (2 * n) + [_SEM] * n_sem + [_ANY] * len(after), out_specs=[_HBM] * (2 * n),
        out_shape=[pltpu.HBM(a.shape, a.dtype) for a in list(srcs) + list(lands)],
        input_output_aliases={i: i for i in range(2 * n)},
        compiler_params=pltpu.CompilerParams(has_side_effects=_EFFECT),
    )(*srcs, *lands, *sems, *after)
    return res[:n], res[n:]


def _allreduce_small(pack):
    R, C = pack.shape

    def body(in_ref, out_ref, slots, send_sems, recv_sems):
        x, y, c = lax.axis_index("x"), lax.axis_index("y"), lax.axis_index("c")
        me = 4 * x + 2 * y + c
        slots[me] = in_ref[...]
        copies = []
        for k in range(1, N_DEV):
            peer = (x ^ ((k >> 2) & 1), y ^ ((k >> 1) & 1), c ^ (k & 1))
            cp = pltpu.make_async_remote_copy(in_ref, slots.at[me], send_sems.at[k - 1], recv_sems.at[k - 1],
                                              device_id=peer, device_id_type=MESH)
            cp.start()
            copies.append(cp)
        for cp in copies:
            cp.wait()
        acc = slots[0]
        for d in range(1, N_DEV):
            acc = acc + slots[d]
        out_ref[...] = acc

    return pl.pallas_call(
        body, name="allreduce_small", out_shape=_sds((R, C), F32),
        in_specs=[pl.BlockSpec(memory_space=pltpu.VMEM)], out_specs=pl.BlockSpec(memory_space=pltpu.VMEM),
        scratch_shapes=[pltpu.VMEM((N_DEV, R, C), F32), pltpu.SemaphoreType.DMA((N_DEV - 1,)),
                        pltpu.SemaphoreType.DMA((N_DEV - 1,))],
        compiler_params=_params(),
    )(pack)


_Z_CQ, _Z_CKV, _Z_HQ, _Z_HFF, _Z_HFB, _Z_HI, _Z_HG, _Z_KR, _Z_END = 0, 256, 512, 1024, 1536, 2048, 2560, 3072, 3200


def _to_z_layout(wt):
    pad = jnp.zeros((_Z_END - _Z_KR - QK_ROPE, wt.shape[1]), wt.dtype)
    return jnp.concatenate([wt[:512], wt[512 + QK_ROPE:], wt[512:512 + QK_ROPE], pad], axis=0)


def _from_z_layout(wt):
    return jnp.concatenate([wt[:512], wt[_Z_KR:_Z_KR + QK_ROPE], wt[512:_Z_KR]], axis=0)


def _col_shards_to_full(g):
    return jnp.transpose(g, (1, 0, 2)).reshape(g.shape[1], -1)


def _full_to_col_shards(w):
    r, c = w.shape
    return jnp.transpose(w.reshape(r, N_CHIPS, c // N_CHIPS), (1, 0, 2))


def _full_to_row_shards(w):
    r, c = w.shape
    return w.reshape(N_CHIPS, r // N_CHIPS, c)


def kernel(x, p, positions, g_mix, w_in, g_qa, g_kva, w_qb, w_kvb, g_qn, g_kn, lb_param, g_hgo, w_o, g_ffn, w_gate, w_up, w_down, g_ple, w_ple_gate, w_ple_proj, loss_target, m_g_mix, m_w_in, m_g_qa, m_g_kva, m_w_qb, m_w_kvb, m_g_qn, m_g_kn, m_lb_param, m_g_hgo, m_w_o, m_g_ffn, m_w_gate, m_w_up, m_w_down, m_g_ple, m_w_ple_gate, m_w_ple_proj, v_g_mix, v_w_in, v_g_qa, v_g_kva, v_w_qb, v_w_kvb, v_g_qn, v_g_kn, v_lb_param, v_g_hgo, v_w_o, v_g_ffn, v_w_gate, v_w_up, v_w_down, v_g_ple, v_w_ple_gate, v_w_ple_proj):
    w_named = dict(g_mix=g_mix, w_in=w_in, g_qa=g_qa, g_kva=g_kva, w_qb=w_qb, w_kvb=w_kvb, g_qn=g_qn, g_kn=g_kn,
                   lb_param=lb_param, g_hgo=g_hgo, w_o=w_o, g_ffn=g_ffn, w_gate=w_gate, w_up=w_up, w_down=w_down,
                   g_ple=g_ple, w_ple_gate=w_ple_gate, w_ple_proj=w_ple_proj)
    m_named = dict(g_mix=m_g_mix, w_in=m_w_in, g_qa=m_g_qa, g_kva=m_g_kva, w_qb=m_w_qb, w_kvb=m_w_kvb, g_qn=m_g_qn,
                   g_kn=m_g_kn, lb_param=m_lb_param, g_hgo=m_g_hgo, w_o=m_w_o, g_ffn=m_g_ffn, w_gate=m_w_gate,
                   w_up=m_w_up, w_down=m_w_down, g_ple=m_g_ple, w_ple_gate=m_w_ple_gate, w_ple_proj=m_w_ple_proj)
    v_named = dict(g_mix=v_g_mix, w_in=v_w_in, g_qa=v_g_qa, g_kva=v_g_kva, w_qb=v_w_qb, w_kvb=v_w_kvb, g_qn=v_g_qn,
                   g_kn=v_g_kn, lb_param=v_lb_param, g_hgo=v_g_hgo, w_o=v_w_o, g_ffn=v_g_ffn, w_gate=v_w_gate,
                   w_up=v_w_up, w_down=v_w_down, g_ple=v_g_ple, w_ple_gate=v_w_ple_gate, w_ple_proj=v_w_ple_proj)
    order = list(w_named)
    transposed = ("w_in", "w_qb", "w_gate", "w_up")
    col_sharded = ("w_kvb", "w_ple_proj")
    row_sharded = ("w_o", "w_down", "w_ple_gate")
    big = transposed + col_sharded + row_sharded

    def view(n, a):
        return jnp.transpose(a[0]) if n in transposed else a[0]

    def unview(n, a):
        return (jnp.transpose(a) if n in transposed else a)[None]

    def to_shards(n, g):
        return _full_to_col_shards(g) if n in col_sharded else _full_to_row_shards(g)

    x2d, p2d, tgt = x[0], p[0, 0], loss_target[0]
    T, D = x2d.shape

    lb_flat = lb_param.reshape(-1, lb_param.shape[-1])
    gather_groups = (("w_in",), ("w_qb", "w_kvb"), ("w_o", "w_gate", "w_up"), ("w_down", "w_ple_gate", "w_ple_proj"))
    gather_started = []

    casts = {n: view(n, w_named[n]).astype(BF16) for n in big}

    def gather_start(gi, after):
        srcs = [casts[n] for n in gather_groups[gi]] + ([lb_flat] if gi == 0 else [])
        started, token = _exchange_start(srcs, mode="gather", name=f"gather_start_{gi}", after=after)
        gather_started.append(started)
        return token

    full = {}

    def gather_wait(gi, after):
        _, got = _exchange_wait(gather_started[gi], after, name=f"gather_wait_{gi}")
        for n, g in zip(gather_groups[gi], got):
            full[n] = _col_shards_to_full(g) if n in col_sharded else g.reshape(-1, g.shape[-1])
        return got

    g_hgo_row = g_hgo.reshape(1, -1)

    inv_freq = ROPE_THETA ** (-jnp.arange(0, QK_ROPE, 2, dtype=F32) / QK_ROPE)
    ang = positions[0].astype(F32)[:, None] * inv_freq
    cos, sin = jnp.cos(ang), jnp.sin(ang)
    token = gather_start(0, None)
    h1 = _stage(_rms, [x2d], [g_mix], [_sds((T, D), BF16)], [], name="norm_mix", after=token)[0]
    got = gather_wait(0, [h1, cos, sin] + [casts[n] for g in gather_groups[1:] for n in g])
    token = got[0]
    for gi in range(1, len(gather_groups)):
        token = gather_start(gi, token)
    lb_full = _col_shards_to_full(got[-1]).reshape(lb_param.shape[0], lb_param.shape[1], -1)
    w_in_zt = _to_z_layout(full["w_in"])
    z = _mm(h1, w_in_zt, tb=True, name="in_proj", after=token)
    qn, kvn = _stage(_mla_a_fn, [_cols(z, 256, 0), _cols(z, 256, 1)], [g_qa, g_kva],
                     [_sds((T, 256), BF16), _sds((T, 256), BF16)], [], name="mla_latent_norm")
    gather_wait(1, qn)
    q_raw = _mm(qn, full["w_qb"], tb=True, name="q_up")
    kv_raw = _mm(kvn, full["w_kvb"], name="kv_up")
    kr = _cols(z, LANE, _Z_KR // LANE)
    q, k, v = _stage(_mla_b_fn, [q_raw, kv_raw, kr, cos, sin], [g_qn, g_kn],
                     [_sds((MLA_HEADS, T, QK_HEAD), BF16), _sds((MLA_HEADS, T, QK_HEAD), BF16),
                      _sds((MLA_HEADS, T, V_HEAD), BF16)], [], name="mla_qk_norm_rope")
    att, lse = _attention_fwd(q, k, v)

    lower = _lower_bounds(lb_full)
    lower3 = lower.reshape(2, HG_HEADS, 1, HG_DK)
    o_f, st_f = _gla_fwd(z, lower3[0], rev=False, col_q=_Z_HQ, col_f=_Z_HFF, col_v=_Z_HI, hp=GLA_FWD_HEADS)
    o_b, st_b = _gla_fwd(z, lower3[1], rev=True, col_q=_Z_HQ, col_f=_Z_HFB, col_v=_Z_HI, hp=GLA_FWD_HEADS)
    hg = _cols(z, 512, _Z_HG // 512)
    mix = _stage(_post_fn, [att, o_f, o_b, hg], [g_hgo_row], [_sds((T, att.shape[1] + o_f.shape[1]), BF16)], [],
                 name="mix_out")[0]
    gather_wait(2, mix)
    x1, h2 = _mm_fused(mix, full["w_o"], _resid_norm_fn, [x2d], [g_ffn], [F32, BF16], [], full_rows=True,
                       name="out_proj")
    gt, up, act = _mm_fused(h2, full["w_gate"], lambda gt, up: (gt, up, _swiglu_fn(gt, up)), [], [], [BF16, BF16, BF16],
                            [], tb=True, b2=full["w_up"], name="ffn_gate_up")
    gather_wait(3, act)
    x2, h3 = _mm_fused(act, full["w_down"], _resid_norm_fn, [x1], [g_ple], [F32, BF16], [], full_rows=True,
                       name="ffn_down")
    pp = _mm(p2d, full["w_ple_proj"], name="ple_proj")
    dx3, dpg, dpp, loss_part = _mm_fused(
        h3, full["w_ple_gate"], lambda acc, pp, x2, tgt: _ple_loss_fn(x2, acc, pp, tgt), [pp, x2, tgt], [],
        [F32, BF16, BF16], [_sds((1, 1), F32)], full_rows=True, name="ple_gate_loss")

    grads = {}
    scatter_groups = (("w_ple_proj", "w_ple_gate", "w_down"), ("w_gate", "w_up", "w_o"), ("w_qb", "w_kvb", "w_in"))
    scatter_started = []

    def scatter_start(gi):
        srcs = [to_shards(n, grads[n]) for n in scatter_groups[gi]]
        started, token = _exchange_start(srcs, mode="scatter", name=f"scatter_start_{gi}")
        scatter_started.append(started)
        return token

    chip = 2 * lax.axis_index("x") + lax.axis_index("y")
    swap_started = []

    def reduce_group(gi, after):
        shards, recvs = _exchange_wait(scatter_started[gi], after, name=f"scatter_wait_{gi}")
        sums = [_sum4(s, r, chip.reshape(1), name="sum_" + n) for n, s, r in zip(scatter_groups[gi], shards, recvs)]
        started, token = _exchange_start(sums, mode="swap", name=f"swap_start_{gi}")
        swap_started.append(started)
        return token

    grads["w_ple_proj"] = _mm(p2d, dpp, ta=True, out_dtype=BF16, name="d_w_ple_proj")
    grads["w_ple_gate"] = _mm(h3, dpg, ta=True, out_dtype=BF16, name="d_w_ple_gate")
    dx2, grads["g_ple"] = _mm_fused(
        dpg, full["w_ple_gate"], lambda acc, x2, dx3, g: _norm_bwd_fn(x2, acc, dx3, g), [x2, dx3], [g_ple],
        [F32], [_sds((1, D), F32)], tb=True, full_rows=True, name="d_h3_norm_ple_bwd")
    dgt, dup = _mm_fused(dx2, full["w_down"], lambda acc, gt, up: _swiglu_bwd_fn(gt.astype(F32), up.astype(F32), acc), [gt, up], [],
                         [BF16, BF16], [], tb=True, name="d_act_swiglu_bwd")
    grads["w_down"] = _mm(act, dx2, ta=True, out_dtype=BF16, name="d_w_down")
    token = scatter_start(0)
    grads["w_gate"] = _mm(dgt, h2, ta=True, out_dtype=BF16, name="d_w_gate")
    grads["w_up"] = _mm(dup, h2, ta=True, out_dtype=BF16, name="d_w_up")
    dh2 = _mm(dgt, full["w_gate"], name="d_h2_gate", after=token)
    dx1, grads["g_ffn"] = _mm_fused(
        dup, full["w_up"], lambda acc, dh2, x1, dx2, g: _norm_bwd_fn(x1, acc + dh2, dx2, g), [dh2, x1, dx2], [g_ffn],
        [F32], [_sds((1, D), F32)], full_rows=True, name="d_h2_norm_ffn_bwd")
    dmix = _mm(dx1, full["w_o"], tb=True, name="d_mix")
    grads["w_o"] = _mm(mix, dx1, ta=True, out_dtype=BF16, name="d_w_o")

    token = scatter_start(1)
    half = MLA_HEADS * V_HEAD
    do, dhg, dg_hgo = _stage(_post_bwd_fn, [o_f, o_b, hg, _cols(dmix, half, 1)], [g_hgo_row],
                             [_sds((T, half), F32), _sds((T, half), BF16)], [_sds((1, half), F32)], name="mix_out_bwd",
                             after=token)
    grads["g_hgo"] = dg_hgo
    dhq_f, dhi_f, dhf_f, dlow_f = _gla_bwd(z, lower3[0], st_f, do, None, rev=False,
                                           col_q=_Z_HQ, col_f=_Z_HFF, col_v=_Z_HI, hp=GLA_BWD_HEADS)
    dhq, dhi, dhf_b, dlow_b = _gla_bwd(z, lower3[1], st_b, do, (dhq_f, dhi_f), rev=True,
                                       col_q=_Z_HQ, col_f=_Z_HFB, col_v=_Z_HI, hp=GLA_BWD_HEADS)

    dq, dk, dv = _attention_bwd(q, k, v, att, lse, dmix)
    token = reduce_group(0, dq)
    dq_raw, dkv_raw, dkr, grads["g_qn"], grads["g_kn"] = _stage(
        _mla_b_bwd_fn, [q_raw, kv_raw, kr, cos, sin, dq, dk, dv], [g_qn, g_kn],
        [_sds(q_raw.shape, BF16), _sds(kv_raw.shape, BF16), _sds((T, LANE), BF16)],
        [_sds(g_qn.shape, F32), _sds(g_kn.shape, F32)], name="mla_qk_norm_rope_bwd", after=token)
    grads["w_qb"] = _mm(dq_raw, qn, ta=True, out_dtype=BF16, name="d_w_qb")
    grads["w_kvb"] = _mm(kvn, dkv_raw, ta=True, out_dtype=BF16, name="d_w_kvb")
    dqn = _mm(dq_raw, full["w_qb"], name="d_qn")
    dkvn = _mm(dkv_raw, full["w_kvb"], tb=True, name="d_kvn")
    dcq, dckv, grads["g_qa"], grads["g_kva"] = _stage(
        _mla_a_bwd_fn, [_cols(z, 256, 0), _cols(z, 256, 1), dqn, dkvn], [g_qa, g_kva],
        [_sds((T, 256), BF16), _sds((T, 256), BF16)], [_sds(g_qa.shape, F32), _sds(g_kva.shape, F32)],
        name="mla_latent_norm_bwd")
    token = reduce_group(1, dcq)
    dz = jnp.concatenate([dcq, dckv, dhq, dhf_f, dhf_b, dhi, dhg, dkr], axis=1)
    grads["w_in"] = _from_z_layout(_mm(dz, h1, ta=True, out_dtype=BF16, name="d_w_in", after=token))
    token = scatter_start(2)
    grad_x, grads["g_mix"] = _mm_fused(
        dz, w_in_zt, lambda acc, x, dx1, g: _norm_bwd_fn(x, acc, dx1, g), [x2d, dx1], [g_mix],
        [F32], [_sds((1, D), F32)], full_rows=True, name="d_h1_norm_mix_bwd", after=token)

    out_g, out_d, out_m, out_v = {}, {}, {}, {}

    def update_group(gi, after):
        mine, theirs = _exchange_wait(swap_started[gi], after, name=f"swap_wait_{gi}")
        for n, a, b in zip(scatter_groups[gi], mine, theirs):
            out_g[n], out_d[n], out_m[n], out_v[n] = (
                unview(n, t) for t in _adamw(view(n, w_named[n]), a, b, view(n, m_named[n]), view(n, v_named[n]),
                                             name="adamw_" + n))
        return out_v[scatter_groups[gi][-1]]

    done = update_group(0, grad_x)
    done = update_group(1, done)
    token = reduce_group(2, done)
    update_group(2, token)

    small = ("g_mix", "g_qa", "g_kva", "g_qn", "g_kn", "g_hgo", "g_ffn", "g_ple")
    small_all = small + ("lb_param",)
    width = -(-max(w_named[n].size for n in small_all) // LANE) * LANE

    def row(a):
        a = a.reshape(1, -1)
        return jnp.pad(a, ((0, 0), (0, width - a.shape[1])))

    dlower = jnp.concatenate([dlow_f.reshape(1, -1), dlow_b.reshape(1, -1)], axis=0)
    pack = jnp.concatenate([row(grads[n]) for n in small] + [row(dlower[0]), row(dlower[1]), row(loss_part)]
                           + [jnp.zeros((5, width), F32)], axis=0)
    red = _allreduce_small(pack)
    loss = red[10, 0]

    outs = _adamw_small(red, lb_full, [w_named[n] for n in small_all], [m_named[n] for n in small_all],
                        [v_named[n] for n in small_all])
    for i, n in enumerate(small_all):
        out_g[n], out_d[n], out_m[n], out_v[n] = outs[4 * i:4 * i + 4]

    return (loss, grad_x[None], *[out_g[n] for n in order], *[out_d[n] for n in order],
            *[out_m[n] for n in order], *[out_v[n] for n in order])
```

```python
import functools
import itertools

import jax
import jax.numpy as jnp
from jax import lax
from jax.experimental import pallas as pl
from jax.experimental.pallas import tpu as pltpu

F32 = jnp.float32
BF16 = jnp.bfloat16
MESH = pl.DeviceIdType.MESH

EPS = 1e-6
ROPE_THETA = 10000.0
MLA_HEADS = 4
QK_NOPE = 128
QK_ROPE = 64
QK_HEAD = QK_NOPE + QK_ROPE
V_HEAD = 128
HG_HEADS = 4
HG_DK = 128
CHUNK = 64
ADAM_LR = 0.001
ADAM_B1 = 0.9
ADAM_B2 = 0.999
ADAM_EPS = 1e-08
ADAM_WD = 0.01
ADAM_STEP = 10

LANE = 128
VMEM_LIMIT = 56 * 1024 * 1024
TOK_TILE = 256
GLA_GROUP = 16
GLA_FWD_HEADS = 4
GLA_BWD_HEADS = 2
ATT_TQ = 1024
ATT_TK = 1024
ATT_CHUNK = 512
LOG2_E = 1.4426950408889634
N_CHIPS = 4
N_DEV = 8


_ANY = pl.BlockSpec(memory_space=pl.ANY)


def _params(dims=None, **kw):
    return pltpu.CompilerParams(dimension_semantics=dims, vmem_limit_bytes=VMEM_LIMIT, **kw)


def _tile_candidates(n, cap):
    out = [d for d in range(LANE, min(n, cap) + 1, LANE) if n % d == 0]
    if n <= cap and n not in out:
        out.append(n)
    return out or [n]


MM_VMEM_BUDGET = 40 * 1024 * 1024
MM_MIN_ROWS = 256
MM_MAX_ROWS = 1536
HBM_BYTES_PER_S = 2.8e12
MXU_FLOPS_PER_S = 8e14
STEP_OVERHEAD_S = 0.35e-6


def _mm_tiles(M, N, K, a_bytes, b_bytes, o_bytes, has_add, full_rows=False, full_k=False):
    best = None
    for tm in [t for t in _tile_candidates(M, MM_MAX_ROWS) if t >= min(M, MM_MIN_ROWS)]:
        for tn in ([N] if full_rows else _tile_candidates(N, N)):
            for tk in ([K] if full_k else _tile_candidates(K, K)):
                ni, nj, nk = M // tm, N // tn, K // tk
                vmem = 2 * (tm * tk * a_bytes + tk * tn * b_bytes + tm * tn * o_bytes * (2 if has_add else 1))
                vmem += tm * tn * 4 * (2 if nk > 1 else 1)
                vmem += (tm * tk * 2 if a_bytes > 2 else 0) + (tk * tn * 2 if b_bytes > 2 else 0)
                if vmem > MM_VMEM_BUDGET:
                    continue
                moved = M * K * a_bytes * (nj if nk > 1 else 1) + K * N * b_bytes * (1 if nj == nk == 1 else ni)
                moved += M * N * o_bytes * (2 if has_add else 1)
                t = max(moved / HBM_BYTES_PER_S, 2 * M * N * K / MXU_FLOPS_PER_S) + ni * nj * nk * STEP_OVERHEAD_S
                if best is None or t < best[0]:
                    best = (t, tm, tn, tk)
    assert best is not None, (M, N, K)
    return best[1:]


def _dot_raw(a, b, kind):
    nb = a.ndim - 2
    batch = ((0,), (0,)) if nb else ((), ())
    ca = nb if kind == "tn" else nb + 1
    cb = nb + 1 if kind == "nt" else nb
    return lax.dot_general(a.astype(BF16), b.astype(BF16), (((ca,), (cb,)), batch), preferred_element_type=F32)


@functools.partial(jax.custom_vjp, nondiff_argnums=(2,))
def _bdot(a, b, kind):
    return _dot_raw(a, b, kind)


def _bdot_fwd(a, b, kind):
    return _dot_raw(a, b, kind), (a, b)


def _bdot_bwd(kind, res, g):
    a, b = res
    if kind == "nn":
        da, db = _bdot(g, b, "nt"), _bdot(a, g, "tn")
    elif kind == "nt":
        da, db = _bdot(g, b, "nn"), _bdot(g, a, "tn")
    else:
        da, db = _bdot(b, g, "nt"), _bdot(a, g, "nn")
    return da.astype(a.dtype), db.astype(b.dtype)


_bdot.defvjp(_bdot_fwd, _bdot_bwd)


def _mm(a, b, *, name, ta=False, tb=False, add=None, out_dtype=F32, after=None):
    K, M = a.shape if ta else a.shape[::-1]
    N, Kb = b.shape if tb else b.shape[::-1]
    assert K == Kb, (a.shape, b.shape, ta, tb)
    tm, tn, tk = _mm_tiles(M, N, K, a.dtype.itemsize, b.dtype.itemsize, jnp.dtype(out_dtype).itemsize, add is not None)
    nk = K // tk
    kind = "tn" if ta else ("nt" if tb else "nn")
    assert not (ta and tb)
    a_spec = pl.BlockSpec((tk, tm), lambda i, j, k: (k, i)) if ta else pl.BlockSpec((tm, tk), lambda i, j, k: (i, k))
    b_spec = pl.BlockSpec((tn, tk), lambda i, j, k: (j, k)) if tb else pl.BlockSpec((tk, tn), lambda i, j, k: (k, j))
    o_spec = pl.BlockSpec((tm, tn), lambda i, j, k: (i, j))
    has_add = add is not None

    def body(*refs):
        a_ref, b_ref = refs[0], refs[1]
        add_ref = refs[2] if has_add else None
        o_ref = refs[n_in]
        part = _dot_raw(a_ref[...], b_ref[...], kind)
        if nk == 1:
            if has_add:
                part = part + add_ref[...].astype(F32)
            o_ref[...] = part.astype(o_ref.dtype)
            return
        acc_ref = refs[-1]
        k = pl.program_id(2)

        @pl.when(k == 0)
        def _():
            acc_ref[...] = part

        @pl.when(k > 0)
        def _():
            acc_ref[...] += part

        @pl.when(k == nk - 1)
        def _():
            r = acc_ref[...]
            if has_add:
                r = r + add_ref[...].astype(F32)
            o_ref[...] = r.astype(o_ref.dtype)

    ins = [a, b] + ([add] if has_add else []) + ([after] if after is not None else [])
    in_specs = [a_spec, b_spec] + ([o_spec] if has_add else []) + ([_ANY] if after is not None else [])
    n_in = len(ins)
    return pl.pallas_call(
        body, name=name, grid=(M // tm, N // tn, nk), in_specs=in_specs, out_specs=o_spec,
        out_shape=jax.ShapeDtypeStruct((M, N), out_dtype),
        scratch_shapes=[pltpu.VMEM((tm, tn), F32)] if nk > 1 else [],
        compiler_params=_params(("parallel", "parallel", "arbitrary")),
    )(*ins)


def _mm_fused(a, b, fn, tiles, params, out_dtypes, sums, *, name, ta=False, tb=False, full_rows=False, after=None,
              b2=None):
    K, M = a.shape if ta else a.shape[::-1]
    N, Kb = b.shape if tb else b.shape[::-1]
    assert K == Kb and not (ta and tb), (a.shape, b.shape, ta, tb)
    per_elem = sum(t.dtype.itemsize for t in tiles) + sum(jnp.dtype(d).itemsize for d in out_dtypes)
    n_b = 1 if b2 is None else 2
    tm, tn, tk = _mm_tiles(M, N, K, a.dtype.itemsize, n_b * b.dtype.itemsize, per_elem, False, full_rows, b2 is not None)
    nk = K // tk
    kind = "tn" if ta else ("nt" if tb else "nn")
    a_spec = pl.BlockSpec((tk, tm), lambda i, j, k: (k, i)) if ta else pl.BlockSpec((tm, tk), lambda i, j, k: (i, k))
    b_spec = pl.BlockSpec((tn, tk), lambda i, j, k: (j, k)) if tb else pl.BlockSpec((tk, tn), lambda i, j, k: (k, j))
    o_spec = pl.BlockSpec((tm, tn), lambda i, j, k: (i, j))
    ins = [a, b] + ([b2] if b2 is not None else []) + list(tiles) + list(params) + ([after] if after is not None else [])
    in_specs = [a_spec] + [b_spec] * n_b + [o_spec] * len(tiles)
    in_specs += [pl.BlockSpec(p.shape, lambda i, j, k, nd=p.ndim: (0,) * nd) for p in params]
    in_specs += [_ANY] if after is not None else []
    n_in, n_t, n_p, n_o = len(ins), len(tiles), len(params), len(out_dtypes)

    def body(*refs):
        outs, sum_refs = refs[n_in:n_in + n_o], refs[n_in + n_o:n_in + n_o + len(sums)]

        def finish(*products):
            res = fn(*products, *[t[...] for t in refs[1 + n_b:1 + n_b + n_t + n_p]])
            for o_ref, v in zip(outs, res[:n_o]):
                o_ref[...] = v.astype(o_ref.dtype)
            first = jnp.logical_and(pl.program_id(0) == 0, pl.program_id(1) == 0)
            for s_ref, v in zip(sum_refs, res[n_o:]):
                @pl.when(first)
                def _(s_ref=s_ref, v=v):
                    s_ref[...] = v

                @pl.when(jnp.logical_not(first))
                def _(s_ref=s_ref, v=v):
                    s_ref[...] += v

        part = _dot_raw(refs[0][...], refs[1][...], kind)
        if nk == 1:
            finish(part, *([_dot_raw(refs[0][...], refs[2][...], kind)] if b2 is not None else []))
            return
        acc_ref = refs[-1]
        k = pl.program_id(2)

        @pl.when(k == 0)
        def _():
            acc_ref[...] = part

        @pl.when(k > 0)
        def _():
            acc_ref[...] += part

        @pl.when(k == nk - 1)
        def _():
            finish(acc_ref[...])

    out_shape = [_sds((M, N), d) for d in out_dtypes] + list(sums)
    out_specs = [o_spec] * n_o + [pl.BlockSpec(s.shape, lambda i, j, k, nd=len(s.shape): (0,) * nd) for s in sums]
    order = ("arbitrary",) * 3 if sums else ("parallel", "parallel", "arbitrary")
    return pl.pallas_call(
        body, name=name, grid=(M // tm, N // tn, nk), in_specs=in_specs, out_specs=out_specs, out_shape=out_shape,
        scratch_shapes=[pltpu.VMEM((tm, tn), F32)] if nk > 1 else [], compiler_params=_params(order),
    )(*ins)


def _cols(arr, width, block):
    return (arr, width, block)


def _stage(fn, tiles, params, out_tiles, out_sums, *, name, tile=TOK_TILE, after=None):
    def tok_spec(shape, width=None, block=0):
        if len(shape) == 2:
            w = shape[1] if width is None else width
            return pl.BlockSpec((tile, w), lambda i: (i, block))
        return pl.BlockSpec((shape[0], tile, shape[2]), lambda i: (0, i, 0))

    arrays, in_specs = [], []
    for t in tiles:
        if isinstance(t, tuple):
            arr, width, block = t
            arrays.append(arr)
            in_specs.append(tok_spec(arr.shape, width, block))
        else:
            arrays.append(t)
            in_specs.append(tok_spec(t.shape))
    n_tok = arrays[0].shape[0] if arrays[0].ndim == 2 else arrays[0].shape[1]
    for p in params:
        arrays.append(p)
        in_specs.append(pl.BlockSpec(p.shape, lambda i, nd=p.ndim: (0,) * nd))
    out_shape = list(out_tiles) + list(out_sums)
    out_specs = [tok_spec(o.shape) for o in out_tiles]
    out_specs += [pl.BlockSpec(o.shape, lambda i, nd=len(o.shape): (0,) * nd) for o in out_sums]
    n_fn, n_ot = len(arrays), len(out_tiles)
    if after is not None:
        arrays.append(after)
        in_specs.append(_ANY)
    n_in = len(arrays)

    def body(*refs):
        res = fn(*[r[...] for r in refs[:n_fn]])
        if not isinstance(res, (tuple, list)):
            res = (res,)
        outs = refs[n_in:]
        for o_ref, r in zip(outs[:n_ot], res[:n_ot]):
            o_ref[...] = r.astype(o_ref.dtype)
        i = pl.program_id(0)
        for o_ref, r in zip(outs[n_ot:], res[n_ot:]):
            @pl.when(i == 0)
            def _(o_ref=o_ref, r=r):
                o_ref[...] = r.astype(o_ref.dtype)

            @pl.when(i > 0)
            def _(o_ref=o_ref, r=r):
                o_ref[...] += r.astype(o_ref.dtype)

    res = pl.pallas_call(
        body, name=name, grid=(n_tok // tile,), in_specs=in_specs, out_specs=out_specs, out_shape=out_shape,
        compiler_params=_params(("arbitrary",)),
    )(*arrays)
    return res


def _sds(shape, dtype):
    return jax.ShapeDtypeStruct(tuple(shape), dtype)


def _sigmoid(x):
    return 1.0 / (1.0 + jnp.exp(-x))


def _sigmoid_t(x):
    return 0.5 * jnp.tanh(0.5 * x) + 0.5


def _rms(x, g):
    return x * lax.rsqrt(jnp.mean(x * x, axis=-1, keepdims=True) + EPS) * g


def _norm_bwd_fn(x, dh, dres, g):
    _, vjp = jax.vjp(_rms, x, g)
    dx, dg = vjp(dh)
    return dx + dres, dg


def _mla_a_fn(cq, ckv, g_qa, g_kva):
    return _rms(cq, g_qa), _rms(ckv, g_kva)


def _mla_a_bwd_fn(cq, ckv, dqn, dkvn, g_qa, g_kva):
    _, vjp = jax.vjp(_mla_a_fn, cq, ckv, g_qa, g_kva)
    return vjp((dqn, dkvn))


def _rope(t, cos, sin):
    t1, t2 = t[:, :QK_ROPE // 2], t[:, QK_ROPE // 2:]
    return jnp.concatenate([t1 * cos - t2 * sin, t1 * sin + t2 * cos], axis=-1)


def _mla_b_fn(q_raw, kv_raw, kr, cos, sin, g_qn, g_kn):
    krope = kr[:, :QK_ROPE]
    qs, ks, vs = [], [], []
    for h in range(MLA_HEADS):
        qh = _rms(q_raw[:, h * QK_HEAD:(h + 1) * QK_HEAD], g_qn)
        kvh = kv_raw[:, h * (QK_NOPE + V_HEAD):(h + 1) * (QK_NOPE + V_HEAD)]
        kh = _rms(jnp.concatenate([kvh[:, :QK_NOPE], krope], axis=-1), g_kn)
        qs.append(jnp.concatenate([qh[:, :QK_NOPE], _rope(qh[:, QK_NOPE:], cos, sin)], axis=-1))
        ks.append(jnp.concatenate([kh[:, :QK_NOPE], _rope(kh[:, QK_NOPE:], cos, sin)], axis=-1))
        vs.append(kvh[:, QK_NOPE:])
    return jnp.stack(qs), jnp.stack(ks), jnp.stack(vs)


def _mla_b_bwd_fn(q_raw, kv_raw, kr, cos, sin, dq, dk, dv, g_qn, g_kn):
    _, vjp = jax.vjp(lambda a, b, c, d, e: _mla_b_fn(a, b, c, cos, sin, d, e), q_raw, kv_raw, kr, g_qn, g_kn)
    return vjp((dq, dk, dv))


def _post_fn(a, o_f, o_b, hg, g_hgo):
    o = o_f + o_b
    parts = [a]
    for h in range(HG_HEADS):
        s = slice(h * HG_DK, (h + 1) * HG_DK)
        gate = hg[:, s]
        parts.append(_rms(o[:, s], g_hgo[:, s]) * (gate * _sigmoid(gate)))
    return jnp.concatenate(parts, axis=-1)


def _post_bwd_fn(o_f, o_b, hg, dr, g_hgo):
    def f(o, hg, g):
        return _post_fn(jnp.zeros_like(o), o, jnp.zeros_like(o), hg, g)[:, o.shape[1]:]
    _, vjp = jax.vjp(f, o_f + o_b, hg, g_hgo)
    return vjp(dr)


def _swiglu_fn(gt, up):
    return gt * _sigmoid(gt) * up


def _resid_norm_fn(acc, x, g):
    x_new = acc + x
    return x_new, _rms(x_new, g)


def _swiglu_bwd_fn(gt, up, dact):
    _, vjp = jax.vjp(_swiglu_fn, gt, up)
    return vjp(dact)


def _ple_loss_fn(x2, pg, pp, target):
    gate = _sigmoid(pg)
    err = x2 + gate * pp - target
    dx3 = err * (1.0 / err.shape[-1])
    loss = 0.5 * jnp.sum(jnp.mean(err * err, axis=-1, keepdims=True), axis=0, keepdims=True)
    return dx3, dx3 * pp * gate * (1.0 - gate), dx3 * gate, loss


def _attention_fwd(q, k, v):
    H, T, D = q.shape
    DV = v.shape[-1]
    tq, ck = min(ATT_TQ, T), min(ATT_CHUNK, T)
    c2 = (D ** -0.5) * LOG2_E

    def body(q_ref, k_ref, v_ref, o_ref, lse_ref):
        q_i = q_ref[0]

        def chunk(c, carry):
            m, l, acc = carry
            rows = pl.ds(pl.multiple_of(c * ck, ck), ck)
            s = _dot_raw(q_i, k_ref[0, rows, :], "nt")
            m_new = jnp.maximum(m, jnp.max(s, axis=-1, keepdims=True))
            p = jnp.exp2((s - m_new) * c2)
            alpha = jnp.exp2((m - m_new) * c2)
            l = l * alpha + jnp.sum(p, axis=-1, keepdims=True)
            acc = acc * alpha + _dot_raw(p, v_ref[0, rows, :], "nn")
            return m_new, l, acc

        init = (jnp.full((tq, 1), -jnp.inf, F32), jnp.zeros((tq, 1), F32), jnp.zeros((tq, DV), F32))
        m, l, acc = lax.fori_loop(0, T // ck, chunk, init, unroll=True)
        o_ref[...] = acc / l
        lse_ref[0] = m * c2 + jnp.log2(l)

    return pl.pallas_call(
        body, name="attention_fwd", grid=(H, T // tq),
        in_specs=[pl.BlockSpec((1, tq, D), lambda h, i: (h, i, 0)),
                  pl.BlockSpec((1, T, D), lambda h, i: (h, 0, 0)),
                  pl.BlockSpec((1, T, DV), lambda h, i: (h, 0, 0))],
        out_specs=[pl.BlockSpec((tq, DV), lambda h, i: (i, h)),
                   pl.BlockSpec((1, tq, 1), lambda h, i: (h, i, 0))],
        out_shape=[_sds((T, H * DV), F32), _sds((H, T, 1), F32)],
        compiler_params=_params(("parallel", "parallel")),
    )(q, k, v)


def _attention_bwd(q, k, v, o, lse2, dmix):
    H, T, D = q.shape
    DV = v.shape[-1]
    tk, cq = min(ATT_TK, T), min(ATT_CHUNK, T)
    scale = D ** -0.5
    c2 = scale * LOG2_E

    def body(q_ref, k_ref, v_ref, o_ref, lse_ref, do_ref, dq_ref, dk_ref, dv_ref, delta_ref):
        j = pl.program_id(1)

        @pl.when(j == 0)
        def _():
            delta = lax.dot_general(jnp.ones((8, DV), F32), do_ref[...] * o_ref[...], (((1,), (1,)), ((), ())),
                                    precision=lax.Precision.HIGHEST, preferred_element_type=F32)
            for i in range(T // cq):
                delta_ref[i] = delta[:, i * cq:(i + 1) * cq]
            dq_ref[0] = jnp.zeros((T, D), F32)

        k_j, v_j = k_ref[0], v_ref[0]
        dk_ref[0] = jnp.zeros((tk, D), F32)
        dv_ref[0] = jnp.zeros((tk, DV), F32)

        def chunk(c, carry):
            rows = pl.ds(pl.multiple_of(c * cq, cq), cq)
            q_c = q_ref[0, rows, :]
            do_c = do_ref[rows, :].astype(BF16)
            st = _dot_raw(k_j, q_c, "nt")
            pt = jnp.exp2(st * c2 - lse_ref[0, c])
            dv_ref[0] += _dot_raw(pt, do_c, "nn")
            dpt = _dot_raw(v_j, do_c, "nt")
            dst = pt * (dpt - delta_ref[c, 0:1, :]) * scale
            dk_ref[0] += _dot_raw(dst, q_c, "nn")
            dq_ref[0, rows, :] += _dot_raw(dst, k_j, "tn")
            return carry

        lax.fori_loop(0, T // cq, chunk, 0, unroll=True)

    return pl.pallas_call(
        body, name="attention_bwd", grid=(H, T // tk),
        in_specs=[pl.BlockSpec((1, T, D), lambda h, j: (h, 0, 0)),
                  pl.BlockSpec((1, tk, D), lambda h, j: (h, j, 0)),
                  pl.BlockSpec((1, tk, DV), lambda h, j: (h, j, 0)),
                  pl.BlockSpec((T, DV), lambda h, j: (0, h)),
                  pl.BlockSpec((1, T // cq, 1, cq), lambda h, j: (h, 0, 0, 0)),
                  pl.BlockSpec((T, DV), lambda h, j: (0, h))],
        out_specs=[pl.BlockSpec((1, T, D), lambda h, j: (h, 0, 0)),
                   pl.BlockSpec((1, tk, D), lambda h, j: (h, j, 0)),
                   pl.BlockSpec((1, tk, DV), lambda h, j: (h, j, 0))],
        out_shape=[_sds((H, T, D), F32), _sds((H, T, D), F32), _sds((H, T, DV), F32)],
        scratch_shapes=[pltpu.VMEM((T // cq, 8, cq), F32)],
        compiler_params=_params(("parallel", "arbitrary")),
    )(q, k, v, o, lse2.reshape(H, T // cq, 1, cq), dmix)


def _split3_dot(ones, x, kind):
    hi = x.astype(BF16)
    rest = x - hi.astype(F32)
    mid = rest.astype(BF16)
    lo = (rest - mid.astype(F32)).astype(BF16)
    return (_dot_raw(ones, hi, kind) + _dot_raw(ones, mid, kind)) + _dot_raw(ones, lo, kind)


@jax.custom_vjp
def _running_sum(x, tri):
    return _split3_dot(tri, x, "nn")


def _running_sum_fwd(x, tri):
    return _split3_dot(tri, x, "nn"), tri


def _running_sum_bwd(tri, g):
    return _split3_dot(tri, g, "tn"), jnp.zeros_like(tri)


_running_sum.defvjp(_running_sum_fwd, _running_sum_bwd)


def _gla_block(hq, hf, hi, lower, st_in, *, rev, dot):
    rows, dk = hq.shape
    G, C = rows // CHUNK, CHUNK
    q = hq * _sigmoid_t(hq)
    f = lower + (1.0 - lower) * _sigmoid_t(hf)
    k = 1.0 - f
    logf = jnp.log2(f)
    q3, k3, v3, lf3 = (t.reshape(G, C, dk) for t in (q, k, hi, logf))
    r = lax.broadcasted_iota(jnp.int32, (C, C), 0)
    c = lax.broadcasted_iota(jnp.int32, (C, C), 1)
    tri = ((r <= c) if rev else (r >= c)).astype(F32)
    b = _running_sum(lf3, jnp.broadcast_to(tri, (G, C, C)))
    tpos = lax.broadcasted_iota(jnp.int32, (1, C, 1), 1)
    first_half = (tpos >= C // 2) if rev else (tpos <= C // 2 - 1)
    b_mid = jnp.sum(jnp.where(first_half, lf3, 0.0), axis=1, keepdims=True)
    b_last = jnp.sum(lf3, axis=1, keepdims=True)
    a = dot(q3 * jnp.exp2(b - b_mid), k3 * jnp.exp2(b_mid - b), "nt") * tri
    o_intra = dot(a, v3, "nn")
    kv_t = dot(v3, k3 * jnp.exp2(b_last - b), "tn")
    decay = jnp.exp2(b_last)
    qd = q3 * jnp.exp2(b)
    st = st_in
    o_inter = [None] * G
    for g in (reversed(range(G)) if rev else range(G)):
        o_inter[g] = dot(qd[g], st, "nt")
        st = st * decay[g] + kv_t[g]
    o = o_intra.reshape(rows, dk) + jnp.concatenate(o_inter, axis=0)
    return o, st


def _gla_fwd(z, lower3, *, rev, col_q, col_f, col_v, hp):
    T = z.shape[0]
    rows = min(GLA_GROUP * CHUNK, T)
    nb = T // rows
    wide = hp * HG_DK
    blk = (lambda n: nb - 1 - n) if rev else (lambda n: n)

    def body(hq_ref, hf_ref, hi_ref, low_ref, o_ref, st_out_ref, st_ref):
        @pl.when(pl.program_id(1) == 0)
        def _():
            st_ref[...] = jnp.zeros_like(st_ref)

        st_in = [st_ref[i] for i in range(hp)]
        heads = []
        for i in range(hp):
            cols = slice(i * HG_DK, (i + 1) * HG_DK)
            heads.append(_gla_block(hq_ref[:, cols], hf_ref[:, cols], hi_ref[:, cols], low_ref[i], st_in[i], rev=rev,
                                    dot=_dot_raw))
        for i, (o, st) in enumerate(heads):
            st_out_ref[i, 0] = st_in[i]
            o_ref[:, i * HG_DK:(i + 1) * HG_DK] = o
            st_ref[i] = st

    def zspec(col):
        return pl.BlockSpec((rows, wide), lambda h, n: (blk(n), col // wide + h))

    return pl.pallas_call(
        body, name="gla_fwd_rev" if rev else "gla_fwd", grid=(HG_HEADS // hp, nb),
        in_specs=[zspec(col_q), zspec(col_f), zspec(col_v), pl.BlockSpec((hp, 1, HG_DK), lambda h, n: (h, 0, 0))],
        out_specs=[pl.BlockSpec((rows, wide), lambda h, n: (blk(n), h)),
                   pl.BlockSpec((hp, 1, HG_DK, HG_DK), lambda h, n: (h, blk(n), 0, 0))],
        out_shape=[_sds((T, HG_HEADS * HG_DK), F32), _sds((HG_HEADS, nb, HG_DK, HG_DK), F32)],
        scratch_shapes=[pltpu.VMEM((hp, HG_DK, HG_DK), F32)],
        compiler_params=_params(("parallel", "arbitrary")),
    )(z, z, z, lower3)


def _gla_bwd(z, lower3, states, do, prev, *, rev, col_q, col_f, col_v, hp):
    T = z.shape[0]
    rows = min(GLA_GROUP * CHUNK, T)
    nb = T // rows
    wide = hp * HG_DK
    blk = (lambda n: n) if rev else (lambda n: nb - 1 - n)
    has_prev = prev is not None
    fn = functools.partial(_gla_block, rev=rev, dot=_bdot)

    def body(*refs):
        hq_ref, hf_ref, hi_ref, low_ref, st_ref, do_ref = refs[:6]
        rest = refs[6:]
        if has_prev:
            pq_ref, pi_ref = rest[:2]
            rest = rest[2:]
        dhq_ref, dhi_ref, dhf_ref, dlow_ref, dst_ref = rest
        n = pl.program_id(1)

        @pl.when(n == 0)
        def _():
            dst_ref[...] = jnp.zeros_like(dst_ref)

        dst_in = [dst_ref[i] for i in range(hp)]
        heads = []
        for i in range(hp):
            cols = slice(i * HG_DK, (i + 1) * HG_DK)
            _, vjp = jax.vjp(fn, hq_ref[:, cols], hf_ref[:, cols], hi_ref[:, cols], low_ref[i], st_ref[i, 0])
            dhq, dhf, dhi, dlow, dst = vjp((do_ref[:, cols], dst_in[i]))
            if has_prev:
                dhq = dhq + pq_ref[:, cols]
                dhi = dhi + pi_ref[:, cols]
            heads.append((dhq, dhf, dhi, dlow, dst))
        for i, (dhq, dhf, dhi, dlow, dst) in enumerate(heads):
            cols = slice(i * HG_DK, (i + 1) * HG_DK)
            dst_ref[i] = dst
            dhq_ref[:, cols] = dhq.astype(dhq_ref.dtype)
            dhi_ref[:, cols] = dhi.astype(dhi_ref.dtype)
            dhf_ref[:, cols] = dhf.astype(dhf_ref.dtype)

        @pl.when(n == 0)
        def _():
            for i in range(hp):
                dlow_ref[i] = heads[i][3]

        @pl.when(n > 0)
        def _():
            for i in range(hp):
                dlow_ref[i] += heads[i][3]

    def zspec(col):
        return pl.BlockSpec((rows, wide), lambda h, n: (blk(n), col // wide + h))

    hspec = pl.BlockSpec((rows, wide), lambda h, n: (blk(n), h))
    in_specs = [zspec(col_q), zspec(col_f), zspec(col_v), pl.BlockSpec((hp, 1, HG_DK), lambda h, n: (h, 0, 0)),
                pl.BlockSpec((hp, 1, HG_DK, HG_DK), lambda h, n: (h, blk(n), 0, 0)), hspec]
    ins = [z, z, z, lower3, states, do]
    if has_prev:
        in_specs += [hspec, hspec]
        ins += list(prev)
    full_wide = HG_HEADS * HG_DK
    acc_dtype = BF16 if has_prev else F32
    return pl.pallas_call(
        body, name="gla_bwd_rev" if rev else "gla_bwd", grid=(HG_HEADS // hp, nb),
        in_specs=in_specs,
        out_specs=[hspec, hspec, hspec, pl.BlockSpec((hp, 1, HG_DK), lambda h, n: (h, 0, 0))],
        out_shape=[_sds((T, full_wide), acc_dtype), _sds((T, full_wide), acc_dtype), _sds((T, full_wide), BF16),
                   _sds((HG_HEADS, 1, HG_DK), F32)],
        scratch_shapes=[pltpu.VMEM((hp, HG_DK, HG_DK), F32)],
        compiler_params=_params(("parallel", "arbitrary")),
    )(*ins)


def _lower_fn(lb):
    e = jnp.exp(lb - jnp.max(lb, axis=0, keepdims=True))
    return (e / jnp.sum(e, axis=0, keepdims=True))[0]


def _lower_bounds(lb):
    def body(lb_ref, o_ref):
        o_ref[...] = _lower_fn(lb_ref[...])
    return pl.pallas_call(body, name="lower_bounds", out_shape=_sds(lb.shape[1:], F32))(lb)


def _row_tile(r, cap=1024):
    best = None
    for t in range(16, min(r, cap) + 1, 16):
        if r % t == 0:
            best = t
    return best if best is not None else r


def _sum4(shards, recv, chip, *, name):
    _, R, C = shards.shape
    tr = _row_tile(R)

    def body(chip_ref, o_ref, r_ref, out_ref):
        out_ref[...] = ((o_ref[0].astype(F32) + r_ref[0].astype(F32)) + r_ref[1].astype(F32)) + r_ref[2].astype(F32)

    grid_spec = pltpu.PrefetchScalarGridSpec(
        num_scalar_prefetch=1, grid=(R // tr,),
        in_specs=[pl.BlockSpec((1, tr, C), lambda i, chip_ref: (chip_ref[0], i, 0)),
                  pl.BlockSpec((3, tr, C), lambda i, chip_ref: (0, i, 0))],
        out_specs=pl.BlockSpec((tr, C), lambda i, chip_ref: (i, 0)))
    return pl.pallas_call(
        body, name=name, grid_spec=grid_spec, out_shape=_sds((R, C), F32), compiler_params=_params(("parallel",)),
    )(chip, shards, recv)


def _adamw_math(w, g, m, v):
    m = ADAM_B1 * m + (1.0 - ADAM_B1) * g
    v = ADAM_B2 * v + (1.0 - ADAM_B2) * (g * g)
    m_hat = m / (1.0 - ADAM_B1 ** ADAM_STEP)
    v_hat = v / (1.0 - ADAM_B2 ** ADAM_STEP)
    delta = -ADAM_LR * (m_hat / (jnp.sqrt(v_hat) + ADAM_EPS) + ADAM_WD * w)
    return delta, m, v


def _adamw(w, g_a, g_b, m, v, *, name):
    R, C = w.shape
    tr = _row_tile(R)
    two = g_b is not None

    def body(*refs):
        w_ref, ga_ref = refs[0], refs[1]
        rest = refs[2:]
        g = ga_ref[...]
        if two:
            g = g + rest[0][...]
            rest = rest[1:]
        m_ref, v_ref, g_out, d_out, m_out, v_out = rest
        delta, m_new, v_new = _adamw_math(w_ref[...], g, m_ref[...], v_ref[...])
        g_out[...] = g
        d_out[...] = delta
        m_out[...] = m_new
        v_out[...] = v_new

    spec = pl.BlockSpec((tr, C), lambda i: (i, 0))
    ins = [w, g_a] + ([g_b] if two else []) + [m, v]
    return pl.pallas_call(
        body, name=name, grid=(R // tr,), in_specs=[spec] * len(ins), out_specs=[spec] * 4,
        out_shape=[_sds((R, C), F32)] * 4, compiler_params=_params(("parallel",)),
    )(*ins)


def _adamw_small(red, lb_full, ws, ms, vs):
    n = len(ws)

    def pieces(shape):
        out = []
        for j, idx in enumerate(itertools.product(*[range(d) for d in shape[:-1]])):
            out.append((idx[:-1] + (slice(idx[-1], idx[-1] + 1), slice(None)), j * shape[-1]))
        return out

    def body(*refs):
        red_ref, lb_ref = refs[0], refs[1]
        w_refs, m_refs, v_refs = refs[2:2 + n], refs[2 + n:2 + 2 * n], refs[2 + 2 * n:2 + 3 * n]
        out_refs = refs[2 + 3 * n:]
        chip = 2 * lax.axis_index("x") + lax.axis_index("y")
        n_f, shard = lb_ref.shape[-1], w_refs[n - 1].shape[-1]
        _, vjp = jax.vjp(_lower_fn, lb_ref[...])
        dlb = vjp(red_ref[8:10, 0:n_f])[0]
        for i in range(n):
            width = w_refs[i].shape[-1]
            for j, (at, lane) in enumerate(pieces(w_refs[i].shape)):
                if i < n - 1:
                    g = red_ref[i:i + 1, lane:lane + width]
                else:
                    row = dlb[j // 2][j % 2:j % 2 + 1]
                    g = sum(jnp.where(chip == q, row[:, q * shard:(q + 1) * shard], 0.0) for q in range(N_CHIPS))
                delta, m_new, v_new = _adamw_math(w_refs[i][at], g, m_refs[i][at], v_refs[i][at])
                for o_ref, val in zip(out_refs[4 * i:4 * i + 4], (g, delta, m_new, v_new)):
                    o_ref[at] = val

    return pl.pallas_call(
        body, name="adamw_small", out_shape=[_sds(w.shape, F32) for w in ws for _ in range(4)],
    )(red, lb_full, *ws, *ms, *vs)


def _chip_peers():
    x, y, c = lax.axis_index("x"), lax.axis_index("y"), lax.axis_index("c")
    return (x, y, c), 2 * x + y, [(1 - x, y), (x, 1 - y), (1 - x, 1 - y)]


_HBM = pl.BlockSpec(memory_space=pltpu.HBM)
_SEM = pl.BlockSpec(memory_space=pltpu.SEMAPHORE)
_EFFECT = pltpu.SideEffectType.DATAFLOW_SIDE_EFFECTING


def _exchange_copies(srcs, lands, sems, mode):
    (x, y, c), me, chips = _chip_peers()
    copies = []
    for t, (src, land) in enumerate(zip(srcs, lands)):
        if mode == "swap":
            copies.append(pltpu.make_async_remote_copy(src, land, sems[0].at[3 * t], sems[1].at[3 * t],
                                                       device_id=(x, y, 1 - c), device_id_type=MESH))
            continue
        for k, (px, py) in enumerate(chips):
            gather = mode == "gather"
            copies.append(pltpu.make_async_remote_copy(
                src if gather else src.at[2 * px + py], land.at[me] if gather else land.at[k],
                sems[0].at[3 * t + k], sems[1].at[3 * t + k], device_id=(px, py, c), device_id_type=MESH))
        if mode == "gather":
            copies.append(pltpu.make_async_copy(src, land.at[me], sems[2].at[t]))
    return copies


def _exchange_start(srcs, *, mode, name, after=None):
    n = len(srcs)
    n_sem = 3 if mode == "gather" else 2
    n_in = 2 * n + (after is not None)
    land_shape = {"gather": lambda s: (N_CHIPS,) + s.shape, "scatter": lambda s: (3,) + s.shape[1:], "swap": lambda s: s.shape}
    lands = [_sds(land_shape[mode](s), s.dtype) for s in srcs]

    def body(*refs):
        for cp in _exchange_copies(refs[:n], refs[n:2 * n], refs[n_in:n_in + n_sem], mode):
            cp.start()
        token = refs[-1]
        token[...] = jnp.zeros_like(token)

    sem_shapes = [pltpu.SemaphoreType.DMA((3 * n,)), pltpu.SemaphoreType.DMA((3 * n,))]
    sem_shapes += [pltpu.SemaphoreType.DMA((n,))] if mode == "gather" else []
    thru = [pltpu.HBM(s.shape, s.dtype) for s in srcs] + [pltpu.HBM(l.shape, l.dtype) for l in lands]
    res = pl.pallas_call(
        body, name=name, in_specs=[_HBM] * (2 * n) + [_ANY] * (after is not None),
        out_specs=[_SEM] * n_sem + [_HBM] * (2 * n) + [pl.BlockSpec(memory_space=pltpu.VMEM)],
        out_shape=sem_shapes + thru + [_sds((8, LANE), F32)], input_output_aliases={i: n_sem + i for i in range(2 * n)},
        compiler_params=pltpu.CompilerParams(has_side_effects=_EFFECT),
    )(*[pltpu.with_memory_space_constraint(s, pltpu.HBM) for s in srcs],
      *[pltpu.with_memory_space_constraint(lax.empty(l.shape, l.dtype), pltpu.HBM) for l in lands],
      *([after] if after is not None else []))
    return (res[:n_sem], res[n_sem:n_sem + n], res[n_sem + n:n_sem + 2 * n], mode), res[-1]


def _exchange_wait(started, after, *, name):
    sems, srcs, lands, mode = started
    n, n_sem = len(srcs), len(sems)
    after = list(after) if isinstance(after, (list, tuple)) else [after]

    def body(*refs):
        for cp in _exchange_copies(refs[:n], refs[n:2 * n], refs[2 * n:2 * n + n_sem], mode):
            cp.wait()

    res = pl.pallas_call(
        body, name=name, in_specs=[_HBM] * (2 * n) + [_SEM] * n_sem + [_ANY] * len(after), out_specs=[_HBM] * (2 * n),
        out_shape=[pltpu.HBM(a.shape, a.dtype) for a in list(srcs) + list(lands)],
        input_output_aliases={i: i for i in range(2 * n)},
        compiler_params=pltpu.CompilerParams(has_side_effects=_EFFECT),
    )(*srcs, *lands, *sems, *after)
    return res[:n], res[n:]


def _allreduce_small(pack):
    R, C = pack.shape

    def body(in_ref, out_ref, slots, send_sems, recv_sems):
        x, y, c = lax.axis_index("x"), lax.axis_index("y"), lax.axis_index("c")
        me = 4 * x + 2 * y + c
        slots[me] = in_ref[...]
        copies = []
        for k in range(1, N_DEV):
            peer = (x ^ ((k >> 2) & 1), y ^ ((k >> 1) & 1), c ^ (k & 1))
            cp = pltpu.make_async_remote_copy(in_ref, slots.at[me], send_sems.at[k - 1], recv_sems.at[k - 1],
                                              device_id=peer, device_id_type=MESH)
            cp.start()
            copies.append(cp)
        for cp in copies:
            cp.wait()
        acc = slots[0]
        for d in range(1, N_DEV):
            acc = acc + slots[d]
        out_ref[...] = acc

    return pl.pallas_call(
        body, name="allreduce_small", out_shape=_sds((R, C), F32),
        in_specs=[pl.BlockSpec(memory_space=pltpu.VMEM)], out_specs=pl.BlockSpec(memory_space=pltpu.VMEM),
        scratch_shapes=[pltpu.VMEM((N_DEV, R, C), F32), pltpu.SemaphoreType.DMA((N_DEV - 1,)),
                        pltpu.SemaphoreType.DMA((N_DEV - 1,))],
        compiler_params=_params(),
    )(pack)


_Z_CQ, _Z_CKV, _Z_HQ, _Z_HFF, _Z_HFB, _Z_HI, _Z_HG, _Z_KR, _Z_END = 0, 256, 512, 1024, 1536, 2048, 2560, 3072, 3200


def _to_z_layout(wt):
    pad = jnp.zeros((_Z_END - _Z_KR - QK_ROPE, wt.shape[1]), wt.dtype)
    return jnp.concatenate([wt[:512], wt[512 + QK_ROPE:], wt[512:512 + QK_ROPE], pad], axis=0)


def _from_z_layout(wt):
    return jnp.concatenate([wt[:512], wt[_Z_KR:_Z_KR + QK_ROPE], wt[512:_Z_KR]], axis=0)


def _col_shards_to_full(g):
    return jnp.transpose(g, (1, 0, 2)).reshape(g.shape[1], -1)


def _full_to_col_shards(w):
    r, c = w.shape
    return jnp.transpose(w.reshape(r, N_CHIPS, c // N_CHIPS), (1, 0, 2))


def _full_to_row_shards(w):
    r, c = w.shape
    return w.reshape(N_CHIPS, r // N_CHIPS, c)


def kernel(x, p, positions, g_mix, w_in, g_qa, g_kva, w_qb, w_kvb, g_qn, g_kn, lb_param, g_hgo, w_o, g_ffn, w_gate, w_up, w_down, g_ple, w_ple_gate, w_ple_proj, loss_target, m_g_mix, m_w_in, m_g_qa, m_g_kva, m_w_qb, m_w_kvb, m_g_qn, m_g_kn, m_lb_param, m_g_hgo, m_w_o, m_g_ffn, m_w_gate, m_w_up, m_w_down, m_g_ple, m_w_ple_gate, m_w_ple_proj, v_g_mix, v_w_in, v_g_qa, v_g_kva, v_w_qb, v_w_kvb, v_g_qn, v_g_kn, v_lb_param, v_g_hgo, v_w_o, v_g_ffn, v_w_gate, v_w_up, v_w_down, v_g_ple, v_w_ple_gate, v_w_ple_proj):
    w_named = dict(g_mix=g_mix, w_in=w_in, g_qa=g_qa, g_kva=g_kva, w_qb=w_qb, w_kvb=w_kvb, g_qn=g_qn, g_kn=g_kn,
                   lb_param=lb_param, g_hgo=g_hgo, w_o=w_o, g_ffn=g_ffn, w_gate=w_gate, w_up=w_up, w_down=w_down,
                   g_ple=g_ple, w_ple_gate=w_ple_gate, w_ple_proj=w_ple_proj)
    m_named = dict(g_mix=m_g_mix, w_in=m_w_in, g_qa=m_g_qa, g_kva=m_g_kva, w_qb=m_w_qb, w_kvb=m_w_kvb, g_qn=m_g_qn,
                   g_kn=m_g_kn, lb_param=m_lb_param, g_hgo=m_g_hgo, w_o=m_w_o, g_ffn=m_g_ffn, w_gate=m_w_gate,
                   w_up=m_w_up, w_down=m_w_down, g_ple=m_g_ple, w_ple_gate=m_w_ple_gate, w_ple_proj=m_w_ple_proj)
    v_named = dict(g_mix=v_g_mix, w_in=v_w_in, g_qa=v_g_qa, g_kva=v_g_kva, w_qb=v_w_qb, w_kvb=v_w_kvb, g_qn=v_g_qn,
                   g_kn=v_g_kn, lb_param=v_lb_param, g_hgo=v_g_hgo, w_o=v_w_o, g_ffn=v_g_ffn, w_gate=v_w_gate,
                   w_up=v_w_up, w_down=v_w_down, g_ple=v_g_ple, w_ple_gate=v_w_ple_gate, w_ple_proj=v_w_ple_proj)
    order = list(w_named)
    transposed = ("w_in", "w_qb", "w_gate", "w_up")
    col_sharded = ("w_kvb", "w_ple_proj")
    row_sharded = ("w_o", "w_down", "w_ple_gate")
    big = transposed + col_sharded + row_sharded

    def view(n, a):
        return jnp.transpose(a[0]) if n in transposed else a[0]

    def unview(n, a):
        return (jnp.transpose(a) if n in transposed else a)[None]

    def to_shards(n, g):
        return _full_to_col_shards(g) if n in col_sharded else _full_to_row_shards(g)

    x2d, p2d, tgt = x[0], p[0, 0], loss_target[0]
    T, D = x2d.shape

    lb_flat = lb_param.reshape(-1, lb_param.shape[-1])
    gather_groups = (("w_in",), ("w_qb", "w_kvb"), ("w_o", "w_gate", "w_up"), ("w_down", "w_ple_gate", "w_ple_proj"))
    gather_started = []

    casts = {n: view(n, w_named[n]).astype(BF16) for n in big}

    def gather_start(gi, after):
        srcs = [casts[n] for n in gather_groups[gi]] + ([lb_flat] if gi == 0 else [])
        started, token = _exchange_start(srcs, mode="gather", name=f"gather_start_{gi}", after=after)
        gather_started.append(started)
        return token

    full = {}

    def gather_wait(gi, after):
        _, got = _exchange_wait(gather_started[gi], after, name=f"gather_wait_{gi}")
        for n, g in zip(gather_groups[gi], got):
            full[n] = _col_shards_to_full(g) if n in col_sharded else g.reshape(-1, g.shape[-1])
        return got

    g_hgo_row = g_hgo.reshape(1, -1)

    inv_freq = ROPE_THETA ** (-jnp.arange(0, QK_ROPE, 2, dtype=F32) / QK_ROPE)
    ang = positions[0].astype(F32)[:, None] * inv_freq
    cos, sin = jnp.cos(ang), jnp.sin(ang)
    token = gather_start(0, None)
    h1 = _stage(_rms, [x2d], [g_mix], [_sds((T, D), BF16)], [], name="norm_mix", after=token)[0]
    got = gather_wait(0, [h1, cos, sin] + [casts[n] for g in gather_groups[1:] for n in g])
    token = got[0]
    for gi in range(1, len(gather_groups)):
        token = gather_start(gi, token)
    lb_full = _col_shards_to_full(got[-1]).reshape(lb_param.shape[0], lb_param.shape[1], -1)
    w_in_zt = _to_z_layout(full["w_in"])
    z = _mm(h1, w_in_zt, tb=True, name="in_proj", after=token)
    qn, kvn = _stage(_mla_a_fn, [_cols(z, 256, 0), _cols(z, 256, 1)], [g_qa, g_kva],
                     [_sds((T, 256), BF16), _sds((T, 256), BF16)], [], name="mla_latent_norm")
    gather_wait(1, qn)
    q_raw = _mm(qn, full["w_qb"], tb=True, name="q_up")
    kv_raw = _mm(kvn, full["w_kvb"], name="kv_up")
    kr = _cols(z, LANE, _Z_KR // LANE)
    q, k, v = _stage(_mla_b_fn, [q_raw, kv_raw, kr, cos, sin], [g_qn, g_kn],
                     [_sds((MLA_HEADS, T, QK_HEAD), BF16), _sds((MLA_HEADS, T, QK_HEAD), BF16),
                      _sds((MLA_HEADS, T, V_HEAD), BF16)], [], name="mla_qk_norm_rope")
    att, lse = _attention_fwd(q, k, v)

    lower = _lower_bounds(lb_full)
    lower3 = lower.reshape(2, HG_HEADS, 1, HG_DK)
    o_f, st_f = _gla_fwd(z, lower3[0], rev=False, col_q=_Z_HQ, col_f=_Z_HFF, col_v=_Z_HI, hp=GLA_FWD_HEADS)
    o_b, st_b = _gla_fwd(z, lower3[1], rev=True, col_q=_Z_HQ, col_f=_Z_HFB, col_v=_Z_HI, hp=GLA_FWD_HEADS)
    hg = _cols(z, 512, _Z_HG // 512)
    mix = _stage(_post_fn, [att, o_f, o_b, hg], [g_hgo_row], [_sds((T, att.shape[1] + o_f.shape[1]), BF16)], [],
                 name="mix_out")[0]
    gather_wait(2, mix)
    x1, h2 = _mm_fused(mix, full["w_o"], _resid_norm_fn, [x2d], [g_ffn], [F32, BF16], [], full_rows=True,
                       name="out_proj")
    gt, up, act = _mm_fused(h2, full["w_gate"], lambda gt, up: (gt, up, _swiglu_fn(gt, up)), [], [], [BF16, BF16, BF16],
                            [], tb=True, b2=full["w_up"], name="ffn_gate_up")
    gather_wait(3, act)
    x2, h3 = _mm_fused(act, full["w_down"], _resid_norm_fn, [x1], [g_ple], [F32, BF16], [], full_rows=True,
                       name="ffn_down")
    pp = _mm(p2d, full["w_ple_proj"], name="ple_proj")
    dx3, dpg, dpp, loss_part = _mm_fused(
        h3, full["w_ple_gate"], lambda acc, pp, x2, tgt: _ple_loss_fn(x2, acc, pp, tgt), [pp, x2, tgt], [],
        [F32, BF16, BF16], [_sds((1, 1), F32)], full_rows=True, name="ple_gate_loss")

    grads = {}
    scatter_groups = (("w_ple_proj", "w_ple_gate", "w_down"), ("w_gate", "w_up", "w_o"), ("w_qb", "w_kvb", "w_in"))
    scatter_started = []

    def scatter_start(gi):
        srcs = [to_shards(n, grads[n]) for n in scatter_groups[gi]]
        started, token = _exchange_start(srcs, mode="scatter", name=f"scatter_start_{gi}")
        scatter_started.append(started)
        return token

    chip = 2 * lax.axis_index("x") + lax.axis_index("y")
    swap_started = []

    def reduce_group(gi, after):
        shards, recvs = _exchange_wait(scatter_started[gi], after, name=f"scatter_wait_{gi}")
        sums = [_sum4(s, r, chip.reshape(1), name="sum_" + n) for n, s, r in zip(scatter_groups[gi], shards, recvs)]
        started, token = _exchange_start(sums, mode="swap", name=f"swap_start_{gi}")
        swap_started.append(started)
        return token

    grads["w_ple_proj"] = _mm(p2d, dpp, ta=True, out_dtype=BF16, name="d_w_ple_proj")
    grads["w_ple_gate"] = _mm(h3, dpg, ta=True, out_dtype=BF16, name="d_w_ple_gate")
    dx2, grads["g_ple"] = _mm_fused(
        dpg, full["w_ple_gate"], lambda acc, x2, dx3, g: _norm_bwd_fn(x2, acc, dx3, g), [x2, dx3], [g_ple],
        [F32], [_sds((1, D), F32)], tb=True, full_rows=True, name="d_h3_norm_ple_bwd")
    dgt, dup = _mm_fused(dx2, full["w_down"], lambda acc, gt, up: _swiglu_bwd_fn(gt.astype(F32), up.astype(F32), acc), [gt, up], [],
                         [BF16, BF16], [], tb=True, name="d_act_swiglu_bwd")
    grads["w_down"] = _mm(act, dx2, ta=True, out_dtype=BF16, name="d_w_down")
    token = scatter_start(0)
    grads["w_gate"] = _mm(dgt, h2, ta=True, out_dtype=BF16, name="d_w_gate")
    grads["w_up"] = _mm(dup, h2, ta=True, out_dtype=BF16, name="d_w_up")
    dh2 = _mm(dgt, full["w_gate"], name="d_h2_gate", after=token)
    dx1, grads["g_ffn"] = _mm_fused(
        dup, full["w_up"], lambda acc, dh2, x1, dx2, g: _norm_bwd_fn(x1, acc + dh2, dx2, g), [dh2, x1, dx2], [g_ffn],
        [F32], [_sds((1, D), F32)], full_rows=True, name="d_h2_norm_ffn_bwd")
    dmix = _mm(dx1, full["w_o"], tb=True, name="d_mix")
    grads["w_o"] = _mm(mix, dx1, ta=True, out_dtype=BF16, name="d_w_o")

    token = scatter_start(1)
    half = MLA_HEADS * V_HEAD
    do, dhg, dg_hgo = _stage(_post_bwd_fn, [o_f, o_b, hg, _cols(dmix, half, 1)], [g_hgo_row],
                             [_sds((T, half), F32), _sds((T, half), BF16)], [_sds((1, half), F32)], name="mix_out_bwd",
                             after=token)
    grads["g_hgo"] = dg_hgo
    dhq_f, dhi_f, dhf_f, dlow_f = _gla_bwd(z, lower3[0], st_f, do, None, rev=False,
                                           col_q=_Z_HQ, col_f=_Z_HFF, col_v=_Z_HI, hp=GLA_BWD_HEADS)
    dhq, dhi, dhf_b, dlow_b = _gla_bwd(z, lower3[1], st_b, do, (dhq_f, dhi_f), rev=True,
                                       col_q=_Z_HQ, col_f=_Z_HFB, col_v=_Z_HI, hp=GLA_BWD_HEADS)

    dq, dk, dv = _attention_bwd(q, k, v, att, lse, dmix)
    token = reduce_group(0, dq)
    dq_raw, dkv_raw, dkr, grads["g_qn"], grads["g_kn"] = _stage(
        _mla_b_bwd_fn, [q_raw, kv_raw, kr, cos, sin, dq, dk, dv], [g_qn, g_kn],
        [_sds(q_raw.shape, BF16), _sds(kv_raw.shape, BF16), _sds((T, LANE), BF16)],
        [_sds(g_qn.shape, F32), _sds(g_kn.shape, F32)], name="mla_qk_norm_rope_bwd", after=token)
    grads["w_qb"] = _mm(dq_raw, qn, ta=True, out_dtype=BF16, name="d_w_qb")
    grads["w_kvb"] = _mm(kvn, dkv_raw, ta=True, out_dtype=BF16, name="d_w_kvb")
    dqn = _mm(dq_raw, full["w_qb"], name="d_qn")
    dkvn = _mm(dkv_raw, full["w_kvb"], tb=True, name="d_kvn")
    dcq, dckv, grads["g_qa"], grads["g_kva"] = _stage(
        _mla_a_bwd_fn, [_cols(z, 256, 0), _cols(z, 256, 1), dqn, dkvn], [g_qa, g_kva],
        [_sds((T, 256), BF16), _sds((T, 256), BF16)], [_sds(g_qa.shape, F32), _sds(g_kva.shape, F32)],
        name="mla_latent_norm_bwd")
    token = reduce_group(1, dcq)
    dz = jnp.concatenate([dcq, dckv, dhq, dhf_f, dhf_b, dhi, dhg, dkr], axis=1)
    grads["w_in"] = _from_z_layout(_mm(dz, h1, ta=True, out_dtype=BF16, name="d_w_in", after=token))
    token = scatter_start(2)
    grad_x, grads["g_mix"] = _mm_fused(
        dz, w_in_zt, lambda acc, x, dx1, g: _norm_bwd_fn(x, acc, dx1, g), [x2d, dx1], [g_mix],
        [F32], [_sds((1, D), F32)], full_rows=True, name="d_h1_norm_mix_bwd", after=token)

    out_g, out_d, out_m, out_v = {}, {}, {}, {}

    def update_group(gi, after):
        mine, theirs = _exchange_wait(swap_started[gi], after, name=f"swap_wait_{gi}")
        for n, a, b in zip(scatter_groups[gi], mine, theirs):
            out_g[n], out_d[n], out_m[n], out_v[n] = (
                unview(n, t) for t in _adamw(view(n, w_named[n]), a, b, view(n, m_named[n]), view(n, v_named[n]),
                                             name="adamw_" + n))
        return out_v[scatter_groups[gi][-1]]

    done = update_group(0, grad_x)
    done = update_group(1, done)
    token = reduce_group(2, done)
    update_group(2, token)

    small = ("g_mix", "g_qa", "g_kva", "g_qn", "g_kn", "g_hgo", "g_ffn", "g_ple")
    small_all = small + ("lb_param",)
    width = -(-max(w_named[n].size for n in small_all) // LANE) * LANE

    def row(a):
        a = a.reshape(1, -1)
        return jnp.pad(a, ((0, 0), (0, width - a.shape[1])))

    dlower = jnp.concatenate([dlow_f.reshape(1, -1), dlow_b.reshape(1, -1)], axis=0)
    pack = jnp.concatenate([row(grads[n]) for n in small] + [row(dlower[0]), row(dlower[1]), row(loss_part)]
                           + [jnp.zeros((5, width), F32)], axis=0)
    red = _allreduce_small(pack)
    loss = red[10, 0]

    outs = _adamw_small(red, lb_full, [w_named[n] for n in small_all], [m_named[n] for n in small_all],
                        [v_named[n] for n in small_all])
    for i, n in enumerate(small_all):
        out_g[n], out_d[n], out_m[n], out_v[n] = outs[4 * i:4 * i + 4]

    return (loss, grad_x[None], *[out_g[n] for n in order], *[out_d[n] for n in order],
            *[out_m[n] for n in order], *[out_v[n] for n in order])
```

```python
import functools
import itertools

import jax
import jax.numpy as jnp
from jax import lax
from jax.experimental import pallas as pl
from jax.experimental.pallas import tpu as pltpu

F32 = jnp.float32
BF16 = jnp.bfloat16
MESH = pl.DeviceIdType.MESH

EPS = 1e-6
ROPE_THETA = 10000.0
MLA_HEADS = 4
QK_NOPE = 128
QK_ROPE = 64
QK_HEAD = QK_NOPE + QK_ROPE
V_HEAD = 128
HG_HEADS = 4
HG_DK = 128
CHUNK = 64
ADAM_LR = 0.001
ADAM_B1 = 0.9
ADAM_B2 = 0.999
ADAM_EPS = 1e-08
ADAM_WD = 0.01
ADAM_STEP = 10

LANE = 128
VMEM_LIMIT = 56 * 1024 * 1024
TOK_TILE = 256
GLA_GROUP = 16
GLA_FWD_HEADS = 4
GLA_BWD_HEADS = 2
ATT_TQ = 1024
ATT_TK = 1024
ATT_CHUNK = 512
LOG2_E = 1.4426950408889634
N_CHIPS = 4
N_DEV = 8


_ANY = pl.BlockSpec(memory_space=pl.ANY)


def _params(dims=None, **kw):
    return pltpu.CompilerParams(dimension_semantics=dims, vmem_limit_bytes=VMEM_LIMIT, **kw)


def _tile_candidates(n, cap):
    out = [d for d in range(LANE, min(n, cap) + 1, LANE) if n % d == 0]
    if n <= cap and n not in out:
        out.append(n)
    return out or [n]


MM_VMEM_BUDGET = 40 * 1024 * 1024
MM_MIN_ROWS = 256
MM_MAX_ROWS = 1536
HBM_BYTES_PER_S = 2.8e12
MXU_FLOPS_PER_S = 8e14
STEP_OVERHEAD_S = 0.35e-6


def _mm_tiles(M, N, K, a_bytes, b_bytes, o_bytes, has_add, full_rows=False, full_k=False, n_a=1, n_b=1):
    cast_a, cast_b = a_bytes > 2, b_bytes > 2
    a_bytes, b_bytes = n_a * a_bytes, n_b * b_bytes
    best = None
    for tm in [t for t in _tile_candidates(M, MM_MAX_ROWS) if t >= min(M, MM_MIN_ROWS)]:
        for tn in ([N] if full_rows else _tile_candidates(N, N)):
            for tk in ([K] if full_k else _tile_candidates(K, K)):
                ni, nj, nk = M // tm, N // tn, K // tk
                vmem = 2 * (tm * tk * a_bytes + tk * tn * b_bytes + tm * tn * o_bytes * (2 if has_add else 1))
                vmem += tm * tn * 4 * (2 if nk > 1 else 1)
                vmem += (tm * tk * 2 * n_a if cast_a else 0) + (tk * tn * 2 * n_b if cast_b else 0)
                if vmem > MM_VMEM_BUDGET:
                    continue
                moved = M * K * a_bytes * (nj if nk > 1 else 1) + K * N * b_bytes * (1 if nj == nk == 1 else ni)
                moved += M * N * o_bytes * (2 if has_add else 1)
                t = max(moved / HBM_BYTES_PER_S, 2 * M * N * K / MXU_FLOPS_PER_S) + ni * nj * nk * STEP_OVERHEAD_S
                if best is None or t < best[0]:
                    best = (t, tm, tn, tk)
    assert best is not None, (M, N, K)
    return best[1:]


def _dot_raw(a, b, kind):
    nb = a.ndim - 2
    batch = ((0,), (0,)) if nb else ((), ())
    ca = nb if kind == "tn" else nb + 1
    cb = nb + 1 if kind == "nt" else nb
    return lax.dot_general(a.astype(BF16), b.astype(BF16), (((ca,), (cb,)), batch), preferred_element_type=F32)


@functools.partial(jax.custom_vjp, nondiff_argnums=(2,))
def _bdot(a, b, kind):
    return _dot_raw(a, b, kind)


def _bdot_fwd(a, b, kind):
    return _dot_raw(a, b, kind), (a, b)


def _bdot_bwd(kind, res, g):
    a, b = res
    if kind == "nn":
        da, db = _bdot(g, b, "nt"), _bdot(a, g, "tn")
    elif kind == "nt":
        da, db = _bdot(g, b, "nn"), _bdot(g, a, "tn")
    else:
        da, db = _bdot(b, g, "nt"), _bdot(a, g, "nn")
    return da.astype(a.dtype), db.astype(b.dtype)


_bdot.defvjp(_bdot_fwd, _bdot_bwd)


def _mm(a, b, *, name, ta=False, tb=False, add=None, out_dtype=F32, after=None):
    K, M = a.shape if ta else a.shape[::-1]
    N, Kb = b.shape if tb else b.shape[::-1]
    assert K == Kb, (a.shape, b.shape, ta, tb)
    tm, tn, tk = _mm_tiles(M, N, K, a.dtype.itemsize, b.dtype.itemsize, jnp.dtype(out_dtype).itemsize, add is not None)
    nk = K // tk
    kind = "tn" if ta else ("nt" if tb else "nn")
    assert not (ta and tb)
    a_spec = pl.BlockSpec((tk, tm), lambda i, j, k: (k, i)) if ta else pl.BlockSpec((tm, tk), lambda i, j, k: (i, k))
    b_spec = pl.BlockSpec((tn, tk), lambda i, j, k: (j, k)) if tb else pl.BlockSpec((tk, tn), lambda i, j, k: (k, j))
    o_spec = pl.BlockSpec((tm, tn), lambda i, j, k: (i, j))
    has_add = add is not None

    def body(*refs):
        a_ref, b_ref = refs[0], refs[1]
        add_ref = refs[2] if has_add else None
        o_ref = refs[n_in]
        part = _dot_raw(a_ref[...], b_ref[...], kind)
        if nk == 1:
            if has_add:
                part = part + add_ref[...].astype(F32)
            o_ref[...] = part.astype(o_ref.dtype)
            return
        acc_ref = refs[-1]
        k = pl.program_id(2)

        @pl.when(k == 0)
        def _():
            acc_ref[...] = part

        @pl.when(k > 0)
        def _():
            acc_ref[...] += part

        @pl.when(k == nk - 1)
        def _():
            r = acc_ref[...]
            if has_add:
                r = r + add_ref[...].astype(F32)
            o_ref[...] = r.astype(o_ref.dtype)

    ins = [a, b] + ([add] if has_add else []) + ([after] if after is not None else [])
    in_specs = [a_spec, b_spec] + ([o_spec] if has_add else []) + ([_ANY] if after is not None else [])
    n_in = len(ins)
    return pl.pallas_call(
        body, name=name, grid=(M // tm, N // tn, nk), in_specs=in_specs, out_specs=o_spec,
        out_shape=jax.ShapeDtypeStruct((M, N), out_dtype),
        scratch_shapes=[pltpu.VMEM((tm, tn), F32)] if nk > 1 else [],
        compiler_params=_params(("parallel", "parallel", "arbitrary")),
    )(*ins)


def _mm_fused(a, b, fn, tiles, params, out_dtypes, sums, *, name, ta=False, tb=False, full_rows=False, after=None,
              b2=None, a2=None):
    K, M = a.shape if ta else a.shape[::-1]
    N, Kb = b.shape if tb else b.shape[::-1]
    assert K == Kb and not (ta and tb), (a.shape, b.shape, ta, tb)
    per_elem = sum(t.dtype.itemsize for t in tiles) + sum(jnp.dtype(d).itemsize for d in out_dtypes)
    n_b = 1 if b2 is None else 2
    n_a = 1 if a2 is None else 2
    tm, tn, tk = _mm_tiles(M, N, K, a.dtype.itemsize, b.dtype.itemsize, per_elem, False, full_rows, b2 is not None,
                           n_a, n_b)
    nk = K // tk
    kind = "tn" if ta else ("nt" if tb else "nn")
    a_spec = pl.BlockSpec((tk, tm), lambda i, j, k: (k, i)) if ta else pl.BlockSpec((tm, tk), lambda i, j, k: (i, k))
    b_spec = pl.BlockSpec((tn, tk), lambda i, j, k: (j, k)) if tb else pl.BlockSpec((tk, tn), lambda i, j, k: (k, j))
    o_spec = pl.BlockSpec((tm, tn), lambda i, j, k: (i, j))
    ins = [a, b] + ([b2] if b2 is not None else []) + ([a2] if a2 is not None else [])
    ins += list(tiles) + list(params) + ([after] if after is not None else [])
    in_specs = [a_spec] + [b_spec] * n_b + [a_spec] * (n_a - 1) + [o_spec] * len(tiles)
    in_specs += [pl.BlockSpec(p.shape, lambda i, j, k, nd=p.ndim: (0,) * nd) for p in params]
    in_specs += [_ANY] if after is not None else []
    n_in, n_t, n_p, n_o = len(ins), len(tiles), len(params), len(out_dtypes)

    def body(*refs):
        outs, sum_refs = refs[n_in:n_in + n_o], refs[n_in + n_o:n_in + n_o + len(sums)]

        def finish(*products):
            res = fn(*products, *[t[...] for t in refs[n_a + n_b:n_a + n_b + n_t + n_p]])
            for o_ref, v in zip(outs, res[:n_o]):
                o_ref[...] = v.astype(o_ref.dtype)
            first = jnp.logical_and(pl.program_id(0) == 0, pl.program_id(1) == 0)
            for s_ref, v in zip(sum_refs, res[n_o:]):
                @pl.when(first)
                def _(s_ref=s_ref, v=v):
                    s_ref[...] = v

                @pl.when(jnp.logical_not(first))
                def _(s_ref=s_ref, v=v):
                    s_ref[...] += v

        part = _dot_raw(refs[0][...], refs[1][...], kind)
        if nk == 1 and a2 is not None:
            finish(part + _dot_raw(refs[3][...], refs[2][...], kind))
            return
        if nk == 1:
            finish(part, *([_dot_raw(refs[0][...], refs[2][...], kind)] if b2 is not None else []))
            return
        acc_ref = refs[-1]
        k = pl.program_id(2)

        @pl.when(k == 0)
        def _():
            acc_ref[...] = part

        @pl.when(k > 0)
        def _():
            acc_ref[...] += part

        @pl.when(k == nk - 1)
        def _():
            finish(acc_ref[...])

    out_shape = [_sds((M, N), d) for d in out_dtypes] + list(sums)
    out_specs = [o_spec] * n_o + [pl.BlockSpec(s.shape, lambda i, j, k, nd=len(s.shape): (0,) * nd) for s in sums]
    order = ("arbitrary",) * 3 if sums else ("parallel", "parallel", "arbitrary")
    return pl.pallas_call(
        body, name=name, grid=(M // tm, N // tn, nk), in_specs=in_specs, out_specs=out_specs, out_shape=out_shape,
        scratch_shapes=[pltpu.VMEM((tm, tn), F32)] if nk > 1 else [], compiler_params=_params(order),
    )(*ins)


def _cols(arr, width, block):
    return (arr, width, block)


def _stage(fn, tiles, params, out_tiles, out_sums, *, name, tile=TOK_TILE, after=None):
    def tok_spec(shape, width=None, block=0):
        if len(shape) == 2:
            w = shape[1] if width is None else width
            return pl.BlockSpec((tile, w), lambda i: (i, block))
        return pl.BlockSpec((shape[0], tile, shape[2]), lambda i: (0, i, 0))

    arrays, in_specs = [], []
    for t in tiles:
        if isinstance(t, tuple):
            arr, width, block = t
            arrays.append(arr)
            in_specs.append(tok_spec(arr.shape, width, block))
        else:
            arrays.append(t)
            in_specs.append(tok_spec(t.shape))
    n_tok = arrays[0].shape[0] if arrays[0].ndim == 2 else arrays[0].shape[1]
    for p in params:
        arrays.append(p)
        in_specs.append(pl.BlockSpec(p.shape, lambda i, nd=p.ndim: (0,) * nd))
    out_shape = list(out_tiles) + list(out_sums)
    out_specs = [tok_spec(o.shape) for o in out_tiles]
    out_specs += [pl.BlockSpec(o.shape, lambda i, nd=len(o.shape): (0,) * nd) for o in out_sums]
    n_fn, n_ot = len(arrays), len(out_tiles)
    if after is not None:
        arrays.append(after)
        in_specs.append(_ANY)
    n_in = len(arrays)

    def body(*refs):
        res = fn(*[r[...] for r in refs[:n_fn]])
        if not isinstance(res, (tuple, list)):
            res = (res,)
        outs = refs[n_in:]
        for o_ref, r in zip(outs[:n_ot], res[:n_ot]):
            o_ref[...] = r.astype(o_ref.dtype)
        i = pl.program_id(0)
        for o_ref, r in zip(outs[n_ot:], res[n_ot:]):
            @pl.when(i == 0)
            def _(o_ref=o_ref, r=r):
                o_ref[...] = r.astype(o_ref.dtype)

            @pl.when(i > 0)
            def _(o_ref=o_ref, r=r):
                o_ref[...] += r.astype(o_ref.dtype)

    res = pl.pallas_call(
        body, name=name, grid=(n_tok // tile,), in_specs=in_specs, out_specs=out_specs, out_shape=out_shape,
        compiler_params=_params(("arbitrary",)),
    )(*arrays)
    return res


def _sds(shape, dtype):
    return jax.ShapeDtypeStruct(tuple(shape), dtype)


def _sigmoid(x):
    return 0.5 * jnp.tanh(0.5 * x) + 0.5


def _rms(x, g):
    return x * lax.rsqrt(jnp.mean(x * x, axis=-1, keepdims=True) + EPS) * g


def _norm_bwd_fn(x, dh, dres, g):
    _, vjp = jax.vjp(_rms, x, g)
    dx, dg = vjp(dh)
    return dx + dres, dg


def _mla_a_fn(cq, ckv, g_qa, g_kva):
    return _rms(cq, g_qa), _rms(ckv, g_kva)


def _mla_a_bwd_fn(cq, ckv, dqn, dkvn, g_qa, g_kva):
    _, vjp = jax.vjp(_mla_a_fn, cq, ckv, g_qa, g_kva)
    return vjp((dqn, dkvn))


def _rope(t, cos, sin):
    t1, t2 = t[:, :QK_ROPE // 2], t[:, QK_ROPE // 2:]
    return jnp.concatenate([t1 * cos - t2 * sin, t1 * sin + t2 * cos], axis=-1)


def _mla_b_fn(q_raw, kv_raw, kr, cos, sin, g_qn, g_kn):
    krope = kr[:, :QK_ROPE]
    qs, ks, vs = [], [], []
    for h in range(MLA_HEADS):
        qh = _rms(q_raw[:, h * QK_HEAD:(h + 1) * QK_HEAD], g_qn)
        kvh = kv_raw[:, h * (QK_NOPE + V_HEAD):(h + 1) * (QK_NOPE + V_HEAD)]
        kh = _rms(jnp.concatenate([kvh[:, :QK_NOPE], krope], axis=-1), g_kn)
        qs.append(jnp.concatenate([qh[:, :QK_NOPE], _rope(qh[:, QK_NOPE:], cos, sin)], axis=-1))
        ks.append(jnp.concatenate([kh[:, :QK_NOPE], _rope(kh[:, QK_NOPE:], cos, sin)], axis=-1))
        vs.append(kvh[:, QK_NOPE:])
    return jnp.stack(qs), jnp.stack(ks), jnp.stack(vs)


def _mla_b_bwd_fn(q_raw, kv_raw, kr, cos, sin, dq, dk, dv, g_qn, g_kn):
    _, vjp = jax.vjp(lambda a, b, c, d, e: _mla_b_fn(a, b, c, cos, sin, d, e), q_raw, kv_raw, kr, g_qn, g_kn)
    return vjp((dq, dk, dv))


def _post_fn(a, o_f, o_b, hg, g_hgo):
    o = o_f + o_b
    parts = [a]
    for h in range(HG_HEADS):
        s = slice(h * HG_DK, (h + 1) * HG_DK)
        gate = hg[:, s]
        parts.append(_rms(o[:, s], g_hgo[:, s]) * (gate * _sigmoid(gate)))
    return jnp.concatenate(parts, axis=-1)


def _post_bwd_fn(o_f, o_b, hg, dr, g_hgo):
    def f(o, hg, g):
        return _post_fn(jnp.zeros_like(o), o, jnp.zeros_like(o), hg, g)[:, o.shape[1]:]
    _, vjp = jax.vjp(f, o_f + o_b, hg, g_hgo)
    return vjp(dr)


def _swiglu_fn(gt, up):
    return gt * _sigmoid(gt) * up


def _resid_norm_fn(acc, x, g):
    x_new = acc + x
    return x_new, _rms(x_new, g)


def _swiglu_bwd_fn(gt, up, dact):
    _, vjp = jax.vjp(_swiglu_fn, gt, up)
    return vjp(dact)


def _ple_loss_fn(x2, pg, pp, target):
    gate = _sigmoid(pg)
    err = x2 + gate * pp - target
    dx3 = err * (1.0 / err.shape[-1])
    loss = 0.5 * jnp.sum(jnp.mean(err * err, axis=-1, keepdims=True), axis=0, keepdims=True)
    return dx3, dx3 * pp * gate * (1.0 - gate), dx3 * gate, loss


def _attention_fwd(q, k, v):
    H, T, D = q.shape
    DV = v.shape[-1]
    tq, ck = min(ATT_TQ, T), min(ATT_CHUNK, T)
    c2 = (D ** -0.5) * LOG2_E

    def body(q_ref, k_ref, v_ref, o_ref, lse_ref):
        q_i = q_ref[0]

        def chunk(c, carry):
            m, l, acc = carry
            rows = pl.ds(pl.multiple_of(c * ck, ck), ck)
            s = _dot_raw(q_i, k_ref[0, rows, :], "nt")
            m_new = jnp.maximum(m, jnp.max(s, axis=-1, keepdims=True))
            p = jnp.exp2((s - m_new) * c2)
            alpha = jnp.exp2((m - m_new) * c2)
            l = l * alpha + jnp.sum(p, axis=-1, keepdims=True)
            acc = acc * alpha + _dot_raw(p, v_ref[0, rows, :], "nn")
            return m_new, l, acc

        init = (jnp.full((tq, 1), -jnp.inf, F32), jnp.zeros((tq, 1), F32), jnp.zeros((tq, DV), F32))
        m, l, acc = lax.fori_loop(0, T // ck, chunk, init, unroll=True)
        o_ref[...] = acc / l
        lse_ref[0] = m * c2 + jnp.log2(l)

    return pl.pallas_call(
        body, name="attention_fwd", grid=(H, T // tq),
        in_specs=[pl.BlockSpec((1, tq, D), lambda h, i: (h, i, 0)),
                  pl.BlockSpec((1, T, D), lambda h, i: (h, 0, 0)),
                  pl.BlockSpec((1, T, DV), lambda h, i: (h, 0, 0))],
        out_specs=[pl.BlockSpec((tq, DV), lambda h, i: (i, h)),
                   pl.BlockSpec((1, tq, 1), lambda h, i: (h, i, 0))],
        out_shape=[_sds((T, H * DV), F32), _sds((H, T, 1), F32)],
        compiler_params=_params(("parallel", "parallel")),
    )(q, k, v)


def _attention_bwd(q, k, v, o, lse2, dmix):
    H, T, D = q.shape
    DV = v.shape[-1]
    tk, cq = min(ATT_TK, T), min(ATT_CHUNK, T)
    scale = D ** -0.5
    c2 = scale * LOG2_E

    def body(q_ref, k_ref, v_ref, o_ref, lse_ref, do_ref, dq_ref, dk_ref, dv_ref, delta_ref):
        j = pl.program_id(1)

        @pl.when(j == 0)
        def _():
            delta = lax.dot_general(jnp.ones((8, DV), F32), do_ref[...] * o_ref[...], (((1,), (1,)), ((), ())),
                                    precision=lax.Precision.HIGHEST, preferred_element_type=F32)
            for i in range(T // cq):
                delta_ref[i] = delta[:, i * cq:(i + 1) * cq]
            dq_ref[0] = jnp.zeros((T, D), F32)

        k_j, v_j = k_ref[0], v_ref[0]
        dk_ref[0] = jnp.zeros((tk, D), F32)
        dv_ref[0] = jnp.zeros((tk, DV), F32)

        def chunk(c, carry):
            rows = pl.ds(pl.multiple_of(c * cq, cq), cq)
            q_c = q_ref[0, rows, :]
            do_c = do_ref[rows, :].astype(BF16)
            st = _dot_raw(k_j, q_c, "nt")
            pt = jnp.exp2(st * c2 - lse_ref[0, c])
            dv_ref[0] += _dot_raw(pt, do_c, "nn")
            dpt = _dot_raw(v_j, do_c, "nt")
            dst = pt * (dpt - delta_ref[c, 0:1, :]) * scale
            dk_ref[0] += _dot_raw(dst, q_c, "nn")
            dq_ref[0, rows, :] += _dot_raw(dst, k_j, "tn")
            return carry

        lax.fori_loop(0, T // cq, chunk, 0, unroll=True)

    return pl.pallas_call(
        body, name="attention_bwd", grid=(H, T // tk),
        in_specs=[pl.BlockSpec((1, T, D), lambda h, j: (h, 0, 0)),
                  pl.BlockSpec((1, tk, D), lambda h, j: (h, j, 0)),
                  pl.BlockSpec((1, tk, DV), lambda h, j: (h, j, 0)),
                  pl.BlockSpec((T, DV), lambda h, j: (0, h)),
                  pl.BlockSpec((1, T // cq, 1, cq), lambda h, j: (h, 0, 0, 0)),
                  pl.BlockSpec((T, DV), lambda h, j: (0, h))],
        out_specs=[pl.BlockSpec((1, T, D), lambda h, j: (h, 0, 0)),
                   pl.BlockSpec((1, tk, D), lambda h, j: (h, j, 0)),
                   pl.BlockSpec((1, tk, DV), lambda h, j: (h, j, 0))],
        out_shape=[_sds((H, T, D), F32), _sds((H, T, D), F32), _sds((H, T, DV), F32)],
        scratch_shapes=[pltpu.VMEM((T // cq, 8, cq), F32)],
        compiler_params=_params(("parallel", "arbitrary")),
    )(q, k, v, o, lse2.reshape(H, T // cq, 1, cq), dmix)


def _split3_dot(ones, x, kind):
    hi = x.astype(BF16)
    rest = x - hi.astype(F32)
    mid = rest.astype(BF16)
    lo = (rest - mid.astype(F32)).astype(BF16)
    return (_dot_raw(ones, hi, kind) + _dot_raw(ones, mid, kind)) + _dot_raw(ones, lo, kind)


@jax.custom_vjp
def _running_sum(x, tri):
    return _split3_dot(tri, x, "nn")


def _running_sum_fwd(x, tri):
    return _split3_dot(tri, x, "nn"), tri


def _running_sum_bwd(tri, g):
    return _split3_dot(tri, g, "tn"), jnp.zeros_like(tri)


_running_sum.defvjp(_running_sum_fwd, _running_sum_bwd)


def _gla_block(hq, hf, hi, lower, st_in, *, rev, dot):
    rows, dk = hq.shape
    G, C = rows // CHUNK, CHUNK
    q = hq * _sigmoid(hq)
    f = lower + (1.0 - lower) * _sigmoid(hf)
    k = 1.0 - f
    logf = jnp.log2(f)
    q3, k3, v3, lf3 = (t.reshape(G, C, dk) for t in (q, k, hi, logf))
    r = lax.broadcasted_iota(jnp.int32, (C, C), 0)
    c = lax.broadcasted_iota(jnp.int32, (C, C), 1)
    tri = ((r <= c) if rev else (r >= c)).astype(F32)
    b = _running_sum(lf3, jnp.broadcast_to(tri, (G, C, C)))
    tpos = lax.broadcasted_iota(jnp.int32, (1, C, 1), 1)
    first_half = (tpos >= C // 2) if rev else (tpos <= C // 2 - 1)
    b_mid = jnp.sum(jnp.where(first_half, lf3, 0.0), axis=1, keepdims=True)
    b_last = jnp.sum(lf3, axis=1, keepdims=True)
    a = dot(q3 * jnp.exp2(b - b_mid), k3 * jnp.exp2(b_mid - b), "nt") * tri
    o_intra = dot(a, v3, "nn")
    kv_t = dot(v3, k3 * jnp.exp2(b_last - b), "tn")
    decay = jnp.exp2(b_last)
    qd = q3 * jnp.exp2(b)
    st = st_in
    o_inter = [None] * G
    for g in (reversed(range(G)) if rev else range(G)):
        o_inter[g] = dot(qd[g], st, "nt")
        st = st * decay[g] + kv_t[g]
    o = o_intra.reshape(rows, dk) + jnp.concatenate(o_inter, axis=0)
    return o, st


def _gla_fwd(z, lower3, *, rev, col_q, col_f, col_v, hp):
    T = z.shape[0]
    rows = min(GLA_GROUP * CHUNK, T)
    nb = T // rows
    wide = hp * HG_DK
    blk = (lambda n: nb - 1 - n) if rev else (lambda n: n)

    def body(hq_ref, hf_ref, hi_ref, low_ref, o_ref, st_out_ref, st_ref):
        @pl.when(pl.program_id(1) == 0)
        def _():
            st_ref[...] = jnp.zeros_like(st_ref)

        st_in = [st_ref[i] for i in range(hp)]
        heads = []
        for i in range(hp):
            cols = slice(i * HG_DK, (i + 1) * HG_DK)
            heads.append(_gla_block(hq_ref[:, cols], hf_ref[:, cols], hi_ref[:, cols], low_ref[i], st_in[i], rev=rev,
                                    dot=_dot_raw))
        for i, (o, st) in enumerate(heads):
            st_out_ref[i, 0] = st_in[i]
            o_ref[:, i * HG_DK:(i + 1) * HG_DK] = o
            st_ref[i] = st

    def zspec(col):
        return pl.BlockSpec((rows, wide), lambda h, n: (blk(n), col // wide + h))

    return pl.pallas_call(
        body, name="gla_fwd_rev" if rev else "gla_fwd", grid=(HG_HEADS // hp, nb),
        in_specs=[zspec(col_q), zspec(col_f), zspec(col_v), pl.BlockSpec((hp, 1, HG_DK), lambda h, n: (h, 0, 0))],
        out_specs=[pl.BlockSpec((rows, wide), lambda h, n: (blk(n), h)),
                   pl.BlockSpec((hp, 1, HG_DK, HG_DK), lambda h, n: (h, blk(n), 0, 0))],
        out_shape=[_sds((T, HG_HEADS * HG_DK), F32), _sds((HG_HEADS, nb, HG_DK, HG_DK), F32)],
        scratch_shapes=[pltpu.VMEM((hp, HG_DK, HG_DK), F32)],
        compiler_params=_params(("parallel", "arbitrary")),
    )(z, z, z, lower3)


def _gla_bwd(z, lower3, states, do, prev, *, rev, col_q, col_f, col_v, hp):
    T = z.shape[0]
    rows = min(GLA_GROUP * CHUNK, T)
    nb = T // rows
    wide = hp * HG_DK
    blk = (lambda n: n) if rev else (lambda n: nb - 1 - n)
    has_prev = prev is not None
    fn = functools.partial(_gla_block, rev=rev, dot=_bdot)

    def body(*refs):
        hq_ref, hf_ref, hi_ref, low_ref, st_ref, do_ref = refs[:6]
        rest = refs[6:]
        if has_prev:
            pq_ref, pi_ref = rest[:2]
            rest = rest[2:]
        dhq_ref, dhi_ref, dhf_ref, dlow_ref, dst_ref = rest
        n = pl.program_id(1)

        @pl.when(n == 0)
        def _():
            dst_ref[...] = jnp.zeros_like(dst_ref)

        dst_in = [dst_ref[i] for i in range(hp)]
        heads = []
        for i in range(hp):
            cols = slice(i * HG_DK, (i + 1) * HG_DK)
            _, vjp = jax.vjp(fn, hq_ref[:, cols], hf_ref[:, cols], hi_ref[:, cols], low_ref[i], st_ref[i, 0])
            dhq, dhf, dhi, dlow, dst = vjp((do_ref[:, cols], dst_in[i]))
            if has_prev:
                dhq = dhq + pq_ref[:, cols]
                dhi = dhi + pi_ref[:, cols]
            heads.append((dhq, dhf, dhi, dlow, dst))
        for i, (dhq, dhf, dhi, dlow, dst) in enumerate(heads):
            cols = slice(i * HG_DK, (i + 1) * HG_DK)
            dst_ref[i] = dst
            dhq_ref[:, cols] = dhq.astype(dhq_ref.dtype)
            dhi_ref[:, cols] = dhi.astype(dhi_ref.dtype)
            dhf_ref[:, cols] = dhf.astype(dhf_ref.dtype)

        @pl.when(n == 0)
        def _():
            for i in range(hp):
                dlow_ref[i] = heads[i][3]

        @pl.when(n > 0)
        def _():
            for i in range(hp):
                dlow_ref[i] += heads[i][3]

    def zspec(col):
        return pl.BlockSpec((rows, wide), lambda h, n: (blk(n), col // wide + h))

    hspec = pl.BlockSpec((rows, wide), lambda h, n: (blk(n), h))
    in_specs = [zspec(col_q), zspec(col_f), zspec(col_v), pl.BlockSpec((hp, 1, HG_DK), lambda h, n: (h, 0, 0)),
                pl.BlockSpec((hp, 1, HG_DK, HG_DK), lambda h, n: (h, blk(n), 0, 0)), hspec]
    ins = [z, z, z, lower3, states, do]
    if has_prev:
        in_specs += [hspec, hspec]
        ins += list(prev)
    full_wide = HG_HEADS * HG_DK
    acc_dtype = BF16 if has_prev else F32
    return pl.pallas_call(
        body, name="gla_bwd_rev" if rev else "gla_bwd", grid=(HG_HEADS // hp, nb),
        in_specs=in_specs,
        out_specs=[hspec, hspec, hspec, pl.BlockSpec((hp, 1, HG_DK), lambda h, n: (h, 0, 0))],
        out_shape=[_sds((T, full_wide), acc_dtype), _sds((T, full_wide), acc_dtype), _sds((T, full_wide), BF16),
                   _sds((HG_HEADS, 1, HG_DK), F32)],
        scratch_shapes=[pltpu.VMEM((hp, HG_DK, HG_DK), F32)],
        compiler_params=_params(("parallel", "arbitrary")),
    )(*ins)


def _lower_fn(lb):
    e = jnp.exp(lb - jnp.max(lb, axis=0, keepdims=True))
    return (e / jnp.sum(e, axis=0, keepdims=True))[0]


def _lower_bounds(lb):
    def body(lb_ref, o_ref):
        o_ref[...] = _lower_fn(lb_ref[...])
    return pl.pallas_call(body, name="lower_bounds", out_shape=_sds(lb.shape[1:], F32))(lb)


def _row_tile(r, cap=1024):
    best = None
    for t in range(16, min(r, cap) + 1, 16):
        if r % t == 0:
            best = t
    return best if best is not None else r


def _sum4(shards, recv, chip, *, name):
    _, R, C = shards.shape
    tr = _row_tile(R)

    def body(chip_ref, o_ref, r_ref, out_ref):
        out_ref[...] = ((o_ref[0].astype(F32) + r_ref[0].astype(F32)) + r_ref[1].astype(F32)) + r_ref[2].astype(F32)

    grid_spec = pltpu.PrefetchScalarGridSpec(
        num_scalar_prefetch=1, grid=(R // tr,),
        in_specs=[pl.BlockSpec((1, tr, C), lambda i, chip_ref: (chip_ref[0], i, 0)),
                  pl.BlockSpec((3, tr, C), lambda i, chip_ref: (0, i, 0))],
        out_specs=pl.BlockSpec((tr, C), lambda i, chip_ref: (i, 0)))
    return pl.pallas_call(
        body, name=name, grid_spec=grid_spec, out_shape=_sds((R, C), F32), compiler_params=_params(("parallel",)),
    )(chip, shards, recv)


def _adamw_math(w, g, m, v):
    m = ADAM_B1 * m + (1.0 - ADAM_B1) * g
    v = ADAM_B2 * v + (1.0 - ADAM_B2) * (g * g)
    m_hat = m / (1.0 - ADAM_B1 ** ADAM_STEP)
    v_hat = v / (1.0 - ADAM_B2 ** ADAM_STEP)
    delta = -ADAM_LR * (m_hat / (jnp.sqrt(v_hat) + ADAM_EPS) + ADAM_WD * w)
    return delta, m, v


def _adamw(w, g_a, g_b, m, v, *, name):
    R, C = w.shape
    tr = _row_tile(R)
    two = g_b is not None

    def body(*refs):
        w_ref, ga_ref = refs[0], refs[1]
        rest = refs[2:]
        g = ga_ref[...]
        if two:
            g = g + rest[0][...]
            rest = rest[1:]
        m_ref, v_ref, g_out, d_out, m_out, v_out = rest
        delta, m_new, v_new = _adamw_math(w_ref[...], g, m_ref[...], v_ref[...])
        g_out[...] = g
        d_out[...] = delta
        m_out[...] = m_new
        v_out[...] = v_new

    spec = pl.BlockSpec((tr, C), lambda i: (i, 0))
    ins = [w, g_a] + ([g_b] if two else []) + [m, v]
    return pl.pallas_call(
        body, name=name, grid=(R // tr,), in_specs=[spec] * len(ins), out_specs=[spec] * 4,
        out_shape=[_sds((R, C), F32)] * 4, compiler_params=_params(("parallel",)),
    )(*ins)


def _adamw_small(red, lb_full, ws, ms, vs):
    n = len(ws)

    def pieces(shape):
        out = []
        for j, idx in enumerate(itertools.product(*[range(d) for d in shape[:-1]])):
            out.append((idx[:-1] + (slice(idx[-1], idx[-1] + 1), slice(None)), j * shape[-1]))
        return out

    def body(*refs):
        red_ref, lb_ref = refs[0], refs[1]
        w_refs, m_refs, v_refs = refs[2:2 + n], refs[2 + n:2 + 2 * n], refs[2 + 2 * n:2 + 3 * n]
        out_refs = refs[2 + 3 * n:]
        chip = 2 * lax.axis_index("x") + lax.axis_index("y")
        n_f, shard = lb_ref.shape[-1], w_refs[n - 1].shape[-1]
        _, vjp = jax.vjp(_lower_fn, lb_ref[...])
        dlb = vjp(red_ref[8:10, 0:n_f])[0]
        for i in range(n):
            width = w_refs[i].shape[-1]
            for j, (at, lane) in enumerate(pieces(w_refs[i].shape)):
                if i < n - 1:
                    g = red_ref[i:i + 1, lane:lane + width]
                else:
                    row = dlb[j // 2][j % 2:j % 2 + 1]
                    g = sum(jnp.where(chip == q, row[:, q * shard:(q + 1) * shard], 0.0) for q in range(N_CHIPS))
                delta, m_new, v_new = _adamw_math(w_refs[i][at], g, m_refs[i][at], v_refs[i][at])
                for o_ref, val in zip(out_refs[4 * i:4 * i + 4], (g, delta, m_new, v_new)):
                    o_ref[at] = val

    return pl.pallas_call(
        body, name="adamw_small", out_shape=[_sds(w.shape, F32) for w in ws for _ in range(4)],
    )(red, lb_full, *ws, *ms, *vs)


def _chip_peers():
    x, y, c = lax.axis_index("x"), lax.axis_index("y"), lax.axis_index("c")
    return (x, y, c), 2 * x + y, [(1 - x, y), (x, 1 - y), (1 - x, 1 - y)]


_HBM = pl.BlockSpec(memory_space=pltpu.HBM)
_SEM = pl.BlockSpec(memory_space=pltpu.SEMAPHORE)
_EFFECT = pltpu.SideEffectType.DATAFLOW_SIDE_EFFECTING


def _exchange_copies(srcs, lands, sems, mode):
    (x, y, c), me, chips = _chip_peers()
    copies = []
    for t, (src, land) in enumerate(zip(srcs, lands)):
        if mode == "swap":
            copies.append(pltpu.make_async_remote_copy(src, land, sems[0].at[3 * t], sems[1].at[3 * t],
                                                       device_id=(x, y, 1 - c), device_id_type=MESH))
            continue
        for k, (px, py) in enumerate(chips):
            gather = mode == "gather"
            copies.append(pltpu.make_async_remote_copy(
                src if gather else src.at[2 * px + py], land.at[me] if gather else land.at[k],
                sems[0].at[3 * t + k], sems[1].at[3 * t + k], device_id=(px, py, c), device_id_type=MESH))
        if mode == "gather":
            copies.append(pltpu.make_async_copy(src, land.at[me], sems[2].at[t]))
    return copies


def _exchange_start(srcs, *, mode, name, after=None):
    n = len(srcs)
    n_sem = 3 if mode == "gather" else 2
    n_in = 2 * n + (after is not None)
    land_shape = {"gather": lambda s: (N_CHIPS,) + s.shape, "scatter": lambda s: (3,) + s.shape[1:], "swap": lambda s: s.shape}
    lands = [_sds(land_shape[mode](s), s.dtype) for s in srcs]

    def body(*refs):
        for cp in _exchange_copies(refs[:n], refs[n:2 * n], refs[n_in:n_in + n_sem], mode):
            cp.start()
        token = refs[-1]
        token[...] = jnp.zeros_like(token)

    sem_shapes = [pltpu.SemaphoreType.DMA((3 * n,)), pltpu.SemaphoreType.DMA((3 * n,))]
    sem_shapes += [pltpu.SemaphoreType.DMA((n,))] if mode == "gather" else []
    thru = [pltpu.HBM(s.shape, s.dtype) for s in srcs] + [pltpu.HBM(l.shape, l.dtype) for l in lands]
    res = pl.pallas_call(
        body, name=name, in_specs=[_HBM] * (2 * n) + [_ANY] * (after is not None),
        out_specs=[_SEM] * n_sem + [_HBM] * (2 * n) + [pl.BlockSpec(memory_space=pltpu.VMEM)],
        out_shape=sem_shapes + thru + [_sds((8, LANE), F32)], input_output_aliases={i: n_sem + i for i in range(2 * n)},
        compiler_params=pltpu.CompilerParams(has_side_effects=_EFFECT),
    )(*[pltpu.with_memory_space_constraint(s, pltpu.HBM) for s in srcs],
      *[pltpu.with_memory_space_constraint(lax.empty(l.shape, l.dtype), pltpu.HBM) for l in lands],
      *([after] if after is not None else []))
    return (res[:n_sem], res[n_sem:n_sem + n], res[n_sem + n:n_sem + 2 * n], mode), res[-1]


def _exchange_wait(started, after, *, name):
    sems, srcs, lands, mode = started
    n, n_sem = len(srcs), len(sems)
    after = list(after) if isinstance(after, (list, tuple)) else [after]

    def body(*refs):
        for cp in _exchange_copies(refs[:n], refs[n:2 * n], refs[2 * n:2 * n + n_sem], mode):
            cp.wait()

    res = pl.pallas_call(
        body, name=name, in_specs=[_HBM] * (2 * n) + [_SEM] * n_sem + [_ANY] * len(after), out_specs=[_HBM] * (2 * n),
        out_shape=[pltpu.HBM(a.shape, a.dtype) for a in list(srcs) + list(lands)],
        input_output_aliases={i: i for i in range(2 * n)},
        compiler_params=pltpu.CompilerParams(has_side_effects=_EFFECT),
    )(*srcs, *lands, *sems, *after)
    return res[:n], res[n:]


def _allreduce_small(pack):
    R, C = pack.shape

    def body(in_ref, out_ref, slots, send_sems, recv_sems):
        x, y, c = lax.axis_index("x"), lax.axis_index("y"), lax.axis_index("c")
        me = 4 * x + 2 * y + c
        slots[me] = in_ref[...]
        copies = []
        for k in range(1, N_DEV):
            peer = (x ^ ((k >> 2) & 1), y ^ ((k >> 1) & 1), c ^ (k & 1))
            cp = pltpu.make_async_remote_copy(in_ref, slots.at[me], send_sems.at[k - 1], recv_sems.at[k - 1],
                                              device_id=peer, device_id_type=MESH)
            cp.start()
            copies.append(cp)
        for cp in copies:
            cp.wait()
        acc = slots[0]
        for d in range(1, N_DEV):
            acc = acc + slots[d]
        out_ref[...] = acc

    return pl.pallas_call(
        body, name="allreduce_small", out_shape=_sds((R, C), F32),
        in_specs=[pl.BlockSpec(memory_space=pltpu.VMEM)], out_specs=pl.BlockSpec(memory_space=pltpu.VMEM),
        scratch_shapes=[pltpu.VMEM((N_DEV, R, C), F32), pltpu.SemaphoreType.DMA((N_DEV - 1,)),
                        pltpu.SemaphoreType.DMA((N_DEV - 1,))],
        compiler_params=_params(),
    )(pack)


_Z_CQ, _Z_CKV, _Z_HQ, _Z_HFF, _Z_HFB, _Z_HI, _Z_HG, _Z_KR, _Z_END = 0, 256, 512, 1024, 1536, 2048, 2560, 3072, 3200


def _to_z_layout(wt):
    pad = jnp.zeros((_Z_END - _Z_KR - QK_ROPE, wt.shape[1]), wt.dtype)
    return jnp.concatenate([wt[:512], wt[512 + QK_ROPE:], wt[512:512 + QK_ROPE], pad], axis=0)


def _from_z_layout(wt):
    return jnp.concatenate([wt[:512], wt[_Z_KR:_Z_KR + QK_ROPE], wt[512:_Z_KR]], axis=0)


def _col_shards_to_full(g):
    return jnp.transpose(g, (1, 0, 2)).reshape(g.shape[1], -1)


def _full_to_col_shards(w):
    r, c = w.shape
    return jnp.transpose(w.reshape(r, N_CHIPS, c // N_CHIPS), (1, 0, 2))


def _full_to_row_shards(w):
    r, c = w.shape
    return w.reshape(N_CHIPS, r // N_CHIPS, c)


def kernel(x, p, positions, g_mix, w_in, g_qa, g_kva, w_qb, w_kvb, g_qn, g_kn, lb_param, g_hgo, w_o, g_ffn, w_gate, w_up, w_down, g_ple, w_ple_gate, w_ple_proj, loss_target, m_g_mix, m_w_in, m_g_qa, m_g_kva, m_w_qb, m_w_kvb, m_g_qn, m_g_kn, m_lb_param, m_g_hgo, m_w_o, m_g_ffn, m_w_gate, m_w_up, m_w_down, m_g_ple, m_w_ple_gate, m_w_ple_proj, v_g_mix, v_w_in, v_g_qa, v_g_kva, v_w_qb, v_w_kvb, v_g_qn, v_g_kn, v_lb_param, v_g_hgo, v_w_o, v_g_ffn, v_w_gate, v_w_up, v_w_down, v_g_ple, v_w_ple_gate, v_w_ple_proj):
    w_named = dict(g_mix=g_mix, w_in=w_in, g_qa=g_qa, g_kva=g_kva, w_qb=w_qb, w_kvb=w_kvb, g_qn=g_qn, g_kn=g_kn,
                   lb_param=lb_param, g_hgo=g_hgo, w_o=w_o, g_ffn=g_ffn, w_gate=w_gate, w_up=w_up, w_down=w_down,
                   g_ple=g_ple, w_ple_gate=w_ple_gate, w_ple_proj=w_ple_proj)
    m_named = dict(g_mix=m_g_mix, w_in=m_w_in, g_qa=m_g_qa, g_kva=m_g_kva, w_qb=m_w_qb, w_kvb=m_w_kvb, g_qn=m_g_qn,
                   g_kn=m_g_kn, lb_param=m_lb_param, g_hgo=m_g_hgo, w_o=m_w_o, g_ffn=m_g_ffn, w_gate=m_w_gate,
                   w_up=m_w_up, w_down=m_w_down, g_ple=m_g_ple, w_ple_gate=m_w_ple_gate, w_ple_proj=m_w_ple_proj)
    v_named = dict(g_mix=v_g_mix, w_in=v_w_in, g_qa=v_g_qa, g_kva=v_g_kva, w_qb=v_w_qb, w_kvb=v_w_kvb, g_qn=v_g_qn,
                   g_kn=v_g_kn, lb_param=v_lb_param, g_hgo=v_g_hgo, w_o=v_w_o, g_ffn=v_g_ffn, w_gate=v_w_gate,
                   w_up=v_w_up, w_down=v_w_down, g_ple=v_g_ple, w_ple_gate=v_w_ple_gate, w_ple_proj=v_w_ple_proj)
    order = list(w_named)
    transposed = ("w_in", "w_qb", "w_gate", "w_up")
    col_sharded = ("w_kvb", "w_ple_proj")
    row_sharded = ("w_o", "w_down", "w_ple_gate")
    big = transposed + col_sharded + row_sharded

    def view(n, a):
        return jnp.transpose(a[0]) if n in transposed else a[0]

    def unview(n, a):
        return (jnp.transpose(a) if n in transposed else a)[None]

    def to_shards(n, g):
        return _full_to_col_shards(g) if n in col_sharded else _full_to_row_shards(g)

    x2d, p2d, tgt = x[0], p[0, 0], loss_target[0]
    T, D = x2d.shape

    lb_flat = lb_param.reshape(-1, lb_param.shape[-1])
    gather_groups = (("w_in",), ("w_qb", "w_kvb"), ("w_o", "w_gate", "w_up", "w_down", "w_ple_gate", "w_ple_proj"))
    gather_started = []

    casts = {n: view(n, w_named[n]).astype(BF16) for n in big}

    def gather_start(gi, after):
        srcs = [casts[n] for n in gather_groups[gi]] + ([lb_flat] if gi == 0 else [])
        started, token = _exchange_start(srcs, mode="gather", name=f"gather_start_{gi}", after=after)
        gather_started.append(started)
        return token

    full = {}

    def gather_wait(gi, after):
        _, got = _exchange_wait(gather_started[gi], after, name=f"gather_wait_{gi}")
        for n, g in zip(gather_groups[gi], got):
            full[n] = _col_shards_to_full(g) if n in col_sharded else g.reshape(-1, g.shape[-1])
        return got

    g_hgo_row = g_hgo.reshape(1, -1)

    inv_freq = ROPE_THETA ** (-jnp.arange(0, QK_ROPE, 2, dtype=F32) / QK_ROPE)
    ang = positions[0].astype(F32)[:, None] * inv_freq
    cos, sin = jnp.cos(ang), jnp.sin(ang)
    token = gather_start(0, None)
    h1 = _stage(_rms, [x2d], [g_mix], [_sds((T, D), BF16)], [], name="norm_mix", after=token)[0]
    got = gather_wait(0, [h1, cos, sin] + [casts[n] for g in gather_groups[1:] for n in g])
    token = got[0]
    for gi in range(1, len(gather_groups)):
        token = gather_start(gi, token)
    lb_full = _col_shards_to_full(got[-1]).reshape(lb_param.shape[0], lb_param.shape[1], -1)
    w_in_zt = _to_z_layout(full["w_in"])
    z = _mm(h1, w_in_zt, tb=True, name="in_proj", after=token)
    qn, kvn = _stage(_mla_a_fn, [_cols(z, 256, 0), _cols(z, 256, 1)], [g_qa, g_kva],
                     [_sds((T, 256), BF16), _sds((T, 256), BF16)], [], name="mla_latent_norm")
    gather_wait(1, qn)
    q_raw = _mm(qn, full["w_qb"], tb=True, name="q_up")
    kv_raw = _mm(kvn, full["w_kvb"], name="kv_up")
    kr = _cols(z, LANE, _Z_KR // LANE)
    q, k, v = _stage(_mla_b_fn, [q_raw, kv_raw, kr, cos, sin], [g_qn, g_kn],
                     [_sds((MLA_HEADS, T, QK_HEAD), BF16), _sds((MLA_HEADS, T, QK_HEAD), BF16),
                      _sds((MLA_HEADS, T, V_HEAD), BF16)], [], name="mla_qk_norm_rope")
    att, lse = _attention_fwd(q, k, v)

    lower = _lower_bounds(lb_full)
    lower3 = lower.reshape(2, HG_HEADS, 1, HG_DK)
    o_f, st_f = _gla_fwd(z, lower3[0], rev=False, col_q=_Z_HQ, col_f=_Z_HFF, col_v=_Z_HI, hp=GLA_FWD_HEADS)
    o_b, st_b = _gla_fwd(z, lower3[1], rev=True, col_q=_Z_HQ, col_f=_Z_HFB, col_v=_Z_HI, hp=GLA_FWD_HEADS)
    hg = _cols(z, 512, _Z_HG // 512)
    mix = _stage(_post_fn, [att, o_f, o_b, hg], [g_hgo_row], [_sds((T, att.shape[1] + o_f.shape[1]), BF16)], [],
                 name="mix_out")[0]
    gather_wait(2, mix)
    x1, h2 = _mm_fused(mix, full["w_o"], _resid_norm_fn, [x2d], [g_ffn], [F32, BF16], [], full_rows=True,
                       name="out_proj")
    gt, up, act = _mm_fused(h2, full["w_gate"], lambda gt, up: (gt, up, _swiglu_fn(gt, up)), [], [], [BF16, BF16, BF16],
                            [], tb=True, b2=full["w_up"], name="ffn_gate_up")
    x2, h3 = _mm_fused(act, full["w_down"], _resid_norm_fn, [x1], [g_ple], [F32, BF16], [], full_rows=True,
                       name="ffn_down")
    pp = _mm(p2d, full["w_ple_proj"], name="ple_proj")
    dx3, dpg, dpp, loss_part = _mm_fused(
        h3, full["w_ple_gate"], lambda acc, pp, x2, tgt: _ple_loss_fn(x2, acc, pp, tgt), [pp, x2, tgt], [],
        [F32, BF16, BF16], [_sds((1, 1), F32)], full_rows=True, name="ple_gate_loss")

    grads = {}
    scatter_groups = (("w_ple_proj", "w_ple_gate", "w_down", "w_gate", "w_up", "w_o"), ("w_qb", "w_kvb", "w_in"))
    scatter_started = []

    def scatter_start(gi):
        srcs = [to_shards(n, grads[n]) for n in scatter_groups[gi]]
        started, token = _exchange_start(srcs, mode="scatter", name=f"scatter_start_{gi}")
        scatter_started.append(started)
        return token

    chip = 2 * lax.axis_index("x") + lax.axis_index("y")
    swap_started = []

    def reduce_group(gi, after):
        shards, recvs = _exchange_wait(scatter_started[gi], after, name=f"scatter_wait_{gi}")
        sums = [_sum4(s, r, chip.reshape(1), name="sum_" + n) for n, s, r in zip(scatter_groups[gi], shards, recvs)]
        started, token = _exchange_start(sums, mode="swap", name=f"swap_start_{gi}")
        swap_started.append(started)
        return token

    grads["w_ple_proj"] = _mm(p2d, dpp, ta=True, out_dtype=BF16, name="d_w_ple_proj")
    grads["w_ple_gate"] = _mm(h3, dpg, ta=True, out_dtype=BF16, name="d_w_ple_gate")
    dx2, grads["g_ple"] = _mm_fused(
        dpg, full["w_ple_gate"], lambda acc, x2, dx3, g: _norm_bwd_fn(x2, acc, dx3, g), [x2, dx3], [g_ple],
        [F32], [_sds((1, D), F32)], tb=True, full_rows=True, name="d_h3_norm_ple_bwd")
    dgt, dup = _mm_fused(dx2, full["w_down"], lambda acc, gt, up: _swiglu_bwd_fn(gt.astype(F32), up.astype(F32), acc), [gt, up], [],
                         [BF16, BF16], [], tb=True, name="d_act_swiglu_bwd")
    grads["w_down"] = _mm(act, dx2, ta=True, out_dtype=BF16, name="d_w_down")
    grads["w_gate"] = _mm(dgt, h2, ta=True, out_dtype=BF16, name="d_w_gate")
    grads["w_up"] = _mm(dup, h2, ta=True, out_dtype=BF16, name="d_w_up")
    dx1, grads["g_ffn"] = _mm_fused(
        dgt, full["w_gate"], lambda acc, x1, dx2, g: _norm_bwd_fn(x1, acc, dx2, g), [x1, dx2], [g_ffn],
        [F32], [_sds((1, D), F32)], full_rows=True, a2=dup, b2=full["w_up"], name="d_h2_norm_ffn_bwd")
    dmix = _mm(dx1, full["w_o"], tb=True, name="d_mix")
    grads["w_o"] = _mm(mix, dx1, ta=True, out_dtype=BF16, name="d_w_o")

    token = scatter_start(0)
    half = MLA_HEADS * V_HEAD
    do, dhg, dg_hgo = _stage(_post_bwd_fn, [o_f, o_b, hg, _cols(dmix, half, 1)], [g_hgo_row],
                             [_sds((T, half), F32), _sds((T, half), BF16)], [_sds((1, half), F32)], name="mix_out_bwd",
                             after=token)
    grads["g_hgo"] = dg_hgo
    dhq_f, dhi_f, dhf_f, dlow_f = _gla_bwd(z, lower3[0], st_f, do, None, rev=False,
                                           col_q=_Z_HQ, col_f=_Z_HFF, col_v=_Z_HI, hp=GLA_BWD_HEADS)
    dhq, dhi, dhf_b, dlow_b = _gla_bwd(z, lower3[1], st_b, do, (dhq_f, dhi_f), rev=True,
                                       col_q=_Z_HQ, col_f=_Z_HFB, col_v=_Z_HI, hp=GLA_BWD_HEADS)

    dq, dk, dv = _attention_bwd(q, k, v, att, lse, dmix)
    dq_raw, dkv_raw, dkr, grads["g_qn"], grads["g_kn"] = _stage(
        _mla_b_bwd_fn, [q_raw, kv_raw, kr, cos, sin, dq, dk, dv], [g_qn, g_kn],
        [_sds(q_raw.shape, BF16), _sds(kv_raw.shape, BF16), _sds((T, LANE), BF16)],
        [_sds(g_qn.shape, F32), _sds(g_kn.shape, F32)], name="mla_qk_norm_rope_bwd")
    grads["w_qb"] = _mm(dq_raw, qn, ta=True, out_dtype=BF16, name="d_w_qb")
    grads["w_kvb"] = _mm(kvn, dkv_raw, ta=True, out_dtype=BF16, name="d_w_kvb")
    dqn = _mm(dq_raw, full["w_qb"], name="d_qn")
    dkvn = _mm(dkv_raw, full["w_kvb"], tb=True, name="d_kvn")
    dcq, dckv, grads["g_qa"], grads["g_kva"] = _stage(
        _mla_a_bwd_fn, [_cols(z, 256, 0), _cols(z, 256, 1), dqn, dkvn], [g_qa, g_kva],
        [_sds((T, 256), BF16), _sds((T, 256), BF16)], [_sds(g_qa.shape, F32), _sds(g_kva.shape, F32)],
        name="mla_latent_norm_bwd")
    token = reduce_group(0, dcq)
    dz = jnp.concatenate([dcq, dckv, dhq, dhf_f, dhf_b, dhi, dhg, dkr], axis=1)
    grads["w_in"] = _from_z_layout(_mm(dz, h1, ta=True, out_dtype=BF16, name="d_w_in", after=token))
    token = scatter_start(1)
    grad_x, grads["g_mix"] = _mm_fused(
        dz, w_in_zt, lambda acc, x, dx1, g: _norm_bwd_fn(x, acc, dx1, g), [x2d, dx1], [g_mix],
        [F32], [_sds((1, D), F32)], full_rows=True, name="d_h1_norm_mix_bwd", after=token)

    out_g, out_d, out_m, out_v = {}, {}, {}, {}

    def update_group(gi, after):
        mine, theirs = _exchange_wait(swap_started[gi], after, name=f"swap_wait_{gi}")
        for n, a, b in zip(scatter_groups[gi], mine, theirs):
            out_g[n], out_d[n], out_m[n], out_v[n] = (
                unview(n, t) for t in _adamw(view(n, w_named[n]), a, b, view(n, m_named[n]), view(n, v_named[n]),
                                             name="adamw_" + n))
        return out_v[scatter_groups[gi][-1]]

    done = update_group(0, grad_x)
    token = reduce_group(1, done)
    update_group(1, token)

    small = ("g_mix", "g_qa", "g_kva", "g_qn", "g_kn", "g_hgo", "g_ffn", "g_ple")
    small_all = small + ("lb_param",)
    width = -(-max(w_named[n].size for n in small_all) // LANE) * LANE

    def row(a):
        a = a.reshape(1, -1)
        return jnp.pad(a, ((0, 0), (0, width - a.shape[1])))

    dlower = jnp.concatenate([dlow_f.reshape(1, -1), dlow_b.reshape(1, -1)], axis=0)
    pack = jnp.concatenate([row(grads[n]) for n in small] + [row(dlower[0]), row(dlower[1]), row(loss_part)]
                           + [jnp.zeros((5, width), F32)], axis=0)
    red = _allreduce_small(pack)
    loss = red[10, 0]

    outs = _adamw_small(red, lb_full, [w_named[n] for n in small_all], [m_named[n] for n in small_all],
                        [v_named[n] for n in small_all])
    for i, n in enumerate(small_all):
        out_g[n], out_d[n], out_m[n], out_v[n] = outs[4 * i:4 * i + 4]

    return (loss, grad_x[None], *[out_g[n] for n in order], *[out_d[n] for n in order],
            *[out_m[n] for n in order], *[out_v[n] for n in order])
```

```python
import functools
import itertools

import jax
import jax.numpy as jnp
from jax import lax
from jax.experimental import pallas as pl
from jax.experimental.pallas import tpu as pltpu

F32 = jnp.float32
BF16 = jnp.bfloat16
MESH = pl.DeviceIdType.MESH

EPS = 1e-6
ROPE_THETA = 10000.0
MLA_HEADS = 4
QK_NOPE = 128
QK_ROPE = 64
QK_HEAD = QK_NOPE + QK_ROPE
V_HEAD = 128
HG_HEADS = 4
HG_DK = 128
CHUNK = 64
ADAM_LR = 0.001
ADAM_B1 = 0.9
ADAM_B2 = 0.999
ADAM_EPS = 1e-08
ADAM_WD = 0.01
ADAM_STEP = 10

LANE = 128
VMEM_LIMIT = 56 * 1024 * 1024
TOK_TILE = 256
GLA_GROUP = 16
GLA_FWD_HEADS = 4
GLA_BWD_HEADS = 2
ATT_TQ = 1024
ATT_TK = 1024
ATT_CHUNK = 512
LOG2_E = 1.4426950408889634
N_CHIPS = 4
N_DEV = 8


_ANY = pl.BlockSpec(memory_space=pl.ANY)


def _params(dims=None, **kw):
    return pltpu.CompilerParams(dimension_semantics=dims, vmem_limit_bytes=VMEM_LIMIT, **kw)


def _tile_candidates(n, cap):
    out = [d for d in range(LANE, min(n, cap) + 1, LANE) if n % d == 0]
    if n <= cap and n not in out:
        out.append(n)
    return out or [n]


MM_VMEM_BUDGET = 40 * 1024 * 1024
MM_MIN_ROWS = 256
MM_MAX_ROWS = 1536
HBM_BYTES_PER_S = 2.8e12
MXU_FLOPS_PER_S = 8e14
STEP_OVERHEAD_S = 0.35e-6


def _mm_tiles(M, N, K, a_bytes, b_bytes, o_bytes, has_add, full_rows=False, full_k=False, n_a=1, n_b=1):
    cast_a, cast_b = a_bytes > 2, b_bytes > 2
    a_bytes, b_bytes = n_a * a_bytes, n_b * b_bytes
    best = None
    for tm in [t for t in _tile_candidates(M, MM_MAX_ROWS) if t >= min(M, MM_MIN_ROWS)]:
        for tn in ([N] if full_rows else _tile_candidates(N, N)):
            for tk in ([K] if full_k else _tile_candidates(K, K)):
                ni, nj, nk = M // tm, N // tn, K // tk
                vmem = 2 * (tm * tk * a_bytes + tk * tn * b_bytes + tm * tn * o_bytes * (2 if has_add else 1))
                vmem += tm * tn * 4 * (2 if nk > 1 else 1)
                vmem += (tm * tk * 2 * n_a if cast_a else 0) + (tk * tn * 2 * n_b if cast_b else 0)
                if vmem > MM_VMEM_BUDGET:
                    continue
                moved = M * K * a_bytes * (nj if nk > 1 else 1) + K * N * b_bytes * (1 if nj == nk == 1 else ni)
                moved += M * N * o_bytes * (2 if has_add else 1)
                t = max(moved / HBM_BYTES_PER_S, 2 * M * N * K / MXU_FLOPS_PER_S) + ni * nj * nk * STEP_OVERHEAD_S
                if best is None or t < best[0]:
                    best = (t, tm, tn, tk)
    assert best is not None, (M, N, K)
    return best[1:]


def _dot_raw(a, b, kind):
    nb = a.ndim - 2
    batch = ((0,), (0,)) if nb else ((), ())
    ca = nb if kind == "tn" else nb + 1
    cb = nb + 1 if kind == "nt" else nb
    return lax.dot_general(a.astype(BF16), b.astype(BF16), (((ca,), (cb,)), batch), preferred_element_type=F32)


@functools.partial(jax.custom_vjp, nondiff_argnums=(2,))
def _bdot(a, b, kind):
    return _dot_raw(a, b, kind)


def _bdot_fwd(a, b, kind):
    return _dot_raw(a, b, kind), (a, b)


def _bdot_bwd(kind, res, g):
    a, b = res
    if kind == "nn":
        da, db = _bdot(g, b, "nt"), _bdot(a, g, "tn")
    elif kind == "nt":
        da, db = _bdot(g, b, "nn"), _bdot(g, a, "tn")
    else:
        da, db = _bdot(b, g, "nt"), _bdot(a, g, "nn")
    return da.astype(a.dtype), db.astype(b.dtype)


_bdot.defvjp(_bdot_fwd, _bdot_bwd)


def _mm(a, b, *, name, ta=False, tb=False, add=None, out_dtype=F32, after=None):
    K, M = a.shape if ta else a.shape[::-1]
    N, Kb = b.shape if tb else b.shape[::-1]
    assert K == Kb, (a.shape, b.shape, ta, tb)
    tm, tn, tk = _mm_tiles(M, N, K, a.dtype.itemsize, b.dtype.itemsize, jnp.dtype(out_dtype).itemsize, add is not None)
    nk = K // tk
    kind = "tn" if ta else ("nt" if tb else "nn")
    assert not (ta and tb)
    a_spec = pl.BlockSpec((tk, tm), lambda i, j, k: (k, i)) if ta else pl.BlockSpec((tm, tk), lambda i, j, k: (i, k))
    b_spec = pl.BlockSpec((tn, tk), lambda i, j, k: (j, k)) if tb else pl.BlockSpec((tk, tn), lambda i, j, k: (k, j))
    o_spec = pl.BlockSpec((tm, tn), lambda i, j, k: (i, j))
    has_add = add is not None

    def body(*refs):
        a_ref, b_ref = refs[0], refs[1]
        add_ref = refs[2] if has_add else None
        o_ref = refs[n_in]
        part = _dot_raw(a_ref[...], b_ref[...], kind)
        if nk == 1:
            if has_add:
                part = part + add_ref[...].astype(F32)
            o_ref[...] = part.astype(o_ref.dtype)
            return
        acc_ref = refs[-1]
        k = pl.program_id(2)

        @pl.when(k == 0)
        def _():
            acc_ref[...] = part

        @pl.when(k > 0)
        def _():
            acc_ref[...] += part

        @pl.when(k == nk - 1)
        def _():
            r = acc_ref[...]
            if has_add:
                r = r + add_ref[...].astype(F32)
            o_ref[...] = r.astype(o_ref.dtype)

    ins = [a, b] + ([add] if has_add else []) + ([after] if after is not None else [])
    in_specs = [a_spec, b_spec] + ([o_spec] if has_add else []) + ([_ANY] if after is not None else [])
    n_in = len(ins)
    return pl.pallas_call(
        body, name=name, grid=(M // tm, N // tn, nk), in_specs=in_specs, out_specs=o_spec,
        out_shape=jax.ShapeDtypeStruct((M, N), out_dtype),
        scratch_shapes=[pltpu.VMEM((tm, tn), F32)] if nk > 1 else [],
        compiler_params=_params(("parallel", "parallel", "arbitrary")),
    )(*ins)


def _mm_fused(a, b, fn, tiles, params, out_dtypes, sums, *, name, ta=False, tb=False, full_rows=False, after=None,
              b2=None, a2=None):
    K, M = a.shape if ta else a.shape[::-1]
    N, Kb = b.shape if tb else b.shape[::-1]
    assert K == Kb and not (ta and tb), (a.shape, b.shape, ta, tb)
    per_elem = sum(t.dtype.itemsize for t in tiles) + sum(jnp.dtype(d).itemsize for d in out_dtypes)
    n_b = 1 if b2 is None else 2
    n_a = 1 if a2 is None else 2
    tm, tn, tk = _mm_tiles(M, N, K, a.dtype.itemsize, b.dtype.itemsize, per_elem, False, full_rows, b2 is not None,
                           n_a, n_b)
    nk = K // tk
    kind = "tn" if ta else ("nt" if tb else "nn")
    a_spec = pl.BlockSpec((tk, tm), lambda i, j, k: (k, i)) if ta else pl.BlockSpec((tm, tk), lambda i, j, k: (i, k))
    b_spec = pl.BlockSpec((tn, tk), lambda i, j, k: (j, k)) if tb else pl.BlockSpec((tk, tn), lambda i, j, k: (k, j))
    o_spec = pl.BlockSpec((tm, tn), lambda i, j, k: (i, j))
    ins = [a, b] + ([b2] if b2 is not None else []) + ([a2] if a2 is not None else [])
    ins += list(tiles) + list(params) + ([after] if after is not None else [])
    in_specs = [a_spec] + [b_spec] * n_b + [a_spec] * (n_a - 1) + [o_spec] * len(tiles)
    in_specs += [pl.BlockSpec(p.shape, lambda i, j, k, nd=p.ndim: (0,) * nd) for p in params]
    in_specs += [_ANY] if after is not None else []
    n_in, n_t, n_p, n_o = len(ins), len(tiles), len(params), len(out_dtypes)

    def body(*refs):
        outs, sum_refs = refs[n_in:n_in + n_o], refs[n_in + n_o:n_in + n_o + len(sums)]

        def finish(*products):
            res = fn(*products, *[t[...] for t in refs[n_a + n_b:n_a + n_b + n_t + n_p]])
            for o_ref, v in zip(outs, res[:n_o]):
                o_ref[...] = v.astype(o_ref.dtype)
            first = jnp.logical_and(pl.program_id(0) == 0, pl.program_id(1) == 0)
            for s_ref, v in zip(sum_refs, res[n_o:]):
                @pl.when(first)
                def _(s_ref=s_ref, v=v):
                    s_ref[...] = v

                @pl.when(jnp.logical_not(first))
                def _(s_ref=s_ref, v=v):
                    s_ref[...] += v

        part = _dot_raw(refs[0][...], refs[1][...], kind)
        if nk == 1 and a2 is not None:
            finish(part + _dot_raw(refs[3][...], refs[2][...], kind))
            return
        if nk == 1:
            finish(part, *([_dot_raw(refs[0][...], refs[2][...], kind)] if b2 is not None else []))
            return
        acc_ref = refs[-1]
        k = pl.program_id(2)

        @pl.when(k == 0)
        def _():
            acc_ref[...] = part

        @pl.when(k > 0)
        def _():
            acc_ref[...] += part

        @pl.when(k == nk - 1)
        def _():
            finish(acc_ref[...])

    out_shape = [_sds((M, N), d) for d in out_dtypes] + list(sums)
    out_specs = [o_spec] * n_o + [pl.BlockSpec(s.shape, lambda i, j, k, nd=len(s.shape): (0,) * nd) for s in sums]
    order = ("arbitrary",) * 3 if sums else ("parallel", "parallel", "arbitrary")
    return pl.pallas_call(
        body, name=name, grid=(M // tm, N // tn, nk), in_specs=in_specs, out_specs=out_specs, out_shape=out_shape,
        scratch_shapes=[pltpu.VMEM((tm, tn), F32)] if nk > 1 else [], compiler_params=_params(order),
    )(*ins)


def _cols(arr, width, block):
    return (arr, width, block)


def _stage(fn, tiles, params, out_tiles, out_sums, *, name, tile=TOK_TILE, after=None):
    def tok_spec(shape, width=None, block=0):
        if len(shape) == 2:
            w = shape[1] if width is None else width
            return pl.BlockSpec((tile, w), lambda i: (i, block))
        return pl.BlockSpec((shape[0], tile, shape[2]), lambda i: (0, i, 0))

    arrays, in_specs = [], []
    for t in tiles:
        if isinstance(t, tuple):
            arr, width, block = t
            arrays.append(arr)
            in_specs.append(tok_spec(arr.shape, width, block))
        else:
            arrays.append(t)
            in_specs.append(tok_spec(t.shape))
    n_tok = arrays[0].shape[0] if arrays[0].ndim == 2 else arrays[0].shape[1]
    for p in params:
        arrays.append(p)
        in_specs.append(pl.BlockSpec(p.shape, lambda i, nd=p.ndim: (0,) * nd))
    out_shape = list(out_tiles) + list(out_sums)
    out_specs = [tok_spec(o.shape) for o in out_tiles]
    out_specs += [pl.BlockSpec(o.shape, lambda i, nd=len(o.shape): (0,) * nd) for o in out_sums]
    n_fn, n_ot = len(arrays), len(out_tiles)
    if after is not None:
        arrays.append(after)
        in_specs.append(_ANY)
    n_in = len(arrays)

    def body(*refs):
        res = fn(*[r[...] for r in refs[:n_fn]])
        if not isinstance(res, (tuple, list)):
            res = (res,)
        outs = refs[n_in:]
        for o_ref, r in zip(outs[:n_ot], res[:n_ot]):
            o_ref[...] = r.astype(o_ref.dtype)
        i = pl.program_id(0)
        for o_ref, r in zip(outs[n_ot:], res[n_ot:]):
            @pl.when(i == 0)
            def _(o_ref=o_ref, r=r):
                o_ref[...] = r.astype(o_ref.dtype)

            @pl.when(i > 0)
            def _(o_ref=o_ref, r=r):
                o_ref[...] += r.astype(o_ref.dtype)

    res = pl.pallas_call(
        body, name=name, grid=(n_tok // tile,), in_specs=in_specs, out_specs=out_specs, out_shape=out_shape,
        compiler_params=_params(("arbitrary",)),
    )(*arrays)
    return res


def _sds(shape, dtype):
    return jax.ShapeDtypeStruct(tuple(shape), dtype)


def _sigmoid(x):
    return 0.5 * jnp.tanh(0.5 * x) + 0.5


def _rms(x, g):
    return x * lax.rsqrt(jnp.mean(x * x, axis=-1, keepdims=True) + EPS) * g


def _norm_bwd_fn(x, dh, dres, g):
    _, vjp = jax.vjp(_rms, x, g)
    dx, dg = vjp(dh)
    return dx + dres, dg


def _mla_a_fn(cq, ckv, g_qa, g_kva):
    return _rms(cq, g_qa), _rms(ckv, g_kva)


def _mla_a_bwd_fn(cq, ckv, dqn, dkvn, g_qa, g_kva):
    _, vjp = jax.vjp(_mla_a_fn, cq, ckv, g_qa, g_kva)
    return vjp((dqn, dkvn))


def _rope(t, cos, sin):
    t1, t2 = t[:, :QK_ROPE // 2], t[:, QK_ROPE // 2:]
    return jnp.concatenate([t1 * cos - t2 * sin, t1 * sin + t2 * cos], axis=-1)


def _mla_b_fn(q_raw, kv_raw, kr, cos, sin, g_qn, g_kn):
    krope = kr[:, :QK_ROPE]
    qs, ks, vs = [], [], []
    for h in range(MLA_HEADS):
        qh = _rms(q_raw[:, h * QK_HEAD:(h + 1) * QK_HEAD], g_qn)
        kvh = kv_raw[:, h * (QK_NOPE + V_HEAD):(h + 1) * (QK_NOPE + V_HEAD)]
        kh = _rms(jnp.concatenate([kvh[:, :QK_NOPE], krope], axis=-1), g_kn)
        qs.append(jnp.concatenate([qh[:, :QK_NOPE], _rope(qh[:, QK_NOPE:], cos, sin)], axis=-1))
        ks.append(jnp.concatenate([kh[:, :QK_NOPE], _rope(kh[:, QK_NOPE:], cos, sin)], axis=-1))
        vs.append(kvh[:, QK_NOPE:])
    return jnp.stack(qs), jnp.stack(ks), jnp.stack(vs)


def _mla_b_bwd_fn(q_raw, kv_raw, kr, cos, sin, dq, dk, dv, g_qn, g_kn):
    _, vjp = jax.vjp(lambda a, b, c, d, e: _mla_b_fn(a, b, c, cos, sin, d, e), q_raw, kv_raw, kr, g_qn, g_kn)
    return vjp((dq, dk, dv))


def _post_fn(a, o_f, o_b, hg, g_hgo):
    o = o_f + o_b
    parts = [a]
    for h in range(HG_HEADS):
        s = slice(h * HG_DK, (h + 1) * HG_DK)
        gate = hg[:, s]
        parts.append(_rms(o[:, s], g_hgo[:, s]) * (gate * _sigmoid(gate)))
    return jnp.concatenate(parts, axis=-1)


def _post_bwd_fn(o_f, o_b, hg, dr, g_hgo):
    def f(o, hg, g):
        return _post_fn(jnp.zeros_like(o), o, jnp.zeros_like(o), hg, g)[:, o.shape[1]:]
    _, vjp = jax.vjp(f, o_f + o_b, hg, g_hgo)
    return vjp(dr)


def _swiglu_fn(gt, up):
    return gt * _sigmoid(gt) * up


def _resid_norm_fn(acc, x, g):
    x_new = acc + x
    return x_new, _rms(x_new, g)


def _swiglu_bwd_fn(gt, up, dact):
    _, vjp = jax.vjp(_swiglu_fn, gt, up)
    return vjp(dact)


def _ple_loss_fn(x2, pg, pp, target):
    gate = _sigmoid(pg)
    err = x2 + gate * pp - target
    dx3 = err * (1.0 / err.shape[-1])
    loss = 0.5 * jnp.sum(jnp.mean(err * err, axis=-1, keepdims=True), axis=0, keepdims=True)
    return dx3, dx3 * pp * gate * (1.0 - gate), dx3 * gate, loss


def _attention_fwd(q, k, v):
    H, T, D = q.shape
    DV = v.shape[-1]
    tq, ck = min(ATT_TQ, T), min(ATT_CHUNK, T)
    c2 = (D ** -0.5) * LOG2_E

    def body(q_ref, k_ref, v_ref, o_ref, lse_ref):
        q_i = q_ref[0]

        def chunk(c, carry):
            m, l, acc = carry
            rows = pl.ds(pl.multiple_of(c * ck, ck), ck)
            s = _dot_raw(q_i, k_ref[0, rows, :], "nt")
            m_new = jnp.maximum(m, jnp.max(s, axis=-1, keepdims=True))
            p = jnp.exp2((s - m_new) * c2)
            alpha = jnp.exp2((m - m_new) * c2)
            l = l * alpha + jnp.sum(p, axis=-1, keepdims=True)
            acc = acc * alpha + _dot_raw(p, v_ref[0, rows, :], "nn")
            return m_new, l, acc

        init = (jnp.full((tq, 1), -jnp.inf, F32), jnp.zeros((tq, 1), F32), jnp.zeros((tq, DV), F32))
        m, l, acc = lax.fori_loop(0, T // ck, chunk, init, unroll=True)
        o_ref[...] = acc / l
        lse_ref[0] = m * c2 + jnp.log2(l)

    return pl.pallas_call(
        body, name="attention_fwd", grid=(H, T // tq),
        in_specs=[pl.BlockSpec((1, tq, D), lambda h, i: (h, i, 0)),
                  pl.BlockSpec((1, T, D), lambda h, i: (h, 0, 0)),
                  pl.BlockSpec((1, T, DV), lambda h, i: (h, 0, 0))],
        out_specs=[pl.BlockSpec((tq, DV), lambda h, i: (i, h)),
                   pl.BlockSpec((1, tq, 1), lambda h, i: (h, i, 0))],
        out_shape=[_sds((T, H * DV), F32), _sds((H, T, 1), F32)],
        compiler_params=_params(("parallel", "parallel")),
    )(q, k, v)


def _attention_bwd(q, k, v, o, lse2, dmix):
    H, T, D = q.shape
    DV = v.shape[-1]
    tk, cq = min(ATT_TK, T), min(ATT_CHUNK, T)
    scale = D ** -0.5
    c2 = scale * LOG2_E

    def body(q_ref, k_ref, v_ref, o_ref, lse_ref, do_ref, dq_ref, dk_ref, dv_ref, delta_ref):
        j = pl.program_id(1)

        @pl.when(j == 0)
        def _():
            delta = lax.dot_general(jnp.ones((8, DV), F32), do_ref[...] * o_ref[...], (((1,), (1,)), ((), ())),
                                    precision=lax.Precision.HIGHEST, preferred_element_type=F32)
            for i in range(T // cq):
                delta_ref[i] = delta[:, i * cq:(i + 1) * cq]
            dq_ref[0] = jnp.zeros((T, D), F32)

        k_j, v_j = k_ref[0], v_ref[0]
        dk_ref[0] = jnp.zeros((tk, D), F32)
        dv_ref[0] = jnp.zeros((tk, DV), F32)

        def chunk(c, carry):
            rows = pl.ds(pl.multiple_of(c * cq, cq), cq)
            q_c = q_ref[0, rows, :]
            do_c = do_ref[rows, :].astype(BF16)
            st = _dot_raw(k_j, q_c, "nt")
            pt = jnp.exp2(st * c2 - lse_ref[0, c])
            dv_ref[0] += _dot_raw(pt, do_c, "nn")
            dpt = _dot_raw(v_j, do_c, "nt")
            dst = pt * (dpt - delta_ref[c, 0:1, :]) * scale
            dk_ref[0] += _dot_raw(dst, q_c, "nn")
            dq_ref[0, rows, :] += _dot_raw(dst, k_j, "tn")
            return carry

        lax.fori_loop(0, T // cq, chunk, 0, unroll=True)

    return pl.pallas_call(
        body, name="attention_bwd", grid=(H, T // tk),
        in_specs=[pl.BlockSpec((1, T, D), lambda h, j: (h, 0, 0)),
                  pl.BlockSpec((1, tk, D), lambda h, j: (h, j, 0)),
                  pl.BlockSpec((1, tk, DV), lambda h, j: (h, j, 0)),
                  pl.BlockSpec((T, DV), lambda h, j: (0, h)),
                  pl.BlockSpec((1, T // cq, 1, cq), lambda h, j: (h, 0, 0, 0)),
                  pl.BlockSpec((T, DV), lambda h, j: (0, h))],
        out_specs=[pl.BlockSpec((1, T, D), lambda h, j: (h, 0, 0)),
                   pl.BlockSpec((1, tk, D), lambda h, j: (h, j, 0)),
                   pl.BlockSpec((1, tk, DV), lambda h, j: (h, j, 0))],
        out_shape=[_sds((H, T, D), F32), _sds((H, T, D), F32), _sds((H, T, DV), F32)],
        scratch_shapes=[pltpu.VMEM((T // cq, 8, cq), F32)],
        compiler_params=_params(("parallel", "arbitrary")),
    )(q, k, v, o, lse2.reshape(H, T // cq, 1, cq), dmix)


def _split3_dot(ones, x, kind):
    hi = x.astype(BF16)
    rest = x - hi.astype(F32)
    mid = rest.astype(BF16)
    lo = (rest - mid.astype(F32)).astype(BF16)
    return (_dot_raw(ones, hi, kind) + _dot_raw(ones, mid, kind)) + _dot_raw(ones, lo, kind)


@jax.custom_vjp
def _running_sum(x, tri):
    return _split3_dot(tri, x, "nn")


def _running_sum_fwd(x, tri):
    return _split3_dot(tri, x, "nn"), tri


def _running_sum_bwd(tri, g):
    return _split3_dot(tri, g, "tn"), jnp.zeros_like(tri)


_running_sum.defvjp(_running_sum_fwd, _running_sum_bwd)


def _gla_block(hq, hf, hi, lower, st_in, *, rev, dot):
    rows, dk = hq.shape
    G, C = rows // CHUNK, CHUNK
    q = hq * _sigmoid(hq)
    f = lower + (1.0 - lower) * _sigmoid(hf)
    k = 1.0 - f
    logf = jnp.log2(f)
    q3, k3, v3, lf3 = (t.reshape(G, C, dk) for t in (q, k, hi, logf))
    r = lax.broadcasted_iota(jnp.int32, (C, C), 0)
    c = lax.broadcasted_iota(jnp.int32, (C, C), 1)
    tri = ((r <= c) if rev else (r >= c)).astype(F32)
    b = _running_sum(lf3, jnp.broadcast_to(tri, (G, C, C)))
    tpos = lax.broadcasted_iota(jnp.int32, (1, C, 1), 1)
    first_half = (tpos >= C // 2) if rev else (tpos <= C // 2 - 1)
    b_mid = jnp.sum(jnp.where(first_half, lf3, 0.0), axis=1, keepdims=True)
    b_last = jnp.sum(lf3, axis=1, keepdims=True)
    a = dot(q3 * jnp.exp2(b - b_mid), k3 * jnp.exp2(b_mid - b), "nt") * tri
    o_intra = dot(a, v3, "nn")
    kv_t = dot(v3, k3 * jnp.exp2(b_last - b), "tn")
    decay = jnp.exp2(b_last)
    qd = q3 * jnp.exp2(b)
    st = st_in
    o_inter = [None] * G
    for g in (reversed(range(G)) if rev else range(G)):
        o_inter[g] = dot(qd[g], st, "nt")
        st = st * decay[g] + kv_t[g]
    o = o_intra.reshape(rows, dk) + jnp.concatenate(o_inter, axis=0)
    return o, st


def _gla_fwd(z, lower3, *, rev, col_q, col_f, col_v, hp):
    T = z.shape[0]
    rows = min(GLA_GROUP * CHUNK, T)
    nb = T // rows
    wide = hp * HG_DK
    blk = (lambda n: nb - 1 - n) if rev else (lambda n: n)

    def body(hq_ref, hf_ref, hi_ref, low_ref, o_ref, st_out_ref, st_ref):
        @pl.when(pl.program_id(1) == 0)
        def _():
            st_ref[...] = jnp.zeros_like(st_ref)

        st_in = [st_ref[i] for i in range(hp)]
        heads = []
        for i in range(hp):
            cols = slice(i * HG_DK, (i + 1) * HG_DK)
            heads.append(_gla_block(hq_ref[:, cols], hf_ref[:, cols], hi_ref[:, cols], low_ref[i], st_in[i], rev=rev,
                                    dot=_dot_raw))
        for i, (o, st) in enumerate(heads):
            st_out_ref[i, 0] = st_in[i]
            o_ref[:, i * HG_DK:(i + 1) * HG_DK] = o
            st_ref[i] = st

    def zspec(col):
        return pl.BlockSpec((rows, wide), lambda h, n: (blk(n), col // wide + h))

    return pl.pallas_call(
        body, name="gla_fwd_rev" if rev else "gla_fwd", grid=(HG_HEADS // hp, nb),
        in_specs=[zspec(col_q), zspec(col_f), zspec(col_v), pl.BlockSpec((hp, 1, HG_DK), lambda h, n: (h, 0, 0))],
        out_specs=[pl.BlockSpec((rows, wide), lambda h, n: (blk(n), h)),
                   pl.BlockSpec((hp, 1, HG_DK, HG_DK), lambda h, n: (h, blk(n), 0, 0))],
        out_shape=[_sds((T, HG_HEADS * HG_DK), F32), _sds((HG_HEADS, nb, HG_DK, HG_DK), F32)],
        scratch_shapes=[pltpu.VMEM((hp, HG_DK, HG_DK), F32)],
        compiler_params=_params(("parallel", "arbitrary")),
    )(z, z, z, lower3)


def _gla_bwd(z, lower3, states, do, prev, *, rev, col_q, col_f, col_v, hp):
    T = z.shape[0]
    rows = min(GLA_GROUP * CHUNK, T)
    nb = T // rows
    wide = hp * HG_DK
    blk = (lambda n: n) if rev else (lambda n: nb - 1 - n)
    has_prev = prev is not None
    fn = functools.partial(_gla_block, rev=rev, dot=_bdot)

    def body(*refs):
        hq_ref, hf_ref, hi_ref, low_ref, st_ref, do_ref = refs[:6]
        rest = refs[6:]
        if has_prev:
            pq_ref, pi_ref = rest[:2]
            rest = rest[2:]
        dhq_ref, dhi_ref, dhf_ref, dlow_ref, dst_ref = rest
        n = pl.program_id(1)

        @pl.when(n == 0)
        def _():
            dst_ref[...] = jnp.zeros_like(dst_ref)

        dst_in = [dst_ref[i] for i in range(hp)]
        heads = []
        for i in range(hp):
            cols = slice(i * HG_DK, (i + 1) * HG_DK)
            _, vjp = jax.vjp(fn, hq_ref[:, cols], hf_ref[:, cols], hi_ref[:, cols], low_ref[i], st_ref[i, 0])
            dhq, dhf, dhi, dlow, dst = vjp((do_ref[:, cols], dst_in[i]))
            if has_prev:
                dhq = dhq + pq_ref[:, cols]
                dhi = dhi + pi_ref[:, cols]
            heads.append((dhq, dhf, dhi, dlow, dst))
        for i, (dhq, dhf, dhi, dlow, dst) in enumerate(heads):
            cols = slice(i * HG_DK, (i + 1) * HG_DK)
            dst_ref[i] = dst
            dhq_ref[:, cols] = dhq.astype(dhq_ref.dtype)
            dhi_ref[:, cols] = dhi.astype(dhi_ref.dtype)
            dhf_ref[:, cols] = dhf.astype(dhf_ref.dtype)

        @pl.when(n == 0)
        def _():
            for i in range(hp):
                dlow_ref[i] = heads[i][3]

        @pl.when(n > 0)
        def _():
            for i in range(hp):
                dlow_ref[i] += heads[i][3]

    def zspec(col):
        return pl.BlockSpec((rows, wide), lambda h, n: (blk(n), col // wide + h))

    hspec = pl.BlockSpec((rows, wide), lambda h, n: (blk(n), h))
    in_specs = [zspec(col_q), zspec(col_f), zspec(col_v), pl.BlockSpec((hp, 1, HG_DK), lambda h, n: (h, 0, 0)),
                pl.BlockSpec((hp, 1, HG_DK, HG_DK), lambda h, n: (h, blk(n), 0, 0)), hspec]
    ins = [z, z, z, lower3, states, do]
    if has_prev:
        in_specs += [hspec, hspec]
        ins += list(prev)
    full_wide = HG_HEADS * HG_DK
    acc_dtype = BF16 if has_prev else F32
    return pl.pallas_call(
        body, name="gla_bwd_rev" if rev else "gla_bwd", grid=(HG_HEADS // hp, nb),
        in_specs=in_specs,
        out_specs=[hspec, hspec, hspec, pl.BlockSpec((hp, 1, HG_DK), lambda h, n: (h, 0, 0))],
        out_shape=[_sds((T, full_wide), acc_dtype), _sds((T, full_wide), acc_dtype), _sds((T, full_wide), BF16),
                   _sds((HG_HEADS, 1, HG_DK), F32)],
        scratch_shapes=[pltpu.VMEM((hp, HG_DK, HG_DK), F32)],
        compiler_params=_params(("parallel", "arbitrary")),
    )(*ins)


def _lower_fn(lb):
    e = jnp.exp(lb - jnp.max(lb, axis=0, keepdims=True))
    return (e / jnp.sum(e, axis=0, keepdims=True))[0]


def _lower_bounds(lb):
    def body(lb_ref, o_ref):
        o_ref[...] = _lower_fn(lb_ref[...])
    return pl.pallas_call(body, name="lower_bounds", out_shape=_sds(lb.shape[1:], F32))(lb)


def _row_tile(r, cap=1024):
    best = None
    for t in range(16, min(r, cap) + 1, 16):
        if r % t == 0:
            best = t
    return best if best is not None else r


def _sum4(shards, recv, chip, *, name):
    _, R, C = shards.shape
    tr = _row_tile(R)

    def body(chip_ref, o_ref, r_ref, out_ref):
        out_ref[...] = ((o_ref[0].astype(F32) + r_ref[0].astype(F32)) + r_ref[1].astype(F32)) + r_ref[2].astype(F32)

    grid_spec = pltpu.PrefetchScalarGridSpec(
        num_scalar_prefetch=1, grid=(R // tr,),
        in_specs=[pl.BlockSpec((1, tr, C), lambda i, chip_ref: (chip_ref[0], i, 0)),
                  pl.BlockSpec((3, tr, C), lambda i, chip_ref: (0, i, 0))],
        out_specs=pl.BlockSpec((tr, C), lambda i, chip_ref: (i, 0)))
    return pl.pallas_call(
        body, name=name, grid_spec=grid_spec, out_shape=_sds((R, C), F32), compiler_params=_params(("parallel",)),
    )(chip, shards, recv)


def _adamw_math(w, g, m, v):
    m = ADAM_B1 * m + (1.0 - ADAM_B1) * g
    v = ADAM_B2 * v + (1.0 - ADAM_B2) * (g * g)
    m_hat = m / (1.0 - ADAM_B1 ** ADAM_STEP)
    v_hat = v / (1.0 - ADAM_B2 ** ADAM_STEP)
    delta = -ADAM_LR * (m_hat / (jnp.sqrt(v_hat) + ADAM_EPS) + ADAM_WD * w)
    return delta, m, v


def _adamw(w, g_a, g_b, m, v, *, name):
    R, C = w.shape
    tr = _row_tile(R)
    two = g_b is not None

    def body(*refs):
        w_ref, ga_ref = refs[0], refs[1]
        rest = refs[2:]
        g = ga_ref[...]
        if two:
            g = g + rest[0][...]
            rest = rest[1:]
        m_ref, v_ref, g_out, d_out, m_out, v_out = rest
        delta, m_new, v_new = _adamw_math(w_ref[...], g, m_ref[...], v_ref[...])
        g_out[...] = g
        d_out[...] = delta
        m_out[...] = m_new
        v_out[...] = v_new

    spec = pl.BlockSpec((tr, C), lambda i: (i, 0))
    ins = [w, g_a] + ([g_b] if two else []) + [m, v]
    return pl.pallas_call(
        body, name=name, grid=(R // tr,), in_specs=[spec] * len(ins), out_specs=[spec] * 4,
        out_shape=[_sds((R, C), F32)] * 4, compiler_params=_params(("parallel",)),
    )(*ins)


def _adamw_small(red, lb_full, ws, ms, vs):
    n = len(ws)

    def pieces(shape):
        out = []
        for j, idx in enumerate(itertools.product(*[range(d) for d in shape[:-1]])):
            out.append((idx[:-1] + (slice(idx[-1], idx[-1] + 1), slice(None)), j * shape[-1]))
        return out

    def body(*refs):
        red_ref, lb_ref = refs[0], refs[1]
        w_refs, m_refs, v_refs = refs[2:2 + n], refs[2 + n:2 + 2 * n], refs[2 + 2 * n:2 + 3 * n]
        out_refs = refs[2 + 3 * n:]
        chip = 2 * lax.axis_index("x") + lax.axis_index("y")
        n_f, shard = lb_ref.shape[-1], w_refs[n - 1].shape[-1]
        _, vjp = jax.vjp(_lower_fn, lb_ref[...])
        dlb = vjp(red_ref[8:10, 0:n_f])[0]
        for i in range(n):
            width = w_refs[i].shape[-1]
            for j, (at, lane) in enumerate(pieces(w_refs[i].shape)):
                if i < n - 1:
                    g = red_ref[i:i + 1, lane:lane + width]
                else:
                    row = dlb[j // 2][j % 2:j % 2 + 1]
                    g = sum(jnp.where(chip == q, row[:, q * shard:(q + 1) * shard], 0.0) for q in range(N_CHIPS))
                delta, m_new, v_new = _adamw_math(w_refs[i][at], g, m_refs[i][at], v_refs[i][at])
                for o_ref, val in zip(out_refs[4 * i:4 * i + 4], (g, delta, m_new, v_new)):
                    o_ref[at] = val

    return pl.pallas_call(
        body, name="adamw_small", out_shape=[_sds(w.shape, F32) for w in ws for _ in range(4)],
    )(red, lb_full, *ws, *ms, *vs)


def _chip_peers():
    x, y, c = lax.axis_index("x"), lax.axis_index("y"), lax.axis_index("c")
    return (x, y, c), 2 * x + y, [(1 - x, y), (x, 1 - y), (1 - x, 1 - y)]


_HBM = pl.BlockSpec(memory_space=pltpu.HBM)
_SEM = pl.BlockSpec(memory_space=pltpu.SEMAPHORE)
_EFFECT = pltpu.SideEffectType.DATAFLOW_SIDE_EFFECTING


def _exchange_copies(srcs, lands, sems, mode):
    (x, y, c), me, chips = _chip_peers()
    copies = []
    for t, (src, land) in enumerate(zip(srcs, lands)):
        if mode == "swap":
            copies.append(pltpu.make_async_remote_copy(src, land, sems[0].at[3 * t], sems[1].at[3 * t],
                                                       device_id=(x, y, 1 - c), device_id_type=MESH))
            continue
        for k, (px, py) in enumerate(chips):
            gather = mode == "gather"
            copies.append(pltpu.make_async_remote_copy(
                src if gather else src.at[2 * px + py], land.at[me] if gather else land.at[k],
                sems[0].at[3 * t + k], sems[1].at[3 * t + k], device_id=(px, py, c), device_id_type=MESH))
        if mode == "gather":
            copies.append(pltpu.make_async_copy(src, land.at[me], sems[2].at[t]))
    return copies


def _exchange_start(srcs, *, mode, name, after=None):
    n = len(srcs)
    n_sem = 3 if mode == "gather" else 2
    n_in = 2 * n + (after is not None)
    land_shape = {"gather": lambda s: (N_CHIPS,) + s.shape, "scatter": lambda s: (3,) + s.shape[1:], "swap": lambda s: s.shape}
    lands = [_sds(land_shape[mode](s), s.dtype) for s in srcs]

    def body(*refs):
        for cp in _exchange_copies(refs[:n], refs[n:2 * n], refs[n_in:n_in + n_sem], mode):
            cp.start()
        token = refs[-1]
        token[...] = jnp.zeros_like(token)

    sem_shapes = [pltpu.SemaphoreType.DMA((3 * n,)), pltpu.SemaphoreType.DMA((3 * n,))]
    sem_shapes += [pltpu.SemaphoreType.DMA((n,))] if mode == "gather" else []
    thru = [pltpu.HBM(s.shape, s.dtype) for s in srcs] + [pltpu.HBM(l.shape, l.dtype) for l in lands]
    res = pl.pallas_call(
        body, name=name, in_specs=[_HBM] * (2 * n) + [_ANY] * (after is not None),
        out_specs=[_SEM] * n_sem + [_HBM] * (2 * n) + [pl.BlockSpec(memory_space=pltpu.VMEM)],
        out_shape=sem_shapes + thru + [_sds((8, LANE), F32)], input_output_aliases={i: n_sem + i for i in range(2 * n)},
        compiler_params=pltpu.CompilerParams(has_side_effects=_EFFECT),
    )(*[pltpu.with_memory_space_constraint(s, pltpu.HBM) for s in srcs],
      *[pltpu.with_memory_space_constraint(lax.empty(l.shape, l.dtype), pltpu.HBM) for l in lands],
      *([after] if after is not None else []))
    return (res[:n_sem], res[n_sem:n_sem + n], res[n_sem + n:n_sem + 2 * n], mode), res[-1]


def _exchange_wait(started, after, *, name):
    sems, srcs, lands, mode = started
    n, n_sem = len(srcs), len(sems)
    after = list(after) if isinstance(after, (list, tuple)) else [after]

    def body(*refs):
        for cp in _exchange_copies(refs[:n], refs[n:2 * n], refs[2 * n:2 * n + n_sem], mode):
            cp.wait()

    res = pl.pallas_call(
        body, name=name, in_specs=[_HBM] * (2 * n) + [_SEM] * n_sem + [_ANY] * len(after), out_specs=[_HBM] * (2 * n),
        out_shape=[pltpu.HBM(a.shape, a.dtype) for a in list(srcs) + list(lands)],
        input_output_aliases={i: i for i in range(2 * n)},
        compiler_params=pltpu.CompilerParams(has_side_effects=_EFFECT),
    )(*srcs, *lands, *sems, *after)
    return res[:n], res[n:]


def _allreduce_small(pack):
    R, C = pack.shape

    def body(in_ref, out_ref, slots, send_sems, recv_sems):
        x, y, c = lax.axis_index("x"), lax.axis_index("y"), lax.axis_index("c")
        me = 4 * x + 2 * y + c
        slots[me] = in_ref[...]
        copies = []
        for k in range(1, N_DEV):
            peer = (x ^ ((k >> 2) & 1), y ^ ((k >> 1) & 1), c ^ (k & 1))
            cp = pltpu.make_async_remote_copy(in_ref, slots.at[me], send_sems.at[k - 1], recv_sems.at[k - 1],
                                              device_id=peer, device_id_type=MESH)
            cp.start()
            copies.append(cp)
        for cp in copies:
            cp.wait()
        acc = slots[0]
        for d in range(1, N_DEV):
            acc = acc + slots[d]
        out_ref[...] = acc

    return pl.pallas_call(
        body, name="allreduce_small", out_shape=_sds((R, C), F32),
        in_specs=[pl.BlockSpec(memory_space=pltpu.VMEM)], out_specs=pl.BlockSpec(memory_space=pltpu.VMEM),
        scratch_shapes=[pltpu.VMEM((N_DEV, R, C), F32), pltpu.SemaphoreType.DMA((N_DEV - 1,)),
                        pltpu.SemaphoreType.DMA((N_DEV - 1,))],
        compiler_params=_params(),
    )(pack)


_Z_CQ, _Z_CKV, _Z_HQ, _Z_HFF, _Z_HFB, _Z_HI, _Z_HG, _Z_KR, _Z_END = 0, 256, 512, 1024, 1536, 2048, 2560, 3072, 3200


def _to_z_layout(wt):
    pad = jnp.zeros((_Z_END - _Z_KR - QK_ROPE, wt.shape[1]), wt.dtype)
    return jnp.concatenate([wt[:512], wt[512 + QK_ROPE:], wt[512:512 + QK_ROPE], pad], axis=0)


def _from_z_layout(wt):
    return jnp.concatenate([wt[:512], wt[_Z_KR:_Z_KR + QK_ROPE], wt[512:_Z_KR]], axis=0)


def _col_shards_to_full(g):
    return jnp.transpose(g, (1, 0, 2)).reshape(g.shape[1], -1)


def _full_to_col_shards(w):
    r, c = w.shape
    return jnp.transpose(w.reshape(r, N_CHIPS, c // N_CHIPS), (1, 0, 2))


def _full_to_row_shards(w):
    r, c = w.shape
    return w.reshape(N_CHIPS, r // N_CHIPS, c)


def kernel(x, p, positions, g_mix, w_in, g_qa, g_kva, w_qb, w_kvb, g_qn, g_kn, lb_param, g_hgo, w_o, g_ffn, w_gate, w_up, w_down, g_ple, w_ple_gate, w_ple_proj, loss_target, m_g_mix, m_w_in, m_g_qa, m_g_kva, m_w_qb, m_w_kvb, m_g_qn, m_g_kn, m_lb_param, m_g_hgo, m_w_o, m_g_ffn, m_w_gate, m_w_up, m_w_down, m_g_ple, m_w_ple_gate, m_w_ple_proj, v_g_mix, v_w_in, v_g_qa, v_g_kva, v_w_qb, v_w_kvb, v_g_qn, v_g_kn, v_lb_param, v_g_hgo, v_w_o, v_g_ffn, v_w_gate, v_w_up, v_w_down, v_g_ple, v_w_ple_gate, v_w_ple_proj):
    w_named = dict(g_mix=g_mix, w_in=w_in, g_qa=g_qa, g_kva=g_kva, w_qb=w_qb, w_kvb=w_kvb, g_qn=g_qn, g_kn=g_kn,
                   lb_param=lb_param, g_hgo=g_hgo, w_o=w_o, g_ffn=g_ffn, w_gate=w_gate, w_up=w_up, w_down=w_down,
                   g_ple=g_ple, w_ple_gate=w_ple_gate, w_ple_proj=w_ple_proj)
    m_named = dict(g_mix=m_g_mix, w_in=m_w_in, g_qa=m_g_qa, g_kva=m_g_kva, w_qb=m_w_qb, w_kvb=m_w_kvb, g_qn=m_g_qn,
                   g_kn=m_g_kn, lb_param=m_lb_param, g_hgo=m_g_hgo, w_o=m_w_o, g_ffn=m_g_ffn, w_gate=m_w_gate,
                   w_up=m_w_up, w_down=m_w_down, g_ple=m_g_ple, w_ple_gate=m_w_ple_gate, w_ple_proj=m_w_ple_proj)
    v_named = dict(g_mix=v_g_mix, w_in=v_w_in, g_qa=v_g_qa, g_kva=v_g_kva, w_qb=v_w_qb, w_kvb=v_w_kvb, g_qn=v_g_qn,
                   g_kn=v_g_kn, lb_param=v_lb_param, g_hgo=v_g_hgo, w_o=v_w_o, g_ffn=v_g_ffn, w_gate=v_w_gate,
                   w_up=v_w_up, w_down=v_w_down, g_ple=v_g_ple, w_ple_gate=v_w_ple_gate, w_ple_proj=v_w_ple_proj)
    order = list(w_named)
    transposed = ("w_in", "w_qb", "w_gate", "w_up")
    col_sharded = ("w_kvb", "w_ple_proj")
    row_sharded = ("w_o", "w_down", "w_ple_gate")
    big = transposed + col_sharded + row_sharded

    def view(n, a):
        return jnp.transpose(a[0]) if n in transposed else a[0]

    def unview(n, a):
        return (jnp.transpose(a) if n in transposed else a)[None]

    def to_shards(n, g):
        return _full_to_col_shards(g) if n in col_sharded else _full_to_row_shards(g)

    x2d, p2d, tgt = x[0], p[0, 0], loss_target[0]
    T, D = x2d.shape

    lb_flat = lb_param.reshape(-1, lb_param.shape[-1])
    gather_groups = (("w_in",), ("w_qb", "w_kvb"), ("w_o", "w_gate", "w_up", "w_down", "w_ple_gate", "w_ple_proj"))
    gather_started = []

    casts = {n: view(n, w_named[n]).astype(BF16) for n in big}

    def gather_start(gi, after):
        srcs = [casts[n] for n in gather_groups[gi]] + ([lb_flat] if gi == 0 else [])
        started, token = _exchange_start(srcs, mode="gather", name=f"gather_start_{gi}", after=after)
        gather_started.append(started)
        return token

    full = {}

    def gather_wait(gi, after):
        _, got = _exchange_wait(gather_started[gi], after, name=f"gather_wait_{gi}")
        for n, g in zip(gather_groups[gi], got):
            full[n] = _col_shards_to_full(g) if n in col_sharded else g.reshape(-1, g.shape[-1])
        return got

    g_hgo_row = g_hgo.reshape(1, -1)

    inv_freq = ROPE_THETA ** (-jnp.arange(0, QK_ROPE, 2, dtype=F32) / QK_ROPE)
    ang = positions[0].astype(F32)[:, None] * inv_freq
    cos, sin = jnp.cos(ang), jnp.sin(ang)
    token = gather_start(0, None)
    h1 = _stage(_rms, [x2d], [g_mix], [_sds((T, D), BF16)], [], name="norm_mix", after=token)[0]
    got = gather_wait(0, [h1, cos, sin] + [casts[n] for g in gather_groups[1:] for n in g])
    token = got[0]
    for gi in range(1, len(gather_groups)):
        token = gather_start(gi, token)
    lb_full = _col_shards_to_full(got[-1]).reshape(lb_param.shape[0], lb_param.shape[1], -1)
    w_in_zt = _to_z_layout(full["w_in"])
    z = _mm(h1, w_in_zt, tb=True, name="in_proj", after=token)
    qn, kvn = _stage(_mla_a_fn, [_cols(z, 256, 0), _cols(z, 256, 1)], [g_qa, g_kva],
                     [_sds((T, 256), BF16), _sds((T, 256), BF16)], [], name="mla_latent_norm")
    gather_wait(1, qn)
    q_raw = _mm(qn, full["w_qb"], tb=True, name="q_up")
    kv_raw = _mm(kvn, full["w_kvb"], name="kv_up")
    kr = _cols(z, LANE, _Z_KR // LANE)
    q, k, v = _stage(_mla_b_fn, [q_raw, kv_raw, kr, cos, sin], [g_qn, g_kn],
                     [_sds((MLA_HEADS, T, QK_HEAD), BF16), _sds((MLA_HEADS, T, QK_HEAD), BF16),
                      _sds((MLA_HEADS, T, V_HEAD), BF16)], [], name="mla_qk_norm_rope")
    att, lse = _attention_fwd(q, k, v)

    lower = _lower_bounds(lb_full)
    lower3 = lower.reshape(2, HG_HEADS, 1, HG_DK)
    o_f, st_f = _gla_fwd(z, lower3[0], rev=False, col_q=_Z_HQ, col_f=_Z_HFF, col_v=_Z_HI, hp=GLA_FWD_HEADS)
    o_b, st_b = _gla_fwd(z, lower3[1], rev=True, col_q=_Z_HQ, col_f=_Z_HFB, col_v=_Z_HI, hp=GLA_FWD_HEADS)
    hg = _cols(z, 512, _Z_HG // 512)
    mix = _stage(_post_fn, [att, o_f, o_b, hg], [g_hgo_row], [_sds((T, att.shape[1] + o_f.shape[1]), BF16)], [],
                 name="mix_out")[0]
    gather_wait(2, mix)
    x1, h2 = _mm_fused(mix, full["w_o"], _resid_norm_fn, [x2d], [g_ffn], [F32, BF16], [], full_rows=True,
                       name="out_proj")
    gt, up, act = _mm_fused(h2, full["w_gate"], lambda gt, up: (gt, up, _swiglu_fn(gt, up)), [], [], [BF16, BF16, BF16],
                            [], tb=True, b2=full["w_up"], name="ffn_gate_up")
    x2, h3 = _mm_fused(act, full["w_down"], _resid_norm_fn, [x1], [g_ple], [F32, BF16], [], full_rows=True,
                       name="ffn_down")
    pp = _mm(p2d, full["w_ple_proj"], name="ple_proj")
    dx3, dpg, dpp, loss_part = _mm_fused(
        h3, full["w_ple_gate"], lambda acc, pp, x2, tgt: _ple_loss_fn(x2, acc, pp, tgt), [pp, x2, tgt], [],
        [F32, BF16, BF16], [_sds((1, 1), F32)], full_rows=True, name="ple_gate_loss")

    grads = {}
    scatter_groups = (("w_ple_proj", "w_ple_gate", "w_down", "w_gate", "w_up", "w_o"), ("w_qb", "w_kvb", "w_in"))
    scatter_started = []

    def scatter_start(gi):
        srcs = [to_shards(n, grads[n]) for n in scatter_groups[gi]]
        started, token = _exchange_start(srcs, mode="scatter", name=f"scatter_start_{gi}")
        scatter_started.append(started)
        return token

    chip = 2 * lax.axis_index("x") + lax.axis_index("y")
    swap_started = []

    def reduce_group(gi, after):
        shards, recvs = _exchange_wait(scatter_started[gi], after, name=f"scatter_wait_{gi}")
        sums = [_sum4(s, r, chip.reshape(1), name="sum_" + n) for n, s, r in zip(scatter_groups[gi], shards, recvs)]
        started, token = _exchange_start(sums, mode="swap", name=f"swap_start_{gi}")
        swap_started.append(started)
        return token

    grads["w_ple_proj"] = _mm(p2d, dpp, ta=True, out_dtype=BF16, name="d_w_ple_proj")
    grads["w_ple_gate"] = _mm(h3, dpg, ta=True, out_dtype=BF16, name="d_w_ple_gate")
    dx2, grads["g_ple"] = _mm_fused(
        dpg, full["w_ple_gate"], lambda acc, x2, dx3, g: _norm_bwd_fn(x2, acc, dx3, g), [x2, dx3], [g_ple],
        [F32], [_sds((1, D), F32)], tb=True, full_rows=True, name="d_h3_norm_ple_bwd")
    dgt, dup = _mm_fused(dx2, full["w_down"], lambda acc, gt, up: _swiglu_bwd_fn(gt.astype(F32), up.astype(F32), acc), [gt, up], [],
                         [BF16, BF16], [], tb=True, name="d_act_swiglu_bwd")
    grads["w_down"] = _mm(act, dx2, ta=True, out_dtype=BF16, name="d_w_down")
    grads["w_gate"] = _mm(dgt, h2, ta=True, out_dtype=BF16, name="d_w_gate")
    grads["w_up"] = _mm(dup, h2, ta=True, out_dtype=BF16, name="d_w_up")
    dx1, grads["g_ffn"] = _mm_fused(
        dgt, full["w_gate"], lambda acc, x1, dx2, g: _norm_bwd_fn(x1, acc, dx2, g), [x1, dx2], [g_ffn],
        [F32], [_sds((1, D), F32)], full_rows=True, a2=dup, b2=full["w_up"], name="d_h2_norm_ffn_bwd")
    grads["w_o"] = _mm(mix, dx1, ta=True, out_dtype=BF16, name="d_w_o")
    token = scatter_start(0)
    dmix = _mm(dx1, full["w_o"], tb=True, name="d_mix", after=token)

    half = MLA_HEADS * V_HEAD
    do, dhg, dg_hgo = _stage(_post_bwd_fn, [o_f, o_b, hg, _cols(dmix, half, 1)], [g_hgo_row],
                             [_sds((T, half), F32), _sds((T, half), BF16)], [_sds((1, half), F32)], name="mix_out_bwd")
    grads["g_hgo"] = dg_hgo
    dhq_f, dhi_f, dhf_f, dlow_f = _gla_bwd(z, lower3[0], st_f, do, None, rev=False,
                                           col_q=_Z_HQ, col_f=_Z_HFF, col_v=_Z_HI, hp=GLA_BWD_HEADS)
    dhq, dhi, dhf_b, dlow_b = _gla_bwd(z, lower3[1], st_b, do, (dhq_f, dhi_f), rev=True,
                                       col_q=_Z_HQ, col_f=_Z_HFB, col_v=_Z_HI, hp=GLA_BWD_HEADS)

    dq, dk, dv = _attention_bwd(q, k, v, att, lse, dmix)
    dq_raw, dkv_raw, dkr, grads["g_qn"], grads["g_kn"] = _stage(
        _mla_b_bwd_fn, [q_raw, kv_raw, kr, cos, sin, dq, dk, dv], [g_qn, g_kn],
        [_sds(q_raw.shape, BF16), _sds(kv_raw.shape, BF16), _sds((T, LANE), BF16)],
        [_sds(g_qn.shape, F32), _sds(g_kn.shape, F32)], name="mla_qk_norm_rope_bwd")
    grads["w_qb"] = _mm(dq_raw, qn, ta=True, out_dtype=BF16, name="d_w_qb")
    grads["w_kvb"] = _mm(kvn, dkv_raw, ta=True, out_dtype=BF16, name="d_w_kvb")
    dqn = _mm(dq_raw, full["w_qb"], name="d_qn")
    dkvn = _mm(dkv_raw, full["w_kvb"], tb=True, name="d_kvn")
    dcq, dckv, grads["g_qa"], grads["g_kva"] = _stage(
        _mla_a_bwd_fn, [_cols(z, 256, 0), _cols(z, 256, 1), dqn, dkvn], [g_qa, g_kva],
        [_sds((T, 256), BF16), _sds((T, 256), BF16)], [_sds(g_qa.shape, F32), _sds(g_kva.shape, F32)],
        name="mla_latent_norm_bwd")
    token = reduce_group(0, dcq)
    dz = jnp.concatenate([dcq, dckv, dhq, dhf_f, dhf_b, dhi, dhg, dkr], axis=1)
    grads["w_in"] = _from_z_layout(_mm(dz, h1, ta=True, out_dtype=BF16, name="d_w_in", after=token))
    token = scatter_start(1)
    grad_x, grads["g_mix"] = _mm_fused(
        dz, w_in_zt, lambda acc, x, dx1, g: _norm_bwd_fn(x, acc, dx1, g), [x2d, dx1], [g_mix],
        [F32], [_sds((1, D), F32)], full_rows=True, name="d_h1_norm_mix_bwd", after=token)

    out_g, out_d, out_m, out_v = {}, {}, {}, {}

    def update_group(gi, after):
        mine, theirs = _exchange_wait(swap_started[gi], after, name=f"swap_wait_{gi}")
        for n, a, b in zip(scatter_groups[gi], mine, theirs):
            out_g[n], out_d[n], out_m[n], out_v[n] = (
                unview(n, t) for t in _adamw(view(n, w_named[n]), a, b, view(n, m_named[n]), view(n, v_named[n]),
                                             name="adamw_" + n))
        return out_v[scatter_groups[gi][-1]]

    done = update_group(0, grad_x)
    token = reduce_group(1, done)
    update_group(1, token)

    small = ("g_mix", "g_qa", "g_kva", "g_qn", "g_kn", "g_hgo", "g_ffn", "g_ple")
    small_all = small + ("lb_param",)
    width = -(-max(w_named[n].size for n in small_all) // LANE) * LANE

    def row(a):
        a = a.reshape(1, -1)
        return jnp.pad(a, ((0, 0), (0, width - a.shape[1])))

    dlower = jnp.concatenate([dlow_f.reshape(1, -1), dlow_b.reshape(1, -1)], axis=0)
    pack = jnp.concatenate([row(grads[n]) for n in small] + [row(dlower[0]), row(dlower[1]), row(loss_part)]
                           + [jnp.zeros((5, width), F32)], axis=0)
    red = _allreduce_small(pack)
    loss = red[10, 0]

    outs = _adamw_small(red, lb_full, [w_named[n] for n in small_all], [m_named[n] for n in small_all],
                        [v_named[n] for n in small_all])
    for i, n in enumerate(small_all):
        out_g[n], out_d[n], out_m[n], out_v[n] = outs[4 * i:4 * i + 4]

    return (loss, grad_x[None], *[out_g[n] for n in order], *[out_d[n] for n in order],
            *[out_m[n] for n in order], *[out_v[n] for n in order])
```

```python
import functools
import itertools

import jax
import jax.numpy as jnp
from jax import lax
from jax.experimental import pallas as pl
from jax.experimental.pallas import tpu as pltpu

F32 = jnp.float32
BF16 = jnp.bfloat16
MESH = pl.DeviceIdType.MESH

EPS = 1e-6
ROPE_THETA = 10000.0
MLA_HEADS = 4
QK_NOPE = 128
QK_ROPE = 64
QK_HEAD = QK_NOPE + QK_ROPE
V_HEAD = 128
HG_HEADS = 4
HG_DK = 128
CHUNK = 64
ADAM_LR = 0.001
ADAM_B1 = 0.9
ADAM_B2 = 0.999
ADAM_EPS = 1e-08
ADAM_WD = 0.01
ADAM_STEP = 10

LANE = 128
VMEM_LIMIT = 56 * 1024 * 1024
TOK_TILE = 256
GLA_GROUP = 16
GLA_FWD_HEADS = 4
GLA_BWD_HEADS = 2
ATT_TQ = 1024
ATT_TK = 1024
ATT_CHUNK = 512
LOG2_E = 1.4426950408889634
N_CHIPS = 4
N_DEV = 8


_ANY = pl.BlockSpec(memory_space=pl.ANY)


def _params(dims=None, **kw):
    return pltpu.CompilerParams(dimension_semantics=dims, vmem_limit_bytes=VMEM_LIMIT, **kw)


def _tile_candidates(n, cap):
    out = [d for d in range(LANE, min(n, cap) + 1, LANE) if n % d == 0]
    if n <= cap and n not in out:
        out.append(n)
    return out or [n]


MM_VMEM_BUDGET = 40 * 1024 * 1024
MM_MIN_ROWS = 256
MM_MAX_ROWS = 1536
HBM_BYTES_PER_S = 2.8e12
MXU_FLOPS_PER_S = 8e14
STEP_OVERHEAD_S = 0.35e-6


def _mm_tiles(M, N, K, a_bytes, b_bytes, o_bytes, has_add, full_rows=False, full_k=False, n_a=1, n_b=1):
    cast_a, cast_b = a_bytes > 2, b_bytes > 2
    a_bytes, b_bytes = n_a * a_bytes, n_b * b_bytes
    best = None
    for tm in [t for t in _tile_candidates(M, MM_MAX_ROWS) if t >= min(M, MM_MIN_ROWS)]:
        for tn in ([N] if full_rows else _tile_candidates(N, N)):
            for tk in ([K] if full_k else _tile_candidates(K, K)):
                ni, nj, nk = M // tm, N // tn, K // tk
                vmem = 2 * (tm * tk * a_bytes + tk * tn * b_bytes + tm * tn * o_bytes * (2 if has_add else 1))
                vmem += tm * tn * 4 * (2 if nk > 1 else 1)
                vmem += (tm * tk * 2 * n_a if cast_a else 0) + (tk * tn * 2 * n_b if cast_b else 0)
                if vmem > MM_VMEM_BUDGET:
                    continue
                moved = M * K * a_bytes * (nj if nk > 1 else 1) + K * N * b_bytes * (1 if nj == nk == 1 else ni)
                moved += M * N * o_bytes * (2 if has_add else 1)
                t = max(moved / HBM_BYTES_PER_S, 2 * M * N * K / MXU_FLOPS_PER_S) + ni * nj * nk * STEP_OVERHEAD_S
                if best is None or t < best[0]:
                    best = (t, tm, tn, tk)
    assert best is not None, (M, N, K)
    return best[1:]


def _dot_raw(a, b, kind):
    nb = a.ndim - 2
    batch = ((0,), (0,)) if nb else ((), ())
    ca = nb if kind == "tn" else nb + 1
    cb = nb + 1 if kind == "nt" else nb
    return lax.dot_general(a.astype(BF16), b.astype(BF16), (((ca,), (cb,)), batch), preferred_element_type=F32)


@functools.partial(jax.custom_vjp, nondiff_argnums=(2,))
def _bdot(a, b, kind):
    return _dot_raw(a, b, kind)


def _bdot_fwd(a, b, kind):
    return _dot_raw(a, b, kind), (a, b)


def _bdot_bwd(kind, res, g):
    a, b = res
    if kind == "nn":
        da, db = _bdot(g, b, "nt"), _bdot(a, g, "tn")
    elif kind == "nt":
        da, db = _bdot(g, b, "nn"), _bdot(g, a, "tn")
    else:
        da, db = _bdot(b, g, "nt"), _bdot(a, g, "nn")
    return da.astype(a.dtype), db.astype(b.dtype)


_bdot.defvjp(_bdot_fwd, _bdot_bwd)


def _mm(a, b, *, name, ta=False, tb=False, add=None, out_dtype=F32, after=None):
    K, M = a.shape if ta else a.shape[::-1]
    N, Kb = b.shape if tb else b.shape[::-1]
    assert K == Kb, (a.shape, b.shape, ta, tb)
    tm, tn, tk = _mm_tiles(M, N, K, a.dtype.itemsize, b.dtype.itemsize, jnp.dtype(out_dtype).itemsize, add is not None)
    nk = K // tk
    kind = "tn" if ta else ("nt" if tb else "nn")
    assert not (ta and tb)
    a_spec = pl.BlockSpec((tk, tm), lambda i, j, k: (k, i)) if ta else pl.BlockSpec((tm, tk), lambda i, j, k: (i, k))
    b_spec = pl.BlockSpec((tn, tk), lambda i, j, k: (j, k)) if tb else pl.BlockSpec((tk, tn), lambda i, j, k: (k, j))
    o_spec = pl.BlockSpec((tm, tn), lambda i, j, k: (i, j))
    has_add = add is not None

    def body(*refs):
        a_ref, b_ref = refs[0], refs[1]
        add_ref = refs[2] if has_add else None
        o_ref = refs[n_in]
        part = _dot_raw(a_ref[...], b_ref[...], kind)
        if nk == 1:
            if has_add:
                part = part + add_ref[...].astype(F32)
            o_ref[...] = part.astype(o_ref.dtype)
            return
        acc_ref = refs[-1]
        k = pl.program_id(2)

        @pl.when(k == 0)
        def _():
            acc_ref[...] = part

        @pl.when(k > 0)
        def _():
            acc_ref[...] += part

        @pl.when(k == nk - 1)
        def _():
            r = acc_ref[...]
            if has_add:
                r = r + add_ref[...].astype(F32)
            o_ref[...] = r.astype(o_ref.dtype)

    ins = [a, b] + ([add] if has_add else []) + ([after] if after is not None else [])
    in_specs = [a_spec, b_spec] + ([o_spec] if has_add else []) + ([_ANY] if after is not None else [])
    n_in = len(ins)
    return pl.pallas_call(
        body, name=name, grid=(M // tm, N // tn, nk), in_specs=in_specs, out_specs=o_spec,
        out_shape=jax.ShapeDtypeStruct((M, N), out_dtype),
        scratch_shapes=[pltpu.VMEM((tm, tn), F32)] if nk > 1 else [],
        compiler_params=_params(("parallel", "parallel", "arbitrary")),
    )(*ins)


def _mm_fused(a, b, fn, tiles, params, out_dtypes, sums, *, name, ta=False, tb=False, full_rows=False, after=None,
              b2=None, a2=None):
    K, M = a.shape if ta else a.shape[::-1]
    N, Kb = b.shape if tb else b.shape[::-1]
    assert K == Kb and not (ta and tb), (a.shape, b.shape, ta, tb)
    per_elem = sum(t.dtype.itemsize for t in tiles) + sum(jnp.dtype(d).itemsize for d in out_dtypes)
    n_b = 1 if b2 is None else 2
    n_a = 1 if a2 is None else 2
    tm, tn, tk = _mm_tiles(M, N, K, a.dtype.itemsize, b.dtype.itemsize, per_elem, False, full_rows, b2 is not None,
                           n_a, n_b)
    nk = K // tk
    kind = "tn" if ta else ("nt" if tb else "nn")
    a_spec = pl.BlockSpec((tk, tm), lambda i, j, k: (k, i)) if ta else pl.BlockSpec((tm, tk), lambda i, j, k: (i, k))
    b_spec = pl.BlockSpec((tn, tk), lambda i, j, k: (j, k)) if tb else pl.BlockSpec((tk, tn), lambda i, j, k: (k, j))
    o_spec = pl.BlockSpec((tm, tn), lambda i, j, k: (i, j))
    ins = [a, b] + ([b2] if b2 is not None else []) + ([a2] if a2 is not None else [])
    ins += list(tiles) + list(params) + ([after] if after is not None else [])
    in_specs = [a_spec] + [b_spec] * n_b + [a_spec] * (n_a - 1) + [o_spec] * len(tiles)
    in_specs += [pl.BlockSpec(p.shape, lambda i, j, k, nd=p.ndim: (0,) * nd) for p in params]
    in_specs += [_ANY] if after is not None else []
    n_in, n_t, n_p, n_o = len(ins), len(tiles), len(params), len(out_dtypes)

    def body(*refs):
        outs, sum_refs = refs[n_in:n_in + n_o], refs[n_in + n_o:n_in + n_o + len(sums)]

        def finish(*products):
            res = fn(*products, *[t[...] for t in refs[n_a + n_b:n_a + n_b + n_t + n_p]])
            for o_ref, v in zip(outs, res[:n_o]):
                o_ref[...] = v.astype(o_ref.dtype)
            first = jnp.logical_and(pl.program_id(0) == 0, pl.program_id(1) == 0)
            for s_ref, v in zip(sum_refs, res[n_o:]):
                @pl.when(first)
                def _(s_ref=s_ref, v=v):
                    s_ref[...] = v

                @pl.when(jnp.logical_not(first))
                def _(s_ref=s_ref, v=v):
                    s_ref[...] += v

        part = _dot_raw(refs[0][...], refs[1][...], kind)
        if nk == 1 and a2 is not None:
            finish(part + _dot_raw(refs[3][...], refs[2][...], kind))
            return
        if nk == 1:
            finish(part, *([_dot_raw(refs[0][...], refs[2][...], kind)] if b2 is not None else []))
            return
        acc_ref = refs[-1]
        k = pl.program_id(2)

        @pl.when(k == 0)
        def _():
            acc_ref[...] = part

        @pl.when(k > 0)
        def _():
            acc_ref[...] += part

        @pl.when(k == nk - 1)
        def _():
            finish(acc_ref[...])

    out_shape = [_sds((M, N), d) for d in out_dtypes] + list(sums)
    out_specs = [o_spec] * n_o + [pl.BlockSpec(s.shape, lambda i, j, k, nd=len(s.shape): (0,) * nd) for s in sums]
    order = ("arbitrary",) * 3 if sums else ("parallel", "parallel", "arbitrary")
    return pl.pallas_call(
        body, name=name, grid=(M // tm, N // tn, nk), in_specs=in_specs, out_specs=out_specs, out_shape=out_shape,
        scratch_shapes=[pltpu.VMEM((tm, tn), F32)] if nk > 1 else [], compiler_params=_params(order),
    )(*ins)


def _cols(arr, width, block):
    return (arr, width, block)


def _stage(fn, tiles, params, out_tiles, out_sums, *, name, tile=TOK_TILE, after=None):
    def tok_spec(shape, width=None, block=0):
        if len(shape) == 2:
            w = shape[1] if width is None else width
            return pl.BlockSpec((tile, w), lambda i: (i, block))
        return pl.BlockSpec((shape[0], tile, shape[2]), lambda i: (0, i, 0))

    arrays, in_specs = [], []
    for t in tiles:
        if isinstance(t, tuple):
            arr, width, block = t
            arrays.append(arr)
            in_specs.append(tok_spec(arr.shape, width, block))
        else:
            arrays.append(t)
            in_specs.append(tok_spec(t.shape))
    n_tok = arrays[0].shape[0] if arrays[0].ndim == 2 else arrays[0].shape[1]
    for p in params:
        arrays.append(p)
        in_specs.append(pl.BlockSpec(p.shape, lambda i, nd=p.ndim: (0,) * nd))
    out_shape = list(out_tiles) + list(out_sums)
    out_specs = [tok_spec(o.shape) for o in out_tiles]
    out_specs += [pl.BlockSpec(o.shape, lambda i, nd=len(o.shape): (0,) * nd) for o in out_sums]
    n_fn, n_ot = len(arrays), len(out_tiles)
    if after is not None:
        arrays.append(after)
        in_specs.append(_ANY)
    n_in = len(arrays)

    def body(*refs):
        res = fn(*[r[...] for r in refs[:n_fn]])
        if not isinstance(res, (tuple, list)):
            res = (res,)
        outs = refs[n_in:]
        for o_ref, r in zip(outs[:n_ot], res[:n_ot]):
            o_ref[...] = r.astype(o_ref.dtype)
        i = pl.program_id(0)
        for o_ref, r in zip(outs[n_ot:], res[n_ot:]):
            @pl.when(i == 0)
            def _(o_ref=o_ref, r=r):
                o_ref[...] = r.astype(o_ref.dtype)

            @pl.when(i > 0)
            def _(o_ref=o_ref, r=r):
                o_ref[...] += r.astype(o_ref.dtype)

    res = pl.pallas_call(
        body, name=name, grid=(n_tok // tile,), in_specs=in_specs, out_specs=out_specs, out_shape=out_shape,
        compiler_params=_params(("arbitrary",)),
    )(*arrays)
    return res


def _sds(shape, dtype):
    return jax.ShapeDtypeStruct(tuple(shape), dtype)


def _sigmoid(x):
    return 0.5 * jnp.tanh(0.5 * x) + 0.5


def _rms(x, g):
    return x * lax.rsqrt(jnp.mean(x * x, axis=-1, keepdims=True) + EPS) * g


def _norm_bwd_fn(x, dh, dres, g):
    _, vjp = jax.vjp(_rms, x, g)
    dx, dg = vjp(dh)
    return dx + dres, dg


def _mla_a_fn(cq, ckv, g_qa, g_kva):
    return _rms(cq, g_qa), _rms(ckv, g_kva)


def _mla_a_bwd_fn(cq, ckv, dqn, dkvn, g_qa, g_kva):
    _, vjp = jax.vjp(_mla_a_fn, cq, ckv, g_qa, g_kva)
    return vjp((dqn, dkvn))


def _rope(t, cos, sin):
    t1, t2 = t[:, :QK_ROPE // 2], t[:, QK_ROPE // 2:]
    return jnp.concatenate([t1 * cos - t2 * sin, t1 * sin + t2 * cos], axis=-1)


def _mla_b_fn(q_raw, kv_raw, kr, cos, sin, g_qn, g_kn):
    krope = kr[:, :QK_ROPE]
    qs, ks, vs = [], [], []
    for h in range(MLA_HEADS):
        qh = _rms(q_raw[:, h * QK_HEAD:(h + 1) * QK_HEAD], g_qn)
        kvh = kv_raw[:, h * (QK_NOPE + V_HEAD):(h + 1) * (QK_NOPE + V_HEAD)]
        kh = _rms(jnp.concatenate([kvh[:, :QK_NOPE], krope], axis=-1), g_kn)
        qs.append(jnp.concatenate([qh[:, :QK_NOPE], _rope(qh[:, QK_NOPE:], cos, sin)], axis=-1))
        ks.append(jnp.concatenate([kh[:, :QK_NOPE], _rope(kh[:, QK_NOPE:], cos, sin)], axis=-1))
        vs.append(kvh[:, QK_NOPE:])
    return jnp.stack(qs), jnp.stack(ks), jnp.stack(vs)


def _mla_b_bwd_fn(q_raw, kv_raw, kr, cos, sin, dq, dk, dv, g_qn, g_kn):
    _, vjp = jax.vjp(lambda a, b, c, d, e: _mla_b_fn(a, b, c, cos, sin, d, e), q_raw, kv_raw, kr, g_qn, g_kn)
    return vjp((dq, dk, dv))


def _post_fn(a, o_f, o_b, hg, g_hgo):
    o = o_f + o_b
    parts = [a]
    for h in range(HG_HEADS):
        s = slice(h * HG_DK, (h + 1) * HG_DK)
        gate = hg[:, s]
        parts.append(_rms(o[:, s], g_hgo[:, s]) * (gate * _sigmoid(gate)))
    return jnp.concatenate(parts, axis=-1)


def _post_bwd_fn(o_f, o_b, hg, dr, g_hgo):
    def f(o, hg, g):
        return _post_fn(jnp.zeros_like(o), o, jnp.zeros_like(o), hg, g)[:, o.shape[1]:]
    _, vjp = jax.vjp(f, o_f + o_b, hg, g_hgo)
    return vjp(dr)


def _swiglu_fn(gt, up):
    return gt * _sigmoid(gt) * up


def _resid_norm_fn(acc, x, g):
    x_new = acc + x
    return x_new, _rms(x_new, g)


def _swiglu_bwd_fn(gt, up, dact):
    _, vjp = jax.vjp(_swiglu_fn, gt, up)
    return vjp(dact)


def _ple_loss_fn(x2, pg, pp, target):
    gate = _sigmoid(pg)
    err = x2 + gate * pp - target
    dx3 = err * (1.0 / err.shape[-1])
    loss = 0.5 * jnp.sum(jnp.mean(err * err, axis=-1, keepdims=True), axis=0, keepdims=True)
    return dx3, dx3 * pp * gate * (1.0 - gate), dx3 * gate, loss


def _attention_fwd(q, k, v):
    H, T, D = q.shape
    DV = v.shape[-1]
    tq, ck = min(ATT_TQ, T), min(ATT_CHUNK, T)
    c2 = (D ** -0.5) * LOG2_E

    def body(q_ref, k_ref, v_ref, o_ref, lse_ref):
        q_i = q_ref[0]

        def chunk(c, carry):
            m, l, acc = carry
            rows = pl.ds(pl.multiple_of(c * ck, ck), ck)
            s = _dot_raw(q_i, k_ref[0, rows, :], "nt")
            m_new = jnp.maximum(m, jnp.max(s, axis=-1, keepdims=True))
            p = jnp.exp2((s - m_new) * c2)
            alpha = jnp.exp2((m - m_new) * c2)
            l = l * alpha + jnp.sum(p, axis=-1, keepdims=True)
            acc = acc * alpha + _dot_raw(p, v_ref[0, rows, :], "nn")
            return m_new, l, acc

        init = (jnp.full((tq, 1), -jnp.inf, F32), jnp.zeros((tq, 1), F32), jnp.zeros((tq, DV), F32))
        m, l, acc = lax.fori_loop(0, T // ck, chunk, init, unroll=True)
        o_ref[...] = acc / l
        lse_ref[0] = m * c2 + jnp.log2(l)

    return pl.pallas_call(
        body, name="attention_fwd", grid=(H, T // tq),
        in_specs=[pl.BlockSpec((1, tq, D), lambda h, i: (h, i, 0)),
                  pl.BlockSpec((1, T, D), lambda h, i: (h, 0, 0)),
                  pl.BlockSpec((1, T, DV), lambda h, i: (h, 0, 0))],
        out_specs=[pl.BlockSpec((tq, DV), lambda h, i: (i, h)),
                   pl.BlockSpec((1, tq, 1), lambda h, i: (h, i, 0))],
        out_shape=[_sds((T, H * DV), F32), _sds((H, T, 1), F32)],
        compiler_params=_params(("parallel", "parallel")),
    )(q, k, v)


def _attention_bwd(q, k, v, o, lse2, dmix):
    H, T, D = q.shape
    DV = v.shape[-1]
    tk, cq = min(ATT_TK, T), min(ATT_CHUNK, T)
    scale = D ** -0.5
    c2 = scale * LOG2_E

    def body(q_ref, k_ref, v_ref, o_ref, lse_ref, do_ref, dq_ref, dk_ref, dv_ref, delta_ref):
        j = pl.program_id(1)

        @pl.when(j == 0)
        def _():
            delta = lax.dot_general(jnp.ones((8, DV), F32), do_ref[...] * o_ref[...], (((1,), (1,)), ((), ())),
                                    precision=lax.Precision.HIGHEST, preferred_element_type=F32)
            for i in range(T // cq):
                delta_ref[i] = delta[:, i * cq:(i + 1) * cq]
            dq_ref[0] = jnp.zeros((T, D), F32)

        k_j, v_j = k_ref[0], v_ref[0]
        dk_ref[0] = jnp.zeros((tk, D), F32)
        dv_ref[0] = jnp.zeros((tk, DV), F32)

        def chunk(c, carry):
            rows = pl.ds(pl.multiple_of(c * cq, cq), cq)
            q_c = q_ref[0, rows, :]
            do_c = do_ref[rows, :].astype(BF16)
            st = _dot_raw(k_j, q_c, "nt")
            pt = jnp.exp2(st * c2 - lse_ref[0, c])
            dv_ref[0] += _dot_raw(pt, do_c, "nn")
            dpt = _dot_raw(v_j, do_c, "nt")
            dst = pt * (dpt - delta_ref[c, 0:1, :]) * scale
            dk_ref[0] += _dot_raw(dst, q_c, "nn")
            dq_ref[0, rows, :] += _dot_raw(dst, k_j, "tn")
            return carry

        lax.fori_loop(0, T // cq, chunk, 0, unroll=True)

    return pl.pallas_call(
        body, name="attention_bwd", grid=(H, T // tk),
        in_specs=[pl.BlockSpec((1, T, D), lambda h, j: (h, 0, 0)),
                  pl.BlockSpec((1, tk, D), lambda h, j: (h, j, 0)),
                  pl.BlockSpec((1, tk, DV), lambda h, j: (h, j, 0)),
                  pl.BlockSpec((T, DV), lambda h, j: (0, h)),
                  pl.BlockSpec((1, T // cq, 1, cq), lambda h, j: (h, 0, 0, 0)),
                  pl.BlockSpec((T, DV), lambda h, j: (0, h))],
        out_specs=[pl.BlockSpec((1, T, D), lambda h, j: (h, 0, 0)),
                   pl.BlockSpec((1, tk, D), lambda h, j: (h, j, 0)),
                   pl.BlockSpec((1, tk, DV), lambda h, j: (h, j, 0))],
        out_shape=[_sds((H, T, D), F32), _sds((H, T, D), F32), _sds((H, T, DV), F32)],
        scratch_shapes=[pltpu.VMEM((T // cq, 8, cq), F32)],
        compiler_params=_params(("parallel", "arbitrary")),
    )(q, k, v, o, lse2.reshape(H, T // cq, 1, cq), dmix)


def _split3_dot(ones, x, kind):
    hi = x.astype(BF16)
    rest = x - hi.astype(F32)
    mid = rest.astype(BF16)
    lo = (rest - mid.astype(F32)).astype(BF16)
    return (_dot_raw(ones, hi, kind) + _dot_raw(ones, mid, kind)) + _dot_raw(ones, lo, kind)


@jax.custom_vjp
def _running_sum(x, tri):
    return _split3_dot(tri, x, "nn")


def _running_sum_fwd(x, tri):
    return _split3_dot(tri, x, "nn"), tri


def _running_sum_bwd(tri, g):
    return _split3_dot(tri, g, "tn"), jnp.zeros_like(tri)


_running_sum.defvjp(_running_sum_fwd, _running_sum_bwd)


def _gla_block(hq, hf, hi, lower, st_in, *, rev, dot):
    rows, dk = hq.shape
    G, C = rows // CHUNK, CHUNK
    q = hq * _sigmoid(hq)
    f = lower + (1.0 - lower) * _sigmoid(hf)
    k = 1.0 - f
    logf = jnp.log2(f)
    q3, k3, v3, lf3 = (t.reshape(G, C, dk) for t in (q, k, hi, logf))
    r = lax.broadcasted_iota(jnp.int32, (C, C), 0)
    c = lax.broadcasted_iota(jnp.int32, (C, C), 1)
    tri = ((r <= c) if rev else (r >= c)).astype(F32)
    b = _running_sum(lf3, jnp.broadcast_to(tri, (G, C, C)))
    tpos = lax.broadcasted_iota(jnp.int32, (1, C, 1), 1)
    first_half = (tpos >= C // 2) if rev else (tpos <= C // 2 - 1)
    b_mid = jnp.sum(jnp.where(first_half, lf3, 0.0), axis=1, keepdims=True)
    b_last = jnp.sum(lf3, axis=1, keepdims=True)
    a = dot(q3 * jnp.exp2(b - b_mid), k3 * jnp.exp2(b_mid - b), "nt") * tri
    o_intra = dot(a, v3, "nn")
    kv_t = dot(v3, k3 * jnp.exp2(b_last - b), "tn")
    decay = jnp.exp2(b_last)
    qd = q3 * jnp.exp2(b)
    st = st_in
    o_inter = [None] * G
    for g in (reversed(range(G)) if rev else range(G)):
        o_inter[g] = dot(qd[g], st, "nt")
        st = st * decay[g] + kv_t[g]
    o = o_intra.reshape(rows, dk) + jnp.concatenate(o_inter, axis=0)
    return o, st


def _gla_fwd(z, lower3, *, rev, col_q, col_f, col_v, hp):
    T = z.shape[0]
    rows = min(GLA_GROUP * CHUNK, T)
    nb = T // rows
    wide = hp * HG_DK
    blk = (lambda n: nb - 1 - n) if rev else (lambda n: n)

    def body(hq_ref, hf_ref, hi_ref, low_ref, o_ref, st_out_ref, st_ref):
        @pl.when(pl.program_id(1) == 0)
        def _():
            st_ref[...] = jnp.zeros_like(st_ref)

        st_in = [st_ref[i] for i in range(hp)]
        heads = []
        for i in range(hp):
            cols = slice(i * HG_DK, (i + 1) * HG_DK)
            heads.append(_gla_block(hq_ref[:, cols], hf_ref[:, cols], hi_ref[:, cols], low_ref[i], st_in[i], rev=rev,
                                    dot=_dot_raw))
        for i, (o, st) in enumerate(heads):
            st_out_ref[i, 0] = st_in[i]
            o_ref[:, i * HG_DK:(i + 1) * HG_DK] = o
            st_ref[i] = st

    def zspec(col):
        return pl.BlockSpec((rows, wide), lambda h, n: (blk(n), col // wide + h))

    return pl.pallas_call(
        body, name="gla_fwd_rev" if rev else "gla_fwd", grid=(HG_HEADS // hp, nb),
        in_specs=[zspec(col_q), zspec(col_f), zspec(col_v), pl.BlockSpec((hp, 1, HG_DK), lambda h, n: (h, 0, 0))],
        out_specs=[pl.BlockSpec((rows, wide), lambda h, n: (blk(n), h)),
                   pl.BlockSpec((hp, 1, HG_DK, HG_DK), lambda h, n: (h, blk(n), 0, 0))],
        out_shape=[_sds((T, HG_HEADS * HG_DK), F32), _sds((HG_HEADS, nb, HG_DK, HG_DK), F32)],
        scratch_shapes=[pltpu.VMEM((hp, HG_DK, HG_DK), F32)],
        compiler_params=_params(("parallel", "arbitrary")),
    )(z, z, z, lower3)


def _gla_bwd(z, lower3, states, do, prev, *, rev, col_q, col_f, col_v, hp):
    T = z.shape[0]
    rows = min(GLA_GROUP * CHUNK, T)
    nb = T // rows
    wide = hp * HG_DK
    blk = (lambda n: n) if rev else (lambda n: nb - 1 - n)
    has_prev = prev is not None
    fn = functools.partial(_gla_block, rev=rev, dot=_bdot)

    def body(*refs):
        hq_ref, hf_ref, hi_ref, low_ref, st_ref, do_ref = refs[:6]
        rest = refs[6:]
        if has_prev:
            pq_ref, pi_ref = rest[:2]
            rest = rest[2:]
        dhq_ref, dhi_ref, dhf_ref, dlow_ref, dst_ref = rest
        n = pl.program_id(1)

        @pl.when(n == 0)
        def _():
            dst_ref[...] = jnp.zeros_like(dst_ref)

        dst_in = [dst_ref[i] for i in range(hp)]
        heads = []
        for i in range(hp):
            cols = slice(i * HG_DK, (i + 1) * HG_DK)
            _, vjp = jax.vjp(fn, hq_ref[:, cols], hf_ref[:, cols], hi_ref[:, cols], low_ref[i], st_ref[i, 0])
            dhq, dhf, dhi, dlow, dst = vjp((do_ref[:, cols], dst_in[i]))
            if has_prev:
                dhq = dhq + pq_ref[:, cols]
                dhi = dhi + pi_ref[:, cols]
            heads.append((dhq, dhf, dhi, dlow, dst))
        for i, (dhq, dhf, dhi, dlow, dst) in enumerate(heads):
            cols = slice(i * HG_DK, (i + 1) * HG_DK)
            dst_ref[i] = dst
            dhq_ref[:, cols] = dhq.astype(dhq_ref.dtype)
            dhi_ref[:, cols] = dhi.astype(dhi_ref.dtype)
            dhf_ref[:, cols] = dhf.astype(dhf_ref.dtype)

        @pl.when(n == 0)
        def _():
            for i in range(hp):
                dlow_ref[i] = heads[i][3]

        @pl.when(n > 0)
        def _():
            for i in range(hp):
                dlow_ref[i] += heads[i][3]

    def zspec(col):
        return pl.BlockSpec((rows, wide), lambda h, n: (blk(n), col // wide + h))

    hspec = pl.BlockSpec((rows, wide), lambda h, n: (blk(n), h))
    in_specs = [zspec(col_q), zspec(col_f), zspec(col_v), pl.BlockSpec((hp, 1, HG_DK), lambda h, n: (h, 0, 0)),
                pl.BlockSpec((hp, 1, HG_DK, HG_DK), lambda h, n: (h, blk(n), 0, 0)), hspec]
    ins = [z, z, z, lower3, states, do]
    if has_prev:
        in_specs += [hspec, hspec]
        ins += list(prev)
    full_wide = HG_HEADS * HG_DK
    acc_dtype = BF16 if has_prev else F32
    return pl.pallas_call(
        body, name="gla_bwd_rev" if rev else "gla_bwd", grid=(HG_HEADS // hp, nb),
        in_specs=in_specs,
        out_specs=[hspec, hspec, hspec, pl.BlockSpec((hp, 1, HG_DK), lambda h, n: (h, 0, 0))],
        out_shape=[_sds((T, full_wide), acc_dtype), _sds((T, full_wide), acc_dtype), _sds((T, full_wide), BF16),
                   _sds((HG_HEADS, 1, HG_DK), F32)],
        scratch_shapes=[pltpu.VMEM((hp, HG_DK, HG_DK), F32)],
        compiler_params=_params(("parallel", "arbitrary")),
    )(*ins)


def _lower_fn(lb):
    e = jnp.exp(lb - jnp.max(lb, axis=0, keepdims=True))
    return (e / jnp.sum(e, axis=0, keepdims=True))[0]


def _lower_bounds(lb):
    def body(lb_ref, o_ref):
        o_ref[...] = _lower_fn(lb_ref[...])
    return pl.pallas_call(body, name="lower_bounds", out_shape=_sds(lb.shape[1:], F32))(lb)


def _row_tile(r, cap=1024):
    best = None
    for t in range(16, min(r, cap) + 1, 16):
        if r % t == 0:
            best = t
    return best if best is not None else r


def _sum4(shards, recv, chip, *, name):
    _, R, C = shards.shape
    tr = _row_tile(R)

    def body(chip_ref, o_ref, r_ref, out_ref):
        out_ref[...] = ((o_ref[0].astype(F32) + r_ref[0].astype(F32)) + r_ref[1].astype(F32)) + r_ref[2].astype(F32)

    grid_spec = pltpu.PrefetchScalarGridSpec(
        num_scalar_prefetch=1, grid=(R // tr,),
        in_specs=[pl.BlockSpec((1, tr, C), lambda i, chip_ref: (chip_ref[0], i, 0)),
                  pl.BlockSpec((3, tr, C), lambda i, chip_ref: (0, i, 0))],
        out_specs=pl.BlockSpec((tr, C), lambda i, chip_ref: (i, 0)))
    return pl.pallas_call(
        body, name=name, grid_spec=grid_spec, out_shape=_sds((R, C), F32), compiler_params=_params(("parallel",)),
    )(chip, shards, recv)


def _adamw_math(w, g, m, v):
    m = ADAM_B1 * m + (1.0 - ADAM_B1) * g
    v = ADAM_B2 * v + (1.0 - ADAM_B2) * (g * g)
    m_hat = m / (1.0 - ADAM_B1 ** ADAM_STEP)
    v_hat = v / (1.0 - ADAM_B2 ** ADAM_STEP)
    delta = -ADAM_LR * (m_hat / (jnp.sqrt(v_hat) + ADAM_EPS) + ADAM_WD * w)
    return delta, m, v


def _adamw(w, g_a, g_b, m, v, *, name):
    R, C = w.shape
    tr = _row_tile(R)
    two = g_b is not None

    def body(*refs):
        w_ref, ga_ref = refs[0], refs[1]
        rest = refs[2:]
        g = ga_ref[...]
        if two:
            g = g + rest[0][...]
            rest = rest[1:]
        m_ref, v_ref, g_out, d_out, m_out, v_out = rest
        delta, m_new, v_new = _adamw_math(w_ref[...], g, m_ref[...], v_ref[...])
        g_out[...] = g
        d_out[...] = delta
        m_out[...] = m_new
        v_out[...] = v_new

    spec = pl.BlockSpec((tr, C), lambda i: (i, 0))
    ins = [w, g_a] + ([g_b] if two else []) + [m, v]
    return pl.pallas_call(
        body, name=name, grid=(R // tr,), in_specs=[spec] * len(ins), out_specs=[spec] * 4,
        out_shape=[_sds((R, C), F32)] * 4, compiler_params=_params(("parallel",)),
    )(*ins)


def _adamw_small(red, lb_full, ws, ms, vs):
    n = len(ws)

    def pieces(shape):
        out = []
        for j, idx in enumerate(itertools.product(*[range(d) for d in shape[:-1]])):
            out.append((idx[:-1] + (slice(idx[-1], idx[-1] + 1), slice(None)), j * shape[-1]))
        return out

    def body(*refs):
        red_ref, lb_ref = refs[0], refs[1]
        w_refs, m_refs, v_refs = refs[2:2 + n], refs[2 + n:2 + 2 * n], refs[2 + 2 * n:2 + 3 * n]
        out_refs = refs[2 + 3 * n:]
        chip = 2 * lax.axis_index("x") + lax.axis_index("y")
        n_f, shard = lb_ref.shape[-1], w_refs[n - 1].shape[-1]
        _, vjp = jax.vjp(_lower_fn, lb_ref[...])
        dlb = vjp(red_ref[8:10, 0:n_f])[0]
        for i in range(n):
            width = w_refs[i].shape[-1]
            for j, (at, lane) in enumerate(pieces(w_refs[i].shape)):
                if i < n - 1:
                    g = red_ref[i:i + 1, lane:lane + width]
                else:
                    row = dlb[j // 2][j % 2:j % 2 + 1]
                    g = sum(jnp.where(chip == q, row[:, q * shard:(q + 1) * shard], 0.0) for q in range(N_CHIPS))
                delta, m_new, v_new = _adamw_math(w_refs[i][at], g, m_refs[i][at], v_refs[i][at])
                for o_ref, val in zip(out_refs[4 * i:4 * i + 4], (g, delta, m_new, v_new)):
                    o_ref[at] = val

    return pl.pallas_call(
        body, name="adamw_small", out_shape=[_sds(w.shape, F32) for w in ws for _ in range(4)],
    )(red, lb_full, *ws, *ms, *vs)


def _chip_peers():
    x, y, c = lax.axis_index("x"), lax.axis_index("y"), lax.axis_index("c")
    return (x, y, c), 2 * x + y, [(1 - x, y), (x, 1 - y), (1 - x, 1 - y)]


_HBM = pl.BlockSpec(memory_space=pltpu.HBM)
_SEM = pl.BlockSpec(memory_space=pltpu.SEMAPHORE)
_EFFECT = pltpu.SideEffectType.DATAFLOW_SIDE_EFFECTING


def _exchange_copies(srcs, lands, sems, mode):
    (x, y, c), me, chips = _chip_peers()
    copies = []
    for t, (src, land) in enumerate(zip(srcs, lands)):
        if mode == "swap":
            copies.append(pltpu.make_async_remote_copy(src, land, sems[0].at[3 * t], sems[1].at[3 * t],
                                                       device_id=(x, y, 1 - c), device_id_type=MESH))
            continue
        for k, (px, py) in enumerate(chips):
            gather = mode == "gather"
            copies.append(pltpu.make_async_remote_copy(
                src if gather else src.at[2 * px + py], land.at[me] if gather else land.at[k],
                sems[0].at[3 * t + k], sems[1].at[3 * t + k], device_id=(px, py, c), device_id_type=MESH))
        if mode == "gather":
            copies.append(pltpu.make_async_copy(src, land.at[me], sems[2].at[t]))
    return copies


def _exchange_start(srcs, *, mode, name, after=None):
    n = len(srcs)
    n_sem = 3 if mode == "gather" else 2
    n_in = 2 * n + (after is not None)
    land_shape = {"gather": lambda s: (N_CHIPS,) + s.shape, "scatter": lambda s: (3,) + s.shape[1:], "swap": lambda s: s.shape}
    lands = [_sds(land_shape[mode](s), s.dtype) for s in srcs]

    def body(*refs):
        for cp in _exchange_copies(refs[:n], refs[n:2 * n], refs[n_in:n_in + n_sem], mode):
            cp.start()
        token = refs[-1]
        token[...] = jnp.zeros_like(token)

    sem_shapes = [pltpu.SemaphoreType.DMA((3 * n,)), pltpu.SemaphoreType.DMA((3 * n,))]
    sem_shapes += [pltpu.SemaphoreType.DMA((n,))] if mode == "gather" else []
    thru = [pltpu.HBM(s.shape, s.dtype) for s in srcs] + [pltpu.HBM(l.shape, l.dtype) for l in lands]
    res = pl.pallas_call(
        body, name=name, in_specs=[_HBM] * (2 * n) + [_ANY] * (after is not None),
        out_specs=[_SEM] * n_sem + [_HBM] * (2 * n) + [pl.BlockSpec(memory_space=pltpu.VMEM)],
        out_shape=sem_shapes + thru + [_sds((8, LANE), F32)], input_output_aliases={i: n_sem + i for i in range(2 * n)},
        compiler_params=pltpu.CompilerParams(has_side_effects=_EFFECT),
    )(*[pltpu.with_memory_space_constraint(s, pltpu.HBM) for s in srcs],
      *[pltpu.with_memory_space_constraint(lax.empty(l.shape, l.dtype), pltpu.HBM) for l in lands],
      *([after] if after is not None else []))
    return (res[:n_sem], res[n_sem:n_sem + n], res[n_sem + n:n_sem + 2 * n], mode), res[-1]


def _exchange_wait(started, after, *, name):
    sems, srcs, lands, mode = started
    n, n_sem = len(srcs), len(sems)
    after = list(after) if isinstance(after, (list, tuple)) else [after]

    def body(*refs):
        for cp in _exchange_copies(refs[:n], refs[n:2 * n], refs[2 * n:2 * n + n_sem], mode):
            cp.wait()

    res = pl.pallas_call(
        body, name=name, in_specs=[_HBM] * (2 * n) + [_SEM] * n_sem + [_ANY] * len(after), out_specs=[_HBM] * (2 * n),
        out_shape=[pltpu.HBM(a.shape, a.dtype) for a in list(srcs) + list(lands)],
        input_output_aliases={i: i for i in range(2 * n)},
        compiler_params=pltpu.CompilerParams(has_side_effects=_EFFECT),
    )(*srcs, *lands, *sems, *after)
    return res[:n], res[n:]


def _allreduce_small(pack, after):
    R, C = pack.shape

    def body(in_ref, after_ref, out_ref, slots, send_sems, recv_sems):
        x, y, c = lax.axis_index("x"), lax.axis_index("y"), lax.axis_index("c")
        me = 4 * x + 2 * y + c
        slots[me] = in_ref[...]
        copies = []
        for k in range(1, N_DEV):
            peer = (x ^ ((k >> 2) & 1), y ^ ((k >> 1) & 1), c ^ (k & 1))
            cp = pltpu.make_async_remote_copy(in_ref, slots.at[me], send_sems.at[k - 1], recv_sems.at[k - 1],
                                              device_id=peer, device_id_type=MESH)
            cp.start()
            copies.append(cp)
        for cp in copies:
            cp.wait()
        acc = slots[0]
        for d in range(1, N_DEV):
            acc = acc + slots[d]
        out_ref[...] = acc

    return pl.pallas_call(
        body, name="allreduce_small", out_shape=_sds((R, C), F32),
        in_specs=[pl.BlockSpec(memory_space=pltpu.VMEM), _ANY], out_specs=pl.BlockSpec(memory_space=pltpu.VMEM),
        scratch_shapes=[pltpu.VMEM((N_DEV, R, C), F32), pltpu.SemaphoreType.DMA((N_DEV - 1,)),
                        pltpu.SemaphoreType.DMA((N_DEV - 1,))],
        compiler_params=_params(),
    )(pack, after)


_Z_CQ, _Z_CKV, _Z_HQ, _Z_HFF, _Z_HFB, _Z_HI, _Z_HG, _Z_KR, _Z_END = 0, 256, 512, 1024, 1536, 2048, 2560, 3072, 3200


def _to_z_layout(wt):
    pad = jnp.zeros((_Z_END - _Z_KR - QK_ROPE, wt.shape[1]), wt.dtype)
    return jnp.concatenate([wt[:512], wt[512 + QK_ROPE:], wt[512:512 + QK_ROPE], pad], axis=0)


def _from_z_layout(wt):
    return jnp.concatenate([wt[:512], wt[_Z_KR:_Z_KR + QK_ROPE], wt[512:_Z_KR]], axis=0)


def _col_shards_to_full(g):
    return jnp.transpose(g, (1, 0, 2)).reshape(g.shape[1], -1)


def _full_to_col_shards(w):
    r, c = w.shape
    return jnp.transpose(w.reshape(r, N_CHIPS, c // N_CHIPS), (1, 0, 2))


def _full_to_row_shards(w):
    r, c = w.shape
    return w.reshape(N_CHIPS, r // N_CHIPS, c)


def kernel(x, p, positions, g_mix, w_in, g_qa, g_kva, w_qb, w_kvb, g_qn, g_kn, lb_param, g_hgo, w_o, g_ffn, w_gate, w_up, w_down, g_ple, w_ple_gate, w_ple_proj, loss_target, m_g_mix, m_w_in, m_g_qa, m_g_kva, m_w_qb, m_w_kvb, m_g_qn, m_g_kn, m_lb_param, m_g_hgo, m_w_o, m_g_ffn, m_w_gate, m_w_up, m_w_down, m_g_ple, m_w_ple_gate, m_w_ple_proj, v_g_mix, v_w_in, v_g_qa, v_g_kva, v_w_qb, v_w_kvb, v_g_qn, v_g_kn, v_lb_param, v_g_hgo, v_w_o, v_g_ffn, v_w_gate, v_w_up, v_w_down, v_g_ple, v_w_ple_gate, v_w_ple_proj):
    w_named = dict(g_mix=g_mix, w_in=w_in, g_qa=g_qa, g_kva=g_kva, w_qb=w_qb, w_kvb=w_kvb, g_qn=g_qn, g_kn=g_kn,
                   lb_param=lb_param, g_hgo=g_hgo, w_o=w_o, g_ffn=g_ffn, w_gate=w_gate, w_up=w_up, w_down=w_down,
                   g_ple=g_ple, w_ple_gate=w_ple_gate, w_ple_proj=w_ple_proj)
    m_named = dict(g_mix=m_g_mix, w_in=m_w_in, g_qa=m_g_qa, g_kva=m_g_kva, w_qb=m_w_qb, w_kvb=m_w_kvb, g_qn=m_g_qn,
                   g_kn=m_g_kn, lb_param=m_lb_param, g_hgo=m_g_hgo, w_o=m_w_o, g_ffn=m_g_ffn, w_gate=m_w_gate,
                   w_up=m_w_up, w_down=m_w_down, g_ple=m_g_ple, w_ple_gate=m_w_ple_gate, w_ple_proj=m_w_ple_proj)
    v_named = dict(g_mix=v_g_mix, w_in=v_w_in, g_qa=v_g_qa, g_kva=v_g_kva, w_qb=v_w_qb, w_kvb=v_w_kvb, g_qn=v_g_qn,
                   g_kn=v_g_kn, lb_param=v_lb_param, g_hgo=v_g_hgo, w_o=v_w_o, g_ffn=v_g_ffn, w_gate=v_w_gate,
                   w_up=v_w_up, w_down=v_w_down, g_ple=v_g_ple, w_ple_gate=v_w_ple_gate, w_ple_proj=v_w_ple_proj)
    order = list(w_named)
    transposed = ("w_in", "w_qb", "w_gate", "w_up")
    col_sharded = ("w_kvb", "w_ple_proj")
    row_sharded = ("w_o", "w_down", "w_ple_gate")
    big = transposed + col_sharded + row_sharded

    def view(n, a):
        return jnp.transpose(a[0]) if n in transposed else a[0]

    def unview(n, a):
        return (jnp.transpose(a) if n in transposed else a)[None]

    def to_shards(n, g):
        return _full_to_col_shards(g) if n in col_sharded else _full_to_row_shards(g)

    x2d, p2d, tgt = x[0], p[0, 0], loss_target[0]
    T, D = x2d.shape

    lb_flat = lb_param.reshape(-1, lb_param.shape[-1])
    gather_groups = (("w_in",), ("w_qb", "w_kvb"), ("w_o", "w_gate", "w_up", "w_down", "w_ple_gate", "w_ple_proj"))
    gather_started = []

    casts = {n: view(n, w_named[n]).astype(BF16) for n in big}

    def gather_start(gi, after):
        srcs = [casts[n] for n in gather_groups[gi]] + ([lb_flat] if gi == 0 else [])
        started, token = _exchange_start(srcs, mode="gather", name=f"gather_start_{gi}", after=after)
        gather_started.append(started)
        return token

    full = {}

    def gather_wait(gi, after):
        _, got = _exchange_wait(gather_started[gi], after, name=f"gather_wait_{gi}")
        for n, g in zip(gather_groups[gi], got):
            full[n] = _col_shards_to_full(g) if n in col_sharded else g.reshape(-1, g.shape[-1])
        return got

    g_hgo_row = g_hgo.reshape(1, -1)

    inv_freq = ROPE_THETA ** (-jnp.arange(0, QK_ROPE, 2, dtype=F32) / QK_ROPE)
    ang = positions[0].astype(F32)[:, None] * inv_freq
    cos, sin = jnp.cos(ang), jnp.sin(ang)
    token = gather_start(0, None)
    h1 = _stage(_rms, [x2d], [g_mix], [_sds((T, D), BF16)], [], name="norm_mix", after=token)[0]
    got = gather_wait(0, [h1, cos, sin] + [casts[n] for g in gather_groups[1:] for n in g])
    token = got[0]
    for gi in range(1, len(gather_groups)):
        token = gather_start(gi, token)
    lb_full = _col_shards_to_full(got[-1]).reshape(lb_param.shape[0], lb_param.shape[1], -1)
    w_in_zt = _to_z_layout(full["w_in"])
    z = _mm(h1, w_in_zt, tb=True, name="in_proj", after=token)
    qn, kvn = _stage(_mla_a_fn, [_cols(z, 256, 0), _cols(z, 256, 1)], [g_qa, g_kva],
                     [_sds((T, 256), BF16), _sds((T, 256), BF16)], [], name="mla_latent_norm")
    gather_wait(1, qn)
    q_raw = _mm(qn, full["w_qb"], tb=True, name="q_up")
    kv_raw = _mm(kvn, full["w_kvb"], name="kv_up")
    kr = _cols(z, LANE, _Z_KR // LANE)
    q, k, v = _stage(_mla_b_fn, [q_raw, kv_raw, kr, cos, sin], [g_qn, g_kn],
                     [_sds((MLA_HEADS, T, QK_HEAD), BF16), _sds((MLA_HEADS, T, QK_HEAD), BF16),
                      _sds((MLA_HEADS, T, V_HEAD), BF16)], [], name="mla_qk_norm_rope")
    att, lse = _attention_fwd(q, k, v)

    lower = _lower_bounds(lb_full)
    lower3 = lower.reshape(2, HG_HEADS, 1, HG_DK)
    o_f, st_f = _gla_fwd(z, lower3[0], rev=False, col_q=_Z_HQ, col_f=_Z_HFF, col_v=_Z_HI, hp=GLA_FWD_HEADS)
    o_b, st_b = _gla_fwd(z, lower3[1], rev=True, col_q=_Z_HQ, col_f=_Z_HFB, col_v=_Z_HI, hp=GLA_FWD_HEADS)
    hg = _cols(z, 512, _Z_HG // 512)
    mix = _stage(_post_fn, [att, o_f, o_b, hg], [g_hgo_row], [_sds((T, att.shape[1] + o_f.shape[1]), BF16)], [],
                 name="mix_out")[0]
    gather_wait(2, mix)
    x1, h2 = _mm_fused(mix, full["w_o"], _resid_norm_fn, [x2d], [g_ffn], [F32, BF16], [], full_rows=True,
                       name="out_proj")
    gt, up, act = _mm_fused(h2, full["w_gate"], lambda gt, up: (gt, up, _swiglu_fn(gt, up)), [], [], [BF16, BF16, BF16],
                            [], tb=True, b2=full["w_up"], name="ffn_gate_up")
    x2, h3 = _mm_fused(act, full["w_down"], _resid_norm_fn, [x1], [g_ple], [F32, BF16], [], full_rows=True,
                       name="ffn_down")
    pp = _mm(p2d, full["w_ple_proj"], name="ple_proj")
    dx3, dpg, dpp, loss_part = _mm_fused(
        h3, full["w_ple_gate"], lambda acc, pp, x2, tgt: _ple_loss_fn(x2, acc, pp, tgt), [pp, x2, tgt], [],
        [F32, BF16, BF16], [_sds((1, 1), F32)], full_rows=True, name="ple_gate_loss")

    grads = {}
    scatter_groups = (("w_ple_proj", "w_ple_gate", "w_down", "w_gate", "w_up", "w_o"), ("w_qb", "w_kvb", "w_in"))
    scatter_started = []

    def scatter_start(gi):
        srcs = [to_shards(n, grads[n]) for n in scatter_groups[gi]]
        started, token = _exchange_start(srcs, mode="scatter", name=f"scatter_start_{gi}")
        scatter_started.append(started)
        return token

    chip = 2 * lax.axis_index("x") + lax.axis_index("y")
    swap_started = []

    def reduce_group(gi, after):
        shards, recvs = _exchange_wait(scatter_started[gi], after, name=f"scatter_wait_{gi}")
        sums = [_sum4(s, r, chip.reshape(1), name="sum_" + n) for n, s, r in zip(scatter_groups[gi], shards, recvs)]
        started, token = _exchange_start(sums, mode="swap", name=f"swap_start_{gi}")
        swap_started.append(started)
        return token

    grads["w_ple_proj"] = _mm(p2d, dpp, ta=True, out_dtype=BF16, name="d_w_ple_proj")
    grads["w_ple_gate"] = _mm(h3, dpg, ta=True, out_dtype=BF16, name="d_w_ple_gate")
    dx2, grads["g_ple"] = _mm_fused(
        dpg, full["w_ple_gate"], lambda acc, x2, dx3, g: _norm_bwd_fn(x2, acc, dx3, g), [x2, dx3], [g_ple],
        [F32], [_sds((1, D), F32)], tb=True, full_rows=True, name="d_h3_norm_ple_bwd")
    dgt, dup = _mm_fused(dx2, full["w_down"], lambda acc, gt, up: _swiglu_bwd_fn(gt.astype(F32), up.astype(F32), acc), [gt, up], [],
                         [BF16, BF16], [], tb=True, name="d_act_swiglu_bwd")
    grads["w_down"] = _mm(act, dx2, ta=True, out_dtype=BF16, name="d_w_down")
    grads["w_gate"] = _mm(dgt, h2, ta=True, out_dtype=BF16, name="d_w_gate")
    grads["w_up"] = _mm(dup, h2, ta=True, out_dtype=BF16, name="d_w_up")
    dx1, grads["g_ffn"] = _mm_fused(
        dgt, full["w_gate"], lambda acc, x1, dx2, g: _norm_bwd_fn(x1, acc, dx2, g), [x1, dx2], [g_ffn],
        [F32], [_sds((1, D), F32)], full_rows=True, a2=dup, b2=full["w_up"], name="d_h2_norm_ffn_bwd")
    grads["w_o"] = _mm(mix, dx1, ta=True, out_dtype=BF16, name="d_w_o")
    token = scatter_start(0)
    dmix = _mm(dx1, full["w_o"], tb=True, name="d_mix", after=token)

    half = MLA_HEADS * V_HEAD
    do, dhg, dg_hgo = _stage(_post_bwd_fn, [o_f, o_b, hg, _cols(dmix, half, 1)], [g_hgo_row],
                             [_sds((T, half), F32), _sds((T, half), BF16)], [_sds((1, half), F32)], name="mix_out_bwd")
    grads["g_hgo"] = dg_hgo
    dhq_f, dhi_f, dhf_f, dlow_f = _gla_bwd(z, lower3[0], st_f, do, None, rev=False,
                                           col_q=_Z_HQ, col_f=_Z_HFF, col_v=_Z_HI, hp=GLA_BWD_HEADS)
    dhq, dhi, dhf_b, dlow_b = _gla_bwd(z, lower3[1], st_b, do, (dhq_f, dhi_f), rev=True,
                                       col_q=_Z_HQ, col_f=_Z_HFB, col_v=_Z_HI, hp=GLA_BWD_HEADS)

    dq, dk, dv = _attention_bwd(q, k, v, att, lse, dmix)
    dq_raw, dkv_raw, dkr, grads["g_qn"], grads["g_kn"] = _stage(
        _mla_b_bwd_fn, [q_raw, kv_raw, kr, cos, sin, dq, dk, dv], [g_qn, g_kn],
        [_sds(q_raw.shape, BF16), _sds(kv_raw.shape, BF16), _sds((T, LANE), BF16)],
        [_sds(g_qn.shape, F32), _sds(g_kn.shape, F32)], name="mla_qk_norm_rope_bwd")
    grads["w_qb"] = _mm(dq_raw, qn, ta=True, out_dtype=BF16, name="d_w_qb")
    grads["w_kvb"] = _mm(kvn, dkv_raw, ta=True, out_dtype=BF16, name="d_w_kvb")
    dqn = _mm(dq_raw, full["w_qb"], name="d_qn")
    dkvn = _mm(dkv_raw, full["w_kvb"], tb=True, name="d_kvn")
    dcq, dckv, grads["g_qa"], grads["g_kva"] = _stage(
        _mla_a_bwd_fn, [_cols(z, 256, 0), _cols(z, 256, 1), dqn, dkvn], [g_qa, g_kva],
        [_sds((T, 256), BF16), _sds((T, 256), BF16)], [_sds(g_qa.shape, F32), _sds(g_kva.shape, F32)],
        name="mla_latent_norm_bwd")
    token = reduce_group(0, dcq)
    dz = jnp.concatenate([dcq, dckv, dhq, dhf_f, dhf_b, dhi, dhg, dkr], axis=1)
    grads["w_in"] = _from_z_layout(_mm(dz, h1, ta=True, out_dtype=BF16, name="d_w_in", after=token))
    token = scatter_start(1)
    grad_x, grads["g_mix"] = _mm_fused(
        dz, w_in_zt, lambda acc, x, dx1, g: _norm_bwd_fn(x, acc, dx1, g), [x2d, dx1], [g_mix],
        [F32], [_sds((1, D), F32)], full_rows=True, name="d_h1_norm_mix_bwd", after=token)

    out_g, out_d, out_m, out_v = {}, {}, {}, {}

    def update_group(gi, after):
        mine, theirs = _exchange_wait(swap_started[gi], after, name=f"swap_wait_{gi}")
        for n, a, b in zip(scatter_groups[gi], mine, theirs):
            out_g[n], out_d[n], out_m[n], out_v[n] = (
                unview(n, t) for t in _adamw(view(n, w_named[n]), a, b, view(n, m_named[n]), view(n, v_named[n]),
                                             name="adamw_" + n))
        return out_v[scatter_groups[gi][-1]]

    done = update_group(0, grad_x)
    token = reduce_group(1, done)
    done = update_group(1, token)

    small = ("g_mix", "g_qa", "g_kva", "g_qn", "g_kn", "g_hgo", "g_ffn", "g_ple")
    small_all = small + ("lb_param",)
    width = -(-max(w_named[n].size for n in small_all) // LANE) * LANE

    def row(a):
        a = a.reshape(1, -1)
        return jnp.pad(a, ((0, 0), (0, width - a.shape[1])))

    dlower = jnp.concatenate([dlow_f.reshape(1, -1), dlow_b.reshape(1, -1)], axis=0)
    pack = jnp.concatenate([row(grads[n]) for n in small] + [row(dlower[0]), row(dlower[1]), row(loss_part)]
                           + [jnp.zeros((5, width), F32)], axis=0)
    red = _allreduce_small(pack, done)
    loss = red[10, 0]

    outs = _adamw_small(red, lb_full, [w_named[n] for n in small_all], [m_named[n] for n in small_all],
                        [v_named[n] for n in small_all])
    for i, n in enumerate(small_all):
        out_g[n], out_d[n], out_m[n], out_v[n] = outs[4 * i:4 * i + 4]

    return (loss, grad_x[None], *[out_g[n] for n in order], *[out_d[n] for n in order],
            *[out_m[n] for n in order], *[out_v[n] for n in order])
```

```python
import functools
import itertools

import jax
import jax.numpy as jnp
from jax import lax
from jax.experimental import pallas as pl
from jax.experimental.pallas import tpu as pltpu

F32 = jnp.float32
BF16 = jnp.bfloat16
MESH = pl.DeviceIdType.MESH

EPS = 1e-6
ROPE_THETA = 10000.0
MLA_HEADS = 4
QK_NOPE = 128
QK_ROPE = 64
QK_HEAD = QK_NOPE + QK_ROPE
V_HEAD = 128
HG_HEADS = 4
HG_DK = 128
CHUNK = 64
ADAM_LR = 0.001
ADAM_B1 = 0.9
ADAM_B2 = 0.999
ADAM_EPS = 1e-08
ADAM_WD = 0.01
ADAM_STEP = 10

LANE = 128
VMEM_LIMIT = 56 * 1024 * 1024
TOK_TILE = 256
GLA_GROUP = 16
GLA_FWD_HEADS = 4
GLA_BWD_HEADS = 2
ATT_TQ = 1024
ATT_TK = 1024
ATT_CHUNK = 512
LOG2_E = 1.4426950408889634
N_CHIPS = 4
N_DEV = 8


_ANY = pl.BlockSpec(memory_space=pl.ANY)


def _params(dims=None, **kw):
    return pltpu.CompilerParams(dimension_semantics=dims, vmem_limit_bytes=VMEM_LIMIT, **kw)


def _tile_candidates(n, cap):
    out = [d for d in range(LANE, min(n, cap) + 1, LANE) if n % d == 0]
    if n <= cap and n not in out:
        out.append(n)
    return out or [n]


MM_VMEM_BUDGET = 40 * 1024 * 1024
MM_MIN_ROWS = 256
MM_MAX_ROWS = 1536
HBM_BYTES_PER_S = 2.8e12
MXU_FLOPS_PER_S = 8e14
STEP_OVERHEAD_S = 0.35e-6


def _mm_tiles(M, N, K, a_bytes, b_bytes, o_bytes, has_add, full_rows=False, full_k=False, n_a=1, n_b=1):
    cast_a, cast_b = a_bytes > 2, b_bytes > 2
    a_bytes, b_bytes = n_a * a_bytes, n_b * b_bytes
    best = None
    for tm in [t for t in _tile_candidates(M, MM_MAX_ROWS) if t >= min(M, MM_MIN_ROWS)]:
        for tn in ([N] if full_rows else _tile_candidates(N, N)):
            for tk in ([K] if full_k else _tile_candidates(K, K)):
                ni, nj, nk = M // tm, N // tn, K // tk
                vmem = 2 * (tm * tk * a_bytes + tk * tn * b_bytes + tm * tn * o_bytes * (2 if has_add else 1))
                vmem += tm * tn * 4 * (2 if nk > 1 else 1)
                vmem += (tm * tk * 2 * n_a if cast_a else 0) + (tk * tn * 2 * n_b if cast_b else 0)
                if vmem > MM_VMEM_BUDGET:
                    continue
                moved = M * K * a_bytes * (nj if nk > 1 else 1) + K * N * b_bytes * (1 if nj == nk == 1 else ni)
                moved += M * N * o_bytes * (2 if has_add else 1)
                t = max(moved / HBM_BYTES_PER_S, 2 * M * N * K / MXU_FLOPS_PER_S) + ni * nj * nk * STEP_OVERHEAD_S
                if best is None or t < best[0]:
                    best = (t, tm, tn, tk)
    assert best is not None, (M, N, K)
    return best[1:]


def _dot_raw(a, b, kind):
    nb = a.ndim - 2
    batch = ((0,), (0,)) if nb else ((), ())
    ca = nb if kind == "tn" else nb + 1
    cb = nb + 1 if kind == "nt" else nb
    return lax.dot_general(a.astype(BF16), b.astype(BF16), (((ca,), (cb,)), batch), preferred_element_type=F32)


@functools.partial(jax.custom_vjp, nondiff_argnums=(2,))
def _bdot(a, b, kind):
    return _dot_raw(a, b, kind)


def _bdot_fwd(a, b, kind):
    return _dot_raw(a, b, kind), (a, b)


def _bdot_bwd(kind, res, g):
    a, b = res
    if kind == "nn":
        da, db = _bdot(g, b, "nt"), _bdot(a, g, "tn")
    elif kind == "nt":
        da, db = _bdot(g, b, "nn"), _bdot(g, a, "tn")
    else:
        da, db = _bdot(b, g, "nt"), _bdot(a, g, "nn")
    return da.astype(a.dtype), db.astype(b.dtype)


_bdot.defvjp(_bdot_fwd, _bdot_bwd)


def _mm(a, b, *, name, ta=False, tb=False, add=None, out_dtype=F32, after=None):
    K, M = a.shape if ta else a.shape[::-1]
    N, Kb = b.shape if tb else b.shape[::-1]
    assert K == Kb, (a.shape, b.shape, ta, tb)
    tm, tn, tk = _mm_tiles(M, N, K, a.dtype.itemsize, b.dtype.itemsize, jnp.dtype(out_dtype).itemsize, add is not None)
    nk = K // tk
    kind = "tn" if ta else ("nt" if tb else "nn")
    assert not (ta and tb)
    a_spec = pl.BlockSpec((tk, tm), lambda i, j, k: (k, i)) if ta else pl.BlockSpec((tm, tk), lambda i, j, k: (i, k))
    b_spec = pl.BlockSpec((tn, tk), lambda i, j, k: (j, k)) if tb else pl.BlockSpec((tk, tn), lambda i, j, k: (k, j))
    o_spec = pl.BlockSpec((tm, tn), lambda i, j, k: (i, j))
    has_add = add is not None

    def body(*refs):
        a_ref, b_ref = refs[0], refs[1]
        add_ref = refs[2] if has_add else None
        o_ref = refs[n_in]
        part = _dot_raw(a_ref[...], b_ref[...], kind)
        if nk == 1:
            if has_add:
                part = part + add_ref[...].astype(F32)
            o_ref[...] = part.astype(o_ref.dtype)
            return
        acc_ref = refs[-1]
        k = pl.program_id(2)

        @pl.when(k == 0)
        def _():
            acc_ref[...] = part

        @pl.when(k > 0)
        def _():
            acc_ref[...] += part

        @pl.when(k == nk - 1)
        def _():
            r = acc_ref[...]
            if has_add:
                r = r + add_ref[...].astype(F32)
            o_ref[...] = r.astype(o_ref.dtype)

    ins = [a, b] + ([add] if has_add else []) + ([after] if after is not None else [])
    in_specs = [a_spec, b_spec] + ([o_spec] if has_add else []) + ([_ANY] if after is not None else [])
    n_in = len(ins)
    return pl.pallas_call(
        body, name=name, grid=(M // tm, N // tn, nk), in_specs=in_specs, out_specs=o_spec,
        out_shape=jax.ShapeDtypeStruct((M, N), out_dtype),
        scratch_shapes=[pltpu.VMEM((tm, tn), F32)] if nk > 1 else [],
        compiler_params=_params(("parallel", "parallel", "arbitrary")),
    )(*ins)


def _mm_fused(a, b, fn, tiles, params, out_dtypes, sums, *, name, ta=False, tb=False, full_rows=False, after=None,
              b2=None, a2=None):
    K, M = a.shape if ta else a.shape[::-1]
    N, Kb = b.shape if tb else b.shape[::-1]
    assert K == Kb and not (ta and tb), (a.shape, b.shape, ta, tb)
    per_elem = sum(t.dtype.itemsize for t in tiles) + sum(jnp.dtype(d).itemsize for d in out_dtypes)
    n_b = 1 if b2 is None else 2
    n_a = 1 if a2 is None else 2
    tm, tn, tk = _mm_tiles(M, N, K, a.dtype.itemsize, b.dtype.itemsize, per_elem, False, full_rows, b2 is not None,
                           n_a, n_b)
    nk = K // tk
    kind = "tn" if ta else ("nt" if tb else "nn")
    a_spec = pl.BlockSpec((tk, tm), lambda i, j, k: (k, i)) if ta else pl.BlockSpec((tm, tk), lambda i, j, k: (i, k))
    b_spec = pl.BlockSpec((tn, tk), lambda i, j, k: (j, k)) if tb else pl.BlockSpec((tk, tn), lambda i, j, k: (k, j))
    o_spec = pl.BlockSpec((tm, tn), lambda i, j, k: (i, j))
    ins = [a, b] + ([b2] if b2 is not None else []) + ([a2] if a2 is not None else [])
    ins += list(tiles) + list(params) + ([after] if after is not None else [])
    in_specs = [a_spec] + [b_spec] * n_b + [a_spec] * (n_a - 1) + [o_spec] * len(tiles)
    in_specs += [pl.BlockSpec(p.shape, lambda i, j, k, nd=p.ndim: (0,) * nd) for p in params]
    in_specs += [_ANY] if after is not None else []
    n_in, n_t, n_p, n_o = len(ins), len(tiles), len(params), len(out_dtypes)

    def body(*refs):
        outs, sum_refs = refs[n_in:n_in + n_o], refs[n_in + n_o:n_in + n_o + len(sums)]

        def finish(*products):
            res = fn(*products, *[t[...] for t in refs[n_a + n_b:n_a + n_b + n_t + n_p]])
            for o_ref, v in zip(outs, res[:n_o]):
                o_ref[...] = v.astype(o_ref.dtype)
            first = jnp.logical_and(pl.program_id(0) == 0, pl.program_id(1) == 0)
            for s_ref, v in zip(sum_refs, res[n_o:]):
                @pl.when(first)
                def _(s_ref=s_ref, v=v):
                    s_ref[...] = v

                @pl.when(jnp.logical_not(first))
                def _(s_ref=s_ref, v=v):
                    s_ref[...] += v

        part = _dot_raw(refs[0][...], refs[1][...], kind)
        if nk == 1 and a2 is not None:
            finish(part + _dot_raw(refs[3][...], refs[2][...], kind))
            return
        if nk == 1:
            finish(part, *([_dot_raw(refs[0][...], refs[2][...], kind)] if b2 is not None else []))
            return
        acc_ref = refs[-1]
        k = pl.program_id(2)

        @pl.when(k == 0)
        def _():
            acc_ref[...] = part

        @pl.when(k > 0)
        def _():
            acc_ref[...] += part

        @pl.when(k == nk - 1)
        def _():
            finish(acc_ref[...])

    out_shape = [_sds((M, N), d) for d in out_dtypes] + list(sums)
    out_specs = [o_spec] * n_o + [pl.BlockSpec(s.shape, lambda i, j, k, nd=len(s.shape): (0,) * nd) for s in sums]
    order = ("arbitrary",) * 3 if sums else ("parallel", "parallel", "arbitrary")
    return pl.pallas_call(
        body, name=name, grid=(M // tm, N // tn, nk), in_specs=in_specs, out_specs=out_specs, out_shape=out_shape,
        scratch_shapes=[pltpu.VMEM((tm, tn), F32)] if nk > 1 else [], compiler_params=_params(order),
    )(*ins)


def _cols(arr, width, block):
    return (arr, width, block)


def _stage(fn, tiles, params, out_tiles, out_sums, *, name, tile=TOK_TILE, after=None):
    def tok_spec(shape, width=None, block=0):
        if len(shape) == 2:
            w = shape[1] if width is None else width
            return pl.BlockSpec((tile, w), lambda i: (i, block))
        return pl.BlockSpec((shape[0], tile, shape[2]), lambda i: (0, i, 0))

    arrays, in_specs = [], []
    for t in tiles:
        if isinstance(t, tuple):
            arr, width, block = t
            arrays.append(arr)
            in_specs.append(tok_spec(arr.shape, width, block))
        else:
            arrays.append(t)
            in_specs.append(tok_spec(t.shape))
    n_tok = arrays[0].shape[0] if arrays[0].ndim == 2 else arrays[0].shape[1]
    for p in params:
        arrays.append(p)
        in_specs.append(pl.BlockSpec(p.shape, lambda i, nd=p.ndim: (0,) * nd))
    out_shape = list(out_tiles) + list(out_sums)
    out_specs = [tok_spec(o.shape) for o in out_tiles]
    out_specs += [pl.BlockSpec(o.shape, lambda i, nd=len(o.shape): (0,) * nd) for o in out_sums]
    n_fn, n_ot = len(arrays), len(out_tiles)
    if after is not None:
        arrays.append(after)
        in_specs.append(_ANY)
    n_in = len(arrays)

    def body(*refs):
        res = fn(*[r[...] for r in refs[:n_fn]])
        if not isinstance(res, (tuple, list)):
            res = (res,)
        outs = refs[n_in:]
        for o_ref, r in zip(outs[:n_ot], res[:n_ot]):
            o_ref[...] = r.astype(o_ref.dtype)
        i = pl.program_id(0)
        for o_ref, r in zip(outs[n_ot:], res[n_ot:]):
            @pl.when(i == 0)
            def _(o_ref=o_ref, r=r):
                o_ref[...] = r.astype(o_ref.dtype)

            @pl.when(i > 0)
            def _(o_ref=o_ref, r=r):
                o_ref[...] += r.astype(o_ref.dtype)

    res = pl.pallas_call(
        body, name=name, grid=(n_tok // tile,), in_specs=in_specs, out_specs=out_specs, out_shape=out_shape,
        compiler_params=_params(("arbitrary",)),
    )(*arrays)
    return res


def _sds(shape, dtype):
    return jax.ShapeDtypeStruct(tuple(shape), dtype)


def _sigmoid(x):
    return 0.5 * jnp.tanh(0.5 * x) + 0.5


def _rms(x, g):
    return x * lax.rsqrt(jnp.mean(x * x, axis=-1, keepdims=True) + EPS) * g


def _norm_bwd_fn(x, dh, dres, g):
    _, vjp = jax.vjp(_rms, x, g)
    dx, dg = vjp(dh)
    return dx + dres, dg


def _mla_a_fn(cq, ckv, g_qa, g_kva):
    return _rms(cq, g_qa), _rms(ckv, g_kva)


def _mla_a_bwd_fn(cq, ckv, dqn, dkvn, g_qa, g_kva):
    _, vjp = jax.vjp(_mla_a_fn, cq, ckv, g_qa, g_kva)
    return vjp((dqn, dkvn))


def _rope(t, cos, sin):
    t1, t2 = t[:, :QK_ROPE // 2], t[:, QK_ROPE // 2:]
    return jnp.concatenate([t1 * cos - t2 * sin, t1 * sin + t2 * cos], axis=-1)


def _mla_b_fn(q_raw, kv_raw, kr, cos, sin, g_qn, g_kn):
    krope = kr[:, :QK_ROPE]
    qs, ks, vs = [], [], []
    for h in range(MLA_HEADS):
        qh = _rms(q_raw[:, h * QK_HEAD:(h + 1) * QK_HEAD], g_qn)
        kvh = kv_raw[:, h * (QK_NOPE + V_HEAD):(h + 1) * (QK_NOPE + V_HEAD)]
        kh = _rms(jnp.concatenate([kvh[:, :QK_NOPE], krope], axis=-1), g_kn)
        qs.append(jnp.concatenate([qh[:, :QK_NOPE], _rope(qh[:, QK_NOPE:], cos, sin)], axis=-1))
        ks.append(jnp.concatenate([kh[:, :QK_NOPE], _rope(kh[:, QK_NOPE:], cos, sin)], axis=-1))
        vs.append(kvh[:, QK_NOPE:])
    return jnp.stack(qs), jnp.stack(ks), jnp.stack(vs)


def _mla_b_bwd_fn(q_raw, kv_raw, kr, cos, sin, dq, dk, dv, g_qn, g_kn):
    _, vjp = jax.vjp(lambda a, b, c, d, e: _mla_b_fn(a, b, c, cos, sin, d, e), q_raw, kv_raw, kr, g_qn, g_kn)
    return vjp((dq, dk, dv))


def _post_fn(a, o_f, o_b, hg, g_hgo):
    o = o_f + o_b
    parts = [a]
    for h in range(HG_HEADS):
        s = slice(h * HG_DK, (h + 1) * HG_DK)
        gate = hg[:, s]
        parts.append(_rms(o[:, s], g_hgo[:, s]) * (gate * _sigmoid(gate)))
    return jnp.concatenate(parts, axis=-1)


def _post_bwd_fn(o_f, o_b, hg, dr, g_hgo):
    def f(o, hg, g):
        return _post_fn(jnp.zeros_like(o), o, jnp.zeros_like(o), hg, g)[:, o.shape[1]:]
    _, vjp = jax.vjp(f, o_f + o_b, hg, g_hgo)
    return vjp(dr)


def _swiglu_fn(gt, up):
    return gt * _sigmoid(gt) * up


def _resid_norm_fn(acc, x, g):
    x_new = acc + x
    return x_new, _rms(x_new, g)


def _swiglu_bwd_fn(gt, up, dact):
    _, vjp = jax.vjp(_swiglu_fn, gt, up)
    return vjp(dact)


def _ple_loss_fn(x2, pg, pp, target):
    gate = _sigmoid(pg)
    err = x2 + gate * pp - target
    dx3 = err * (1.0 / err.shape[-1])
    loss = 0.5 * jnp.sum(jnp.mean(err * err, axis=-1, keepdims=True), axis=0, keepdims=True)
    return dx3, dx3 * pp * gate * (1.0 - gate), dx3 * gate, loss


def _attention_fwd(q, k, v):
    H, T, D = q.shape
    DV = v.shape[-1]
    tq, ck = min(ATT_TQ, T), min(ATT_CHUNK, T)
    c2 = (D ** -0.5) * LOG2_E

    def body(q_ref, k_ref, v_ref, o_ref, lse_ref):
        q_i = q_ref[0]

        def chunk(c, carry):
            m, l, acc = carry
            rows = pl.ds(pl.multiple_of(c * ck, ck), ck)
            s = _dot_raw(q_i, k_ref[0, rows, :], "nt")
            m_new = jnp.maximum(m, jnp.max(s, axis=-1, keepdims=True))
            p = jnp.exp2((s - m_new) * c2)
            alpha = jnp.exp2((m - m_new) * c2)
            l = l * alpha + jnp.sum(p, axis=-1, keepdims=True)
            acc = acc * alpha + _dot_raw(p, v_ref[0, rows, :], "nn")
            return m_new, l, acc

        init = (jnp.full((tq, 1), -jnp.inf, F32), jnp.zeros((tq, 1), F32), jnp.zeros((tq, DV), F32))
        m, l, acc = lax.fori_loop(0, T // ck, chunk, init, unroll=True)
        o_ref[...] = acc / l
        lse_ref[0] = m * c2 + jnp.log2(l)

    return pl.pallas_call(
        body, name="attention_fwd", grid=(H, T // tq),
        in_specs=[pl.BlockSpec((1, tq, D), lambda h, i: (h, i, 0)),
                  pl.BlockSpec((1, T, D), lambda h, i: (h, 0, 0)),
                  pl.BlockSpec((1, T, DV), lambda h, i: (h, 0, 0))],
        out_specs=[pl.BlockSpec((tq, DV), lambda h, i: (i, h)),
                   pl.BlockSpec((1, tq, 1), lambda h, i: (h, i, 0))],
        out_shape=[_sds((T, H * DV), F32), _sds((H, T, 1), F32)],
        compiler_params=_params(("parallel", "parallel")),
    )(q, k, v)


def _attention_bwd(q, k, v, o, lse2, dmix):
    H, T, D = q.shape
    DV = v.shape[-1]
    tk, cq = min(ATT_TK, T), min(ATT_CHUNK, T)
    scale = D ** -0.5
    c2 = scale * LOG2_E

    def body(q_ref, k_ref, v_ref, o_ref, lse_ref, do_ref, dq_ref, dk_ref, dv_ref, delta_ref):
        j = pl.program_id(1)

        @pl.when(j == 0)
        def _():
            delta = lax.dot_general(jnp.ones((8, DV), F32), do_ref[...] * o_ref[...], (((1,), (1,)), ((), ())),
                                    precision=lax.Precision.HIGHEST, preferred_element_type=F32)
            for i in range(T // cq):
                delta_ref[i] = delta[:, i * cq:(i + 1) * cq]
            dq_ref[0] = jnp.zeros((T, D), F32)

        k_j, v_j = k_ref[0], v_ref[0]
        dk_ref[0] = jnp.zeros((tk, D), F32)
        dv_ref[0] = jnp.zeros((tk, DV), F32)

        def chunk(c, carry):
            rows = pl.ds(pl.multiple_of(c * cq, cq), cq)
            q_c = q_ref[0, rows, :]
            do_c = do_ref[rows, :].astype(BF16)
            st = _dot_raw(k_j, q_c, "nt")
            pt = jnp.exp2(st * c2 - lse_ref[0, c])
            dv_ref[0] += _dot_raw(pt, do_c, "nn")
            dpt = _dot_raw(v_j, do_c, "nt")
            dst = pt * (dpt - delta_ref[c, 0:1, :]) * scale
            dk_ref[0] += _dot_raw(dst, q_c, "nn")
            dq_ref[0, rows, :] += _dot_raw(dst, k_j, "tn")
            return carry

        lax.fori_loop(0, T // cq, chunk, 0, unroll=True)

    return pl.pallas_call(
        body, name="attention_bwd", grid=(H, T // tk),
        in_specs=[pl.BlockSpec((1, T, D), lambda h, j: (h, 0, 0)),
                  pl.BlockSpec((1, tk, D), lambda h, j: (h, j, 0)),
                  pl.BlockSpec((1, tk, DV), lambda h, j: (h, j, 0)),
                  pl.BlockSpec((T, DV), lambda h, j: (0, h)),
                  pl.BlockSpec((1, T // cq, 1, cq), lambda h, j: (h, 0, 0, 0)),
                  pl.BlockSpec((T, DV), lambda h, j: (0, h))],
        out_specs=[pl.BlockSpec((1, T, D), lambda h, j: (h, 0, 0)),
                   pl.BlockSpec((1, tk, D), lambda h, j: (h, j, 0)),
                   pl.BlockSpec((1, tk, DV), lambda h, j: (h, j, 0))],
        out_shape=[_sds((H, T, D), F32), _sds((H, T, D), F32), _sds((H, T, DV), F32)],
        scratch_shapes=[pltpu.VMEM((T // cq, 8, cq), F32)],
        compiler_params=_params(("parallel", "arbitrary")),
    )(q, k, v, o, lse2.reshape(H, T // cq, 1, cq), dmix)


def _split3_dot(ones, x, kind):
    hi = x.astype(BF16)
    rest = x - hi.astype(F32)
    mid = rest.astype(BF16)
    lo = (rest - mid.astype(F32)).astype(BF16)
    return (_dot_raw(ones, hi, kind) + _dot_raw(ones, mid, kind)) + _dot_raw(ones, lo, kind)


@jax.custom_vjp
def _running_sum(x, tri):
    return _split3_dot(tri, x, "nn")


def _running_sum_fwd(x, tri):
    return _split3_dot(tri, x, "nn"), tri


def _running_sum_bwd(tri, g):
    return _split3_dot(tri, g, "tn"), jnp.zeros_like(tri)


_running_sum.defvjp(_running_sum_fwd, _running_sum_bwd)


def _gla_block(hq, hf, hi, lower, st_in, *, rev, dot):
    rows, dk = hq.shape
    G, C = rows // CHUNK, CHUNK
    q = hq * _sigmoid(hq)
    f = lower + (1.0 - lower) * _sigmoid(hf)
    k = 1.0 - f
    logf = jnp.log2(f)
    q3, k3, v3, lf3 = (t.reshape(G, C, dk) for t in (q, k, hi, logf))
    r = lax.broadcasted_iota(jnp.int32, (C, C), 0)
    c = lax.broadcasted_iota(jnp.int32, (C, C), 1)
    tri = ((r <= c) if rev else (r >= c)).astype(F32)
    b = _running_sum(lf3, jnp.broadcast_to(tri, (G, C, C)))
    tpos = lax.broadcasted_iota(jnp.int32, (1, C, 1), 1)
    first_half = (tpos >= C // 2) if rev else (tpos <= C // 2 - 1)
    b_mid = jnp.sum(jnp.where(first_half, lf3, 0.0), axis=1, keepdims=True)
    b_last = jnp.sum(lf3, axis=1, keepdims=True)
    a = dot(q3 * jnp.exp2(b - b_mid), k3 * jnp.exp2(b_mid - b), "nt") * tri
    o_intra = dot(a, v3, "nn")
    kv_t = dot(v3, k3 * jnp.exp2(b_last - b), "tn")
    decay = jnp.exp2(b_last)
    qd = q3 * jnp.exp2(b)
    st = st_in
    o_inter = [None] * G
    for g in (reversed(range(G)) if rev else range(G)):
        o_inter[g] = dot(qd[g], st, "nt")
        st = st * decay[g] + kv_t[g]
    o = o_intra.reshape(rows, dk) + jnp.concatenate(o_inter, axis=0)
    return o, st


def _gla_fwd(z, lower3, *, rev, col_q, col_f, col_v, hp):
    T = z.shape[0]
    rows = min(GLA_GROUP * CHUNK, T)
    nb = T // rows
    wide = hp * HG_DK
    blk = (lambda n: nb - 1 - n) if rev else (lambda n: n)

    def body(hq_ref, hf_ref, hi_ref, low_ref, o_ref, st_out_ref, st_ref):
        @pl.when(pl.program_id(1) == 0)
        def _():
            st_ref[...] = jnp.zeros_like(st_ref)

        st_in = [st_ref[i] for i in range(hp)]
        heads = []
        for i in range(hp):
            cols = slice(i * HG_DK, (i + 1) * HG_DK)
            heads.append(_gla_block(hq_ref[:, cols], hf_ref[:, cols], hi_ref[:, cols], low_ref[i], st_in[i], rev=rev,
                                    dot=_dot_raw))
        for i, (o, st) in enumerate(heads):
            st_out_ref[i, 0] = st_in[i]
            o_ref[:, i * HG_DK:(i + 1) * HG_DK] = o
            st_ref[i] = st

    def zspec(col):
        return pl.BlockSpec((rows, wide), lambda h, n: (blk(n), col // wide + h))

    return pl.pallas_call(
        body, name="gla_fwd_rev" if rev else "gla_fwd", grid=(HG_HEADS // hp, nb),
        in_specs=[zspec(col_q), zspec(col_f), zspec(col_v), pl.BlockSpec((hp, 1, HG_DK), lambda h, n: (h, 0, 0))],
        out_specs=[pl.BlockSpec((rows, wide), lambda h, n: (blk(n), h)),
                   pl.BlockSpec((hp, 1, HG_DK, HG_DK), lambda h, n: (h, blk(n), 0, 0))],
        out_shape=[_sds((T, HG_HEADS * HG_DK), F32), _sds((HG_HEADS, nb, HG_DK, HG_DK), F32)],
        scratch_shapes=[pltpu.VMEM((hp, HG_DK, HG_DK), F32)],
        compiler_params=_params(("parallel", "arbitrary")),
    )(z, z, z, lower3)


def _gla_bwd(z, lower3, states, do, prev, *, rev, col_q, col_f, col_v, hp):
    T = z.shape[0]
    rows = min(GLA_GROUP * CHUNK, T)
    nb = T // rows
    wide = hp * HG_DK
    blk = (lambda n: n) if rev else (lambda n: nb - 1 - n)
    has_prev = prev is not None
    fn = functools.partial(_gla_block, rev=rev, dot=_bdot)

    def body(*refs):
        hq_ref, hf_ref, hi_ref, low_ref, st_ref, do_ref = refs[:6]
        rest = refs[6:]
        if has_prev:
            pq_ref, pi_ref = rest[:2]
            rest = rest[2:]
        dhq_ref, dhi_ref, dhf_ref, dlow_ref, dst_ref = rest
        n = pl.program_id(1)

        @pl.when(n == 0)
        def _():
            dst_ref[...] = jnp.zeros_like(dst_ref)

        dst_in = [dst_ref[i] for i in range(hp)]
        heads = []
        for i in range(hp):
            cols = slice(i * HG_DK, (i + 1) * HG_DK)
            _, vjp = jax.vjp(fn, hq_ref[:, cols], hf_ref[:, cols], hi_ref[:, cols], low_ref[i], st_ref[i, 0])
            dhq, dhf, dhi, dlow, dst = vjp((do_ref[:, cols], dst_in[i]))
            if has_prev:
                dhq = dhq + pq_ref[:, cols]
                dhi = dhi + pi_ref[:, cols]
            heads.append((dhq, dhf, dhi, dlow, dst))
        for i, (dhq, dhf, dhi, dlow, dst) in enumerate(heads):
            cols = slice(i * HG_DK, (i + 1) * HG_DK)
            dst_ref[i] = dst
            dhq_ref[:, cols] = dhq.astype(dhq_ref.dtype)
            dhi_ref[:, cols] = dhi.astype(dhi_ref.dtype)
            dhf_ref[:, cols] = dhf.astype(dhf_ref.dtype)

        @pl.when(n == 0)
        def _():
            for i in range(hp):
                dlow_ref[i] = heads[i][3]

        @pl.when(n > 0)
        def _():
            for i in range(hp):
                dlow_ref[i] += heads[i][3]

    def zspec(col):
        return pl.BlockSpec((rows, wide), lambda h, n: (blk(n), col // wide + h))

    hspec = pl.BlockSpec((rows, wide), lambda h, n: (blk(n), h))
    in_specs = [zspec(col_q), zspec(col_f), zspec(col_v), pl.BlockSpec((hp, 1, HG_DK), lambda h, n: (h, 0, 0)),
                pl.BlockSpec((hp, 1, HG_DK, HG_DK), lambda h, n: (h, blk(n), 0, 0)), hspec]
    ins = [z, z, z, lower3, states, do]
    if has_prev:
        in_specs += [hspec, hspec]
        ins += list(prev)
    full_wide = HG_HEADS * HG_DK
    acc_dtype = BF16 if has_prev else F32
    return pl.pallas_call(
        body, name="gla_bwd_rev" if rev else "gla_bwd", grid=(HG_HEADS // hp, nb),
        in_specs=in_specs,
        out_specs=[hspec, hspec, hspec, pl.BlockSpec((hp, 1, HG_DK), lambda h, n: (h, 0, 0))],
        out_shape=[_sds((T, full_wide), acc_dtype), _sds((T, full_wide), acc_dtype), _sds((T, full_wide), BF16),
                   _sds((HG_HEADS, 1, HG_DK), F32)],
        scratch_shapes=[pltpu.VMEM((hp, HG_DK, HG_DK), F32)],
        compiler_params=_params(("parallel", "arbitrary")),
    )(*ins)


def _lower_fn(lb):
    e = jnp.exp(lb - jnp.max(lb, axis=0, keepdims=True))
    return (e / jnp.sum(e, axis=0, keepdims=True))[0]


def _lower_bounds(lb):
    def body(lb_ref, o_ref):
        o_ref[...] = _lower_fn(lb_ref[...])
    return pl.pallas_call(body, name="lower_bounds", out_shape=_sds(lb.shape[1:], F32))(lb)


def _row_tile(r, cap=1024):
    best = None
    for t in range(16, min(r, cap) + 1, 16):
        if r % t == 0:
            best = t
    return best if best is not None else r


def _sum4(shards, recv, chip, *, name):
    _, R, C = shards.shape
    tr = _row_tile(R)

    def body(chip_ref, o_ref, r_ref, out_ref):
        out_ref[...] = ((o_ref[0].astype(F32) + r_ref[0].astype(F32)) + r_ref[1].astype(F32)) + r_ref[2].astype(F32)

    grid_spec = pltpu.PrefetchScalarGridSpec(
        num_scalar_prefetch=1, grid=(R // tr,),
        in_specs=[pl.BlockSpec((1, tr, C), lambda i, chip_ref: (chip_ref[0], i, 0)),
                  pl.BlockSpec((3, tr, C), lambda i, chip_ref: (0, i, 0))],
        out_specs=pl.BlockSpec((tr, C), lambda i, chip_ref: (i, 0)))
    return pl.pallas_call(
        body, name=name, grid_spec=grid_spec, out_shape=_sds((R, C), F32), compiler_params=_params(("parallel",)),
    )(chip, shards, recv)


def _adamw_math(w, g, m, v):
    m = ADAM_B1 * m + (1.0 - ADAM_B1) * g
    v = ADAM_B2 * v + (1.0 - ADAM_B2) * (g * g)
    m_hat = m / (1.0 - ADAM_B1 ** ADAM_STEP)
    v_hat = v / (1.0 - ADAM_B2 ** ADAM_STEP)
    delta = -ADAM_LR * (m_hat / (jnp.sqrt(v_hat) + ADAM_EPS) + ADAM_WD * w)
    return delta, m, v


def _adamw(w, g_a, g_b, m, v, *, name):
    R, C = w.shape
    tr = _row_tile(R)
    two = g_b is not None

    def body(*refs):
        w_ref, ga_ref = refs[0], refs[1]
        rest = refs[2:]
        g = ga_ref[...]
        if two:
            g = g + rest[0][...]
            rest = rest[1:]
        m_ref, v_ref, g_out, d_out, m_out, v_out = rest
        delta, m_new, v_new = _adamw_math(w_ref[...], g, m_ref[...], v_ref[...])
        g_out[...] = g
        d_out[...] = delta
        m_out[...] = m_new
        v_out[...] = v_new

    spec = pl.BlockSpec((tr, C), lambda i: (i, 0))
    ins = [w, g_a] + ([g_b] if two else []) + [m, v]
    return pl.pallas_call(
        body, name=name, grid=(R // tr,), in_specs=[spec] * len(ins), out_specs=[spec] * 4,
        out_shape=[_sds((R, C), F32)] * 4, compiler_params=_params(("parallel",)),
    )(*ins)


def _adamw_small(red, lb_full, ws, ms, vs):
    n = len(ws)

    def pieces(shape):
        out = []
        for j, idx in enumerate(itertools.product(*[range(d) for d in shape[:-1]])):
            out.append((idx[:-1] + (slice(idx[-1], idx[-1] + 1), slice(None)), j * shape[-1]))
        return out

    def body(*refs):
        red_ref, lb_ref = refs[0], refs[1]
        w_refs, m_refs, v_refs = refs[2:2 + n], refs[2 + n:2 + 2 * n], refs[2 + 2 * n:2 + 3 * n]
        out_refs = refs[2 + 3 * n:]
        chip = 2 * lax.axis_index("x") + lax.axis_index("y")
        n_f, shard = lb_ref.shape[-1], w_refs[n - 1].shape[-1]
        _, vjp = jax.vjp(_lower_fn, lb_ref[...])
        dlb = vjp(red_ref[8:10, 0:n_f])[0]
        for i in range(n):
            width = w_refs[i].shape[-1]
            for j, (at, lane) in enumerate(pieces(w_refs[i].shape)):
                if i < n - 1:
                    g = red_ref[i:i + 1, lane:lane + width]
                else:
                    row = dlb[j // 2][j % 2:j % 2 + 1]
                    g = sum(jnp.where(chip == q, row[:, q * shard:(q + 1) * shard], 0.0) for q in range(N_CHIPS))
                delta, m_new, v_new = _adamw_math(w_refs[i][at], g, m_refs[i][at], v_refs[i][at])
                for o_ref, val in zip(out_refs[4 * i:4 * i + 4], (g, delta, m_new, v_new)):
                    o_ref[at] = val

    return pl.pallas_call(
        body, name="adamw_small", out_shape=[_sds(w.shape, F32) for w in ws for _ in range(4)],
    )(red, lb_full, *ws, *ms, *vs)


def _chip_peers():
    x, y, c = lax.axis_index("x"), lax.axis_index("y"), lax.axis_index("c")
    return (x, y, c), 2 * x + y, [(1 - x, y), (x, 1 - y), (1 - x, 1 - y)]


_HBM = pl.BlockSpec(memory_space=pltpu.HBM)
_SEM = pl.BlockSpec(memory_space=pltpu.SEMAPHORE)
_EFFECT = pltpu.SideEffectType.DATAFLOW_SIDE_EFFECTING


def _exchange_copies(srcs, lands, sems, mode):
    (x, y, c), me, chips = _chip_peers()
    copies = []
    for t, (src, land) in enumerate(zip(srcs, lands)):
        if mode == "swap":
            copies.append(pltpu.make_async_remote_copy(src, land, sems[0].at[3 * t], sems[1].at[3 * t],
                                                       device_id=(x, y, 1 - c), device_id_type=MESH))
            continue
        for k, (px, py) in enumerate(chips):
            gather = mode == "gather"
            copies.append(pltpu.make_async_remote_copy(
                src if gather else src.at[2 * px + py], land.at[me] if gather else land.at[k],
                sems[0].at[3 * t + k], sems[1].at[3 * t + k], device_id=(px, py, c), device_id_type=MESH))
        if mode == "gather":
            copies.append(pltpu.make_async_copy(src, land.at[me], sems[2].at[t]))
    return copies


def _exchange_start(srcs, *, mode, name, after=None):
    n = len(srcs)
    n_sem = 3 if mode == "gather" else 2
    n_in = 2 * n + (after is not None)
    land_shape = {"gather": lambda s: (N_CHIPS,) + s.shape, "scatter": lambda s: (3,) + s.shape[1:], "swap": lambda s: s.shape}
    lands = [_sds(land_shape[mode](s), s.dtype) for s in srcs]

    def body(*refs):
        for cp in _exchange_copies(refs[:n], refs[n:2 * n], refs[n_in:n_in + n_sem], mode):
            cp.start()
        token = refs[-1]
        token[...] = jnp.zeros_like(token)

    sem_shapes = [pltpu.SemaphoreType.DMA((3 * n,)), pltpu.SemaphoreType.DMA((3 * n,))]
    sem_shapes += [pltpu.SemaphoreType.DMA((n,))] if mode == "gather" else []
    thru = [pltpu.HBM(s.shape, s.dtype) for s in srcs] + [pltpu.HBM(l.shape, l.dtype) for l in lands]
    res = pl.pallas_call(
        body, name=name, in_specs=[_HBM] * (2 * n) + [_ANY] * (after is not None),
        out_specs=[_SEM] * n_sem + [_HBM] * (2 * n) + [pl.BlockSpec(memory_space=pltpu.VMEM)],
        out_shape=sem_shapes + thru + [_sds((8, LANE), F32)], input_output_aliases={i: n_sem + i for i in range(2 * n)},
        compiler_params=pltpu.CompilerParams(has_side_effects=_EFFECT),
    )(*[pltpu.with_memory_space_constraint(s, pltpu.HBM) for s in srcs],
      *[pltpu.with_memory_space_constraint(lax.empty(l.shape, l.dtype), pltpu.HBM) for l in lands],
      *([after] if after is not None else []))
    return (res[:n_sem], res[n_sem:n_sem + n], res[n_sem + n:n_sem + 2 * n], mode), res[-1]


def _exchange_wait(started, after, *, name):
    sems, srcs, lands, mode = started
    n, n_sem = len(srcs), len(sems)
    after = list(after) if isinstance(after, (list, tuple)) else [after]

    def body(*refs):
        for cp in _exchange_copies(refs[:n], refs[n:2 * n], refs[2 * n:2 * n + n_sem], mode):
            cp.wait()

    res = pl.pallas_call(
        body, name=name, in_specs=[_HBM] * (2 * n) + [_SEM] * n_sem + [_ANY] * len(after), out_specs=[_HBM] * (2 * n),
        out_shape=[pltpu.HBM(a.shape, a.dtype) for a in list(srcs) + list(lands)],
        input_output_aliases={i: i for i in range(2 * n)},
        compiler_params=pltpu.CompilerParams(has_side_effects=_EFFECT),
    )(*srcs, *lands, *sems, *after)
    return res[:n], res[n:]


def _allreduce_small(pack, after):
    R, C = pack.shape

    def body(in_ref, after_ref, out_ref, slots, send_sems, recv_sems):
        x, y, c = lax.axis_index("x"), lax.axis_index("y"), lax.axis_index("c")
        me = 4 * x + 2 * y + c
        slots[me] = in_ref[...]
        copies = []
        for k in range(1, N_DEV):
            peer = (x ^ ((k >> 2) & 1), y ^ ((k >> 1) & 1), c ^ (k & 1))
            cp = pltpu.make_async_remote_copy(in_ref, slots.at[me], send_sems.at[k - 1], recv_sems.at[k - 1],
                                              device_id=peer, device_id_type=MESH)
            cp.start()
            copies.append(cp)
        for cp in copies:
            cp.wait()
        acc = slots[0]
        for d in range(1, N_DEV):
            acc = acc + slots[d]
        out_ref[...] = acc

    return pl.pallas_call(
        body, name="allreduce_small", out_shape=_sds((R, C), F32),
        in_specs=[pl.BlockSpec(memory_space=pltpu.VMEM), _ANY], out_specs=pl.BlockSpec(memory_space=pltpu.VMEM),
        scratch_shapes=[pltpu.VMEM((N_DEV, R, C), F32), pltpu.SemaphoreType.DMA((N_DEV - 1,)),
                        pltpu.SemaphoreType.DMA((N_DEV - 1,))],
        compiler_params=_params(),
    )(pack, after)


_Z_CQ, _Z_CKV, _Z_HQ, _Z_HFF, _Z_HFB, _Z_HI, _Z_HG, _Z_KR, _Z_END = 0, 256, 512, 1024, 1536, 2048, 2560, 3072, 3200


def _to_z_layout(wt):
    pad = jnp.zeros((_Z_END - _Z_KR - QK_ROPE, wt.shape[1]), wt.dtype)
    return jnp.concatenate([wt[:512], wt[512 + QK_ROPE:], wt[512:512 + QK_ROPE], pad], axis=0)


def _from_z_layout(wt):
    return jnp.concatenate([wt[:512], wt[_Z_KR:_Z_KR + QK_ROPE], wt[512:_Z_KR]], axis=0)


def _col_shards_to_full(g):
    return jnp.transpose(g, (1, 0, 2)).reshape(g.shape[1], -1)


def _full_to_col_shards(w):
    r, c = w.shape
    return jnp.transpose(w.reshape(r, N_CHIPS, c // N_CHIPS), (1, 0, 2))


def _full_to_row_shards(w):
    r, c = w.shape
    return w.reshape(N_CHIPS, r // N_CHIPS, c)


def kernel(x, p, positions, g_mix, w_in, g_qa, g_kva, w_qb, w_kvb, g_qn, g_kn, lb_param, g_hgo, w_o, g_ffn, w_gate, w_up, w_down, g_ple, w_ple_gate, w_ple_proj, loss_target, m_g_mix, m_w_in, m_g_qa, m_g_kva, m_w_qb, m_w_kvb, m_g_qn, m_g_kn, m_lb_param, m_g_hgo, m_w_o, m_g_ffn, m_w_gate, m_w_up, m_w_down, m_g_ple, m_w_ple_gate, m_w_ple_proj, v_g_mix, v_w_in, v_g_qa, v_g_kva, v_w_qb, v_w_kvb, v_g_qn, v_g_kn, v_lb_param, v_g_hgo, v_w_o, v_g_ffn, v_w_gate, v_w_up, v_w_down, v_g_ple, v_w_ple_gate, v_w_ple_proj):
    w_named = dict(g_mix=g_mix, w_in=w_in, g_qa=g_qa, g_kva=g_kva, w_qb=w_qb, w_kvb=w_kvb, g_qn=g_qn, g_kn=g_kn,
                   lb_param=lb_param, g_hgo=g_hgo, w_o=w_o, g_ffn=g_ffn, w_gate=w_gate, w_up=w_up, w_down=w_down,
                   g_ple=g_ple, w_ple_gate=w_ple_gate, w_ple_proj=w_ple_proj)
    m_named = dict(g_mix=m_g_mix, w_in=m_w_in, g_qa=m_g_qa, g_kva=m_g_kva, w_qb=m_w_qb, w_kvb=m_w_kvb, g_qn=m_g_qn,
                   g_kn=m_g_kn, lb_param=m_lb_param, g_hgo=m_g_hgo, w_o=m_w_o, g_ffn=m_g_ffn, w_gate=m_w_gate,
                   w_up=m_w_up, w_down=m_w_down, g_ple=m_g_ple, w_ple_gate=m_w_ple_gate, w_ple_proj=m_w_ple_proj)
    v_named = dict(g_mix=v_g_mix, w_in=v_w_in, g_qa=v_g_qa, g_kva=v_g_kva, w_qb=v_w_qb, w_kvb=v_w_kvb, g_qn=v_g_qn,
                   g_kn=v_g_kn, lb_param=v_lb_param, g_hgo=v_g_hgo, w_o=v_w_o, g_ffn=v_g_ffn, w_gate=v_w_gate,
                   w_up=v_w_up, w_down=v_w_down, g_ple=v_g_ple, w_ple_gate=v_w_ple_gate, w_ple_proj=v_w_ple_proj)
    order = list(w_named)
    transposed = ("w_in", "w_qb", "w_gate", "w_up")
    col_sharded = ("w_kvb", "w_ple_proj")
    row_sharded = ("w_o", "w_down", "w_ple_gate")
    big = transposed + col_sharded + row_sharded

    def view(n, a):
        return jnp.transpose(a[0]) if n in transposed else a[0]

    def unview(n, a):
        return (jnp.transpose(a) if n in transposed else a)[None]

    def to_shards(n, g):
        return _full_to_col_shards(g) if n in col_sharded else _full_to_row_shards(g)

    x2d, p2d, tgt = x[0], p[0, 0], loss_target[0]
    T, D = x2d.shape

    lb_flat = lb_param.reshape(-1, lb_param.shape[-1])
    gather_groups = (("w_in",), ("w_qb", "w_kvb"), ("w_o", "w_gate", "w_up", "w_down", "w_ple_gate", "w_ple_proj"))
    gather_started = []

    casts = {n: view(n, w_named[n]).astype(BF16) for n in big}

    def gather_start(gi, after):
        srcs = [casts[n] for n in gather_groups[gi]] + ([lb_flat] if gi == 0 else [])
        started, token = _exchange_start(srcs, mode="gather", name=f"gather_start_{gi}", after=after)
        gather_started.append(started)
        return token

    full = {}

    def gather_wait(gi, after):
        _, got = _exchange_wait(gather_started[gi], after, name=f"gather_wait_{gi}")
        for n, g in zip(gather_groups[gi], got):
            full[n] = _col_shards_to_full(g) if n in col_sharded else g.reshape(-1, g.shape[-1])
        return got

    g_hgo_row = g_hgo.reshape(1, -1)

    inv_freq = ROPE_THETA ** (-jnp.arange(0, QK_ROPE, 2, dtype=F32) / QK_ROPE)
    ang = positions[0].astype(F32)[:, None] * inv_freq
    cos, sin = jnp.cos(ang), jnp.sin(ang)
    token = gather_start(0, None)
    h1 = _stage(_rms, [x2d], [g_mix], [_sds((T, D), BF16)], [], name="norm_mix", after=token)[0]
    got = gather_wait(0, [h1, cos, sin] + [casts[n] for g in gather_groups[1:] for n in g])
    token = got[0]
    for gi in range(1, len(gather_groups)):
        token = gather_start(gi, token)
    lb_full = _col_shards_to_full(got[-1]).reshape(lb_param.shape[0], lb_param.shape[1], -1)
    w_in_zt = _to_z_layout(full["w_in"])
    z = _mm(h1, w_in_zt, tb=True, name="in_proj", after=token)
    qn, kvn = _stage(_mla_a_fn, [_cols(z, 256, 0), _cols(z, 256, 1)], [g_qa, g_kva],
                     [_sds((T, 256), BF16), _sds((T, 256), BF16)], [], name="mla_latent_norm")
    gather_wait(1, qn)
    q_raw = _mm(qn, full["w_qb"], tb=True, name="q_up")
    kv_raw = _mm(kvn, full["w_kvb"], name="kv_up")
    kr = _cols(z, LANE, _Z_KR // LANE)
    q, k, v = _stage(_mla_b_fn, [q_raw, kv_raw, kr, cos, sin], [g_qn, g_kn],
                     [_sds((MLA_HEADS, T, QK_HEAD), BF16), _sds((MLA_HEADS, T, QK_HEAD), BF16),
                      _sds((MLA_HEADS, T, V_HEAD), BF16)], [], name="mla_qk_norm_rope")
    att, lse = _attention_fwd(q, k, v)

    lower = _lower_bounds(lb_full)
    lower3 = lower.reshape(2, HG_HEADS, 1, HG_DK)
    o_f, st_f = _gla_fwd(z, lower3[0], rev=False, col_q=_Z_HQ, col_f=_Z_HFF, col_v=_Z_HI, hp=GLA_FWD_HEADS)
    o_b, st_b = _gla_fwd(z, lower3[1], rev=True, col_q=_Z_HQ, col_f=_Z_HFB, col_v=_Z_HI, hp=GLA_FWD_HEADS)
    hg = _cols(z, 512, _Z_HG // 512)
    mix = _stage(_post_fn, [att, o_f, o_b, hg], [g_hgo_row], [_sds((T, att.shape[1] + o_f.shape[1]), BF16)], [],
                 name="mix_out")[0]
    gather_wait(2, mix)
    x1, h2 = _mm_fused(mix, full["w_o"], _resid_norm_fn, [x2d], [g_ffn], [F32, BF16], [], full_rows=True,
                       name="out_proj")
    gt, up, act = _mm_fused(h2, full["w_gate"], lambda gt, up: (gt, up, _swiglu_fn(gt, up)), [], [], [BF16, BF16, BF16],
                            [], tb=True, b2=full["w_up"], name="ffn_gate_up")
    x2, h3 = _mm_fused(act, full["w_down"], _resid_norm_fn, [x1], [g_ple], [F32, BF16], [], full_rows=True,
                       name="ffn_down")
    pp = _mm(p2d, full["w_ple_proj"], name="ple_proj")
    dx3, dpg, dpp, loss_part = _mm_fused(
        h3, full["w_ple_gate"], lambda acc, pp, x2, tgt: _ple_loss_fn(x2, acc, pp, tgt), [pp, x2, tgt], [],
        [F32, BF16, BF16], [_sds((1, 1), F32)], full_rows=True, name="ple_gate_loss")

    grads = {}
    scatter_groups = (("w_ple_proj", "w_ple_gate", "w_down", "w_gate", "w_up", "w_o"), ("w_qb", "w_kvb", "w_in"))
    scatter_started = []

    def scatter_start(gi):
        srcs = [to_shards(n, grads[n]) for n in scatter_groups[gi]]
        started, token = _exchange_start(srcs, mode="scatter", name=f"scatter_start_{gi}")
        scatter_started.append(started)
        return token

    chip = 2 * lax.axis_index("x") + lax.axis_index("y")
    swap_started = []

    def reduce_group(gi, after):
        shards, recvs = _exchange_wait(scatter_started[gi], after, name=f"scatter_wait_{gi}")
        sums = [_sum4(s, r, chip.reshape(1), name="sum_" + n) for n, s, r in zip(scatter_groups[gi], shards, recvs)]
        started, token = _exchange_start(sums, mode="swap", name=f"swap_start_{gi}")
        swap_started.append(started)
        return token

    grads["w_ple_proj"] = _mm(p2d, dpp, ta=True, out_dtype=BF16, name="d_w_ple_proj")
    grads["w_ple_gate"] = _mm(h3, dpg, ta=True, out_dtype=BF16, name="d_w_ple_gate")
    dx2, grads["g_ple"] = _mm_fused(
        dpg, full["w_ple_gate"], lambda acc, x2, dx3, g: _norm_bwd_fn(x2, acc, dx3, g), [x2, dx3], [g_ple],
        [F32], [_sds((1, D), F32)], tb=True, full_rows=True, name="d_h3_norm_ple_bwd")
    dgt, dup = _mm_fused(dx2, full["w_down"], lambda acc, gt, up: _swiglu_bwd_fn(gt.astype(F32), up.astype(F32), acc), [gt, up], [],
                         [BF16, BF16], [], tb=True, name="d_act_swiglu_bwd")
    grads["w_down"] = _mm(act, dx2, ta=True, out_dtype=BF16, name="d_w_down")
    grads["w_gate"] = _mm(dgt, h2, ta=True, out_dtype=BF16, name="d_w_gate")
    grads["w_up"] = _mm(dup, h2, ta=True, out_dtype=BF16, name="d_w_up")
    dx1, grads["g_ffn"] = _mm_fused(
        dgt, full["w_gate"], lambda acc, x1, dx2, g: _norm_bwd_fn(x1, acc, dx2, g), [x1, dx2], [g_ffn],
        [F32], [_sds((1, D), F32)], full_rows=True, a2=dup, b2=full["w_up"], name="d_h2_norm_ffn_bwd")
    grads["w_o"] = _mm(mix, dx1, ta=True, out_dtype=BF16, name="d_w_o")
    token = scatter_start(0)
    dmix = _mm(dx1, full["w_o"], tb=True, name="d_mix", after=token)

    half = MLA_HEADS * V_HEAD
    do, dhg, dg_hgo = _stage(_post_bwd_fn, [o_f, o_b, hg, _cols(dmix, half, 1)], [g_hgo_row],
                             [_sds((T, half), F32), _sds((T, half), BF16)], [_sds((1, half), F32)], name="mix_out_bwd")
    grads["g_hgo"] = dg_hgo
    dhq_f, dhi_f, dhf_f, dlow_f = _gla_bwd(z, lower3[0], st_f, do, None, rev=False,
                                           col_q=_Z_HQ, col_f=_Z_HFF, col_v=_Z_HI, hp=GLA_BWD_HEADS)
    dhq, dhi, dhf_b, dlow_b = _gla_bwd(z, lower3[1], st_b, do, (dhq_f, dhi_f), rev=True,
                                       col_q=_Z_HQ, col_f=_Z_HFB, col_v=_Z_HI, hp=GLA_BWD_HEADS)

    dq, dk, dv = _attention_bwd(q, k, v, att, lse, dmix)
    dq_raw, dkv_raw, dkr, grads["g_qn"], grads["g_kn"] = _stage(
        _mla_b_bwd_fn, [q_raw, kv_raw, kr, cos, sin, dq, dk, dv], [g_qn, g_kn],
        [_sds(q_raw.shape, BF16), _sds(kv_raw.shape, BF16), _sds((T, LANE), BF16)],
        [_sds(g_qn.shape, F32), _sds(g_kn.shape, F32)], name="mla_qk_norm_rope_bwd")
    grads["w_qb"] = _mm(dq_raw, qn, ta=True, out_dtype=BF16, name="d_w_qb")
    grads["w_kvb"] = _mm(kvn, dkv_raw, ta=True, out_dtype=BF16, name="d_w_kvb")
    dqn = _mm(dq_raw, full["w_qb"], name="d_qn")
    dkvn = _mm(dkv_raw, full["w_kvb"], tb=True, name="d_kvn")
    dcq, dckv, grads["g_qa"], grads["g_kva"] = _stage(
        _mla_a_bwd_fn, [_cols(z, 256, 0), _cols(z, 256, 1), dqn, dkvn], [g_qa, g_kva],
        [_sds((T, 256), BF16), _sds((T, 256), BF16)], [_sds(g_qa.shape, F32), _sds(g_kva.shape, F32)],
        name="mla_latent_norm_bwd")
    token = reduce_group(0, dcq)
    dz = jnp.concatenate([dcq, dckv, dhq, dhf_f, dhf_b, dhi, dhg, dkr], axis=1)
    grads["w_in"] = _from_z_layout(_mm(dz, h1, ta=True, out_dtype=BF16, name="d_w_in", after=token))
    token = scatter_start(1)
    grad_x, grads["g_mix"] = _mm_fused(
        dz, w_in_zt, lambda acc, x, dx1, g: _norm_bwd_fn(x, acc, dx1, g), [x2d, dx1], [g_mix],
        [F32], [_sds((1, D), F32)], full_rows=True, name="d_h1_norm_mix_bwd", after=token)

    out_g, out_d, out_m, out_v = {}, {}, {}, {}

    def update_group(gi, after):
        mine, theirs = _exchange_wait(swap_started[gi], after, name=f"swap_wait_{gi}")
        for n, a, b in zip(scatter_groups[gi], mine, theirs):
            out_g[n], out_d[n], out_m[n], out_v[n] = (
                unview(n, t) for t in _adamw(view(n, w_named[n]), a, b, view(n, m_named[n]), view(n, v_named[n]),
                                             name="adamw_" + n))
        return [out_v[n] for n in scatter_groups[gi]]

    done = update_group(0, grad_x)
    token = reduce_group(1, done)
    done = update_group(1, token)

    small = ("g_mix", "g_qa", "g_kva", "g_qn", "g_kn", "g_hgo", "g_ffn", "g_ple")
    small_all = small + ("lb_param",)
    width = -(-max(w_named[n].size for n in small_all) // LANE) * LANE

    def row(a):
        a = a.reshape(1, -1)
        return jnp.pad(a, ((0, 0), (0, width - a.shape[1])))

    dlower = jnp.concatenate([dlow_f.reshape(1, -1), dlow_b.reshape(1, -1)], axis=0)
    pack = jnp.concatenate([row(grads[n]) for n in small] + [row(dlower[0]), row(dlower[1]), row(loss_part)]
                           + [jnp.zeros((5, width), F32)], axis=0)
    red = _allreduce_small(pack, done[-1])
    loss = red[10, 0]

    outs = _adamw_small(red, lb_full, [w_named[n] for n in small_all], [m_named[n] for n in small_all],
                        [v_named[n] for n in small_all])
    for i, n in enumerate(small_all):
        out_g[n], out_d[n], out_m[n], out_v[n] = outs[4 * i:4 * i + 4]

    return (loss, grad_x[None], *[out_g[n] for n in order], *[out_d[n] for n in order],
            *[out_m[n] for n in order], *[out_v[n] for n in order])
```

```python
import functools
import itertools

import jax
import jax.numpy as jnp
from jax import lax
from jax.experimental import pallas as pl
from jax.experimental.pallas import tpu as pltpu

F32 = jnp.float32
BF16 = jnp.bfloat16
MESH = pl.DeviceIdType.MESH

EPS = 1e-6
ROPE_THETA = 10000.0
MLA_HEADS = 4
QK_NOPE = 128
QK_ROPE = 64
QK_HEAD = QK_NOPE + QK_ROPE
V_HEAD = 128
HG_HEADS = 4
HG_DK = 128
CHUNK = 64
ADAM_LR = 0.001
ADAM_B1 = 0.9
ADAM_B2 = 0.999
ADAM_EPS = 1e-08
ADAM_WD = 0.01
ADAM_STEP = 10

LANE = 128
VMEM_LIMIT = 56 * 1024 * 1024
TOK_TILE = 256
GLA_GROUP = 16
GLA_FWD_HEADS = 4
GLA_BWD_HEADS = 2
ATT_TQ = 1024
ATT_TK = 1024
ATT_CHUNK = 512
LOG2_E = 1.4426950408889634
N_CHIPS = 4
N_DEV = 8


_ANY = pl.BlockSpec(memory_space=pl.ANY)


def _params(dims=None, **kw):
    return pltpu.CompilerParams(dimension_semantics=dims, vmem_limit_bytes=VMEM_LIMIT, **kw)


def _tile_candidates(n, cap):
    out = [d for d in range(LANE, min(n, cap) + 1, LANE) if n % d == 0]
    if n <= cap and n not in out:
        out.append(n)
    return out or [n]


MM_VMEM_BUDGET = 40 * 1024 * 1024
MM_MIN_ROWS = 256
MM_MAX_ROWS = 1536
HBM_BYTES_PER_S = 2.8e12
MXU_FLOPS_PER_S = 8e14
STEP_OVERHEAD_S = 0.35e-6


def _mm_tiles(M, N, K, a_bytes, b_bytes, o_bytes, has_add, full_rows=False, full_k=False, n_a=1, n_b=1):
    cast_a, cast_b = a_bytes > 2, b_bytes > 2
    a_bytes, b_bytes = n_a * a_bytes, n_b * b_bytes
    best = None
    for tm in [t for t in _tile_candidates(M, MM_MAX_ROWS) if t >= min(M, MM_MIN_ROWS)]:
        for tn in ([N] if full_rows else _tile_candidates(N, N)):
            for tk in ([K] if full_k else _tile_candidates(K, K)):
                ni, nj, nk = M // tm, N // tn, K // tk
                vmem = 2 * (tm * tk * a_bytes + tk * tn * b_bytes + tm * tn * o_bytes * (2 if has_add else 1))
                vmem += tm * tn * 4 * (2 if nk > 1 else 1)
                vmem += (tm * tk * 2 * n_a if cast_a else 0) + (tk * tn * 2 * n_b if cast_b else 0)
                if vmem > MM_VMEM_BUDGET:
                    continue
                moved = M * K * a_bytes * (nj if nk > 1 else 1) + K * N * b_bytes * (1 if nj == nk == 1 else ni)
                moved += M * N * o_bytes * (2 if has_add else 1)
                t = max(moved / HBM_BYTES_PER_S, 2 * M * N * K / MXU_FLOPS_PER_S) + ni * nj * nk * STEP_OVERHEAD_S
                if best is None or t < best[0]:
                    best = (t, tm, tn, tk)
    assert best is not None, (M, N, K)
    return best[1:]


def _dot_raw(a, b, kind):
    nb = a.ndim - 2
    batch = ((0,), (0,)) if nb else ((), ())
    ca = nb if kind == "tn" else nb + 1
    cb = nb + 1 if kind == "nt" else nb
    return lax.dot_general(a.astype(BF16), b.astype(BF16), (((ca,), (cb,)), batch), preferred_element_type=F32)


@functools.partial(jax.custom_vjp, nondiff_argnums=(2,))
def _bdot(a, b, kind):
    return _dot_raw(a, b, kind)


def _bdot_fwd(a, b, kind):
    return _dot_raw(a, b, kind), (a, b)


def _bdot_bwd(kind, res, g):
    a, b = res
    if kind == "nn":
        da, db = _bdot(g, b, "nt"), _bdot(a, g, "tn")
    elif kind == "nt":
        da, db = _bdot(g, b, "nn"), _bdot(g, a, "tn")
    else:
        da, db = _bdot(b, g, "nt"), _bdot(a, g, "nn")
    return da.astype(a.dtype), db.astype(b.dtype)


_bdot.defvjp(_bdot_fwd, _bdot_bwd)


def _mm(a, b, *, name, ta=False, tb=False, add=None, out_dtype=F32, after=None):
    K, M = a.shape if ta else a.shape[::-1]
    N, Kb = b.shape if tb else b.shape[::-1]
    assert K == Kb, (a.shape, b.shape, ta, tb)
    tm, tn, tk = _mm_tiles(M, N, K, a.dtype.itemsize, b.dtype.itemsize, jnp.dtype(out_dtype).itemsize, add is not None)
    nk = K // tk
    kind = "tn" if ta else ("nt" if tb else "nn")
    assert not (ta and tb)
    a_spec = pl.BlockSpec((tk, tm), lambda i, j, k: (k, i)) if ta else pl.BlockSpec((tm, tk), lambda i, j, k: (i, k))
    b_spec = pl.BlockSpec((tn, tk), lambda i, j, k: (j, k)) if tb else pl.BlockSpec((tk, tn), lambda i, j, k: (k, j))
    o_spec = pl.BlockSpec((tm, tn), lambda i, j, k: (i, j))
    has_add = add is not None

    def body(*refs):
        a_ref, b_ref = refs[0], refs[1]
        add_ref = refs[2] if has_add else None
        o_ref = refs[n_in]
        part = _dot_raw(a_ref[...], b_ref[...], kind)
        if nk == 1:
            if has_add:
                part = part + add_ref[...].astype(F32)
            o_ref[...] = part.astype(o_ref.dtype)
            return
        acc_ref = refs[-1]
        k = pl.program_id(2)

        @pl.when(k == 0)
        def _():
            acc_ref[...] = part

        @pl.when(k > 0)
        def _():
            acc_ref[...] += part

        @pl.when(k == nk - 1)
        def _():
            r = acc_ref[...]
            if has_add:
                r = r + add_ref[...].astype(F32)
            o_ref[...] = r.astype(o_ref.dtype)

    ins = [a, b] + ([add] if has_add else []) + ([after] if after is not None else [])
    in_specs = [a_spec, b_spec] + ([o_spec] if has_add else []) + ([_ANY] if after is not None else [])
    n_in = len(ins)
    return pl.pallas_call(
        body, name=name, grid=(M // tm, N // tn, nk), in_specs=in_specs, out_specs=o_spec,
        out_shape=jax.ShapeDtypeStruct((M, N), out_dtype),
        scratch_shapes=[pltpu.VMEM((tm, tn), F32)] if nk > 1 else [],
        compiler_params=_params(("parallel", "parallel", "arbitrary")),
    )(*ins)


def _mm_fused(a, b, fn, tiles, params, out_dtypes, sums, *, name, ta=False, tb=False, full_rows=False, after=None,
              b2=None, a2=None):
    K, M = a.shape if ta else a.shape[::-1]
    N, Kb = b.shape if tb else b.shape[::-1]
    assert K == Kb and not (ta and tb), (a.shape, b.shape, ta, tb)
    per_elem = sum(t.dtype.itemsize for t in tiles) + sum(jnp.dtype(d).itemsize for d in out_dtypes)
    n_b = 1 if b2 is None else 2
    n_a = 1 if a2 is None else 2
    tm, tn, tk = _mm_tiles(M, N, K, a.dtype.itemsize, b.dtype.itemsize, per_elem, False, full_rows, b2 is not None,
                           n_a, n_b)
    nk = K // tk
    kind = "tn" if ta else ("nt" if tb else "nn")
    a_spec = pl.BlockSpec((tk, tm), lambda i, j, k: (k, i)) if ta else pl.BlockSpec((tm, tk), lambda i, j, k: (i, k))
    b_spec = pl.BlockSpec((tn, tk), lambda i, j, k: (j, k)) if tb else pl.BlockSpec((tk, tn), lambda i, j, k: (k, j))
    o_spec = pl.BlockSpec((tm, tn), lambda i, j, k: (i, j))
    ins = [a, b] + ([b2] if b2 is not None else []) + ([a2] if a2 is not None else [])
    ins += list(tiles) + list(params) + ([after] if after is not None else [])
    in_specs = [a_spec] + [b_spec] * n_b + [a_spec] * (n_a - 1) + [o_spec] * len(tiles)
    in_specs += [pl.BlockSpec(p.shape, lambda i, j, k, nd=p.ndim: (0,) * nd) for p in params]
    in_specs += [_ANY] if after is not None else []
    n_in, n_t, n_p, n_o = len(ins), len(tiles), len(params), len(out_dtypes)

    def body(*refs):
        outs, sum_refs = refs[n_in:n_in + n_o], refs[n_in + n_o:n_in + n_o + len(sums)]

        def finish(*products):
            res = fn(*products, *[t[...] for t in refs[n_a + n_b:n_a + n_b + n_t + n_p]])
            for o_ref, v in zip(outs, res[:n_o]):
                o_ref[...] = v.astype(o_ref.dtype)
            first = jnp.logical_and(pl.program_id(0) == 0, pl.program_id(1) == 0)
            for s_ref, v in zip(sum_refs, res[n_o:]):
                @pl.when(first)
                def _(s_ref=s_ref, v=v):
                    s_ref[...] = v

                @pl.when(jnp.logical_not(first))
                def _(s_ref=s_ref, v=v):
                    s_ref[...] += v

        part = _dot_raw(refs[0][...], refs[1][...], kind)
        if nk == 1 and a2 is not None:
            finish(part + _dot_raw(refs[3][...], refs[2][...], kind))
            return
        if nk == 1:
            finish(part, *([_dot_raw(refs[0][...], refs[2][...], kind)] if b2 is not None else []))
            return
        acc_ref = refs[-1]
        k = pl.program_id(2)

        @pl.when(k == 0)
        def _():
            acc_ref[...] = part

        @pl.when(k > 0)
        def _():
            acc_ref[...] += part

        @pl.when(k == nk - 1)
        def _():
            finish(acc_ref[...])

    out_shape = [_sds((M, N), d) for d in out_dtypes] + list(sums)
    out_specs = [o_spec] * n_o + [pl.BlockSpec(s.shape, lambda i, j, k, nd=len(s.shape): (0,) * nd) for s in sums]
    order = ("arbitrary",) * 3 if sums else ("parallel", "parallel", "arbitrary")
    return pl.pallas_call(
        body, name=name, grid=(M // tm, N // tn, nk), in_specs=in_specs, out_specs=out_specs, out_shape=out_shape,
        scratch_shapes=[pltpu.VMEM((tm, tn), F32)] if nk > 1 else [], compiler_params=_params(order),
    )(*ins)


def _cols(arr, width, block):
    return (arr, width, block)


def _stage(fn, tiles, params, out_tiles, out_sums, *, name, tile=TOK_TILE, after=None):
    def tok_spec(shape, width=None, block=0):
        if len(shape) == 2:
            w = shape[1] if width is None else width
            return pl.BlockSpec((tile, w), lambda i: (i, block))
        return pl.BlockSpec((shape[0], tile, shape[2]), lambda i: (0, i, 0))

    arrays, in_specs = [], []
    for t in tiles:
        if isinstance(t, tuple):
            arr, width, block = t
            arrays.append(arr)
            in_specs.append(tok_spec(arr.shape, width, block))
        else:
            arrays.append(t)
            in_specs.append(tok_spec(t.shape))
    n_tok = arrays[0].shape[0] if arrays[0].ndim == 2 else arrays[0].shape[1]
    for p in params:
        arrays.append(p)
        in_specs.append(pl.BlockSpec(p.shape, lambda i, nd=p.ndim: (0,) * nd))
    out_shape = list(out_tiles) + list(out_sums)
    out_specs = [tok_spec(o.shape) for o in out_tiles]
    out_specs += [pl.BlockSpec(o.shape, lambda i, nd=len(o.shape): (0,) * nd) for o in out_sums]
    n_fn, n_ot = len(arrays), len(out_tiles)
    if after is not None:
        arrays.append(after)
        in_specs.append(_ANY)
    n_in = len(arrays)

    def body(*refs):
        res = fn(*[r[...] for r in refs[:n_fn]])
        if not isinstance(res, (tuple, list)):
            res = (res,)
        outs = refs[n_in:]
        for o_ref, r in zip(outs[:n_ot], res[:n_ot]):
            o_ref[...] = r.astype(o_ref.dtype)
        i = pl.program_id(0)
        for o_ref, r in zip(outs[n_ot:], res[n_ot:]):
            @pl.when(i == 0)
            def _(o_ref=o_ref, r=r):
                o_ref[...] = r.astype(o_ref.dtype)

            @pl.when(i > 0)
            def _(o_ref=o_ref, r=r):
                o_ref[...] += r.astype(o_ref.dtype)

    res = pl.pallas_call(
        body, name=name, grid=(n_tok // tile,), in_specs=in_specs, out_specs=out_specs, out_shape=out_shape,
        compiler_params=_params(("arbitrary",)),
    )(*arrays)
    return res


def _sds(shape, dtype):
    return jax.ShapeDtypeStruct(tuple(shape), dtype)


def _sigmoid(x):
    return 0.5 * jnp.tanh(0.5 * x) + 0.5


def _rms(x, g):
    return x * lax.rsqrt(jnp.mean(x * x, axis=-1, keepdims=True) + EPS) * g


def _norm_bwd_fn(x, dh, dres, g):
    _, vjp = jax.vjp(_rms, x, g)
    dx, dg = vjp(dh)
    return dx + dres, dg


def _mla_a_fn(cq, ckv, g_qa, g_kva):
    return _rms(cq, g_qa), _rms(ckv, g_kva)


def _mla_a_bwd_fn(cq, ckv, dqn, dkvn, g_qa, g_kva):
    _, vjp = jax.vjp(_mla_a_fn, cq, ckv, g_qa, g_kva)
    return vjp((dqn, dkvn))


def _rope(t, cos, sin):
    t1, t2 = t[:, :QK_ROPE // 2], t[:, QK_ROPE // 2:]
    return jnp.concatenate([t1 * cos - t2 * sin, t1 * sin + t2 * cos], axis=-1)


def _mla_b_fn(q_raw, kv_raw, kr, cos, sin, g_qn, g_kn):
    krope = kr[:, :QK_ROPE]
    qs, ks, vs = [], [], []
    for h in range(MLA_HEADS):
        qh = _rms(q_raw[:, h * QK_HEAD:(h + 1) * QK_HEAD], g_qn)
        kvh = kv_raw[:, h * (QK_NOPE + V_HEAD):(h + 1) * (QK_NOPE + V_HEAD)]
        kh = _rms(jnp.concatenate([kvh[:, :QK_NOPE], krope], axis=-1), g_kn)
        qs.append(jnp.concatenate([qh[:, :QK_NOPE], _rope(qh[:, QK_NOPE:], cos, sin)], axis=-1))
        ks.append(jnp.concatenate([kh[:, :QK_NOPE], _rope(kh[:, QK_NOPE:], cos, sin)], axis=-1))
        vs.append(kvh[:, QK_NOPE:])
    return jnp.stack(qs), jnp.stack(ks), jnp.stack(vs)


def _mla_b_bwd_fn(q_raw, kv_raw, kr, cos, sin, dq, dk, dv, g_qn, g_kn):
    _, vjp = jax.vjp(lambda a, b, c, d, e: _mla_b_fn(a, b, c, cos, sin, d, e), q_raw, kv_raw, kr, g_qn, g_kn)
    return vjp((dq, dk, dv))


def _post_fn(a, o_f, o_b, hg, g_hgo):
    o = o_f + o_b
    parts = [a]
    for h in range(HG_HEADS):
        s = slice(h * HG_DK, (h + 1) * HG_DK)
        gate = hg[:, s]
        parts.append(_rms(o[:, s], g_hgo[:, s]) * (gate * _sigmoid(gate)))
    return jnp.concatenate(parts, axis=-1)


def _post_bwd_fn(o_f, o_b, hg, dr, g_hgo):
    def f(o, hg, g):
        return _post_fn(jnp.zeros_like(o), o, jnp.zeros_like(o), hg, g)[:, o.shape[1]:]
    _, vjp = jax.vjp(f, o_f + o_b, hg, g_hgo)
    return vjp(dr)


def _swiglu_fn(gt, up):
    return gt * _sigmoid(gt) * up


def _resid_norm_fn(acc, x, g):
    x_new = acc + x
    return x_new, _rms(x_new, g)


def _swiglu_bwd_fn(gt, up, dact):
    _, vjp = jax.vjp(_swiglu_fn, gt, up)
    return vjp(dact)


def _ple_loss_fn(x2, pg, pp, target):
    gate = _sigmoid(pg)
    err = x2 + gate * pp - target
    dx3 = err * (1.0 / err.shape[-1])
    loss = 0.5 * jnp.sum(jnp.mean(err * err, axis=-1, keepdims=True), axis=0, keepdims=True)
    return dx3, dx3 * pp * gate * (1.0 - gate), dx3 * gate, loss


def _attention_fwd(q, k, v):
    H, T, D = q.shape
    DV = v.shape[-1]
    tq, ck = min(ATT_TQ, T), min(ATT_CHUNK, T)
    c2 = (D ** -0.5) * LOG2_E

    def body(q_ref, k_ref, v_ref, o_ref, lse_ref):
        q_i = q_ref[0]

        def chunk(c, carry):
            m, l, acc = carry
            rows = pl.ds(pl.multiple_of(c * ck, ck), ck)
            s = _dot_raw(q_i, k_ref[0, rows, :], "nt")
            m_new = jnp.maximum(m, jnp.max(s, axis=-1, keepdims=True))
            p = jnp.exp2((s - m_new) * c2)
            alpha = jnp.exp2((m - m_new) * c2)
            l = l * alpha + jnp.sum(p, axis=-1, keepdims=True)
            acc = acc * alpha + _dot_raw(p, v_ref[0, rows, :], "nn")
            return m_new, l, acc

        init = (jnp.full((tq, 1), -jnp.inf, F32), jnp.zeros((tq, 1), F32), jnp.zeros((tq, DV), F32))
        m, l, acc = lax.fori_loop(0, T // ck, chunk, init, unroll=True)
        o_ref[...] = acc / l
        lse_ref[0] = m * c2 + jnp.log2(l)

    return pl.pallas_call(
        body, name="attention_fwd", grid=(H, T // tq),
        in_specs=[pl.BlockSpec((1, tq, D), lambda h, i: (h, i, 0)),
                  pl.BlockSpec((1, T, D), lambda h, i: (h, 0, 0)),
                  pl.BlockSpec((1, T, DV), lambda h, i: (h, 0, 0))],
        out_specs=[pl.BlockSpec((tq, DV), lambda h, i: (i, h)),
                   pl.BlockSpec((1, tq, 1), lambda h, i: (h, i, 0))],
        out_shape=[_sds((T, H * DV), F32), _sds((H, T, 1), F32)],
        compiler_params=_params(("parallel", "parallel")),
    )(q, k, v)


def _attention_bwd(q, k, v, o, lse2, dmix):
    H, T, D = q.shape
    DV = v.shape[-1]
    tk, cq = min(ATT_TK, T), min(ATT_CHUNK, T)
    scale = D ** -0.5
    c2 = scale * LOG2_E

    def body(q_ref, k_ref, v_ref, o_ref, lse_ref, do_ref, dq_ref, dk_ref, dv_ref, delta_ref):
        j = pl.program_id(1)

        @pl.when(j == 0)
        def _():
            delta = lax.dot_general(jnp.ones((8, DV), F32), do_ref[...] * o_ref[...], (((1,), (1,)), ((), ())),
                                    precision=lax.Precision.HIGHEST, preferred_element_type=F32)
            for i in range(T // cq):
                delta_ref[i] = delta[:, i * cq:(i + 1) * cq]
            dq_ref[0] = jnp.zeros((T, D), F32)

        k_j, v_j = k_ref[0], v_ref[0]
        dk_ref[0] = jnp.zeros((tk, D), F32)
        dv_ref[0] = jnp.zeros((tk, DV), F32)

        def chunk(c, carry):
            rows = pl.ds(pl.multiple_of(c * cq, cq), cq)
            q_c = q_ref[0, rows, :]
            do_c = do_ref[rows, :].astype(BF16)
            st = _dot_raw(k_j, q_c, "nt")
            pt = jnp.exp2(st * c2 - lse_ref[0, c])
            dv_ref[0] += _dot_raw(pt, do_c, "nn")
            dpt = _dot_raw(v_j, do_c, "nt")
            dst = pt * (dpt - delta_ref[c, 0:1, :]) * scale
            dk_ref[0] += _dot_raw(dst, q_c, "nn")
            dq_ref[0, rows, :] += _dot_raw(dst, k_j, "tn")
            return carry

        lax.fori_loop(0, T // cq, chunk, 0, unroll=True)

    return pl.pallas_call(
        body, name="attention_bwd", grid=(H, T // tk),
        in_specs=[pl.BlockSpec((1, T, D), lambda h, j: (h, 0, 0)),
                  pl.BlockSpec((1, tk, D), lambda h, j: (h, j, 0)),
                  pl.BlockSpec((1, tk, DV), lambda h, j: (h, j, 0)),
                  pl.BlockSpec((T, DV), lambda h, j: (0, h)),
                  pl.BlockSpec((1, T // cq, 1, cq), lambda h, j: (h, 0, 0, 0)),
                  pl.BlockSpec((T, DV), lambda h, j: (0, h))],
        out_specs=[pl.BlockSpec((1, T, D), lambda h, j: (h, 0, 0)),
                   pl.BlockSpec((1, tk, D), lambda h, j: (h, j, 0)),
                   pl.BlockSpec((1, tk, DV), lambda h, j: (h, j, 0))],
        out_shape=[_sds((H, T, D), F32), _sds((H, T, D), F32), _sds((H, T, DV), F32)],
        scratch_shapes=[pltpu.VMEM((T // cq, 8, cq), F32)],
        compiler_params=_params(("parallel", "arbitrary")),
    )(q, k, v, o, lse2.reshape(H, T // cq, 1, cq), dmix)


def _split3_dot(ones, x, kind):
    hi = x.astype(BF16)
    rest = x - hi.astype(F32)
    mid = rest.astype(BF16)
    lo = (rest - mid.astype(F32)).astype(BF16)
    return (_dot_raw(ones, hi, kind) + _dot_raw(ones, mid, kind)) + _dot_raw(ones, lo, kind)


@jax.custom_vjp
def _running_sum(x, tri):
    return _split3_dot(tri, x, "nn")


def _running_sum_fwd(x, tri):
    return _split3_dot(tri, x, "nn"), tri


def _running_sum_bwd(tri, g):
    return _split3_dot(tri, g, "tn"), jnp.zeros_like(tri)


_running_sum.defvjp(_running_sum_fwd, _running_sum_bwd)


def _gla_block(hq, hf, hi, lower, st_in, *, rev, dot):
    rows, dk = hq.shape
    G, C = rows // CHUNK, CHUNK
    q = hq * _sigmoid(hq)
    f = lower + (1.0 - lower) * _sigmoid(hf)
    k = 1.0 - f
    logf = jnp.log2(f)
    q3, k3, v3, lf3 = (t.reshape(G, C, dk) for t in (q, k, hi, logf))
    r = lax.broadcasted_iota(jnp.int32, (C, C), 0)
    c = lax.broadcasted_iota(jnp.int32, (C, C), 1)
    tri = ((r <= c) if rev else (r >= c)).astype(F32)
    b = _running_sum(lf3, jnp.broadcast_to(tri, (G, C, C)))
    tpos = lax.broadcasted_iota(jnp.int32, (1, C, 1), 1)
    first_half = (tpos >= C // 2) if rev else (tpos <= C // 2 - 1)
    b_mid = jnp.sum(jnp.where(first_half, lf3, 0.0), axis=1, keepdims=True)
    b_last = jnp.sum(lf3, axis=1, keepdims=True)
    a = dot(q3 * jnp.exp2(b - b_mid), k3 * jnp.exp2(b_mid - b), "nt") * tri
    o_intra = dot(a, v3, "nn")
    kv_t = dot(v3, k3 * jnp.exp2(b_last - b), "tn")
    decay = jnp.exp2(b_last)
    qd = q3 * jnp.exp2(b)
    st = st_in
    o_inter = [None] * G
    for g in (reversed(range(G)) if rev else range(G)):
        o_inter[g] = dot(qd[g], st, "nt")
        st = st * decay[g] + kv_t[g]
    o = o_intra.reshape(rows, dk) + jnp.concatenate(o_inter, axis=0)
    return o, st


def _gla_fwd(z, lower3, *, rev, col_q, col_f, col_v, hp):
    T = z.shape[0]
    rows = min(GLA_GROUP * CHUNK, T)
    nb = T // rows
    wide = hp * HG_DK
    blk = (lambda n: nb - 1 - n) if rev else (lambda n: n)

    def body(hq_ref, hf_ref, hi_ref, low_ref, o_ref, st_out_ref, st_ref):
        @pl.when(pl.program_id(1) == 0)
        def _():
            st_ref[...] = jnp.zeros_like(st_ref)

        st_in = [st_ref[i] for i in range(hp)]
        heads = []
        for i in range(hp):
            cols = slice(i * HG_DK, (i + 1) * HG_DK)
            heads.append(_gla_block(hq_ref[:, cols], hf_ref[:, cols], hi_ref[:, cols], low_ref[i], st_in[i], rev=rev,
                                    dot=_dot_raw))
        for i, (o, st) in enumerate(heads):
            st_out_ref[i, 0] = st_in[i]
            o_ref[:, i * HG_DK:(i + 1) * HG_DK] = o
            st_ref[i] = st

    def zspec(col):
        return pl.BlockSpec((rows, wide), lambda h, n: (blk(n), col // wide + h))

    return pl.pallas_call(
        body, name="gla_fwd_rev" if rev else "gla_fwd", grid=(HG_HEADS // hp, nb),
        in_specs=[zspec(col_q), zspec(col_f), zspec(col_v), pl.BlockSpec((hp, 1, HG_DK), lambda h, n: (h, 0, 0))],
        out_specs=[pl.BlockSpec((rows, wide), lambda h, n: (blk(n), h)),
                   pl.BlockSpec((hp, 1, HG_DK, HG_DK), lambda h, n: (h, blk(n), 0, 0))],
        out_shape=[_sds((T, HG_HEADS * HG_DK), F32), _sds((HG_HEADS, nb, HG_DK, HG_DK), F32)],
        scratch_shapes=[pltpu.VMEM((hp, HG_DK, HG_DK), F32)],
        compiler_params=_params(("parallel", "arbitrary")),
    )(z, z, z, lower3)


def _gla_bwd(z, lower3, states, do, prev, *, rev, col_q, col_f, col_v, hp):
    T = z.shape[0]
    rows = min(GLA_GROUP * CHUNK, T)
    nb = T // rows
    wide = hp * HG_DK
    blk = (lambda n: n) if rev else (lambda n: nb - 1 - n)
    has_prev = prev is not None
    fn = functools.partial(_gla_block, rev=rev, dot=_bdot)

    def body(*refs):
        hq_ref, hf_ref, hi_ref, low_ref, st_ref, do_ref = refs[:6]
        rest = refs[6:]
        if has_prev:
            pq_ref, pi_ref = rest[:2]
            rest = rest[2:]
        dhq_ref, dhi_ref, dhf_ref, dlow_ref, dst_ref = rest
        n = pl.program_id(1)

        @pl.when(n == 0)
        def _():
            dst_ref[...] = jnp.zeros_like(dst_ref)

        dst_in = [dst_ref[i] for i in range(hp)]
        heads = []
        for i in range(hp):
            cols = slice(i * HG_DK, (i + 1) * HG_DK)
            _, vjp = jax.vjp(fn, hq_ref[:, cols], hf_ref[:, cols], hi_ref[:, cols], low_ref[i], st_ref[i, 0])
            dhq, dhf, dhi, dlow, dst = vjp((do_ref[:, cols], dst_in[i]))
            if has_prev:
                dhq = dhq + pq_ref[:, cols]
                dhi = dhi + pi_ref[:, cols]
            heads.append((dhq, dhf, dhi, dlow, dst))
        for i, (dhq, dhf, dhi, dlow, dst) in enumerate(heads):
            cols = slice(i * HG_DK, (i + 1) * HG_DK)
            dst_ref[i] = dst
            dhq_ref[:, cols] = dhq.astype(dhq_ref.dtype)
            dhi_ref[:, cols] = dhi.astype(dhi_ref.dtype)
            dhf_ref[:, cols] = dhf.astype(dhf_ref.dtype)

        @pl.when(n == 0)
        def _():
            for i in range(hp):
                dlow_ref[i] = heads[i][3]

        @pl.when(n > 0)
        def _():
            for i in range(hp):
                dlow_ref[i] += heads[i][3]

    def zspec(col):
        return pl.BlockSpec((rows, wide), lambda h, n: (blk(n), col // wide + h))

    hspec = pl.BlockSpec((rows, wide), lambda h, n: (blk(n), h))
    in_specs = [zspec(col_q), zspec(col_f), zspec(col_v), pl.BlockSpec((hp, 1, HG_DK), lambda h, n: (h, 0, 0)),
                pl.BlockSpec((hp, 1, HG_DK, HG_DK), lambda h, n: (h, blk(n), 0, 0)), hspec]
    ins = [z, z, z, lower3, states, do]
    if has_prev:
        in_specs += [hspec, hspec]
        ins += list(prev)
    full_wide = HG_HEADS * HG_DK
    acc_dtype = BF16 if has_prev else F32
    return pl.pallas_call(
        body, name="gla_bwd_rev" if rev else "gla_bwd", grid=(HG_HEADS // hp, nb),
        in_specs=in_specs,
        out_specs=[hspec, hspec, hspec, pl.BlockSpec((hp, 1, HG_DK), lambda h, n: (h, 0, 0))],
        out_shape=[_sds((T, full_wide), acc_dtype), _sds((T, full_wide), acc_dtype), _sds((T, full_wide), BF16),
                   _sds((HG_HEADS, 1, HG_DK), F32)],
        scratch_shapes=[pltpu.VMEM((hp, HG_DK, HG_DK), F32)],
        compiler_params=_params(("parallel", "arbitrary")),
    )(*ins)


def _lower_fn(lb):
    e = jnp.exp(lb - jnp.max(lb, axis=0, keepdims=True))
    return (e / jnp.sum(e, axis=0, keepdims=True))[0]


def _lower_bounds(lb):
    def body(lb_ref, o_ref):
        o_ref[...] = _lower_fn(lb_ref[...])
    return pl.pallas_call(body, name="lower_bounds", out_shape=_sds(lb.shape[1:], F32))(lb)


def _row_tile(r, cap=1024):
    best = None
    for t in range(16, min(r, cap) + 1, 16):
        if r % t == 0:
            best = t
    return best if best is not None else r


def _sum4(shards, recv, chip, *, name):
    _, R, C = shards.shape
    tr = _row_tile(R)

    def body(chip_ref, o_ref, r_ref, out_ref):
        out_ref[...] = ((o_ref[0].astype(F32) + r_ref[0].astype(F32)) + r_ref[1].astype(F32)) + r_ref[2].astype(F32)

    grid_spec = pltpu.PrefetchScalarGridSpec(
        num_scalar_prefetch=1, grid=(R // tr,),
        in_specs=[pl.BlockSpec((1, tr, C), lambda i, chip_ref: (chip_ref[0], i, 0)),
                  pl.BlockSpec((3, tr, C), lambda i, chip_ref: (0, i, 0))],
        out_specs=pl.BlockSpec((tr, C), lambda i, chip_ref: (i, 0)))
    return pl.pallas_call(
        body, name=name, grid_spec=grid_spec, out_shape=_sds((R, C), F32), compiler_params=_params(("parallel",)),
    )(chip, shards, recv)


def _adamw_math(w, g, m, v):
    m = ADAM_B1 * m + (1.0 - ADAM_B1) * g
    v = ADAM_B2 * v + (1.0 - ADAM_B2) * (g * g)
    m_hat = m / (1.0 - ADAM_B1 ** ADAM_STEP)
    v_hat = v / (1.0 - ADAM_B2 ** ADAM_STEP)
    delta = -ADAM_LR * (m_hat / (jnp.sqrt(v_hat) + ADAM_EPS) + ADAM_WD * w)
    return delta, m, v


def _adamw(w, g_a, g_b, m, v, *, name):
    R, C = w.shape
    tr = _row_tile(R)
    two = g_b is not None

    def body(*refs):
        w_ref, ga_ref = refs[0], refs[1]
        rest = refs[2:]
        g = ga_ref[...]
        if two:
            g = g + rest[0][...]
            rest = rest[1:]
        m_ref, v_ref, g_out, d_out, m_out, v_out, token = rest
        delta, m_new, v_new = _adamw_math(w_ref[...], g, m_ref[...], v_ref[...])
        g_out[...] = g
        d_out[...] = delta
        m_out[...] = m_new
        v_out[...] = v_new
        token[...] = jnp.zeros_like(token)

    spec = pl.BlockSpec((tr, C), lambda i: (i, 0))
    ins = [w, g_a] + ([g_b] if two else []) + [m, v]
    return pl.pallas_call(
        body, name=name, grid=(R // tr,), in_specs=[spec] * len(ins),
        out_specs=[spec] * 4 + [pl.BlockSpec((8, LANE), lambda i: (0, 0))],
        out_shape=[_sds((R, C), F32)] * 4 + [_sds((8, LANE), F32)], compiler_params=_params(("arbitrary",)),
    )(*ins)


def _adamw_small(red, lb_full, ws, ms, vs):
    n = len(ws)

    def pieces(shape):
        out = []
        for j, idx in enumerate(itertools.product(*[range(d) for d in shape[:-1]])):
            out.append((idx[:-1] + (slice(idx[-1], idx[-1] + 1), slice(None)), j * shape[-1]))
        return out

    def body(*refs):
        red_ref, lb_ref = refs[0], refs[1]
        w_refs, m_refs, v_refs = refs[2:2 + n], refs[2 + n:2 + 2 * n], refs[2 + 2 * n:2 + 3 * n]
        out_refs = refs[2 + 3 * n:]
        chip = 2 * lax.axis_index("x") + lax.axis_index("y")
        n_f, shard = lb_ref.shape[-1], w_refs[n - 1].shape[-1]
        _, vjp = jax.vjp(_lower_fn, lb_ref[...])
        dlb = vjp(red_ref[8:10, 0:n_f])[0]
        for i in range(n):
            width = w_refs[i].shape[-1]
            for j, (at, lane) in enumerate(pieces(w_refs[i].shape)):
                if i < n - 1:
                    g = red_ref[i:i + 1, lane:lane + width]
                else:
                    row = dlb[j // 2][j % 2:j % 2 + 1]
                    g = sum(jnp.where(chip == q, row[:, q * shard:(q + 1) * shard], 0.0) for q in range(N_CHIPS))
                delta, m_new, v_new = _adamw_math(w_refs[i][at], g, m_refs[i][at], v_refs[i][at])
                for o_ref, val in zip(out_refs[4 * i:4 * i + 4], (g, delta, m_new, v_new)):
                    o_ref[at] = val

    return pl.pallas_call(
        body, name="adamw_small", out_shape=[_sds(w.shape, F32) for w in ws for _ in range(4)],
    )(red, lb_full, *ws, *ms, *vs)


def _chip_peers():
    x, y, c = lax.axis_index("x"), lax.axis_index("y"), lax.axis_index("c")
    return (x, y, c), 2 * x + y, [(1 - x, y), (x, 1 - y), (1 - x, 1 - y)]


_HBM = pl.BlockSpec(memory_space=pltpu.HBM)
_SEM = pl.BlockSpec(memory_space=pltpu.SEMAPHORE)
_EFFECT = pltpu.SideEffectType.DATAFLOW_SIDE_EFFECTING


def _exchange_copies(srcs, lands, sems, mode):
    (x, y, c), me, chips = _chip_peers()
    copies = []
    for t, (src, land) in enumerate(zip(srcs, lands)):
        if mode == "swap":
            copies.append(pltpu.make_async_remote_copy(src, land, sems[0].at[3 * t], sems[1].at[3 * t],
                                                       device_id=(x, y, 1 - c), device_id_type=MESH))
            continue
        for k, (px, py) in enumerate(chips):
            gather = mode == "gather"
            copies.append(pltpu.make_async_remote_copy(
                src if gather else src.at[2 * px + py], land.at[me] if gather else land.at[k],
                sems[0].at[3 * t + k], sems[1].at[3 * t + k], device_id=(px, py, c), device_id_type=MESH))
        if mode == "gather":
            copies.append(pltpu.make_async_copy(src, land.at[me], sems[2].at[t]))
    return copies


def _exchange_start(srcs, *, mode, name, after=None):
    n = len(srcs)
    n_sem = 3 if mode == "gather" else 2
    n_in = 2 * n + (after is not None)
    land_shape = {"gather": lambda s: (N_CHIPS,) + s.shape, "scatter": lambda s: (3,) + s.shape[1:], "swap": lambda s: s.shape}
    lands = [_sds(land_shape[mode](s), s.dtype) for s in srcs]

    def body(*refs):
        for cp in _exchange_copies(refs[:n], refs[n:2 * n], refs[n_in:n_in + n_sem], mode):
            cp.start()
        token = refs[-1]
        token[...] = jnp.zeros_like(token)

    sem_shapes = [pltpu.SemaphoreType.DMA((3 * n,)), pltpu.SemaphoreType.DMA((3 * n,))]
    sem_shapes += [pltpu.SemaphoreType.DMA((n,))] if mode == "gather" else []
    thru = [pltpu.HBM(s.shape, s.dtype) for s in srcs] + [pltpu.HBM(l.shape, l.dtype) for l in lands]
    res = pl.pallas_call(
        body, name=name, in_specs=[_HBM] * (2 * n) + [_ANY] * (after is not None),
        out_specs=[_SEM] * n_sem + [_HBM] * (2 * n) + [pl.BlockSpec(memory_space=pltpu.VMEM)],
        out_shape=sem_shapes + thru + [_sds((8, LANE), F32)], input_output_aliases={i: n_sem + i for i in range(2 * n)},
        compiler_params=pltpu.CompilerParams(has_side_effects=_EFFECT),
    )(*[pltpu.with_memory_space_constraint(s, pltpu.HBM) for s in srcs],
      *[pltpu.with_memory_space_constraint(lax.empty(l.shape, l.dtype), pltpu.HBM) for l in lands],
      *([after] if after is not None else []))
    return (res[:n_sem], res[n_sem:n_sem + n], res[n_sem + n:n_sem + 2 * n], mode), res[-1]


def _exchange_wait(started, after, *, name):
    sems, srcs, lands, mode = started
    n, n_sem = len(srcs), len(sems)
    after = list(after) if isinstance(after, (list, tuple)) else [after]

    def body(*refs):
        for cp in _exchange_copies(refs[:n], refs[n:2 * n], refs[2 * n:2 * n + n_sem], mode):
            cp.wait()

    res = pl.pallas_call(
        body, name=name, in_specs=[_HBM] * (2 * n) + [_SEM] * n_sem + [_ANY] * len(after), out_specs=[_HBM] * (2 * n),
        out_shape=[pltpu.HBM(a.shape, a.dtype) for a in list(srcs) + list(lands)],
        input_output_aliases={i: i for i in range(2 * n)},
        compiler_params=pltpu.CompilerParams(has_side_effects=_EFFECT),
    )(*srcs, *lands, *sems, *after)
    return res[:n], res[n:]


def _allreduce_small(pack, after):
    R, C = pack.shape

    def body(in_ref, after_ref, out_ref, slots, send_sems, recv_sems):
        x, y, c = lax.axis_index("x"), lax.axis_index("y"), lax.axis_index("c")
        me = 4 * x + 2 * y + c
        slots[me] = in_ref[...]
        copies = []
        for k in range(1, N_DEV):
            peer = (x ^ ((k >> 2) & 1), y ^ ((k >> 1) & 1), c ^ (k & 1))
            cp = pltpu.make_async_remote_copy(in_ref, slots.at[me], send_sems.at[k - 1], recv_sems.at[k - 1],
                                              device_id=peer, device_id_type=MESH)
            cp.start()
            copies.append(cp)
        for cp in copies:
            cp.wait()
        acc = slots[0]
        for d in range(1, N_DEV):
            acc = acc + slots[d]
        out_ref[...] = acc

    return pl.pallas_call(
        body, name="allreduce_small", out_shape=_sds((R, C), F32),
        in_specs=[pl.BlockSpec(memory_space=pltpu.VMEM), _ANY], out_specs=pl.BlockSpec(memory_space=pltpu.VMEM),
        scratch_shapes=[pltpu.VMEM((N_DEV, R, C), F32), pltpu.SemaphoreType.DMA((N_DEV - 1,)),
                        pltpu.SemaphoreType.DMA((N_DEV - 1,))],
        compiler_params=_params(),
    )(pack, after)


_Z_CQ, _Z_CKV, _Z_HQ, _Z_HFF, _Z_HFB, _Z_HI, _Z_HG, _Z_KR, _Z_END = 0, 256, 512, 1024, 1536, 2048, 2560, 3072, 3200


def _to_z_layout(wt):
    pad = jnp.zeros((_Z_END - _Z_KR - QK_ROPE, wt.shape[1]), wt.dtype)
    return jnp.concatenate([wt[:512], wt[512 + QK_ROPE:], wt[512:512 + QK_ROPE], pad], axis=0)


def _from_z_layout(wt):
    return jnp.concatenate([wt[:512], wt[_Z_KR:_Z_KR + QK_ROPE], wt[512:_Z_KR]], axis=0)


def _col_shards_to_full(g):
    return jnp.transpose(g, (1, 0, 2)).reshape(g.shape[1], -1)


def _full_to_col_shards(w):
    r, c = w.shape
    return jnp.transpose(w.reshape(r, N_CHIPS, c // N_CHIPS), (1, 0, 2))


def _full_to_row_shards(w):
    r, c = w.shape
    return w.reshape(N_CHIPS, r // N_CHIPS, c)


def kernel(x, p, positions, g_mix, w_in, g_qa, g_kva, w_qb, w_kvb, g_qn, g_kn, lb_param, g_hgo, w_o, g_ffn, w_gate, w_up, w_down, g_ple, w_ple_gate, w_ple_proj, loss_target, m_g_mix, m_w_in, m_g_qa, m_g_kva, m_w_qb, m_w_kvb, m_g_qn, m_g_kn, m_lb_param, m_g_hgo, m_w_o, m_g_ffn, m_w_gate, m_w_up, m_w_down, m_g_ple, m_w_ple_gate, m_w_ple_proj, v_g_mix, v_w_in, v_g_qa, v_g_kva, v_w_qb, v_w_kvb, v_g_qn, v_g_kn, v_lb_param, v_g_hgo, v_w_o, v_g_ffn, v_w_gate, v_w_up, v_w_down, v_g_ple, v_w_ple_gate, v_w_ple_proj):
    w_named = dict(g_mix=g_mix, w_in=w_in, g_qa=g_qa, g_kva=g_kva, w_qb=w_qb, w_kvb=w_kvb, g_qn=g_qn, g_kn=g_kn,
                   lb_param=lb_param, g_hgo=g_hgo, w_o=w_o, g_ffn=g_ffn, w_gate=w_gate, w_up=w_up, w_down=w_down,
                   g_ple=g_ple, w_ple_gate=w_ple_gate, w_ple_proj=w_ple_proj)
    m_named = dict(g_mix=m_g_mix, w_in=m_w_in, g_qa=m_g_qa, g_kva=m_g_kva, w_qb=m_w_qb, w_kvb=m_w_kvb, g_qn=m_g_qn,
                   g_kn=m_g_kn, lb_param=m_lb_param, g_hgo=m_g_hgo, w_o=m_w_o, g_ffn=m_g_ffn, w_gate=m_w_gate,
                   w_up=m_w_up, w_down=m_w_down, g_ple=m_g_ple, w_ple_gate=m_w_ple_gate, w_ple_proj=m_w_ple_proj)
    v_named = dict(g_mix=v_g_mix, w_in=v_w_in, g_qa=v_g_qa, g_kva=v_g_kva, w_qb=v_w_qb, w_kvb=v_w_kvb, g_qn=v_g_qn,
                   g_kn=v_g_kn, lb_param=v_lb_param, g_hgo=v_g_hgo, w_o=v_w_o, g_ffn=v_g_ffn, w_gate=v_w_gate,
                   w_up=v_w_up, w_down=v_w_down, g_ple=v_g_ple, w_ple_gate=v_w_ple_gate, w_ple_proj=v_w_ple_proj)
    order = list(w_named)
    transposed = ("w_in", "w_qb", "w_gate", "w_up")
    col_sharded = ("w_kvb", "w_ple_proj")
    row_sharded = ("w_o", "w_down", "w_ple_gate")
    big = transposed + col_sharded + row_sharded

    def view(n, a):
        return jnp.transpose(a[0]) if n in transposed else a[0]

    def unview(n, a):
        return (jnp.transpose(a) if n in transposed else a)[None]

    def to_shards(n, g):
        return _full_to_col_shards(g) if n in col_sharded else _full_to_row_shards(g)

    x2d, p2d, tgt = x[0], p[0, 0], loss_target[0]
    T, D = x2d.shape

    lb_flat = lb_param.reshape(-1, lb_param.shape[-1])
    gather_groups = (("w_in",), ("w_qb", "w_kvb"), ("w_o", "w_gate", "w_up", "w_down", "w_ple_gate", "w_ple_proj"))
    gather_started = []

    casts = {n: view(n, w_named[n]).astype(BF16) for n in big}

    def gather_start(gi, after):
        srcs = [casts[n] for n in gather_groups[gi]] + ([lb_flat] if gi == 0 else [])
        started, token = _exchange_start(srcs, mode="gather", name=f"gather_start_{gi}", after=after)
        gather_started.append(started)
        return token

    full = {}

    def gather_wait(gi, after):
        _, got = _exchange_wait(gather_started[gi], after, name=f"gather_wait_{gi}")
        for n, g in zip(gather_groups[gi], got):
            full[n] = _col_shards_to_full(g) if n in col_sharded else g.reshape(-1, g.shape[-1])
        return got

    g_hgo_row = g_hgo.reshape(1, -1)

    inv_freq = ROPE_THETA ** (-jnp.arange(0, QK_ROPE, 2, dtype=F32) / QK_ROPE)
    ang = positions[0].astype(F32)[:, None] * inv_freq
    cos, sin = jnp.cos(ang), jnp.sin(ang)
    token = gather_start(0, None)
    h1 = _stage(_rms, [x2d], [g_mix], [_sds((T, D), BF16)], [], name="norm_mix", after=token)[0]
    got = gather_wait(0, [h1, cos, sin] + [casts[n] for g in gather_groups[1:] for n in g])
    token = got[0]
    for gi in range(1, len(gather_groups)):
        token = gather_start(gi, token)
    lb_full = _col_shards_to_full(got[-1]).reshape(lb_param.shape[0], lb_param.shape[1], -1)
    w_in_zt = _to_z_layout(full["w_in"])
    z = _mm(h1, w_in_zt, tb=True, name="in_proj", after=token)
    qn, kvn = _stage(_mla_a_fn, [_cols(z, 256, 0), _cols(z, 256, 1)], [g_qa, g_kva],
                     [_sds((T, 256), BF16), _sds((T, 256), BF16)], [], name="mla_latent_norm")
    gather_wait(1, qn)
    q_raw = _mm(qn, full["w_qb"], tb=True, name="q_up")
    kv_raw = _mm(kvn, full["w_kvb"], name="kv_up")
    kr = _cols(z, LANE, _Z_KR // LANE)
    q, k, v = _stage(_mla_b_fn, [q_raw, kv_raw, kr, cos, sin], [g_qn, g_kn],
                     [_sds((MLA_HEADS, T, QK_HEAD), BF16), _sds((MLA_HEADS, T, QK_HEAD), BF16),
                      _sds((MLA_HEADS, T, V_HEAD), BF16)], [], name="mla_qk_norm_rope")
    att, lse = _attention_fwd(q, k, v)

    lower = _lower_bounds(lb_full)
    lower3 = lower.reshape(2, HG_HEADS, 1, HG_DK)
    o_f, st_f = _gla_fwd(z, lower3[0], rev=False, col_q=_Z_HQ, col_f=_Z_HFF, col_v=_Z_HI, hp=GLA_FWD_HEADS)
    o_b, st_b = _gla_fwd(z, lower3[1], rev=True, col_q=_Z_HQ, col_f=_Z_HFB, col_v=_Z_HI, hp=GLA_FWD_HEADS)
    hg = _cols(z, 512, _Z_HG // 512)
    mix = _stage(_post_fn, [att, o_f, o_b, hg], [g_hgo_row], [_sds((T, att.shape[1] + o_f.shape[1]), BF16)], [],
                 name="mix_out")[0]
    gather_wait(2, mix)
    x1, h2 = _mm_fused(mix, full["w_o"], _resid_norm_fn, [x2d], [g_ffn], [F32, BF16], [], full_rows=True,
                       name="out_proj")
    gt, up, act = _mm_fused(h2, full["w_gate"], lambda gt, up: (gt, up, _swiglu_fn(gt, up)), [], [], [BF16, BF16, BF16],
                            [], tb=True, b2=full["w_up"], name="ffn_gate_up")
    x2, h3 = _mm_fused(act, full["w_down"], _resid_norm_fn, [x1], [g_ple], [F32, BF16], [], full_rows=True,
                       name="ffn_down")
    pp = _mm(p2d, full["w_ple_proj"], name="ple_proj")
    dx3, dpg, dpp, loss_part = _mm_fused(
        h3, full["w_ple_gate"], lambda acc, pp, x2, tgt: _ple_loss_fn(x2, acc, pp, tgt), [pp, x2, tgt], [],
        [F32, BF16, BF16], [_sds((1, 1), F32)], full_rows=True, name="ple_gate_loss")

    grads = {}
    scatter_groups = (("w_ple_proj", "w_ple_gate", "w_down", "w_gate", "w_up", "w_o"), ("w_qb", "w_kvb", "w_in"))
    scatter_started = []

    def scatter_start(gi):
        srcs = [to_shards(n, grads[n]) for n in scatter_groups[gi]]
        started, token = _exchange_start(srcs, mode="scatter", name=f"scatter_start_{gi}")
        scatter_started.append(started)
        return token

    chip = 2 * lax.axis_index("x") + lax.axis_index("y")
    swap_started = []

    def reduce_group(gi, after):
        shards, recvs = _exchange_wait(scatter_started[gi], after, name=f"scatter_wait_{gi}")
        sums = [_sum4(s, r, chip.reshape(1), name="sum_" + n) for n, s, r in zip(scatter_groups[gi], shards, recvs)]
        started, token = _exchange_start(sums, mode="swap", name=f"swap_start_{gi}")
        swap_started.append(started)
        return token

    grads["w_ple_proj"] = _mm(p2d, dpp, ta=True, out_dtype=BF16, name="d_w_ple_proj")
    grads["w_ple_gate"] = _mm(h3, dpg, ta=True, out_dtype=BF16, name="d_w_ple_gate")
    dx2, grads["g_ple"] = _mm_fused(
        dpg, full["w_ple_gate"], lambda acc, x2, dx3, g: _norm_bwd_fn(x2, acc, dx3, g), [x2, dx3], [g_ple],
        [F32], [_sds((1, D), F32)], tb=True, full_rows=True, name="d_h3_norm_ple_bwd")
    dgt, dup = _mm_fused(dx2, full["w_down"], lambda acc, gt, up: _swiglu_bwd_fn(gt.astype(F32), up.astype(F32), acc), [gt, up], [],
                         [BF16, BF16], [], tb=True, name="d_act_swiglu_bwd")
    grads["w_down"] = _mm(act, dx2, ta=True, out_dtype=BF16, name="d_w_down")
    grads["w_gate"] = _mm(dgt, h2, ta=True, out_dtype=BF16, name="d_w_gate")
    grads["w_up"] = _mm(dup, h2, ta=True, out_dtype=BF16, name="d_w_up")
    dx1, grads["g_ffn"] = _mm_fused(
        dgt, full["w_gate"], lambda acc, x1, dx2, g: _norm_bwd_fn(x1, acc, dx2, g), [x1, dx2], [g_ffn],
        [F32], [_sds((1, D), F32)], full_rows=True, a2=dup, b2=full["w_up"], name="d_h2_norm_ffn_bwd")
    grads["w_o"] = _mm(mix, dx1, ta=True, out_dtype=BF16, name="d_w_o")
    token = scatter_start(0)
    dmix = _mm(dx1, full["w_o"], tb=True, name="d_mix", after=token)

    half = MLA_HEADS * V_HEAD
    do, dhg, dg_hgo = _stage(_post_bwd_fn, [o_f, o_b, hg, _cols(dmix, half, 1)], [g_hgo_row],
                             [_sds((T, half), F32), _sds((T, half), BF16)], [_sds((1, half), F32)], name="mix_out_bwd")
    grads["g_hgo"] = dg_hgo
    dhq_f, dhi_f, dhf_f, dlow_f = _gla_bwd(z, lower3[0], st_f, do, None, rev=False,
                                           col_q=_Z_HQ, col_f=_Z_HFF, col_v=_Z_HI, hp=GLA_BWD_HEADS)
    dhq, dhi, dhf_b, dlow_b = _gla_bwd(z, lower3[1], st_b, do, (dhq_f, dhi_f), rev=True,
                                       col_q=_Z_HQ, col_f=_Z_HFB, col_v=_Z_HI, hp=GLA_BWD_HEADS)

    dq, dk, dv = _attention_bwd(q, k, v, att, lse, dmix)
    dq_raw, dkv_raw, dkr, grads["g_qn"], grads["g_kn"] = _stage(
        _mla_b_bwd_fn, [q_raw, kv_raw, kr, cos, sin, dq, dk, dv], [g_qn, g_kn],
        [_sds(q_raw.shape, BF16), _sds(kv_raw.shape, BF16), _sds((T, LANE), BF16)],
        [_sds(g_qn.shape, F32), _sds(g_kn.shape, F32)], name="mla_qk_norm_rope_bwd")
    grads["w_qb"] = _mm(dq_raw, qn, ta=True, out_dtype=BF16, name="d_w_qb")
    grads["w_kvb"] = _mm(kvn, dkv_raw, ta=True, out_dtype=BF16, name="d_w_kvb")
    dqn = _mm(dq_raw, full["w_qb"], name="d_qn")
    dkvn = _mm(dkv_raw, full["w_kvb"], tb=True, name="d_kvn")
    dcq, dckv, grads["g_qa"], grads["g_kva"] = _stage(
        _mla_a_bwd_fn, [_cols(z, 256, 0), _cols(z, 256, 1), dqn, dkvn], [g_qa, g_kva],
        [_sds((T, 256), BF16), _sds((T, 256), BF16)], [_sds(g_qa.shape, F32), _sds(g_kva.shape, F32)],
        name="mla_latent_norm_bwd")
    token = reduce_group(0, dcq)
    dz = jnp.concatenate([dcq, dckv, dhq, dhf_f, dhf_b, dhi, dhg, dkr], axis=1)
    grads["w_in"] = _from_z_layout(_mm(dz, h1, ta=True, out_dtype=BF16, name="d_w_in", after=token))
    token = scatter_start(1)
    grad_x, grads["g_mix"] = _mm_fused(
        dz, w_in_zt, lambda acc, x, dx1, g: _norm_bwd_fn(x, acc, dx1, g), [x2d, dx1], [g_mix],
        [F32], [_sds((1, D), F32)], full_rows=True, name="d_h1_norm_mix_bwd", after=token)

    out_g, out_d, out_m, out_v = {}, {}, {}, {}

    def update_group(gi, after):
        mine, theirs = _exchange_wait(swap_started[gi], after, name=f"swap_wait_{gi}")
        tokens = []
        for n, a, b in zip(scatter_groups[gi], mine, theirs):
            *res, token = _adamw(view(n, w_named[n]), a, b, view(n, m_named[n]), view(n, v_named[n]), name="adamw_" + n)
            out_g[n], out_d[n], out_m[n], out_v[n] = (unview(n, t) for t in res)
            tokens.append(token)
        return tokens

    done = update_group(0, grads["g_mix"])
    token = reduce_group(1, done)
    done = update_group(1, token)

    small = ("g_mix", "g_qa", "g_kva", "g_qn", "g_kn", "g_hgo", "g_ffn", "g_ple")
    small_all = small + ("lb_param",)
    width = -(-max(w_named[n].size for n in small_all) // LANE) * LANE

    def row(a):
        a = a.reshape(1, -1)
        return jnp.pad(a, ((0, 0), (0, width - a.shape[1])))

    dlower = jnp.concatenate([dlow_f.reshape(1, -1), dlow_b.reshape(1, -1)], axis=0)
    pack = jnp.concatenate([row(grads[n]) for n in small] + [row(dlower[0]), row(dlower[1]), row(loss_part)]
                           + [jnp.zeros((5, width), F32)], axis=0)
    red = _allreduce_small(pack, done[-1])
    loss = red[10, 0]

    outs = _adamw_small(red, lb_full, [w_named[n] for n in small_all], [m_named[n] for n in small_all],
                        [v_named[n] for n in small_all])
    for i, n in enumerate(small_all):
        out_g[n], out_d[n], out_m[n], out_v[n] = outs[4 * i:4 * i + 4]

    return (loss, grad_x[None], *[out_g[n] for n in order], *[out_d[n] for n in order],
            *[out_m[n] for n in order], *[out_v[n] for n in order])
```

```python
import functools
import itertools

import jax
import jax.numpy as jnp
from jax import lax
from jax.experimental import pallas as pl
from jax.experimental.pallas import tpu as pltpu

F32 = jnp.float32
BF16 = jnp.bfloat16
MESH = pl.DeviceIdType.MESH

EPS = 1e-6
ROPE_THETA = 10000.0
MLA_HEADS = 4
QK_NOPE = 128
QK_ROPE = 64
QK_HEAD = QK_NOPE + QK_ROPE
V_HEAD = 128
HG_HEADS = 4
HG_DK = 128
CHUNK = 64
ADAM_LR = 0.001
ADAM_B1 = 0.9
ADAM_B2 = 0.999
ADAM_EPS = 1e-08
ADAM_WD = 0.01
ADAM_STEP = 10

LANE = 128
VMEM_LIMIT = 56 * 1024 * 1024
TOK_TILE = 256
GLA_GROUP = 16
GLA_FWD_HEADS = 4
GLA_BWD_HEADS = 2
ATT_TQ = 1024
ATT_TK = 1024
ATT_CHUNK = 512
LOG2_E = 1.4426950408889634
N_CHIPS = 4
N_DEV = 8


_ANY = pl.BlockSpec(memory_space=pl.ANY)


def _params(dims=None, **kw):
    return pltpu.CompilerParams(dimension_semantics=dims, vmem_limit_bytes=VMEM_LIMIT, **kw)


def _tile_candidates(n, cap):
    out = [d for d in range(LANE, min(n, cap) + 1, LANE) if n % d == 0]
    if n <= cap and n not in out:
        out.append(n)
    return out or [n]


MM_VMEM_BUDGET = 40 * 1024 * 1024
MM_MIN_ROWS = 256
MM_MAX_ROWS = 1536
HBM_BYTES_PER_S = 2.8e12
MXU_FLOPS_PER_S = 8e14
STEP_OVERHEAD_S = 0.35e-6


def _mm_tiles(M, N, K, a_bytes, b_bytes, o_bytes, has_add, full_rows=False, full_k=False, n_a=1, n_b=1):
    cast_a, cast_b = a_bytes > 2, b_bytes > 2
    a_bytes, b_bytes = n_a * a_bytes, n_b * b_bytes
    best = None
    for tm in [t for t in _tile_candidates(M, MM_MAX_ROWS) if t >= min(M, MM_MIN_ROWS)]:
        for tn in ([N] if full_rows else _tile_candidates(N, N)):
            for tk in ([K] if full_k else _tile_candidates(K, K)):
                ni, nj, nk = M // tm, N // tn, K // tk
                vmem = 2 * (tm * tk * a_bytes + tk * tn * b_bytes + tm * tn * o_bytes * (2 if has_add else 1))
                vmem += tm * tn * 4 * (2 if nk > 1 else 1)
                vmem += (tm * tk * 2 * n_a if cast_a else 0) + (tk * tn * 2 * n_b if cast_b else 0)
                if vmem > MM_VMEM_BUDGET:
                    continue
                moved = M * K * a_bytes * (nj if nk > 1 else 1) + K * N * b_bytes * (1 if nj == nk == 1 else ni)
                moved += M * N * o_bytes * (2 if has_add else 1)
                t = max(moved / HBM_BYTES_PER_S, 2 * M * N * K / MXU_FLOPS_PER_S) + ni * nj * nk * STEP_OVERHEAD_S
                if best is None or t < best[0]:
                    best = (t, tm, tn, tk)
    assert best is not None, (M, N, K)
    return best[1:]


def _dot_raw(a, b, kind):
    nb = a.ndim - 2
    batch = ((0,), (0,)) if nb else ((), ())
    ca = nb if kind == "tn" else nb + 1
    cb = nb + 1 if kind == "nt" else nb
    return lax.dot_general(a.astype(BF16), b.astype(BF16), (((ca,), (cb,)), batch), preferred_element_type=F32)


@functools.partial(jax.custom_vjp, nondiff_argnums=(2,))
def _bdot(a, b, kind):
    return _dot_raw(a, b, kind)


def _bdot_fwd(a, b, kind):
    return _dot_raw(a, b, kind), (a, b)


def _bdot_bwd(kind, res, g):
    a, b = res
    if kind == "nn":
        da, db = _bdot(g, b, "nt"), _bdot(a, g, "tn")
    elif kind == "nt":
        da, db = _bdot(g, b, "nn"), _bdot(g, a, "tn")
    else:
        da, db = _bdot(b, g, "nt"), _bdot(a, g, "nn")
    return da.astype(a.dtype), db.astype(b.dtype)


_bdot.defvjp(_bdot_fwd, _bdot_bwd)


def _mm(a, b, *, name, ta=False, tb=False, add=None, out_dtype=F32, after=None):
    K, M = a.shape if ta else a.shape[::-1]
    N, Kb = b.shape if tb else b.shape[::-1]
    assert K == Kb, (a.shape, b.shape, ta, tb)
    tm, tn, tk = _mm_tiles(M, N, K, a.dtype.itemsize, b.dtype.itemsize, jnp.dtype(out_dtype).itemsize, add is not None)
    nk = K // tk
    kind = "tn" if ta else ("nt" if tb else "nn")
    assert not (ta and tb)
    a_spec = pl.BlockSpec((tk, tm), lambda i, j, k: (k, i)) if ta else pl.BlockSpec((tm, tk), lambda i, j, k: (i, k))
    b_spec = pl.BlockSpec((tn, tk), lambda i, j, k: (j, k)) if tb else pl.BlockSpec((tk, tn), lambda i, j, k: (k, j))
    o_spec = pl.BlockSpec((tm, tn), lambda i, j, k: (i, j))
    has_add = add is not None

    def body(*refs):
        a_ref, b_ref = refs[0], refs[1]
        add_ref = refs[2] if has_add else None
        o_ref = refs[n_in]
        part = _dot_raw(a_ref[...], b_ref[...], kind)
        if nk == 1:
            if has_add:
                part = part + add_ref[...].astype(F32)
            o_ref[...] = part.astype(o_ref.dtype)
            return
        acc_ref = refs[-1]
        k = pl.program_id(2)

        @pl.when(k == 0)
        def _():
            acc_ref[...] = part

        @pl.when(k > 0)
        def _():
            acc_ref[...] += part

        @pl.when(k == nk - 1)
        def _():
            r = acc_ref[...]
            if has_add:
                r = r + add_ref[...].astype(F32)
            o_ref[...] = r.astype(o_ref.dtype)

    ins = [a, b] + ([add] if has_add else []) + ([after] if after is not None else [])
    in_specs = [a_spec, b_spec] + ([o_spec] if has_add else []) + ([_ANY] if after is not None else [])
    n_in = len(ins)
    return pl.pallas_call(
        body, name=name, grid=(M // tm, N // tn, nk), in_specs=in_specs, out_specs=o_spec,
        out_shape=jax.ShapeDtypeStruct((M, N), out_dtype),
        scratch_shapes=[pltpu.VMEM((tm, tn), F32)] if nk > 1 else [],
        compiler_params=_params(("parallel", "parallel", "arbitrary")),
    )(*ins)


def _mm_fused(a, b, fn, tiles, params, out_dtypes, sums, *, name, ta=False, tb=False, full_rows=False, after=None,
              b2=None, a2=None):
    K, M = a.shape if ta else a.shape[::-1]
    N, Kb = b.shape if tb else b.shape[::-1]
    assert K == Kb and not (ta and tb), (a.shape, b.shape, ta, tb)
    per_elem = sum(t.dtype.itemsize for t in tiles) + sum(jnp.dtype(d).itemsize for d in out_dtypes)
    n_b = 1 if b2 is None else 2
    n_a = 1 if a2 is None else 2
    tm, tn, tk = _mm_tiles(M, N, K, a.dtype.itemsize, b.dtype.itemsize, per_elem, False, full_rows, b2 is not None,
                           n_a, n_b)
    nk = K // tk
    kind = "tn" if ta else ("nt" if tb else "nn")
    a_spec = pl.BlockSpec((tk, tm), lambda i, j, k: (k, i)) if ta else pl.BlockSpec((tm, tk), lambda i, j, k: (i, k))
    b_spec = pl.BlockSpec((tn, tk), lambda i, j, k: (j, k)) if tb else pl.BlockSpec((tk, tn), lambda i, j, k: (k, j))
    o_spec = pl.BlockSpec((tm, tn), lambda i, j, k: (i, j))
    ins = [a, b] + ([b2] if b2 is not None else []) + ([a2] if a2 is not None else [])
    ins += list(tiles) + list(params) + ([after] if after is not None else [])
    in_specs = [a_spec] + [b_spec] * n_b + [a_spec] * (n_a - 1) + [o_spec] * len(tiles)
    in_specs += [pl.BlockSpec(p.shape, lambda i, j, k, nd=p.ndim: (0,) * nd) for p in params]
    in_specs += [_ANY] if after is not None else []
    n_in, n_t, n_p, n_o = len(ins), len(tiles), len(params), len(out_dtypes)

    def body(*refs):
        outs, sum_refs = refs[n_in:n_in + n_o], refs[n_in + n_o:n_in + n_o + len(sums)]

        def finish(*products):
            res = fn(*products, *[t[...] for t in refs[n_a + n_b:n_a + n_b + n_t + n_p]])
            for o_ref, v in zip(outs, res[:n_o]):
                o_ref[...] = v.astype(o_ref.dtype)
            first = jnp.logical_and(pl.program_id(0) == 0, pl.program_id(1) == 0)
            for s_ref, v in zip(sum_refs, res[n_o:]):
                @pl.when(first)
                def _(s_ref=s_ref, v=v):
                    s_ref[...] = v

                @pl.when(jnp.logical_not(first))
                def _(s_ref=s_ref, v=v):
                    s_ref[...] += v

        part = _dot_raw(refs[0][...], refs[1][...], kind)
        if nk == 1 and a2 is not None:
            finish(part + _dot_raw(refs[3][...], refs[2][...], kind))
            return
        if nk == 1:
            finish(part, *([_dot_raw(refs[0][...], refs[2][...], kind)] if b2 is not None else []))
            return
        acc_ref = refs[-1]
        k = pl.program_id(2)

        @pl.when(k == 0)
        def _():
            acc_ref[...] = part

        @pl.when(k > 0)
        def _():
            acc_ref[...] += part

        @pl.when(k == nk - 1)
        def _():
            finish(acc_ref[...])

    out_shape = [_sds((M, N), d) for d in out_dtypes] + list(sums)
    out_specs = [o_spec] * n_o + [pl.BlockSpec(s.shape, lambda i, j, k, nd=len(s.shape): (0,) * nd) for s in sums]
    order = ("arbitrary",) * 3 if sums else ("parallel", "parallel", "arbitrary")
    return pl.pallas_call(
        body, name=name, grid=(M // tm, N // tn, nk), in_specs=in_specs, out_specs=out_specs, out_shape=out_shape,
        scratch_shapes=[pltpu.VMEM((tm, tn), F32)] if nk > 1 else [], compiler_params=_params(order),
    )(*ins)


def _cols(arr, width, block):
    return (arr, width, block)


def _stage(fn, tiles, params, out_tiles, out_sums, *, name, tile=TOK_TILE, after=None):
    def tok_spec(shape, width=None, block=0):
        if len(shape) == 2:
            w = shape[1] if width is None else width
            return pl.BlockSpec((tile, w), lambda i: (i, block))
        return pl.BlockSpec((shape[0], tile, shape[2]), lambda i: (0, i, 0))

    arrays, in_specs = [], []
    for t in tiles:
        if isinstance(t, tuple):
            arr, width, block = t
            arrays.append(arr)
            in_specs.append(tok_spec(arr.shape, width, block))
        else:
            arrays.append(t)
            in_specs.append(tok_spec(t.shape))
    n_tok = arrays[0].shape[0] if arrays[0].ndim == 2 else arrays[0].shape[1]
    for p in params:
        arrays.append(p)
        in_specs.append(pl.BlockSpec(p.shape, lambda i, nd=p.ndim: (0,) * nd))
    out_shape = list(out_tiles) + list(out_sums)
    out_specs = [tok_spec(o.shape) for o in out_tiles]
    out_specs += [pl.BlockSpec(o.shape, lambda i, nd=len(o.shape): (0,) * nd) for o in out_sums]
    n_fn, n_ot = len(arrays), len(out_tiles)
    if after is not None:
        arrays.append(after)
        in_specs.append(_ANY)
    n_in = len(arrays)

    def body(*refs):
        res = fn(*[r[...] for r in refs[:n_fn]])
        if not isinstance(res, (tuple, list)):
            res = (res,)
        outs = refs[n_in:]
        for o_ref, r in zip(outs[:n_ot], res[:n_ot]):
            o_ref[...] = r.astype(o_ref.dtype)
        i = pl.program_id(0)
        for o_ref, r in zip(outs[n_ot:], res[n_ot:]):
            @pl.when(i == 0)
            def _(o_ref=o_ref, r=r):
                o_ref[...] = r.astype(o_ref.dtype)

            @pl.when(i > 0)
            def _(o_ref=o_ref, r=r):
                o_ref[...] += r.astype(o_ref.dtype)

    res = pl.pallas_call(
        body, name=name, grid=(n_tok // tile,), in_specs=in_specs, out_specs=out_specs, out_shape=out_shape,
        compiler_params=_params(("arbitrary",)),
    )(*arrays)
    return res


def _sds(shape, dtype):
    return jax.ShapeDtypeStruct(tuple(shape), dtype)


def _sigmoid(x):
    return 0.5 * jnp.tanh(0.5 * x) + 0.5


def _rms(x, g):
    return x * lax.rsqrt(jnp.mean(x * x, axis=-1, keepdims=True) + EPS) * g


def _norm_bwd_fn(x, dh, dres, g):
    _, vjp = jax.vjp(_rms, x, g)
    dx, dg = vjp(dh)
    return dx + dres, dg


def _mla_a_fn(cq, ckv, g_qa, g_kva):
    return _rms(cq, g_qa), _rms(ckv, g_kva)


def _mla_a_bwd_fn(cq, ckv, dqn, dkvn, g_qa, g_kva):
    _, vjp = jax.vjp(_mla_a_fn, cq, ckv, g_qa, g_kva)
    return vjp((dqn, dkvn))


def _rope(t, cos, sin):
    t1, t2 = t[:, :QK_ROPE // 2], t[:, QK_ROPE // 2:]
    return jnp.concatenate([t1 * cos - t2 * sin, t1 * sin + t2 * cos], axis=-1)


def _mla_b_fn(q_raw, kv_raw, kr, cos, sin, g_qn, g_kn):
    krope = kr[:, :QK_ROPE]
    qs, ks, vs = [], [], []
    for h in range(MLA_HEADS):
        qh = _rms(q_raw[:, h * QK_HEAD:(h + 1) * QK_HEAD], g_qn)
        kvh = kv_raw[:, h * (QK_NOPE + V_HEAD):(h + 1) * (QK_NOPE + V_HEAD)]
        kh = _rms(jnp.concatenate([kvh[:, :QK_NOPE], krope], axis=-1), g_kn)
        qs.append(jnp.concatenate([qh[:, :QK_NOPE], _rope(qh[:, QK_NOPE:], cos, sin)], axis=-1))
        ks.append(jnp.concatenate([kh[:, :QK_NOPE], _rope(kh[:, QK_NOPE:], cos, sin)], axis=-1))
        vs.append(kvh[:, QK_NOPE:])
    return jnp.stack(qs), jnp.stack(ks), jnp.stack(vs)


def _mla_b_bwd_fn(q_raw, kv_raw, kr, cos, sin, dq, dk, dv, g_qn, g_kn):
    _, vjp = jax.vjp(lambda a, b, c, d, e: _mla_b_fn(a, b, c, cos, sin, d, e), q_raw, kv_raw, kr, g_qn, g_kn)
    return vjp((dq, dk, dv))


def _post_fn(a, o_f, o_b, hg, g_hgo):
    o = o_f + o_b
    parts = [a]
    for h in range(HG_HEADS):
        s = slice(h * HG_DK, (h + 1) * HG_DK)
        gate = hg[:, s]
        parts.append(_rms(o[:, s], g_hgo[:, s]) * (gate * _sigmoid(gate)))
    return jnp.concatenate(parts, axis=-1)


def _post_bwd_fn(o_f, o_b, hg, dr, g_hgo):
    def f(o, hg, g):
        return _post_fn(jnp.zeros_like(o), o, jnp.zeros_like(o), hg, g)[:, o.shape[1]:]
    _, vjp = jax.vjp(f, o_f + o_b, hg, g_hgo)
    return vjp(dr)


def _swiglu_fn(gt, up):
    return gt * _sigmoid(gt) * up


def _resid_norm_fn(acc, x, g):
    x_new = acc + x
    return x_new, _rms(x_new, g)


def _swiglu_bwd_fn(gt, up, dact):
    s = _sigmoid(gt)
    silu = gt * s
    return dact * up * (s + silu * (1.0 - s)), dact * silu


def _ple_loss_fn(x2, pg, pp, target):
    gate = _sigmoid(pg)
    err = x2 + gate * pp - target
    dx3 = err * (1.0 / err.shape[-1])
    loss = 0.5 * jnp.sum(jnp.mean(err * err, axis=-1, keepdims=True), axis=0, keepdims=True)
    return dx3, dx3 * pp * gate * (1.0 - gate), dx3 * gate, loss


def _attention_fwd(q, k, v):
    H, T, D = q.shape
    DV = v.shape[-1]
    tq, ck = min(ATT_TQ, T), min(ATT_CHUNK, T)
    c2 = (D ** -0.5) * LOG2_E

    def body(q_ref, k_ref, v_ref, o_ref, lse_ref):
        q_i = q_ref[0]

        def chunk(c, carry):
            m, l, acc = carry
            rows = pl.ds(pl.multiple_of(c * ck, ck), ck)
            s = _dot_raw(q_i, k_ref[0, rows, :], "nt")
            m_new = jnp.maximum(m, jnp.max(s, axis=-1, keepdims=True))
            p = jnp.exp2((s - m_new) * c2)
            alpha = jnp.exp2((m - m_new) * c2)
            l = l * alpha + jnp.sum(p, axis=-1, keepdims=True)
            acc = acc * alpha + _dot_raw(p, v_ref[0, rows, :], "nn")
            return m_new, l, acc

        init = (jnp.full((tq, 1), -jnp.inf, F32), jnp.zeros((tq, 1), F32), jnp.zeros((tq, DV), F32))
        m, l, acc = lax.fori_loop(0, T // ck, chunk, init, unroll=True)
        o_ref[...] = acc / l
        lse_ref[0] = m * c2 + jnp.log2(l)

    return pl.pallas_call(
        body, name="attention_fwd", grid=(H, T // tq),
        in_specs=[pl.BlockSpec((1, tq, D), lambda h, i: (h, i, 0)),
                  pl.BlockSpec((1, T, D), lambda h, i: (h, 0, 0)),
                  pl.BlockSpec((1, T, DV), lambda h, i: (h, 0, 0))],
        out_specs=[pl.BlockSpec((tq, DV), lambda h, i: (i, h)),
                   pl.BlockSpec((1, tq, 1), lambda h, i: (h, i, 0))],
        out_shape=[_sds((T, H * DV), F32), _sds((H, T, 1), F32)],
        compiler_params=_params(("parallel", "parallel")),
    )(q, k, v)


def _attention_bwd(q, k, v, o, lse2, dmix):
    H, T, D = q.shape
    DV = v.shape[-1]
    tk, cq = min(ATT_TK, T), min(ATT_CHUNK, T)
    scale = D ** -0.5
    c2 = scale * LOG2_E

    def body(q_ref, k_ref, v_ref, o_ref, lse_ref, do_ref, dq_ref, dk_ref, dv_ref, delta_ref):
        j = pl.program_id(1)

        @pl.when(j == 0)
        def _():
            delta = lax.dot_general(jnp.ones((8, DV), F32), do_ref[...] * o_ref[...], (((1,), (1,)), ((), ())),
                                    precision=lax.Precision.HIGHEST, preferred_element_type=F32)
            for i in range(T // cq):
                delta_ref[i] = delta[:, i * cq:(i + 1) * cq]
            dq_ref[0] = jnp.zeros((T, D), F32)

        k_j, v_j = k_ref[0], v_ref[0]
        dk_ref[0] = jnp.zeros((tk, D), F32)
        dv_ref[0] = jnp.zeros((tk, DV), F32)

        def chunk(c, carry):
            rows = pl.ds(pl.multiple_of(c * cq, cq), cq)
            q_c = q_ref[0, rows, :]
            do_c = do_ref[rows, :].astype(BF16)
            st = _dot_raw(k_j, q_c, "nt")
            pt = jnp.exp2(st * c2 - lse_ref[0, c])
            dv_ref[0] += _dot_raw(pt, do_c, "nn")
            dpt = _dot_raw(v_j, do_c, "nt")
            dst = pt * (dpt - delta_ref[c, 0:1, :]) * scale
            dk_ref[0] += _dot_raw(dst, q_c, "nn")
            dq_ref[0, rows, :] += _dot_raw(dst, k_j, "tn")
            return carry

        lax.fori_loop(0, T // cq, chunk, 0, unroll=True)

    return pl.pallas_call(
        body, name="attention_bwd", grid=(H, T // tk),
        in_specs=[pl.BlockSpec((1, T, D), lambda h, j: (h, 0, 0)),
                  pl.BlockSpec((1, tk, D), lambda h, j: (h, j, 0)),
                  pl.BlockSpec((1, tk, DV), lambda h, j: (h, j, 0)),
                  pl.BlockSpec((T, DV), lambda h, j: (0, h)),
                  pl.BlockSpec((1, T // cq, 1, cq), lambda h, j: (h, 0, 0, 0)),
                  pl.BlockSpec((T, DV), lambda h, j: (0, h))],
        out_specs=[pl.BlockSpec((1, T, D), lambda h, j: (h, 0, 0)),
                   pl.BlockSpec((1, tk, D), lambda h, j: (h, j, 0)),
                   pl.BlockSpec((1, tk, DV), lambda h, j: (h, j, 0))],
        out_shape=[_sds((H, T, D), F32), _sds((H, T, D), F32), _sds((H, T, DV), F32)],
        scratch_shapes=[pltpu.VMEM((T // cq, 8, cq), F32)],
        compiler_params=_params(("parallel", "arbitrary")),
    )(q, k, v, o, lse2.reshape(H, T // cq, 1, cq), dmix)


def _split3_dot(ones, x, kind):
    hi = x.astype(BF16)
    rest = x - hi.astype(F32)
    mid = rest.astype(BF16)
    lo = (rest - mid.astype(F32)).astype(BF16)
    return (_dot_raw(ones, hi, kind) + _dot_raw(ones, mid, kind)) + _dot_raw(ones, lo, kind)


@jax.custom_vjp
def _running_sum(x, tri):
    return _split3_dot(tri, x, "nn")


def _running_sum_fwd(x, tri):
    return _split3_dot(tri, x, "nn"), tri


def _running_sum_bwd(tri, g):
    return _split3_dot(tri, g, "tn"), jnp.zeros_like(tri)


_running_sum.defvjp(_running_sum_fwd, _running_sum_bwd)


def _gla_block(hq, hf, hi, lower, st_in, *, rev, dot):
    rows, dk = hq.shape
    G, C = rows // CHUNK, CHUNK
    q = hq * _sigmoid(hq)
    f = lower + (1.0 - lower) * _sigmoid(hf)
    k = 1.0 - f
    logf = jnp.log2(f)
    q3, k3, v3, lf3 = (t.reshape(G, C, dk) for t in (q, k, hi, logf))
    r = lax.broadcasted_iota(jnp.int32, (C, C), 0)
    c = lax.broadcasted_iota(jnp.int32, (C, C), 1)
    tri = ((r <= c) if rev else (r >= c)).astype(F32)
    b = _running_sum(lf3, jnp.broadcast_to(tri, (G, C, C)))
    tpos = lax.broadcasted_iota(jnp.int32, (1, C, 1), 1)
    first_half = (tpos >= C // 2) if rev else (tpos <= C // 2 - 1)
    b_mid = jnp.sum(jnp.where(first_half, lf3, 0.0), axis=1, keepdims=True)
    b_last = jnp.sum(lf3, axis=1, keepdims=True)
    a = dot(q3 * jnp.exp2(b - b_mid), k3 * jnp.exp2(b_mid - b), "nt") * tri
    o_intra = dot(a, v3, "nn")
    kv_t = dot(v3, k3 * jnp.exp2(b_last - b), "tn")
    decay = jnp.exp2(b_last)
    qd = q3 * jnp.exp2(b)
    st = st_in
    o_inter = [None] * G
    for g in (reversed(range(G)) if rev else range(G)):
        o_inter[g] = dot(qd[g], st, "nt")
        st = st * decay[g] + kv_t[g]
    o = o_intra.reshape(rows, dk) + jnp.concatenate(o_inter, axis=0)
    return o, st


def _gla_fwd(z, lower3, *, rev, col_q, col_f, col_v, hp):
    T = z.shape[0]
    rows = min(GLA_GROUP * CHUNK, T)
    nb = T // rows
    wide = hp * HG_DK
    blk = (lambda n: nb - 1 - n) if rev else (lambda n: n)

    def body(hq_ref, hf_ref, hi_ref, low_ref, o_ref, st_out_ref, st_ref):
        @pl.when(pl.program_id(1) == 0)
        def _():
            st_ref[...] = jnp.zeros_like(st_ref)

        st_in = [st_ref[i] for i in range(hp)]
        heads = []
        for i in range(hp):
            cols = slice(i * HG_DK, (i + 1) * HG_DK)
            heads.append(_gla_block(hq_ref[:, cols], hf_ref[:, cols], hi_ref[:, cols], low_ref[i], st_in[i], rev=rev,
                                    dot=_dot_raw))
        for i, (o, st) in enumerate(heads):
            st_out_ref[i, 0] = st_in[i]
            o_ref[:, i * HG_DK:(i + 1) * HG_DK] = o
            st_ref[i] = st

    def zspec(col):
        return pl.BlockSpec((rows, wide), lambda h, n: (blk(n), col // wide + h))

    return pl.pallas_call(
        body, name="gla_fwd_rev" if rev else "gla_fwd", grid=(HG_HEADS // hp, nb),
        in_specs=[zspec(col_q), zspec(col_f), zspec(col_v), pl.BlockSpec((hp, 1, HG_DK), lambda h, n: (h, 0, 0))],
        out_specs=[pl.BlockSpec((rows, wide), lambda h, n: (blk(n), h)),
                   pl.BlockSpec((hp, 1, HG_DK, HG_DK), lambda h, n: (h, blk(n), 0, 0))],
        out_shape=[_sds((T, HG_HEADS * HG_DK), F32), _sds((HG_HEADS, nb, HG_DK, HG_DK), F32)],
        scratch_shapes=[pltpu.VMEM((hp, HG_DK, HG_DK), F32)],
        compiler_params=_params(("parallel", "arbitrary")),
    )(z, z, z, lower3)


def _gla_bwd(z, lower3, states, do, prev, *, rev, col_q, col_f, col_v, hp):
    T = z.shape[0]
    rows = min(GLA_GROUP * CHUNK, T)
    nb = T // rows
    wide = hp * HG_DK
    blk = (lambda n: n) if rev else (lambda n: nb - 1 - n)
    has_prev = prev is not None
    fn = functools.partial(_gla_block, rev=rev, dot=_bdot)

    def body(*refs):
        hq_ref, hf_ref, hi_ref, low_ref, st_ref, do_ref = refs[:6]
        rest = refs[6:]
        if has_prev:
            pq_ref, pi_ref = rest[:2]
            rest = rest[2:]
        dhq_ref, dhi_ref, dhf_ref, dlow_ref, dst_ref = rest
        n = pl.program_id(1)

        @pl.when(n == 0)
        def _():
            dst_ref[...] = jnp.zeros_like(dst_ref)

        dst_in = [dst_ref[i] for i in range(hp)]
        heads = []
        for i in range(hp):
            cols = slice(i * HG_DK, (i + 1) * HG_DK)
            _, vjp = jax.vjp(fn, hq_ref[:, cols], hf_ref[:, cols], hi_ref[:, cols], low_ref[i], st_ref[i, 0])
            dhq, dhf, dhi, dlow, dst = vjp((do_ref[:, cols], dst_in[i]))
            if has_prev:
                dhq = dhq + pq_ref[:, cols]
                dhi = dhi + pi_ref[:, cols]
            heads.append((dhq, dhf, dhi, dlow, dst))
        for i, (dhq, dhf, dhi, dlow, dst) in enumerate(heads):
            cols = slice(i * HG_DK, (i + 1) * HG_DK)
            dst_ref[i] = dst
            dhq_ref[:, cols] = dhq.astype(dhq_ref.dtype)
            dhi_ref[:, cols] = dhi.astype(dhi_ref.dtype)
            dhf_ref[:, cols] = dhf.astype(dhf_ref.dtype)

        @pl.when(n == 0)
        def _():
            for i in range(hp):
                dlow_ref[i] = heads[i][3]

        @pl.when(n > 0)
        def _():
            for i in range(hp):
                dlow_ref[i] += heads[i][3]

    def zspec(col):
        return pl.BlockSpec((rows, wide), lambda h, n: (blk(n), col // wide + h))

    hspec = pl.BlockSpec((rows, wide), lambda h, n: (blk(n), h))
    in_specs = [zspec(col_q), zspec(col_f), zspec(col_v), pl.BlockSpec((hp, 1, HG_DK), lambda h, n: (h, 0, 0)),
                pl.BlockSpec((hp, 1, HG_DK, HG_DK), lambda h, n: (h, blk(n), 0, 0)), hspec]
    ins = [z, z, z, lower3, states, do]
    if has_prev:
        in_specs += [hspec, hspec]
        ins += list(prev)
    full_wide = HG_HEADS * HG_DK
    acc_dtype = BF16 if has_prev else F32
    return pl.pallas_call(
        body, name="gla_bwd_rev" if rev else "gla_bwd", grid=(HG_HEADS // hp, nb),
        in_specs=in_specs,
        out_specs=[hspec, hspec, hspec, pl.BlockSpec((hp, 1, HG_DK), lambda h, n: (h, 0, 0))],
        out_shape=[_sds((T, full_wide), acc_dtype), _sds((T, full_wide), acc_dtype), _sds((T, full_wide), BF16),
                   _sds((HG_HEADS, 1, HG_DK), F32)],
        scratch_shapes=[pltpu.VMEM((hp, HG_DK, HG_DK), F32)],
        compiler_params=_params(("parallel", "arbitrary")),
    )(*ins)


def _lower_fn(lb):
    e = jnp.exp(lb - jnp.max(lb, axis=0, keepdims=True))
    return (e / jnp.sum(e, axis=0, keepdims=True))[0]


def _lower_bounds(lb):
    def body(lb_ref, o_ref):
        o_ref[...] = _lower_fn(lb_ref[...])
    return pl.pallas_call(body, name="lower_bounds", out_shape=_sds(lb.shape[1:], F32))(lb)


def _row_tile(r, cap=1024):
    best = None
    for t in range(16, min(r, cap) + 1, 16):
        if r % t == 0:
            best = t
    return best if best is not None else r


def _sum4(shards, recv, chip, *, name):
    _, R, C = shards.shape
    tr = _row_tile(R)

    def body(chip_ref, o_ref, r_ref, out_ref):
        out_ref[...] = ((o_ref[0].astype(F32) + r_ref[0].astype(F32)) + r_ref[1].astype(F32)) + r_ref[2].astype(F32)

    grid_spec = pltpu.PrefetchScalarGridSpec(
        num_scalar_prefetch=1, grid=(R // tr,),
        in_specs=[pl.BlockSpec((1, tr, C), lambda i, chip_ref: (chip_ref[0], i, 0)),
                  pl.BlockSpec((3, tr, C), lambda i, chip_ref: (0, i, 0))],
        out_specs=pl.BlockSpec((tr, C), lambda i, chip_ref: (i, 0)))
    return pl.pallas_call(
        body, name=name, grid_spec=grid_spec, out_shape=_sds((R, C), F32), compiler_params=_params(("parallel",)),
    )(chip, shards, recv)


def _adamw_math(w, g, m, v):
    m = ADAM_B1 * m + (1.0 - ADAM_B1) * g
    v = ADAM_B2 * v + (1.0 - ADAM_B2) * (g * g)
    m_hat = m / (1.0 - ADAM_B1 ** ADAM_STEP)
    v_hat = v / (1.0 - ADAM_B2 ** ADAM_STEP)
    delta = -ADAM_LR * (m_hat / (jnp.sqrt(v_hat) + ADAM_EPS) + ADAM_WD * w)
    return delta, m, v


def _adamw(w, g_a, g_b, m, v, *, name):
    R, C = w.shape
    tr = _row_tile(R)
    two = g_b is not None

    def body(*refs):
        w_ref, ga_ref = refs[0], refs[1]
        rest = refs[2:]
        g = ga_ref[...]
        if two:
            g = g + rest[0][...]
            rest = rest[1:]
        m_ref, v_ref, g_out, d_out, m_out, v_out, token = rest
        delta, m_new, v_new = _adamw_math(w_ref[...], g, m_ref[...], v_ref[...])
        g_out[...] = g
        d_out[...] = delta
        m_out[...] = m_new
        v_out[...] = v_new
        token[...] = jnp.zeros_like(token)

    spec = pl.BlockSpec((tr, C), lambda i: (i, 0))
    ins = [w, g_a] + ([g_b] if two else []) + [m, v]
    return pl.pallas_call(
        body, name=name, grid=(R // tr,), in_specs=[spec] * len(ins),
        out_specs=[spec] * 4 + [pl.BlockSpec((8, LANE), lambda i: (0, 0))],
        out_shape=[_sds((R, C), F32)] * 4 + [_sds((8, LANE), F32)], compiler_params=_params(("arbitrary",)),
    )(*ins)


def _adamw_small(red, lb_full, ws, ms, vs):
    n = len(ws)

    def pieces(shape):
        out = []
        for j, idx in enumerate(itertools.product(*[range(d) for d in shape[:-1]])):
            out.append((idx[:-1] + (slice(idx[-1], idx[-1] + 1), slice(None)), j * shape[-1]))
        return out

    def body(*refs):
        red_ref, lb_ref = refs[0], refs[1]
        w_refs, m_refs, v_refs = refs[2:2 + n], refs[2 + n:2 + 2 * n], refs[2 + 2 * n:2 + 3 * n]
        out_refs = refs[2 + 3 * n:]
        chip = 2 * lax.axis_index("x") + lax.axis_index("y")
        n_f, shard = lb_ref.shape[-1], w_refs[n - 1].shape[-1]
        _, vjp = jax.vjp(_lower_fn, lb_ref[...])
        dlb = vjp(red_ref[8:10, 0:n_f])[0]
        for i in range(n):
            width = w_refs[i].shape[-1]
            for j, (at, lane) in enumerate(pieces(w_refs[i].shape)):
                if i < n - 1:
                    g = red_ref[i:i + 1, lane:lane + width]
                else:
                    row = dlb[j // 2][j % 2:j % 2 + 1]
                    g = sum(jnp.where(chip == q, row[:, q * shard:(q + 1) * shard], 0.0) for q in range(N_CHIPS))
                delta, m_new, v_new = _adamw_math(w_refs[i][at], g, m_refs[i][at], v_refs[i][at])
                for o_ref, val in zip(out_refs[4 * i:4 * i + 4], (g, delta, m_new, v_new)):
                    o_ref[at] = val

    return pl.pallas_call(
        body, name="adamw_small", out_shape=[_sds(w.shape, F32) for w in ws for _ in range(4)],
    )(red, lb_full, *ws, *ms, *vs)


def _chip_peers():
    x, y, c = lax.axis_index("x"), lax.axis_index("y"), lax.axis_index("c")
    return (x, y, c), 2 * x + y, [(1 - x, y), (x, 1 - y), (1 - x, 1 - y)]


_HBM = pl.BlockSpec(memory_space=pltpu.HBM)
_SEM = pl.BlockSpec(memory_space=pltpu.SEMAPHORE)
_EFFECT = pltpu.SideEffectType.DATAFLOW_SIDE_EFFECTING


def _exchange_copies(srcs, lands, sems, mode):
    (x, y, c), me, chips = _chip_peers()
    copies = []
    for t, (src, land) in enumerate(zip(srcs, lands)):
        if mode == "swap":
            copies.append(pltpu.make_async_remote_copy(src, land, sems[0].at[3 * t], sems[1].at[3 * t],
                                                       device_id=(x, y, 1 - c), device_id_type=MESH))
            continue
        for k, (px, py) in enumerate(chips):
            gather = mode == "gather"
            copies.append(pltpu.make_async_remote_copy(
                src if gather else src.at[2 * px + py], land.at[me] if gather else land.at[k],
                sems[0].at[3 * t + k], sems[1].at[3 * t + k], device_id=(px, py, c), device_id_type=MESH))
        if mode == "gather":
            copies.append(pltpu.make_async_copy(src, land.at[me], sems[2].at[t]))
    return copies


def _exchange_start(srcs, *, mode, name, after=None):
    n = len(srcs)
    n_sem = 3 if mode == "gather" else 2
    n_in = 2 * n + (after is not None)
    land_shape = {"gather": lambda s: (N_CHIPS,) + s.shape, "scatter": lambda s: (3,) + s.shape[1:], "swap": lambda s: s.shape}
    lands = [_sds(land_shape[mode](s), s.dtype) for s in srcs]

    def body(*refs):
        for cp in _exchange_copies(refs[:n], refs[n:2 * n], refs[n_in:n_in + n_sem], mode):
            cp.start()
        token = refs[-1]
        token[...] = jnp.zeros_like(token)

    sem_shapes = [pltpu.SemaphoreType.DMA((3 * n,)), pltpu.SemaphoreType.DMA((3 * n,))]
    sem_shapes += [pltpu.SemaphoreType.DMA((n,))] if mode == "gather" else []
    thru = [pltpu.HBM(s.shape, s.dtype) for s in srcs] + [pltpu.HBM(l.shape, l.dtype) for l in lands]
    res = pl.pallas_call(
        body, name=name, in_specs=[_HBM] * (2 * n) + [_ANY] * (after is not None),
        out_specs=[_SEM] * n_sem + [_HBM] * (2 * n) + [pl.BlockSpec(memory_space=pltpu.VMEM)],
        out_shape=sem_shapes + thru + [_sds((8, LANE), F32)], input_output_aliases={i: n_sem + i for i in range(2 * n)},
        compiler_params=pltpu.CompilerParams(has_side_effects=_EFFECT),
    )(*[pltpu.with_memory_space_constraint(s, pltpu.HBM) for s in srcs],
      *[pltpu.with_memory_space_constraint(lax.empty(l.shape, l.dtype), pltpu.HBM) for l in lands],
      *([after] if after is not None else []))
    return (res[:n_sem], res[n_sem:n_sem + n], res[n_sem + n:n_sem + 2 * n], mode), res[-1]


def _exchange_wait(started, after, *, name):
    sems, srcs, lands, mode = started
    n, n_sem = len(srcs), len(sems)
    after = list(after) if isinstance(after, (list, tuple)) else [after]

    def body(*refs):
        for cp in _exchange_copies(refs[:n], refs[n:2 * n], refs[2 * n:2 * n + n_sem], mode):
            cp.wait()

    res = pl.pallas_call(
        body, name=name, in_specs=[_HBM] * (2 * n) + [_SEM] * n_sem + [_ANY] * len(after), out_specs=[_HBM] * (2 * n),
        out_shape=[pltpu.HBM(a.shape, a.dtype) for a in list(srcs) + list(lands)],
        input_output_aliases={i: i for i in range(2 * n)},
        compiler_params=pltpu.CompilerParams(has_side_effects=_EFFECT),
    )(*srcs, *lands, *sems, *after)
    return res[:n], res[n:]


def _allreduce_small(pack, after):
    R, C = pack.shape

    def body(in_ref, after_ref, out_ref, slots, send_sems, recv_sems):
        x, y, c = lax.axis_index("x"), lax.axis_index("y"), lax.axis_index("c")
        me = 4 * x + 2 * y + c
        slots[me] = in_ref[...]
        copies = []
        for k in range(1, N_DEV):
            peer = (x ^ ((k >> 2) & 1), y ^ ((k >> 1) & 1), c ^ (k & 1))
            cp = pltpu.make_async_remote_copy(in_ref, slots.at[me], send_sems.at[k - 1], recv_sems.at[k - 1],
                                              device_id=peer, device_id_type=MESH)
            cp.start()
            copies.append(cp)
        for cp in copies:
            cp.wait()
        acc = slots[0]
        for d in range(1, N_DEV):
            acc = acc + slots[d]
        out_ref[...] = acc

    return pl.pallas_call(
        body, name="allreduce_small", out_shape=_sds((R, C), F32),
        in_specs=[pl.BlockSpec(memory_space=pltpu.VMEM), _ANY], out_specs=pl.BlockSpec(memory_space=pltpu.VMEM),
        scratch_shapes=[pltpu.VMEM((N_DEV, R, C), F32), pltpu.SemaphoreType.DMA((N_DEV - 1,)),
                        pltpu.SemaphoreType.DMA((N_DEV - 1,))],
        compiler_params=_params(),
    )(pack, after)


_Z_CQ, _Z_CKV, _Z_HQ, _Z_HFF, _Z_HFB, _Z_HI, _Z_HG, _Z_KR, _Z_END = 0, 256, 512, 1024, 1536, 2048, 2560, 3072, 3200


def _to_z_layout(wt):
    pad = jnp.zeros((_Z_END - _Z_KR - QK_ROPE, wt.shape[1]), wt.dtype)
    return jnp.concatenate([wt[:512], wt[512 + QK_ROPE:], wt[512:512 + QK_ROPE], pad], axis=0)


def _from_z_layout(wt):
    return jnp.concatenate([wt[:512], wt[_Z_KR:_Z_KR + QK_ROPE], wt[512:_Z_KR]], axis=0)


def _col_shards_to_full(g):
    return jnp.transpose(g, (1, 0, 2)).reshape(g.shape[1], -1)


def _full_to_col_shards(w):
    r, c = w.shape
    return jnp.transpose(w.reshape(r, N_CHIPS, c // N_CHIPS), (1, 0, 2))


def _full_to_row_shards(w):
    r, c = w.shape
    return w.reshape(N_CHIPS, r // N_CHIPS, c)


def kernel(x, p, positions, g_mix, w_in, g_qa, g_kva, w_qb, w_kvb, g_qn, g_kn, lb_param, g_hgo, w_o, g_ffn, w_gate, w_up, w_down, g_ple, w_ple_gate, w_ple_proj, loss_target, m_g_mix, m_w_in, m_g_qa, m_g_kva, m_w_qb, m_w_kvb, m_g_qn, m_g_kn, m_lb_param, m_g_hgo, m_w_o, m_g_ffn, m_w_gate, m_w_up, m_w_down, m_g_ple, m_w_ple_gate, m_w_ple_proj, v_g_mix, v_w_in, v_g_qa, v_g_kva, v_w_qb, v_w_kvb, v_g_qn, v_g_kn, v_lb_param, v_g_hgo, v_w_o, v_g_ffn, v_w_gate, v_w_up, v_w_down, v_g_ple, v_w_ple_gate, v_w_ple_proj):
    w_named = dict(g_mix=g_mix, w_in=w_in, g_qa=g_qa, g_kva=g_kva, w_qb=w_qb, w_kvb=w_kvb, g_qn=g_qn, g_kn=g_kn,
                   lb_param=lb_param, g_hgo=g_hgo, w_o=w_o, g_ffn=g_ffn, w_gate=w_gate, w_up=w_up, w_down=w_down,
                   g_ple=g_ple, w_ple_gate=w_ple_gate, w_ple_proj=w_ple_proj)
    m_named = dict(g_mix=m_g_mix, w_in=m_w_in, g_qa=m_g_qa, g_kva=m_g_kva, w_qb=m_w_qb, w_kvb=m_w_kvb, g_qn=m_g_qn,
                   g_kn=m_g_kn, lb_param=m_lb_param, g_hgo=m_g_hgo, w_o=m_w_o, g_ffn=m_g_ffn, w_gate=m_w_gate,
                   w_up=m_w_up, w_down=m_w_down, g_ple=m_g_ple, w_ple_gate=m_w_ple_gate, w_ple_proj=m_w_ple_proj)
    v_named = dict(g_mix=v_g_mix, w_in=v_w_in, g_qa=v_g_qa, g_kva=v_g_kva, w_qb=v_w_qb, w_kvb=v_w_kvb, g_qn=v_g_qn,
                   g_kn=v_g_kn, lb_param=v_lb_param, g_hgo=v_g_hgo, w_o=v_w_o, g_ffn=v_g_ffn, w_gate=v_w_gate,
                   w_up=v_w_up, w_down=v_w_down, g_ple=v_g_ple, w_ple_gate=v_w_ple_gate, w_ple_proj=v_w_ple_proj)
    order = list(w_named)
    transposed = ("w_in", "w_qb", "w_gate", "w_up")
    col_sharded = ("w_kvb", "w_ple_proj")
    row_sharded = ("w_o", "w_down", "w_ple_gate")
    big = transposed + col_sharded + row_sharded

    def view(n, a):
        return jnp.transpose(a[0]) if n in transposed else a[0]

    def unview(n, a):
        return (jnp.transpose(a) if n in transposed else a)[None]

    def to_shards(n, g):
        return _full_to_col_shards(g) if n in col_sharded else _full_to_row_shards(g)

    x2d, p2d, tgt = x[0], p[0, 0], loss_target[0]
    T, D = x2d.shape

    lb_flat = lb_param.reshape(-1, lb_param.shape[-1])
    gather_groups = (("w_in",), ("w_qb", "w_kvb"), ("w_o", "w_gate", "w_up", "w_down", "w_ple_gate", "w_ple_proj"))
    gather_started = []

    casts = {n: view(n, w_named[n]).astype(BF16) for n in big}

    def gather_start(gi, after):
        srcs = [casts[n] for n in gather_groups[gi]] + ([lb_flat] if gi == 0 else [])
        started, token = _exchange_start(srcs, mode="gather", name=f"gather_start_{gi}", after=after)
        gather_started.append(started)
        return token

    full = {}

    def gather_wait(gi, after):
        _, got = _exchange_wait(gather_started[gi], after, name=f"gather_wait_{gi}")
        for n, g in zip(gather_groups[gi], got):
            full[n] = _col_shards_to_full(g) if n in col_sharded else g.reshape(-1, g.shape[-1])
        return got

    g_hgo_row = g_hgo.reshape(1, -1)

    inv_freq = ROPE_THETA ** (-jnp.arange(0, QK_ROPE, 2, dtype=F32) / QK_ROPE)
    ang = positions[0].astype(F32)[:, None] * inv_freq
    cos, sin = jnp.cos(ang), jnp.sin(ang)
    token = gather_start(0, None)
    h1 = _stage(_rms, [x2d], [g_mix], [_sds((T, D), BF16)], [], name="norm_mix", after=token)[0]
    got = gather_wait(0, [h1, cos, sin] + [casts[n] for g in gather_groups[1:] for n in g])
    token = got[0]
    for gi in range(1, len(gather_groups)):
        token = gather_start(gi, token)
    lb_full = _col_shards_to_full(got[-1]).reshape(lb_param.shape[0], lb_param.shape[1], -1)
    w_in_zt = _to_z_layout(full["w_in"])
    z = _mm(h1, w_in_zt, tb=True, name="in_proj", after=token)
    qn, kvn = _stage(_mla_a_fn, [_cols(z, 256, 0), _cols(z, 256, 1)], [g_qa, g_kva],
                     [_sds((T, 256), BF16), _sds((T, 256), BF16)], [], name="mla_latent_norm")
    gather_wait(1, qn)
    q_raw = _mm(qn, full["w_qb"], tb=True, name="q_up")
    kv_raw = _mm(kvn, full["w_kvb"], name="kv_up")
    kr = _cols(z, LANE, _Z_KR // LANE)
    q, k, v = _stage(_mla_b_fn, [q_raw, kv_raw, kr, cos, sin], [g_qn, g_kn],
                     [_sds((MLA_HEADS, T, QK_HEAD), BF16), _sds((MLA_HEADS, T, QK_HEAD), BF16),
                      _sds((MLA_HEADS, T, V_HEAD), BF16)], [], name="mla_qk_norm_rope")
    att, lse = _attention_fwd(q, k, v)

    lower = _lower_bounds(lb_full)
    lower3 = lower.reshape(2, HG_HEADS, 1, HG_DK)
    o_f, st_f = _gla_fwd(z, lower3[0], rev=False, col_q=_Z_HQ, col_f=_Z_HFF, col_v=_Z_HI, hp=GLA_FWD_HEADS)
    o_b, st_b = _gla_fwd(z, lower3[1], rev=True, col_q=_Z_HQ, col_f=_Z_HFB, col_v=_Z_HI, hp=GLA_FWD_HEADS)
    hg = _cols(z, 512, _Z_HG // 512)
    mix = _stage(_post_fn, [att, o_f, o_b, hg], [g_hgo_row], [_sds((T, att.shape[1] + o_f.shape[1]), BF16)], [],
                 name="mix_out")[0]
    gather_wait(2, mix)
    x1, h2 = _mm_fused(mix, full["w_o"], _resid_norm_fn, [x2d], [g_ffn], [F32, BF16], [], full_rows=True,
                       name="out_proj")
    gt, up, act = _mm_fused(h2, full["w_gate"], lambda gt, up: (gt, up, _swiglu_fn(gt, up)), [], [], [BF16, BF16, BF16],
                            [], tb=True, b2=full["w_up"], name="ffn_gate_up")
    x2, h3 = _mm_fused(act, full["w_down"], _resid_norm_fn, [x1], [g_ple], [F32, BF16], [], full_rows=True,
                       name="ffn_down")
    pp = _mm(p2d, full["w_ple_proj"], name="ple_proj")
    dx3, dpg, dpp, loss_part = _mm_fused(
        h3, full["w_ple_gate"], lambda acc, pp, x2, tgt: _ple_loss_fn(x2, acc, pp, tgt), [pp, x2, tgt], [],
        [F32, BF16, BF16], [_sds((1, 1), F32)], full_rows=True, name="ple_gate_loss")

    grads = {}
    scatter_groups = (("w_ple_proj", "w_ple_gate", "w_down", "w_gate", "w_up", "w_o"), ("w_qb", "w_kvb", "w_in"))
    scatter_started = []

    def scatter_start(gi):
        srcs = [to_shards(n, grads[n]) for n in scatter_groups[gi]]
        started, token = _exchange_start(srcs, mode="scatter", name=f"scatter_start_{gi}")
        scatter_started.append(started)
        return token

    chip = 2 * lax.axis_index("x") + lax.axis_index("y")
    swap_started = []

    def reduce_group(gi, after):
        shards, recvs = _exchange_wait(scatter_started[gi], after, name=f"scatter_wait_{gi}")
        sums = [_sum4(s, r, chip.reshape(1), name="sum_" + n) for n, s, r in zip(scatter_groups[gi], shards, recvs)]
        started, token = _exchange_start(sums, mode="swap", name=f"swap_start_{gi}")
        swap_started.append(started)
        return token

    grads["w_ple_proj"] = _mm(p2d, dpp, ta=True, out_dtype=BF16, name="d_w_ple_proj")
    grads["w_ple_gate"] = _mm(h3, dpg, ta=True, out_dtype=BF16, name="d_w_ple_gate")
    dx2, grads["g_ple"] = _mm_fused(
        dpg, full["w_ple_gate"], lambda acc, x2, dx3, g: _norm_bwd_fn(x2, acc, dx3, g), [x2, dx3], [g_ple],
        [F32], [_sds((1, D), F32)], tb=True, full_rows=True, name="d_h3_norm_ple_bwd")
    dgt, dup = _mm_fused(dx2, full["w_down"], lambda acc, gt, up: _swiglu_bwd_fn(gt.astype(F32), up.astype(F32), acc), [gt, up], [],
                         [BF16, BF16], [], tb=True, name="d_act_swiglu_bwd")
    grads["w_down"] = _mm(act, dx2, ta=True, out_dtype=BF16, name="d_w_down")
    grads["w_gate"] = _mm(dgt, h2, ta=True, out_dtype=BF16, name="d_w_gate")
    grads["w_up"] = _mm(dup, h2, ta=True, out_dtype=BF16, name="d_w_up")
    dx1, grads["g_ffn"] = _mm_fused(
        dgt, full["w_gate"], lambda acc, x1, dx2, g: _norm_bwd_fn(x1, acc, dx2, g), [x1, dx2], [g_ffn],
        [F32], [_sds((1, D), F32)], full_rows=True, a2=dup, b2=full["w_up"], name="d_h2_norm_ffn_bwd")
    grads["w_o"] = _mm(mix, dx1, ta=True, out_dtype=BF16, name="d_w_o")
    token = scatter_start(0)
    dmix = _mm(dx1, full["w_o"], tb=True, name="d_mix", after=token)

    half = MLA_HEADS * V_HEAD
    do, dhg, dg_hgo = _stage(_post_bwd_fn, [o_f, o_b, hg, _cols(dmix, half, 1)], [g_hgo_row],
                             [_sds((T, half), F32), _sds((T, half), BF16)], [_sds((1, half), F32)], name="mix_out_bwd")
    grads["g_hgo"] = dg_hgo
    dhq_f, dhi_f, dhf_f, dlow_f = _gla_bwd(z, lower3[0], st_f, do, None, rev=False,
                                           col_q=_Z_HQ, col_f=_Z_HFF, col_v=_Z_HI, hp=GLA_BWD_HEADS)
    dhq, dhi, dhf_b, dlow_b = _gla_bwd(z, lower3[1], st_b, do, (dhq_f, dhi_f), rev=True,
                                       col_q=_Z_HQ, col_f=_Z_HFB, col_v=_Z_HI, hp=GLA_BWD_HEADS)

    dq, dk, dv = _attention_bwd(q, k, v, att, lse, dmix)
    dq_raw, dkv_raw, dkr, grads["g_qn"], grads["g_kn"] = _stage(
        _mla_b_bwd_fn, [q_raw, kv_raw, kr, cos, sin, dq, dk, dv], [g_qn, g_kn],
        [_sds(q_raw.shape, BF16), _sds(kv_raw.shape, BF16), _sds((T, LANE), BF16)],
        [_sds(g_qn.shape, F32), _sds(g_kn.shape, F32)], name="mla_qk_norm_rope_bwd")
    grads["w_qb"] = _mm(dq_raw, qn, ta=True, out_dtype=BF16, name="d_w_qb")
    grads["w_kvb"] = _mm(kvn, dkv_raw, ta=True, out_dtype=BF16, name="d_w_kvb")
    dqn = _mm(dq_raw, full["w_qb"], name="d_qn")
    dkvn = _mm(dkv_raw, full["w_kvb"], tb=True, name="d_kvn")
    dcq, dckv, grads["g_qa"], grads["g_kva"] = _stage(
        _mla_a_bwd_fn, [_cols(z, 256, 0), _cols(z, 256, 1), dqn, dkvn], [g_qa, g_kva],
        [_sds((T, 256), BF16), _sds((T, 256), BF16)], [_sds(g_qa.shape, F32), _sds(g_kva.shape, F32)],
        name="mla_latent_norm_bwd")
    token = reduce_group(0, dcq)
    dz = jnp.concatenate([dcq, dckv, dhq, dhf_f, dhf_b, dhi, dhg, dkr], axis=1)
    grads["w_in"] = _from_z_layout(_mm(dz, h1, ta=True, out_dtype=BF16, name="d_w_in", after=token))
    token = scatter_start(1)
    grad_x, grads["g_mix"] = _mm_fused(
        dz, w_in_zt, lambda acc, x, dx1, g: _norm_bwd_fn(x, acc, dx1, g), [x2d, dx1], [g_mix],
        [F32], [_sds((1, D), F32)], full_rows=True, name="d_h1_norm_mix_bwd", after=token)

    out_g, out_d, out_m, out_v = {}, {}, {}, {}

    def update_group(gi, after):
        mine, theirs = _exchange_wait(swap_started[gi], after, name=f"swap_wait_{gi}")
        tokens = []
        for n, a, b in zip(scatter_groups[gi], mine, theirs):
            *res, token = _adamw(view(n, w_named[n]), a, b, view(n, m_named[n]), view(n, v_named[n]), name="adamw_" + n)
            out_g[n], out_d[n], out_m[n], out_v[n] = (unview(n, t) for t in res)
            tokens.append(token)
        return tokens

    done = update_group(0, grads["g_mix"])
    token = reduce_group(1, done)
    done = update_group(1, token)

    small = ("g_mix", "g_qa", "g_kva", "g_qn", "g_kn", "g_hgo", "g_ffn", "g_ple")
    small_all = small + ("lb_param",)
    width = -(-max(w_named[n].size for n in small_all) // LANE) * LANE

    def row(a):
        a = a.reshape(1, -1)
        return jnp.pad(a, ((0, 0), (0, width - a.shape[1])))

    dlower = jnp.concatenate([dlow_f.reshape(1, -1), dlow_b.reshape(1, -1)], axis=0)
    pack = jnp.concatenate([row(grads[n]) for n in small] + [row(dlower[0]), row(dlower[1]), row(loss_part)]
                           + [jnp.zeros((5, width), F32)], axis=0)
    red = _allreduce_small(pack, done[-1])
    loss = red[10, 0]

    outs = _adamw_small(red, lb_full, [w_named[n] for n in small_all], [m_named[n] for n in small_all],
                        [v_named[n] for n in small_all])
    for i, n in enumerate(small_all):
        out_g[n], out_d[n], out_m[n], out_v[n] = outs[4 * i:4 * i + 4]

    return (loss, grad_x[None], *[out_g[n] for n in order], *[out_d[n] for n in order],
            *[out_m[n] for n in order], *[out_v[n] for n in order])
```

```python
import functools
import itertools

import jax
import jax.numpy as jnp
from jax import lax
from jax.experimental import pallas as pl
from jax.experimental.pallas import tpu as pltpu

F32 = jnp.float32
BF16 = jnp.bfloat16
MESH = pl.DeviceIdType.MESH

EPS = 1e-6
ROPE_THETA = 10000.0
MLA_HEADS = 4
QK_NOPE = 128
QK_ROPE = 64
QK_HEAD = QK_NOPE + QK_ROPE
V_HEAD = 128
HG_HEADS = 4
HG_DK = 128
CHUNK = 64
ADAM_LR = 0.001
ADAM_B1 = 0.9
ADAM_B2 = 0.999
ADAM_EPS = 1e-08
ADAM_WD = 0.01
ADAM_STEP = 10

LANE = 128
VMEM_LIMIT = 56 * 1024 * 1024
TOK_TILE = 256
GLA_GROUP = 16
GLA_FWD_HEADS = 4
GLA_BWD_HEADS = 2
ATT_TQ = 1024
ATT_TK = 1024
ATT_CHUNK = 512
LOG2_E = 1.4426950408889634
N_CHIPS = 4
N_DEV = 8


_ANY = pl.BlockSpec(memory_space=pl.ANY)


def _params(dims=None, **kw):
    return pltpu.CompilerParams(dimension_semantics=dims, vmem_limit_bytes=VMEM_LIMIT, **kw)


def _tile_candidates(n, cap):
    out = [d for d in range(LANE, min(n, cap) + 1, LANE) if n % d == 0]
    if n <= cap and n not in out:
        out.append(n)
    return out or [n]


MM_VMEM_BUDGET = 40 * 1024 * 1024
MM_MIN_ROWS = 256
MM_MAX_ROWS = 1536
HBM_BYTES_PER_S = 2.8e12
MXU_FLOPS_PER_S = 8e14
STEP_OVERHEAD_S = 0.35e-6


def _mm_tiles(M, N, K, a_bytes, b_bytes, o_bytes, has_add, full_rows=False, full_k=False, n_a=1, n_b=1):
    cast_a, cast_b = a_bytes > 2, b_bytes > 2
    a_bytes, b_bytes = n_a * a_bytes, n_b * b_bytes
    best = None
    for tm in [t for t in _tile_candidates(M, MM_MAX_ROWS) if t >= min(M, MM_MIN_ROWS)]:
        for tn in ([N] if full_rows else _tile_candidates(N, N)):
            for tk in ([K] if full_k else _tile_candidates(K, K)):
                ni, nj, nk = M // tm, N // tn, K // tk
                vmem = 2 * (tm * tk * a_bytes + tk * tn * b_bytes + tm * tn * o_bytes * (2 if has_add else 1))
                vmem += tm * tn * 4 * (2 if nk > 1 else 1)
                vmem += (tm * tk * 2 * n_a if cast_a else 0) + (tk * tn * 2 * n_b if cast_b else 0)
                if vmem > MM_VMEM_BUDGET:
                    continue
                moved = M * K * a_bytes * (nj if nk > 1 else 1) + K * N * b_bytes * (1 if nj == nk == 1 else ni)
                moved += M * N * o_bytes * (2 if has_add else 1)
                t = max(moved / HBM_BYTES_PER_S, 2 * M * N * K / MXU_FLOPS_PER_S) + ni * nj * nk * STEP_OVERHEAD_S
                if best is None or t < best[0]:
                    best = (t, tm, tn, tk)
    assert best is not None, (M, N, K)
    return best[1:]


def _dot_raw(a, b, kind):
    nb = a.ndim - 2
    batch = ((0,), (0,)) if nb else ((), ())
    ca = nb if kind == "tn" else nb + 1
    cb = nb + 1 if kind == "nt" else nb
    return lax.dot_general(a.astype(BF16), b.astype(BF16), (((ca,), (cb,)), batch), preferred_element_type=F32)


@functools.partial(jax.custom_vjp, nondiff_argnums=(2,))
def _bdot(a, b, kind):
    return _dot_raw(a, b, kind)


def _bdot_fwd(a, b, kind):
    return _dot_raw(a, b, kind), (a, b)


def _bdot_bwd(kind, res, g):
    a, b = res
    if kind == "nn":
        da, db = _bdot(g, b, "nt"), _bdot(a, g, "tn")
    elif kind == "nt":
        da, db = _bdot(g, b, "nn"), _bdot(g, a, "tn")
    else:
        da, db = _bdot(b, g, "nt"), _bdot(a, g, "nn")
    return da.astype(a.dtype), db.astype(b.dtype)


_bdot.defvjp(_bdot_fwd, _bdot_bwd)


def _mm(a, b, *, name, ta=False, tb=False, add=None, out_dtype=F32, after=None):
    K, M = a.shape if ta else a.shape[::-1]
    N, Kb = b.shape if tb else b.shape[::-1]
    assert K == Kb, (a.shape, b.shape, ta, tb)
    tm, tn, tk = _mm_tiles(M, N, K, a.dtype.itemsize, b.dtype.itemsize, jnp.dtype(out_dtype).itemsize, add is not None)
    nk = K // tk
    kind = "tn" if ta else ("nt" if tb else "nn")
    assert not (ta and tb)
    a_spec = pl.BlockSpec((tk, tm), lambda i, j, k: (k, i)) if ta else pl.BlockSpec((tm, tk), lambda i, j, k: (i, k))
    b_spec = pl.BlockSpec((tn, tk), lambda i, j, k: (j, k)) if tb else pl.BlockSpec((tk, tn), lambda i, j, k: (k, j))
    o_spec = pl.BlockSpec((tm, tn), lambda i, j, k: (i, j))
    has_add = add is not None

    def body(*refs):
        a_ref, b_ref = refs[0], refs[1]
        add_ref = refs[2] if has_add else None
        o_ref = refs[n_in]
        part = _dot_raw(a_ref[...], b_ref[...], kind)
        if nk == 1:
            if has_add:
                part = part + add_ref[...].astype(F32)
            o_ref[...] = part.astype(o_ref.dtype)
            return
        acc_ref = refs[-1]
        k = pl.program_id(2)

        @pl.when(k == 0)
        def _():
            acc_ref[...] = part

        @pl.when(k > 0)
        def _():
            acc_ref[...] += part

        @pl.when(k == nk - 1)
        def _():
            r = acc_ref[...]
            if has_add:
                r = r + add_ref[...].astype(F32)
            o_ref[...] = r.astype(o_ref.dtype)

    ins = [a, b] + ([add] if has_add else []) + ([after] if after is not None else [])
    in_specs = [a_spec, b_spec] + ([o_spec] if has_add else []) + ([_ANY] if after is not None else [])
    n_in = len(ins)
    return pl.pallas_call(
        body, name=name, grid=(M // tm, N // tn, nk), in_specs=in_specs, out_specs=o_spec,
        out_shape=jax.ShapeDtypeStruct((M, N), out_dtype),
        scratch_shapes=[pltpu.VMEM((tm, tn), F32)] if nk > 1 else [],
        compiler_params=_params(("parallel", "parallel", "arbitrary")),
    )(*ins)


def _mm_fused(a, b, fn, tiles, params, out_dtypes, sums, *, name, ta=False, tb=False, full_rows=False, after=None,
              b2=None, a2=None):
    K, M = a.shape if ta else a.shape[::-1]
    N, Kb = b.shape if tb else b.shape[::-1]
    assert K == Kb and not (ta and tb), (a.shape, b.shape, ta, tb)
    per_elem = sum(t.dtype.itemsize for t in tiles) + sum(jnp.dtype(d).itemsize for d in out_dtypes)
    n_b = 1 if b2 is None else 2
    n_a = 1 if a2 is None else 2
    tm, tn, tk = _mm_tiles(M, N, K, a.dtype.itemsize, b.dtype.itemsize, per_elem, False, full_rows, b2 is not None,
                           n_a, n_b)
    nk = K // tk
    kind = "tn" if ta else ("nt" if tb else "nn")
    a_spec = pl.BlockSpec((tk, tm), lambda i, j, k: (k, i)) if ta else pl.BlockSpec((tm, tk), lambda i, j, k: (i, k))
    b_spec = pl.BlockSpec((tn, tk), lambda i, j, k: (j, k)) if tb else pl.BlockSpec((tk, tn), lambda i, j, k: (k, j))
    o_spec = pl.BlockSpec((tm, tn), lambda i, j, k: (i, j))
    ins = [a, b] + ([b2] if b2 is not None else []) + ([a2] if a2 is not None else [])
    ins += list(tiles) + list(params) + ([after] if after is not None else [])
    in_specs = [a_spec] + [b_spec] * n_b + [a_spec] * (n_a - 1) + [o_spec] * len(tiles)
    in_specs += [pl.BlockSpec(p.shape, lambda i, j, k, nd=p.ndim: (0,) * nd) for p in params]
    in_specs += [_ANY] if after is not None else []
    n_in, n_t, n_p, n_o = len(ins), len(tiles), len(params), len(out_dtypes)

    def body(*refs):
        outs, sum_refs = refs[n_in:n_in + n_o], refs[n_in + n_o:n_in + n_o + len(sums)]

        def finish(*products):
            res = fn(*products, *[t[...] for t in refs[n_a + n_b:n_a + n_b + n_t + n_p]])
            for o_ref, v in zip(outs, res[:n_o]):
                o_ref[...] = v.astype(o_ref.dtype)
            first = jnp.logical_and(pl.program_id(0) == 0, pl.program_id(1) == 0)
            for s_ref, v in zip(sum_refs, res[n_o:]):
                @pl.when(first)
                def _(s_ref=s_ref, v=v):
                    s_ref[...] = v

                @pl.when(jnp.logical_not(first))
                def _(s_ref=s_ref, v=v):
                    s_ref[...] += v

        part = _dot_raw(refs[0][...], refs[1][...], kind)
        if nk == 1 and a2 is not None:
            finish(part + _dot_raw(refs[3][...], refs[2][...], kind))
            return
        if nk == 1:
            finish(part, *([_dot_raw(refs[0][...], refs[2][...], kind)] if b2 is not None else []))
            return
        acc_ref = refs[-1]
        k = pl.program_id(2)

        @pl.when(k == 0)
        def _():
            acc_ref[...] = part

        @pl.when(k > 0)
        def _():
            acc_ref[...] += part

        @pl.when(k == nk - 1)
        def _():
            finish(acc_ref[...])

    out_shape = [_sds((M, N), d) for d in out_dtypes] + list(sums)
    out_specs = [o_spec] * n_o + [pl.BlockSpec(s.shape, lambda i, j, k, nd=len(s.shape): (0,) * nd) for s in sums]
    order = ("arbitrary",) * 3 if sums else ("parallel", "parallel", "arbitrary")
    return pl.pallas_call(
        body, name=name, grid=(M // tm, N // tn, nk), in_specs=in_specs, out_specs=out_specs, out_shape=out_shape,
        scratch_shapes=[pltpu.VMEM((tm, tn), F32)] if nk > 1 else [], compiler_params=_params(order),
    )(*ins)


def _cols(arr, width, block):
    return (arr, width, block)


def _stage(fn, tiles, params, out_tiles, out_sums, *, name, tile=TOK_TILE, after=None):
    def tok_spec(shape, width=None, block=0):
        if len(shape) == 2:
            w = shape[1] if width is None else width
            return pl.BlockSpec((tile, w), lambda i: (i, block))
        return pl.BlockSpec((shape[0], tile, shape[2]), lambda i: (0, i, 0))

    arrays, in_specs = [], []
    for t in tiles:
        if isinstance(t, tuple):
            arr, width, block = t
            arrays.append(arr)
            in_specs.append(tok_spec(arr.shape, width, block))
        else:
            arrays.append(t)
            in_specs.append(tok_spec(t.shape))
    n_tok = arrays[0].shape[0] if arrays[0].ndim == 2 else arrays[0].shape[1]
    for p in params:
        arrays.append(p)
        in_specs.append(pl.BlockSpec(p.shape, lambda i, nd=p.ndim: (0,) * nd))
    out_shape = list(out_tiles) + list(out_sums)
    out_specs = [tok_spec(o.shape) for o in out_tiles]
    out_specs += [pl.BlockSpec(o.shape, lambda i, nd=len(o.shape): (0,) * nd) for o in out_sums]
    n_fn, n_ot = len(arrays), len(out_tiles)
    if after is not None:
        arrays.append(after)
        in_specs.append(_ANY)
    n_in = len(arrays)

    def body(*refs):
        res = fn(*[r[...] for r in refs[:n_fn]])
        if not isinstance(res, (tuple, list)):
            res = (res,)
        outs = refs[n_in:]
        for o_ref, r in zip(outs[:n_ot], res[:n_ot]):
            o_ref[...] = r.astype(o_ref.dtype)
        i = pl.program_id(0)
        for o_ref, r in zip(outs[n_ot:], res[n_ot:]):
            @pl.when(i == 0)
            def _(o_ref=o_ref, r=r):
                o_ref[...] = r.astype(o_ref.dtype)

            @pl.when(i > 0)
            def _(o_ref=o_ref, r=r):
                o_ref[...] += r.astype(o_ref.dtype)

    res = pl.pallas_call(
        body, name=name, grid=(n_tok // tile,), in_specs=in_specs, out_specs=out_specs, out_shape=out_shape,
        compiler_params=_params(("arbitrary",)),
    )(*arrays)
    return res


def _sds(shape, dtype):
    return jax.ShapeDtypeStruct(tuple(shape), dtype)


def _sigmoid(x):
    return 0.5 * jnp.tanh(0.5 * x) + 0.5


def _rms(x, g):
    return x * lax.rsqrt(jnp.mean(x * x, axis=-1, keepdims=True) + EPS) * g


def _norm_bwd_fn(x, dh, dres, g):
    r = lax.rsqrt(jnp.mean(x * x, axis=-1, keepdims=True) + EPS)
    xr = x * r
    dhg = dh * g
    dx = r * (dhg - xr * jnp.mean(xr * dhg, axis=-1, keepdims=True))
    return dx + dres, jnp.sum(dh * xr, axis=0, keepdims=True)


def _mla_a_fn(cq, ckv, g_qa, g_kva):
    return _rms(cq, g_qa), _rms(ckv, g_kva)


def _mla_a_bwd_fn(cq, ckv, dqn, dkvn, g_qa, g_kva):
    _, vjp = jax.vjp(_mla_a_fn, cq, ckv, g_qa, g_kva)
    return vjp((dqn, dkvn))


def _rope(t, cos, sin):
    t1, t2 = t[:, :QK_ROPE // 2], t[:, QK_ROPE // 2:]
    return jnp.concatenate([t1 * cos - t2 * sin, t1 * sin + t2 * cos], axis=-1)


def _mla_b_fn(q_raw, kv_raw, kr, cos, sin, g_qn, g_kn):
    krope = kr[:, :QK_ROPE]
    qs, ks, vs = [], [], []
    for h in range(MLA_HEADS):
        qh = _rms(q_raw[:, h * QK_HEAD:(h + 1) * QK_HEAD], g_qn)
        kvh = kv_raw[:, h * (QK_NOPE + V_HEAD):(h + 1) * (QK_NOPE + V_HEAD)]
        kh = _rms(jnp.concatenate([kvh[:, :QK_NOPE], krope], axis=-1), g_kn)
        qs.append(jnp.concatenate([qh[:, :QK_NOPE], _rope(qh[:, QK_NOPE:], cos, sin)], axis=-1))
        ks.append(jnp.concatenate([kh[:, :QK_NOPE], _rope(kh[:, QK_NOPE:], cos, sin)], axis=-1))
        vs.append(kvh[:, QK_NOPE:])
    return jnp.stack(qs), jnp.stack(ks), jnp.stack(vs)


def _mla_b_bwd_fn(q_raw, kv_raw, kr, cos, sin, dq, dk, dv, g_qn, g_kn):
    _, vjp = jax.vjp(lambda a, b, c, d, e: _mla_b_fn(a, b, c, cos, sin, d, e), q_raw, kv_raw, kr, g_qn, g_kn)
    return vjp((dq, dk, dv))


def _post_fn(a, o_f, o_b, hg, g_hgo):
    o = o_f + o_b
    parts = [a]
    for h in range(HG_HEADS):
        s = slice(h * HG_DK, (h + 1) * HG_DK)
        gate = hg[:, s]
        parts.append(_rms(o[:, s], g_hgo[:, s]) * (gate * _sigmoid(gate)))
    return jnp.concatenate(parts, axis=-1)


def _post_bwd_fn(o_f, o_b, hg, dr, g_hgo):
    def f(o, hg, g):
        return _post_fn(jnp.zeros_like(o), o, jnp.zeros_like(o), hg, g)[:, o.shape[1]:]
    _, vjp = jax.vjp(f, o_f + o_b, hg, g_hgo)
    return vjp(dr)


def _swiglu_fn(gt, up):
    return gt * _sigmoid(gt) * up


def _resid_norm_fn(acc, x, g):
    x_new = acc + x
    return x_new, _rms(x_new, g)


def _swiglu_bwd_fn(gt, up, dact):
    s = _sigmoid(gt)
    silu = gt * s
    return dact * up * (s + silu * (1.0 - s)), dact * silu


def _ple_loss_fn(x2, pg, pp, target):
    gate = _sigmoid(pg)
    err = x2 + gate * pp - target
    dx3 = err * (1.0 / err.shape[-1])
    loss = 0.5 * jnp.sum(jnp.mean(err * err, axis=-1, keepdims=True), axis=0, keepdims=True)
    return dx3, dx3 * pp * gate * (1.0 - gate), dx3 * gate, loss


def _attention_fwd(q, k, v):
    H, T, D = q.shape
    DV = v.shape[-1]
    tq, ck = min(ATT_TQ, T), min(ATT_CHUNK, T)
    c2 = (D ** -0.5) * LOG2_E

    def body(q_ref, k_ref, v_ref, o_ref, lse_ref):
        q_i = q_ref[0]

        def chunk(c, carry):
            m, l, acc = carry
            rows = pl.ds(pl.multiple_of(c * ck, ck), ck)
            s = _dot_raw(q_i, k_ref[0, rows, :], "nt")
            m_new = jnp.maximum(m, jnp.max(s, axis=-1, keepdims=True))
            p = jnp.exp2((s - m_new) * c2)
            alpha = jnp.exp2((m - m_new) * c2)
            l = l * alpha + jnp.sum(p, axis=-1, keepdims=True)
            acc = acc * alpha + _dot_raw(p, v_ref[0, rows, :], "nn")
            return m_new, l, acc

        init = (jnp.full((tq, 1), -jnp.inf, F32), jnp.zeros((tq, 1), F32), jnp.zeros((tq, DV), F32))
        m, l, acc = lax.fori_loop(0, T // ck, chunk, init, unroll=True)
        o_ref[...] = acc / l
        lse_ref[0] = m * c2 + jnp.log2(l)

    return pl.pallas_call(
        body, name="attention_fwd", grid=(H, T // tq),
        in_specs=[pl.BlockSpec((1, tq, D), lambda h, i: (h, i, 0)),
                  pl.BlockSpec((1, T, D), lambda h, i: (h, 0, 0)),
                  pl.BlockSpec((1, T, DV), lambda h, i: (h, 0, 0))],
        out_specs=[pl.BlockSpec((tq, DV), lambda h, i: (i, h)),
                   pl.BlockSpec((1, tq, 1), lambda h, i: (h, i, 0))],
        out_shape=[_sds((T, H * DV), F32), _sds((H, T, 1), F32)],
        compiler_params=_params(("parallel", "parallel")),
    )(q, k, v)


def _attention_bwd(q, k, v, o, lse2, dmix):
    H, T, D = q.shape
    DV = v.shape[-1]
    tk, cq = min(ATT_TK, T), min(ATT_CHUNK, T)
    scale = D ** -0.5
    c2 = scale * LOG2_E

    def body(q_ref, k_ref, v_ref, o_ref, lse_ref, do_ref, dq_ref, dk_ref, dv_ref, delta_ref):
        j = pl.program_id(1)

        @pl.when(j == 0)
        def _():
            delta = lax.dot_general(jnp.ones((8, DV), F32), do_ref[...] * o_ref[...], (((1,), (1,)), ((), ())),
                                    precision=lax.Precision.HIGHEST, preferred_element_type=F32)
            for i in range(T // cq):
                delta_ref[i] = delta[:, i * cq:(i + 1) * cq]
            dq_ref[0] = jnp.zeros((T, D), F32)

        k_j, v_j = k_ref[0], v_ref[0]
        dk_ref[0] = jnp.zeros((tk, D), F32)
        dv_ref[0] = jnp.zeros((tk, DV), F32)

        def chunk(c, carry):
            rows = pl.ds(pl.multiple_of(c * cq, cq), cq)
            q_c = q_ref[0, rows, :]
            do_c = do_ref[rows, :].astype(BF16)
            st = _dot_raw(k_j, q_c, "nt")
            pt = jnp.exp2(st * c2 - lse_ref[0, c])
            dv_ref[0] += _dot_raw(pt, do_c, "nn")
            dpt = _dot_raw(v_j, do_c, "nt")
            dst = pt * (dpt - delta_ref[c, 0:1, :]) * scale
            dk_ref[0] += _dot_raw(dst, q_c, "nn")
            dq_ref[0, rows, :] += _dot_raw(dst, k_j, "tn")
            return carry

        lax.fori_loop(0, T // cq, chunk, 0, unroll=True)

    return pl.pallas_call(
        body, name="attention_bwd", grid=(H, T // tk),
        in_specs=[pl.BlockSpec((1, T, D), lambda h, j: (h, 0, 0)),
                  pl.BlockSpec((1, tk, D), lambda h, j: (h, j, 0)),
                  pl.BlockSpec((1, tk, DV), lambda h, j: (h, j, 0)),
                  pl.BlockSpec((T, DV), lambda h, j: (0, h)),
                  pl.BlockSpec((1, T // cq, 1, cq), lambda h, j: (h, 0, 0, 0)),
                  pl.BlockSpec((T, DV), lambda h, j: (0, h))],
        out_specs=[pl.BlockSpec((1, T, D), lambda h, j: (h, 0, 0)),
                   pl.BlockSpec((1, tk, D), lambda h, j: (h, j, 0)),
                   pl.BlockSpec((1, tk, DV), lambda h, j: (h, j, 0))],
        out_shape=[_sds((H, T, D), F32), _sds((H, T, D), F32), _sds((H, T, DV), F32)],
        scratch_shapes=[pltpu.VMEM((T // cq, 8, cq), F32)],
        compiler_params=_params(("parallel", "arbitrary")),
    )(q, k, v, o, lse2.reshape(H, T // cq, 1, cq), dmix)


def _split3_dot(ones, x, kind):
    hi = x.astype(BF16)
    rest = x - hi.astype(F32)
    mid = rest.astype(BF16)
    lo = (rest - mid.astype(F32)).astype(BF16)
    return (_dot_raw(ones, hi, kind) + _dot_raw(ones, mid, kind)) + _dot_raw(ones, lo, kind)


@jax.custom_vjp
def _running_sum(x, tri):
    return _split3_dot(tri, x, "nn")


def _running_sum_fwd(x, tri):
    return _split3_dot(tri, x, "nn"), tri


def _running_sum_bwd(tri, g):
    return _split3_dot(tri, g, "tn"), jnp.zeros_like(tri)


_running_sum.defvjp(_running_sum_fwd, _running_sum_bwd)


def _gla_block(hq, hf, hi, lower, st_in, *, rev, dot):
    rows, dk = hq.shape
    G, C = rows // CHUNK, CHUNK
    q = hq * _sigmoid(hq)
    f = lower + (1.0 - lower) * _sigmoid(hf)
    k = 1.0 - f
    logf = jnp.log2(f)
    q3, k3, v3, lf3 = (t.reshape(G, C, dk) for t in (q, k, hi, logf))
    r = lax.broadcasted_iota(jnp.int32, (C, C), 0)
    c = lax.broadcasted_iota(jnp.int32, (C, C), 1)
    tri = ((r <= c) if rev else (r >= c)).astype(F32)
    b = _running_sum(lf3, jnp.broadcast_to(tri, (G, C, C)))
    tpos = lax.broadcasted_iota(jnp.int32, (1, C, 1), 1)
    first_half = (tpos >= C // 2) if rev else (tpos <= C // 2 - 1)
    b_mid = jnp.sum(jnp.where(first_half, lf3, 0.0), axis=1, keepdims=True)
    b_last = jnp.sum(lf3, axis=1, keepdims=True)
    a = dot(q3 * jnp.exp2(b - b_mid), k3 * jnp.exp2(b_mid - b), "nt") * tri
    o_intra = dot(a, v3, "nn")
    kv_t = dot(v3, k3 * jnp.exp2(b_last - b), "tn")
    decay = jnp.exp2(b_last)
    qd = q3 * jnp.exp2(b)
    st = st_in
    o_inter = [None] * G
    for g in (reversed(range(G)) if rev else range(G)):
        o_inter[g] = dot(qd[g], st, "nt")
        st = st * decay[g] + kv_t[g]
    o = o_intra.reshape(rows, dk) + jnp.concatenate(o_inter, axis=0)
    return o, st


def _gla_fwd(z, lower3, *, rev, col_q, col_f, col_v, hp):
    T = z.shape[0]
    rows = min(GLA_GROUP * CHUNK, T)
    nb = T // rows
    wide = hp * HG_DK
    blk = (lambda n: nb - 1 - n) if rev else (lambda n: n)

    def body(hq_ref, hf_ref, hi_ref, low_ref, o_ref, st_out_ref, st_ref):
        @pl.when(pl.program_id(1) == 0)
        def _():
            st_ref[...] = jnp.zeros_like(st_ref)

        st_in = [st_ref[i] for i in range(hp)]
        heads = []
        for i in range(hp):
            cols = slice(i * HG_DK, (i + 1) * HG_DK)
            heads.append(_gla_block(hq_ref[:, cols], hf_ref[:, cols], hi_ref[:, cols], low_ref[i], st_in[i], rev=rev,
                                    dot=_dot_raw))
        for i, (o, st) in enumerate(heads):
            st_out_ref[i, 0] = st_in[i]
            o_ref[:, i * HG_DK:(i + 1) * HG_DK] = o
            st_ref[i] = st

    def zspec(col):
        return pl.BlockSpec((rows, wide), lambda h, n: (blk(n), col // wide + h))

    return pl.pallas_call(
        body, name="gla_fwd_rev" if rev else "gla_fwd", grid=(HG_HEADS // hp, nb),
        in_specs=[zspec(col_q), zspec(col_f), zspec(col_v), pl.BlockSpec((hp, 1, HG_DK), lambda h, n: (h, 0, 0))],
        out_specs=[pl.BlockSpec((rows, wide), lambda h, n: (blk(n), h)),
                   pl.BlockSpec((hp, 1, HG_DK, HG_DK), lambda h, n: (h, blk(n), 0, 0))],
        out_shape=[_sds((T, HG_HEADS * HG_DK), F32), _sds((HG_HEADS, nb, HG_DK, HG_DK), F32)],
        scratch_shapes=[pltpu.VMEM((hp, HG_DK, HG_DK), F32)],
        compiler_params=_params(("parallel", "arbitrary")),
    )(z, z, z, lower3)


def _gla_bwd(z, lower3, states, do, prev, *, rev, col_q, col_f, col_v, hp):
    T = z.shape[0]
    rows = min(GLA_GROUP * CHUNK, T)
    nb = T // rows
    wide = hp * HG_DK
    blk = (lambda n: n) if rev else (lambda n: nb - 1 - n)
    has_prev = prev is not None
    fn = functools.partial(_gla_block, rev=rev, dot=_bdot)

    def body(*refs):
        hq_ref, hf_ref, hi_ref, low_ref, st_ref, do_ref = refs[:6]
        rest = refs[6:]
        if has_prev:
            pq_ref, pi_ref = rest[:2]
            rest = rest[2:]
        dhq_ref, dhi_ref, dhf_ref, dlow_ref, dst_ref = rest
        n = pl.program_id(1)

        @pl.when(n == 0)
        def _():
            dst_ref[...] = jnp.zeros_like(dst_ref)

        dst_in = [dst_ref[i] for i in range(hp)]
        heads = []
        for i in range(hp):
            cols = slice(i * HG_DK, (i + 1) * HG_DK)
            _, vjp = jax.vjp(fn, hq_ref[:, cols], hf_ref[:, cols], hi_ref[:, cols], low_ref[i], st_ref[i, 0])
            dhq, dhf, dhi, dlow, dst = vjp((do_ref[:, cols], dst_in[i]))
            if has_prev:
                dhq = dhq + pq_ref[:, cols]
                dhi = dhi + pi_ref[:, cols]
            heads.append((dhq, dhf, dhi, dlow, dst))
        for i, (dhq, dhf, dhi, dlow, dst) in enumerate(heads):
            cols = slice(i * HG_DK, (i + 1) * HG_DK)
            dst_ref[i] = dst
            dhq_ref[:, cols] = dhq.astype(dhq_ref.dtype)
            dhi_ref[:, cols] = dhi.astype(dhi_ref.dtype)
            dhf_ref[:, cols] = dhf.astype(dhf_ref.dtype)

        @pl.when(n == 0)
        def _():
            for i in range(hp):
                dlow_ref[i] = heads[i][3]

        @pl.when(n > 0)
        def _():
            for i in range(hp):
                dlow_ref[i] += heads[i][3]

    def zspec(col):
        return pl.BlockSpec((rows, wide), lambda h, n: (blk(n), col // wide + h))

    hspec = pl.BlockSpec((rows, wide), lambda h, n: (blk(n), h))
    in_specs = [zspec(col_q), zspec(col_f), zspec(col_v), pl.BlockSpec((hp, 1, HG_DK), lambda h, n: (h, 0, 0)),
                pl.BlockSpec((hp, 1, HG_DK, HG_DK), lambda h, n: (h, blk(n), 0, 0)), hspec]
    ins = [z, z, z, lower3, states, do]
    if has_prev:
        in_specs += [hspec, hspec]
        ins += list(prev)
    full_wide = HG_HEADS * HG_DK
    acc_dtype = BF16 if has_prev else F32
    return pl.pallas_call(
        body, name="gla_bwd_rev" if rev else "gla_bwd", grid=(HG_HEADS // hp, nb),
        in_specs=in_specs,
        out_specs=[hspec, hspec, hspec, pl.BlockSpec((hp, 1, HG_DK), lambda h, n: (h, 0, 0))],
        out_shape=[_sds((T, full_wide), acc_dtype), _sds((T, full_wide), acc_dtype), _sds((T, full_wide), BF16),
                   _sds((HG_HEADS, 1, HG_DK), F32)],
        scratch_shapes=[pltpu.VMEM((hp, HG_DK, HG_DK), F32)],
        compiler_params=_params(("parallel", "arbitrary")),
    )(*ins)


def _lower_fn(lb):
    e = jnp.exp(lb - jnp.max(lb, axis=0, keepdims=True))
    return (e / jnp.sum(e, axis=0, keepdims=True))[0]


def _lower_bounds(lb):
    def body(lb_ref, o_ref):
        o_ref[...] = _lower_fn(lb_ref[...])
    return pl.pallas_call(body, name="lower_bounds", out_shape=_sds(lb.shape[1:], F32))(lb)


def _row_tile(r, cap=1024):
    best = None
    for t in range(16, min(r, cap) + 1, 16):
        if r % t == 0:
            best = t
    return best if best is not None else r


def _sum4(shards, recv, chip, *, name):
    _, R, C = shards.shape
    tr = _row_tile(R)

    def body(chip_ref, o_ref, r_ref, out_ref):
        out_ref[...] = ((o_ref[0].astype(F32) + r_ref[0].astype(F32)) + r_ref[1].astype(F32)) + r_ref[2].astype(F32)

    grid_spec = pltpu.PrefetchScalarGridSpec(
        num_scalar_prefetch=1, grid=(R // tr,),
        in_specs=[pl.BlockSpec((1, tr, C), lambda i, chip_ref: (chip_ref[0], i, 0)),
                  pl.BlockSpec((3, tr, C), lambda i, chip_ref: (0, i, 0))],
        out_specs=pl.BlockSpec((tr, C), lambda i, chip_ref: (i, 0)))
    return pl.pallas_call(
        body, name=name, grid_spec=grid_spec, out_shape=_sds((R, C), F32), compiler_params=_params(("parallel",)),
    )(chip, shards, recv)


def _adamw_math(w, g, m, v):
    m = ADAM_B1 * m + (1.0 - ADAM_B1) * g
    v = ADAM_B2 * v + (1.0 - ADAM_B2) * (g * g)
    m_hat = m / (1.0 - ADAM_B1 ** ADAM_STEP)
    v_hat = v / (1.0 - ADAM_B2 ** ADAM_STEP)
    delta = -ADAM_LR * (m_hat / (jnp.sqrt(v_hat) + ADAM_EPS) + ADAM_WD * w)
    return delta, m, v


def _adamw(w, g_a, g_b, m, v, *, name):
    R, C = w.shape
    tr = _row_tile(R)
    two = g_b is not None

    def body(*refs):
        w_ref, ga_ref = refs[0], refs[1]
        rest = refs[2:]
        g = ga_ref[...]
        if two:
            g = g + rest[0][...]
            rest = rest[1:]
        m_ref, v_ref, g_out, d_out, m_out, v_out, token = rest
        delta, m_new, v_new = _adamw_math(w_ref[...], g, m_ref[...], v_ref[...])
        g_out[...] = g
        d_out[...] = delta
        m_out[...] = m_new
        v_out[...] = v_new
        token[...] = jnp.zeros_like(token)

    spec = pl.BlockSpec((tr, C), lambda i: (i, 0))
    ins = [w, g_a] + ([g_b] if two else []) + [m, v]
    return pl.pallas_call(
        body, name=name, grid=(R // tr,), in_specs=[spec] * len(ins),
        out_specs=[spec] * 4 + [pl.BlockSpec((8, LANE), lambda i: (0, 0))],
        out_shape=[_sds((R, C), F32)] * 4 + [_sds((8, LANE), F32)], compiler_params=_params(("arbitrary",)),
    )(*ins)


def _adamw_small(red, lb_full, ws, ms, vs):
    n = len(ws)

    def pieces(shape):
        out = []
        for j, idx in enumerate(itertools.product(*[range(d) for d in shape[:-1]])):
            out.append((idx[:-1] + (slice(idx[-1], idx[-1] + 1), slice(None)), j * shape[-1]))
        return out

    def body(*refs):
        red_ref, lb_ref = refs[0], refs[1]
        w_refs, m_refs, v_refs = refs[2:2 + n], refs[2 + n:2 + 2 * n], refs[2 + 2 * n:2 + 3 * n]
        out_refs = refs[2 + 3 * n:]
        chip = 2 * lax.axis_index("x") + lax.axis_index("y")
        n_f, shard = lb_ref.shape[-1], w_refs[n - 1].shape[-1]
        _, vjp = jax.vjp(_lower_fn, lb_ref[...])
        dlb = vjp(red_ref[8:10, 0:n_f])[0]
        for i in range(n):
            width = w_refs[i].shape[-1]
            for j, (at, lane) in enumerate(pieces(w_refs[i].shape)):
                if i < n - 1:
                    g = red_ref[i:i + 1, lane:lane + width]
                else:
                    row = dlb[j // 2][j % 2:j % 2 + 1]
                    g = sum(jnp.where(chip == q, row[:, q * shard:(q + 1) * shard], 0.0) for q in range(N_CHIPS))
                delta, m_new, v_new = _adamw_math(w_refs[i][at], g, m_refs[i][at], v_refs[i][at])
                for o_ref, val in zip(out_refs[4 * i:4 * i + 4], (g, delta, m_new, v_new)):
                    o_ref[at] = val

    return pl.pallas_call(
        body, name="adamw_small", out_shape=[_sds(w.shape, F32) for w in ws for _ in range(4)],
    )(red, lb_full, *ws, *ms, *vs)


def _chip_peers():
    x, y, c = lax.axis_index("x"), lax.axis_index("y"), lax.axis_index("c")
    return (x, y, c), 2 * x + y, [(1 - x, y), (x, 1 - y), (1 - x, 1 - y)]


_HBM = pl.BlockSpec(memory_space=pltpu.HBM)
_SEM = pl.BlockSpec(memory_space=pltpu.SEMAPHORE)
_EFFECT = pltpu.SideEffectType.DATAFLOW_SIDE_EFFECTING


def _exchange_copies(srcs, lands, sems, mode):
    (x, y, c), me, chips = _chip_peers()
    copies = []
    for t, (src, land) in enumerate(zip(srcs, lands)):
        if mode == "swap":
            copies.append(pltpu.make_async_remote_copy(src, land, sems[0].at[3 * t], sems[1].at[3 * t],
                                                       device_id=(x, y, 1 - c), device_id_type=MESH))
            continue
        for k, (px, py) in enumerate(chips):
            gather = mode == "gather"
            copies.append(pltpu.make_async_remote_copy(
                src if gather else src.at[2 * px + py], land.at[me] if gather else land.at[k],
                sems[0].at[3 * t + k], sems[1].at[3 * t + k], device_id=(px, py, c), device_id_type=MESH))
        if mode == "gather":
            copies.append(pltpu.make_async_copy(src, land.at[me], sems[2].at[t]))
    return copies


def _exchange_start(srcs, *, mode, name, after=None):
    n = len(srcs)
    n_sem = 3 if mode == "gather" else 2
    n_in = 2 * n + (after is not None)
    land_shape = {"gather": lambda s: (N_CHIPS,) + s.shape, "scatter": lambda s: (3,) + s.shape[1:], "swap": lambda s: s.shape}
    lands = [_sds(land_shape[mode](s), s.dtype) for s in srcs]

    def body(*refs):
        for cp in _exchange_copies(refs[:n], refs[n:2 * n], refs[n_in:n_in + n_sem], mode):
            cp.start()
        token = refs[-1]
        token[...] = jnp.zeros_like(token)

    sem_shapes = [pltpu.SemaphoreType.DMA((3 * n,)), pltpu.SemaphoreType.DMA((3 * n,))]
    sem_shapes += [pltpu.SemaphoreType.DMA((n,))] if mode == "gather" else []
    thru = [pltpu.HBM(s.shape, s.dtype) for s in srcs] + [pltpu.HBM(l.shape, l.dtype) for l in lands]
    res = pl.pallas_call(
        body, name=name, in_specs=[_HBM] * (2 * n) + [_ANY] * (after is not None),
        out_specs=[_SEM] * n_sem + [_HBM] * (2 * n) + [pl.BlockSpec(memory_space=pltpu.VMEM)],
        out_shape=sem_shapes + thru + [_sds((8, LANE), F32)], input_output_aliases={i: n_sem + i for i in range(2 * n)},
        compiler_params=pltpu.CompilerParams(has_side_effects=_EFFECT),
    )(*[pltpu.with_memory_space_constraint(s, pltpu.HBM) for s in srcs],
      *[pltpu.with_memory_space_constraint(lax.empty(l.shape, l.dtype), pltpu.HBM) for l in lands],
      *([after] if after is not None else []))
    return (res[:n_sem], res[n_sem:n_sem + n], res[n_sem + n:n_sem + 2 * n], mode), res[-1]


def _exchange_wait(started, after, *, name):
    sems, srcs, lands, mode = started
    n, n_sem = len(srcs), len(sems)
    after = list(after) if isinstance(after, (list, tuple)) else [after]

    def body(*refs):
        for cp in _exchange_copies(refs[:n], refs[n:2 * n], refs[2 * n:2 * n + n_sem], mode):
            cp.wait()

    res = pl.pallas_call(
        body, name=name, in_specs=[_HBM] * (2 * n) + [_SEM] * n_sem + [_ANY] * len(after), out_specs=[_HBM] * (2 * n),
        out_shape=[pltpu.HBM(a.shape, a.dtype) for a in list(srcs) + list(lands)],
        input_output_aliases={i: i for i in range(2 * n)},
        compiler_params=pltpu.CompilerParams(has_side_effects=_EFFECT),
    )(*srcs, *lands, *sems, *after)
    return res[:n], res[n:]


def _allreduce_small(pack, after):
    R, C = pack.shape

    def body(in_ref, after_ref, out_ref, slots, send_sems, recv_sems):
        x, y, c = lax.axis_index("x"), lax.axis_index("y"), lax.axis_index("c")
        me = 4 * x + 2 * y + c
        slots[me] = in_ref[...]
        copies = []
        for k in range(1, N_DEV):
            peer = (x ^ ((k >> 2) & 1), y ^ ((k >> 1) & 1), c ^ (k & 1))
            cp = pltpu.make_async_remote_copy(in_ref, slots.at[me], send_sems.at[k - 1], recv_sems.at[k - 1],
                                              device_id=peer, device_id_type=MESH)
            cp.start()
            copies.append(cp)
        for cp in copies:
            cp.wait()
        acc = slots[0]
        for d in range(1, N_DEV):
            acc = acc + slots[d]
        out_ref[...] = acc

    return pl.pallas_call(
        body, name="allreduce_small", out_shape=_sds((R, C), F32),
        in_specs=[pl.BlockSpec(memory_space=pltpu.VMEM), _ANY], out_specs=pl.BlockSpec(memory_space=pltpu.VMEM),
        scratch_shapes=[pltpu.VMEM((N_DEV, R, C), F32), pltpu.SemaphoreType.DMA((N_DEV - 1,)),
                        pltpu.SemaphoreType.DMA((N_DEV - 1,))],
        compiler_params=_params(),
    )(pack, after)


_Z_CQ, _Z_CKV, _Z_HQ, _Z_HFF, _Z_HFB, _Z_HI, _Z_HG, _Z_KR, _Z_END = 0, 256, 512, 1024, 1536, 2048, 2560, 3072, 3200


def _to_z_layout(wt):
    pad = jnp.zeros((_Z_END - _Z_KR - QK_ROPE, wt.shape[1]), wt.dtype)
    return jnp.concatenate([wt[:512], wt[512 + QK_ROPE:], wt[512:512 + QK_ROPE], pad], axis=0)


def _from_z_layout(wt):
    return jnp.concatenate([wt[:512], wt[_Z_KR:_Z_KR + QK_ROPE], wt[512:_Z_KR]], axis=0)


def _col_shards_to_full(g):
    return jnp.transpose(g, (1, 0, 2)).reshape(g.shape[1], -1)


def _full_to_col_shards(w):
    r, c = w.shape
    return jnp.transpose(w.reshape(r, N_CHIPS, c // N_CHIPS), (1, 0, 2))


def _full_to_row_shards(w):
    r, c = w.shape
    return w.reshape(N_CHIPS, r // N_CHIPS, c)


def kernel(x, p, positions, g_mix, w_in, g_qa, g_kva, w_qb, w_kvb, g_qn, g_kn, lb_param, g_hgo, w_o, g_ffn, w_gate, w_up, w_down, g_ple, w_ple_gate, w_ple_proj, loss_target, m_g_mix, m_w_in, m_g_qa, m_g_kva, m_w_qb, m_w_kvb, m_g_qn, m_g_kn, m_lb_param, m_g_hgo, m_w_o, m_g_ffn, m_w_gate, m_w_up, m_w_down, m_g_ple, m_w_ple_gate, m_w_ple_proj, v_g_mix, v_w_in, v_g_qa, v_g_kva, v_w_qb, v_w_kvb, v_g_qn, v_g_kn, v_lb_param, v_g_hgo, v_w_o, v_g_ffn, v_w_gate, v_w_up, v_w_down, v_g_ple, v_w_ple_gate, v_w_ple_proj):
    w_named = dict(g_mix=g_mix, w_in=w_in, g_qa=g_qa, g_kva=g_kva, w_qb=w_qb, w_kvb=w_kvb, g_qn=g_qn, g_kn=g_kn,
                   lb_param=lb_param, g_hgo=g_hgo, w_o=w_o, g_ffn=g_ffn, w_gate=w_gate, w_up=w_up, w_down=w_down,
                   g_ple=g_ple, w_ple_gate=w_ple_gate, w_ple_proj=w_ple_proj)
    m_named = dict(g_mix=m_g_mix, w_in=m_w_in, g_qa=m_g_qa, g_kva=m_g_kva, w_qb=m_w_qb, w_kvb=m_w_kvb, g_qn=m_g_qn,
                   g_kn=m_g_kn, lb_param=m_lb_param, g_hgo=m_g_hgo, w_o=m_w_o, g_ffn=m_g_ffn, w_gate=m_w_gate,
                   w_up=m_w_up, w_down=m_w_down, g_ple=m_g_ple, w_ple_gate=m_w_ple_gate, w_ple_proj=m_w_ple_proj)
    v_named = dict(g_mix=v_g_mix, w_in=v_w_in, g_qa=v_g_qa, g_kva=v_g_kva, w_qb=v_w_qb, w_kvb=v_w_kvb, g_qn=v_g_qn,
                   g_kn=v_g_kn, lb_param=v_lb_param, g_hgo=v_g_hgo, w_o=v_w_o, g_ffn=v_g_ffn, w_gate=v_w_gate,
                   w_up=v_w_up, w_down=v_w_down, g_ple=v_g_ple, w_ple_gate=v_w_ple_gate, w_ple_proj=v_w_ple_proj)
    order = list(w_named)
    transposed = ("w_in", "w_qb", "w_gate", "w_up")
    col_sharded = ("w_kvb", "w_ple_proj")
    row_sharded = ("w_o", "w_down", "w_ple_gate")
    big = transposed + col_sharded + row_sharded

    def view(n, a):
        return jnp.transpose(a[0]) if n in transposed else a[0]

    def unview(n, a):
        return (jnp.transpose(a) if n in transposed else a)[None]

    def to_shards(n, g):
        return _full_to_col_shards(g) if n in col_sharded else _full_to_row_shards(g)

    x2d, p2d, tgt = x[0], p[0, 0], loss_target[0]
    T, D = x2d.shape

    lb_flat = lb_param.reshape(-1, lb_param.shape[-1])
    gather_groups = (("w_in",), ("w_qb", "w_kvb"), ("w_o", "w_gate", "w_up", "w_down", "w_ple_gate", "w_ple_proj"))
    gather_started = []

    casts = {n: view(n, w_named[n]).astype(BF16) for n in big}

    def gather_start(gi, after):
        srcs = [casts[n] for n in gather_groups[gi]] + ([lb_flat] if gi == 0 else [])
        started, token = _exchange_start(srcs, mode="gather", name=f"gather_start_{gi}", after=after)
        gather_started.append(started)
        return token

    full = {}

    def gather_wait(gi, after):
        _, got = _exchange_wait(gather_started[gi], after, name=f"gather_wait_{gi}")
        for n, g in zip(gather_groups[gi], got):
            full[n] = _col_shards_to_full(g) if n in col_sharded else g.reshape(-1, g.shape[-1])
        return got

    g_hgo_row = g_hgo.reshape(1, -1)

    inv_freq = ROPE_THETA ** (-jnp.arange(0, QK_ROPE, 2, dtype=F32) / QK_ROPE)
    ang = positions[0].astype(F32)[:, None] * inv_freq
    cos, sin = jnp.cos(ang), jnp.sin(ang)
    token = gather_start(0, None)
    h1 = _stage(_rms, [x2d], [g_mix], [_sds((T, D), BF16)], [], name="norm_mix", after=token)[0]
    got = gather_wait(0, [h1, cos, sin] + [casts[n] for g in gather_groups[1:] for n in g])
    token = got[0]
    for gi in range(1, len(gather_groups)):
        token = gather_start(gi, token)
    lb_full = _col_shards_to_full(got[-1]).reshape(lb_param.shape[0], lb_param.shape[1], -1)
    w_in_zt = _to_z_layout(full["w_in"])
    z = _mm(h1, w_in_zt, tb=True, name="in_proj", after=token)
    qn, kvn = _stage(_mla_a_fn, [_cols(z, 256, 0), _cols(z, 256, 1)], [g_qa, g_kva],
                     [_sds((T, 256), BF16), _sds((T, 256), BF16)], [], name="mla_latent_norm")
    gather_wait(1, qn)
    q_raw = _mm(qn, full["w_qb"], tb=True, name="q_up")
    kv_raw = _mm(kvn, full["w_kvb"], name="kv_up")
    kr = _cols(z, LANE, _Z_KR // LANE)
    q, k, v = _stage(_mla_b_fn, [q_raw, kv_raw, kr, cos, sin], [g_qn, g_kn],
                     [_sds((MLA_HEADS, T, QK_HEAD), BF16), _sds((MLA_HEADS, T, QK_HEAD), BF16),
                      _sds((MLA_HEADS, T, V_HEAD), BF16)], [], name="mla_qk_norm_rope")
    att, lse = _attention_fwd(q, k, v)

    lower = _lower_bounds(lb_full)
    lower3 = lower.reshape(2, HG_HEADS, 1, HG_DK)
    o_f, st_f = _gla_fwd(z, lower3[0], rev=False, col_q=_Z_HQ, col_f=_Z_HFF, col_v=_Z_HI, hp=GLA_FWD_HEADS)
    o_b, st_b = _gla_fwd(z, lower3[1], rev=True, col_q=_Z_HQ, col_f=_Z_HFB, col_v=_Z_HI, hp=GLA_FWD_HEADS)
    hg = _cols(z, 512, _Z_HG // 512)
    mix = _stage(_post_fn, [att, o_f, o_b, hg], [g_hgo_row], [_sds((T, att.shape[1] + o_f.shape[1]), BF16)], [],
                 name="mix_out")[0]
    gather_wait(2, mix)
    x1, h2 = _mm_fused(mix, full["w_o"], _resid_norm_fn, [x2d], [g_ffn], [F32, BF16], [], full_rows=True,
                       name="out_proj")
    gt, up, act = _mm_fused(h2, full["w_gate"], lambda gt, up: (gt, up, _swiglu_fn(gt, up)), [], [], [BF16, BF16, BF16],
                            [], tb=True, b2=full["w_up"], name="ffn_gate_up")
    x2, h3 = _mm_fused(act, full["w_down"], _resid_norm_fn, [x1], [g_ple], [F32, BF16], [], full_rows=True,
                       name="ffn_down")
    pp = _mm(p2d, full["w_ple_proj"], name="ple_proj")
    dx3, dpg, dpp, loss_part = _mm_fused(
        h3, full["w_ple_gate"], lambda acc, pp, x2, tgt: _ple_loss_fn(x2, acc, pp, tgt), [pp, x2, tgt], [],
        [F32, BF16, BF16], [_sds((1, 1), F32)], full_rows=True, name="ple_gate_loss")

    grads = {}
    scatter_groups = (("w_ple_proj", "w_ple_gate", "w_down", "w_gate", "w_up", "w_o"), ("w_qb", "w_kvb", "w_in"))
    scatter_started = []

    def scatter_start(gi):
        srcs = [to_shards(n, grads[n]) for n in scatter_groups[gi]]
        started, token = _exchange_start(srcs, mode="scatter", name=f"scatter_start_{gi}")
        scatter_started.append(started)
        return token

    chip = 2 * lax.axis_index("x") + lax.axis_index("y")
    swap_started = []

    def reduce_group(gi, after):
        shards, recvs = _exchange_wait(scatter_started[gi], after, name=f"scatter_wait_{gi}")
        sums = [_sum4(s, r, chip.reshape(1), name="sum_" + n) for n, s, r in zip(scatter_groups[gi], shards, recvs)]
        started, token = _exchange_start(sums, mode="swap", name=f"swap_start_{gi}")
        swap_started.append(started)
        return token

    grads["w_ple_proj"] = _mm(p2d, dpp, ta=True, out_dtype=BF16, name="d_w_ple_proj")
    grads["w_ple_gate"] = _mm(h3, dpg, ta=True, out_dtype=BF16, name="d_w_ple_gate")
    dx2, grads["g_ple"] = _mm_fused(
        dpg, full["w_ple_gate"], lambda acc, x2, dx3, g: _norm_bwd_fn(x2, acc, dx3, g), [x2, dx3], [g_ple],
        [F32], [_sds((1, D), F32)], tb=True, full_rows=True, name="d_h3_norm_ple_bwd")
    dgt, dup = _mm_fused(dx2, full["w_down"], lambda acc, gt, up: _swiglu_bwd_fn(gt.astype(F32), up.astype(F32), acc), [gt, up], [],
                         [BF16, BF16], [], tb=True, name="d_act_swiglu_bwd")
    grads["w_down"] = _mm(act, dx2, ta=True, out_dtype=BF16, name="d_w_down")
    grads["w_gate"] = _mm(dgt, h2, ta=True, out_dtype=BF16, name="d_w_gate")
    grads["w_up"] = _mm(dup, h2, ta=True, out_dtype=BF16, name="d_w_up")
    dx1, grads["g_ffn"] = _mm_fused(
        dgt, full["w_gate"], lambda acc, x1, dx2, g: _norm_bwd_fn(x1, acc, dx2, g), [x1, dx2], [g_ffn],
        [F32], [_sds((1, D), F32)], full_rows=True, a2=dup, b2=full["w_up"], name="d_h2_norm_ffn_bwd")
    grads["w_o"] = _mm(mix, dx1, ta=True, out_dtype=BF16, name="d_w_o")
    token = scatter_start(0)
    dmix = _mm(dx1, full["w_o"], tb=True, name="d_mix", after=token)

    half = MLA_HEADS * V_HEAD
    do, dhg, dg_hgo = _stage(_post_bwd_fn, [o_f, o_b, hg, _cols(dmix, half, 1)], [g_hgo_row],
                             [_sds((T, half), F32), _sds((T, half), BF16)], [_sds((1, half), F32)], name="mix_out_bwd")
    grads["g_hgo"] = dg_hgo
    dhq_f, dhi_f, dhf_f, dlow_f = _gla_bwd(z, lower3[0], st_f, do, None, rev=False,
                                           col_q=_Z_HQ, col_f=_Z_HFF, col_v=_Z_HI, hp=GLA_BWD_HEADS)
    dhq, dhi, dhf_b, dlow_b = _gla_bwd(z, lower3[1], st_b, do, (dhq_f, dhi_f), rev=True,
                                       col_q=_Z_HQ, col_f=_Z_HFB, col_v=_Z_HI, hp=GLA_BWD_HEADS)

    dq, dk, dv = _attention_bwd(q, k, v, att, lse, dmix)
    dq_raw, dkv_raw, dkr, grads["g_qn"], grads["g_kn"] = _stage(
        _mla_b_bwd_fn, [q_raw, kv_raw, kr, cos, sin, dq, dk, dv], [g_qn, g_kn],
        [_sds(q_raw.shape, BF16), _sds(kv_raw.shape, BF16), _sds((T, LANE), BF16)],
        [_sds(g_qn.shape, F32), _sds(g_kn.shape, F32)], name="mla_qk_norm_rope_bwd")
    grads["w_qb"] = _mm(dq_raw, qn, ta=True, out_dtype=BF16, name="d_w_qb")
    grads["w_kvb"] = _mm(kvn, dkv_raw, ta=True, out_dtype=BF16, name="d_w_kvb")
    dqn = _mm(dq_raw, full["w_qb"], name="d_qn")
    dkvn = _mm(dkv_raw, full["w_kvb"], tb=True, name="d_kvn")
    dcq, dckv, grads["g_qa"], grads["g_kva"] = _stage(
        _mla_a_bwd_fn, [_cols(z, 256, 0), _cols(z, 256, 1), dqn, dkvn], [g_qa, g_kva],
        [_sds((T, 256), BF16), _sds((T, 256), BF16)], [_sds(g_qa.shape, F32), _sds(g_kva.shape, F32)],
        name="mla_latent_norm_bwd")
    token = reduce_group(0, dcq)
    dz = jnp.concatenate([dcq, dckv, dhq, dhf_f, dhf_b, dhi, dhg, dkr], axis=1)
    grads["w_in"] = _from_z_layout(_mm(dz, h1, ta=True, out_dtype=BF16, name="d_w_in", after=token))
    token = scatter_start(1)
    grad_x, grads["g_mix"] = _mm_fused(
        dz, w_in_zt, lambda acc, x, dx1, g: _norm_bwd_fn(x, acc, dx1, g), [x2d, dx1], [g_mix],
        [F32], [_sds((1, D), F32)], full_rows=True, name="d_h1_norm_mix_bwd", after=token)

    out_g, out_d, out_m, out_v = {}, {}, {}, {}

    def update_group(gi, after):
        mine, theirs = _exchange_wait(swap_started[gi], after, name=f"swap_wait_{gi}")
        tokens = []
        for n, a, b in zip(scatter_groups[gi], mine, theirs):
            *res, token = _adamw(view(n, w_named[n]), a, b, view(n, m_named[n]), view(n, v_named[n]), name="adamw_" + n)
            out_g[n], out_d[n], out_m[n], out_v[n] = (unview(n, t) for t in res)
            tokens.append(token)
        return tokens

    done = update_group(0, grads["g_mix"])
    token = reduce_group(1, done)
    done = update_group(1, token)

    small = ("g_mix", "g_qa", "g_kva", "g_qn", "g_kn", "g_hgo", "g_ffn", "g_ple")
    small_all = small + ("lb_param",)
    width = -(-max(w_named[n].size for n in small_all) // LANE) * LANE

    def row(a):
        a = a.reshape(1, -1)
        return jnp.pad(a, ((0, 0), (0, width - a.shape[1])))

    dlower = jnp.concatenate([dlow_f.reshape(1, -1), dlow_b.reshape(1, -1)], axis=0)
    pack = jnp.concatenate([row(grads[n]) for n in small] + [row(dlower[0]), row(dlower[1]), row(loss_part)]
                           + [jnp.zeros((5, width), F32)], axis=0)
    red = _allreduce_small(pack, done[-1])
    loss = red[10, 0]

    outs = _adamw_small(red, lb_full, [w_named[n] for n in small_all], [m_named[n] for n in small_all],
                        [v_named[n] for n in small_all])
    for i, n in enumerate(small_all):
        out_g[n], out_d[n], out_m[n], out_v[n] = outs[4 * i:4 * i + 4]

    return (loss, grad_x[None], *[out_g[n] for n in order], *[out_d[n] for n in order],
            *[out_m[n] for n in order], *[out_v[n] for n in order])
```

```python
import functools
import itertools

import jax
import jax.numpy as jnp
from jax import lax
from jax.experimental import pallas as pl
from jax.experimental.pallas import tpu as pltpu

F32 = jnp.float32
BF16 = jnp.bfloat16
MESH = pl.DeviceIdType.MESH

EPS = 1e-6
ROPE_THETA = 10000.0
MLA_HEADS = 4
QK_NOPE = 128
QK_ROPE = 64
QK_HEAD = QK_NOPE + QK_ROPE
V_HEAD = 128
HG_HEADS = 4
HG_DK = 128
CHUNK = 64
ADAM_LR = 0.001
ADAM_B1 = 0.9
ADAM_B2 = 0.999
ADAM_EPS = 1e-08
ADAM_WD = 0.01
ADAM_STEP = 10

LANE = 128
VMEM_LIMIT = 56 * 1024 * 1024
TOK_TILE = 256
GLA_GROUP = 16
GLA_FWD_HEADS = 4
GLA_BWD_HEADS = 2
ATT_TQ = 1024
ATT_TK = 4096
ATT_CHUNK = 1024
LOG2_E = 1.4426950408889634
N_CHIPS = 4
N_DEV = 8


_ANY = pl.BlockSpec(memory_space=pl.ANY)


def _params(dims=None, **kw):
    return pltpu.CompilerParams(dimension_semantics=dims, vmem_limit_bytes=VMEM_LIMIT, **kw)


def _tile_candidates(n, cap):
    out = [d for d in range(LANE, min(n, cap) + 1, LANE) if n % d == 0]
    if n <= cap and n not in out:
        out.append(n)
    return out or [n]


MM_VMEM_BUDGET = 40 * 1024 * 1024
MM_MIN_ROWS = 256
MM_MAX_ROWS = 1536
HBM_BYTES_PER_S = 2.8e12
MXU_FLOPS_PER_S = 8e14
STEP_OVERHEAD_S = 0.35e-6


def _mm_tiles(M, N, K, a_bytes, b_bytes, o_bytes, has_add, full_rows=False, full_k=False, n_a=1, n_b=1):
    cast_a, cast_b = a_bytes > 2, b_bytes > 2
    a_bytes, b_bytes = n_a * a_bytes, n_b * b_bytes
    best = None
    for tm in [t for t in _tile_candidates(M, MM_MAX_ROWS) if t >= min(M, MM_MIN_ROWS)]:
        for tn in ([N] if full_rows else _tile_candidates(N, N)):
            for tk in ([K] if full_k else _tile_candidates(K, K)):
                ni, nj, nk = M // tm, N // tn, K // tk
                vmem = 2 * (tm * tk * a_bytes + tk * tn * b_bytes + tm * tn * o_bytes * (2 if has_add else 1))
                vmem += tm * tn * 4 * (2 if nk > 1 else 1)
                vmem += (tm * tk * 2 * n_a if cast_a else 0) + (tk * tn * 2 * n_b if cast_b else 0)
                if vmem > MM_VMEM_BUDGET:
                    continue
                moved = M * K * a_bytes * (nj if nk > 1 else 1) + K * N * b_bytes * (1 if nj == nk == 1 else ni)
                moved += M * N * o_bytes * (2 if has_add else 1)
                t = max(moved / HBM_BYTES_PER_S, 2 * M * N * K / MXU_FLOPS_PER_S) + ni * nj * nk * STEP_OVERHEAD_S
                if best is None or t < best[0]:
                    best = (t, tm, tn, tk)
    assert best is not None, (M, N, K)
    return best[1:]


def _dot_raw(a, b, kind):
    nb = a.ndim - 2
    batch = ((0,), (0,)) if nb else ((), ())
    ca = nb if kind == "tn" else nb + 1
    cb = nb + 1 if kind == "nt" else nb
    return lax.dot_general(a.astype(BF16), b.astype(BF16), (((ca,), (cb,)), batch), preferred_element_type=F32)


@functools.partial(jax.custom_vjp, nondiff_argnums=(2,))
def _bdot(a, b, kind):
    return _dot_raw(a, b, kind)


def _bdot_fwd(a, b, kind):
    return _dot_raw(a, b, kind), (a, b)


def _bdot_bwd(kind, res, g):
    a, b = res
    if kind == "nn":
        da, db = _bdot(g, b, "nt"), _bdot(a, g, "tn")
    elif kind == "nt":
        da, db = _bdot(g, b, "nn"), _bdot(g, a, "tn")
    else:
        da, db = _bdot(b, g, "nt"), _bdot(a, g, "nn")
    return da.astype(a.dtype), db.astype(b.dtype)


_bdot.defvjp(_bdot_fwd, _bdot_bwd)


def _mm(a, b, *, name, ta=False, tb=False, add=None, out_dtype=F32, after=None):
    K, M = a.shape if ta else a.shape[::-1]
    N, Kb = b.shape if tb else b.shape[::-1]
    assert K == Kb, (a.shape, b.shape, ta, tb)
    tm, tn, tk = _mm_tiles(M, N, K, a.dtype.itemsize, b.dtype.itemsize, jnp.dtype(out_dtype).itemsize, add is not None)
    nk = K // tk
    kind = "tn" if ta else ("nt" if tb else "nn")
    assert not (ta and tb)
    a_spec = pl.BlockSpec((tk, tm), lambda i, j, k: (k, i)) if ta else pl.BlockSpec((tm, tk), lambda i, j, k: (i, k))
    b_spec = pl.BlockSpec((tn, tk), lambda i, j, k: (j, k)) if tb else pl.BlockSpec((tk, tn), lambda i, j, k: (k, j))
    o_spec = pl.BlockSpec((tm, tn), lambda i, j, k: (i, j))
    has_add = add is not None

    def body(*refs):
        a_ref, b_ref = refs[0], refs[1]
        add_ref = refs[2] if has_add else None
        o_ref = refs[n_in]
        part = _dot_raw(a_ref[...], b_ref[...], kind)
        if nk == 1:
            if has_add:
                part = part + add_ref[...].astype(F32)
            o_ref[...] = part.astype(o_ref.dtype)
            return
        acc_ref = refs[-1]
        k = pl.program_id(2)

        @pl.when(k == 0)
        def _():
            acc_ref[...] = part

        @pl.when(k > 0)
        def _():
            acc_ref[...] += part

        @pl.when(k == nk - 1)
        def _():
            r = acc_ref[...]
            if has_add:
                r = r + add_ref[...].astype(F32)
            o_ref[...] = r.astype(o_ref.dtype)

    ins = [a, b] + ([add] if has_add else []) + ([after] if after is not None else [])
    in_specs = [a_spec, b_spec] + ([o_spec] if has_add else []) + ([_ANY] if after is not None else [])
    n_in = len(ins)
    return pl.pallas_call(
        body, name=name, grid=(M // tm, N // tn, nk), in_specs=in_specs, out_specs=o_spec,
        out_shape=jax.ShapeDtypeStruct((M, N), out_dtype),
        scratch_shapes=[pltpu.VMEM((tm, tn), F32)] if nk > 1 else [],
        compiler_params=_params(("parallel", "parallel", "arbitrary")),
    )(*ins)


def _mm_fused(a, b, fn, tiles, params, out_dtypes, sums, *, name, ta=False, tb=False, full_rows=False, after=None,
              b2=None, a2=None):
    K, M = a.shape if ta else a.shape[::-1]
    N, Kb = b.shape if tb else b.shape[::-1]
    assert K == Kb and not (ta and tb), (a.shape, b.shape, ta, tb)
    per_elem = sum(t.dtype.itemsize for t in tiles) + sum(jnp.dtype(d).itemsize for d in out_dtypes)
    n_b = 1 if b2 is None else 2
    n_a = 1 if a2 is None else 2
    tm, tn, tk = _mm_tiles(M, N, K, a.dtype.itemsize, b.dtype.itemsize, per_elem, False, full_rows, b2 is not None,
                           n_a, n_b)
    nk = K // tk
    kind = "tn" if ta else ("nt" if tb else "nn")
    a_spec = pl.BlockSpec((tk, tm), lambda i, j, k: (k, i)) if ta else pl.BlockSpec((tm, tk), lambda i, j, k: (i, k))
    b_spec = pl.BlockSpec((tn, tk), lambda i, j, k: (j, k)) if tb else pl.BlockSpec((tk, tn), lambda i, j, k: (k, j))
    o_spec = pl.BlockSpec((tm, tn), lambda i, j, k: (i, j))
    ins = [a, b] + ([b2] if b2 is not None else []) + ([a2] if a2 is not None else [])
    ins += list(tiles) + list(params) + ([after] if after is not None else [])
    in_specs = [a_spec] + [b_spec] * n_b + [a_spec] * (n_a - 1) + [o_spec] * len(tiles)
    in_specs += [pl.BlockSpec(p.shape, lambda i, j, k, nd=p.ndim: (0,) * nd) for p in params]
    in_specs += [_ANY] if after is not None else []
    n_in, n_t, n_p, n_o = len(ins), len(tiles), len(params), len(out_dtypes)

    def body(*refs):
        outs, sum_refs = refs[n_in:n_in + n_o], refs[n_in + n_o:n_in + n_o + len(sums)]

        def finish(*products):
            res = fn(*products, *[t[...] for t in refs[n_a + n_b:n_a + n_b + n_t + n_p]])
            for o_ref, v in zip(outs, res[:n_o]):
                o_ref[...] = v.astype(o_ref.dtype)
            first = jnp.logical_and(pl.program_id(0) == 0, pl.program_id(1) == 0)
            for s_ref, v in zip(sum_refs, res[n_o:]):
                @pl.when(first)
                def _(s_ref=s_ref, v=v):
                    s_ref[...] = v

                @pl.when(jnp.logical_not(first))
                def _(s_ref=s_ref, v=v):
                    s_ref[...] += v

        part = _dot_raw(refs[0][...], refs[1][...], kind)
        if nk == 1 and a2 is not None:
            finish(part + _dot_raw(refs[3][...], refs[2][...], kind))
            return
        if nk == 1:
            finish(part, *([_dot_raw(refs[0][...], refs[2][...], kind)] if b2 is not None else []))
            return
        acc_ref = refs[-1]
        k = pl.program_id(2)

        @pl.when(k == 0)
        def _():
            acc_ref[...] = part

        @pl.when(k > 0)
        def _():
            acc_ref[...] += part

        @pl.when(k == nk - 1)
        def _():
            finish(acc_ref[...])

    out_shape = [_sds((M, N), d) for d in out_dtypes] + list(sums)
    out_specs = [o_spec] * n_o + [pl.BlockSpec(s.shape, lambda i, j, k, nd=len(s.shape): (0,) * nd) for s in sums]
    order = ("arbitrary",) * 3 if sums else ("parallel", "parallel", "arbitrary")
    return pl.pallas_call(
        body, name=name, grid=(M // tm, N // tn, nk), in_specs=in_specs, out_specs=out_specs, out_shape=out_shape,
        scratch_shapes=[pltpu.VMEM((tm, tn), F32)] if nk > 1 else [], compiler_params=_params(order),
    )(*ins)


def _cols(arr, width, block):
    return (arr, width, block)


def _stage(fn, tiles, params, out_tiles, out_sums, *, name, tile=TOK_TILE, after=None):
    def tok_spec(shape, width=None, block=0):
        if len(shape) == 2:
            w = shape[1] if width is None else width
            return pl.BlockSpec((tile, w), lambda i: (i, block))
        return pl.BlockSpec((shape[0], tile, shape[2]), lambda i: (0, i, 0))

    arrays, in_specs = [], []
    for t in tiles:
        if isinstance(t, tuple):
            arr, width, block = t
            arrays.append(arr)
            in_specs.append(tok_spec(arr.shape, width, block))
        else:
            arrays.append(t)
            in_specs.append(tok_spec(t.shape))
    n_tok = arrays[0].shape[0] if arrays[0].ndim == 2 else arrays[0].shape[1]
    for p in params:
        arrays.append(p)
        in_specs.append(pl.BlockSpec(p.shape, lambda i, nd=p.ndim: (0,) * nd))
    out_shape = list(out_tiles) + list(out_sums)
    out_specs = [tok_spec(o.shape) for o in out_tiles]
    out_specs += [pl.BlockSpec(o.shape, lambda i, nd=len(o.shape): (0,) * nd) for o in out_sums]
    n_fn, n_ot = len(arrays), len(out_tiles)
    if after is not None:
        arrays.append(after)
        in_specs.append(_ANY)
    n_in = len(arrays)

    def body(*refs):
        res = fn(*[r[...] for r in refs[:n_fn]])
        if not isinstance(res, (tuple, list)):
            res = (res,)
        outs = refs[n_in:]
        for o_ref, r in zip(outs[:n_ot], res[:n_ot]):
            o_ref[...] = r.astype(o_ref.dtype)
        i = pl.program_id(0)
        for o_ref, r in zip(outs[n_ot:], res[n_ot:]):
            @pl.when(i == 0)
            def _(o_ref=o_ref, r=r):
                o_ref[...] = r.astype(o_ref.dtype)

            @pl.when(i > 0)
            def _(o_ref=o_ref, r=r):
                o_ref[...] += r.astype(o_ref.dtype)

    res = pl.pallas_call(
        body, name=name, grid=(n_tok // tile,), in_specs=in_specs, out_specs=out_specs, out_shape=out_shape,
        compiler_params=_params(("arbitrary",)),
    )(*arrays)
    return res


def _sds(shape, dtype):
    return jax.ShapeDtypeStruct(tuple(shape), dtype)


def _sigmoid(x):
    return 0.5 * jnp.tanh(0.5 * x) + 0.5


def _rms(x, g):
    return x * lax.rsqrt(jnp.mean(x * x, axis=-1, keepdims=True) + EPS) * g


def _norm_bwd_fn(x, dh, dres, g):
    r = lax.rsqrt(jnp.mean(x * x, axis=-1, keepdims=True) + EPS)
    xr = x * r
    dhg = dh * g
    dx = r * (dhg - xr * jnp.mean(xr * dhg, axis=-1, keepdims=True))
    return dx + dres, jnp.sum(dh * xr, axis=0, keepdims=True)


def _mla_a_fn(cq, ckv, g_qa, g_kva):
    return _rms(cq, g_qa), _rms(ckv, g_kva)


def _mla_a_bwd_fn(cq, ckv, dqn, dkvn, g_qa, g_kva):
    _, vjp = jax.vjp(_mla_a_fn, cq, ckv, g_qa, g_kva)
    return vjp((dqn, dkvn))


def _rope(t, cos, sin):
    t1, t2 = t[:, :QK_ROPE // 2], t[:, QK_ROPE // 2:]
    return jnp.concatenate([t1 * cos - t2 * sin, t1 * sin + t2 * cos], axis=-1)


def _mla_b_fn(q_raw, kv_raw, kr, cos, sin, g_qn, g_kn):
    krope = kr[:, :QK_ROPE]
    qs, ks, vs = [], [], []
    for h in range(MLA_HEADS):
        qh = _rms(q_raw[:, h * QK_HEAD:(h + 1) * QK_HEAD], g_qn)
        kvh = kv_raw[:, h * (QK_NOPE + V_HEAD):(h + 1) * (QK_NOPE + V_HEAD)]
        kh = _rms(jnp.concatenate([kvh[:, :QK_NOPE], krope], axis=-1), g_kn)
        qs.append(jnp.concatenate([qh[:, :QK_NOPE], _rope(qh[:, QK_NOPE:], cos, sin)], axis=-1))
        ks.append(jnp.concatenate([kh[:, :QK_NOPE], _rope(kh[:, QK_NOPE:], cos, sin)], axis=-1))
        vs.append(kvh[:, QK_NOPE:])
    return jnp.stack(qs), jnp.stack(ks), jnp.stack(vs)


def _mla_b_bwd_fn(q_raw, kv_raw, kr, cos, sin, dq, dk, dv, g_qn, g_kn):
    _, vjp = jax.vjp(lambda a, b, c, d, e: _mla_b_fn(a, b, c, cos, sin, d, e), q_raw, kv_raw, kr, g_qn, g_kn)
    return vjp((dq, dk, dv))


def _post_fn(a, o_f, o_b, hg, g_hgo):
    o = o_f + o_b
    parts = [a]
    for h in range(HG_HEADS):
        s = slice(h * HG_DK, (h + 1) * HG_DK)
        gate = hg[:, s]
        parts.append(_rms(o[:, s], g_hgo[:, s]) * (gate * _sigmoid(gate)))
    return jnp.concatenate(parts, axis=-1)


def _post_bwd_fn(o_f, o_b, hg, dr, g_hgo):
    def f(o, hg, g):
        return _post_fn(jnp.zeros_like(o), o, jnp.zeros_like(o), hg, g)[:, o.shape[1]:]
    _, vjp = jax.vjp(f, o_f + o_b, hg, g_hgo)
    return vjp(dr)


def _swiglu_fn(gt, up):
    return gt * _sigmoid(gt) * up


def _resid_norm_fn(acc, x, g):
    x_new = acc + x
    return x_new, _rms(x_new, g)


def _swiglu_bwd_fn(gt, up, dact):
    s = _sigmoid(gt)
    silu = gt * s
    return dact * up * (s + silu * (1.0 - s)), dact * silu


def _ple_loss_fn(x2, pg, pp, target):
    gate = _sigmoid(pg)
    err = x2 + gate * pp - target
    dx3 = err * (1.0 / err.shape[-1])
    loss = 0.5 * jnp.sum(jnp.mean(err * err, axis=-1, keepdims=True), axis=0, keepdims=True)
    return dx3, dx3 * pp * gate * (1.0 - gate), dx3 * gate, loss


def _attention_fwd(q, k, v):
    H, T, D = q.shape
    DV = v.shape[-1]
    tq, ck = min(ATT_TQ, T), min(ATT_CHUNK, T)
    c2 = (D ** -0.5) * LOG2_E

    def body(q_ref, k_ref, v_ref, o_ref, lse_ref):
        q_i = q_ref[0]

        def chunk(c, carry):
            m, l, acc = carry
            rows = pl.ds(pl.multiple_of(c * ck, ck), ck)
            s = _dot_raw(q_i, k_ref[0, rows, :], "nt")
            m_new = jnp.maximum(m, jnp.max(s, axis=-1, keepdims=True))
            p = jnp.exp2((s - m_new) * c2)
            alpha = jnp.exp2((m - m_new) * c2)
            l = l * alpha + jnp.sum(p, axis=-1, keepdims=True)
            acc = acc * alpha + _dot_raw(p, v_ref[0, rows, :], "nn")
            return m_new, l, acc

        init = (jnp.full((tq, 1), -jnp.inf, F32), jnp.zeros((tq, 1), F32), jnp.zeros((tq, DV), F32))
        m, l, acc = lax.fori_loop(0, T // ck, chunk, init, unroll=True)
        o_ref[...] = acc / l
        lse_ref[0] = m * c2 + jnp.log2(l)

    return pl.pallas_call(
        body, name="attention_fwd", grid=(H, T // tq),
        in_specs=[pl.BlockSpec((1, tq, D), lambda h, i: (h, i, 0)),
                  pl.BlockSpec((1, T, D), lambda h, i: (h, 0, 0)),
                  pl.BlockSpec((1, T, DV), lambda h, i: (h, 0, 0))],
        out_specs=[pl.BlockSpec((tq, DV), lambda h, i: (i, h)),
                   pl.BlockSpec((1, tq, 1), lambda h, i: (h, i, 0))],
        out_shape=[_sds((T, H * DV), F32), _sds((H, T, 1), F32)],
        compiler_params=_params(("parallel", "parallel")),
    )(q, k, v)


def _attention_bwd(q, k, v, o, lse2, dmix):
    H, T, D = q.shape
    DV = v.shape[-1]
    tk, cq = min(ATT_TK, T), min(ATT_CHUNK, T)
    scale = D ** -0.5
    c2 = scale * LOG2_E

    def body(q_ref, k_ref, v_ref, o_ref, lse_ref, do_ref, dq_ref, dk_ref, dv_ref, delta_ref):
        j = pl.program_id(1)

        @pl.when(j == 0)
        def _():
            delta = lax.dot_general(jnp.ones((8, DV), F32), do_ref[...] * o_ref[...], (((1,), (1,)), ((), ())),
                                    precision=lax.Precision.HIGHEST, preferred_element_type=F32)
            for i in range(T // cq):
                delta_ref[i] = delta[:, i * cq:(i + 1) * cq]
            dq_ref[0] = jnp.zeros((T, D), F32)

        k_j, v_j = k_ref[0], v_ref[0]
        dk_ref[0] = jnp.zeros((tk, D), F32)
        dv_ref[0] = jnp.zeros((tk, DV), F32)

        def chunk(c, carry):
            rows = pl.ds(pl.multiple_of(c * cq, cq), cq)
            q_c = q_ref[0, rows, :]
            do_c = do_ref[rows, :].astype(BF16)
            st = _dot_raw(k_j, q_c, "nt")
            pt = jnp.exp2(st * c2 - lse_ref[0, c])
            dv_ref[0] += _dot_raw(pt, do_c, "nn")
            dpt = _dot_raw(v_j, do_c, "nt")
            dst = pt * (dpt - delta_ref[c, 0:1, :]) * scale
            dk_ref[0] += _dot_raw(dst, q_c, "nn")
            dq_ref[0, rows, :] += _dot_raw(dst, k_j, "tn")
            return carry

        lax.fori_loop(0, T // cq, chunk, 0, unroll=True)

    return pl.pallas_call(
        body, name="attention_bwd", grid=(H, T // tk),
        in_specs=[pl.BlockSpec((1, T, D), lambda h, j: (h, 0, 0)),
                  pl.BlockSpec((1, tk, D), lambda h, j: (h, j, 0)),
                  pl.BlockSpec((1, tk, DV), lambda h, j: (h, j, 0)),
                  pl.BlockSpec((T, DV), lambda h, j: (0, h)),
                  pl.BlockSpec((1, T // cq, 1, cq), lambda h, j: (h, 0, 0, 0)),
                  pl.BlockSpec((T, DV), lambda h, j: (0, h))],
        out_specs=[pl.BlockSpec((1, T, D), lambda h, j: (h, 0, 0)),
                   pl.BlockSpec((1, tk, D), lambda h, j: (h, j, 0)),
                   pl.BlockSpec((1, tk, DV), lambda h, j: (h, j, 0))],
        out_shape=[_sds((H, T, D), F32), _sds((H, T, D), F32), _sds((H, T, DV), F32)],
        scratch_shapes=[pltpu.VMEM((T // cq, 8, cq), F32)],
        compiler_params=_params(("parallel", "arbitrary")),
    )(q, k, v, o, lse2.reshape(H, T // cq, 1, cq), dmix)


def _split3_dot(ones, x, kind):
    hi = x.astype(BF16)
    rest = x - hi.astype(F32)
    mid = rest.astype(BF16)
    lo = (rest - mid.astype(F32)).astype(BF16)
    return (_dot_raw(ones, hi, kind) + _dot_raw(ones, mid, kind)) + _dot_raw(ones, lo, kind)


@jax.custom_vjp
def _running_sum(x, tri):
    return _split3_dot(tri, x, "nn")


def _running_sum_fwd(x, tri):
    return _split3_dot(tri, x, "nn"), tri


def _running_sum_bwd(tri, g):
    return _split3_dot(tri, g, "tn"), jnp.zeros_like(tri)


_running_sum.defvjp(_running_sum_fwd, _running_sum_bwd)


def _gla_block(hq, hf, hi, lower, st_in, *, rev, dot):
    rows, dk = hq.shape
    G, C = rows // CHUNK, CHUNK
    q = hq * _sigmoid(hq)
    f = lower + (1.0 - lower) * _sigmoid(hf)
    k = 1.0 - f
    logf = jnp.log2(f)
    q3, k3, v3, lf3 = (t.reshape(G, C, dk) for t in (q, k, hi, logf))
    r = lax.broadcasted_iota(jnp.int32, (C, C), 0)
    c = lax.broadcasted_iota(jnp.int32, (C, C), 1)
    tri = ((r <= c) if rev else (r >= c)).astype(F32)
    b = _running_sum(lf3, jnp.broadcast_to(tri, (G, C, C)))
    tpos = lax.broadcasted_iota(jnp.int32, (1, C, 1), 1)
    first_half = (tpos >= C // 2) if rev else (tpos <= C // 2 - 1)
    b_mid = jnp.sum(jnp.where(first_half, lf3, 0.0), axis=1, keepdims=True)
    b_last = jnp.sum(lf3, axis=1, keepdims=True)
    a = dot(q3 * jnp.exp2(b - b_mid), k3 * jnp.exp2(b_mid - b), "nt") * tri
    o_intra = dot(a, v3, "nn")
    kv_t = dot(v3, k3 * jnp.exp2(b_last - b), "tn")
    decay = jnp.exp2(b_last)
    qd = q3 * jnp.exp2(b)
    st = st_in
    o_inter = [None] * G
    for g in (reversed(range(G)) if rev else range(G)):
        o_inter[g] = dot(qd[g], st, "nt")
        st = st * decay[g] + kv_t[g]
    o = o_intra.reshape(rows, dk) + jnp.concatenate(o_inter, axis=0)
    return o, st


def _gla_fwd(z, lower3, *, rev, col_q, col_f, col_v, hp):
    T = z.shape[0]
    rows = min(GLA_GROUP * CHUNK, T)
    nb = T // rows
    wide = hp * HG_DK
    blk = (lambda n: nb - 1 - n) if rev else (lambda n: n)

    def body(hq_ref, hf_ref, hi_ref, low_ref, o_ref, st_out_ref, st_ref):
        @pl.when(pl.program_id(1) == 0)
        def _():
            st_ref[...] = jnp.zeros_like(st_ref)

        st_in = [st_ref[i] for i in range(hp)]
        heads = []
        for i in range(hp):
            cols = slice(i * HG_DK, (i + 1) * HG_DK)
            heads.append(_gla_block(hq_ref[:, cols], hf_ref[:, cols], hi_ref[:, cols], low_ref[i], st_in[i], rev=rev,
                                    dot=_dot_raw))
        for i, (o, st) in enumerate(heads):
            st_out_ref[i, 0] = st_in[i]
            o_ref[:, i * HG_DK:(i + 1) * HG_DK] = o
            st_ref[i] = st

    def zspec(col):
        return pl.BlockSpec((rows, wide), lambda h, n: (blk(n), col // wide + h))

    return pl.pallas_call(
        body, name="gla_fwd_rev" if rev else "gla_fwd", grid=(HG_HEADS // hp, nb),
        in_specs=[zspec(col_q), zspec(col_f), zspec(col_v), pl.BlockSpec((hp, 1, HG_DK), lambda h, n: (h, 0, 0))],
        out_specs=[pl.BlockSpec((rows, wide), lambda h, n: (blk(n), h)),
                   pl.BlockSpec((hp, 1, HG_DK, HG_DK), lambda h, n: (h, blk(n), 0, 0))],
        out_shape=[_sds((T, HG_HEADS * HG_DK), F32), _sds((HG_HEADS, nb, HG_DK, HG_DK), F32)],
        scratch_shapes=[pltpu.VMEM((hp, HG_DK, HG_DK), F32)],
        compiler_params=_params(("parallel", "arbitrary")),
    )(z, z, z, lower3)


def _gla_bwd(z, lower3, states, do, prev, *, rev, col_q, col_f, col_v, hp):
    T = z.shape[0]
    rows = min(GLA_GROUP * CHUNK, T)
    nb = T // rows
    wide = hp * HG_DK
    blk = (lambda n: n) if rev else (lambda n: nb - 1 - n)
    has_prev = prev is not None
    fn = functools.partial(_gla_block, rev=rev, dot=_bdot)

    def body(*refs):
        hq_ref, hf_ref, hi_ref, low_ref, st_ref, do_ref = refs[:6]
        rest = refs[6:]
        if has_prev:
            pq_ref, pi_ref = rest[:2]
            rest = rest[2:]
        dhq_ref, dhi_ref, dhf_ref, dlow_ref, dst_ref = rest
        n = pl.program_id(1)

        @pl.when(n == 0)
        def _():
            dst_ref[...] = jnp.zeros_like(dst_ref)

        dst_in = [dst_ref[i] for i in range(hp)]
        heads = []
        for i in range(hp):
            cols = slice(i * HG_DK, (i + 1) * HG_DK)
            _, vjp = jax.vjp(fn, hq_ref[:, cols], hf_ref[:, cols], hi_ref[:, cols], low_ref[i], st_ref[i, 0])
            dhq, dhf, dhi, dlow, dst = vjp((do_ref[:, cols], dst_in[i]))
            if has_prev:
                dhq = dhq + pq_ref[:, cols]
                dhi = dhi + pi_ref[:, cols]
            heads.append((dhq, dhf, dhi, dlow, dst))
        for i, (dhq, dhf, dhi, dlow, dst) in enumerate(heads):
            cols = slice(i * HG_DK, (i + 1) * HG_DK)
            dst_ref[i] = dst
            dhq_ref[:, cols] = dhq.astype(dhq_ref.dtype)
            dhi_ref[:, cols] = dhi.astype(dhi_ref.dtype)
            dhf_ref[:, cols] = dhf.astype(dhf_ref.dtype)

        @pl.when(n == 0)
        def _():
            for i in range(hp):
                dlow_ref[i] = heads[i][3]

        @pl.when(n > 0)
        def _():
            for i in range(hp):
                dlow_ref[i] += heads[i][3]

    def zspec(col):
        return pl.BlockSpec((rows, wide), lambda h, n: (blk(n), col // wide + h))

    hspec = pl.BlockSpec((rows, wide), lambda h, n: (blk(n), h))
    in_specs = [zspec(col_q), zspec(col_f), zspec(col_v), pl.BlockSpec((hp, 1, HG_DK), lambda h, n: (h, 0, 0)),
                pl.BlockSpec((hp, 1, HG_DK, HG_DK), lambda h, n: (h, blk(n), 0, 0)), hspec]
    ins = [z, z, z, lower3, states, do]
    if has_prev:
        in_specs += [hspec, hspec]
        ins += list(prev)
    full_wide = HG_HEADS * HG_DK
    acc_dtype = BF16 if has_prev else F32
    return pl.pallas_call(
        body, name="gla_bwd_rev" if rev else "gla_bwd", grid=(HG_HEADS // hp, nb),
        in_specs=in_specs,
        out_specs=[hspec, hspec, hspec, pl.BlockSpec((hp, 1, HG_DK), lambda h, n: (h, 0, 0))],
        out_shape=[_sds((T, full_wide), acc_dtype), _sds((T, full_wide), acc_dtype), _sds((T, full_wide), BF16),
                   _sds((HG_HEADS, 1, HG_DK), F32)],
        scratch_shapes=[pltpu.VMEM((hp, HG_DK, HG_DK), F32)],
        compiler_params=_params(("parallel", "arbitrary")),
    )(*ins)


def _lower_fn(lb):
    e = jnp.exp(lb - jnp.max(lb, axis=0, keepdims=True))
    return (e / jnp.sum(e, axis=0, keepdims=True))[0]


def _lower_bounds(lb):
    def body(lb_ref, o_ref):
        o_ref[...] = _lower_fn(lb_ref[...])
    return pl.pallas_call(body, name="lower_bounds", out_shape=_sds(lb.shape[1:], F32))(lb)


def _row_tile(r, cap=1024):
    best = None
    for t in range(16, min(r, cap) + 1, 16):
        if r % t == 0:
            best = t
    return best if best is not None else r


def _sum4(shards, recv, chip, *, name):
    _, R, C = shards.shape
    tr = _row_tile(R)

    def body(chip_ref, o_ref, r_ref, out_ref):
        out_ref[...] = ((o_ref[0].astype(F32) + r_ref[0].astype(F32)) + r_ref[1].astype(F32)) + r_ref[2].astype(F32)

    grid_spec = pltpu.PrefetchScalarGridSpec(
        num_scalar_prefetch=1, grid=(R // tr,),
        in_specs=[pl.BlockSpec((1, tr, C), lambda i, chip_ref: (chip_ref[0], i, 0)),
                  pl.BlockSpec((3, tr, C), lambda i, chip_ref: (0, i, 0))],
        out_specs=pl.BlockSpec((tr, C), lambda i, chip_ref: (i, 0)))
    return pl.pallas_call(
        body, name=name, grid_spec=grid_spec, out_shape=_sds((R, C), F32), compiler_params=_params(("parallel",)),
    )(chip, shards, recv)


def _adamw_math(w, g, m, v):
    m = ADAM_B1 * m + (1.0 - ADAM_B1) * g
    v = ADAM_B2 * v + (1.0 - ADAM_B2) * (g * g)
    m_hat = m / (1.0 - ADAM_B1 ** ADAM_STEP)
    v_hat = v / (1.0 - ADAM_B2 ** ADAM_STEP)
    delta = -ADAM_LR * (m_hat / (jnp.sqrt(v_hat) + ADAM_EPS) + ADAM_WD * w)
    return delta, m, v


def _adamw(w, g_a, g_b, m, v, *, name):
    R, C = w.shape
    tr = _row_tile(R)
    two = g_b is not None

    def body(*refs):
        w_ref, ga_ref = refs[0], refs[1]
        rest = refs[2:]
        g = ga_ref[...]
        if two:
            g = g + rest[0][...]
            rest = rest[1:]
        m_ref, v_ref, g_out, d_out, m_out, v_out, token = rest
        delta, m_new, v_new = _adamw_math(w_ref[...], g, m_ref[...], v_ref[...])
        g_out[...] = g
        d_out[...] = delta
        m_out[...] = m_new
        v_out[...] = v_new
        token[...] = jnp.zeros_like(token)

    spec = pl.BlockSpec((tr, C), lambda i: (i, 0))
    ins = [w, g_a] + ([g_b] if two else []) + [m, v]
    return pl.pallas_call(
        body, name=name, grid=(R // tr,), in_specs=[spec] * len(ins),
        out_specs=[spec] * 4 + [pl.BlockSpec((8, LANE), lambda i: (0, 0))],
        out_shape=[_sds((R, C), F32)] * 4 + [_sds((8, LANE), F32)], compiler_params=_params(("arbitrary",)),
    )(*ins)


def _adamw_small(red, lb_full, ws, ms, vs):
    n = len(ws)

    def pieces(shape):
        out = []
        for j, idx in enumerate(itertools.product(*[range(d) for d in shape[:-1]])):
            out.append((idx[:-1] + (slice(idx[-1], idx[-1] + 1), slice(None)), j * shape[-1]))
        return out

    def body(*refs):
        red_ref, lb_ref = refs[0], refs[1]
        w_refs, m_refs, v_refs = refs[2:2 + n], refs[2 + n:2 + 2 * n], refs[2 + 2 * n:2 + 3 * n]
        out_refs = refs[2 + 3 * n:]
        chip = 2 * lax.axis_index("x") + lax.axis_index("y")
        n_f, shard = lb_ref.shape[-1], w_refs[n - 1].shape[-1]
        _, vjp = jax.vjp(_lower_fn, lb_ref[...])
        dlb = vjp(red_ref[8:10, 0:n_f])[0]
        for i in range(n):
            width = w_refs[i].shape[-1]
            for j, (at, lane) in enumerate(pieces(w_refs[i].shape)):
                if i < n - 1:
                    g = red_ref[i:i + 1, lane:lane + width]
                else:
                    row = dlb[j // 2][j % 2:j % 2 + 1]
                    g = sum(jnp.where(chip == q, row[:, q * shard:(q + 1) * shard], 0.0) for q in range(N_CHIPS))
                delta, m_new, v_new = _adamw_math(w_refs[i][at], g, m_refs[i][at], v_refs[i][at])
                for o_ref, val in zip(out_refs[4 * i:4 * i + 4], (g, delta, m_new, v_new)):
                    o_ref[at] = val

    return pl.pallas_call(
        body, name="adamw_small", out_shape=[_sds(w.shape, F32) for w in ws for _ in range(4)],
    )(red, lb_full, *ws, *ms, *vs)


def _chip_peers():
    x, y, c = lax.axis_index("x"), lax.axis_index("y"), lax.axis_index("c")
    return (x, y, c), 2 * x + y, [(1 - x, y), (x, 1 - y), (1 - x, 1 - y)]


_HBM = pl.BlockSpec(memory_space=pltpu.HBM)
_SEM = pl.BlockSpec(memory_space=pltpu.SEMAPHORE)
_EFFECT = pltpu.SideEffectType.DATAFLOW_SIDE_EFFECTING


def _exchange_copies(srcs, lands, sems, mode):
    (x, y, c), me, chips = _chip_peers()
    copies = []
    for t, (src, land) in enumerate(zip(srcs, lands)):
        if mode == "swap":
            copies.append(pltpu.make_async_remote_copy(src, land, sems[0].at[3 * t], sems[1].at[3 * t],
                                                       device_id=(x, y, 1 - c), device_id_type=MESH))
            continue
        for k, (px, py) in enumerate(chips):
            gather = mode == "gather"
            copies.append(pltpu.make_async_remote_copy(
                src if gather else src.at[2 * px + py], land.at[me] if gather else land.at[k],
                sems[0].at[3 * t + k], sems[1].at[3 * t + k], device_id=(px, py, c), device_id_type=MESH))
        if mode == "gather":
            copies.append(pltpu.make_async_copy(src, land.at[me], sems[2].at[t]))
    return copies


def _exchange_start(srcs, *, mode, name, after=None):
    n = len(srcs)
    n_sem = 3 if mode == "gather" else 2
    n_in = 2 * n + (after is not None)
    land_shape = {"gather": lambda s: (N_CHIPS,) + s.shape, "scatter": lambda s: (3,) + s.shape[1:], "swap": lambda s: s.shape}
    lands = [_sds(land_shape[mode](s), s.dtype) for s in srcs]

    def body(*refs):
        for cp in _exchange_copies(refs[:n], refs[n:2 * n], refs[n_in:n_in + n_sem], mode):
            cp.start()
        token = refs[-1]
        token[...] = jnp.zeros_like(token)

    sem_shapes = [pltpu.SemaphoreType.DMA((3 * n,)), pltpu.SemaphoreType.DMA((3 * n,))]
    sem_shapes += [pltpu.SemaphoreType.DMA((n,))] if mode == "gather" else []
    thru = [pltpu.HBM(s.shape, s.dtype) for s in srcs] + [pltpu.HBM(l.shape, l.dtype) for l in lands]
    res = pl.pallas_call(
        body, name=name, in_specs=[_HBM] * (2 * n) + [_ANY] * (after is not None),
        out_specs=[_SEM] * n_sem + [_HBM] * (2 * n) + [pl.BlockSpec(memory_space=pltpu.VMEM)],
        out_shape=sem_shapes + thru + [_sds((8, LANE), F32)], input_output_aliases={i: n_sem + i for i in range(2 * n)},
        compiler_params=pltpu.CompilerParams(has_side_effects=_EFFECT),
    )(*[pltpu.with_memory_space_constraint(s, pltpu.HBM) for s in srcs],
      *[pltpu.with_memory_space_constraint(lax.empty(l.shape, l.dtype), pltpu.HBM) for l in lands],
      *([after] if after is not None else []))
    return (res[:n_sem], res[n_sem:n_sem + n], res[n_sem + n:n_sem + 2 * n], mode), res[-1]


def _exchange_wait(started, after, *, name):
    sems, srcs, lands, mode = started
    n, n_sem = len(srcs), len(sems)
    after = list(after) if isinstance(after, (list, tuple)) else [after]

    def body(*refs):
        for cp in _exchange_copies(refs[:n], refs[n:2 * n], refs[2 * n:2 * n + n_sem], mode):
            cp.wait()

    res = pl.pallas_call(
        body, name=name, in_specs=[_HBM] * (2 * n) + [_SEM] * n_sem + [_ANY] * len(after), out_specs=[_HBM] * (2 * n),
        out_shape=[pltpu.HBM(a.shape, a.dtype) for a in list(srcs) + list(lands)],
        input_output_aliases={i: i for i in range(2 * n)},
        compiler_params=pltpu.CompilerParams(has_side_effects=_EFFECT),
    )(*srcs, *lands, *sems, *after)
    return res[:n], res[n:]


def _allreduce_small(pack, after):
    R, C = pack.shape

    def body(in_ref, after_ref, out_ref, slots, send_sems, recv_sems):
        x, y, c = lax.axis_index("x"), lax.axis_index("y"), lax.axis_index("c")
        me = 4 * x + 2 * y + c
        slots[me] = in_ref[...]
        copies = []
        for k in range(1, N_DEV):
            peer = (x ^ ((k >> 2) & 1), y ^ ((k >> 1) & 1), c ^ (k & 1))
            cp = pltpu.make_async_remote_copy(in_ref, slots.at[me], send_sems.at[k - 1], recv_sems.at[k - 1],
                                              device_id=peer, device_id_type=MESH)
            cp.start()
            copies.append(cp)
        for cp in copies:
            cp.wait()
        acc = slots[0]
        for d in range(1, N_DEV):
            acc = acc + slots[d]
        out_ref[...] = acc

    return pl.pallas_call(
        body, name="allreduce_small", out_shape=_sds((R, C), F32),
        in_specs=[pl.BlockSpec(memory_space=pltpu.VMEM), _ANY], out_specs=pl.BlockSpec(memory_space=pltpu.VMEM),
        scratch_shapes=[pltpu.VMEM((N_DEV, R, C), F32), pltpu.SemaphoreType.DMA((N_DEV - 1,)),
                        pltpu.SemaphoreType.DMA((N_DEV - 1,))],
        compiler_params=_params(),
    )(pack, after)


_Z_CQ, _Z_CKV, _Z_HQ, _Z_HFF, _Z_HFB, _Z_HI, _Z_HG, _Z_KR, _Z_END = 0, 256, 512, 1024, 1536, 2048, 2560, 3072, 3200


def _to_z_layout(wt):
    pad = jnp.zeros((_Z_END - _Z_KR - QK_ROPE, wt.shape[1]), wt.dtype)
    return jnp.concatenate([wt[:512], wt[512 + QK_ROPE:], wt[512:512 + QK_ROPE], pad], axis=0)


def _from_z_layout(wt):
    return jnp.concatenate([wt[:512], wt[_Z_KR:_Z_KR + QK_ROPE], wt[512:_Z_KR]], axis=0)


def _col_shards_to_full(g):
    return jnp.transpose(g, (1, 0, 2)).reshape(g.shape[1], -1)


def _full_to_col_shards(w):
    r, c = w.shape
    return jnp.transpose(w.reshape(r, N_CHIPS, c // N_CHIPS), (1, 0, 2))


def _full_to_row_shards(w):
    r, c = w.shape
    return w.reshape(N_CHIPS, r // N_CHIPS, c)


def kernel(x, p, positions, g_mix, w_in, g_qa, g_kva, w_qb, w_kvb, g_qn, g_kn, lb_param, g_hgo, w_o, g_ffn, w_gate, w_up, w_down, g_ple, w_ple_gate, w_ple_proj, loss_target, m_g_mix, m_w_in, m_g_qa, m_g_kva, m_w_qb, m_w_kvb, m_g_qn, m_g_kn, m_lb_param, m_g_hgo, m_w_o, m_g_ffn, m_w_gate, m_w_up, m_w_down, m_g_ple, m_w_ple_gate, m_w_ple_proj, v_g_mix, v_w_in, v_g_qa, v_g_kva, v_w_qb, v_w_kvb, v_g_qn, v_g_kn, v_lb_param, v_g_hgo, v_w_o, v_g_ffn, v_w_gate, v_w_up, v_w_down, v_g_ple, v_w_ple_gate, v_w_ple_proj):
    w_named = dict(g_mix=g_mix, w_in=w_in, g_qa=g_qa, g_kva=g_kva, w_qb=w_qb, w_kvb=w_kvb, g_qn=g_qn, g_kn=g_kn,
                   lb_param=lb_param, g_hgo=g_hgo, w_o=w_o, g_ffn=g_ffn, w_gate=w_gate, w_up=w_up, w_down=w_down,
                   g_ple=g_ple, w_ple_gate=w_ple_gate, w_ple_proj=w_ple_proj)
    m_named = dict(g_mix=m_g_mix, w_in=m_w_in, g_qa=m_g_qa, g_kva=m_g_kva, w_qb=m_w_qb, w_kvb=m_w_kvb, g_qn=m_g_qn,
                   g_kn=m_g_kn, lb_param=m_lb_param, g_hgo=m_g_hgo, w_o=m_w_o, g_ffn=m_g_ffn, w_gate=m_w_gate,
                   w_up=m_w_up, w_down=m_w_down, g_ple=m_g_ple, w_ple_gate=m_w_ple_gate, w_ple_proj=m_w_ple_proj)
    v_named = dict(g_mix=v_g_mix, w_in=v_w_in, g_qa=v_g_qa, g_kva=v_g_kva, w_qb=v_w_qb, w_kvb=v_w_kvb, g_qn=v_g_qn,
                   g_kn=v_g_kn, lb_param=v_lb_param, g_hgo=v_g_hgo, w_o=v_w_o, g_ffn=v_g_ffn, w_gate=v_w_gate,
                   w_up=v_w_up, w_down=v_w_down, g_ple=v_g_ple, w_ple_gate=v_w_ple_gate, w_ple_proj=v_w_ple_proj)
    order = list(w_named)
    transposed = ("w_in", "w_qb", "w_gate", "w_up")
    col_sharded = ("w_kvb", "w_ple_proj")
    row_sharded = ("w_o", "w_down", "w_ple_gate")
    big = transposed + col_sharded + row_sharded

    def view(n, a):
        return jnp.transpose(a[0]) if n in transposed else a[0]

    def unview(n, a):
        return (jnp.transpose(a) if n in transposed else a)[None]

    def to_shards(n, g):
        return _full_to_col_shards(g) if n in col_sharded else _full_to_row_shards(g)

    x2d, p2d, tgt = x[0], p[0, 0], loss_target[0]
    T, D = x2d.shape

    lb_flat = lb_param.reshape(-1, lb_param.shape[-1])
    gather_groups = (("w_in",), ("w_qb", "w_kvb"), ("w_o", "w_gate", "w_up", "w_down", "w_ple_gate", "w_ple_proj"))
    gather_started = []

    casts = {n: view(n, w_named[n]).astype(BF16) for n in big}

    def gather_start(gi, after):
        srcs = [casts[n] for n in gather_groups[gi]] + ([lb_flat] if gi == 0 else [])
        started, token = _exchange_start(srcs, mode="gather", name=f"gather_start_{gi}", after=after)
        gather_started.append(started)
        return token

    full = {}

    def gather_wait(gi, after):
        _, got = _exchange_wait(gather_started[gi], after, name=f"gather_wait_{gi}")
        for n, g in zip(gather_groups[gi], got):
            full[n] = _col_shards_to_full(g) if n in col_sharded else g.reshape(-1, g.shape[-1])
        return got

    g_hgo_row = g_hgo.reshape(1, -1)

    inv_freq = ROPE_THETA ** (-jnp.arange(0, QK_ROPE, 2, dtype=F32) / QK_ROPE)
    ang = positions[0].astype(F32)[:, None] * inv_freq
    cos, sin = jnp.cos(ang), jnp.sin(ang)
    token = gather_start(0, None)
    h1 = _stage(_rms, [x2d], [g_mix], [_sds((T, D), BF16)], [], name="norm_mix", after=token)[0]
    got = gather_wait(0, [h1, cos, sin] + [casts[n] for g in gather_groups[1:] for n in g])
    token = got[0]
    for gi in range(1, len(gather_groups)):
        token = gather_start(gi, token)
    lb_full = _col_shards_to_full(got[-1]).reshape(lb_param.shape[0], lb_param.shape[1], -1)
    w_in_zt = _to_z_layout(full["w_in"])
    z = _mm(h1, w_in_zt, tb=True, name="in_proj", after=token)
    qn, kvn = _stage(_mla_a_fn, [_cols(z, 256, 0), _cols(z, 256, 1)], [g_qa, g_kva],
                     [_sds((T, 256), BF16), _sds((T, 256), BF16)], [], name="mla_latent_norm")
    gather_wait(1, qn)
    q_raw = _mm(qn, full["w_qb"], tb=True, name="q_up")
    kv_raw = _mm(kvn, full["w_kvb"], name="kv_up")
    kr = _cols(z, LANE, _Z_KR // LANE)
    q, k, v = _stage(_mla_b_fn, [q_raw, kv_raw, kr, cos, sin], [g_qn, g_kn],
                     [_sds((MLA_HEADS, T, QK_HEAD), BF16), _sds((MLA_HEADS, T, QK_HEAD), BF16),
                      _sds((MLA_HEADS, T, V_HEAD), BF16)], [], name="mla_qk_norm_rope")
    att, lse = _attention_fwd(q, k, v)

    lower = _lower_bounds(lb_full)
    lower3 = lower.reshape(2, HG_HEADS, 1, HG_DK)
    o_f, st_f = _gla_fwd(z, lower3[0], rev=False, col_q=_Z_HQ, col_f=_Z_HFF, col_v=_Z_HI, hp=GLA_FWD_HEADS)
    o_b, st_b = _gla_fwd(z, lower3[1], rev=True, col_q=_Z_HQ, col_f=_Z_HFB, col_v=_Z_HI, hp=GLA_FWD_HEADS)
    hg = _cols(z, 512, _Z_HG // 512)
    mix = _stage(_post_fn, [att, o_f, o_b, hg], [g_hgo_row], [_sds((T, att.shape[1] + o_f.shape[1]), BF16)], [],
                 name="mix_out")[0]
    gather_wait(2, mix)
    x1, h2 = _mm_fused(mix, full["w_o"], _resid_norm_fn, [x2d], [g_ffn], [F32, BF16], [], full_rows=True,
                       name="out_proj")
    gt, up, act = _mm_fused(h2, full["w_gate"], lambda gt, up: (gt, up, _swiglu_fn(gt, up)), [], [], [BF16, BF16, BF16],
                            [], tb=True, b2=full["w_up"], name="ffn_gate_up")
    x2, h3 = _mm_fused(act, full["w_down"], _resid_norm_fn, [x1], [g_ple], [F32, BF16], [], full_rows=True,
                       name="ffn_down")
    pp = _mm(p2d, full["w_ple_proj"], name="ple_proj")
    dx3, dpg, dpp, loss_part = _mm_fused(
        h3, full["w_ple_gate"], lambda acc, pp, x2, tgt: _ple_loss_fn(x2, acc, pp, tgt), [pp, x2, tgt], [],
        [F32, BF16, BF16], [_sds((1, 1), F32)], full_rows=True, name="ple_gate_loss")

    grads = {}
    scatter_groups = (("w_ple_proj", "w_ple_gate", "w_down", "w_gate", "w_up", "w_o"), ("w_qb", "w_kvb", "w_in"))
    scatter_started = []

    def scatter_start(gi):
        srcs = [to_shards(n, grads[n]) for n in scatter_groups[gi]]
        started, token = _exchange_start(srcs, mode="scatter", name=f"scatter_start_{gi}")
        scatter_started.append(started)
        return token

    chip = 2 * lax.axis_index("x") + lax.axis_index("y")
    swap_started = []

    def reduce_group(gi, after):
        shards, recvs = _exchange_wait(scatter_started[gi], after, name=f"scatter_wait_{gi}")
        sums = [_sum4(s, r, chip.reshape(1), name="sum_" + n) for n, s, r in zip(scatter_groups[gi], shards, recvs)]
        started, token = _exchange_start(sums, mode="swap", name=f"swap_start_{gi}")
        swap_started.append(started)
        return token

    grads["w_ple_proj"] = _mm(p2d, dpp, ta=True, out_dtype=BF16, name="d_w_ple_proj")
    grads["w_ple_gate"] = _mm(h3, dpg, ta=True, out_dtype=BF16, name="d_w_ple_gate")
    dx2, grads["g_ple"] = _mm_fused(
        dpg, full["w_ple_gate"], lambda acc, x2, dx3, g: _norm_bwd_fn(x2, acc, dx3, g), [x2, dx3], [g_ple],
        [F32], [_sds((1, D), F32)], tb=True, full_rows=True, name="d_h3_norm_ple_bwd")
    dgt, dup = _mm_fused(dx2, full["w_down"], lambda acc, gt, up: _swiglu_bwd_fn(gt.astype(F32), up.astype(F32), acc), [gt, up], [],
                         [BF16, BF16], [], tb=True, name="d_act_swiglu_bwd")
    grads["w_down"] = _mm(act, dx2, ta=True, out_dtype=BF16, name="d_w_down")
    grads["w_gate"] = _mm(dgt, h2, ta=True, out_dtype=BF16, name="d_w_gate")
    grads["w_up"] = _mm(dup, h2, ta=True, out_dtype=BF16, name="d_w_up")
    dx1, grads["g_ffn"] = _mm_fused(
        dgt, full["w_gate"], lambda acc, x1, dx2, g: _norm_bwd_fn(x1, acc, dx2, g), [x1, dx2], [g_ffn],
        [F32], [_sds((1, D), F32)], full_rows=True, a2=dup, b2=full["w_up"], name="d_h2_norm_ffn_bwd")
    grads["w_o"] = _mm(mix, dx1, ta=True, out_dtype=BF16, name="d_w_o")
    token = scatter_start(0)
    dmix = _mm(dx1, full["w_o"], tb=True, name="d_mix", after=token)

    half = MLA_HEADS * V_HEAD
    do, dhg, dg_hgo = _stage(_post_bwd_fn, [o_f, o_b, hg, _cols(dmix, half, 1)], [g_hgo_row],
                             [_sds((T, half), F32), _sds((T, half), BF16)], [_sds((1, half), F32)], name="mix_out_bwd")
    grads["g_hgo"] = dg_hgo
    dhq_f, dhi_f, dhf_f, dlow_f = _gla_bwd(z, lower3[0], st_f, do, None, rev=False,
                                           col_q=_Z_HQ, col_f=_Z_HFF, col_v=_Z_HI, hp=GLA_BWD_HEADS)
    dhq, dhi, dhf_b, dlow_b = _gla_bwd(z, lower3[1], st_b, do, (dhq_f, dhi_f), rev=True,
                                       col_q=_Z_HQ, col_f=_Z_HFB, col_v=_Z_HI, hp=GLA_BWD_HEADS)

    dq, dk, dv = _attention_bwd(q, k, v, att, lse, dmix)
    dq_raw, dkv_raw, dkr, grads["g_qn"], grads["g_kn"] = _stage(
        _mla_b_bwd_fn, [q_raw, kv_raw, kr, cos, sin, dq, dk, dv], [g_qn, g_kn],
        [_sds(q_raw.shape, BF16), _sds(kv_raw.shape, BF16), _sds((T, LANE), BF16)],
        [_sds(g_qn.shape, F32), _sds(g_kn.shape, F32)], name="mla_qk_norm_rope_bwd")
    grads["w_qb"] = _mm(dq_raw, qn, ta=True, out_dtype=BF16, name="d_w_qb")
    grads["w_kvb"] = _mm(kvn, dkv_raw, ta=True, out_dtype=BF16, name="d_w_kvb")
    dqn = _mm(dq_raw, full["w_qb"], name="d_qn")
    dkvn = _mm(dkv_raw, full["w_kvb"], tb=True, name="d_kvn")
    dcq, dckv, grads["g_qa"], grads["g_kva"] = _stage(
        _mla_a_bwd_fn, [_cols(z, 256, 0), _cols(z, 256, 1), dqn, dkvn], [g_qa, g_kva],
        [_sds((T, 256), BF16), _sds((T, 256), BF16)], [_sds(g_qa.shape, F32), _sds(g_kva.shape, F32)],
        name="mla_latent_norm_bwd")
    token = reduce_group(0, dcq)
    dz = jnp.concatenate([dcq, dckv, dhq, dhf_f, dhf_b, dhi, dhg, dkr], axis=1)
    grads["w_in"] = _from_z_layout(_mm(dz, h1, ta=True, out_dtype=BF16, name="d_w_in", after=token))
    token = scatter_start(1)
    grad_x, grads["g_mix"] = _mm_fused(
        dz, w_in_zt, lambda acc, x, dx1, g: _norm_bwd_fn(x, acc, dx1, g), [x2d, dx1], [g_mix],
        [F32], [_sds((1, D), F32)], full_rows=True, name="d_h1_norm_mix_bwd", after=token)

    out_g, out_d, out_m, out_v = {}, {}, {}, {}

    def update_group(gi, after):
        mine, theirs = _exchange_wait(swap_started[gi], after, name=f"swap_wait_{gi}")
        tokens = []
        for n, a, b in zip(scatter_groups[gi], mine, theirs):
            *res, token = _adamw(view(n, w_named[n]), a, b, view(n, m_named[n]), view(n, v_named[n]), name="adamw_" + n)
            out_g[n], out_d[n], out_m[n], out_v[n] = (unview(n, t) for t in res)
            tokens.append(token)
        return tokens

    done = update_group(0, grads["g_mix"])
    token = reduce_group(1, done)
    done = update_group(1, token)

    small = ("g_mix", "g_qa", "g_kva", "g_qn", "g_kn", "g_hgo", "g_ffn", "g_ple")
    small_all = small + ("lb_param",)
    width = -(-max(w_named[n].size for n in small_all) // LANE) * LANE

    def row(a):
        a = a.reshape(1, -1)
        return jnp.pad(a, ((0, 0), (0, width - a.shape[1])))

    dlower = jnp.concatenate([dlow_f.reshape(1, -1), dlow_b.reshape(1, -1)], axis=0)
    pack = jnp.concatenate([row(grads[n]) for n in small] + [row(dlower[0]), row(dlower[1]), row(loss_part)]
                           + [jnp.zeros((5, width), F32)], axis=0)
    red = _allreduce_small(pack, done[-1])
    loss = red[10, 0]

    outs = _adamw_small(red, lb_full, [w_named[n] for n in small_all], [m_named[n] for n in small_all],
                        [v_named[n] for n in small_all])
    for i, n in enumerate(small_all):
        out_g[n], out_d[n], out_m[n], out_v[n] = outs[4 * i:4 * i + 4]

    return (loss, grad_x[None], *[out_g[n] for n in order], *[out_d[n] for n in order],
            *[out_m[n] for n in order], *[out_v[n] for n in order])
```

```python
import functools
import itertools

import jax
import jax.numpy as jnp
from jax import lax
from jax.experimental import pallas as pl
from jax.experimental.pallas import tpu as pltpu

F32 = jnp.float32
BF16 = jnp.bfloat16
MESH = pl.DeviceIdType.MESH

EPS = 1e-6
ROPE_THETA = 10000.0
MLA_HEADS = 4
QK_NOPE = 128
QK_ROPE = 64
QK_HEAD = QK_NOPE + QK_ROPE
V_HEAD = 128
HG_HEADS = 4
HG_DK = 128
CHUNK = 64
ADAM_LR = 0.001
ADAM_B1 = 0.9
ADAM_B2 = 0.999
ADAM_EPS = 1e-08
ADAM_WD = 0.01
ADAM_STEP = 10

LANE = 128
VMEM_LIMIT = 56 * 1024 * 1024
TOK_TILE = 256
GLA_GROUP = 16
GLA_FWD_HEADS = 4
GLA_BWD_HEADS = 2
ATT_TQ = 1024
ATT_TK = 1024
ATT_CHUNK = 1024
LOG2_E = 1.4426950408889634
N_CHIPS = 4
N_DEV = 8


_ANY = pl.BlockSpec(memory_space=pl.ANY)


def _params(dims=None, **kw):
    return pltpu.CompilerParams(dimension_semantics=dims, vmem_limit_bytes=VMEM_LIMIT, **kw)


def _tile_candidates(n, cap):
    out = [d for d in range(LANE, min(n, cap) + 1, LANE) if n % d == 0]
    if n <= cap and n not in out:
        out.append(n)
    return out or [n]


MM_VMEM_BUDGET = 40 * 1024 * 1024
MM_MIN_ROWS = 256
MM_MAX_ROWS = 1536
HBM_BYTES_PER_S = 2.8e12
MXU_FLOPS_PER_S = 8e14
STEP_OVERHEAD_S = 0.35e-6


def _mm_tiles(M, N, K, a_bytes, b_bytes, o_bytes, has_add, full_rows=False, full_k=False, n_a=1, n_b=1):
    cast_a, cast_b = a_bytes > 2, b_bytes > 2
    a_bytes, b_bytes = n_a * a_bytes, n_b * b_bytes
    best = None
    for tm in [t for t in _tile_candidates(M, MM_MAX_ROWS) if t >= min(M, MM_MIN_ROWS)]:
        for tn in ([N] if full_rows else _tile_candidates(N, N)):
            for tk in ([K] if full_k else _tile_candidates(K, K)):
                ni, nj, nk = M // tm, N // tn, K // tk
                vmem = 2 * (tm * tk * a_bytes + tk * tn * b_bytes + tm * tn * o_bytes * (2 if has_add else 1))
                vmem += tm * tn * 4 * (2 if nk > 1 else 1)
                vmem += (tm * tk * 2 * n_a if cast_a else 0) + (tk * tn * 2 * n_b if cast_b else 0)
                if vmem > MM_VMEM_BUDGET:
                    continue
                moved = M * K * a_bytes * (nj if nk > 1 else 1) + K * N * b_bytes * (1 if nj == nk == 1 else ni)
                moved += M * N * o_bytes * (2 if has_add else 1)
                t = max(moved / HBM_BYTES_PER_S, 2 * M * N * K / MXU_FLOPS_PER_S) + ni * nj * nk * STEP_OVERHEAD_S
                if best is None or t < best[0]:
                    best = (t, tm, tn, tk)
    assert best is not None, (M, N, K)
    return best[1:]


def _dot_raw(a, b, kind):
    nb = a.ndim - 2
    batch = ((0,), (0,)) if nb else ((), ())
    ca = nb if kind == "tn" else nb + 1
    cb = nb + 1 if kind == "nt" else nb
    return lax.dot_general(a.astype(BF16), b.astype(BF16), (((ca,), (cb,)), batch), preferred_element_type=F32)


@functools.partial(jax.custom_vjp, nondiff_argnums=(2,))
def _bdot(a, b, kind):
    return _dot_raw(a, b, kind)


def _bdot_fwd(a, b, kind):
    return _dot_raw(a, b, kind), (a, b)


def _bdot_bwd(kind, res, g):
    a, b = res
    if kind == "nn":
        da, db = _bdot(g, b, "nt"), _bdot(a, g, "tn")
    elif kind == "nt":
        da, db = _bdot(g, b, "nn"), _bdot(g, a, "tn")
    else:
        da, db = _bdot(b, g, "nt"), _bdot(a, g, "nn")
    return da.astype(a.dtype), db.astype(b.dtype)


_bdot.defvjp(_bdot_fwd, _bdot_bwd)


def _mm(a, b, *, name, ta=False, tb=False, add=None, out_dtype=F32, after=None):
    K, M = a.shape if ta else a.shape[::-1]
    N, Kb = b.shape if tb else b.shape[::-1]
    assert K == Kb, (a.shape, b.shape, ta, tb)
    tm, tn, tk = _mm_tiles(M, N, K, a.dtype.itemsize, b.dtype.itemsize, jnp.dtype(out_dtype).itemsize, add is not None)
    nk = K // tk
    kind = "tn" if ta else ("nt" if tb else "nn")
    assert not (ta and tb)
    a_spec = pl.BlockSpec((tk, tm), lambda i, j, k: (k, i)) if ta else pl.BlockSpec((tm, tk), lambda i, j, k: (i, k))
    b_spec = pl.BlockSpec((tn, tk), lambda i, j, k: (j, k)) if tb else pl.BlockSpec((tk, tn), lambda i, j, k: (k, j))
    o_spec = pl.BlockSpec((tm, tn), lambda i, j, k: (i, j))
    has_add = add is not None

    def body(*refs):
        a_ref, b_ref = refs[0], refs[1]
        add_ref = refs[2] if has_add else None
        o_ref = refs[n_in]
        part = _dot_raw(a_ref[...], b_ref[...], kind)
        if nk == 1:
            if has_add:
                part = part + add_ref[...].astype(F32)
            o_ref[...] = part.astype(o_ref.dtype)
            return
        acc_ref = refs[-1]
        k = pl.program_id(2)

        @pl.when(k == 0)
        def _():
            acc_ref[...] = part

        @pl.when(k > 0)
        def _():
            acc_ref[...] += part

        @pl.when(k == nk - 1)
        def _():
            r = acc_ref[...]
            if has_add:
                r = r + add_ref[...].astype(F32)
            o_ref[...] = r.astype(o_ref.dtype)

    ins = [a, b] + ([add] if has_add else []) + ([after] if after is not None else [])
    in_specs = [a_spec, b_spec] + ([o_spec] if has_add else []) + ([_ANY] if after is not None else [])
    n_in = len(ins)
    return pl.pallas_call(
        body, name=name, grid=(M // tm, N // tn, nk), in_specs=in_specs, out_specs=o_spec,
        out_shape=jax.ShapeDtypeStruct((M, N), out_dtype),
        scratch_shapes=[pltpu.VMEM((tm, tn), F32)] if nk > 1 else [],
        compiler_params=_params(("parallel", "parallel", "arbitrary")),
    )(*ins)


def _mm_fused(a, b, fn, tiles, params, out_dtypes, sums, *, name, ta=False, tb=False, full_rows=False, after=None,
              b2=None, a2=None):
    K, M = a.shape if ta else a.shape[::-1]
    N, Kb = b.shape if tb else b.shape[::-1]
    assert K == Kb and not (ta and tb), (a.shape, b.shape, ta, tb)
    per_elem = sum(t.dtype.itemsize for t in tiles) + sum(jnp.dtype(d).itemsize for d in out_dtypes)
    n_b = 1 if b2 is None else 2
    n_a = 1 if a2 is None else 2
    tm, tn, tk = _mm_tiles(M, N, K, a.dtype.itemsize, b.dtype.itemsize, per_elem, False, full_rows, b2 is not None,
                           n_a, n_b)
    nk = K // tk
    kind = "tn" if ta else ("nt" if tb else "nn")
    a_spec = pl.BlockSpec((tk, tm), lambda i, j, k: (k, i)) if ta else pl.BlockSpec((tm, tk), lambda i, j, k: (i, k))
    b_spec = pl.BlockSpec((tn, tk), lambda i, j, k: (j, k)) if tb else pl.BlockSpec((tk, tn), lambda i, j, k: (k, j))
    o_spec = pl.BlockSpec((tm, tn), lambda i, j, k: (i, j))
    ins = [a, b] + ([b2] if b2 is not None else []) + ([a2] if a2 is not None else [])
    ins += list(tiles) + list(params) + ([after] if after is not None else [])
    in_specs = [a_spec] + [b_spec] * n_b + [a_spec] * (n_a - 1) + [o_spec] * len(tiles)
    in_specs += [pl.BlockSpec(p.shape, lambda i, j, k, nd=p.ndim: (0,) * nd) for p in params]
    in_specs += [_ANY] if after is not None else []
    n_in, n_t, n_p, n_o = len(ins), len(tiles), len(params), len(out_dtypes)

    def body(*refs):
        outs, sum_refs = refs[n_in:n_in + n_o], refs[n_in + n_o:n_in + n_o + len(sums)]

        def finish(*products):
            res = fn(*products, *[t[...] for t in refs[n_a + n_b:n_a + n_b + n_t + n_p]])
            for o_ref, v in zip(outs, res[:n_o]):
                o_ref[...] = v.astype(o_ref.dtype)
            first = jnp.logical_and(pl.program_id(0) == 0, pl.program_id(1) == 0)
            for s_ref, v in zip(sum_refs, res[n_o:]):
                @pl.when(first)
                def _(s_ref=s_ref, v=v):
                    s_ref[...] = v

                @pl.when(jnp.logical_not(first))
                def _(s_ref=s_ref, v=v):
                    s_ref[...] += v

        part = _dot_raw(refs[0][...], refs[1][...], kind)
        if nk == 1 and a2 is not None:
            finish(part + _dot_raw(refs[3][...], refs[2][...], kind))
            return
        if nk == 1:
            finish(part, *([_dot_raw(refs[0][...], refs[2][...], kind)] if b2 is not None else []))
            return
        acc_ref = refs[-1]
        k = pl.program_id(2)

        @pl.when(k == 0)
        def _():
            acc_ref[...] = part

        @pl.when(k > 0)
        def _():
            acc_ref[...] += part

        @pl.when(k == nk - 1)
        def _():
            finish(acc_ref[...])

    out_shape = [_sds((M, N), d) for d in out_dtypes] + list(sums)
    out_specs = [o_spec] * n_o + [pl.BlockSpec(s.shape, lambda i, j, k, nd=len(s.shape): (0,) * nd) for s in sums]
    order = ("arbitrary",) * 3 if sums else ("parallel", "parallel", "arbitrary")
    return pl.pallas_call(
        body, name=name, grid=(M // tm, N // tn, nk), in_specs=in_specs, out_specs=out_specs, out_shape=out_shape,
        scratch_shapes=[pltpu.VMEM((tm, tn), F32)] if nk > 1 else [], compiler_params=_params(order),
    )(*ins)


def _cols(arr, width, block):
    return (arr, width, block)


def _stage(fn, tiles, params, out_tiles, out_sums, *, name, tile=TOK_TILE, after=None):
    def tok_spec(shape, width=None, block=0):
        if len(shape) == 2:
            w = shape[1] if width is None else width
            return pl.BlockSpec((tile, w), lambda i: (i, block))
        return pl.BlockSpec((shape[0], tile, shape[2]), lambda i: (0, i, 0))

    arrays, in_specs = [], []
    for t in tiles:
        if isinstance(t, tuple):
            arr, width, block = t
            arrays.append(arr)
            in_specs.append(tok_spec(arr.shape, width, block))
        else:
            arrays.append(t)
            in_specs.append(tok_spec(t.shape))
    n_tok = arrays[0].shape[0] if arrays[0].ndim == 2 else arrays[0].shape[1]
    for p in params:
        arrays.append(p)
        in_specs.append(pl.BlockSpec(p.shape, lambda i, nd=p.ndim: (0,) * nd))
    out_shape = list(out_tiles) + list(out_sums)
    out_specs = [tok_spec(o.shape) for o in out_tiles]
    out_specs += [pl.BlockSpec(o.shape, lambda i, nd=len(o.shape): (0,) * nd) for o in out_sums]
    n_fn, n_ot = len(arrays), len(out_tiles)
    if after is not None:
        arrays.append(after)
        in_specs.append(_ANY)
    n_in = len(arrays)

    def body(*refs):
        res = fn(*[r[...] for r in refs[:n_fn]])
        if not isinstance(res, (tuple, list)):
            res = (res,)
        outs = refs[n_in:]
        for o_ref, r in zip(outs[:n_ot], res[:n_ot]):
            o_ref[...] = r.astype(o_ref.dtype)
        i = pl.program_id(0)
        for o_ref, r in zip(outs[n_ot:], res[n_ot:]):
            @pl.when(i == 0)
            def _(o_ref=o_ref, r=r):
                o_ref[...] = r.astype(o_ref.dtype)

            @pl.when(i > 0)
            def _(o_ref=o_ref, r=r):
                o_ref[...] += r.astype(o_ref.dtype)

    res = pl.pallas_call(
        body, name=name, grid=(n_tok // tile,), in_specs=in_specs, out_specs=out_specs, out_shape=out_shape,
        compiler_params=_params(("arbitrary",)),
    )(*arrays)
    return res


def _sds(shape, dtype):
    return jax.ShapeDtypeStruct(tuple(shape), dtype)


def _sigmoid(x):
    return 0.5 * jnp.tanh(0.5 * x) + 0.5


def _rms(x, g):
    return x * lax.rsqrt(jnp.mean(x * x, axis=-1, keepdims=True) + EPS) * g


def _norm_bwd_fn(x, dh, dres, g):
    r = lax.rsqrt(jnp.mean(x * x, axis=-1, keepdims=True) + EPS)
    xr = x * r
    dhg = dh * g
    dx = r * (dhg - xr * jnp.mean(xr * dhg, axis=-1, keepdims=True))
    return dx + dres, jnp.sum(dh * xr, axis=0, keepdims=True)


def _mla_a_fn(cq, ckv, g_qa, g_kva):
    return _rms(cq, g_qa), _rms(ckv, g_kva)


def _mla_a_bwd_fn(cq, ckv, dqn, dkvn, g_qa, g_kva):
    _, vjp = jax.vjp(_mla_a_fn, cq, ckv, g_qa, g_kva)
    return vjp((dqn, dkvn))


def _rope(t, cos, sin):
    t1, t2 = t[:, :QK_ROPE // 2], t[:, QK_ROPE // 2:]
    return jnp.concatenate([t1 * cos - t2 * sin, t1 * sin + t2 * cos], axis=-1)


def _mla_b_fn(q_raw, kv_raw, kr, cos, sin, g_qn, g_kn):
    krope = kr[:, :QK_ROPE]
    qs, ks, vs = [], [], []
    for h in range(MLA_HEADS):
        qh = _rms(q_raw[:, h * QK_HEAD:(h + 1) * QK_HEAD], g_qn)
        kvh = kv_raw[:, h * (QK_NOPE + V_HEAD):(h + 1) * (QK_NOPE + V_HEAD)]
        kh = _rms(jnp.concatenate([kvh[:, :QK_NOPE], krope], axis=-1), g_kn)
        qs.append(jnp.concatenate([qh[:, :QK_NOPE], _rope(qh[:, QK_NOPE:], cos, sin)], axis=-1))
        ks.append(jnp.concatenate([kh[:, :QK_NOPE], _rope(kh[:, QK_NOPE:], cos, sin)], axis=-1))
        vs.append(kvh[:, QK_NOPE:])
    return jnp.stack(qs), jnp.stack(ks), jnp.stack(vs)


def _mla_b_bwd_fn(q_raw, kv_raw, kr, cos, sin, dq, dk, dv, g_qn, g_kn):
    _, vjp = jax.vjp(lambda a, b, c, d, e: _mla_b_fn(a, b, c, cos, sin, d, e), q_raw, kv_raw, kr, g_qn, g_kn)
    return vjp((dq, dk, dv))


def _post_fn(a, o_f, o_b, hg, g_hgo):
    o = o_f + o_b
    parts = [a]
    for h in range(HG_HEADS):
        s = slice(h * HG_DK, (h + 1) * HG_DK)
        gate = hg[:, s]
        parts.append(_rms(o[:, s], g_hgo[:, s]) * (gate * _sigmoid(gate)))
    return jnp.concatenate(parts, axis=-1)


def _post_bwd_fn(o_f, o_b, hg, dr, g_hgo):
    def f(o, hg, g):
        return _post_fn(jnp.zeros_like(o), o, jnp.zeros_like(o), hg, g)[:, o.shape[1]:]
    _, vjp = jax.vjp(f, o_f + o_b, hg, g_hgo)
    return vjp(dr)


def _swiglu_fn(gt, up):
    return gt * _sigmoid(gt) * up


def _resid_norm_fn(acc, x, g):
    x_new = acc + x
    return x_new, _rms(x_new, g)


def _swiglu_bwd_fn(gt, up, dact):
    s = _sigmoid(gt)
    silu = gt * s
    return dact * up * (s + silu * (1.0 - s)), dact * silu


def _ple_loss_fn(x2, pg, pp, target):
    gate = _sigmoid(pg)
    err = x2 + gate * pp - target
    dx3 = err * (1.0 / err.shape[-1])
    loss = 0.5 * jnp.sum(jnp.mean(err * err, axis=-1, keepdims=True), axis=0, keepdims=True)
    return dx3, dx3 * pp * gate * (1.0 - gate), dx3 * gate, loss


def _attention_fwd(q, k, v):
    H, T, D = q.shape
    DV = v.shape[-1]
    tq, ck = min(ATT_TQ, T), min(ATT_CHUNK, T)
    c2 = (D ** -0.5) * LOG2_E

    def body(q_ref, k_ref, v_ref, o_ref, lse_ref):
        q_i = q_ref[0]

        def chunk(c, carry):
            m, l, acc = carry
            rows = pl.ds(pl.multiple_of(c * ck, ck), ck)
            s = _dot_raw(q_i, k_ref[0, rows, :], "nt")
            m_new = jnp.maximum(m, jnp.max(s, axis=-1, keepdims=True))
            p = jnp.exp2((s - m_new) * c2)
            alpha = jnp.exp2((m - m_new) * c2)
            l = l * alpha + jnp.sum(p, axis=-1, keepdims=True)
            acc = acc * alpha + _dot_raw(p, v_ref[0, rows, :], "nn")
            return m_new, l, acc

        init = (jnp.full((tq, 1), -jnp.inf, F32), jnp.zeros((tq, 1), F32), jnp.zeros((tq, DV), F32))
        m, l, acc = lax.fori_loop(0, T // ck, chunk, init, unroll=True)
        o_ref[...] = acc / l
        lse_ref[0] = m * c2 + jnp.log2(l)

    return pl.pallas_call(
        body, name="attention_fwd", grid=(H, T // tq),
        in_specs=[pl.BlockSpec((1, tq, D), lambda h, i: (h, i, 0)),
                  pl.BlockSpec((1, T, D), lambda h, i: (h, 0, 0)),
                  pl.BlockSpec((1, T, DV), lambda h, i: (h, 0, 0))],
        out_specs=[pl.BlockSpec((tq, DV), lambda h, i: (i, h)),
                   pl.BlockSpec((1, tq, 1), lambda h, i: (h, i, 0))],
        out_shape=[_sds((T, H * DV), F32), _sds((H, T, 1), F32)],
        compiler_params=_params(("parallel", "parallel")),
    )(q, k, v)


def _attention_bwd(q, k, v, o, lse2, dmix):
    H, T, D = q.shape
    DV = v.shape[-1]
    tk, cq = min(ATT_TK, T), min(ATT_CHUNK, T)
    scale = D ** -0.5
    c2 = scale * LOG2_E

    def body(q_ref, k_ref, v_ref, o_ref, lse_ref, do_ref, dq_ref, dk_ref, dv_ref, delta_ref):
        j = pl.program_id(1)

        @pl.when(j == 0)
        def _():
            delta = lax.dot_general(jnp.ones((8, DV), F32), do_ref[...] * o_ref[...], (((1,), (1,)), ((), ())),
                                    precision=lax.Precision.HIGHEST, preferred_element_type=F32)
            for i in range(T // cq):
                delta_ref[i] = delta[:, i * cq:(i + 1) * cq]
            dq_ref[0] = jnp.zeros((T, D), F32)

        k_j, v_j = k_ref[0], v_ref[0]
        dk_ref[0] = jnp.zeros((tk, D), F32)
        dv_ref[0] = jnp.zeros((tk, DV), F32)

        def chunk(c, carry):
            rows = pl.ds(pl.multiple_of(c * cq, cq), cq)
            q_c = q_ref[0, rows, :]
            do_c = do_ref[rows, :].astype(BF16)
            st = _dot_raw(k_j, q_c, "nt")
            pt = jnp.exp2(st * c2 - lse_ref[0, c])
            dv_ref[0] += _dot_raw(pt, do_c, "nn")
            dpt = _dot_raw(v_j, do_c, "nt")
            dst = pt * (dpt - delta_ref[c, 0:1, :]) * scale
            dk_ref[0] += _dot_raw(dst, q_c, "nn")
            dq_ref[0, rows, :] += _dot_raw(dst, k_j, "tn")
            return carry

        lax.fori_loop(0, T // cq, chunk, 0, unroll=True)

    return pl.pallas_call(
        body, name="attention_bwd", grid=(H, T // tk),
        in_specs=[pl.BlockSpec((1, T, D), lambda h, j: (h, 0, 0)),
                  pl.BlockSpec((1, tk, D), lambda h, j: (h, j, 0)),
                  pl.BlockSpec((1, tk, DV), lambda h, j: (h, j, 0)),
                  pl.BlockSpec((T, DV), lambda h, j: (0, h)),
                  pl.BlockSpec((1, T // cq, 1, cq), lambda h, j: (h, 0, 0, 0)),
                  pl.BlockSpec((T, DV), lambda h, j: (0, h))],
        out_specs=[pl.BlockSpec((1, T, D), lambda h, j: (h, 0, 0)),
                   pl.BlockSpec((1, tk, D), lambda h, j: (h, j, 0)),
                   pl.BlockSpec((1, tk, DV), lambda h, j: (h, j, 0))],
        out_shape=[_sds((H, T, D), F32), _sds((H, T, D), F32), _sds((H, T, DV), F32)],
        scratch_shapes=[pltpu.VMEM((T // cq, 8, cq), F32)],
        compiler_params=_params(("parallel", "arbitrary")),
    )(q, k, v, o, lse2.reshape(H, T // cq, 1, cq), dmix)


def _split3_dot(ones, x, kind):
    hi = x.astype(BF16)
    rest = x - hi.astype(F32)
    mid = rest.astype(BF16)
    lo = (rest - mid.astype(F32)).astype(BF16)
    return (_dot_raw(ones, hi, kind) + _dot_raw(ones, mid, kind)) + _dot_raw(ones, lo, kind)


@jax.custom_vjp
def _running_sum(x, tri):
    return _split3_dot(tri, x, "nn")


def _running_sum_fwd(x, tri):
    return _split3_dot(tri, x, "nn"), tri


def _running_sum_bwd(tri, g):
    return _split3_dot(tri, g, "tn"), jnp.zeros_like(tri)


_running_sum.defvjp(_running_sum_fwd, _running_sum_bwd)


def _gla_block(hq, hf, hi, lower, st_in, *, rev, dot):
    rows, dk = hq.shape
    G, C = rows // CHUNK, CHUNK
    q = hq * _sigmoid(hq)
    f = lower + (1.0 - lower) * _sigmoid(hf)
    k = 1.0 - f
    logf = jnp.log2(f)
    q3, k3, v3, lf3 = (t.reshape(G, C, dk) for t in (q, k, hi, logf))
    r = lax.broadcasted_iota(jnp.int32, (C, C), 0)
    c = lax.broadcasted_iota(jnp.int32, (C, C), 1)
    tri = ((r <= c) if rev else (r >= c)).astype(F32)
    b = _running_sum(lf3, jnp.broadcast_to(tri, (G, C, C)))
    tpos = lax.broadcasted_iota(jnp.int32, (1, C, 1), 1)
    first_half = (tpos >= C // 2) if rev else (tpos <= C // 2 - 1)
    b_mid = jnp.sum(jnp.where(first_half, lf3, 0.0), axis=1, keepdims=True)
    b_last = jnp.sum(lf3, axis=1, keepdims=True)
    a = dot(q3 * jnp.exp2(b - b_mid), k3 * jnp.exp2(b_mid - b), "nt") * tri
    o_intra = dot(a, v3, "nn")
    kv_t = dot(v3, k3 * jnp.exp2(b_last - b), "tn")
    decay = jnp.exp2(b_last)
    qd = q3 * jnp.exp2(b)
    st = st_in
    o_inter = [None] * G
    for g in (reversed(range(G)) if rev else range(G)):
        o_inter[g] = dot(qd[g], st, "nt")
        st = st * decay[g] + kv_t[g]
    o = o_intra.reshape(rows, dk) + jnp.concatenate(o_inter, axis=0)
    return o, st


def _gla_fwd(z, lower3, *, rev, col_q, col_f, col_v, hp):
    T = z.shape[0]
    rows = min(GLA_GROUP * CHUNK, T)
    nb = T // rows
    wide = hp * HG_DK
    blk = (lambda n: nb - 1 - n) if rev else (lambda n: n)

    def body(hq_ref, hf_ref, hi_ref, low_ref, o_ref, st_out_ref, st_ref):
        @pl.when(pl.program_id(1) == 0)
        def _():
            st_ref[...] = jnp.zeros_like(st_ref)

        st_in = [st_ref[i] for i in range(hp)]
        heads = []
        for i in range(hp):
            cols = slice(i * HG_DK, (i + 1) * HG_DK)
            heads.append(_gla_block(hq_ref[:, cols], hf_ref[:, cols], hi_ref[:, cols], low_ref[i], st_in[i], rev=rev,
                                    dot=_dot_raw))
        for i, (o, st) in enumerate(heads):
            st_out_ref[i, 0] = st_in[i]
            o_ref[:, i * HG_DK:(i + 1) * HG_DK] = o
            st_ref[i] = st

    def zspec(col):
        return pl.BlockSpec((rows, wide), lambda h, n: (blk(n), col // wide + h))

    return pl.pallas_call(
        body, name="gla_fwd_rev" if rev else "gla_fwd", grid=(HG_HEADS // hp, nb),
        in_specs=[zspec(col_q), zspec(col_f), zspec(col_v), pl.BlockSpec((hp, 1, HG_DK), lambda h, n: (h, 0, 0))],
        out_specs=[pl.BlockSpec((rows, wide), lambda h, n: (blk(n), h)),
                   pl.BlockSpec((hp, 1, HG_DK, HG_DK), lambda h, n: (h, blk(n), 0, 0))],
        out_shape=[_sds((T, HG_HEADS * HG_DK), F32), _sds((HG_HEADS, nb, HG_DK, HG_DK), F32)],
        scratch_shapes=[pltpu.VMEM((hp, HG_DK, HG_DK), F32)],
        compiler_params=_params(("parallel", "arbitrary")),
    )(z, z, z, lower3)


def _gla_bwd(z, lower3, states, do, prev, *, rev, col_q, col_f, col_v, hp):
    T = z.shape[0]
    rows = min(GLA_GROUP * CHUNK, T)
    nb = T // rows
    wide = hp * HG_DK
    blk = (lambda n: n) if rev else (lambda n: nb - 1 - n)
    has_prev = prev is not None
    fn = functools.partial(_gla_block, rev=rev, dot=_bdot)

    def body(*refs):
        hq_ref, hf_ref, hi_ref, low_ref, st_ref, do_ref = refs[:6]
        rest = refs[6:]
        if has_prev:
            pq_ref, pi_ref = rest[:2]
            rest = rest[2:]
        dhq_ref, dhi_ref, dhf_ref, dlow_ref, dst_ref = rest
        n = pl.program_id(1)

        @pl.when(n == 0)
        def _():
            dst_ref[...] = jnp.zeros_like(dst_ref)

        dst_in = [dst_ref[i] for i in range(hp)]
        heads = []
        for i in range(hp):
            cols = slice(i * HG_DK, (i + 1) * HG_DK)
            _, vjp = jax.vjp(fn, hq_ref[:, cols], hf_ref[:, cols], hi_ref[:, cols], low_ref[i], st_ref[i, 0])
            dhq, dhf, dhi, dlow, dst = vjp((do_ref[:, cols], dst_in[i]))
            if has_prev:
                dhq = dhq + pq_ref[:, cols]
                dhi = dhi + pi_ref[:, cols]
            heads.append((dhq, dhf, dhi, dlow, dst))
        for i, (dhq, dhf, dhi, dlow, dst) in enumerate(heads):
            cols = slice(i * HG_DK, (i + 1) * HG_DK)
            dst_ref[i] = dst
            dhq_ref[:, cols] = dhq.astype(dhq_ref.dtype)
            dhi_ref[:, cols] = dhi.astype(dhi_ref.dtype)
            dhf_ref[:, cols] = dhf.astype(dhf_ref.dtype)

        @pl.when(n == 0)
        def _():
            for i in range(hp):
                dlow_ref[i] = heads[i][3]

        @pl.when(n > 0)
        def _():
            for i in range(hp):
                dlow_ref[i] += heads[i][3]

    def zspec(col):
        return pl.BlockSpec((rows, wide), lambda h, n: (blk(n), col // wide + h))

    hspec = pl.BlockSpec((rows, wide), lambda h, n: (blk(n), h))
    in_specs = [zspec(col_q), zspec(col_f), zspec(col_v), pl.BlockSpec((hp, 1, HG_DK), lambda h, n: (h, 0, 0)),
                pl.BlockSpec((hp, 1, HG_DK, HG_DK), lambda h, n: (h, blk(n), 0, 0)), hspec]
    ins = [z, z, z, lower3, states, do]
    if has_prev:
        in_specs += [hspec, hspec]
        ins += list(prev)
    full_wide = HG_HEADS * HG_DK
    acc_dtype = BF16 if has_prev else F32
    return pl.pallas_call(
        body, name="gla_bwd_rev" if rev else "gla_bwd", grid=(HG_HEADS // hp, nb),
        in_specs=in_specs,
        out_specs=[hspec, hspec, hspec, pl.BlockSpec((hp, 1, HG_DK), lambda h, n: (h, 0, 0))],
        out_shape=[_sds((T, full_wide), acc_dtype), _sds((T, full_wide), acc_dtype), _sds((T, full_wide), BF16),
                   _sds((HG_HEADS, 1, HG_DK), F32)],
        scratch_shapes=[pltpu.VMEM((hp, HG_DK, HG_DK), F32)],
        compiler_params=_params(("parallel", "arbitrary")),
    )(*ins)


def _lower_fn(lb):
    e = jnp.exp(lb - jnp.max(lb, axis=0, keepdims=True))
    return (e / jnp.sum(e, axis=0, keepdims=True))[0]


def _lower_bounds(lb):
    def body(lb_ref, o_ref):
        o_ref[...] = _lower_fn(lb_ref[...])
    return pl.pallas_call(body, name="lower_bounds", out_shape=_sds(lb.shape[1:], F32))(lb)


def _row_tile(r, cap=1024):
    best = None
    for t in range(16, min(r, cap) + 1, 16):
        if r % t == 0:
            best = t
    return best if best is not None else r


def _sum4(shards, recv, chip, *, name):
    _, R, C = shards.shape
    tr = _row_tile(R)

    def body(chip_ref, o_ref, r_ref, out_ref):
        out_ref[...] = ((o_ref[0].astype(F32) + r_ref[0].astype(F32)) + r_ref[1].astype(F32)) + r_ref[2].astype(F32)

    grid_spec = pltpu.PrefetchScalarGridSpec(
        num_scalar_prefetch=1, grid=(R // tr,),
        in_specs=[pl.BlockSpec((1, tr, C), lambda i, chip_ref: (chip_ref[0], i, 0)),
                  pl.BlockSpec((3, tr, C), lambda i, chip_ref: (0, i, 0))],
        out_specs=pl.BlockSpec((tr, C), lambda i, chip_ref: (i, 0)))
    return pl.pallas_call(
        body, name=name, grid_spec=grid_spec, out_shape=_sds((R, C), F32), compiler_params=_params(("parallel",)),
    )(chip, shards, recv)


def _adamw_math(w, g, m, v):
    m = ADAM_B1 * m + (1.0 - ADAM_B1) * g
    v = ADAM_B2 * v + (1.0 - ADAM_B2) * (g * g)
    m_hat = m / (1.0 - ADAM_B1 ** ADAM_STEP)
    v_hat = v / (1.0 - ADAM_B2 ** ADAM_STEP)
    delta = -ADAM_LR * (m_hat / (jnp.sqrt(v_hat) + ADAM_EPS) + ADAM_WD * w)
    return delta, m, v


def _adamw(w, g_a, g_b, m, v, *, name):
    R, C = w.shape
    tr = _row_tile(R)
    two = g_b is not None

    def body(*refs):
        w_ref, ga_ref = refs[0], refs[1]
        rest = refs[2:]
        g = ga_ref[...]
        if two:
            g = g + rest[0][...]
            rest = rest[1:]
        m_ref, v_ref, g_out, d_out, m_out, v_out, token = rest
        delta, m_new, v_new = _adamw_math(w_ref[...], g, m_ref[...], v_ref[...])
        g_out[...] = g
        d_out[...] = delta
        m_out[...] = m_new
        v_out[...] = v_new
        token[...] = jnp.zeros_like(token)

    spec = pl.BlockSpec((tr, C), lambda i: (i, 0))
    ins = [w, g_a] + ([g_b] if two else []) + [m, v]
    return pl.pallas_call(
        body, name=name, grid=(R // tr,), in_specs=[spec] * len(ins),
        out_specs=[spec] * 4 + [pl.BlockSpec((8, LANE), lambda i: (0, 0))],
        out_shape=[_sds((R, C), F32)] * 4 + [_sds((8, LANE), F32)], compiler_params=_params(("arbitrary",)),
    )(*ins)


def _adamw_small(red, lb_full, ws, ms, vs):
    n = len(ws)

    def pieces(shape):
        out = []
        for j, idx in enumerate(itertools.product(*[range(d) for d in shape[:-1]])):
            out.append((idx[:-1] + (slice(idx[-1], idx[-1] + 1), slice(None)), j * shape[-1]))
        return out

    def body(*refs):
        red_ref, lb_ref = refs[0], refs[1]
        w_refs, m_refs, v_refs = refs[2:2 + n], refs[2 + n:2 + 2 * n], refs[2 + 2 * n:2 + 3 * n]
        out_refs = refs[2 + 3 * n:]
        chip = 2 * lax.axis_index("x") + lax.axis_index("y")
        n_f, shard = lb_ref.shape[-1], w_refs[n - 1].shape[-1]
        _, vjp = jax.vjp(_lower_fn, lb_ref[...])
        dlb = vjp(red_ref[8:10, 0:n_f])[0]
        for i in range(n):
            width = w_refs[i].shape[-1]
            for j, (at, lane) in enumerate(pieces(w_refs[i].shape)):
                if i < n - 1:
                    g = red_ref[i:i + 1, lane:lane + width]
                else:
                    row = dlb[j // 2][j % 2:j % 2 + 1]
                    g = sum(jnp.where(chip == q, row[:, q * shard:(q + 1) * shard], 0.0) for q in range(N_CHIPS))
                delta, m_new, v_new = _adamw_math(w_refs[i][at], g, m_refs[i][at], v_refs[i][at])
                for o_ref, val in zip(out_refs[4 * i:4 * i + 4], (g, delta, m_new, v_new)):
                    o_ref[at] = val

    return pl.pallas_call(
        body, name="adamw_small", out_shape=[_sds(w.shape, F32) for w in ws for _ in range(4)],
    )(red, lb_full, *ws, *ms, *vs)


def _chip_peers():
    x, y, c = lax.axis_index("x"), lax.axis_index("y"), lax.axis_index("c")
    return (x, y, c), 2 * x + y, [(1 - x, y), (x, 1 - y), (1 - x, 1 - y)]


_HBM = pl.BlockSpec(memory_space=pltpu.HBM)
_SEM = pl.BlockSpec(memory_space=pltpu.SEMAPHORE)
_EFFECT = pltpu.SideEffectType.DATAFLOW_SIDE_EFFECTING


def _exchange_copies(srcs, lands, sems, mode):
    (x, y, c), me, chips = _chip_peers()
    copies = []
    for t, (src, land) in enumerate(zip(srcs, lands)):
        if mode == "swap":
            copies.append(pltpu.make_async_remote_copy(src, land, sems[0].at[3 * t], sems[1].at[3 * t],
                                                       device_id=(x, y, 1 - c), device_id_type=MESH))
            continue
        for k, (px, py) in enumerate(chips):
            gather = mode == "gather"
            copies.append(pltpu.make_async_remote_copy(
                src if gather else src.at[2 * px + py], land.at[me] if gather else land.at[k],
                sems[0].at[3 * t + k], sems[1].at[3 * t + k], device_id=(px, py, c), device_id_type=MESH))
        if mode == "gather":
            copies.append(pltpu.make_async_copy(src, land.at[me], sems[2].at[t]))
    return copies


def _exchange_start(srcs, *, mode, name, after=None):
    n = len(srcs)
    n_sem = 3 if mode == "gather" else 2
    n_in = 2 * n + (after is not None)
    land_shape = {"gather": lambda s: (N_CHIPS,) + s.shape, "scatter": lambda s: (3,) + s.shape[1:], "swap": lambda s: s.shape}
    lands = [_sds(land_shape[mode](s), s.dtype) for s in srcs]

    def body(*refs):
        for cp in _exchange_copies(refs[:n], refs[n:2 * n], refs[n_in:n_in + n_sem], mode):
            cp.start()
        token = refs[-1]
        token[...] = jnp.zeros_like(token)

    sem_shapes = [pltpu.SemaphoreType.DMA((3 * n,)), pltpu.SemaphoreType.DMA((3 * n,))]
    sem_shapes += [pltpu.SemaphoreType.DMA((n,))] if mode == "gather" else []
    thru = [pltpu.HBM(s.shape, s.dtype) for s in srcs] + [pltpu.HBM(l.shape, l.dtype) for l in lands]
    res = pl.pallas_call(
        body, name=name, in_specs=[_HBM] * (2 * n) + [_ANY] * (after is not None),
        out_specs=[_SEM] * n_sem + [_HBM] * (2 * n) + [pl.BlockSpec(memory_space=pltpu.VMEM)],
        out_shape=sem_shapes + thru + [_sds((8, LANE), F32)], input_output_aliases={i: n_sem + i for i in range(2 * n)},
        compiler_params=pltpu.CompilerParams(has_side_effects=_EFFECT),
    )(*[pltpu.with_memory_space_constraint(s, pltpu.HBM) for s in srcs],
      *[pltpu.with_memory_space_constraint(lax.empty(l.shape, l.dtype), pltpu.HBM) for l in lands],
      *([after] if after is not None else []))
    return (res[:n_sem], res[n_sem:n_sem + n], res[n_sem + n:n_sem + 2 * n], mode), res[-1]


def _exchange_wait(started, after, *, name):
    sems, srcs, lands, mode = started
    n, n_sem = len(srcs), len(sems)
    after = list(after) if isinstance(after, (list, tuple)) else [after]

    def body(*refs):
        for cp in _exchange_copies(refs[:n], refs[n:2 * n], refs[2 * n:2 * n + n_sem], mode):
            cp.wait()

    res = pl.pallas_call(
        body, name=name, in_specs=[_HBM] * (2 * n) + [_SEM] * n_sem + [_ANY] * len(after), out_specs=[_HBM] * (2 * n),
        out_shape=[pltpu.HBM(a.shape, a.dtype) for a in list(srcs) + list(lands)],
        input_output_aliases={i: i for i in range(2 * n)},
        compiler_params=pltpu.CompilerParams(has_side_effects=_EFFECT),
    )(*srcs, *lands, *sems, *after)
    return res[:n], res[n:]


def _allreduce_small(pack, after):
    R, C = pack.shape

    def body(in_ref, after_ref, out_ref, slots, send_sems, recv_sems):
        x, y, c = lax.axis_index("x"), lax.axis_index("y"), lax.axis_index("c")
        me = 4 * x + 2 * y + c
        slots[me] = in_ref[...]
        copies = []
        for k in range(1, N_DEV):
            peer = (x ^ ((k >> 2) & 1), y ^ ((k >> 1) & 1), c ^ (k & 1))
            cp = pltpu.make_async_remote_copy(in_ref, slots.at[me], send_sems.at[k - 1], recv_sems.at[k - 1],
                                              device_id=peer, device_id_type=MESH)
            cp.start()
            copies.append(cp)
        for cp in copies:
            cp.wait()
        acc = slots[0]
        for d in range(1, N_DEV):
            acc = acc + slots[d]
        out_ref[...] = acc

    return pl.pallas_call(
        body, name="allreduce_small", out_shape=_sds((R, C), F32),
        in_specs=[pl.BlockSpec(memory_space=pltpu.VMEM), _ANY], out_specs=pl.BlockSpec(memory_space=pltpu.VMEM),
        scratch_shapes=[pltpu.VMEM((N_DEV, R, C), F32), pltpu.SemaphoreType.DMA((N_DEV - 1,)),
                        pltpu.SemaphoreType.DMA((N_DEV - 1,))],
        compiler_params=_params(),
    )(pack, after)


_Z_CQ, _Z_CKV, _Z_HQ, _Z_HFF, _Z_HFB, _Z_HI, _Z_HG, _Z_KR, _Z_END = 0, 256, 512, 1024, 1536, 2048, 2560, 3072, 3200


def _to_z_layout(wt):
    pad = jnp.zeros((_Z_END - _Z_KR - QK_ROPE, wt.shape[1]), wt.dtype)
    return jnp.concatenate([wt[:512], wt[512 + QK_ROPE:], wt[512:512 + QK_ROPE], pad], axis=0)


def _from_z_layout(wt):
    return jnp.concatenate([wt[:512], wt[_Z_KR:_Z_KR + QK_ROPE], wt[512:_Z_KR]], axis=0)


def _col_shards_to_full(g):
    return jnp.transpose(g, (1, 0, 2)).reshape(g.shape[1], -1)


def _full_to_col_shards(w):
    r, c = w.shape
    return jnp.transpose(w.reshape(r, N_CHIPS, c // N_CHIPS), (1, 0, 2))


def _full_to_row_shards(w):
    r, c = w.shape
    return w.reshape(N_CHIPS, r // N_CHIPS, c)


def kernel(x, p, positions, g_mix, w_in, g_qa, g_kva, w_qb, w_kvb, g_qn, g_kn, lb_param, g_hgo, w_o, g_ffn, w_gate, w_up, w_down, g_ple, w_ple_gate, w_ple_proj, loss_target, m_g_mix, m_w_in, m_g_qa, m_g_kva, m_w_qb, m_w_kvb, m_g_qn, m_g_kn, m_lb_param, m_g_hgo, m_w_o, m_g_ffn, m_w_gate, m_w_up, m_w_down, m_g_ple, m_w_ple_gate, m_w_ple_proj, v_g_mix, v_w_in, v_g_qa, v_g_kva, v_w_qb, v_w_kvb, v_g_qn, v_g_kn, v_lb_param, v_g_hgo, v_w_o, v_g_ffn, v_w_gate, v_w_up, v_w_down, v_g_ple, v_w_ple_gate, v_w_ple_proj):
    w_named = dict(g_mix=g_mix, w_in=w_in, g_qa=g_qa, g_kva=g_kva, w_qb=w_qb, w_kvb=w_kvb, g_qn=g_qn, g_kn=g_kn,
                   lb_param=lb_param, g_hgo=g_hgo, w_o=w_o, g_ffn=g_ffn, w_gate=w_gate, w_up=w_up, w_down=w_down,
                   g_ple=g_ple, w_ple_gate=w_ple_gate, w_ple_proj=w_ple_proj)
    m_named = dict(g_mix=m_g_mix, w_in=m_w_in, g_qa=m_g_qa, g_kva=m_g_kva, w_qb=m_w_qb, w_kvb=m_w_kvb, g_qn=m_g_qn,
                   g_kn=m_g_kn, lb_param=m_lb_param, g_hgo=m_g_hgo, w_o=m_w_o, g_ffn=m_g_ffn, w_gate=m_w_gate,
                   w_up=m_w_up, w_down=m_w_down, g_ple=m_g_ple, w_ple_gate=m_w_ple_gate, w_ple_proj=m_w_ple_proj)
    v_named = dict(g_mix=v_g_mix, w_in=v_w_in, g_qa=v_g_qa, g_kva=v_g_kva, w_qb=v_w_qb, w_kvb=v_w_kvb, g_qn=v_g_qn,
                   g_kn=v_g_kn, lb_param=v_lb_param, g_hgo=v_g_hgo, w_o=v_w_o, g_ffn=v_g_ffn, w_gate=v_w_gate,
                   w_up=v_w_up, w_down=v_w_down, g_ple=v_g_ple, w_ple_gate=v_w_ple_gate, w_ple_proj=v_w_ple_proj)
    order = list(w_named)
    transposed = ("w_in", "w_qb", "w_gate", "w_up")
    col_sharded = ("w_kvb", "w_ple_proj")
    row_sharded = ("w_o", "w_down", "w_ple_gate")
    big = transposed + col_sharded + row_sharded

    def view(n, a):
        return jnp.transpose(a[0]) if n in transposed else a[0]

    def unview(n, a):
        return (jnp.transpose(a) if n in transposed else a)[None]

    def to_shards(n, g):
        return _full_to_col_shards(g) if n in col_sharded else _full_to_row_shards(g)

    x2d, p2d, tgt = x[0], p[0, 0], loss_target[0]
    T, D = x2d.shape

    lb_flat = lb_param.reshape(-1, lb_param.shape[-1])
    gather_groups = (("w_in",), ("w_qb", "w_kvb"), ("w_o", "w_gate", "w_up", "w_down", "w_ple_gate", "w_ple_proj"))
    gather_started = []

    casts = {n: view(n, w_named[n]).astype(BF16) for n in big}

    def gather_start(gi, after):
        srcs = [casts[n] for n in gather_groups[gi]] + ([lb_flat] if gi == 0 else [])
        started, token = _exchange_start(srcs, mode="gather", name=f"gather_start_{gi}", after=after)
        gather_started.append(started)
        return token

    full = {}

    def gather_wait(gi, after):
        _, got = _exchange_wait(gather_started[gi], after, name=f"gather_wait_{gi}")
        for n, g in zip(gather_groups[gi], got):
            full[n] = _col_shards_to_full(g) if n in col_sharded else g.reshape(-1, g.shape[-1])
        return got

    g_hgo_row = g_hgo.reshape(1, -1)

    inv_freq = ROPE_THETA ** (-jnp.arange(0, QK_ROPE, 2, dtype=F32) / QK_ROPE)
    ang = positions[0].astype(F32)[:, None] * inv_freq
    cos, sin = jnp.cos(ang), jnp.sin(ang)
    token = gather_start(0, None)
    h1 = _stage(_rms, [x2d], [g_mix], [_sds((T, D), BF16)], [], name="norm_mix", after=token)[0]
    got = gather_wait(0, [h1, cos, sin] + [casts[n] for g in gather_groups[1:] for n in g])
    token = got[0]
    for gi in range(1, len(gather_groups)):
        token = gather_start(gi, token)
    lb_full = _col_shards_to_full(got[-1]).reshape(lb_param.shape[0], lb_param.shape[1], -1)
    w_in_zt = _to_z_layout(full["w_in"])
    z = _mm(h1, w_in_zt, tb=True, name="in_proj", after=token)
    qn, kvn = _stage(_mla_a_fn, [_cols(z, 256, 0), _cols(z, 256, 1)], [g_qa, g_kva],
                     [_sds((T, 256), BF16), _sds((T, 256), BF16)], [], name="mla_latent_norm")
    gather_wait(1, qn)
    q_raw = _mm(qn, full["w_qb"], tb=True, name="q_up")
    kv_raw = _mm(kvn, full["w_kvb"], name="kv_up")
    kr = _cols(z, LANE, _Z_KR // LANE)
    q, k, v = _stage(_mla_b_fn, [q_raw, kv_raw, kr, cos, sin], [g_qn, g_kn],
                     [_sds((MLA_HEADS, T, QK_HEAD), BF16), _sds((MLA_HEADS, T, QK_HEAD), BF16),
                      _sds((MLA_HEADS, T, V_HEAD), BF16)], [], name="mla_qk_norm_rope")
    att, lse = _attention_fwd(q, k, v)

    lower = _lower_bounds(lb_full)
    lower3 = lower.reshape(2, HG_HEADS, 1, HG_DK)
    o_f, st_f = _gla_fwd(z, lower3[0], rev=False, col_q=_Z_HQ, col_f=_Z_HFF, col_v=_Z_HI, hp=GLA_FWD_HEADS)
    o_b, st_b = _gla_fwd(z, lower3[1], rev=True, col_q=_Z_HQ, col_f=_Z_HFB, col_v=_Z_HI, hp=GLA_FWD_HEADS)
    hg = _cols(z, 512, _Z_HG // 512)
    mix = _stage(_post_fn, [att, o_f, o_b, hg], [g_hgo_row], [_sds((T, att.shape[1] + o_f.shape[1]), BF16)], [],
                 name="mix_out")[0]
    gather_wait(2, mix)
    x1, h2 = _mm_fused(mix, full["w_o"], _resid_norm_fn, [x2d], [g_ffn], [F32, BF16], [], full_rows=True,
                       name="out_proj")
    gt, up, act = _mm_fused(h2, full["w_gate"], lambda gt, up: (gt, up, _swiglu_fn(gt, up)), [], [], [BF16, BF16, BF16],
                            [], tb=True, b2=full["w_up"], name="ffn_gate_up")
    x2, h3 = _mm_fused(act, full["w_down"], _resid_norm_fn, [x1], [g_ple], [F32, BF16], [], full_rows=True,
                       name="ffn_down")
    pp = _mm(p2d, full["w_ple_proj"], name="ple_proj")
    dx3, dpg, dpp, loss_part = _mm_fused(
        h3, full["w_ple_gate"], lambda acc, pp, x2, tgt: _ple_loss_fn(x2, acc, pp, tgt), [pp, x2, tgt], [],
        [F32, BF16, BF16], [_sds((1, 1), F32)], full_rows=True, name="ple_gate_loss")

    grads = {}
    scatter_groups = (("w_ple_proj", "w_ple_gate", "w_down", "w_gate", "w_up", "w_o"), ("w_qb", "w_kvb", "w_in"))
    scatter_started = []

    def scatter_start(gi):
        srcs = [to_shards(n, grads[n]) for n in scatter_groups[gi]]
        started, token = _exchange_start(srcs, mode="scatter", name=f"scatter_start_{gi}")
        scatter_started.append(started)
        return token

    chip = 2 * lax.axis_index("x") + lax.axis_index("y")
    swap_started = []

    def reduce_group(gi, after):
        shards, recvs = _exchange_wait(scatter_started[gi], after, name=f"scatter_wait_{gi}")
        sums = [_sum4(s, r, chip.reshape(1), name="sum_" + n) for n, s, r in zip(scatter_groups[gi], shards, recvs)]
        started, token = _exchange_start(sums, mode="swap", name=f"swap_start_{gi}")
        swap_started.append(started)
        return token

    grads["w_ple_proj"] = _mm(p2d, dpp, ta=True, out_dtype=BF16, name="d_w_ple_proj")
    grads["w_ple_gate"] = _mm(h3, dpg, ta=True, out_dtype=BF16, name="d_w_ple_gate")
    dx2, grads["g_ple"] = _mm_fused(
        dpg, full["w_ple_gate"], lambda acc, x2, dx3, g: _norm_bwd_fn(x2, acc, dx3, g), [x2, dx3], [g_ple],
        [F32], [_sds((1, D), F32)], tb=True, full_rows=True, name="d_h3_norm_ple_bwd")
    dgt, dup = _mm_fused(dx2, full["w_down"], lambda acc, gt, up: _swiglu_bwd_fn(gt.astype(F32), up.astype(F32), acc), [gt, up], [],
                         [BF16, BF16], [], tb=True, name="d_act_swiglu_bwd")
    grads["w_down"] = _mm(act, dx2, ta=True, out_dtype=BF16, name="d_w_down")
    grads["w_gate"] = _mm(dgt, h2, ta=True, out_dtype=BF16, name="d_w_gate")
    grads["w_up"] = _mm(dup, h2, ta=True, out_dtype=BF16, name="d_w_up")
    dx1, grads["g_ffn"] = _mm_fused(
        dgt, full["w_gate"], lambda acc, x1, dx2, g: _norm_bwd_fn(x1, acc, dx2, g), [x1, dx2], [g_ffn],
        [F32], [_sds((1, D), F32)], full_rows=True, a2=dup, b2=full["w_up"], name="d_h2_norm_ffn_bwd")
    grads["w_o"] = _mm(mix, dx1, ta=True, out_dtype=BF16, name="d_w_o")
    token = scatter_start(0)
    dmix = _mm(dx1, full["w_o"], tb=True, name="d_mix", after=token)

    half = MLA_HEADS * V_HEAD
    do, dhg, dg_hgo = _stage(_post_bwd_fn, [o_f, o_b, hg, _cols(dmix, half, 1)], [g_hgo_row],
                             [_sds((T, half), F32), _sds((T, half), BF16)], [_sds((1, half), F32)], name="mix_out_bwd")
    grads["g_hgo"] = dg_hgo
    dhq_f, dhi_f, dhf_f, dlow_f = _gla_bwd(z, lower3[0], st_f, do, None, rev=False,
                                           col_q=_Z_HQ, col_f=_Z_HFF, col_v=_Z_HI, hp=GLA_BWD_HEADS)
    dhq, dhi, dhf_b, dlow_b = _gla_bwd(z, lower3[1], st_b, do, (dhq_f, dhi_f), rev=True,
                                       col_q=_Z_HQ, col_f=_Z_HFB, col_v=_Z_HI, hp=GLA_BWD_HEADS)

    dq, dk, dv = _attention_bwd(q, k, v, att, lse, dmix)
    dq_raw, dkv_raw, dkr, grads["g_qn"], grads["g_kn"] = _stage(
        _mla_b_bwd_fn, [q_raw, kv_raw, kr, cos, sin, dq, dk, dv], [g_qn, g_kn],
        [_sds(q_raw.shape, BF16), _sds(kv_raw.shape, BF16), _sds((T, LANE), BF16)],
        [_sds(g_qn.shape, F32), _sds(g_kn.shape, F32)], name="mla_qk_norm_rope_bwd")
    grads["w_qb"] = _mm(dq_raw, qn, ta=True, out_dtype=BF16, name="d_w_qb")
    grads["w_kvb"] = _mm(kvn, dkv_raw, ta=True, out_dtype=BF16, name="d_w_kvb")
    dqn = _mm(dq_raw, full["w_qb"], name="d_qn")
    dkvn = _mm(dkv_raw, full["w_kvb"], tb=True, name="d_kvn")
    dcq, dckv, grads["g_qa"], grads["g_kva"] = _stage(
        _mla_a_bwd_fn, [_cols(z, 256, 0), _cols(z, 256, 1), dqn, dkvn], [g_qa, g_kva],
        [_sds((T, 256), BF16), _sds((T, 256), BF16)], [_sds(g_qa.shape, F32), _sds(g_kva.shape, F32)],
        name="mla_latent_norm_bwd")
    token = reduce_group(0, dcq)
    dz = jnp.concatenate([dcq, dckv, dhq, dhf_f, dhf_b, dhi, dhg, dkr], axis=1)
    grads["w_in"] = _from_z_layout(_mm(dz, h1, ta=True, out_dtype=BF16, name="d_w_in", after=token))
    token = scatter_start(1)
    grad_x, grads["g_mix"] = _mm_fused(
        dz, w_in_zt, lambda acc, x, dx1, g: _norm_bwd_fn(x, acc, dx1, g), [x2d, dx1], [g_mix],
        [F32], [_sds((1, D), F32)], full_rows=True, name="d_h1_norm_mix_bwd", after=token)

    out_g, out_d, out_m, out_v = {}, {}, {}, {}

    def update_group(gi, after):
        mine, theirs = _exchange_wait(swap_started[gi], after, name=f"swap_wait_{gi}")
        tokens = []
        for n, a, b in zip(scatter_groups[gi], mine, theirs):
            *res, token = _adamw(view(n, w_named[n]), a, b, view(n, m_named[n]), view(n, v_named[n]), name="adamw_" + n)
            out_g[n], out_d[n], out_m[n], out_v[n] = (unview(n, t) for t in res)
            tokens.append(token)
        return tokens

    done = update_group(0, grads["g_mix"])
    token = reduce_group(1, done)
    done = update_group(1, token)

    small = ("g_mix", "g_qa", "g_kva", "g_qn", "g_kn", "g_hgo", "g_ffn", "g_ple")
    small_all = small + ("lb_param",)
    width = -(-max(w_named[n].size for n in small_all) // LANE) * LANE

    def row(a):
        a = a.reshape(1, -1)
        return jnp.pad(a, ((0, 0), (0, width - a.shape[1])))

    dlower = jnp.concatenate([dlow_f.reshape(1, -1), dlow_b.reshape(1, -1)], axis=0)
    pack = jnp.concatenate([row(grads[n]) for n in small] + [row(dlower[0]), row(dlower[1]), row(loss_part)]
                           + [jnp.zeros((5, width), F32)], axis=0)
    red = _allreduce_small(pack, done[-1])
    loss = red[10, 0]

    outs = _adamw_small(red, lb_full, [w_named[n] for n in small_all], [m_named[n] for n in small_all],
                        [v_named[n] for n in small_all])
    for i, n in enumerate(small_all):
        out_g[n], out_d[n], out_m[n], out_v[n] = outs[4 * i:4 * i + 4]

    return (loss, grad_x[None], *[out_g[n] for n in order], *[out_d[n] for n in order],
            *[out_m[n] for n in order], *[out_v[n] for n in order])
```

```python
import functools
import itertools

import jax
import jax.numpy as jnp
from jax import lax
from jax.experimental import pallas as pl
from jax.experimental.pallas import tpu as pltpu

F32 = jnp.float32
BF16 = jnp.bfloat16
MESH = pl.DeviceIdType.MESH

EPS = 1e-6
ROPE_THETA = 10000.0
MLA_HEADS = 4
QK_NOPE = 128
QK_ROPE = 64
QK_HEAD = QK_NOPE + QK_ROPE
V_HEAD = 128
HG_HEADS = 4
HG_DK = 128
CHUNK = 64
ADAM_LR = 0.001
ADAM_B1 = 0.9
ADAM_B2 = 0.999
ADAM_EPS = 1e-08
ADAM_WD = 0.01
ADAM_STEP = 10

LANE = 128
VMEM_LIMIT = 56 * 1024 * 1024
TOK_TILE = 256
GLA_GROUP = 16
GLA_FWD_HEADS = 4
GLA_BWD_HEADS = 2
ATT_TQ = 1024
ATT_TK = 2048
ATT_CHUNK = 1024
LOG2_E = 1.4426950408889634
N_CHIPS = 4
N_DEV = 8


_ANY = pl.BlockSpec(memory_space=pl.ANY)


def _params(dims=None, **kw):
    return pltpu.CompilerParams(dimension_semantics=dims, vmem_limit_bytes=VMEM_LIMIT, **kw)


def _tile_candidates(n, cap):
    out = [d for d in range(LANE, min(n, cap) + 1, LANE) if n % d == 0]
    if n <= cap and n not in out:
        out.append(n)
    return out or [n]


MM_VMEM_BUDGET = 40 * 1024 * 1024
MM_MIN_ROWS = 256
MM_MAX_ROWS = 1536
HBM_BYTES_PER_S = 2.8e12
MXU_FLOPS_PER_S = 8e14
STEP_OVERHEAD_S = 0.35e-6


def _mm_tiles(M, N, K, a_bytes, b_bytes, o_bytes, has_add, full_rows=False, full_k=False, n_a=1, n_b=1):
    cast_a, cast_b = a_bytes > 2, b_bytes > 2
    a_bytes, b_bytes = n_a * a_bytes, n_b * b_bytes
    best = None
    for tm in [t for t in _tile_candidates(M, MM_MAX_ROWS) if t >= min(M, MM_MIN_ROWS)]:
        for tn in ([N] if full_rows else _tile_candidates(N, N)):
            for tk in ([K] if full_k else _tile_candidates(K, K)):
                ni, nj, nk = M // tm, N // tn, K // tk
                vmem = 2 * (tm * tk * a_bytes + tk * tn * b_bytes + tm * tn * o_bytes * (2 if has_add else 1))
                vmem += tm * tn * 4 * (2 if nk > 1 else 1)
                vmem += (tm * tk * 2 * n_a if cast_a else 0) + (tk * tn * 2 * n_b if cast_b else 0)
                if vmem > MM_VMEM_BUDGET:
                    continue
                moved = M * K * a_bytes * (nj if nk > 1 else 1) + K * N * b_bytes * (1 if nj == nk == 1 else ni)
                moved += M * N * o_bytes * (2 if has_add else 1)
                t = max(moved / HBM_BYTES_PER_S, 2 * M * N * K / MXU_FLOPS_PER_S) + ni * nj * nk * STEP_OVERHEAD_S
                if best is None or t < best[0]:
                    best = (t, tm, tn, tk)
    assert best is not None, (M, N, K)
    return best[1:]


def _dot_raw(a, b, kind):
    nb = a.ndim - 2
    batch = ((0,), (0,)) if nb else ((), ())
    ca = nb if kind == "tn" else nb + 1
    cb = nb + 1 if kind == "nt" else nb
    return lax.dot_general(a.astype(BF16), b.astype(BF16), (((ca,), (cb,)), batch), preferred_element_type=F32)


@functools.partial(jax.custom_vjp, nondiff_argnums=(2,))
def _bdot(a, b, kind):
    return _dot_raw(a, b, kind)


def _bdot_fwd(a, b, kind):
    return _dot_raw(a, b, kind), (a, b)


def _bdot_bwd(kind, res, g):
    a, b = res
    if kind == "nn":
        da, db = _bdot(g, b, "nt"), _bdot(a, g, "tn")
    elif kind == "nt":
        da, db = _bdot(g, b, "nn"), _bdot(g, a, "tn")
    else:
        da, db = _bdot(b, g, "nt"), _bdot(a, g, "nn")
    return da.astype(a.dtype), db.astype(b.dtype)


_bdot.defvjp(_bdot_fwd, _bdot_bwd)


def _mm(a, b, *, name, ta=False, tb=False, add=None, out_dtype=F32, after=None):
    K, M = a.shape if ta else a.shape[::-1]
    N, Kb = b.shape if tb else b.shape[::-1]
    assert K == Kb, (a.shape, b.shape, ta, tb)
    tm, tn, tk = _mm_tiles(M, N, K, a.dtype.itemsize, b.dtype.itemsize, jnp.dtype(out_dtype).itemsize, add is not None)
    nk = K // tk
    kind = "tn" if ta else ("nt" if tb else "nn")
    assert not (ta and tb)
    a_spec = pl.BlockSpec((tk, tm), lambda i, j, k: (k, i)) if ta else pl.BlockSpec((tm, tk), lambda i, j, k: (i, k))
    b_spec = pl.BlockSpec((tn, tk), lambda i, j, k: (j, k)) if tb else pl.BlockSpec((tk, tn), lambda i, j, k: (k, j))
    o_spec = pl.BlockSpec((tm, tn), lambda i, j, k: (i, j))
    has_add = add is not None

    def body(*refs):
        a_ref, b_ref = refs[0], refs[1]
        add_ref = refs[2] if has_add else None
        o_ref = refs[n_in]
        part = _dot_raw(a_ref[...], b_ref[...], kind)
        if nk == 1:
            if has_add:
                part = part + add_ref[...].astype(F32)
            o_ref[...] = part.astype(o_ref.dtype)
            return
        acc_ref = refs[-1]
        k = pl.program_id(2)

        @pl.when(k == 0)
        def _():
            acc_ref[...] = part

        @pl.when(k > 0)
        def _():
            acc_ref[...] += part

        @pl.when(k == nk - 1)
        def _():
            r = acc_ref[...]
            if has_add:
                r = r + add_ref[...].astype(F32)
            o_ref[...] = r.astype(o_ref.dtype)

    ins = [a, b] + ([add] if has_add else []) + ([after] if after is not None else [])
    in_specs = [a_spec, b_spec] + ([o_spec] if has_add else []) + ([_ANY] if after is not None else [])
    n_in = len(ins)
    return pl.pallas_call(
        body, name=name, grid=(M // tm, N // tn, nk), in_specs=in_specs, out_specs=o_spec,
        out_shape=jax.ShapeDtypeStruct((M, N), out_dtype),
        scratch_shapes=[pltpu.VMEM((tm, tn), F32)] if nk > 1 else [],
        compiler_params=_params(("parallel", "parallel", "arbitrary")),
    )(*ins)


def _mm_fused(a, b, fn, tiles, params, out_dtypes, sums, *, name, ta=False, tb=False, full_rows=False, after=None,
              b2=None, a2=None):
    K, M = a.shape if ta else a.shape[::-1]
    N, Kb = b.shape if tb else b.shape[::-1]
    assert K == Kb and not (ta and tb), (a.shape, b.shape, ta, tb)
    per_elem = sum(t.dtype.itemsize for t in tiles) + sum(jnp.dtype(d).itemsize for d in out_dtypes)
    n_b = 1 if b2 is None else 2
    n_a = 1 if a2 is None else 2
    tm, tn, tk = _mm_tiles(M, N, K, a.dtype.itemsize, b.dtype.itemsize, per_elem, False, full_rows, b2 is not None,
                           n_a, n_b)
    nk = K // tk
    kind = "tn" if ta else ("nt" if tb else "nn")
    a_spec = pl.BlockSpec((tk, tm), lambda i, j, k: (k, i)) if ta else pl.BlockSpec((tm, tk), lambda i, j, k: (i, k))
    b_spec = pl.BlockSpec((tn, tk), lambda i, j, k: (j, k)) if tb else pl.BlockSpec((tk, tn), lambda i, j, k: (k, j))
    o_spec = pl.BlockSpec((tm, tn), lambda i, j, k: (i, j))
    ins = [a, b] + ([b2] if b2 is not None else []) + ([a2] if a2 is not None else [])
    ins += list(tiles) + list(params) + ([after] if after is not None else [])
    in_specs = [a_spec] + [b_spec] * n_b + [a_spec] * (n_a - 1) + [o_spec] * len(tiles)
    in_specs += [pl.BlockSpec(p.shape, lambda i, j, k, nd=p.ndim: (0,) * nd) for p in params]
    in_specs += [_ANY] if after is not None else []
    n_in, n_t, n_p, n_o = len(ins), len(tiles), len(params), len(out_dtypes)

    def body(*refs):
        outs, sum_refs = refs[n_in:n_in + n_o], refs[n_in + n_o:n_in + n_o + len(sums)]

        def finish(*products):
            res = fn(*products, *[t[...] for t in refs[n_a + n_b:n_a + n_b + n_t + n_p]])
            for o_ref, v in zip(outs, res[:n_o]):
                o_ref[...] = v.astype(o_ref.dtype)
            first = jnp.logical_and(pl.program_id(0) == 0, pl.program_id(1) == 0)
            for s_ref, v in zip(sum_refs, res[n_o:]):
                @pl.when(first)
                def _(s_ref=s_ref, v=v):
                    s_ref[...] = v

                @pl.when(jnp.logical_not(first))
                def _(s_ref=s_ref, v=v):
                    s_ref[...] += v

        part = _dot_raw(refs[0][...], refs[1][...], kind)
        if nk == 1 and a2 is not None:
            finish(part + _dot_raw(refs[3][...], refs[2][...], kind))
            return
        if nk == 1:
            finish(part, *([_dot_raw(refs[0][...], refs[2][...], kind)] if b2 is not None else []))
            return
        acc_ref = refs[-1]
        k = pl.program_id(2)

        @pl.when(k == 0)
        def _():
            acc_ref[...] = part

        @pl.when(k > 0)
        def _():
            acc_ref[...] += part

        @pl.when(k == nk - 1)
        def _():
            finish(acc_ref[...])

    out_shape = [_sds((M, N), d) for d in out_dtypes] + list(sums)
    out_specs = [o_spec] * n_o + [pl.BlockSpec(s.shape, lambda i, j, k, nd=len(s.shape): (0,) * nd) for s in sums]
    order = ("arbitrary",) * 3 if sums else ("parallel", "parallel", "arbitrary")
    return pl.pallas_call(
        body, name=name, grid=(M // tm, N // tn, nk), in_specs=in_specs, out_specs=out_specs, out_shape=out_shape,
        scratch_shapes=[pltpu.VMEM((tm, tn), F32)] if nk > 1 else [], compiler_params=_params(order),
    )(*ins)


def _cols(arr, width, block):
    return (arr, width, block)


def _stage(fn, tiles, params, out_tiles, out_sums, *, name, tile=TOK_TILE, after=None):
    def tok_spec(shape, width=None, block=0):
        if len(shape) == 2:
            w = shape[1] if width is None else width
            return pl.BlockSpec((tile, w), lambda i: (i, block))
        return pl.BlockSpec((shape[0], tile, shape[2]), lambda i: (0, i, 0))

    arrays, in_specs = [], []
    for t in tiles:
        if isinstance(t, tuple):
            arr, width, block = t
            arrays.append(arr)
            in_specs.append(tok_spec(arr.shape, width, block))
        else:
            arrays.append(t)
            in_specs.append(tok_spec(t.shape))
    n_tok = arrays[0].shape[0] if arrays[0].ndim == 2 else arrays[0].shape[1]
    for p in params:
        arrays.append(p)
        in_specs.append(pl.BlockSpec(p.shape, lambda i, nd=p.ndim: (0,) * nd))
    out_shape = list(out_tiles) + list(out_sums)
    out_specs = [tok_spec(o.shape) for o in out_tiles]
    out_specs += [pl.BlockSpec(o.shape, lambda i, nd=len(o.shape): (0,) * nd) for o in out_sums]
    n_fn, n_ot = len(arrays), len(out_tiles)
    if after is not None:
        arrays.append(after)
        in_specs.append(_ANY)
    n_in = len(arrays)

    def body(*refs):
        res = fn(*[r[...] for r in refs[:n_fn]])
        if not isinstance(res, (tuple, list)):
            res = (res,)
        outs = refs[n_in:]
        for o_ref, r in zip(outs[:n_ot], res[:n_ot]):
            o_ref[...] = r.astype(o_ref.dtype)
        i = pl.program_id(0)
        for o_ref, r in zip(outs[n_ot:], res[n_ot:]):
            @pl.when(i == 0)
            def _(o_ref=o_ref, r=r):
                o_ref[...] = r.astype(o_ref.dtype)

            @pl.when(i > 0)
            def _(o_ref=o_ref, r=r):
                o_ref[...] += r.astype(o_ref.dtype)

    res = pl.pallas_call(
        body, name=name, grid=(n_tok // tile,), in_specs=in_specs, out_specs=out_specs, out_shape=out_shape,
        compiler_params=_params(("arbitrary",)),
    )(*arrays)
    return res


def _sds(shape, dtype):
    return jax.ShapeDtypeStruct(tuple(shape), dtype)


def _sigmoid(x):
    return 0.5 * jnp.tanh(0.5 * x) + 0.5


def _rms(x, g):
    return x * lax.rsqrt(jnp.mean(x * x, axis=-1, keepdims=True) + EPS) * g


def _norm_bwd_fn(x, dh, dres, g):
    r = lax.rsqrt(jnp.mean(x * x, axis=-1, keepdims=True) + EPS)
    xr = x * r
    dhg = dh * g
    dx = r * (dhg - xr * jnp.mean(xr * dhg, axis=-1, keepdims=True))
    return dx + dres, jnp.sum(dh * xr, axis=0, keepdims=True)


def _mla_a_fn(cq, ckv, g_qa, g_kva):
    return _rms(cq, g_qa), _rms(ckv, g_kva)


def _mla_a_bwd_fn(cq, ckv, dqn, dkvn, g_qa, g_kva):
    _, vjp = jax.vjp(_mla_a_fn, cq, ckv, g_qa, g_kva)
    return vjp((dqn, dkvn))


def _rope(t, cos, sin):
    t1, t2 = t[:, :QK_ROPE // 2], t[:, QK_ROPE // 2:]
    return jnp.concatenate([t1 * cos - t2 * sin, t1 * sin + t2 * cos], axis=-1)


def _mla_b_fn(q_raw, kv_raw, kr, cos, sin, g_qn, g_kn):
    krope = kr[:, :QK_ROPE]
    qs, ks, vs = [], [], []
    for h in range(MLA_HEADS):
        qh = _rms(q_raw[:, h * QK_HEAD:(h + 1) * QK_HEAD], g_qn)
        kvh = kv_raw[:, h * (QK_NOPE + V_HEAD):(h + 1) * (QK_NOPE + V_HEAD)]
        kh = _rms(jnp.concatenate([kvh[:, :QK_NOPE], krope], axis=-1), g_kn)
        qs.append(jnp.concatenate([qh[:, :QK_NOPE], _rope(qh[:, QK_NOPE:], cos, sin)], axis=-1))
        ks.append(jnp.concatenate([kh[:, :QK_NOPE], _rope(kh[:, QK_NOPE:], cos, sin)], axis=-1))
        vs.append(kvh[:, QK_NOPE:])
    return jnp.stack(qs), jnp.stack(ks), jnp.stack(vs)


def _mla_b_bwd_fn(q_raw, kv_raw, kr, cos, sin, dq, dk, dv, g_qn, g_kn):
    _, vjp = jax.vjp(lambda a, b, c, d, e: _mla_b_fn(a, b, c, cos, sin, d, e), q_raw, kv_raw, kr, g_qn, g_kn)
    return vjp((dq, dk, dv))


def _post_fn(a, o_f, o_b, hg, g_hgo):
    o = o_f + o_b
    parts = [a]
    for h in range(HG_HEADS):
        s = slice(h * HG_DK, (h + 1) * HG_DK)
        gate = hg[:, s]
        parts.append(_rms(o[:, s], g_hgo[:, s]) * (gate * _sigmoid(gate)))
    return jnp.concatenate(parts, axis=-1)


def _post_bwd_fn(o_f, o_b, hg, dr, g_hgo):
    def f(o, hg, g):
        return _post_fn(jnp.zeros_like(o), o, jnp.zeros_like(o), hg, g)[:, o.shape[1]:]
    _, vjp = jax.vjp(f, o_f + o_b, hg, g_hgo)
    return vjp(dr)


def _swiglu_fn(gt, up):
    return gt * _sigmoid(gt) * up


def _resid_norm_fn(acc, x, g):
    x_new = acc + x
    return x_new, _rms(x_new, g)


def _swiglu_bwd_fn(gt, up, dact):
    s = _sigmoid(gt)
    silu = gt * s
    return dact * up * (s + silu * (1.0 - s)), dact * silu


def _ple_loss_fn(x2, pg, pp, target):
    gate = _sigmoid(pg)
    err = x2 + gate * pp - target
    dx3 = err * (1.0 / err.shape[-1])
    loss = 0.5 * jnp.sum(jnp.mean(err * err, axis=-1, keepdims=True), axis=0, keepdims=True)
    return dx3, dx3 * pp * gate * (1.0 - gate), dx3 * gate, loss


def _attention_fwd(q, k, v):
    H, T, D = q.shape
    DV = v.shape[-1]
    tq, ck = min(ATT_TQ, T), min(ATT_CHUNK, T)
    c2 = (D ** -0.5) * LOG2_E

    def body(q_ref, k_ref, v_ref, o_ref, lse_ref):
        q_i = q_ref[0]

        def chunk(c, carry):
            m, l, acc = carry
            rows = pl.ds(pl.multiple_of(c * ck, ck), ck)
            s = _dot_raw(q_i, k_ref[0, rows, :], "nt")
            m_new = jnp.maximum(m, jnp.max(s, axis=-1, keepdims=True))
            p = jnp.exp2((s - m_new) * c2)
            alpha = jnp.exp2((m - m_new) * c2)
            l = l * alpha + jnp.sum(p, axis=-1, keepdims=True)
            acc = acc * alpha + _dot_raw(p, v_ref[0, rows, :], "nn")
            return m_new, l, acc

        init = (jnp.full((tq, 1), -jnp.inf, F32), jnp.zeros((tq, 1), F32), jnp.zeros((tq, DV), F32))
        m, l, acc = lax.fori_loop(0, T // ck, chunk, init, unroll=True)
        o_ref[...] = acc / l
        lse_ref[0] = m * c2 + jnp.log2(l)

    return pl.pallas_call(
        body, name="attention_fwd", grid=(H, T // tq),
        in_specs=[pl.BlockSpec((1, tq, D), lambda h, i: (h, i, 0)),
                  pl.BlockSpec((1, T, D), lambda h, i: (h, 0, 0)),
                  pl.BlockSpec((1, T, DV), lambda h, i: (h, 0, 0))],
        out_specs=[pl.BlockSpec((tq, DV), lambda h, i: (i, h)),
                   pl.BlockSpec((1, tq, 1), lambda h, i: (h, i, 0))],
        out_shape=[_sds((T, H * DV), F32), _sds((H, T, 1), F32)],
        compiler_params=_params(("parallel", "parallel")),
    )(q, k, v)


def _attention_bwd(q, k, v, o, lse2, dmix):
    H, T, D = q.shape
    DV = v.shape[-1]
    tk, cq = min(ATT_TK, T), min(ATT_CHUNK, T)
    scale = D ** -0.5
    c2 = scale * LOG2_E

    def body(q_ref, k_ref, v_ref, o_ref, lse_ref, do_ref, dq_ref, dk_ref, dv_ref, delta_ref):
        j = pl.program_id(1)

        @pl.when(j == 0)
        def _():
            delta = lax.dot_general(jnp.ones((8, DV), F32), do_ref[...] * o_ref[...], (((1,), (1,)), ((), ())),
                                    precision=lax.Precision.HIGHEST, preferred_element_type=F32)
            for i in range(T // cq):
                delta_ref[i] = delta[:, i * cq:(i + 1) * cq]
            dq_ref[0] = jnp.zeros((T, D), F32)

        k_j, v_j = k_ref[0], v_ref[0]
        dk_ref[0] = jnp.zeros((tk, D), F32)
        dv_ref[0] = jnp.zeros((tk, DV), F32)

        def chunk(c, carry):
            rows = pl.ds(pl.multiple_of(c * cq, cq), cq)
            q_c = q_ref[0, rows, :]
            do_c = do_ref[rows, :].astype(BF16)
            st = _dot_raw(k_j, q_c, "nt")
            pt = jnp.exp2(st * c2 - lse_ref[0, c])
            dv_ref[0] += _dot_raw(pt, do_c, "nn")
            dpt = _dot_raw(v_j, do_c, "nt")
            dst = pt * (dpt - delta_ref[c, 0:1, :]) * scale
            dk_ref[0] += _dot_raw(dst, q_c, "nn")
            dq_ref[0, rows, :] += _dot_raw(dst, k_j, "tn")
            return carry

        lax.fori_loop(0, T // cq, chunk, 0, unroll=True)

    return pl.pallas_call(
        body, name="attention_bwd", grid=(H, T // tk),
        in_specs=[pl.BlockSpec((1, T, D), lambda h, j: (h, 0, 0)),
                  pl.BlockSpec((1, tk, D), lambda h, j: (h, j, 0)),
                  pl.BlockSpec((1, tk, DV), lambda h, j: (h, j, 0)),
                  pl.BlockSpec((T, DV), lambda h, j: (0, h)),
                  pl.BlockSpec((1, T // cq, 1, cq), lambda h, j: (h, 0, 0, 0)),
                  pl.BlockSpec((T, DV), lambda h, j: (0, h))],
        out_specs=[pl.BlockSpec((1, T, D), lambda h, j: (h, 0, 0)),
                   pl.BlockSpec((1, tk, D), lambda h, j: (h, j, 0)),
                   pl.BlockSpec((1, tk, DV), lambda h, j: (h, j, 0))],
        out_shape=[_sds((H, T, D), F32), _sds((H, T, D), F32), _sds((H, T, DV), F32)],
        scratch_shapes=[pltpu.VMEM((T // cq, 8, cq), F32)],
        compiler_params=_params(("parallel", "arbitrary")),
    )(q, k, v, o, lse2.reshape(H, T // cq, 1, cq), dmix)


def _split3_dot(ones, x, kind):
    hi = x.astype(BF16)
    rest = x - hi.astype(F32)
    mid = rest.astype(BF16)
    lo = (rest - mid.astype(F32)).astype(BF16)
    return (_dot_raw(ones, hi, kind) + _dot_raw(ones, mid, kind)) + _dot_raw(ones, lo, kind)


@jax.custom_vjp
def _running_sum(x, tri):
    return _split3_dot(tri, x, "nn")


def _running_sum_fwd(x, tri):
    return _split3_dot(tri, x, "nn"), tri


def _running_sum_bwd(tri, g):
    return _split3_dot(tri, g, "tn"), jnp.zeros_like(tri)


_running_sum.defvjp(_running_sum_fwd, _running_sum_bwd)


def _gla_block(hq, hf, hi, lower, st_in, *, rev, dot):
    rows, dk = hq.shape
    G, C = rows // CHUNK, CHUNK
    q = hq * _sigmoid(hq)
    f = lower + (1.0 - lower) * _sigmoid(hf)
    k = 1.0 - f
    logf = jnp.log2(f)
    q3, k3, v3, lf3 = (t.reshape(G, C, dk) for t in (q, k, hi, logf))
    r = lax.broadcasted_iota(jnp.int32, (C, C), 0)
    c = lax.broadcasted_iota(jnp.int32, (C, C), 1)
    tri = ((r <= c) if rev else (r >= c)).astype(F32)
    b = _running_sum(lf3, jnp.broadcast_to(tri, (G, C, C)))
    tpos = lax.broadcasted_iota(jnp.int32, (1, C, 1), 1)
    first_half = (tpos >= C // 2) if rev else (tpos <= C // 2 - 1)
    b_mid = jnp.sum(jnp.where(first_half, lf3, 0.0), axis=1, keepdims=True)
    b_last = jnp.sum(lf3, axis=1, keepdims=True)
    a = dot(q3 * jnp.exp2(b - b_mid), k3 * jnp.exp2(b_mid - b), "nt") * tri
    o_intra = dot(a, v3, "nn")
    kv_t = dot(v3, k3 * jnp.exp2(b_last - b), "tn")
    decay = jnp.exp2(b_last)
    qd = q3 * jnp.exp2(b)
    st = st_in
    o_inter = [None] * G
    for g in (reversed(range(G)) if rev else range(G)):
        o_inter[g] = dot(qd[g], st, "nt")
        st = st * decay[g] + kv_t[g]
    o = o_intra.reshape(rows, dk) + jnp.concatenate(o_inter, axis=0)
    return o, st


def _gla_fwd(z, lower3, *, rev, col_q, col_f, col_v, hp):
    T = z.shape[0]
    rows = min(GLA_GROUP * CHUNK, T)
    nb = T // rows
    wide = hp * HG_DK
    blk = (lambda n: nb - 1 - n) if rev else (lambda n: n)

    def body(hq_ref, hf_ref, hi_ref, low_ref, o_ref, st_out_ref, st_ref):
        @pl.when(pl.program_id(1) == 0)
        def _():
            st_ref[...] = jnp.zeros_like(st_ref)

        st_in = [st_ref[i] for i in range(hp)]
        heads = []
        for i in range(hp):
            cols = slice(i * HG_DK, (i + 1) * HG_DK)
            heads.append(_gla_block(hq_ref[:, cols], hf_ref[:, cols], hi_ref[:, cols], low_ref[i], st_in[i], rev=rev,
                                    dot=_dot_raw))
        for i, (o, st) in enumerate(heads):
            st_out_ref[i, 0] = st_in[i]
            o_ref[:, i * HG_DK:(i + 1) * HG_DK] = o
            st_ref[i] = st

    def zspec(col):
        return pl.BlockSpec((rows, wide), lambda h, n: (blk(n), col // wide + h))

    return pl.pallas_call(
        body, name="gla_fwd_rev" if rev else "gla_fwd", grid=(HG_HEADS // hp, nb),
        in_specs=[zspec(col_q), zspec(col_f), zspec(col_v), pl.BlockSpec((hp, 1, HG_DK), lambda h, n: (h, 0, 0))],
        out_specs=[pl.BlockSpec((rows, wide), lambda h, n: (blk(n), h)),
                   pl.BlockSpec((hp, 1, HG_DK, HG_DK), lambda h, n: (h, blk(n), 0, 0))],
        out_shape=[_sds((T, HG_HEADS * HG_DK), F32), _sds((HG_HEADS, nb, HG_DK, HG_DK), F32)],
        scratch_shapes=[pltpu.VMEM((hp, HG_DK, HG_DK), F32)],
        compiler_params=_params(("parallel", "arbitrary")),
    )(z, z, z, lower3)


def _gla_bwd(z, lower3, states, do, prev, *, rev, col_q, col_f, col_v, hp):
    T = z.shape[0]
    rows = min(GLA_GROUP * CHUNK, T)
    nb = T // rows
    wide = hp * HG_DK
    blk = (lambda n: n) if rev else (lambda n: nb - 1 - n)
    has_prev = prev is not None
    fn = functools.partial(_gla_block, rev=rev, dot=_bdot)

    def body(*refs):
        hq_ref, hf_ref, hi_ref, low_ref, st_ref, do_ref = refs[:6]
        rest = refs[6:]
        if has_prev:
            pq_ref, pi_ref = rest[:2]
            rest = rest[2:]
        dhq_ref, dhi_ref, dhf_ref, dlow_ref, dst_ref = rest
        n = pl.program_id(1)

        @pl.when(n == 0)
        def _():
            dst_ref[...] = jnp.zeros_like(dst_ref)

        dst_in = [dst_ref[i] for i in range(hp)]
        heads = []
        for i in range(hp):
            cols = slice(i * HG_DK, (i + 1) * HG_DK)
            _, vjp = jax.vjp(fn, hq_ref[:, cols], hf_ref[:, cols], hi_ref[:, cols], low_ref[i], st_ref[i, 0])
            dhq, dhf, dhi, dlow, dst = vjp((do_ref[:, cols], dst_in[i]))
            if has_prev:
                dhq = dhq + pq_ref[:, cols]
                dhi = dhi + pi_ref[:, cols]
            heads.append((dhq, dhf, dhi, dlow, dst))
        for i, (dhq, dhf, dhi, dlow, dst) in enumerate(heads):
            cols = slice(i * HG_DK, (i + 1) * HG_DK)
            dst_ref[i] = dst
            dhq_ref[:, cols] = dhq.astype(dhq_ref.dtype)
            dhi_ref[:, cols] = dhi.astype(dhi_ref.dtype)
            dhf_ref[:, cols] = dhf.astype(dhf_ref.dtype)

        @pl.when(n == 0)
        def _():
            for i in range(hp):
                dlow_ref[i] = heads[i][3]

        @pl.when(n > 0)
        def _():
            for i in range(hp):
                dlow_ref[i] += heads[i][3]

    def zspec(col):
        return pl.BlockSpec((rows, wide), lambda h, n: (blk(n), col // wide + h))

    hspec = pl.BlockSpec((rows, wide), lambda h, n: (blk(n), h))
    in_specs = [zspec(col_q), zspec(col_f), zspec(col_v), pl.BlockSpec((hp, 1, HG_DK), lambda h, n: (h, 0, 0)),
                pl.BlockSpec((hp, 1, HG_DK, HG_DK), lambda h, n: (h, blk(n), 0, 0)), hspec]
    ins = [z, z, z, lower3, states, do]
    if has_prev:
        in_specs += [hspec, hspec]
        ins += list(prev)
    full_wide = HG_HEADS * HG_DK
    acc_dtype = BF16 if has_prev else F32
    return pl.pallas_call(
        body, name="gla_bwd_rev" if rev else "gla_bwd", grid=(HG_HEADS // hp, nb),
        in_specs=in_specs,
        out_specs=[hspec, hspec, hspec, pl.BlockSpec((hp, 1, HG_DK), lambda h, n: (h, 0, 0))],
        out_shape=[_sds((T, full_wide), acc_dtype), _sds((T, full_wide), acc_dtype), _sds((T, full_wide), BF16),
                   _sds((HG_HEADS, 1, HG_DK), F32)],
        scratch_shapes=[pltpu.VMEM((hp, HG_DK, HG_DK), F32)],
        compiler_params=_params(("parallel", "arbitrary")),
    )(*ins)


def _lower_fn(lb):
    e = jnp.exp(lb - jnp.max(lb, axis=0, keepdims=True))
    return (e / jnp.sum(e, axis=0, keepdims=True))[0]


def _lower_bounds(lb):
    def body(lb_ref, o_ref):
        o_ref[...] = _lower_fn(lb_ref[...])
    return pl.pallas_call(body, name="lower_bounds", out_shape=_sds(lb.shape[1:], F32))(lb)


def _row_tile(r, cap=1024):
    best = None
    for t in range(16, min(r, cap) + 1, 16):
        if r % t == 0:
            best = t
    return best if best is not None else r


def _sum4(shards, recv, chip, *, name):
    _, R, C = shards.shape
    tr = _row_tile(R)

    def body(chip_ref, o_ref, r_ref, out_ref):
        out_ref[...] = ((o_ref[0].astype(F32) + r_ref[0].astype(F32)) + r_ref[1].astype(F32)) + r_ref[2].astype(F32)

    grid_spec = pltpu.PrefetchScalarGridSpec(
        num_scalar_prefetch=1, grid=(R // tr,),
        in_specs=[pl.BlockSpec((1, tr, C), lambda i, chip_ref: (chip_ref[0], i, 0)),
                  pl.BlockSpec((3, tr, C), lambda i, chip_ref: (0, i, 0))],
        out_specs=pl.BlockSpec((tr, C), lambda i, chip_ref: (i, 0)))
    return pl.pallas_call(
        body, name=name, grid_spec=grid_spec, out_shape=_sds((R, C), F32), compiler_params=_params(("parallel",)),
    )(chip, shards, recv)


def _adamw_math(w, g, m, v):
    m = ADAM_B1 * m + (1.0 - ADAM_B1) * g
    v = ADAM_B2 * v + (1.0 - ADAM_B2) * (g * g)
    m_hat = m / (1.0 - ADAM_B1 ** ADAM_STEP)
    v_hat = v / (1.0 - ADAM_B2 ** ADAM_STEP)
    delta = -ADAM_LR * (m_hat / (jnp.sqrt(v_hat) + ADAM_EPS) + ADAM_WD * w)
    return delta, m, v


def _adamw(w, g_a, g_b, m, v, *, name):
    R, C = w.shape
    tr = _row_tile(R)
    two = g_b is not None

    def body(*refs):
        w_ref, ga_ref = refs[0], refs[1]
        rest = refs[2:]
        g = ga_ref[...]
        if two:
            g = g + rest[0][...]
            rest = rest[1:]
        m_ref, v_ref, g_out, d_out, m_out, v_out, token = rest
        delta, m_new, v_new = _adamw_math(w_ref[...], g, m_ref[...], v_ref[...])
        g_out[...] = g
        d_out[...] = delta
        m_out[...] = m_new
        v_out[...] = v_new
        token[...] = jnp.zeros_like(token)

    spec = pl.BlockSpec((tr, C), lambda i: (i, 0))
    ins = [w, g_a] + ([g_b] if two else []) + [m, v]
    return pl.pallas_call(
        body, name=name, grid=(R // tr,), in_specs=[spec] * len(ins),
        out_specs=[spec] * 4 + [pl.BlockSpec((8, LANE), lambda i: (0, 0))],
        out_shape=[_sds((R, C), F32)] * 4 + [_sds((8, LANE), F32)], compiler_params=_params(("arbitrary",)),
    )(*ins)


def _adamw_small(red, lb_full, ws, ms, vs):
    n = len(ws)

    def pieces(shape):
        out = []
        for j, idx in enumerate(itertools.product(*[range(d) for d in shape[:-1]])):
            out.append((idx[:-1] + (slice(idx[-1], idx[-1] + 1), slice(None)), j * shape[-1]))
        return out

    def body(*refs):
        red_ref, lb_ref = refs[0], refs[1]
        w_refs, m_refs, v_refs = refs[2:2 + n], refs[2 + n:2 + 2 * n], refs[2 + 2 * n:2 + 3 * n]
        out_refs = refs[2 + 3 * n:]
        chip = 2 * lax.axis_index("x") + lax.axis_index("y")
        n_f, shard = lb_ref.shape[-1], w_refs[n - 1].shape[-1]
        _, vjp = jax.vjp(_lower_fn, lb_ref[...])
        dlb = vjp(red_ref[8:10, 0:n_f])[0]
        for i in range(n):
            width = w_refs[i].shape[-1]
            for j, (at, lane) in enumerate(pieces(w_refs[i].shape)):
                if i < n - 1:
                    g = red_ref[i:i + 1, lane:lane + width]
                else:
                    row = dlb[j // 2][j % 2:j % 2 + 1]
                    g = sum(jnp.where(chip == q, row[:, q * shard:(q + 1) * shard], 0.0) for q in range(N_CHIPS))
                delta, m_new, v_new = _adamw_math(w_refs[i][at], g, m_refs[i][at], v_refs[i][at])
                for o_ref, val in zip(out_refs[4 * i:4 * i + 4], (g, delta, m_new, v_new)):
                    o_ref[at] = val

    return pl.pallas_call(
        body, name="adamw_small", out_shape=[_sds(w.shape, F32) for w in ws for _ in range(4)],
    )(red, lb_full, *ws, *ms, *vs)


def _chip_peers():
    x, y, c = lax.axis_index("x"), lax.axis_index("y"), lax.axis_index("c")
    return (x, y, c), 2 * x + y, [(1 - x, y), (x, 1 - y), (1 - x, 1 - y)]


_HBM = pl.BlockSpec(memory_space=pltpu.HBM)
_SEM = pl.BlockSpec(memory_space=pltpu.SEMAPHORE)
_EFFECT = pltpu.SideEffectType.DATAFLOW_SIDE_EFFECTING


def _exchange_copies(srcs, lands, sems, mode):
    (x, y, c), me, chips = _chip_peers()
    copies = []
    for t, (src, land) in enumerate(zip(srcs, lands)):
        if mode == "swap":
            copies.append(pltpu.make_async_remote_copy(src, land, sems[0].at[3 * t], sems[1].at[3 * t],
                                                       device_id=(x, y, 1 - c), device_id_type=MESH))
            continue
        for k, (px, py) in enumerate(chips):
            gather = mode == "gather"
            copies.append(pltpu.make_async_remote_copy(
                src if gather else src.at[2 * px + py], land.at[me] if gather else land.at[k],
                sems[0].at[3 * t + k], sems[1].at[3 * t + k], device_id=(px, py, c), device_id_type=MESH))
        if mode == "gather":
            copies.append(pltpu.make_async_copy(src, land.at[me], sems[2].at[t]))
    return copies


def _exchange_start(srcs, *, mode, name, after=None):
    n = len(srcs)
    n_sem = 3 if mode == "gather" else 2
    n_in = 2 * n + (after is not None)
    land_shape = {"gather": lambda s: (N_CHIPS,) + s.shape, "scatter": lambda s: (3,) + s.shape[1:], "swap": lambda s: s.shape}
    lands = [_sds(land_shape[mode](s), s.dtype) for s in srcs]

    def body(*refs):
        for cp in _exchange_copies(refs[:n], refs[n:2 * n], refs[n_in:n_in + n_sem], mode):
            cp.start()
        token = refs[-1]
        token[...] = jnp.zeros_like(token)

    sem_shapes = [pltpu.SemaphoreType.DMA((3 * n,)), pltpu.SemaphoreType.DMA((3 * n,))]
    sem_shapes += [pltpu.SemaphoreType.DMA((n,))] if mode == "gather" else []
    thru = [pltpu.HBM(s.shape, s.dtype) for s in srcs] + [pltpu.HBM(l.shape, l.dtype) for l in lands]
    res = pl.pallas_call(
        body, name=name, in_specs=[_HBM] * (2 * n) + [_ANY] * (after is not None),
        out_specs=[_SEM] * n_sem + [_HBM] * (2 * n) + [pl.BlockSpec(memory_space=pltpu.VMEM)],
        out_shape=sem_shapes + thru + [_sds((8, LANE), F32)], input_output_aliases={i: n_sem + i for i in range(2 * n)},
        compiler_params=pltpu.CompilerParams(has_side_effects=_EFFECT),
    )(*[pltpu.with_memory_space_constraint(s, pltpu.HBM) for s in srcs],
      *[pltpu.with_memory_space_constraint(lax.empty(l.shape, l.dtype), pltpu.HBM) for l in lands],
      *([after] if after is not None else []))
    return (res[:n_sem], res[n_sem:n_sem + n], res[n_sem + n:n_sem + 2 * n], mode), res[-1]


def _exchange_wait(started, after, *, name):
    sems, srcs, lands, mode = started
    n, n_sem = len(srcs), len(sems)
    after = list(after) if isinstance(after, (list, tuple)) else [after]

    def body(*refs):
        for cp in _exchange_copies(refs[:n], refs[n:2 * n], refs[2 * n:2 * n + n_sem], mode):
            cp.wait()

    res = pl.pallas_call(
        body, name=name, in_specs=[_HBM] * (2 * n) + [_SEM] * n_sem + [_ANY] * len(after), out_specs=[_HBM] * (2 * n),
        out_shape=[pltpu.HBM(a.shape, a.dtype) for a in list(srcs) + list(lands)],
        input_output_aliases={i: i for i in range(2 * n)},
        compiler_params=pltpu.CompilerParams(has_side_effects=_EFFECT),
    )(*srcs, *lands, *sems, *after)
    return res[:n], res[n:]


def _allreduce_small(pack, after):
    R, C = pack.shape

    def body(in_ref, after_ref, out_ref, slots, send_sems, recv_sems):
        x, y, c = lax.axis_index("x"), lax.axis_index("y"), lax.axis_index("c")
        me = 4 * x + 2 * y + c
        slots[me] = in_ref[...]
        copies = []
        for k in range(1, N_DEV):
            peer = (x ^ ((k >> 2) & 1), y ^ ((k >> 1) & 1), c ^ (k & 1))
            cp = pltpu.make_async_remote_copy(in_ref, slots.at[me], send_sems.at[k - 1], recv_sems.at[k - 1],
                                              device_id=peer, device_id_type=MESH)
            cp.start()
            copies.append(cp)
        for cp in copies:
            cp.wait()
        acc = slots[0]
        for d in range(1, N_DEV):
            acc = acc + slots[d]
        out_ref[...] = acc

    return pl.pallas_call(
        body, name="allreduce_small", out_shape=_sds((R, C), F32),
        in_specs=[pl.BlockSpec(memory_space=pltpu.VMEM), _ANY], out_specs=pl.BlockSpec(memory_space=pltpu.VMEM),
        scratch_shapes=[pltpu.VMEM((N_DEV, R, C), F32), pltpu.SemaphoreType.DMA((N_DEV - 1,)),
                        pltpu.SemaphoreType.DMA((N_DEV - 1,))],
        compiler_params=_params(),
    )(pack, after)


_Z_CQ, _Z_CKV, _Z_HQ, _Z_HFF, _Z_HFB, _Z_HI, _Z_HG, _Z_KR, _Z_END = 0, 256, 512, 1024, 1536, 2048, 2560, 3072, 3200


def _to_z_layout(wt):
    pad = jnp.zeros((_Z_END - _Z_KR - QK_ROPE, wt.shape[1]), wt.dtype)
    return jnp.concatenate([wt[:512], wt[512 + QK_ROPE:], wt[512:512 + QK_ROPE], pad], axis=0)


def _from_z_layout(wt):
    return jnp.concatenate([wt[:512], wt[_Z_KR:_Z_KR + QK_ROPE], wt[512:_Z_KR]], axis=0)


def _col_shards_to_full(g):
    return jnp.transpose(g, (1, 0, 2)).reshape(g.shape[1], -1)


def _full_to_col_shards(w):
    r, c = w.shape
    return jnp.transpose(w.reshape(r, N_CHIPS, c // N_CHIPS), (1, 0, 2))


def _full_to_row_shards(w):
    r, c = w.shape
    return w.reshape(N_CHIPS, r // N_CHIPS, c)


def kernel(x, p, positions, g_mix, w_in, g_qa, g_kva, w_qb, w_kvb, g_qn, g_kn, lb_param, g_hgo, w_o, g_ffn, w_gate, w_up, w_down, g_ple, w_ple_gate, w_ple_proj, loss_target, m_g_mix, m_w_in, m_g_qa, m_g_kva, m_w_qb, m_w_kvb, m_g_qn, m_g_kn, m_lb_param, m_g_hgo, m_w_o, m_g_ffn, m_w_gate, m_w_up, m_w_down, m_g_ple, m_w_ple_gate, m_w_ple_proj, v_g_mix, v_w_in, v_g_qa, v_g_kva, v_w_qb, v_w_kvb, v_g_qn, v_g_kn, v_lb_param, v_g_hgo, v_w_o, v_g_ffn, v_w_gate, v_w_up, v_w_down, v_g_ple, v_w_ple_gate, v_w_ple_proj):
    w_named = dict(g_mix=g_mix, w_in=w_in, g_qa=g_qa, g_kva=g_kva, w_qb=w_qb, w_kvb=w_kvb, g_qn=g_qn, g_kn=g_kn,
                   lb_param=lb_param, g_hgo=g_hgo, w_o=w_o, g_ffn=g_ffn, w_gate=w_gate, w_up=w_up, w_down=w_down,
                   g_ple=g_ple, w_ple_gate=w_ple_gate, w_ple_proj=w_ple_proj)
    m_named = dict(g_mix=m_g_mix, w_in=m_w_in, g_qa=m_g_qa, g_kva=m_g_kva, w_qb=m_w_qb, w_kvb=m_w_kvb, g_qn=m_g_qn,
                   g_kn=m_g_kn, lb_param=m_lb_param, g_hgo=m_g_hgo, w_o=m_w_o, g_ffn=m_g_ffn, w_gate=m_w_gate,
                   w_up=m_w_up, w_down=m_w_down, g_ple=m_g_ple, w_ple_gate=m_w_ple_gate, w_ple_proj=m_w_ple_proj)
    v_named = dict(g_mix=v_g_mix, w_in=v_w_in, g_qa=v_g_qa, g_kva=v_g_kva, w_qb=v_w_qb, w_kvb=v_w_kvb, g_qn=v_g_qn,
                   g_kn=v_g_kn, lb_param=v_lb_param, g_hgo=v_g_hgo, w_o=v_w_o, g_ffn=v_g_ffn, w_gate=v_w_gate,
                   w_up=v_w_up, w_down=v_w_down, g_ple=v_g_ple, w_ple_gate=v_w_ple_gate, w_ple_proj=v_w_ple_proj)
    order = list(w_named)
    transposed = ("w_in", "w_qb", "w_gate", "w_up")
    col_sharded = ("w_kvb", "w_ple_proj")
    row_sharded = ("w_o", "w_down", "w_ple_gate")
    big = transposed + col_sharded + row_sharded

    def view(n, a):
        return jnp.transpose(a[0]) if n in transposed else a[0]

    def unview(n, a):
        return (jnp.transpose(a) if n in transposed else a)[None]

    def to_shards(n, g):
        return _full_to_col_shards(g) if n in col_sharded else _full_to_row_shards(g)

    x2d, p2d, tgt = x[0], p[0, 0], loss_target[0]
    T, D = x2d.shape

    lb_flat = lb_param.reshape(-1, lb_param.shape[-1])
    gather_groups = (("w_in",), ("w_qb", "w_kvb"), ("w_o", "w_gate", "w_up", "w_down", "w_ple_gate", "w_ple_proj"))
    gather_started = []

    casts = {n: view(n, w_named[n]).astype(BF16) for n in big}

    def gather_start(gi, after):
        srcs = [casts[n] for n in gather_groups[gi]] + ([lb_flat] if gi == 0 else [])
        started, token = _exchange_start(srcs, mode="gather", name=f"gather_start_{gi}", after=after)
        gather_started.append(started)
        return token

    full = {}

    def gather_wait(gi, after):
        _, got = _exchange_wait(gather_started[gi], after, name=f"gather_wait_{gi}")
        for n, g in zip(gather_groups[gi], got):
            full[n] = _col_shards_to_full(g) if n in col_sharded else g.reshape(-1, g.shape[-1])
        return got

    g_hgo_row = g_hgo.reshape(1, -1)

    inv_freq = ROPE_THETA ** (-jnp.arange(0, QK_ROPE, 2, dtype=F32) / QK_ROPE)
    ang = positions[0].astype(F32)[:, None] * inv_freq
    cos, sin = jnp.cos(ang), jnp.sin(ang)
    token = gather_start(0, None)
    h1 = _stage(_rms, [x2d], [g_mix], [_sds((T, D), BF16)], [], name="norm_mix", after=token)[0]
    got = gather_wait(0, [h1, cos, sin] + [casts[n] for g in gather_groups[1:] for n in g])
    token = got[0]
    for gi in range(1, len(gather_groups)):
        token = gather_start(gi, token)
    lb_full = _col_shards_to_full(got[-1]).reshape(lb_param.shape[0], lb_param.shape[1], -1)
    w_in_zt = _to_z_layout(full["w_in"])
    z = _mm(h1, w_in_zt, tb=True, name="in_proj", after=token)
    qn, kvn = _stage(_mla_a_fn, [_cols(z, 256, 0), _cols(z, 256, 1)], [g_qa, g_kva],
                     [_sds((T, 256), BF16), _sds((T, 256), BF16)], [], name="mla_latent_norm")
    gather_wait(1, qn)
    q_raw = _mm(qn, full["w_qb"], tb=True, name="q_up")
    kv_raw = _mm(kvn, full["w_kvb"], name="kv_up")
    kr = _cols(z, LANE, _Z_KR // LANE)
    q, k, v = _stage(_mla_b_fn, [q_raw, kv_raw, kr, cos, sin], [g_qn, g_kn],
                     [_sds((MLA_HEADS, T, QK_HEAD), BF16), _sds((MLA_HEADS, T, QK_HEAD), BF16),
                      _sds((MLA_HEADS, T, V_HEAD), BF16)], [], name="mla_qk_norm_rope")
    att, lse = _attention_fwd(q, k, v)

    lower = _lower_bounds(lb_full)
    lower3 = lower.reshape(2, HG_HEADS, 1, HG_DK)
    o_f, st_f = _gla_fwd(z, lower3[0], rev=False, col_q=_Z_HQ, col_f=_Z_HFF, col_v=_Z_HI, hp=GLA_FWD_HEADS)
    o_b, st_b = _gla_fwd(z, lower3[1], rev=True, col_q=_Z_HQ, col_f=_Z_HFB, col_v=_Z_HI, hp=GLA_FWD_HEADS)
    hg = _cols(z, 512, _Z_HG // 512)
    mix = _stage(_post_fn, [att, o_f, o_b, hg], [g_hgo_row], [_sds((T, att.shape[1] + o_f.shape[1]), BF16)], [],
                 name="mix_out")[0]
    gather_wait(2, mix)
    x1, h2 = _mm_fused(mix, full["w_o"], _resid_norm_fn, [x2d], [g_ffn], [F32, BF16], [], full_rows=True,
                       name="out_proj")
    gt, up, act = _mm_fused(h2, full["w_gate"], lambda gt, up: (gt, up, _swiglu_fn(gt, up)), [], [], [BF16, BF16, BF16],
                            [], tb=True, b2=full["w_up"], name="ffn_gate_up")
    x2, h3 = _mm_fused(act, full["w_down"], _resid_norm_fn, [x1], [g_ple], [F32, BF16], [], full_rows=True,
                       name="ffn_down")
    pp = _mm(p2d, full["w_ple_proj"], name="ple_proj")
    dx3, dpg, dpp, loss_part = _mm_fused(
        h3, full["w_ple_gate"], lambda acc, pp, x2, tgt: _ple_loss_fn(x2, acc, pp, tgt), [pp, x2, tgt], [],
        [F32, BF16, BF16], [_sds((1, 1), F32)], full_rows=True, name="ple_gate_loss")

    grads = {}
    scatter_groups = (("w_ple_proj", "w_ple_gate", "w_down", "w_gate", "w_up", "w_o"), ("w_qb", "w_kvb", "w_in"))
    scatter_started = []

    def scatter_start(gi):
        srcs = [to_shards(n, grads[n]) for n in scatter_groups[gi]]
        started, token = _exchange_start(srcs, mode="scatter", name=f"scatter_start_{gi}")
        scatter_started.append(started)
        return token

    chip = 2 * lax.axis_index("x") + lax.axis_index("y")
    swap_started = []

    def reduce_group(gi, after):
        shards, recvs = _exchange_wait(scatter_started[gi], after, name=f"scatter_wait_{gi}")
        sums = [_sum4(s, r, chip.reshape(1), name="sum_" + n) for n, s, r in zip(scatter_groups[gi], shards, recvs)]
        started, token = _exchange_start(sums, mode="swap", name=f"swap_start_{gi}")
        swap_started.append(started)
        return token

    grads["w_ple_proj"] = _mm(p2d, dpp, ta=True, out_dtype=BF16, name="d_w_ple_proj")
    grads["w_ple_gate"] = _mm(h3, dpg, ta=True, out_dtype=BF16, name="d_w_ple_gate")
    dx2, grads["g_ple"] = _mm_fused(
        dpg, full["w_ple_gate"], lambda acc, x2, dx3, g: _norm_bwd_fn(x2, acc, dx3, g), [x2, dx3], [g_ple],
        [F32], [_sds((1, D), F32)], tb=True, full_rows=True, name="d_h3_norm_ple_bwd")
    dgt, dup = _mm_fused(dx2, full["w_down"], lambda acc, gt, up: _swiglu_bwd_fn(gt.astype(F32), up.astype(F32), acc), [gt, up], [],
                         [BF16, BF16], [], tb=True, name="d_act_swiglu_bwd")
    grads["w_down"] = _mm(act, dx2, ta=True, out_dtype=BF16, name="d_w_down")
    grads["w_gate"] = _mm(dgt, h2, ta=True, out_dtype=BF16, name="d_w_gate")
    grads["w_up"] = _mm(dup, h2, ta=True, out_dtype=BF16, name="d_w_up")
    dx1, grads["g_ffn"] = _mm_fused(
        dgt, full["w_gate"], lambda acc, x1, dx2, g: _norm_bwd_fn(x1, acc, dx2, g), [x1, dx2], [g_ffn],
        [F32], [_sds((1, D), F32)], full_rows=True, a2=dup, b2=full["w_up"], name="d_h2_norm_ffn_bwd")
    grads["w_o"] = _mm(mix, dx1, ta=True, out_dtype=BF16, name="d_w_o")
    token = scatter_start(0)
    dmix = _mm(dx1, full["w_o"], tb=True, name="d_mix", after=token)

    half = MLA_HEADS * V_HEAD
    do, dhg, dg_hgo = _stage(_post_bwd_fn, [o_f, o_b, hg, _cols(dmix, half, 1)], [g_hgo_row],
                             [_sds((T, half), F32), _sds((T, half), BF16)], [_sds((1, half), F32)], name="mix_out_bwd")
    grads["g_hgo"] = dg_hgo
    dhq_f, dhi_f, dhf_f, dlow_f = _gla_bwd(z, lower3[0], st_f, do, None, rev=False,
                                           col_q=_Z_HQ, col_f=_Z_HFF, col_v=_Z_HI, hp=GLA_BWD_HEADS)
    dhq, dhi, dhf_b, dlow_b = _gla_bwd(z, lower3[1], st_b, do, (dhq_f, dhi_f), rev=True,
                                       col_q=_Z_HQ, col_f=_Z_HFB, col_v=_Z_HI, hp=GLA_BWD_HEADS)

    dq, dk, dv = _attention_bwd(q, k, v, att, lse, dmix)
    dq_raw, dkv_raw, dkr, grads["g_qn"], grads["g_kn"] = _stage(
        _mla_b_bwd_fn, [q_raw, kv_raw, kr, cos, sin, dq, dk, dv], [g_qn, g_kn],
        [_sds(q_raw.shape, BF16), _sds(kv_raw.shape, BF16), _sds((T, LANE), BF16)],
        [_sds(g_qn.shape, F32), _sds(g_kn.shape, F32)], name="mla_qk_norm_rope_bwd")
    grads["w_qb"] = _mm(dq_raw, qn, ta=True, out_dtype=BF16, name="d_w_qb")
    grads["w_kvb"] = _mm(kvn, dkv_raw, ta=True, out_dtype=BF16, name="d_w_kvb")
    dqn = _mm(dq_raw, full["w_qb"], name="d_qn")
    dkvn = _mm(dkv_raw, full["w_kvb"], tb=True, name="d_kvn")
    dcq, dckv, grads["g_qa"], grads["g_kva"] = _stage(
        _mla_a_bwd_fn, [_cols(z, 256, 0), _cols(z, 256, 1), dqn, dkvn], [g_qa, g_kva],
        [_sds((T, 256), BF16), _sds((T, 256), BF16)], [_sds(g_qa.shape, F32), _sds(g_kva.shape, F32)],
        name="mla_latent_norm_bwd")
    token = reduce_group(0, dcq)
    dz = jnp.concatenate([dcq, dckv, dhq, dhf_f, dhf_b, dhi, dhg, dkr], axis=1)
    grads["w_in"] = _from_z_layout(_mm(dz, h1, ta=True, out_dtype=BF16, name="d_w_in", after=token))
    token = scatter_start(1)
    grad_x, grads["g_mix"] = _mm_fused(
        dz, w_in_zt, lambda acc, x, dx1, g: _norm_bwd_fn(x, acc, dx1, g), [x2d, dx1], [g_mix],
        [F32], [_sds((1, D), F32)], full_rows=True, name="d_h1_norm_mix_bwd", after=token)

    out_g, out_d, out_m, out_v = {}, {}, {}, {}

    def update_group(gi, after):
        mine, theirs = _exchange_wait(swap_started[gi], after, name=f"swap_wait_{gi}")
        tokens = []
        for n, a, b in zip(scatter_groups[gi], mine, theirs):
            *res, token = _adamw(view(n, w_named[n]), a, b, view(n, m_named[n]), view(n, v_named[n]), name="adamw_" + n)
            out_g[n], out_d[n], out_m[n], out_v[n] = (unview(n, t) for t in res)
            tokens.append(token)
        return tokens

    done = update_group(0, grads["g_mix"])
    token = reduce_group(1, done)
    done = update_group(1, token)

    small = ("g_mix", "g_qa", "g_kva", "g_qn", "g_kn", "g_hgo", "g_ffn", "g_ple")
    small_all = small + ("lb_param",)
    width = -(-max(w_named[n].size for n in small_all) // LANE) * LANE

    def row(a):
        a = a.reshape(1, -1)
        return jnp.pad(a, ((0, 0), (0, width - a.shape[1])))

    dlower = jnp.concatenate([dlow_f.reshape(1, -1), dlow_b.reshape(1, -1)], axis=0)
    pack = jnp.concatenate([row(grads[n]) for n in small] + [row(dlower[0]), row(dlower[1]), row(loss_part)]
                           + [jnp.zeros((5, width), F32)], axis=0)
    red = _allreduce_small(pack, done[-1])
    loss = red[10, 0]

    outs = _adamw_small(red, lb_full, [w_named[n] for n in small_all], [m_named[n] for n in small_all],
                        [v_named[n] for n in small_all])
    for i, n in enumerate(small_all):
        out_g[n], out_d[n], out_m[n], out_v[n] = outs[4 * i:4 * i + 4]

    return (loss, grad_x[None], *[out_g[n] for n in order], *[out_d[n] for n in order],
            *[out_m[n] for n in order], *[out_v[n] for n in order])
```

```python
import functools
import itertools

import jax
import jax.numpy as jnp
from jax import lax
from jax.experimental import pallas as pl
from jax.experimental.pallas import tpu as pltpu

F32 = jnp.float32
BF16 = jnp.bfloat16
MESH = pl.DeviceIdType.MESH

EPS = 1e-6
ROPE_THETA = 10000.0
MLA_HEADS = 4
QK_NOPE = 128
QK_ROPE = 64
QK_HEAD = QK_NOPE + QK_ROPE
V_HEAD = 128
HG_HEADS = 4
HG_DK = 128
CHUNK = 64
ADAM_LR = 0.001
ADAM_B1 = 0.9
ADAM_B2 = 0.999
ADAM_EPS = 1e-08
ADAM_WD = 0.01
ADAM_STEP = 10

LANE = 128
VMEM_LIMIT = 56 * 1024 * 1024
TOK_TILE = 512
GLA_GROUP = 16
GLA_FWD_HEADS = 4
GLA_BWD_HEADS = 2
ATT_TQ = 1024
ATT_TK = 1024
ATT_CHUNK = 1024
LOG2_E = 1.4426950408889634
N_CHIPS = 4
N_DEV = 8


_ANY = pl.BlockSpec(memory_space=pl.ANY)


def _params(dims=None, **kw):
    return pltpu.CompilerParams(dimension_semantics=dims, vmem_limit_bytes=VMEM_LIMIT, **kw)


def _tile_candidates(n, cap):
    out = [d for d in range(LANE, min(n, cap) + 1, LANE) if n % d == 0]
    if n <= cap and n not in out:
        out.append(n)
    return out or [n]


MM_VMEM_BUDGET = 40 * 1024 * 1024
MM_MIN_ROWS = 256
MM_MAX_ROWS = 1536
HBM_BYTES_PER_S = 2.8e12
MXU_FLOPS_PER_S = 8e14
STEP_OVERHEAD_S = 0.35e-6


def _mm_tiles(M, N, K, a_bytes, b_bytes, o_bytes, has_add, full_rows=False, full_k=False, n_a=1, n_b=1):
    cast_a, cast_b = a_bytes > 2, b_bytes > 2
    a_bytes, b_bytes = n_a * a_bytes, n_b * b_bytes
    best = None
    for tm in [t for t in _tile_candidates(M, MM_MAX_ROWS) if t >= min(M, MM_MIN_ROWS)]:
        for tn in ([N] if full_rows else _tile_candidates(N, N)):
            for tk in ([K] if full_k else _tile_candidates(K, K)):
                ni, nj, nk = M // tm, N // tn, K // tk
                vmem = 2 * (tm * tk * a_bytes + tk * tn * b_bytes + tm * tn * o_bytes * (2 if has_add else 1))
                vmem += tm * tn * 4 * (2 if nk > 1 else 1)
                vmem += (tm * tk * 2 * n_a if cast_a else 0) + (tk * tn * 2 * n_b if cast_b else 0)
                if vmem > MM_VMEM_BUDGET:
                    continue
                moved = M * K * a_bytes * (nj if nk > 1 else 1) + K * N * b_bytes * (1 if nj == nk == 1 else ni)
                moved += M * N * o_bytes * (2 if has_add else 1)
                t = max(moved / HBM_BYTES_PER_S, 2 * M * N * K / MXU_FLOPS_PER_S) + ni * nj * nk * STEP_OVERHEAD_S
                if best is None or t < best[0]:
                    best = (t, tm, tn, tk)
    assert best is not None, (M, N, K)
    return best[1:]


def _dot_raw(a, b, kind):
    nb = a.ndim - 2
    batch = ((0,), (0,)) if nb else ((), ())
    ca = nb if kind == "tn" else nb + 1
    cb = nb + 1 if kind == "nt" else nb
    return lax.dot_general(a.astype(BF16), b.astype(BF16), (((ca,), (cb,)), batch), preferred_element_type=F32)


@functools.partial(jax.custom_vjp, nondiff_argnums=(2,))
def _bdot(a, b, kind):
    return _dot_raw(a, b, kind)


def _bdot_fwd(a, b, kind):
    return _dot_raw(a, b, kind), (a, b)


def _bdot_bwd(kind, res, g):
    a, b = res
    if kind == "nn":
        da, db = _bdot(g, b, "nt"), _bdot(a, g, "tn")
    elif kind == "nt":
        da, db = _bdot(g, b, "nn"), _bdot(g, a, "tn")
    else:
        da, db = _bdot(b, g, "nt"), _bdot(a, g, "nn")
    return da.astype(a.dtype), db.astype(b.dtype)


_bdot.defvjp(_bdot_fwd, _bdot_bwd)


def _mm(a, b, *, name, ta=False, tb=False, add=None, out_dtype=F32, after=None):
    K, M = a.shape if ta else a.shape[::-1]
    N, Kb = b.shape if tb else b.shape[::-1]
    assert K == Kb, (a.shape, b.shape, ta, tb)
    tm, tn, tk = _mm_tiles(M, N, K, a.dtype.itemsize, b.dtype.itemsize, jnp.dtype(out_dtype).itemsize, add is not None)
    nk = K // tk
    kind = "tn" if ta else ("nt" if tb else "nn")
    assert not (ta and tb)
    a_spec = pl.BlockSpec((tk, tm), lambda i, j, k: (k, i)) if ta else pl.BlockSpec((tm, tk), lambda i, j, k: (i, k))
    b_spec = pl.BlockSpec((tn, tk), lambda i, j, k: (j, k)) if tb else pl.BlockSpec((tk, tn), lambda i, j, k: (k, j))
    o_spec = pl.BlockSpec((tm, tn), lambda i, j, k: (i, j))
    has_add = add is not None

    def body(*refs):
        a_ref, b_ref = refs[0], refs[1]
        add_ref = refs[2] if has_add else None
        o_ref = refs[n_in]
        part = _dot_raw(a_ref[...], b_ref[...], kind)
        if nk == 1:
            if has_add:
                part = part + add_ref[...].astype(F32)
            o_ref[...] = part.astype(o_ref.dtype)
            return
        acc_ref = refs[-1]
        k = pl.program_id(2)

        @pl.when(k == 0)
        def _():
            acc_ref[...] = part

        @pl.when(k > 0)
        def _():
            acc_ref[...] += part

        @pl.when(k == nk - 1)
        def _():
            r = acc_ref[...]
            if has_add:
                r = r + add_ref[...].astype(F32)
            o_ref[...] = r.astype(o_ref.dtype)

    ins = [a, b] + ([add] if has_add else []) + ([after] if after is not None else [])
    in_specs = [a_spec, b_spec] + ([o_spec] if has_add else []) + ([_ANY] if after is not None else [])
    n_in = len(ins)
    return pl.pallas_call(
        body, name=name, grid=(M // tm, N // tn, nk), in_specs=in_specs, out_specs=o_spec,
        out_shape=jax.ShapeDtypeStruct((M, N), out_dtype),
        scratch_shapes=[pltpu.VMEM((tm, tn), F32)] if nk > 1 else [],
        compiler_params=_params(("parallel", "parallel", "arbitrary")),
    )(*ins)


def _mm_fused(a, b, fn, tiles, params, out_dtypes, sums, *, name, ta=False, tb=False, full_rows=False, after=None,
              b2=None, a2=None):
    K, M = a.shape if ta else a.shape[::-1]
    N, Kb = b.shape if tb else b.shape[::-1]
    assert K == Kb and not (ta and tb), (a.shape, b.shape, ta, tb)
    per_elem = sum(t.dtype.itemsize for t in tiles) + sum(jnp.dtype(d).itemsize for d in out_dtypes)
    n_b = 1 if b2 is None else 2
    n_a = 1 if a2 is None else 2
    tm, tn, tk = _mm_tiles(M, N, K, a.dtype.itemsize, b.dtype.itemsize, per_elem, False, full_rows, b2 is not None,
                           n_a, n_b)
    nk = K // tk
    kind = "tn" if ta else ("nt" if tb else "nn")
    a_spec = pl.BlockSpec((tk, tm), lambda i, j, k: (k, i)) if ta else pl.BlockSpec((tm, tk), lambda i, j, k: (i, k))
    b_spec = pl.BlockSpec((tn, tk), lambda i, j, k: (j, k)) if tb else pl.BlockSpec((tk, tn), lambda i, j, k: (k, j))
    o_spec = pl.BlockSpec((tm, tn), lambda i, j, k: (i, j))
    ins = [a, b] + ([b2] if b2 is not None else []) + ([a2] if a2 is not None else [])
    ins += list(tiles) + list(params) + ([after] if after is not None else [])
    in_specs = [a_spec] + [b_spec] * n_b + [a_spec] * (n_a - 1) + [o_spec] * len(tiles)
    in_specs += [pl.BlockSpec(p.shape, lambda i, j, k, nd=p.ndim: (0,) * nd) for p in params]
    in_specs += [_ANY] if after is not None else []
    n_in, n_t, n_p, n_o = len(ins), len(tiles), len(params), len(out_dtypes)

    def body(*refs):
        outs, sum_refs = refs[n_in:n_in + n_o], refs[n_in + n_o:n_in + n_o + len(sums)]

        def finish(*products):
            res = fn(*products, *[t[...] for t in refs[n_a + n_b:n_a + n_b + n_t + n_p]])
            for o_ref, v in zip(outs, res[:n_o]):
                o_ref[...] = v.astype(o_ref.dtype)
            first = jnp.logical_and(pl.program_id(0) == 0, pl.program_id(1) == 0)
            for s_ref, v in zip(sum_refs, res[n_o:]):
                @pl.when(first)
                def _(s_ref=s_ref, v=v):
                    s_ref[...] = v

                @pl.when(jnp.logical_not(first))
                def _(s_ref=s_ref, v=v):
                    s_ref[...] += v

        part = _dot_raw(refs[0][...], refs[1][...], kind)
        if nk == 1 and a2 is not None:
            finish(part + _dot_raw(refs[3][...], refs[2][...], kind))
            return
        if nk == 1:
            finish(part, *([_dot_raw(refs[0][...], refs[2][...], kind)] if b2 is not None else []))
            return
        acc_ref = refs[-1]
        k = pl.program_id(2)

        @pl.when(k == 0)
        def _():
            acc_ref[...] = part

        @pl.when(k > 0)
        def _():
            acc_ref[...] += part

        @pl.when(k == nk - 1)
        def _():
            finish(acc_ref[...])

    out_shape = [_sds((M, N), d) for d in out_dtypes] + list(sums)
    out_specs = [o_spec] * n_o + [pl.BlockSpec(s.shape, lambda i, j, k, nd=len(s.shape): (0,) * nd) for s in sums]
    order = ("arbitrary",) * 3 if sums else ("parallel", "parallel", "arbitrary")
    return pl.pallas_call(
        body, name=name, grid=(M // tm, N // tn, nk), in_specs=in_specs, out_specs=out_specs, out_shape=out_shape,
        scratch_shapes=[pltpu.VMEM((tm, tn), F32)] if nk > 1 else [], compiler_params=_params(order),
    )(*ins)


def _cols(arr, width, block):
    return (arr, width, block)


def _stage(fn, tiles, params, out_tiles, out_sums, *, name, tile=TOK_TILE, after=None):
    def tok_spec(shape, width=None, block=0):
        if len(shape) == 2:
            w = shape[1] if width is None else width
            return pl.BlockSpec((tile, w), lambda i: (i, block))
        return pl.BlockSpec((shape[0], tile, shape[2]), lambda i: (0, i, 0))

    arrays, in_specs = [], []
    for t in tiles:
        if isinstance(t, tuple):
            arr, width, block = t
            arrays.append(arr)
            in_specs.append(tok_spec(arr.shape, width, block))
        else:
            arrays.append(t)
            in_specs.append(tok_spec(t.shape))
    n_tok = arrays[0].shape[0] if arrays[0].ndim == 2 else arrays[0].shape[1]
    for p in params:
        arrays.append(p)
        in_specs.append(pl.BlockSpec(p.shape, lambda i, nd=p.ndim: (0,) * nd))
    out_shape = list(out_tiles) + list(out_sums)
    out_specs = [tok_spec(o.shape) for o in out_tiles]
    out_specs += [pl.BlockSpec(o.shape, lambda i, nd=len(o.shape): (0,) * nd) for o in out_sums]
    n_fn, n_ot = len(arrays), len(out_tiles)
    if after is not None:
        arrays.append(after)
        in_specs.append(_ANY)
    n_in = len(arrays)

    def body(*refs):
        res = fn(*[r[...] for r in refs[:n_fn]])
        if not isinstance(res, (tuple, list)):
            res = (res,)
        outs = refs[n_in:]
        for o_ref, r in zip(outs[:n_ot], res[:n_ot]):
            o_ref[...] = r.astype(o_ref.dtype)
        i = pl.program_id(0)
        for o_ref, r in zip(outs[n_ot:], res[n_ot:]):
            @pl.when(i == 0)
            def _(o_ref=o_ref, r=r):
                o_ref[...] = r.astype(o_ref.dtype)

            @pl.when(i > 0)
            def _(o_ref=o_ref, r=r):
                o_ref[...] += r.astype(o_ref.dtype)

    res = pl.pallas_call(
        body, name=name, grid=(n_tok // tile,), in_specs=in_specs, out_specs=out_specs, out_shape=out_shape,
        compiler_params=_params(("arbitrary",)),
    )(*arrays)
    return res


def _sds(shape, dtype):
    return jax.ShapeDtypeStruct(tuple(shape), dtype)


def _sigmoid(x):
    return 0.5 * jnp.tanh(0.5 * x) + 0.5


def _rms(x, g):
    return x * lax.rsqrt(jnp.mean(x * x, axis=-1, keepdims=True) + EPS) * g


def _norm_bwd_fn(x, dh, dres, g):
    r = lax.rsqrt(jnp.mean(x * x, axis=-1, keepdims=True) + EPS)
    xr = x * r
    dhg = dh * g
    dx = r * (dhg - xr * jnp.mean(xr * dhg, axis=-1, keepdims=True))
    return dx + dres, jnp.sum(dh * xr, axis=0, keepdims=True)


def _mla_a_fn(cq, ckv, g_qa, g_kva):
    return _rms(cq, g_qa), _rms(ckv, g_kva)


def _mla_a_bwd_fn(cq, ckv, dqn, dkvn, g_qa, g_kva):
    _, vjp = jax.vjp(_mla_a_fn, cq, ckv, g_qa, g_kva)
    return vjp((dqn, dkvn))


def _rope(t, cos, sin):
    t1, t2 = t[:, :QK_ROPE // 2], t[:, QK_ROPE // 2:]
    return jnp.concatenate([t1 * cos - t2 * sin, t1 * sin + t2 * cos], axis=-1)


def _mla_b_fn(q_raw, kv_raw, kr, cos, sin, g_qn, g_kn):
    krope = kr[:, :QK_ROPE]
    qs, ks, vs = [], [], []
    for h in range(MLA_HEADS):
        qh = _rms(q_raw[:, h * QK_HEAD:(h + 1) * QK_HEAD], g_qn)
        kvh = kv_raw[:, h * (QK_NOPE + V_HEAD):(h + 1) * (QK_NOPE + V_HEAD)]
        kh = _rms(jnp.concatenate([kvh[:, :QK_NOPE], krope], axis=-1), g_kn)
        qs.append(jnp.concatenate([qh[:, :QK_NOPE], _rope(qh[:, QK_NOPE:], cos, sin)], axis=-1))
        ks.append(jnp.concatenate([kh[:, :QK_NOPE], _rope(kh[:, QK_NOPE:], cos, sin)], axis=-1))
        vs.append(kvh[:, QK_NOPE:])
    return jnp.stack(qs), jnp.stack(ks), jnp.stack(vs)


def _mla_b_bwd_fn(q_raw, kv_raw, kr, cos, sin, dq, dk, dv, g_qn, g_kn):
    _, vjp = jax.vjp(lambda a, b, c, d, e: _mla_b_fn(a, b, c, cos, sin, d, e), q_raw, kv_raw, kr, g_qn, g_kn)
    return vjp((dq, dk, dv))


def _post_fn(a, o_f, o_b, hg, g_hgo):
    o = o_f + o_b
    parts = [a]
    for h in range(HG_HEADS):
        s = slice(h * HG_DK, (h + 1) * HG_DK)
        gate = hg[:, s]
        parts.append(_rms(o[:, s], g_hgo[:, s]) * (gate * _sigmoid(gate)))
    return jnp.concatenate(parts, axis=-1)


def _post_bwd_fn(o_f, o_b, hg, dr, g_hgo):
    def f(o, hg, g):
        return _post_fn(jnp.zeros_like(o), o, jnp.zeros_like(o), hg, g)[:, o.shape[1]:]
    _, vjp = jax.vjp(f, o_f + o_b, hg, g_hgo)
    return vjp(dr)


def _swiglu_fn(gt, up):
    return gt * _sigmoid(gt) * up


def _resid_norm_fn(acc, x, g):
    x_new = acc + x
    return x_new, _rms(x_new, g)


def _swiglu_bwd_fn(gt, up, dact):
    s = _sigmoid(gt)
    silu = gt * s
    return dact * up * (s + silu * (1.0 - s)), dact * silu


def _ple_loss_fn(x2, pg, pp, target):
    gate = _sigmoid(pg)
    err = x2 + gate * pp - target
    dx3 = err * (1.0 / err.shape[-1])
    loss = 0.5 * jnp.sum(jnp.mean(err * err, axis=-1, keepdims=True), axis=0, keepdims=True)
    return dx3, dx3 * pp * gate * (1.0 - gate), dx3 * gate, loss


def _attention_fwd(q, k, v):
    H, T, D = q.shape
    DV = v.shape[-1]
    tq, ck = min(ATT_TQ, T), min(ATT_CHUNK, T)
    c2 = (D ** -0.5) * LOG2_E

    def body(q_ref, k_ref, v_ref, o_ref, lse_ref):
        q_i = q_ref[0]

        def chunk(c, carry):
            m, l, acc = carry
            rows = pl.ds(pl.multiple_of(c * ck, ck), ck)
            s = _dot_raw(q_i, k_ref[0, rows, :], "nt")
            m_new = jnp.maximum(m, jnp.max(s, axis=-1, keepdims=True))
            p = jnp.exp2((s - m_new) * c2)
            alpha = jnp.exp2((m - m_new) * c2)
            l = l * alpha + jnp.sum(p, axis=-1, keepdims=True)
            acc = acc * alpha + _dot_raw(p, v_ref[0, rows, :], "nn")
            return m_new, l, acc

        init = (jnp.full((tq, 1), -jnp.inf, F32), jnp.zeros((tq, 1), F32), jnp.zeros((tq, DV), F32))
        m, l, acc = lax.fori_loop(0, T // ck, chunk, init, unroll=True)
        o_ref[...] = acc / l
        lse_ref[0] = m * c2 + jnp.log2(l)

    return pl.pallas_call(
        body, name="attention_fwd", grid=(H, T // tq),
        in_specs=[pl.BlockSpec((1, tq, D), lambda h, i: (h, i, 0)),
                  pl.BlockSpec((1, T, D), lambda h, i: (h, 0, 0)),
                  pl.BlockSpec((1, T, DV), lambda h, i: (h, 0, 0))],
        out_specs=[pl.BlockSpec((tq, DV), lambda h, i: (i, h)),
                   pl.BlockSpec((1, tq, 1), lambda h, i: (h, i, 0))],
        out_shape=[_sds((T, H * DV), F32), _sds((H, T, 1), F32)],
        compiler_params=_params(("parallel", "parallel")),
    )(q, k, v)


def _attention_bwd(q, k, v, o, lse2, dmix):
    H, T, D = q.shape
    DV = v.shape[-1]
    tk, cq = min(ATT_TK, T), min(ATT_CHUNK, T)
    scale = D ** -0.5
    c2 = scale * LOG2_E

    def body(q_ref, k_ref, v_ref, o_ref, lse_ref, do_ref, dq_ref, dk_ref, dv_ref, delta_ref):
        j = pl.program_id(1)

        @pl.when(j == 0)
        def _():
            delta = lax.dot_general(jnp.ones((8, DV), F32), do_ref[...] * o_ref[...], (((1,), (1,)), ((), ())),
                                    precision=lax.Precision.HIGHEST, preferred_element_type=F32)
            for i in range(T // cq):
                delta_ref[i] = delta[:, i * cq:(i + 1) * cq]
            dq_ref[0] = jnp.zeros((T, D), F32)

        k_j, v_j = k_ref[0], v_ref[0]
        dk_ref[0] = jnp.zeros((tk, D), F32)
        dv_ref[0] = jnp.zeros((tk, DV), F32)

        def chunk(c, carry):
            rows = pl.ds(pl.multiple_of(c * cq, cq), cq)
            q_c = q_ref[0, rows, :]
            do_c = do_ref[rows, :].astype(BF16)
            st = _dot_raw(k_j, q_c, "nt")
            pt = jnp.exp2(st * c2 - lse_ref[0, c])
            dv_ref[0] += _dot_raw(pt, do_c, "nn")
            dpt = _dot_raw(v_j, do_c, "nt")
            dst = pt * (dpt - delta_ref[c, 0:1, :]) * scale
            dk_ref[0] += _dot_raw(dst, q_c, "nn")
            dq_ref[0, rows, :] += _dot_raw(dst, k_j, "tn")
            return carry

        lax.fori_loop(0, T // cq, chunk, 0, unroll=True)

    return pl.pallas_call(
        body, name="attention_bwd", grid=(H, T // tk),
        in_specs=[pl.BlockSpec((1, T, D), lambda h, j: (h, 0, 0)),
                  pl.BlockSpec((1, tk, D), lambda h, j: (h, j, 0)),
                  pl.BlockSpec((1, tk, DV), lambda h, j: (h, j, 0)),
                  pl.BlockSpec((T, DV), lambda h, j: (0, h)),
                  pl.BlockSpec((1, T // cq, 1, cq), lambda h, j: (h, 0, 0, 0)),
                  pl.BlockSpec((T, DV), lambda h, j: (0, h))],
        out_specs=[pl.BlockSpec((1, T, D), lambda h, j: (h, 0, 0)),
                   pl.BlockSpec((1, tk, D), lambda h, j: (h, j, 0)),
                   pl.BlockSpec((1, tk, DV), lambda h, j: (h, j, 0))],
        out_shape=[_sds((H, T, D), F32), _sds((H, T, D), F32), _sds((H, T, DV), F32)],
        scratch_shapes=[pltpu.VMEM((T // cq, 8, cq), F32)],
        compiler_params=_params(("parallel", "arbitrary")),
    )(q, k, v, o, lse2.reshape(H, T // cq, 1, cq), dmix)


def _split3_dot(ones, x, kind):
    hi = x.astype(BF16)
    rest = x - hi.astype(F32)
    mid = rest.astype(BF16)
    lo = (rest - mid.astype(F32)).astype(BF16)
    return (_dot_raw(ones, hi, kind) + _dot_raw(ones, mid, kind)) + _dot_raw(ones, lo, kind)


@jax.custom_vjp
def _running_sum(x, tri):
    return _split3_dot(tri, x, "nn")


def _running_sum_fwd(x, tri):
    return _split3_dot(tri, x, "nn"), tri


def _running_sum_bwd(tri, g):
    return _split3_dot(tri, g, "tn"), jnp.zeros_like(tri)


_running_sum.defvjp(_running_sum_fwd, _running_sum_bwd)


def _gla_block(hq, hf, hi, lower, st_in, *, rev, dot):
    rows, dk = hq.shape
    G, C = rows // CHUNK, CHUNK
    q = hq * _sigmoid(hq)
    f = lower + (1.0 - lower) * _sigmoid(hf)
    k = 1.0 - f
    logf = jnp.log2(f)
    q3, k3, v3, lf3 = (t.reshape(G, C, dk) for t in (q, k, hi, logf))
    r = lax.broadcasted_iota(jnp.int32, (C, C), 0)
    c = lax.broadcasted_iota(jnp.int32, (C, C), 1)
    tri = ((r <= c) if rev else (r >= c)).astype(F32)
    b = _running_sum(lf3, jnp.broadcast_to(tri, (G, C, C)))
    tpos = lax.broadcasted_iota(jnp.int32, (1, C, 1), 1)
    first_half = (tpos >= C // 2) if rev else (tpos <= C // 2 - 1)
    b_mid = jnp.sum(jnp.where(first_half, lf3, 0.0), axis=1, keepdims=True)
    b_last = jnp.sum(lf3, axis=1, keepdims=True)
    a = dot(q3 * jnp.exp2(b - b_mid), k3 * jnp.exp2(b_mid - b), "nt") * tri
    o_intra = dot(a, v3, "nn")
    kv_t = dot(v3, k3 * jnp.exp2(b_last - b), "tn")
    decay = jnp.exp2(b_last)
    qd = q3 * jnp.exp2(b)
    st = st_in
    o_inter = [None] * G
    for g in (reversed(range(G)) if rev else range(G)):
        o_inter[g] = dot(qd[g], st, "nt")
        st = st * decay[g] + kv_t[g]
    o = o_intra.reshape(rows, dk) + jnp.concatenate(o_inter, axis=0)
    return o, st


def _gla_fwd(z, lower3, *, rev, col_q, col_f, col_v, hp):
    T = z.shape[0]
    rows = min(GLA_GROUP * CHUNK, T)
    nb = T // rows
    wide = hp * HG_DK
    blk = (lambda n: nb - 1 - n) if rev else (lambda n: n)

    def body(hq_ref, hf_ref, hi_ref, low_ref, o_ref, st_out_ref, st_ref):
        @pl.when(pl.program_id(1) == 0)
        def _():
            st_ref[...] = jnp.zeros_like(st_ref)

        st_in = [st_ref[i] for i in range(hp)]
        heads = []
        for i in range(hp):
            cols = slice(i * HG_DK, (i + 1) * HG_DK)
            heads.append(_gla_block(hq_ref[:, cols], hf_ref[:, cols], hi_ref[:, cols], low_ref[i], st_in[i], rev=rev,
                                    dot=_dot_raw))
        for i, (o, st) in enumerate(heads):
            st_out_ref[i, 0] = st_in[i]
            o_ref[:, i * HG_DK:(i + 1) * HG_DK] = o
            st_ref[i] = st

    def zspec(col):
        return pl.BlockSpec((rows, wide), lambda h, n: (blk(n), col // wide + h))

    return pl.pallas_call(
        body, name="gla_fwd_rev" if rev else "gla_fwd", grid=(HG_HEADS // hp, nb),
        in_specs=[zspec(col_q), zspec(col_f), zspec(col_v), pl.BlockSpec((hp, 1, HG_DK), lambda h, n: (h, 0, 0))],
        out_specs=[pl.BlockSpec((rows, wide), lambda h, n: (blk(n), h)),
                   pl.BlockSpec((hp, 1, HG_DK, HG_DK), lambda h, n: (h, blk(n), 0, 0))],
        out_shape=[_sds((T, HG_HEADS * HG_DK), F32), _sds((HG_HEADS, nb, HG_DK, HG_DK), F32)],
        scratch_shapes=[pltpu.VMEM((hp, HG_DK, HG_DK), F32)],
        compiler_params=_params(("parallel", "arbitrary")),
    )(z, z, z, lower3)


def _gla_bwd(z, lower3, states, do, prev, *, rev, col_q, col_f, col_v, hp):
    T = z.shape[0]
    rows = min(GLA_GROUP * CHUNK, T)
    nb = T // rows
    wide = hp * HG_DK
    blk = (lambda n: n) if rev else (lambda n: nb - 1 - n)
    has_prev = prev is not None
    fn = functools.partial(_gla_block, rev=rev, dot=_bdot)

    def body(*refs):
        hq_ref, hf_ref, hi_ref, low_ref, st_ref, do_ref = refs[:6]
        rest = refs[6:]
        if has_prev:
            pq_ref, pi_ref = rest[:2]
            rest = rest[2:]
        dhq_ref, dhi_ref, dhf_ref, dlow_ref, dst_ref = rest
        n = pl.program_id(1)

        @pl.when(n == 0)
        def _():
            dst_ref[...] = jnp.zeros_like(dst_ref)

        dst_in = [dst_ref[i] for i in range(hp)]
        heads = []
        for i in range(hp):
            cols = slice(i * HG_DK, (i + 1) * HG_DK)
            _, vjp = jax.vjp(fn, hq_ref[:, cols], hf_ref[:, cols], hi_ref[:, cols], low_ref[i], st_ref[i, 0])
            dhq, dhf, dhi, dlow, dst = vjp((do_ref[:, cols], dst_in[i]))
            if has_prev:
                dhq = dhq + pq_ref[:, cols]
                dhi = dhi + pi_ref[:, cols]
            heads.append((dhq, dhf, dhi, dlow, dst))
        for i, (dhq, dhf, dhi, dlow, dst) in enumerate(heads):
            cols = slice(i * HG_DK, (i + 1) * HG_DK)
            dst_ref[i] = dst
            dhq_ref[:, cols] = dhq.astype(dhq_ref.dtype)
            dhi_ref[:, cols] = dhi.astype(dhi_ref.dtype)
            dhf_ref[:, cols] = dhf.astype(dhf_ref.dtype)

        @pl.when(n == 0)
        def _():
            for i in range(hp):
                dlow_ref[i] = heads[i][3]

        @pl.when(n > 0)
        def _():
            for i in range(hp):
                dlow_ref[i] += heads[i][3]

    def zspec(col):
        return pl.BlockSpec((rows, wide), lambda h, n: (blk(n), col // wide + h))

    hspec = pl.BlockSpec((rows, wide), lambda h, n: (blk(n), h))
    in_specs = [zspec(col_q), zspec(col_f), zspec(col_v), pl.BlockSpec((hp, 1, HG_DK), lambda h, n: (h, 0, 0)),
                pl.BlockSpec((hp, 1, HG_DK, HG_DK), lambda h, n: (h, blk(n), 0, 0)), hspec]
    ins = [z, z, z, lower3, states, do]
    if has_prev:
        in_specs += [hspec, hspec]
        ins += list(prev)
    full_wide = HG_HEADS * HG_DK
    acc_dtype = BF16 if has_prev else F32
    return pl.pallas_call(
        body, name="gla_bwd_rev" if rev else "gla_bwd", grid=(HG_HEADS // hp, nb),
        in_specs=in_specs,
        out_specs=[hspec, hspec, hspec, pl.BlockSpec((hp, 1, HG_DK), lambda h, n: (h, 0, 0))],
        out_shape=[_sds((T, full_wide), acc_dtype), _sds((T, full_wide), acc_dtype), _sds((T, full_wide), BF16),
                   _sds((HG_HEADS, 1, HG_DK), F32)],
        scratch_shapes=[pltpu.VMEM((hp, HG_DK, HG_DK), F32)],
        compiler_params=_params(("parallel", "arbitrary")),
    )(*ins)


def _lower_fn(lb):
    e = jnp.exp(lb - jnp.max(lb, axis=0, keepdims=True))
    return (e / jnp.sum(e, axis=0, keepdims=True))[0]


def _lower_bounds(lb):
    def body(lb_ref, o_ref):
        o_ref[...] = _lower_fn(lb_ref[...])
    return pl.pallas_call(body, name="lower_bounds", out_shape=_sds(lb.shape[1:], F32))(lb)


def _row_tile(r, cap=1024):
    best = None
    for t in range(16, min(r, cap) + 1, 16):
        if r % t == 0:
            best = t
    return best if best is not None else r


def _sum4(shards, recv, chip, *, name):
    _, R, C = shards.shape
    tr = _row_tile(R)

    def body(chip_ref, o_ref, r_ref, out_ref):
        out_ref[...] = ((o_ref[0].astype(F32) + r_ref[0].astype(F32)) + r_ref[1].astype(F32)) + r_ref[2].astype(F32)

    grid_spec = pltpu.PrefetchScalarGridSpec(
        num_scalar_prefetch=1, grid=(R // tr,),
        in_specs=[pl.BlockSpec((1, tr, C), lambda i, chip_ref: (chip_ref[0], i, 0)),
                  pl.BlockSpec((3, tr, C), lambda i, chip_ref: (0, i, 0))],
        out_specs=pl.BlockSpec((tr, C), lambda i, chip_ref: (i, 0)))
    return pl.pallas_call(
        body, name=name, grid_spec=grid_spec, out_shape=_sds((R, C), F32), compiler_params=_params(("parallel",)),
    )(chip, shards, recv)


def _adamw_math(w, g, m, v):
    m = ADAM_B1 * m + (1.0 - ADAM_B1) * g
    v = ADAM_B2 * v + (1.0 - ADAM_B2) * (g * g)
    m_hat = m / (1.0 - ADAM_B1 ** ADAM_STEP)
    v_hat = v / (1.0 - ADAM_B2 ** ADAM_STEP)
    delta = -ADAM_LR * (m_hat / (jnp.sqrt(v_hat) + ADAM_EPS) + ADAM_WD * w)
    return delta, m, v


def _adamw(w, g_a, g_b, m, v, *, name):
    R, C = w.shape
    tr = _row_tile(R)
    two = g_b is not None

    def body(*refs):
        w_ref, ga_ref = refs[0], refs[1]
        rest = refs[2:]
        g = ga_ref[...]
        if two:
            g = g + rest[0][...]
            rest = rest[1:]
        m_ref, v_ref, g_out, d_out, m_out, v_out, token = rest
        delta, m_new, v_new = _adamw_math(w_ref[...], g, m_ref[...], v_ref[...])
        g_out[...] = g
        d_out[...] = delta
        m_out[...] = m_new
        v_out[...] = v_new
        token[...] = jnp.zeros_like(token)

    spec = pl.BlockSpec((tr, C), lambda i: (i, 0))
    ins = [w, g_a] + ([g_b] if two else []) + [m, v]
    return pl.pallas_call(
        body, name=name, grid=(R // tr,), in_specs=[spec] * len(ins),
        out_specs=[spec] * 4 + [pl.BlockSpec((8, LANE), lambda i: (0, 0))],
        out_shape=[_sds((R, C), F32)] * 4 + [_sds((8, LANE), F32)], compiler_params=_params(("arbitrary",)),
    )(*ins)


def _adamw_small(red, lb_full, ws, ms, vs):
    n = len(ws)

    def pieces(shape):
        out = []
        for j, idx in enumerate(itertools.product(*[range(d) for d in shape[:-1]])):
            out.append((idx[:-1] + (slice(idx[-1], idx[-1] + 1), slice(None)), j * shape[-1]))
        return out

    def body(*refs):
        red_ref, lb_ref = refs[0], refs[1]
        w_refs, m_refs, v_refs = refs[2:2 + n], refs[2 + n:2 + 2 * n], refs[2 + 2 * n:2 + 3 * n]
        out_refs = refs[2 + 3 * n:]
        chip = 2 * lax.axis_index("x") + lax.axis_index("y")
        n_f, shard = lb_ref.shape[-1], w_refs[n - 1].shape[-1]
        _, vjp = jax.vjp(_lower_fn, lb_ref[...])
        dlb = vjp(red_ref[8:10, 0:n_f])[0]
        for i in range(n):
            width = w_refs[i].shape[-1]
            for j, (at, lane) in enumerate(pieces(w_refs[i].shape)):
                if i < n - 1:
                    g = red_ref[i:i + 1, lane:lane + width]
                else:
                    row = dlb[j // 2][j % 2:j % 2 + 1]
                    g = sum(jnp.where(chip == q, row[:, q * shard:(q + 1) * shard], 0.0) for q in range(N_CHIPS))
                delta, m_new, v_new = _adamw_math(w_refs[i][at], g, m_refs[i][at], v_refs[i][at])
                for o_ref, val in zip(out_refs[4 * i:4 * i + 4], (g, delta, m_new, v_new)):
                    o_ref[at] = val

    return pl.pallas_call(
        body, name="adamw_small", out_shape=[_sds(w.shape, F32) for w in ws for _ in range(4)],
    )(red, lb_full, *ws, *ms, *vs)


def _chip_peers():
    x, y, c = lax.axis_index("x"), lax.axis_index("y"), lax.axis_index("c")
    return (x, y, c), 2 * x + y, [(1 - x, y), (x, 1 - y), (1 - x, 1 - y)]


_HBM = pl.BlockSpec(memory_space=pltpu.HBM)
_SEM = pl.BlockSpec(memory_space=pltpu.SEMAPHORE)
_EFFECT = pltpu.SideEffectType.DATAFLOW_SIDE_EFFECTING


def _exchange_copies(srcs, lands, sems, mode):
    (x, y, c), me, chips = _chip_peers()
    copies = []
    for t, (src, land) in enumerate(zip(srcs, lands)):
        if mode == "swap":
            copies.append(pltpu.make_async_remote_copy(src, land, sems[0].at[3 * t], sems[1].at[3 * t],
                                                       device_id=(x, y, 1 - c), device_id_type=MESH))
            continue
        for k, (px, py) in enumerate(chips):
            gather = mode == "gather"
            copies.append(pltpu.make_async_remote_copy(
                src if gather else src.at[2 * px + py], land.at[me] if gather else land.at[k],
                sems[0].at[3 * t + k], sems[1].at[3 * t + k], device_id=(px, py, c), device_id_type=MESH))
        if mode == "gather":
            copies.append(pltpu.make_async_copy(src, land.at[me], sems[2].at[t]))
    return copies


def _exchange_start(srcs, *, mode, name, after=None):
    n = len(srcs)
    n_sem = 3 if mode == "gather" else 2
    n_in = 2 * n + (after is not None)
    land_shape = {"gather": lambda s: (N_CHIPS,) + s.shape, "scatter": lambda s: (3,) + s.shape[1:], "swap": lambda s: s.shape}
    lands = [_sds(land_shape[mode](s), s.dtype) for s in srcs]

    def body(*refs):
        for cp in _exchange_copies(refs[:n], refs[n:2 * n], refs[n_in:n_in + n_sem], mode):
            cp.start()
        token = refs[-1]
        token[...] = jnp.zeros_like(token)

    sem_shapes = [pltpu.SemaphoreType.DMA((3 * n,)), pltpu.SemaphoreType.DMA((3 * n,))]
    sem_shapes += [pltpu.SemaphoreType.DMA((n,))] if mode == "gather" else []
    thru = [pltpu.HBM(s.shape, s.dtype) for s in srcs] + [pltpu.HBM(l.shape, l.dtype) for l in lands]
    res = pl.pallas_call(
        body, name=name, in_specs=[_HBM] * (2 * n) + [_ANY] * (after is not None),
        out_specs=[_SEM] * n_sem + [_HBM] * (2 * n) + [pl.BlockSpec(memory_space=pltpu.VMEM)],
        out_shape=sem_shapes + thru + [_sds((8, LANE), F32)], input_output_aliases={i: n_sem + i for i in range(2 * n)},
        compiler_params=pltpu.CompilerParams(has_side_effects=_EFFECT),
    )(*[pltpu.with_memory_space_constraint(s, pltpu.HBM) for s in srcs],
      *[pltpu.with_memory_space_constraint(lax.empty(l.shape, l.dtype), pltpu.HBM) for l in lands],
      *([after] if after is not None else []))
    return (res[:n_sem], res[n_sem:n_sem + n], res[n_sem + n:n_sem + 2 * n], mode), res[-1]


def _exchange_wait(started, after, *, name):
    sems, srcs, lands, mode = started
    n, n_sem = len(srcs), len(sems)
    after = list(after) if isinstance(after, (list, tuple)) else [after]

    def body(*refs):
        for cp in _exchange_copies(refs[:n], refs[n:2 * n], refs[2 * n:2 * n + n_sem], mode):
            cp.wait()

    res = pl.pallas_call(
        body, name=name, in_specs=[_HBM] * (2 * n) + [_SEM] * n_sem + [_ANY] * len(after), out_specs=[_HBM] * (2 * n),
        out_shape=[pltpu.HBM(a.shape, a.dtype) for a in list(srcs) + list(lands)],
        input_output_aliases={i: i for i in range(2 * n)},
        compiler_params=pltpu.CompilerParams(has_side_effects=_EFFECT),
    )(*srcs, *lands, *sems, *after)
    return res[:n], res[n:]


def _allreduce_small(pack, after):
    R, C = pack.shape

    def body(in_ref, after_ref, out_ref, slots, send_sems, recv_sems):
        x, y, c = lax.axis_index("x"), lax.axis_index("y"), lax.axis_index("c")
        me = 4 * x + 2 * y + c
        slots[me] = in_ref[...]
        copies = []
        for k in range(1, N_DEV):
            peer = (x ^ ((k >> 2) & 1), y ^ ((k >> 1) & 1), c ^ (k & 1))
            cp = pltpu.make_async_remote_copy(in_ref, slots.at[me], send_sems.at[k - 1], recv_sems.at[k - 1],
                                              device_id=peer, device_id_type=MESH)
            cp.start()
            copies.append(cp)
        for cp in copies:
            cp.wait()
        acc = slots[0]
        for d in range(1, N_DEV):
            acc = acc + slots[d]
        out_ref[...] = acc

    return pl.pallas_call(
        body, name="allreduce_small", out_shape=_sds((R, C), F32),
        in_specs=[pl.BlockSpec(memory_space=pltpu.VMEM), _ANY], out_specs=pl.BlockSpec(memory_space=pltpu.VMEM),
        scratch_shapes=[pltpu.VMEM((N_DEV, R, C), F32), pltpu.SemaphoreType.DMA((N_DEV - 1,)),
                        pltpu.SemaphoreType.DMA((N_DEV - 1,))],
        compiler_params=_params(),
    )(pack, after)


_Z_CQ, _Z_CKV, _Z_HQ, _Z_HFF, _Z_HFB, _Z_HI, _Z_HG, _Z_KR, _Z_END = 0, 256, 512, 1024, 1536, 2048, 2560, 3072, 3200


def _to_z_layout(wt):
    pad = jnp.zeros((_Z_END - _Z_KR - QK_ROPE, wt.shape[1]), wt.dtype)
    return jnp.concatenate([wt[:512], wt[512 + QK_ROPE:], wt[512:512 + QK_ROPE], pad], axis=0)


def _from_z_layout(wt):
    return jnp.concatenate([wt[:512], wt[_Z_KR:_Z_KR + QK_ROPE], wt[512:_Z_KR]], axis=0)


def _col_shards_to_full(g):
    return jnp.transpose(g, (1, 0, 2)).reshape(g.shape[1], -1)


def _full_to_col_shards(w):
    r, c = w.shape
    return jnp.transpose(w.reshape(r, N_CHIPS, c // N_CHIPS), (1, 0, 2))


def _full_to_row_shards(w):
    r, c = w.shape
    return w.reshape(N_CHIPS, r // N_CHIPS, c)


def kernel(x, p, positions, g_mix, w_in, g_qa, g_kva, w_qb, w_kvb, g_qn, g_kn, lb_param, g_hgo, w_o, g_ffn, w_gate, w_up, w_down, g_ple, w_ple_gate, w_ple_proj, loss_target, m_g_mix, m_w_in, m_g_qa, m_g_kva, m_w_qb, m_w_kvb, m_g_qn, m_g_kn, m_lb_param, m_g_hgo, m_w_o, m_g_ffn, m_w_gate, m_w_up, m_w_down, m_g_ple, m_w_ple_gate, m_w_ple_proj, v_g_mix, v_w_in, v_g_qa, v_g_kva, v_w_qb, v_w_kvb, v_g_qn, v_g_kn, v_lb_param, v_g_hgo, v_w_o, v_g_ffn, v_w_gate, v_w_up, v_w_down, v_g_ple, v_w_ple_gate, v_w_ple_proj):
    w_named = dict(g_mix=g_mix, w_in=w_in, g_qa=g_qa, g_kva=g_kva, w_qb=w_qb, w_kvb=w_kvb, g_qn=g_qn, g_kn=g_kn,
                   lb_param=lb_param, g_hgo=g_hgo, w_o=w_o, g_ffn=g_ffn, w_gate=w_gate, w_up=w_up, w_down=w_down,
                   g_ple=g_ple, w_ple_gate=w_ple_gate, w_ple_proj=w_ple_proj)
    m_named = dict(g_mix=m_g_mix, w_in=m_w_in, g_qa=m_g_qa, g_kva=m_g_kva, w_qb=m_w_qb, w_kvb=m_w_kvb, g_qn=m_g_qn,
                   g_kn=m_g_kn, lb_param=m_lb_param, g_hgo=m_g_hgo, w_o=m_w_o, g_ffn=m_g_ffn, w_gate=m_w_gate,
                   w_up=m_w_up, w_down=m_w_down, g_ple=m_g_ple, w_ple_gate=m_w_ple_gate, w_ple_proj=m_w_ple_proj)
    v_named = dict(g_mix=v_g_mix, w_in=v_w_in, g_qa=v_g_qa, g_kva=v_g_kva, w_qb=v_w_qb, w_kvb=v_w_kvb, g_qn=v_g_qn,
                   g_kn=v_g_kn, lb_param=v_lb_param, g_hgo=v_g_hgo, w_o=v_w_o, g_ffn=v_g_ffn, w_gate=v_w_gate,
                   w_up=v_w_up, w_down=v_w_down, g_ple=v_g_ple, w_ple_gate=v_w_ple_gate, w_ple_proj=v_w_ple_proj)
    order = list(w_named)
    transposed = ("w_in", "w_qb", "w_gate", "w_up")
    col_sharded = ("w_kvb", "w_ple_proj")
    row_sharded = ("w_o", "w_down", "w_ple_gate")
    big = transposed + col_sharded + row_sharded

    def view(n, a):
        return jnp.transpose(a[0]) if n in transposed else a[0]

    def unview(n, a):
        return (jnp.transpose(a) if n in transposed else a)[None]

    def to_shards(n, g):
        return _full_to_col_shards(g) if n in col_sharded else _full_to_row_shards(g)

    x2d, p2d, tgt = x[0], p[0, 0], loss_target[0]
    T, D = x2d.shape

    lb_flat = lb_param.reshape(-1, lb_param.shape[-1])
    gather_groups = (("w_in",), ("w_qb", "w_kvb"), ("w_o", "w_gate", "w_up", "w_down", "w_ple_gate", "w_ple_proj"))
    gather_started = []

    casts = {n: view(n, w_named[n]).astype(BF16) for n in big}

    def gather_start(gi, after):
        srcs = [casts[n] for n in gather_groups[gi]] + ([lb_flat] if gi == 0 else [])
        started, token = _exchange_start(srcs, mode="gather", name=f"gather_start_{gi}", after=after)
        gather_started.append(started)
        return token

    full = {}

    def gather_wait(gi, after):
        _, got = _exchange_wait(gather_started[gi], after, name=f"gather_wait_{gi}")
        for n, g in zip(gather_groups[gi], got):
            full[n] = _col_shards_to_full(g) if n in col_sharded else g.reshape(-1, g.shape[-1])
        return got

    g_hgo_row = g_hgo.reshape(1, -1)

    inv_freq = ROPE_THETA ** (-jnp.arange(0, QK_ROPE, 2, dtype=F32) / QK_ROPE)
    ang = positions[0].astype(F32)[:, None] * inv_freq
    cos, sin = jnp.cos(ang), jnp.sin(ang)
    token = gather_start(0, None)
    h1 = _stage(_rms, [x2d], [g_mix], [_sds((T, D), BF16)], [], name="norm_mix", after=token)[0]
    got = gather_wait(0, [h1, cos, sin] + [casts[n] for g in gather_groups[1:] for n in g])
    token = got[0]
    for gi in range(1, len(gather_groups)):
        token = gather_start(gi, token)
    lb_full = _col_shards_to_full(got[-1]).reshape(lb_param.shape[0], lb_param.shape[1], -1)
    w_in_zt = _to_z_layout(full["w_in"])
    z = _mm(h1, w_in_zt, tb=True, name="in_proj", after=token)
    qn, kvn = _stage(_mla_a_fn, [_cols(z, 256, 0), _cols(z, 256, 1)], [g_qa, g_kva],
                     [_sds((T, 256), BF16), _sds((T, 256), BF16)], [], name="mla_latent_norm")
    gather_wait(1, qn)
    q_raw = _mm(qn, full["w_qb"], tb=True, name="q_up")
    kv_raw = _mm(kvn, full["w_kvb"], name="kv_up")
    kr = _cols(z, LANE, _Z_KR // LANE)
    q, k, v = _stage(_mla_b_fn, [q_raw, kv_raw, kr, cos, sin], [g_qn, g_kn],
                     [_sds((MLA_HEADS, T, QK_HEAD), BF16), _sds((MLA_HEADS, T, QK_HEAD), BF16),
                      _sds((MLA_HEADS, T, V_HEAD), BF16)], [], name="mla_qk_norm_rope")
    att, lse = _attention_fwd(q, k, v)

    lower = _lower_bounds(lb_full)
    lower3 = lower.reshape(2, HG_HEADS, 1, HG_DK)
    o_f, st_f = _gla_fwd(z, lower3[0], rev=False, col_q=_Z_HQ, col_f=_Z_HFF, col_v=_Z_HI, hp=GLA_FWD_HEADS)
    o_b, st_b = _gla_fwd(z, lower3[1], rev=True, col_q=_Z_HQ, col_f=_Z_HFB, col_v=_Z_HI, hp=GLA_FWD_HEADS)
    hg = _cols(z, 512, _Z_HG // 512)
    mix = _stage(_post_fn, [att, o_f, o_b, hg], [g_hgo_row], [_sds((T, att.shape[1] + o_f.shape[1]), BF16)], [],
                 name="mix_out")[0]
    gather_wait(2, mix)
    x1, h2 = _mm_fused(mix, full["w_o"], _resid_norm_fn, [x2d], [g_ffn], [F32, BF16], [], full_rows=True,
                       name="out_proj")
    gt, up, act = _mm_fused(h2, full["w_gate"], lambda gt, up: (gt, up, _swiglu_fn(gt, up)), [], [], [BF16, BF16, BF16],
                            [], tb=True, b2=full["w_up"], name="ffn_gate_up")
    x2, h3 = _mm_fused(act, full["w_down"], _resid_norm_fn, [x1], [g_ple], [F32, BF16], [], full_rows=True,
                       name="ffn_down")
    pp = _mm(p2d, full["w_ple_proj"], name="ple_proj")
    dx3, dpg, dpp, loss_part = _mm_fused(
        h3, full["w_ple_gate"], lambda acc, pp, x2, tgt: _ple_loss_fn(x2, acc, pp, tgt), [pp, x2, tgt], [],
        [F32, BF16, BF16], [_sds((1, 1), F32)], full_rows=True, name="ple_gate_loss")

    grads = {}
    scatter_groups = (("w_ple_proj", "w_ple_gate", "w_down", "w_gate", "w_up", "w_o"), ("w_qb", "w_kvb", "w_in"))
    scatter_started = []

    def scatter_start(gi):
        srcs = [to_shards(n, grads[n]) for n in scatter_groups[gi]]
        started, token = _exchange_start(srcs, mode="scatter", name=f"scatter_start_{gi}")
        scatter_started.append(started)
        return token

    chip = 2 * lax.axis_index("x") + lax.axis_index("y")
    swap_started = []

    def reduce_group(gi, after):
        shards, recvs = _exchange_wait(scatter_started[gi], after, name=f"scatter_wait_{gi}")
        sums = [_sum4(s, r, chip.reshape(1), name="sum_" + n) for n, s, r in zip(scatter_groups[gi], shards, recvs)]
        started, token = _exchange_start(sums, mode="swap", name=f"swap_start_{gi}")
        swap_started.append(started)
        return token

    grads["w_ple_proj"] = _mm(p2d, dpp, ta=True, out_dtype=BF16, name="d_w_ple_proj")
    grads["w_ple_gate"] = _mm(h3, dpg, ta=True, out_dtype=BF16, name="d_w_ple_gate")
    dx2, grads["g_ple"] = _mm_fused(
        dpg, full["w_ple_gate"], lambda acc, x2, dx3, g: _norm_bwd_fn(x2, acc, dx3, g), [x2, dx3], [g_ple],
        [F32], [_sds((1, D), F32)], tb=True, full_rows=True, name="d_h3_norm_ple_bwd")
    dgt, dup = _mm_fused(dx2, full["w_down"], lambda acc, gt, up: _swiglu_bwd_fn(gt.astype(F32), up.astype(F32), acc), [gt, up], [],
                         [BF16, BF16], [], tb=True, name="d_act_swiglu_bwd")
    grads["w_down"] = _mm(act, dx2, ta=True, out_dtype=BF16, name="d_w_down")
    grads["w_gate"] = _mm(dgt, h2, ta=True, out_dtype=BF16, name="d_w_gate")
    grads["w_up"] = _mm(dup, h2, ta=True, out_dtype=BF16, name="d_w_up")
    dx1, grads["g_ffn"] = _mm_fused(
        dgt, full["w_gate"], lambda acc, x1, dx2, g: _norm_bwd_fn(x1, acc, dx2, g), [x1, dx2], [g_ffn],
        [F32], [_sds((1, D), F32)], full_rows=True, a2=dup, b2=full["w_up"], name="d_h2_norm_ffn_bwd")
    grads["w_o"] = _mm(mix, dx1, ta=True, out_dtype=BF16, name="d_w_o")
    token = scatter_start(0)
    dmix = _mm(dx1, full["w_o"], tb=True, name="d_mix", after=token)

    half = MLA_HEADS * V_HEAD
    do, dhg, dg_hgo = _stage(_post_bwd_fn, [o_f, o_b, hg, _cols(dmix, half, 1)], [g_hgo_row],
                             [_sds((T, half), F32), _sds((T, half), BF16)], [_sds((1, half), F32)], name="mix_out_bwd")
    grads["g_hgo"] = dg_hgo
    dhq_f, dhi_f, dhf_f, dlow_f = _gla_bwd(z, lower3[0], st_f, do, None, rev=False,
                                           col_q=_Z_HQ, col_f=_Z_HFF, col_v=_Z_HI, hp=GLA_BWD_HEADS)
    dhq, dhi, dhf_b, dlow_b = _gla_bwd(z, lower3[1], st_b, do, (dhq_f, dhi_f), rev=True,
                                       col_q=_Z_HQ, col_f=_Z_HFB, col_v=_Z_HI, hp=GLA_BWD_HEADS)

    dq, dk, dv = _attention_bwd(q, k, v, att, lse, dmix)
    dq_raw, dkv_raw, dkr, grads["g_qn"], grads["g_kn"] = _stage(
        _mla_b_bwd_fn, [q_raw, kv_raw, kr, cos, sin, dq, dk, dv], [g_qn, g_kn],
        [_sds(q_raw.shape, BF16), _sds(kv_raw.shape, BF16), _sds((T, LANE), BF16)],
        [_sds(g_qn.shape, F32), _sds(g_kn.shape, F32)], name="mla_qk_norm_rope_bwd")
    grads["w_qb"] = _mm(dq_raw, qn, ta=True, out_dtype=BF16, name="d_w_qb")
    grads["w_kvb"] = _mm(kvn, dkv_raw, ta=True, out_dtype=BF16, name="d_w_kvb")
    dqn = _mm(dq_raw, full["w_qb"], name="d_qn")
    dkvn = _mm(dkv_raw, full["w_kvb"], tb=True, name="d_kvn")
    dcq, dckv, grads["g_qa"], grads["g_kva"] = _stage(
        _mla_a_bwd_fn, [_cols(z, 256, 0), _cols(z, 256, 1), dqn, dkvn], [g_qa, g_kva],
        [_sds((T, 256), BF16), _sds((T, 256), BF16)], [_sds(g_qa.shape, F32), _sds(g_kva.shape, F32)],
        name="mla_latent_norm_bwd")
    token = reduce_group(0, dcq)
    dz = jnp.concatenate([dcq, dckv, dhq, dhf_f, dhf_b, dhi, dhg, dkr], axis=1)
    grads["w_in"] = _from_z_layout(_mm(dz, h1, ta=True, out_dtype=BF16, name="d_w_in", after=token))
    token = scatter_start(1)
    grad_x, grads["g_mix"] = _mm_fused(
        dz, w_in_zt, lambda acc, x, dx1, g: _norm_bwd_fn(x, acc, dx1, g), [x2d, dx1], [g_mix],
        [F32], [_sds((1, D), F32)], full_rows=True, name="d_h1_norm_mix_bwd", after=token)

    out_g, out_d, out_m, out_v = {}, {}, {}, {}

    def update_group(gi, after):
        mine, theirs = _exchange_wait(swap_started[gi], after, name=f"swap_wait_{gi}")
        tokens = []
        for n, a, b in zip(scatter_groups[gi], mine, theirs):
            *res, token = _adamw(view(n, w_named[n]), a, b, view(n, m_named[n]), view(n, v_named[n]), name="adamw_" + n)
            out_g[n], out_d[n], out_m[n], out_v[n] = (unview(n, t) for t in res)
            tokens.append(token)
        return tokens

    done = update_group(0, grads["g_mix"])
    token = reduce_group(1, done)
    done = update_group(1, token)

    small = ("g_mix", "g_qa", "g_kva", "g_qn", "g_kn", "g_hgo", "g_ffn", "g_ple")
    small_all = small + ("lb_param",)
    width = -(-max(w_named[n].size for n in small_all) // LANE) * LANE

    def row(a):
        a = a.reshape(1, -1)
        return jnp.pad(a, ((0, 0), (0, width - a.shape[1])))

    dlower = jnp.concatenate([dlow_f.reshape(1, -1), dlow_b.reshape(1, -1)], axis=0)
    pack = jnp.concatenate([row(grads[n]) for n in small] + [row(dlower[0]), row(dlower[1]), row(loss_part)]
                           + [jnp.zeros((5, width), F32)], axis=0)
    red = _allreduce_small(pack, done[-1])
    loss = red[10, 0]

    outs = _adamw_small(red, lb_full, [w_named[n] for n in small_all], [m_named[n] for n in small_all],
                        [v_named[n] for n in small_all])
    for i, n in enumerate(small_all):
        out_g[n], out_d[n], out_m[n], out_v[n] = outs[4 * i:4 * i + 4]

    return (loss, grad_x[None], *[out_g[n] for n in order], *[out_d[n] for n in order],
            *[out_m[n] for n in order], *[out_v[n] for n in order])
```

```python
import functools
import itertools

import jax
import jax.numpy as jnp
from jax import lax
from jax.experimental import pallas as pl
from jax.experimental.pallas import tpu as pltpu

F32 = jnp.float32
BF16 = jnp.bfloat16
MESH = pl.DeviceIdType.MESH

EPS = 1e-6
ROPE_THETA = 10000.0
MLA_HEADS = 4
QK_NOPE = 128
QK_ROPE = 64
QK_HEAD = QK_NOPE + QK_ROPE
V_HEAD = 128
HG_HEADS = 4
HG_DK = 128
CHUNK = 64
ADAM_LR = 0.001
ADAM_B1 = 0.9
ADAM_B2 = 0.999
ADAM_EPS = 1e-08
ADAM_WD = 0.01
ADAM_STEP = 10

LANE = 128
VMEM_LIMIT = 56 * 1024 * 1024
TOK_TILE = 512
TOK_TILE_LIGHT = 1024
GLA_GROUP = 16
GLA_FWD_HEADS = 4
GLA_BWD_HEADS = 2
ATT_TQ = 1024
ATT_TK = 1024
ATT_CHUNK = 1024
LOG2_E = 1.4426950408889634
N_CHIPS = 4
N_DEV = 8


_ANY = pl.BlockSpec(memory_space=pl.ANY)


def _params(dims=None, **kw):
    return pltpu.CompilerParams(dimension_semantics=dims, vmem_limit_bytes=VMEM_LIMIT, **kw)


def _tile_candidates(n, cap):
    out = [d for d in range(LANE, min(n, cap) + 1, LANE) if n % d == 0]
    if n <= cap and n not in out:
        out.append(n)
    return out or [n]


MM_VMEM_BUDGET = 40 * 1024 * 1024
MM_MIN_ROWS = 256
MM_MAX_ROWS = 1536
HBM_BYTES_PER_S = 2.8e12
MXU_FLOPS_PER_S = 8e14
STEP_OVERHEAD_S = 0.35e-6


def _mm_tiles(M, N, K, a_bytes, b_bytes, o_bytes, has_add, full_rows=False, full_k=False, n_a=1, n_b=1):
    cast_a, cast_b = a_bytes > 2, b_bytes > 2
    a_bytes, b_bytes = n_a * a_bytes, n_b * b_bytes
    best = None
    for tm in [t for t in _tile_candidates(M, MM_MAX_ROWS) if t >= min(M, MM_MIN_ROWS)]:
        for tn in ([N] if full_rows else _tile_candidates(N, N)):
            for tk in ([K] if full_k else _tile_candidates(K, K)):
                ni, nj, nk = M // tm, N // tn, K // tk
                vmem = 2 * (tm * tk * a_bytes + tk * tn * b_bytes + tm * tn * o_bytes * (2 if has_add else 1))
                vmem += tm * tn * 4 * (2 if nk > 1 else 1)
                vmem += (tm * tk * 2 * n_a if cast_a else 0) + (tk * tn * 2 * n_b if cast_b else 0)
                if vmem > MM_VMEM_BUDGET:
                    continue
                moved = M * K * a_bytes * (nj if nk > 1 else 1) + K * N * b_bytes * (1 if nj == nk == 1 else ni)
                moved += M * N * o_bytes * (2 if has_add else 1)
                t = max(moved / HBM_BYTES_PER_S, 2 * M * N * K / MXU_FLOPS_PER_S) + ni * nj * nk * STEP_OVERHEAD_S
                if best is None or t < best[0]:
                    best = (t, tm, tn, tk)
    assert best is not None, (M, N, K)
    return best[1:]


def _dot_raw(a, b, kind):
    nb = a.ndim - 2
    batch = ((0,), (0,)) if nb else ((), ())
    ca = nb if kind == "tn" else nb + 1
    cb = nb + 1 if kind == "nt" else nb
    return lax.dot_general(a.astype(BF16), b.astype(BF16), (((ca,), (cb,)), batch), preferred_element_type=F32)


@functools.partial(jax.custom_vjp, nondiff_argnums=(2,))
def _bdot(a, b, kind):
    return _dot_raw(a, b, kind)


def _bdot_fwd(a, b, kind):
    return _dot_raw(a, b, kind), (a, b)


def _bdot_bwd(kind, res, g):
    a, b = res
    if kind == "nn":
        da, db = _bdot(g, b, "nt"), _bdot(a, g, "tn")
    elif kind == "nt":
        da, db = _bdot(g, b, "nn"), _bdot(g, a, "tn")
    else:
        da, db = _bdot(b, g, "nt"), _bdot(a, g, "nn")
    return da.astype(a.dtype), db.astype(b.dtype)


_bdot.defvjp(_bdot_fwd, _bdot_bwd)


def _mm(a, b, *, name, ta=False, tb=False, add=None, out_dtype=F32, after=None):
    K, M = a.shape if ta else a.shape[::-1]
    N, Kb = b.shape if tb else b.shape[::-1]
    assert K == Kb, (a.shape, b.shape, ta, tb)
    tm, tn, tk = _mm_tiles(M, N, K, a.dtype.itemsize, b.dtype.itemsize, jnp.dtype(out_dtype).itemsize, add is not None)
    nk = K // tk
    kind = "tn" if ta else ("nt" if tb else "nn")
    assert not (ta and tb)
    a_spec = pl.BlockSpec((tk, tm), lambda i, j, k: (k, i)) if ta else pl.BlockSpec((tm, tk), lambda i, j, k: (i, k))
    b_spec = pl.BlockSpec((tn, tk), lambda i, j, k: (j, k)) if tb else pl.BlockSpec((tk, tn), lambda i, j, k: (k, j))
    o_spec = pl.BlockSpec((tm, tn), lambda i, j, k: (i, j))
    has_add = add is not None

    def body(*refs):
        a_ref, b_ref = refs[0], refs[1]
        add_ref = refs[2] if has_add else None
        o_ref = refs[n_in]
        part = _dot_raw(a_ref[...], b_ref[...], kind)
        if nk == 1:
            if has_add:
                part = part + add_ref[...].astype(F32)
            o_ref[...] = part.astype(o_ref.dtype)
            return
        acc_ref = refs[-1]
        k = pl.program_id(2)

        @pl.when(k == 0)
        def _():
            acc_ref[...] = part

        @pl.when(k > 0)
        def _():
            acc_ref[...] += part

        @pl.when(k == nk - 1)
        def _():
            r = acc_ref[...]
            if has_add:
                r = r + add_ref[...].astype(F32)
            o_ref[...] = r.astype(o_ref.dtype)

    ins = [a, b] + ([add] if has_add else []) + ([after] if after is not None else [])
    in_specs = [a_spec, b_spec] + ([o_spec] if has_add else []) + ([_ANY] if after is not None else [])
    n_in = len(ins)
    return pl.pallas_call(
        body, name=name, grid=(M // tm, N // tn, nk), in_specs=in_specs, out_specs=o_spec,
        out_shape=jax.ShapeDtypeStruct((M, N), out_dtype),
        scratch_shapes=[pltpu.VMEM((tm, tn), F32)] if nk > 1 else [],
        compiler_params=_params(("parallel", "parallel", "arbitrary")),
    )(*ins)


def _mm_fused(a, b, fn, tiles, params, out_dtypes, sums, *, name, ta=False, tb=False, full_rows=False, after=None,
              b2=None, a2=None):
    K, M = a.shape if ta else a.shape[::-1]
    N, Kb = b.shape if tb else b.shape[::-1]
    assert K == Kb and not (ta and tb), (a.shape, b.shape, ta, tb)
    per_elem = sum(t.dtype.itemsize for t in tiles) + sum(jnp.dtype(d).itemsize for d in out_dtypes)
    n_b = 1 if b2 is None else 2
    n_a = 1 if a2 is None else 2
    tm, tn, tk = _mm_tiles(M, N, K, a.dtype.itemsize, b.dtype.itemsize, per_elem, False, full_rows, b2 is not None,
                           n_a, n_b)
    nk = K // tk
    kind = "tn" if ta else ("nt" if tb else "nn")
    a_spec = pl.BlockSpec((tk, tm), lambda i, j, k: (k, i)) if ta else pl.BlockSpec((tm, tk), lambda i, j, k: (i, k))
    b_spec = pl.BlockSpec((tn, tk), lambda i, j, k: (j, k)) if tb else pl.BlockSpec((tk, tn), lambda i, j, k: (k, j))
    o_spec = pl.BlockSpec((tm, tn), lambda i, j, k: (i, j))
    ins = [a, b] + ([b2] if b2 is not None else []) + ([a2] if a2 is not None else [])
    ins += list(tiles) + list(params) + ([after] if after is not None else [])
    in_specs = [a_spec] + [b_spec] * n_b + [a_spec] * (n_a - 1) + [o_spec] * len(tiles)
    in_specs += [pl.BlockSpec(p.shape, lambda i, j, k, nd=p.ndim: (0,) * nd) for p in params]
    in_specs += [_ANY] if after is not None else []
    n_in, n_t, n_p, n_o = len(ins), len(tiles), len(params), len(out_dtypes)

    def body(*refs):
        outs, sum_refs = refs[n_in:n_in + n_o], refs[n_in + n_o:n_in + n_o + len(sums)]

        def finish(*products):
            res = fn(*products, *[t[...] for t in refs[n_a + n_b:n_a + n_b + n_t + n_p]])
            for o_ref, v in zip(outs, res[:n_o]):
                o_ref[...] = v.astype(o_ref.dtype)
            first = jnp.logical_and(pl.program_id(0) == 0, pl.program_id(1) == 0)
            for s_ref, v in zip(sum_refs, res[n_o:]):
                @pl.when(first)
                def _(s_ref=s_ref, v=v):
                    s_ref[...] = v

                @pl.when(jnp.logical_not(first))
                def _(s_ref=s_ref, v=v):
                    s_ref[...] += v

        part = _dot_raw(refs[0][...], refs[1][...], kind)
        if nk == 1 and a2 is not None:
            finish(part + _dot_raw(refs[3][...], refs[2][...], kind))
            return
        if nk == 1:
            finish(part, *([_dot_raw(refs[0][...], refs[2][...], kind)] if b2 is not None else []))
            return
        acc_ref = refs[-1]
        k = pl.program_id(2)

        @pl.when(k == 0)
        def _():
            acc_ref[...] = part

        @pl.when(k > 0)
        def _():
            acc_ref[...] += part

        @pl.when(k == nk - 1)
        def _():
            finish(acc_ref[...])

    out_shape = [_sds((M, N), d) for d in out_dtypes] + list(sums)
    out_specs = [o_spec] * n_o + [pl.BlockSpec(s.shape, lambda i, j, k, nd=len(s.shape): (0,) * nd) for s in sums]
    order = ("arbitrary",) * 3 if sums else ("parallel", "parallel", "arbitrary")
    return pl.pallas_call(
        body, name=name, grid=(M // tm, N // tn, nk), in_specs=in_specs, out_specs=out_specs, out_shape=out_shape,
        scratch_shapes=[pltpu.VMEM((tm, tn), F32)] if nk > 1 else [], compiler_params=_params(order),
    )(*ins)


def _cols(arr, width, block):
    return (arr, width, block)


def _stage(fn, tiles, params, out_tiles, out_sums, *, name, tile=TOK_TILE, after=None):
    first = tiles[0][0] if isinstance(tiles[0], tuple) else tiles[0]
    tile = min(tile, first.shape[0] if first.ndim == 2 else first.shape[1])

    def tok_spec(shape, width=None, block=0):
        if len(shape) == 2:
            w = shape[1] if width is None else width
            return pl.BlockSpec((tile, w), lambda i: (i, block))
        return pl.BlockSpec((shape[0], tile, shape[2]), lambda i: (0, i, 0))

    arrays, in_specs = [], []
    for t in tiles:
        if isinstance(t, tuple):
            arr, width, block = t
            arrays.append(arr)
            in_specs.append(tok_spec(arr.shape, width, block))
        else:
            arrays.append(t)
            in_specs.append(tok_spec(t.shape))
    n_tok = arrays[0].shape[0] if arrays[0].ndim == 2 else arrays[0].shape[1]
    for p in params:
        arrays.append(p)
        in_specs.append(pl.BlockSpec(p.shape, lambda i, nd=p.ndim: (0,) * nd))
    out_shape = list(out_tiles) + list(out_sums)
    out_specs = [tok_spec(o.shape) for o in out_tiles]
    out_specs += [pl.BlockSpec(o.shape, lambda i, nd=len(o.shape): (0,) * nd) for o in out_sums]
    n_fn, n_ot = len(arrays), len(out_tiles)
    if after is not None:
        arrays.append(after)
        in_specs.append(_ANY)
    n_in = len(arrays)

    def body(*refs):
        res = fn(*[r[...] for r in refs[:n_fn]])
        if not isinstance(res, (tuple, list)):
            res = (res,)
        outs = refs[n_in:]
        for o_ref, r in zip(outs[:n_ot], res[:n_ot]):
            o_ref[...] = r.astype(o_ref.dtype)
        i = pl.program_id(0)
        for o_ref, r in zip(outs[n_ot:], res[n_ot:]):
            @pl.when(i == 0)
            def _(o_ref=o_ref, r=r):
                o_ref[...] = r.astype(o_ref.dtype)

            @pl.when(i > 0)
            def _(o_ref=o_ref, r=r):
                o_ref[...] += r.astype(o_ref.dtype)

    res = pl.pallas_call(
        body, name=name, grid=(n_tok // tile,), in_specs=in_specs, out_specs=out_specs, out_shape=out_shape,
        compiler_params=_params(("arbitrary",)),
    )(*arrays)
    return res


def _sds(shape, dtype):
    return jax.ShapeDtypeStruct(tuple(shape), dtype)


def _sigmoid(x):
    return 0.5 * jnp.tanh(0.5 * x) + 0.5


def _rms(x, g):
    return x * lax.rsqrt(jnp.mean(x * x, axis=-1, keepdims=True) + EPS) * g


def _norm_bwd_fn(x, dh, dres, g):
    r = lax.rsqrt(jnp.mean(x * x, axis=-1, keepdims=True) + EPS)
    xr = x * r
    dhg = dh * g
    dx = r * (dhg - xr * jnp.mean(xr * dhg, axis=-1, keepdims=True))
    return dx + dres, jnp.sum(dh * xr, axis=0, keepdims=True)


def _mla_a_fn(cq, ckv, g_qa, g_kva):
    return _rms(cq, g_qa), _rms(ckv, g_kva)


def _mla_a_bwd_fn(cq, ckv, dqn, dkvn, g_qa, g_kva):
    _, vjp = jax.vjp(_mla_a_fn, cq, ckv, g_qa, g_kva)
    return vjp((dqn, dkvn))


def _rope(t, cos, sin):
    t1, t2 = t[:, :QK_ROPE // 2], t[:, QK_ROPE // 2:]
    return jnp.concatenate([t1 * cos - t2 * sin, t1 * sin + t2 * cos], axis=-1)


def _mla_b_fn(q_raw, kv_raw, kr, cos, sin, g_qn, g_kn):
    krope = kr[:, :QK_ROPE]
    qs, ks, vs = [], [], []
    for h in range(MLA_HEADS):
        qh = _rms(q_raw[:, h * QK_HEAD:(h + 1) * QK_HEAD], g_qn)
        kvh = kv_raw[:, h * (QK_NOPE + V_HEAD):(h + 1) * (QK_NOPE + V_HEAD)]
        kh = _rms(jnp.concatenate([kvh[:, :QK_NOPE], krope], axis=-1), g_kn)
        qs.append(jnp.concatenate([qh[:, :QK_NOPE], _rope(qh[:, QK_NOPE:], cos, sin)], axis=-1))
        ks.append(jnp.concatenate([kh[:, :QK_NOPE], _rope(kh[:, QK_NOPE:], cos, sin)], axis=-1))
        vs.append(kvh[:, QK_NOPE:])
    return jnp.stack(qs), jnp.stack(ks), jnp.stack(vs)


def _mla_b_bwd_fn(q_raw, kv_raw, kr, cos, sin, dq, dk, dv, g_qn, g_kn):
    _, vjp = jax.vjp(lambda a, b, c, d, e: _mla_b_fn(a, b, c, cos, sin, d, e), q_raw, kv_raw, kr, g_qn, g_kn)
    return vjp((dq, dk, dv))


def _post_fn(a, o_f, o_b, hg, g_hgo):
    o = o_f + o_b
    parts = [a]
    for h in range(HG_HEADS):
        s = slice(h * HG_DK, (h + 1) * HG_DK)
        gate = hg[:, s]
        parts.append(_rms(o[:, s], g_hgo[:, s]) * (gate * _sigmoid(gate)))
    return jnp.concatenate(parts, axis=-1)


def _post_bwd_fn(o_f, o_b, hg, dr, g_hgo):
    def f(o, hg, g):
        return _post_fn(jnp.zeros_like(o), o, jnp.zeros_like(o), hg, g)[:, o.shape[1]:]
    _, vjp = jax.vjp(f, o_f + o_b, hg, g_hgo)
    return vjp(dr)


def _swiglu_fn(gt, up):
    return gt * _sigmoid(gt) * up


def _resid_norm_fn(acc, x, g):
    x_new = acc + x
    return x_new, _rms(x_new, g)


def _swiglu_bwd_fn(gt, up, dact):
    s = _sigmoid(gt)
    silu = gt * s
    return dact * up * (s + silu * (1.0 - s)), dact * silu


def _ple_loss_fn(x2, pg, pp, target):
    gate = _sigmoid(pg)
    err = x2 + gate * pp - target
    dx3 = err * (1.0 / err.shape[-1])
    loss = 0.5 * jnp.sum(jnp.mean(err * err, axis=-1, keepdims=True), axis=0, keepdims=True)
    return dx3, dx3 * pp * gate * (1.0 - gate), dx3 * gate, loss


def _attention_fwd(q, k, v):
    H, T, D = q.shape
    DV = v.shape[-1]
    tq, ck = min(ATT_TQ, T), min(ATT_CHUNK, T)
    c2 = (D ** -0.5) * LOG2_E

    def body(q_ref, k_ref, v_ref, o_ref, lse_ref):
        q_i = q_ref[0]

        def chunk(c, carry):
            m, l, acc = carry
            rows = pl.ds(pl.multiple_of(c * ck, ck), ck)
            s = _dot_raw(q_i, k_ref[0, rows, :], "nt")
            m_new = jnp.maximum(m, jnp.max(s, axis=-1, keepdims=True))
            p = jnp.exp2((s - m_new) * c2)
            alpha = jnp.exp2((m - m_new) * c2)
            l = l * alpha + jnp.sum(p, axis=-1, keepdims=True)
            acc = acc * alpha + _dot_raw(p, v_ref[0, rows, :], "nn")
            return m_new, l, acc

        init = (jnp.full((tq, 1), -jnp.inf, F32), jnp.zeros((tq, 1), F32), jnp.zeros((tq, DV), F32))
        m, l, acc = lax.fori_loop(0, T // ck, chunk, init, unroll=True)
        o_ref[...] = acc / l
        lse_ref[0] = m * c2 + jnp.log2(l)

    return pl.pallas_call(
        body, name="attention_fwd", grid=(H, T // tq),
        in_specs=[pl.BlockSpec((1, tq, D), lambda h, i: (h, i, 0)),
                  pl.BlockSpec((1, T, D), lambda h, i: (h, 0, 0)),
                  pl.BlockSpec((1, T, DV), lambda h, i: (h, 0, 0))],
        out_specs=[pl.BlockSpec((tq, DV), lambda h, i: (i, h)),
                   pl.BlockSpec((1, tq, 1), lambda h, i: (h, i, 0))],
        out_shape=[_sds((T, H * DV), F32), _sds((H, T, 1), F32)],
        compiler_params=_params(("parallel", "parallel")),
    )(q, k, v)


def _attention_bwd(q, k, v, o, lse2, dmix):
    H, T, D = q.shape
    DV = v.shape[-1]
    tk, cq = min(ATT_TK, T), min(ATT_CHUNK, T)
    scale = D ** -0.5
    c2 = scale * LOG2_E

    def body(q_ref, k_ref, v_ref, o_ref, lse_ref, do_ref, dq_ref, dk_ref, dv_ref, delta_ref):
        j = pl.program_id(1)

        @pl.when(j == 0)
        def _():
            delta = lax.dot_general(jnp.ones((8, DV), F32), do_ref[...] * o_ref[...], (((1,), (1,)), ((), ())),
                                    precision=lax.Precision.HIGHEST, preferred_element_type=F32)
            for i in range(T // cq):
                delta_ref[i] = delta[:, i * cq:(i + 1) * cq]
            dq_ref[0] = jnp.zeros((T, D), F32)

        k_j, v_j = k_ref[0], v_ref[0]
        dk_ref[0] = jnp.zeros((tk, D), F32)
        dv_ref[0] = jnp.zeros((tk, DV), F32)

        def chunk(c, carry):
            rows = pl.ds(pl.multiple_of(c * cq, cq), cq)
            q_c = q_ref[0, rows, :]
            do_c = do_ref[rows, :].astype(BF16)
            st = _dot_raw(k_j, q_c, "nt")
            pt = jnp.exp2(st * c2 - lse_ref[0, c])
            dv_ref[0] += _dot_raw(pt, do_c, "nn")
            dpt = _dot_raw(v_j, do_c, "nt")
            dst = pt * (dpt - delta_ref[c, 0:1, :]) * scale
            dk_ref[0] += _dot_raw(dst, q_c, "nn")
            dq_ref[0, rows, :] += _dot_raw(dst, k_j, "tn")
            return carry

        lax.fori_loop(0, T // cq, chunk, 0, unroll=True)

    return pl.pallas_call(
        body, name="attention_bwd", grid=(H, T // tk),
        in_specs=[pl.BlockSpec((1, T, D), lambda h, j: (h, 0, 0)),
                  pl.BlockSpec((1, tk, D), lambda h, j: (h, j, 0)),
                  pl.BlockSpec((1, tk, DV), lambda h, j: (h, j, 0)),
                  pl.BlockSpec((T, DV), lambda h, j: (0, h)),
                  pl.BlockSpec((1, T // cq, 1, cq), lambda h, j: (h, 0, 0, 0)),
                  pl.BlockSpec((T, DV), lambda h, j: (0, h))],
        out_specs=[pl.BlockSpec((1, T, D), lambda h, j: (h, 0, 0)),
                   pl.BlockSpec((1, tk, D), lambda h, j: (h, j, 0)),
                   pl.BlockSpec((1, tk, DV), lambda h, j: (h, j, 0))],
        out_shape=[_sds((H, T, D), F32), _sds((H, T, D), F32), _sds((H, T, DV), F32)],
        scratch_shapes=[pltpu.VMEM((T // cq, 8, cq), F32)],
        compiler_params=_params(("parallel", "arbitrary")),
    )(q, k, v, o, lse2.reshape(H, T // cq, 1, cq), dmix)


def _split3_dot(ones, x, kind):
    hi = x.astype(BF16)
    rest = x - hi.astype(F32)
    mid = rest.astype(BF16)
    lo = (rest - mid.astype(F32)).astype(BF16)
    return (_dot_raw(ones, hi, kind) + _dot_raw(ones, mid, kind)) + _dot_raw(ones, lo, kind)


@jax.custom_vjp
def _running_sum(x, tri):
    return _split3_dot(tri, x, "nn")


def _running_sum_fwd(x, tri):
    return _split3_dot(tri, x, "nn"), tri


def _running_sum_bwd(tri, g):
    return _split3_dot(tri, g, "tn"), jnp.zeros_like(tri)


_running_sum.defvjp(_running_sum_fwd, _running_sum_bwd)


def _gla_block(hq, hf, hi, lower, st_in, *, rev, dot):
    rows, dk = hq.shape
    G, C = rows // CHUNK, CHUNK
    q = hq * _sigmoid(hq)
    f = lower + (1.0 - lower) * _sigmoid(hf)
    k = 1.0 - f
    logf = jnp.log2(f)
    q3, k3, v3, lf3 = (t.reshape(G, C, dk) for t in (q, k, hi, logf))
    r = lax.broadcasted_iota(jnp.int32, (C, C), 0)
    c = lax.broadcasted_iota(jnp.int32, (C, C), 1)
    tri = ((r <= c) if rev else (r >= c)).astype(F32)
    b = _running_sum(lf3, jnp.broadcast_to(tri, (G, C, C)))
    tpos = lax.broadcasted_iota(jnp.int32, (1, C, 1), 1)
    first_half = (tpos >= C // 2) if rev else (tpos <= C // 2 - 1)
    b_mid = jnp.sum(jnp.where(first_half, lf3, 0.0), axis=1, keepdims=True)
    b_last = jnp.sum(lf3, axis=1, keepdims=True)
    a = dot(q3 * jnp.exp2(b - b_mid), k3 * jnp.exp2(b_mid - b), "nt") * tri
    o_intra = dot(a, v3, "nn")
    kv_t = dot(v3, k3 * jnp.exp2(b_last - b), "tn")
    decay = jnp.exp2(b_last)
    qd = q3 * jnp.exp2(b)
    st = st_in
    o_inter = [None] * G
    for g in (reversed(range(G)) if rev else range(G)):
        o_inter[g] = dot(qd[g], st, "nt")
        st = st * decay[g] + kv_t[g]
    o = o_intra.reshape(rows, dk) + jnp.concatenate(o_inter, axis=0)
    return o, st


def _gla_fwd(z, lower3, *, rev, col_q, col_f, col_v, hp):
    T = z.shape[0]
    rows = min(GLA_GROUP * CHUNK, T)
    nb = T // rows
    wide = hp * HG_DK
    blk = (lambda n: nb - 1 - n) if rev else (lambda n: n)

    def body(hq_ref, hf_ref, hi_ref, low_ref, o_ref, st_out_ref, st_ref):
        @pl.when(pl.program_id(1) == 0)
        def _():
            st_ref[...] = jnp.zeros_like(st_ref)

        st_in = [st_ref[i] for i in range(hp)]
        heads = []
        for i in range(hp):
            cols = slice(i * HG_DK, (i + 1) * HG_DK)
            heads.append(_gla_block(hq_ref[:, cols], hf_ref[:, cols], hi_ref[:, cols], low_ref[i], st_in[i], rev=rev,
                                    dot=_dot_raw))
        for i, (o, st) in enumerate(heads):
            st_out_ref[i, 0] = st_in[i]
            o_ref[:, i * HG_DK:(i + 1) * HG_DK] = o
            st_ref[i] = st

    def zspec(col):
        return pl.BlockSpec((rows, wide), lambda h, n: (blk(n), col // wide + h))

    return pl.pallas_call(
        body, name="gla_fwd_rev" if rev else "gla_fwd", grid=(HG_HEADS // hp, nb),
        in_specs=[zspec(col_q), zspec(col_f), zspec(col_v), pl.BlockSpec((hp, 1, HG_DK), lambda h, n: (h, 0, 0))],
        out_specs=[pl.BlockSpec((rows, wide), lambda h, n: (blk(n), h)),
                   pl.BlockSpec((hp, 1, HG_DK, HG_DK), lambda h, n: (h, blk(n), 0, 0))],
        out_shape=[_sds((T, HG_HEADS * HG_DK), F32), _sds((HG_HEADS, nb, HG_DK, HG_DK), F32)],
        scratch_shapes=[pltpu.VMEM((hp, HG_DK, HG_DK), F32)],
        compiler_params=_params(("parallel", "arbitrary")),
    )(z, z, z, lower3)


def _gla_bwd(z, lower3, states, do, prev, *, rev, col_q, col_f, col_v, hp):
    T = z.shape[0]
    rows = min(GLA_GROUP * CHUNK, T)
    nb = T // rows
    wide = hp * HG_DK
    blk = (lambda n: n) if rev else (lambda n: nb - 1 - n)
    has_prev = prev is not None
    fn = functools.partial(_gla_block, rev=rev, dot=_bdot)

    def body(*refs):
        hq_ref, hf_ref, hi_ref, low_ref, st_ref, do_ref = refs[:6]
        rest = refs[6:]
        if has_prev:
            pq_ref, pi_ref = rest[:2]
            rest = rest[2:]
        dhq_ref, dhi_ref, dhf_ref, dlow_ref, dst_ref = rest
        n = pl.program_id(1)

        @pl.when(n == 0)
        def _():
            dst_ref[...] = jnp.zeros_like(dst_ref)

        dst_in = [dst_ref[i] for i in range(hp)]
        heads = []
        for i in range(hp):
            cols = slice(i * HG_DK, (i + 1) * HG_DK)
            _, vjp = jax.vjp(fn, hq_ref[:, cols], hf_ref[:, cols], hi_ref[:, cols], low_ref[i], st_ref[i, 0])
            dhq, dhf, dhi, dlow, dst = vjp((do_ref[:, cols], dst_in[i]))
            if has_prev:
                dhq = dhq + pq_ref[:, cols]
                dhi = dhi + pi_ref[:, cols]
            heads.append((dhq, dhf, dhi, dlow, dst))
        for i, (dhq, dhf, dhi, dlow, dst) in enumerate(heads):
            cols = slice(i * HG_DK, (i + 1) * HG_DK)
            dst_ref[i] = dst
            dhq_ref[:, cols] = dhq.astype(dhq_ref.dtype)
            dhi_ref[:, cols] = dhi.astype(dhi_ref.dtype)
            dhf_ref[:, cols] = dhf.astype(dhf_ref.dtype)

        @pl.when(n == 0)
        def _():
            for i in range(hp):
                dlow_ref[i] = heads[i][3]

        @pl.when(n > 0)
        def _():
            for i in range(hp):
                dlow_ref[i] += heads[i][3]

    def zspec(col):
        return pl.BlockSpec((rows, wide), lambda h, n: (blk(n), col // wide + h))

    hspec = pl.BlockSpec((rows, wide), lambda h, n: (blk(n), h))
    in_specs = [zspec(col_q), zspec(col_f), zspec(col_v), pl.BlockSpec((hp, 1, HG_DK), lambda h, n: (h, 0, 0)),
                pl.BlockSpec((hp, 1, HG_DK, HG_DK), lambda h, n: (h, blk(n), 0, 0)), hspec]
    ins = [z, z, z, lower3, states, do]
    if has_prev:
        in_specs += [hspec, hspec]
        ins += list(prev)
    full_wide = HG_HEADS * HG_DK
    acc_dtype = BF16 if has_prev else F32
    return pl.pallas_call(
        body, name="gla_bwd_rev" if rev else "gla_bwd", grid=(HG_HEADS // hp, nb),
        in_specs=in_specs,
        out_specs=[hspec, hspec, hspec, pl.BlockSpec((hp, 1, HG_DK), lambda h, n: (h, 0, 0))],
        out_shape=[_sds((T, full_wide), acc_dtype), _sds((T, full_wide), acc_dtype), _sds((T, full_wide), BF16),
                   _sds((HG_HEADS, 1, HG_DK), F32)],
        scratch_shapes=[pltpu.VMEM((hp, HG_DK, HG_DK), F32)],
        compiler_params=_params(("parallel", "arbitrary")),
    )(*ins)


def _lower_fn(lb):
    e = jnp.exp(lb - jnp.max(lb, axis=0, keepdims=True))
    return (e / jnp.sum(e, axis=0, keepdims=True))[0]


def _lower_bounds(lb):
    def body(lb_ref, o_ref):
        o_ref[...] = _lower_fn(lb_ref[...])
    return pl.pallas_call(body, name="lower_bounds", out_shape=_sds(lb.shape[1:], F32))(lb)


def _row_tile(r, cap=1024):
    best = None
    for t in range(16, min(r, cap) + 1, 16):
        if r % t == 0:
            best = t
    return best if best is not None else r


def _sum4(shards, recv, chip, *, name):
    _, R, C = shards.shape
    tr = _row_tile(R)

    def body(chip_ref, o_ref, r_ref, out_ref):
        out_ref[...] = ((o_ref[0].astype(F32) + r_ref[0].astype(F32)) + r_ref[1].astype(F32)) + r_ref[2].astype(F32)

    grid_spec = pltpu.PrefetchScalarGridSpec(
        num_scalar_prefetch=1, grid=(R // tr,),
        in_specs=[pl.BlockSpec((1, tr, C), lambda i, chip_ref: (chip_ref[0], i, 0)),
                  pl.BlockSpec((3, tr, C), lambda i, chip_ref: (0, i, 0))],
        out_specs=pl.BlockSpec((tr, C), lambda i, chip_ref: (i, 0)))
    return pl.pallas_call(
        body, name=name, grid_spec=grid_spec, out_shape=_sds((R, C), F32), compiler_params=_params(("parallel",)),
    )(chip, shards, recv)


def _adamw_math(w, g, m, v):
    m = ADAM_B1 * m + (1.0 - ADAM_B1) * g
    v = ADAM_B2 * v + (1.0 - ADAM_B2) * (g * g)
    m_hat = m / (1.0 - ADAM_B1 ** ADAM_STEP)
    v_hat = v / (1.0 - ADAM_B2 ** ADAM_STEP)
    delta = -ADAM_LR * (m_hat / (jnp.sqrt(v_hat) + ADAM_EPS) + ADAM_WD * w)
    return delta, m, v


def _adamw(w, g_a, g_b, m, v, *, name):
    R, C = w.shape
    tr = _row_tile(R)
    two = g_b is not None

    def body(*refs):
        w_ref, ga_ref = refs[0], refs[1]
        rest = refs[2:]
        g = ga_ref[...]
        if two:
            g = g + rest[0][...]
            rest = rest[1:]
        m_ref, v_ref, g_out, d_out, m_out, v_out, token = rest
        delta, m_new, v_new = _adamw_math(w_ref[...], g, m_ref[...], v_ref[...])
        g_out[...] = g
        d_out[...] = delta
        m_out[...] = m_new
        v_out[...] = v_new
        token[...] = jnp.zeros_like(token)

    spec = pl.BlockSpec((tr, C), lambda i: (i, 0))
    ins = [w, g_a] + ([g_b] if two else []) + [m, v]
    return pl.pallas_call(
        body, name=name, grid=(R // tr,), in_specs=[spec] * len(ins),
        out_specs=[spec] * 4 + [pl.BlockSpec((8, LANE), lambda i: (0, 0))],
        out_shape=[_sds((R, C), F32)] * 4 + [_sds((8, LANE), F32)], compiler_params=_params(("arbitrary",)),
    )(*ins)


def _adamw_small(red, lb_full, ws, ms, vs):
    n = len(ws)

    def pieces(shape):
        out = []
        for j, idx in enumerate(itertools.product(*[range(d) for d in shape[:-1]])):
            out.append((idx[:-1] + (slice(idx[-1], idx[-1] + 1), slice(None)), j * shape[-1]))
        return out

    def body(*refs):
        red_ref, lb_ref = refs[0], refs[1]
        w_refs, m_refs, v_refs = refs[2:2 + n], refs[2 + n:2 + 2 * n], refs[2 + 2 * n:2 + 3 * n]
        out_refs = refs[2 + 3 * n:]
        chip = 2 * lax.axis_index("x") + lax.axis_index("y")
        n_f, shard = lb_ref.shape[-1], w_refs[n - 1].shape[-1]
        _, vjp = jax.vjp(_lower_fn, lb_ref[...])
        dlb = vjp(red_ref[8:10, 0:n_f])[0]
        for i in range(n):
            width = w_refs[i].shape[-1]
            for j, (at, lane) in enumerate(pieces(w_refs[i].shape)):
                if i < n - 1:
                    g = red_ref[i:i + 1, lane:lane + width]
                else:
                    row = dlb[j // 2][j % 2:j % 2 + 1]
                    g = sum(jnp.where(chip == q, row[:, q * shard:(q + 1) * shard], 0.0) for q in range(N_CHIPS))
                delta, m_new, v_new = _adamw_math(w_refs[i][at], g, m_refs[i][at], v_refs[i][at])
                for o_ref, val in zip(out_refs[4 * i:4 * i + 4], (g, delta, m_new, v_new)):
                    o_ref[at] = val

    return pl.pallas_call(
        body, name="adamw_small", out_shape=[_sds(w.shape, F32) for w in ws for _ in range(4)],
    )(red, lb_full, *ws, *ms, *vs)


def _chip_peers():
    x, y, c = lax.axis_index("x"), lax.axis_index("y"), lax.axis_index("c")
    return (x, y, c), 2 * x + y, [(1 - x, y), (x, 1 - y), (1 - x, 1 - y)]


_HBM = pl.BlockSpec(memory_space=pltpu.HBM)
_SEM = pl.BlockSpec(memory_space=pltpu.SEMAPHORE)
_EFFECT = pltpu.SideEffectType.DATAFLOW_SIDE_EFFECTING


def _exchange_copies(srcs, lands, sems, mode):
    (x, y, c), me, chips = _chip_peers()
    copies = []
    for t, (src, land) in enumerate(zip(srcs, lands)):
        if mode == "swap":
            copies.append(pltpu.make_async_remote_copy(src, land, sems[0].at[3 * t], sems[1].at[3 * t],
                                                       device_id=(x, y, 1 - c), device_id_type=MESH))
            continue
        for k, (px, py) in enumerate(chips):
            gather = mode == "gather"
            copies.append(pltpu.make_async_remote_copy(
                src if gather else src.at[2 * px + py], land.at[me] if gather else land.at[k],
                sems[0].at[3 * t + k], sems[1].at[3 * t + k], device_id=(px, py, c), device_id_type=MESH))
        if mode == "gather":
            copies.append(pltpu.make_async_copy(src, land.at[me], sems[2].at[t]))
    return copies


def _exchange_start(srcs, *, mode, name, after=None):
    n = len(srcs)
    n_sem = 3 if mode == "gather" else 2
    n_in = 2 * n + (after is not None)
    land_shape = {"gather": lambda s: (N_CHIPS,) + s.shape, "scatter": lambda s: (3,) + s.shape[1:], "swap": lambda s: s.shape}
    lands = [_sds(land_shape[mode](s), s.dtype) for s in srcs]

    def body(*refs):
        for cp in _exchange_copies(refs[:n], refs[n:2 * n], refs[n_in:n_in + n_sem], mode):
            cp.start()
        token = refs[-1]
        token[...] = jnp.zeros_like(token)

    sem_shapes = [pltpu.SemaphoreType.DMA((3 * n,)), pltpu.SemaphoreType.DMA((3 * n,))]
    sem_shapes += [pltpu.SemaphoreType.DMA((n,))] if mode == "gather" else []
    thru = [pltpu.HBM(s.shape, s.dtype) for s in srcs] + [pltpu.HBM(l.shape, l.dtype) for l in lands]
    res = pl.pallas_call(
        body, name=name, in_specs=[_HBM] * (2 * n) + [_ANY] * (after is not None),
        out_specs=[_SEM] * n_sem + [_HBM] * (2 * n) + [pl.BlockSpec(memory_space=pltpu.VMEM)],
        out_shape=sem_shapes + thru + [_sds((8, LANE), F32)], input_output_aliases={i: n_sem + i for i in range(2 * n)},
        compiler_params=pltpu.CompilerParams(has_side_effects=_EFFECT),
    )(*[pltpu.with_memory_space_constraint(s, pltpu.HBM) for s in srcs],
      *[pltpu.with_memory_space_constraint(lax.empty(l.shape, l.dtype), pltpu.HBM) for l in lands],
      *([after] if after is not None else []))
    return (res[:n_sem], res[n_sem:n_sem + n], res[n_sem + n:n_sem + 2 * n], mode), res[-1]


def _exchange_wait(started, after, *, name):
    sems, srcs, lands, mode = started
    n, n_sem = len(srcs), len(sems)
    after = list(after) if isinstance(after, (list, tuple)) else [after]

    def body(*refs):
        for cp in _exchange_copies(refs[:n], refs[n:2 * n], refs[2 * n:2 * n + n_sem], mode):
            cp.wait()

    res = pl.pallas_call(
        body, name=name, in_specs=[_HBM] * (2 * n) + [_SEM] * n_sem + [_ANY] * len(after), out_specs=[_HBM] * (2 * n),
        out_shape=[pltpu.HBM(a.shape, a.dtype) for a in list(srcs) + list(lands)],
        input_output_aliases={i: i for i in range(2 * n)},
        compiler_params=pltpu.CompilerParams(has_side_effects=_EFFECT),
    )(*srcs, *lands, *sems, *after)
    return res[:n], res[n:]


def _allreduce_small(pack, after):
    R, C = pack.shape

    def body(in_ref, after_ref, out_ref, slots, send_sems, recv_sems):
        x, y, c = lax.axis_index("x"), lax.axis_index("y"), lax.axis_index("c")
        me = 4 * x + 2 * y + c
        slots[me] = in_ref[...]
        copies = []
        for k in range(1, N_DEV):
            peer = (x ^ ((k >> 2) & 1), y ^ ((k >> 1) & 1), c ^ (k & 1))
            cp = pltpu.make_async_remote_copy(in_ref, slots.at[me], send_sems.at[k - 1], recv_sems.at[k - 1],
                                              device_id=peer, device_id_type=MESH)
            cp.start()
            copies.append(cp)
        for cp in copies:
            cp.wait()
        acc = slots[0]
        for d in range(1, N_DEV):
            acc = acc + slots[d]
        out_ref[...] = acc

    return pl.pallas_call(
        body, name="allreduce_small", out_shape=_sds((R, C), F32),
        in_specs=[pl.BlockSpec(memory_space=pltpu.VMEM), _ANY], out_specs=pl.BlockSpec(memory_space=pltpu.VMEM),
        scratch_shapes=[pltpu.VMEM((N_DEV, R, C), F32), pltpu.SemaphoreType.DMA((N_DEV - 1,)),
                        pltpu.SemaphoreType.DMA((N_DEV - 1,))],
        compiler_params=_params(),
    )(pack, after)


_Z_CQ, _Z_CKV, _Z_HQ, _Z_HFF, _Z_HFB, _Z_HI, _Z_HG, _Z_KR, _Z_END = 0, 256, 512, 1024, 1536, 2048, 2560, 3072, 3200


def _to_z_layout(wt):
    pad = jnp.zeros((_Z_END - _Z_KR - QK_ROPE, wt.shape[1]), wt.dtype)
    return jnp.concatenate([wt[:512], wt[512 + QK_ROPE:], wt[512:512 + QK_ROPE], pad], axis=0)


def _from_z_layout(wt):
    return jnp.concatenate([wt[:512], wt[_Z_KR:_Z_KR + QK_ROPE], wt[512:_Z_KR]], axis=0)


def _col_shards_to_full(g):
    return jnp.transpose(g, (1, 0, 2)).reshape(g.shape[1], -1)


def _full_to_col_shards(w):
    r, c = w.shape
    return jnp.transpose(w.reshape(r, N_CHIPS, c // N_CHIPS), (1, 0, 2))


def _full_to_row_shards(w):
    r, c = w.shape
    return w.reshape(N_CHIPS, r // N_CHIPS, c)


def kernel(x, p, positions, g_mix, w_in, g_qa, g_kva, w_qb, w_kvb, g_qn, g_kn, lb_param, g_hgo, w_o, g_ffn, w_gate, w_up, w_down, g_ple, w_ple_gate, w_ple_proj, loss_target, m_g_mix, m_w_in, m_g_qa, m_g_kva, m_w_qb, m_w_kvb, m_g_qn, m_g_kn, m_lb_param, m_g_hgo, m_w_o, m_g_ffn, m_w_gate, m_w_up, m_w_down, m_g_ple, m_w_ple_gate, m_w_ple_proj, v_g_mix, v_w_in, v_g_qa, v_g_kva, v_w_qb, v_w_kvb, v_g_qn, v_g_kn, v_lb_param, v_g_hgo, v_w_o, v_g_ffn, v_w_gate, v_w_up, v_w_down, v_g_ple, v_w_ple_gate, v_w_ple_proj):
    w_named = dict(g_mix=g_mix, w_in=w_in, g_qa=g_qa, g_kva=g_kva, w_qb=w_qb, w_kvb=w_kvb, g_qn=g_qn, g_kn=g_kn,
                   lb_param=lb_param, g_hgo=g_hgo, w_o=w_o, g_ffn=g_ffn, w_gate=w_gate, w_up=w_up, w_down=w_down,
                   g_ple=g_ple, w_ple_gate=w_ple_gate, w_ple_proj=w_ple_proj)
    m_named = dict(g_mix=m_g_mix, w_in=m_w_in, g_qa=m_g_qa, g_kva=m_g_kva, w_qb=m_w_qb, w_kvb=m_w_kvb, g_qn=m_g_qn,
                   g_kn=m_g_kn, lb_param=m_lb_param, g_hgo=m_g_hgo, w_o=m_w_o, g_ffn=m_g_ffn, w_gate=m_w_gate,
                   w_up=m_w_up, w_down=m_w_down, g_ple=m_g_ple, w_ple_gate=m_w_ple_gate, w_ple_proj=m_w_ple_proj)
    v_named = dict(g_mix=v_g_mix, w_in=v_w_in, g_qa=v_g_qa, g_kva=v_g_kva, w_qb=v_w_qb, w_kvb=v_w_kvb, g_qn=v_g_qn,
                   g_kn=v_g_kn, lb_param=v_lb_param, g_hgo=v_g_hgo, w_o=v_w_o, g_ffn=v_g_ffn, w_gate=v_w_gate,
                   w_up=v_w_up, w_down=v_w_down, g_ple=v_g_ple, w_ple_gate=v_w_ple_gate, w_ple_proj=v_w_ple_proj)
    order = list(w_named)
    transposed = ("w_in", "w_qb", "w_gate", "w_up")
    col_sharded = ("w_kvb", "w_ple_proj")
    row_sharded = ("w_o", "w_down", "w_ple_gate")
    big = transposed + col_sharded + row_sharded

    def view(n, a):
        return jnp.transpose(a[0]) if n in transposed else a[0]

    def unview(n, a):
        return (jnp.transpose(a) if n in transposed else a)[None]

    def to_shards(n, g):
        return _full_to_col_shards(g) if n in col_sharded else _full_to_row_shards(g)

    x2d, p2d, tgt = x[0], p[0, 0], loss_target[0]
    T, D = x2d.shape

    lb_flat = lb_param.reshape(-1, lb_param.shape[-1])
    gather_groups = (("w_in",), ("w_qb", "w_kvb"), ("w_o", "w_gate", "w_up", "w_down", "w_ple_gate", "w_ple_proj"))
    gather_started = []

    casts = {n: view(n, w_named[n]).astype(BF16) for n in big}

    def gather_start(gi, after):
        srcs = [casts[n] for n in gather_groups[gi]] + ([lb_flat] if gi == 0 else [])
        started, token = _exchange_start(srcs, mode="gather", name=f"gather_start_{gi}", after=after)
        gather_started.append(started)
        return token

    full = {}

    def gather_wait(gi, after):
        _, got = _exchange_wait(gather_started[gi], after, name=f"gather_wait_{gi}")
        for n, g in zip(gather_groups[gi], got):
            full[n] = _col_shards_to_full(g) if n in col_sharded else g.reshape(-1, g.shape[-1])
        return got

    g_hgo_row = g_hgo.reshape(1, -1)

    inv_freq = ROPE_THETA ** (-jnp.arange(0, QK_ROPE, 2, dtype=F32) / QK_ROPE)
    ang = positions[0].astype(F32)[:, None] * inv_freq
    cos, sin = jnp.cos(ang), jnp.sin(ang)
    token = gather_start(0, None)
    h1 = _stage(_rms, [x2d], [g_mix], [_sds((T, D), BF16)], [], name="norm_mix", after=token, tile=TOK_TILE_LIGHT)[0]
    got = gather_wait(0, [h1, cos, sin] + [casts[n] for g in gather_groups[1:] for n in g])
    token = got[0]
    for gi in range(1, len(gather_groups)):
        token = gather_start(gi, token)
    lb_full = _col_shards_to_full(got[-1]).reshape(lb_param.shape[0], lb_param.shape[1], -1)
    w_in_zt = _to_z_layout(full["w_in"])
    z = _mm(h1, w_in_zt, tb=True, name="in_proj", after=token)
    qn, kvn = _stage(_mla_a_fn, [_cols(z, 256, 0), _cols(z, 256, 1)], [g_qa, g_kva],
                     [_sds((T, 256), BF16), _sds((T, 256), BF16)], [], name="mla_latent_norm", tile=TOK_TILE_LIGHT)
    gather_wait(1, qn)
    q_raw = _mm(qn, full["w_qb"], tb=True, name="q_up")
    kv_raw = _mm(kvn, full["w_kvb"], name="kv_up")
    kr = _cols(z, LANE, _Z_KR // LANE)
    q, k, v = _stage(_mla_b_fn, [q_raw, kv_raw, kr, cos, sin], [g_qn, g_kn],
                     [_sds((MLA_HEADS, T, QK_HEAD), BF16), _sds((MLA_HEADS, T, QK_HEAD), BF16),
                      _sds((MLA_HEADS, T, V_HEAD), BF16)], [], name="mla_qk_norm_rope")
    att, lse = _attention_fwd(q, k, v)

    lower = _lower_bounds(lb_full)
    lower3 = lower.reshape(2, HG_HEADS, 1, HG_DK)
    o_f, st_f = _gla_fwd(z, lower3[0], rev=False, col_q=_Z_HQ, col_f=_Z_HFF, col_v=_Z_HI, hp=GLA_FWD_HEADS)
    o_b, st_b = _gla_fwd(z, lower3[1], rev=True, col_q=_Z_HQ, col_f=_Z_HFB, col_v=_Z_HI, hp=GLA_FWD_HEADS)
    hg = _cols(z, 512, _Z_HG // 512)
    mix = _stage(_post_fn, [att, o_f, o_b, hg], [g_hgo_row], [_sds((T, att.shape[1] + o_f.shape[1]), BF16)], [],
                 name="mix_out", tile=TOK_TILE_LIGHT)[0]
    gather_wait(2, mix)
    x1, h2 = _mm_fused(mix, full["w_o"], _resid_norm_fn, [x2d], [g_ffn], [F32, BF16], [], full_rows=True,
                       name="out_proj")
    gt, up, act = _mm_fused(h2, full["w_gate"], lambda gt, up: (gt, up, _swiglu_fn(gt, up)), [], [], [BF16, BF16, BF16],
                            [], tb=True, b2=full["w_up"], name="ffn_gate_up")
    x2, h3 = _mm_fused(act, full["w_down"], _resid_norm_fn, [x1], [g_ple], [F32, BF16], [], full_rows=True,
                       name="ffn_down")
    pp = _mm(p2d, full["w_ple_proj"], name="ple_proj")
    dx3, dpg, dpp, loss_part = _mm_fused(
        h3, full["w_ple_gate"], lambda acc, pp, x2, tgt: _ple_loss_fn(x2, acc, pp, tgt), [pp, x2, tgt], [],
        [F32, BF16, BF16], [_sds((1, 1), F32)], full_rows=True, name="ple_gate_loss")

    grads = {}
    scatter_groups = (("w_ple_proj", "w_ple_gate", "w_down", "w_gate", "w_up", "w_o"), ("w_qb", "w_kvb", "w_in"))
    scatter_started = []

    def scatter_start(gi):
        srcs = [to_shards(n, grads[n]) for n in scatter_groups[gi]]
        started, token = _exchange_start(srcs, mode="scatter", name=f"scatter_start_{gi}")
        scatter_started.append(started)
        return token

    chip = 2 * lax.axis_index("x") + lax.axis_index("y")
    swap_started = []

    def reduce_group(gi, after):
        shards, recvs = _exchange_wait(scatter_started[gi], after, name=f"scatter_wait_{gi}")
        sums = [_sum4(s, r, chip.reshape(1), name="sum_" + n) for n, s, r in zip(scatter_groups[gi], shards, recvs)]
        started, token = _exchange_start(sums, mode="swap", name=f"swap_start_{gi}")
        swap_started.append(started)
        return token

    grads["w_ple_proj"] = _mm(p2d, dpp, ta=True, out_dtype=BF16, name="d_w_ple_proj")
    grads["w_ple_gate"] = _mm(h3, dpg, ta=True, out_dtype=BF16, name="d_w_ple_gate")
    dx2, grads["g_ple"] = _mm_fused(
        dpg, full["w_ple_gate"], lambda acc, x2, dx3, g: _norm_bwd_fn(x2, acc, dx3, g), [x2, dx3], [g_ple],
        [F32], [_sds((1, D), F32)], tb=True, full_rows=True, name="d_h3_norm_ple_bwd")
    dgt, dup = _mm_fused(dx2, full["w_down"], lambda acc, gt, up: _swiglu_bwd_fn(gt.astype(F32), up.astype(F32), acc), [gt, up], [],
                         [BF16, BF16], [], tb=True, name="d_act_swiglu_bwd")
    grads["w_down"] = _mm(act, dx2, ta=True, out_dtype=BF16, name="d_w_down")
    grads["w_gate"] = _mm(dgt, h2, ta=True, out_dtype=BF16, name="d_w_gate")
    grads["w_up"] = _mm(dup, h2, ta=True, out_dtype=BF16, name="d_w_up")
    dx1, grads["g_ffn"] = _mm_fused(
        dgt, full["w_gate"], lambda acc, x1, dx2, g: _norm_bwd_fn(x1, acc, dx2, g), [x1, dx2], [g_ffn],
        [F32], [_sds((1, D), F32)], full_rows=True, a2=dup, b2=full["w_up"], name="d_h2_norm_ffn_bwd")
    grads["w_o"] = _mm(mix, dx1, ta=True, out_dtype=BF16, name="d_w_o")
    token = scatter_start(0)
    dmix = _mm(dx1, full["w_o"], tb=True, name="d_mix", after=token)

    half = MLA_HEADS * V_HEAD
    do, dhg, dg_hgo = _stage(_post_bwd_fn, [o_f, o_b, hg, _cols(dmix, half, 1)], [g_hgo_row],
                             [_sds((T, half), F32), _sds((T, half), BF16)], [_sds((1, half), F32)], name="mix_out_bwd",
                             tile=TOK_TILE_LIGHT)
    grads["g_hgo"] = dg_hgo
    dhq_f, dhi_f, dhf_f, dlow_f = _gla_bwd(z, lower3[0], st_f, do, None, rev=False,
                                           col_q=_Z_HQ, col_f=_Z_HFF, col_v=_Z_HI, hp=GLA_BWD_HEADS)
    dhq, dhi, dhf_b, dlow_b = _gla_bwd(z, lower3[1], st_b, do, (dhq_f, dhi_f), rev=True,
                                       col_q=_Z_HQ, col_f=_Z_HFB, col_v=_Z_HI, hp=GLA_BWD_HEADS)

    dq, dk, dv = _attention_bwd(q, k, v, att, lse, dmix)
    dq_raw, dkv_raw, dkr, grads["g_qn"], grads["g_kn"] = _stage(
        _mla_b_bwd_fn, [q_raw, kv_raw, kr, cos, sin, dq, dk, dv], [g_qn, g_kn],
        [_sds(q_raw.shape, BF16), _sds(kv_raw.shape, BF16), _sds((T, LANE), BF16)],
        [_sds(g_qn.shape, F32), _sds(g_kn.shape, F32)], name="mla_qk_norm_rope_bwd")
    grads["w_qb"] = _mm(dq_raw, qn, ta=True, out_dtype=BF16, name="d_w_qb")
    grads["w_kvb"] = _mm(kvn, dkv_raw, ta=True, out_dtype=BF16, name="d_w_kvb")
    dqn = _mm(dq_raw, full["w_qb"], name="d_qn")
    dkvn = _mm(dkv_raw, full["w_kvb"], tb=True, name="d_kvn")
    dcq, dckv, grads["g_qa"], grads["g_kva"] = _stage(
        _mla_a_bwd_fn, [_cols(z, 256, 0), _cols(z, 256, 1), dqn, dkvn], [g_qa, g_kva],
        [_sds((T, 256), BF16), _sds((T, 256), BF16)], [_sds(g_qa.shape, F32), _sds(g_kva.shape, F32)],
        name="mla_latent_norm_bwd", tile=TOK_TILE_LIGHT)
    token = reduce_group(0, dcq)
    dz = jnp.concatenate([dcq, dckv, dhq, dhf_f, dhf_b, dhi, dhg, dkr], axis=1)
    grads["w_in"] = _from_z_layout(_mm(dz, h1, ta=True, out_dtype=BF16, name="d_w_in", after=token))
    token = scatter_start(1)
    grad_x, grads["g_mix"] = _mm_fused(
        dz, w_in_zt, lambda acc, x, dx1, g: _norm_bwd_fn(x, acc, dx1, g), [x2d, dx1], [g_mix],
        [F32], [_sds((1, D), F32)], full_rows=True, name="d_h1_norm_mix_bwd", after=token)

    out_g, out_d, out_m, out_v = {}, {}, {}, {}

    def update_group(gi, after):
        mine, theirs = _exchange_wait(swap_started[gi], after, name=f"swap_wait_{gi}")
        tokens = []
        for n, a, b in zip(scatter_groups[gi], mine, theirs):
            *res, token = _adamw(view(n, w_named[n]), a, b, view(n, m_named[n]), view(n, v_named[n]), name="adamw_" + n)
            out_g[n], out_d[n], out_m[n], out_v[n] = (unview(n, t) for t in res)
            tokens.append(token)
        return tokens

    done = update_group(0, grads["g_mix"])
    token = reduce_group(1, done)
    done = update_group(1, token)

    small = ("g_mix", "g_qa", "g_kva", "g_qn", "g_kn", "g_hgo", "g_ffn", "g_ple")
    small_all = small + ("lb_param",)
    width = -(-max(w_named[n].size for n in small_all) // LANE) * LANE

    def row(a):
        a = a.reshape(1, -1)
        return jnp.pad(a, ((0, 0), (0, width - a.shape[1])))

    dlower = jnp.concatenate([dlow_f.reshape(1, -1), dlow_b.reshape(1, -1)], axis=0)
    pack = jnp.concatenate([row(grads[n]) for n in small] + [row(dlower[0]), row(dlower[1]), row(loss_part)]
                           + [jnp.zeros((5, width), F32)], axis=0)
    red = _allreduce_small(pack, done[-1])
    loss = red[10, 0]

    outs = _adamw_small(red, lb_full, [w_named[n] for n in small_all], [m_named[n] for n in small_all],
                        [v_named[n] for n in small_all])
    for i, n in enumerate(small_all):
        out_g[n], out_d[n], out_m[n], out_v[n] = outs[4 * i:4 * i + 4]

    return (loss, grad_x[None], *[out_g[n] for n in order], *[out_d[n] for n in order],
            *[out_m[n] for n in order], *[out_v[n] for n in order])
```

```python
import functools
import itertools

import jax
import jax.numpy as jnp
from jax import lax
from jax.experimental import pallas as pl
from jax.experimental.pallas import tpu as pltpu

F32 = jnp.float32
BF16 = jnp.bfloat16
MESH = pl.DeviceIdType.MESH

EPS = 1e-6
ROPE_THETA = 10000.0
MLA_HEADS = 4
QK_NOPE = 128
QK_ROPE = 64
QK_HEAD = QK_NOPE + QK_ROPE
V_HEAD = 128
HG_HEADS = 4
HG_DK = 128
CHUNK = 64
ADAM_LR = 0.001
ADAM_B1 = 0.9
ADAM_B2 = 0.999
ADAM_EPS = 1e-08
ADAM_WD = 0.01
ADAM_STEP = 10

LANE = 128
VMEM_LIMIT = 56 * 1024 * 1024
TOK_TILE = 512
TOK_TILE_LIGHT = 1024
GLA_GROUP = 16
GLA_FWD_HEADS = 4
GLA_BWD_HEADS = 2
ATT_TQ = 1024
ATT_TK = 1024
ATT_CHUNK = 1024
LOG2_E = 1.4426950408889634
N_CHIPS = 4
N_DEV = 8


_ANY = pl.BlockSpec(memory_space=pl.ANY)


def _params(dims=None, **kw):
    return pltpu.CompilerParams(dimension_semantics=dims, vmem_limit_bytes=VMEM_LIMIT, **kw)


def _tile_candidates(n, cap):
    out = [d for d in range(LANE, min(n, cap) + 1, LANE) if n % d == 0]
    if n <= cap and n not in out:
        out.append(n)
    return out or [n]


MM_VMEM_BUDGET = 40 * 1024 * 1024
MM_MIN_ROWS = 256
MM_MAX_ROWS = 1536
HBM_BYTES_PER_S = 2.8e12
MXU_FLOPS_PER_S = 8e14
STEP_OVERHEAD_S = 0.35e-6


def _mm_tiles(M, N, K, a_bytes, b_bytes, o_bytes, has_add, full_rows=False, full_k=False, n_a=1, n_b=1):
    cast_a, cast_b = a_bytes > 2, b_bytes > 2
    a_bytes, b_bytes = n_a * a_bytes, n_b * b_bytes
    best = None
    for tm in [t for t in _tile_candidates(M, MM_MAX_ROWS) if t >= min(M, MM_MIN_ROWS)]:
        for tn in ([N] if full_rows else _tile_candidates(N, N)):
            for tk in ([K] if full_k else _tile_candidates(K, K)):
                ni, nj, nk = M // tm, N // tn, K // tk
                vmem = 2 * (tm * tk * a_bytes + tk * tn * b_bytes + tm * tn * o_bytes * (2 if has_add else 1))
                vmem += tm * tn * 4 * (2 if nk > 1 else 1)
                vmem += (tm * tk * 2 * n_a if cast_a else 0) + (tk * tn * 2 * n_b if cast_b else 0)
                if vmem > MM_VMEM_BUDGET:
                    continue
                moved = M * K * a_bytes * (nj if nk > 1 else 1) + K * N * b_bytes * (1 if nj == nk == 1 else ni)
                moved += M * N * o_bytes * (2 if has_add else 1)
                t = max(moved / HBM_BYTES_PER_S, 2 * M * N * K / MXU_FLOPS_PER_S) + ni * nj * nk * STEP_OVERHEAD_S
                if best is None or t < best[0]:
                    best = (t, tm, tn, tk)
    assert best is not None, (M, N, K)
    return best[1:]


def _dot_raw(a, b, kind):
    nb = a.ndim - 2
    batch = ((0,), (0,)) if nb else ((), ())
    ca = nb if kind == "tn" else nb + 1
    cb = nb + 1 if kind == "nt" else nb
    return lax.dot_general(a.astype(BF16), b.astype(BF16), (((ca,), (cb,)), batch), preferred_element_type=F32)


@functools.partial(jax.custom_vjp, nondiff_argnums=(2,))
def _bdot(a, b, kind):
    return _dot_raw(a, b, kind)


def _bdot_fwd(a, b, kind):
    return _dot_raw(a, b, kind), (a, b)


def _bdot_bwd(kind, res, g):
    a, b = res
    if kind == "nn":
        da, db = _bdot(g, b, "nt"), _bdot(a, g, "tn")
    elif kind == "nt":
        da, db = _bdot(g, b, "nn"), _bdot(g, a, "tn")
    else:
        da, db = _bdot(b, g, "nt"), _bdot(a, g, "nn")
    return da.astype(a.dtype), db.astype(b.dtype)


_bdot.defvjp(_bdot_fwd, _bdot_bwd)


def _mm(a, b, *, name, ta=False, tb=False, add=None, out_dtype=F32, after=None):
    K, M = a.shape if ta else a.shape[::-1]
    N, Kb = b.shape if tb else b.shape[::-1]
    assert K == Kb, (a.shape, b.shape, ta, tb)
    tm, tn, tk = _mm_tiles(M, N, K, a.dtype.itemsize, b.dtype.itemsize, jnp.dtype(out_dtype).itemsize, add is not None)
    nk = K // tk
    kind = "tn" if ta else ("nt" if tb else "nn")
    assert not (ta and tb)
    a_spec = pl.BlockSpec((tk, tm), lambda i, j, k: (k, i)) if ta else pl.BlockSpec((tm, tk), lambda i, j, k: (i, k))
    b_spec = pl.BlockSpec((tn, tk), lambda i, j, k: (j, k)) if tb else pl.BlockSpec((tk, tn), lambda i, j, k: (k, j))
    o_spec = pl.BlockSpec((tm, tn), lambda i, j, k: (i, j))
    has_add = add is not None

    def body(*refs):
        a_ref, b_ref = refs[0], refs[1]
        add_ref = refs[2] if has_add else None
        o_ref = refs[n_in]
        part = _dot_raw(a_ref[...], b_ref[...], kind)
        if nk == 1:
            if has_add:
                part = part + add_ref[...].astype(F32)
            o_ref[...] = part.astype(o_ref.dtype)
            return
        acc_ref = refs[-1]
        k = pl.program_id(2)

        @pl.when(k == 0)
        def _():
            acc_ref[...] = part

        @pl.when(k > 0)
        def _():
            acc_ref[...] += part

        @pl.when(k == nk - 1)
        def _():
            r = acc_ref[...]
            if has_add:
                r = r + add_ref[...].astype(F32)
            o_ref[...] = r.astype(o_ref.dtype)

    ins = [a, b] + ([add] if has_add else []) + ([after] if after is not None else [])
    in_specs = [a_spec, b_spec] + ([o_spec] if has_add else []) + ([_ANY] if after is not None else [])
    n_in = len(ins)
    return pl.pallas_call(
        body, name=name, grid=(M // tm, N // tn, nk), in_specs=in_specs, out_specs=o_spec,
        out_shape=jax.ShapeDtypeStruct((M, N), out_dtype),
        scratch_shapes=[pltpu.VMEM((tm, tn), F32)] if nk > 1 else [],
        compiler_params=_params(("parallel", "parallel", "arbitrary")),
    )(*ins)


def _mm_fused(a, b, fn, tiles, params, out_dtypes, sums, *, name, ta=False, tb=False, full_rows=False, after=None,
              b2=None, a2=None):
    K, M = a.shape if ta else a.shape[::-1]
    N, Kb = b.shape if tb else b.shape[::-1]
    assert K == Kb and not (ta and tb), (a.shape, b.shape, ta, tb)
    per_elem = sum(t.dtype.itemsize for t in tiles) + sum(jnp.dtype(d).itemsize for d in out_dtypes)
    n_b = 1 if b2 is None else 2
    n_a = 1 if a2 is None else 2
    tm, tn, tk = _mm_tiles(M, N, K, a.dtype.itemsize, b.dtype.itemsize, per_elem, False, full_rows, b2 is not None,
                           n_a, n_b)
    nk = K // tk
    kind = "tn" if ta else ("nt" if tb else "nn")
    a_spec = pl.BlockSpec((tk, tm), lambda i, j, k: (k, i)) if ta else pl.BlockSpec((tm, tk), lambda i, j, k: (i, k))
    b_spec = pl.BlockSpec((tn, tk), lambda i, j, k: (j, k)) if tb else pl.BlockSpec((tk, tn), lambda i, j, k: (k, j))
    o_spec = pl.BlockSpec((tm, tn), lambda i, j, k: (i, j))
    ins = [a, b] + ([b2] if b2 is not None else []) + ([a2] if a2 is not None else [])
    ins += list(tiles) + list(params) + ([after] if after is not None else [])
    in_specs = [a_spec] + [b_spec] * n_b + [a_spec] * (n_a - 1) + [o_spec] * len(tiles)
    in_specs += [pl.BlockSpec(p.shape, lambda i, j, k, nd=p.ndim: (0,) * nd) for p in params]
    in_specs += [_ANY] if after is not None else []
    n_in, n_t, n_p, n_o = len(ins), len(tiles), len(params), len(out_dtypes)

    def body(*refs):
        outs, sum_refs = refs[n_in:n_in + n_o], refs[n_in + n_o:n_in + n_o + len(sums)]

        def finish(*products):
            res = fn(*products, *[t[...] for t in refs[n_a + n_b:n_a + n_b + n_t + n_p]])
            for o_ref, v in zip(outs, res[:n_o]):
                o_ref[...] = v.astype(o_ref.dtype)
            first = jnp.logical_and(pl.program_id(0) == 0, pl.program_id(1) == 0)
            for s_ref, v in zip(sum_refs, res[n_o:]):
                @pl.when(first)
                def _(s_ref=s_ref, v=v):
                    s_ref[...] = v

                @pl.when(jnp.logical_not(first))
                def _(s_ref=s_ref, v=v):
                    s_ref[...] += v

        part = _dot_raw(refs[0][...], refs[1][...], kind)
        if nk == 1 and a2 is not None:
            finish(part + _dot_raw(refs[3][...], refs[2][...], kind))
            return
        if nk == 1:
            finish(part, *([_dot_raw(refs[0][...], refs[2][...], kind)] if b2 is not None else []))
            return
        acc_ref = refs[-1]
        k = pl.program_id(2)

        @pl.when(k == 0)
        def _():
            acc_ref[...] = part

        @pl.when(k > 0)
        def _():
            acc_ref[...] += part

        @pl.when(k == nk - 1)
        def _():
            finish(acc_ref[...])

    out_shape = [_sds((M, N), d) for d in out_dtypes] + list(sums)
    out_specs = [o_spec] * n_o + [pl.BlockSpec(s.shape, lambda i, j, k, nd=len(s.shape): (0,) * nd) for s in sums]
    order = ("arbitrary",) * 3 if sums else ("parallel", "parallel", "arbitrary")
    return pl.pallas_call(
        body, name=name, grid=(M // tm, N // tn, nk), in_specs=in_specs, out_specs=out_specs, out_shape=out_shape,
        scratch_shapes=[pltpu.VMEM((tm, tn), F32)] if nk > 1 else [], compiler_params=_params(order),
    )(*ins)


def _cols(arr, width, block):
    return (arr, width, block)


def _stage(fn, tiles, params, out_tiles, out_sums, *, name, tile=TOK_TILE, after=None):
    first = tiles[0][0] if isinstance(tiles[0], tuple) else tiles[0]
    tile = min(tile, first.shape[0] if first.ndim == 2 else first.shape[1])

    def tok_spec(shape, width=None, block=0):
        if len(shape) == 2:
            w = shape[1] if width is None else width
            return pl.BlockSpec((tile, w), lambda i: (i, block))
        return pl.BlockSpec((shape[0], tile, shape[2]), lambda i: (0, i, 0))

    arrays, in_specs = [], []
    for t in tiles:
        if isinstance(t, tuple):
            arr, width, block = t
            arrays.append(arr)
            in_specs.append(tok_spec(arr.shape, width, block))
        else:
            arrays.append(t)
            in_specs.append(tok_spec(t.shape))
    n_tok = arrays[0].shape[0] if arrays[0].ndim == 2 else arrays[0].shape[1]
    for p in params:
        arrays.append(p)
        in_specs.append(pl.BlockSpec(p.shape, lambda i, nd=p.ndim: (0,) * nd))
    out_shape = list(out_tiles) + list(out_sums)
    out_specs = [tok_spec(o.shape) for o in out_tiles]
    out_specs += [pl.BlockSpec(o.shape, lambda i, nd=len(o.shape): (0,) * nd) for o in out_sums]
    n_fn, n_ot = len(arrays), len(out_tiles)
    if after is not None:
        arrays.append(after)
        in_specs.append(_ANY)
    n_in = len(arrays)

    def body(*refs):
        res = fn(*[r[...] for r in refs[:n_fn]])
        if not isinstance(res, (tuple, list)):
            res = (res,)
        outs = refs[n_in:]
        for o_ref, r in zip(outs[:n_ot], res[:n_ot]):
            o_ref[...] = r.astype(o_ref.dtype)
        i = pl.program_id(0)
        for o_ref, r in zip(outs[n_ot:], res[n_ot:]):
            @pl.when(i == 0)
            def _(o_ref=o_ref, r=r):
                o_ref[...] = r.astype(o_ref.dtype)

            @pl.when(i > 0)
            def _(o_ref=o_ref, r=r):
                o_ref[...] += r.astype(o_ref.dtype)

    res = pl.pallas_call(
        body, name=name, grid=(n_tok // tile,), in_specs=in_specs, out_specs=out_specs, out_shape=out_shape,
        compiler_params=_params(("arbitrary",)),
    )(*arrays)
    return res


def _sds(shape, dtype):
    return jax.ShapeDtypeStruct(tuple(shape), dtype)


def _sigmoid(x):
    return 0.5 * jnp.tanh(0.5 * x) + 0.5


def _rms(x, g):
    return x * lax.rsqrt(jnp.mean(x * x, axis=-1, keepdims=True) + EPS) * g


def _norm_bwd_fn(x, dh, dres, g):
    r = lax.rsqrt(jnp.mean(x * x, axis=-1, keepdims=True) + EPS)
    xr = x * r
    dhg = dh * g
    dx = r * (dhg - xr * jnp.mean(xr * dhg, axis=-1, keepdims=True))
    return dx + dres, jnp.sum(dh * xr, axis=0, keepdims=True)


def _mla_a_fn(cq, ckv, g_qa, g_kva):
    return _rms(cq, g_qa), _rms(ckv, g_kva)


def _mla_a_bwd_fn(cq, ckv, dqn, dkvn, g_qa, g_kva):
    _, vjp = jax.vjp(_mla_a_fn, cq, ckv, g_qa, g_kva)
    return vjp((dqn, dkvn))


def _rope(t, cos, sin):
    t1, t2 = t[:, :QK_ROPE // 2], t[:, QK_ROPE // 2:]
    return jnp.concatenate([t1 * cos - t2 * sin, t1 * sin + t2 * cos], axis=-1)


def _mla_b_fn(q_raw, kv_raw, kr, cos, sin, g_qn, g_kn):
    krope = kr[:, :QK_ROPE]
    qs, ks, vs = [], [], []
    for h in range(MLA_HEADS):
        qh = _rms(q_raw[:, h * QK_HEAD:(h + 1) * QK_HEAD], g_qn)
        kvh = kv_raw[:, h * (QK_NOPE + V_HEAD):(h + 1) * (QK_NOPE + V_HEAD)]
        kh = _rms(jnp.concatenate([kvh[:, :QK_NOPE], krope], axis=-1), g_kn)
        qs.append(jnp.concatenate([qh[:, :QK_NOPE], _rope(qh[:, QK_NOPE:], cos, sin)], axis=-1))
        ks.append(jnp.concatenate([kh[:, :QK_NOPE], _rope(kh[:, QK_NOPE:], cos, sin)], axis=-1))
        vs.append(kvh[:, QK_NOPE:])
    return jnp.stack(qs), jnp.stack(ks), jnp.stack(vs)


def _mla_b_bwd_fn(q_raw, kv_raw, kr, cos, sin, dq, dk, dv, g_qn, g_kn):
    _, vjp = jax.vjp(lambda a, b, c, d, e: _mla_b_fn(a, b, c, cos, sin, d, e), q_raw, kv_raw, kr, g_qn, g_kn)
    return vjp((dq, dk, dv))


def _post_fn(a, o_f, o_b, hg, g_hgo):
    o = o_f + o_b
    parts = [a]
    for h in range(HG_HEADS):
        s = slice(h * HG_DK, (h + 1) * HG_DK)
        gate = hg[:, s]
        parts.append(_rms(o[:, s], g_hgo[:, s]) * (gate * _sigmoid(gate)))
    return jnp.concatenate(parts, axis=-1)


def _post_bwd_fn(o_f, o_b, hg, dr, g_hgo):
    def f(o, hg, g):
        return _post_fn(jnp.zeros_like(o), o, jnp.zeros_like(o), hg, g)[:, o.shape[1]:]
    _, vjp = jax.vjp(f, o_f + o_b, hg, g_hgo)
    return vjp(dr)


def _swiglu_fn(gt, up):
    return gt * _sigmoid(gt) * up


def _resid_norm_fn(acc, x, g):
    x_new = acc + x
    return x_new, _rms(x_new, g)


def _swiglu_bwd_fn(gt, up, dact):
    s = _sigmoid(gt)
    silu = gt * s
    return dact * up * (s + silu * (1.0 - s)), dact * silu


def _ple_loss_fn(x2, pg, pp, target):
    gate = _sigmoid(pg)
    err = x2 + gate * pp - target
    dx3 = err * (1.0 / err.shape[-1])
    loss = 0.5 * jnp.sum(jnp.mean(err * err, axis=-1, keepdims=True), axis=0, keepdims=True)
    return dx3, dx3 * pp * gate * (1.0 - gate), dx3 * gate, loss


def _attention_fwd(q, k, v):
    H, T, D = q.shape
    DV = v.shape[-1]
    tq, ck = min(ATT_TQ, T), min(ATT_CHUNK, T)
    c2 = (D ** -0.5) * LOG2_E

    def body(q_ref, k_ref, v_ref, o_ref, lse_ref):
        q_i = q_ref[0]

        def chunk(c, carry):
            m, l, acc = carry
            rows = pl.ds(pl.multiple_of(c * ck, ck), ck)
            s = _dot_raw(q_i, k_ref[0, rows, :], "nt")
            m_new = jnp.maximum(m, jnp.max(s, axis=-1, keepdims=True))
            p = jnp.exp2((s - m_new) * c2)
            alpha = jnp.exp2((m - m_new) * c2)
            l = l * alpha + jnp.sum(p, axis=-1, keepdims=True)
            acc = acc * alpha + _dot_raw(p, v_ref[0, rows, :], "nn")
            return m_new, l, acc

        init = (jnp.full((tq, 1), -jnp.inf, F32), jnp.zeros((tq, 1), F32), jnp.zeros((tq, DV), F32))
        m, l, acc = lax.fori_loop(0, T // ck, chunk, init, unroll=True)
        o_ref[...] = acc / l
        lse_ref[0] = m * c2 + jnp.log2(l)

    return pl.pallas_call(
        body, name="attention_fwd", grid=(H, T // tq),
        in_specs=[pl.BlockSpec((1, tq, D), lambda h, i: (h, i, 0)),
                  pl.BlockSpec((1, T, D), lambda h, i: (h, 0, 0)),
                  pl.BlockSpec((1, T, DV), lambda h, i: (h, 0, 0))],
        out_specs=[pl.BlockSpec((tq, DV), lambda h, i: (i, h)),
                   pl.BlockSpec((1, tq, 1), lambda h, i: (h, i, 0))],
        out_shape=[_sds((T, H * DV), F32), _sds((H, T, 1), F32)],
        compiler_params=_params(("parallel", "parallel")),
    )(q, k, v)


def _attention_bwd(q, k, v, o, lse2, dmix):
    H, T, D = q.shape
    DV = v.shape[-1]
    tk, cq = min(ATT_TK, T), min(ATT_CHUNK, T)
    scale = D ** -0.5
    c2 = scale * LOG2_E

    def body(q_ref, k_ref, v_ref, o_ref, lse_ref, do_ref, dq_ref, dk_ref, dv_ref, delta_ref):
        j = pl.program_id(1)

        @pl.when(j == 0)
        def _():
            delta = lax.dot_general(jnp.ones((8, DV), F32), do_ref[...] * o_ref[...], (((1,), (1,)), ((), ())),
                                    precision=lax.Precision.HIGHEST, preferred_element_type=F32)
            for i in range(T // cq):
                delta_ref[i] = delta[:, i * cq:(i + 1) * cq]
            dq_ref[0] = jnp.zeros((T, D), F32)

        k_j, v_j = k_ref[0], v_ref[0]
        dk_ref[0] = jnp.zeros((tk, D), F32)
        dv_ref[0] = jnp.zeros((tk, DV), F32)

        def chunk(c, carry):
            rows = pl.ds(pl.multiple_of(c * cq, cq), cq)
            q_c = q_ref[0, rows, :]
            do_c = do_ref[rows, :].astype(BF16)
            st = _dot_raw(k_j, q_c, "nt")
            pt = jnp.exp2(st * c2 - lse_ref[0, c])
            dv_ref[0] += _dot_raw(pt, do_c, "nn")
            dpt = _dot_raw(v_j, do_c, "nt")
            dst = pt * (dpt - delta_ref[c, 0:1, :]) * scale
            dk_ref[0] += _dot_raw(dst, q_c, "nn")
            dq_ref[0, rows, :] += _dot_raw(dst, k_j, "tn")
            return carry

        lax.fori_loop(0, T // cq, chunk, 0, unroll=True)

    return pl.pallas_call(
        body, name="attention_bwd", grid=(H, T // tk),
        in_specs=[pl.BlockSpec((1, T, D), lambda h, j: (h, 0, 0)),
                  pl.BlockSpec((1, tk, D), lambda h, j: (h, j, 0)),
                  pl.BlockSpec((1, tk, DV), lambda h, j: (h, j, 0)),
                  pl.BlockSpec((T, DV), lambda h, j: (0, h)),
                  pl.BlockSpec((1, T // cq, 1, cq), lambda h, j: (h, 0, 0, 0)),
                  pl.BlockSpec((T, DV), lambda h, j: (0, h))],
        out_specs=[pl.BlockSpec((1, T, D), lambda h, j: (h, 0, 0)),
                   pl.BlockSpec((1, tk, D), lambda h, j: (h, j, 0)),
                   pl.BlockSpec((1, tk, DV), lambda h, j: (h, j, 0))],
        out_shape=[_sds((H, T, D), F32), _sds((H, T, D), F32), _sds((H, T, DV), F32)],
        scratch_shapes=[pltpu.VMEM((T // cq, 8, cq), F32)],
        compiler_params=_params(("parallel", "arbitrary")),
    )(q, k, v, o, lse2.reshape(H, T // cq, 1, cq), dmix)


def _split3_dot(ones, x, kind):
    hi = x.astype(BF16)
    rest = x - hi.astype(F32)
    mid = rest.astype(BF16)
    lo = (rest - mid.astype(F32)).astype(BF16)
    return (_dot_raw(ones, hi, kind) + _dot_raw(ones, mid, kind)) + _dot_raw(ones, lo, kind)


@jax.custom_vjp
def _running_sum(x, tri):
    return _split3_dot(tri, x, "nn")


def _running_sum_fwd(x, tri):
    return _split3_dot(tri, x, "nn"), tri


def _running_sum_bwd(tri, g):
    return _split3_dot(tri, g, "tn"), jnp.zeros_like(tri)


_running_sum.defvjp(_running_sum_fwd, _running_sum_bwd)


def _gla_block(hq, hf, hi, lower, st_in, *, rev, dot):
    rows, dk = hq.shape
    G, C = rows // CHUNK, CHUNK
    q = hq * _sigmoid(hq)
    f = lower + (1.0 - lower) * _sigmoid(hf)
    k = 1.0 - f
    logf = jnp.log2(f)
    q3, k3, v3, lf3 = (t.reshape(G, C, dk) for t in (q, k, hi, logf))
    r = lax.broadcasted_iota(jnp.int32, (C, C), 0)
    c = lax.broadcasted_iota(jnp.int32, (C, C), 1)
    tri = ((r <= c) if rev else (r >= c)).astype(F32)
    b = _running_sum(lf3, jnp.broadcast_to(tri, (G, C, C)))
    tpos = lax.broadcasted_iota(jnp.int32, (1, C, 1), 1)
    first_half = (tpos >= C // 2) if rev else (tpos <= C // 2 - 1)
    b_mid = jnp.sum(jnp.where(first_half, lf3, 0.0), axis=1, keepdims=True)
    b_last = jnp.sum(lf3, axis=1, keepdims=True)
    a = dot(q3 * jnp.exp2(b - b_mid), k3 * jnp.exp2(b_mid - b), "nt") * tri
    o_intra = dot(a, v3, "nn")
    kv_t = dot(v3, k3 * jnp.exp2(b_last - b), "tn")
    decay = jnp.exp2(b_last)
    qd = q3 * jnp.exp2(b)
    st = st_in
    o_inter = [None] * G
    for g in (reversed(range(G)) if rev else range(G)):
        o_inter[g] = dot(qd[g], st, "nt")
        st = st * decay[g] + kv_t[g]
    o = o_intra.reshape(rows, dk) + jnp.concatenate(o_inter, axis=0)
    return o, st


def _gla_fwd(z, lower3, *, rev, col_q, col_f, col_v, hp):
    T = z.shape[0]
    rows = min(GLA_GROUP * CHUNK, T)
    nb = T // rows
    wide = hp * HG_DK
    blk = (lambda n: nb - 1 - n) if rev else (lambda n: n)

    def body(hq_ref, hf_ref, hi_ref, low_ref, o_ref, st_out_ref, st_ref):
        @pl.when(pl.program_id(1) == 0)
        def _():
            st_ref[...] = jnp.zeros_like(st_ref)

        st_in = [st_ref[i] for i in range(hp)]
        heads = []
        for i in range(hp):
            cols = slice(i * HG_DK, (i + 1) * HG_DK)
            heads.append(_gla_block(hq_ref[:, cols], hf_ref[:, cols], hi_ref[:, cols], low_ref[i], st_in[i], rev=rev,
                                    dot=_dot_raw))
        for i, (o, st) in enumerate(heads):
            st_out_ref[i, 0] = st_in[i]
            o_ref[:, i * HG_DK:(i + 1) * HG_DK] = o
            st_ref[i] = st

    def zspec(col):
        return pl.BlockSpec((rows, wide), lambda h, n: (blk(n), col // wide + h))

    return pl.pallas_call(
        body, name="gla_fwd_rev" if rev else "gla_fwd", grid=(HG_HEADS // hp, nb),
        in_specs=[zspec(col_q), zspec(col_f), zspec(col_v), pl.BlockSpec((hp, 1, HG_DK), lambda h, n: (h, 0, 0))],
        out_specs=[pl.BlockSpec((rows, wide), lambda h, n: (blk(n), h)),
                   pl.BlockSpec((hp, 1, HG_DK, HG_DK), lambda h, n: (h, blk(n), 0, 0))],
        out_shape=[_sds((T, HG_HEADS * HG_DK), F32), _sds((HG_HEADS, nb, HG_DK, HG_DK), F32)],
        scratch_shapes=[pltpu.VMEM((hp, HG_DK, HG_DK), F32)],
        compiler_params=_params(("parallel", "arbitrary")),
    )(z, z, z, lower3)


def _gla_bwd(z, lower3, states, do, prev, *, rev, col_q, col_f, col_v, hp):
    T = z.shape[0]
    rows = min(GLA_GROUP * CHUNK, T)
    nb = T // rows
    wide = hp * HG_DK
    blk = (lambda n: n) if rev else (lambda n: nb - 1 - n)
    has_prev = prev is not None
    fn = functools.partial(_gla_block, rev=rev, dot=_bdot)

    def body(*refs):
        hq_ref, hf_ref, hi_ref, low_ref, st_ref, do_ref = refs[:6]
        rest = refs[6:]
        if has_prev:
            pq_ref, pi_ref = rest[:2]
            rest = rest[2:]
        dhq_ref, dhi_ref, dhf_ref, dlow_ref, dst_ref = rest
        n = pl.program_id(1)

        @pl.when(n == 0)
        def _():
            dst_ref[...] = jnp.zeros_like(dst_ref)

        dst_in = [dst_ref[i] for i in range(hp)]
        heads = []
        for i in range(hp):
            cols = slice(i * HG_DK, (i + 1) * HG_DK)
            _, vjp = jax.vjp(fn, hq_ref[:, cols], hf_ref[:, cols], hi_ref[:, cols], low_ref[i], st_ref[i, 0])
            dhq, dhf, dhi, dlow, dst = vjp((do_ref[:, cols], dst_in[i]))
            if has_prev:
                dhq = dhq + pq_ref[:, cols]
                dhi = dhi + pi_ref[:, cols]
            heads.append((dhq, dhf, dhi, dlow, dst))
        for i, (dhq, dhf, dhi, dlow, dst) in enumerate(heads):
            cols = slice(i * HG_DK, (i + 1) * HG_DK)
            dst_ref[i] = dst
            dhq_ref[:, cols] = dhq.astype(dhq_ref.dtype)
            dhi_ref[:, cols] = dhi.astype(dhi_ref.dtype)
            dhf_ref[:, cols] = dhf.astype(dhf_ref.dtype)

        @pl.when(n == 0)
        def _():
            for i in range(hp):
                dlow_ref[i] = heads[i][3]

        @pl.when(n > 0)
        def _():
            for i in range(hp):
                dlow_ref[i] += heads[i][3]

    def zspec(col):
        return pl.BlockSpec((rows, wide), lambda h, n: (blk(n), col // wide + h))

    hspec = pl.BlockSpec((rows, wide), lambda h, n: (blk(n), h))
    in_specs = [zspec(col_q), zspec(col_f), zspec(col_v), pl.BlockSpec((hp, 1, HG_DK), lambda h, n: (h, 0, 0)),
                pl.BlockSpec((hp, 1, HG_DK, HG_DK), lambda h, n: (h, blk(n), 0, 0)), hspec]
    ins = [z, z, z, lower3, states, do]
    if has_prev:
        in_specs += [hspec, hspec]
        ins += list(prev)
    full_wide = HG_HEADS * HG_DK
    acc_dtype = BF16 if has_prev else F32
    return pl.pallas_call(
        body, name="gla_bwd_rev" if rev else "gla_bwd", grid=(HG_HEADS // hp, nb),
        in_specs=in_specs,
        out_specs=[hspec, hspec, hspec, pl.BlockSpec((hp, 1, HG_DK), lambda h, n: (h, 0, 0))],
        out_shape=[_sds((T, full_wide), acc_dtype), _sds((T, full_wide), acc_dtype), _sds((T, full_wide), BF16),
                   _sds((HG_HEADS, 1, HG_DK), F32)],
        scratch_shapes=[pltpu.VMEM((hp, HG_DK, HG_DK), F32)],
        compiler_params=_params(("parallel", "arbitrary")),
    )(*ins)


def _lower_fn(lb):
    e = jnp.exp(lb - jnp.max(lb, axis=0, keepdims=True))
    return (e / jnp.sum(e, axis=0, keepdims=True))[0]


def _lower_bounds(lb):
    def body(lb_ref, o_ref):
        o_ref[...] = _lower_fn(lb_ref[...])
    return pl.pallas_call(body, name="lower_bounds", out_shape=_sds(lb.shape[1:], F32))(lb)


def _row_tile(r, cap=1024):
    best = None
    for t in range(16, min(r, cap) + 1, 16):
        if r % t == 0:
            best = t
    return best if best is not None else r


def _sum4(shards, recv, chip, *, name):
    _, R, C = shards.shape
    tr = _row_tile(R)

    def body(chip_ref, o_ref, r_ref, out_ref):
        out_ref[...] = ((o_ref[0].astype(F32) + r_ref[0].astype(F32)) + r_ref[1].astype(F32)) + r_ref[2].astype(F32)

    grid_spec = pltpu.PrefetchScalarGridSpec(
        num_scalar_prefetch=1, grid=(R // tr,),
        in_specs=[pl.BlockSpec((1, tr, C), lambda i, chip_ref: (chip_ref[0], i, 0)),
                  pl.BlockSpec((3, tr, C), lambda i, chip_ref: (0, i, 0))],
        out_specs=pl.BlockSpec((tr, C), lambda i, chip_ref: (i, 0)))
    return pl.pallas_call(
        body, name=name, grid_spec=grid_spec, out_shape=_sds((R, C), F32), compiler_params=_params(("parallel",)),
    )(chip, shards, recv)


def _adamw_math(w, g, m, v):
    m = ADAM_B1 * m + (1.0 - ADAM_B1) * g
    v = ADAM_B2 * v + (1.0 - ADAM_B2) * (g * g)
    m_hat = m / (1.0 - ADAM_B1 ** ADAM_STEP)
    v_hat = v / (1.0 - ADAM_B2 ** ADAM_STEP)
    delta = -ADAM_LR * (m_hat / (jnp.sqrt(v_hat) + ADAM_EPS) + ADAM_WD * w)
    return delta, m, v


def _adamw(w, g_a, g_b, m, v, *, name):
    R, C = w.shape
    tr = _row_tile(R)
    two = g_b is not None

    def body(*refs):
        w_ref, ga_ref = refs[0], refs[1]
        rest = refs[2:]
        g = ga_ref[...]
        if two:
            g = g + rest[0][...]
            rest = rest[1:]
        m_ref, v_ref, g_out, d_out, m_out, v_out, token = rest
        delta, m_new, v_new = _adamw_math(w_ref[...], g, m_ref[...], v_ref[...])
        g_out[...] = g
        d_out[...] = delta
        m_out[...] = m_new
        v_out[...] = v_new
        token[...] = jnp.zeros_like(token)

    spec = pl.BlockSpec((tr, C), lambda i: (i, 0))
    ins = [w, g_a] + ([g_b] if two else []) + [m, v]
    return pl.pallas_call(
        body, name=name, grid=(R // tr,), in_specs=[spec] * len(ins),
        out_specs=[spec] * 4 + [pl.BlockSpec((8, LANE), lambda i: (0, 0))],
        out_shape=[_sds((R, C), F32)] * 4 + [_sds((8, LANE), F32)], compiler_params=_params(("arbitrary",)),
    )(*ins)


def _adamw_small(red, lb_full, ws, ms, vs):
    n = len(ws)

    def pieces(shape):
        out = []
        for j, idx in enumerate(itertools.product(*[range(d) for d in shape[:-1]])):
            out.append((idx[:-1] + (slice(idx[-1], idx[-1] + 1), slice(None)), j * shape[-1]))
        return out

    def body(*refs):
        red_ref, lb_ref = refs[0], refs[1]
        w_refs, m_refs, v_refs = refs[2:2 + n], refs[2 + n:2 + 2 * n], refs[2 + 2 * n:2 + 3 * n]
        out_refs = refs[2 + 3 * n:]
        chip = 2 * lax.axis_index("x") + lax.axis_index("y")
        n_f, shard = lb_ref.shape[-1], w_refs[n - 1].shape[-1]
        _, vjp = jax.vjp(_lower_fn, lb_ref[...])
        dlb = vjp(red_ref[8:10, 0:n_f])[0]
        for i in range(n):
            width = w_refs[i].shape[-1]
            for j, (at, lane) in enumerate(pieces(w_refs[i].shape)):
                if i < n - 1:
                    g = red_ref[i:i + 1, lane:lane + width]
                else:
                    row = dlb[j // 2][j % 2:j % 2 + 1]
                    g = sum(jnp.where(chip == q, row[:, q * shard:(q + 1) * shard], 0.0) for q in range(N_CHIPS))
                delta, m_new, v_new = _adamw_math(w_refs[i][at], g, m_refs[i][at], v_refs[i][at])
                for o_ref, val in zip(out_refs[4 * i:4 * i + 4], (g, delta, m_new, v_new)):
                    o_ref[at] = val

    return pl.pallas_call(
        body, name="adamw_small", out_shape=[_sds(w.shape, F32) for w in ws for _ in range(4)],
    )(red, lb_full, *ws, *ms, *vs)


def _chip_peers():
    x, y, c = lax.axis_index("x"), lax.axis_index("y"), lax.axis_index("c")
    return (x, y, c), 2 * x + y, [(1 - x, y), (x, 1 - y), (1 - x, 1 - y)]


_HBM = pl.BlockSpec(memory_space=pltpu.HBM)
_SEM = pl.BlockSpec(memory_space=pltpu.SEMAPHORE)
_EFFECT = pltpu.SideEffectType.DATAFLOW_SIDE_EFFECTING


def _exchange_copies(srcs, lands, sems, mode):
    (x, y, c), me, chips = _chip_peers()
    copies = []
    for t, (src, land) in enumerate(zip(srcs, lands)):
        if mode == "swap":
            copies.append(pltpu.make_async_remote_copy(src, land, sems[0].at[3 * t], sems[1].at[3 * t],
                                                       device_id=(x, y, 1 - c), device_id_type=MESH))
            continue
        for k, (px, py) in enumerate(chips):
            gather = mode == "gather"
            copies.append(pltpu.make_async_remote_copy(
                src if gather else src.at[2 * px + py], land.at[me] if gather else land.at[k],
                sems[0].at[3 * t + k], sems[1].at[3 * t + k], device_id=(px, py, c), device_id_type=MESH))
        if mode == "gather":
            copies.append(pltpu.make_async_copy(src, land.at[me], sems[2].at[t]))
    return copies


def _exchange_start(srcs, *, mode, name, after=None):
    n = len(srcs)
    n_sem = 3 if mode == "gather" else 2
    n_in = 2 * n + (after is not None)
    land_shape = {"gather": lambda s: (N_CHIPS,) + s.shape, "scatter": lambda s: (3,) + s.shape[1:], "swap": lambda s: s.shape}
    lands = [_sds(land_shape[mode](s), s.dtype) for s in srcs]

    def body(*refs):
        for cp in _exchange_copies(refs[:n], refs[n:2 * n], refs[n_in:n_in + n_sem], mode):
            cp.start()
        token = refs[-1]
        token[...] = jnp.zeros_like(token)

    sem_shapes = [pltpu.SemaphoreType.DMA((3 * n,)), pltpu.SemaphoreType.DMA((3 * n,))]
    sem_shapes += [pltpu.SemaphoreType.DMA((n,))] if mode == "gather" else []
    thru = [pltpu.HBM(s.shape, s.dtype) for s in srcs] + [pltpu.HBM(l.shape, l.dtype) for l in lands]
    res = pl.pallas_call(
        body, name=name, in_specs=[_HBM] * (2 * n) + [_ANY] * (after is not None),
        out_specs=[_SEM] * n_sem + [_HBM] * (2 * n) + [pl.BlockSpec(memory_space=pltpu.VMEM)],
        out_shape=sem_shapes + thru + [_sds((8, LANE), F32)], input_output_aliases={i: n_sem + i for i in range(2 * n)},
        compiler_params=pltpu.CompilerParams(has_side_effects=_EFFECT),
    )(*[pltpu.with_memory_space_constraint(s, pltpu.HBM) for s in srcs],
      *[pltpu.with_memory_space_constraint(lax.empty(l.shape, l.dtype), pltpu.HBM) for l in lands],
      *([after] if after is not None else []))
    return (res[:n_sem], res[n_sem:n_sem + n], res[n_sem + n:n_sem + 2 * n], mode), res[-1]


def _exchange_wait(started, after, *, name):
    sems, srcs, lands, mode = started
    n, n_sem = len(srcs), len(sems)
    after = list(after) if isinstance(after, (list, tuple)) else [after]

    def body(*refs):
        for cp in _exchange_copies(refs[:n], refs[n:2 * n], refs[2 * n:2 * n + n_sem], mode):
            cp.wait()

    res = pl.pallas_call(
        body, name=name, in_specs=[_HBM] * (2 * n) + [_SEM] * n_sem + [_ANY] * len(after), out_specs=[_HBM] * (2 * n),
        out_shape=[pltpu.HBM(a.shape, a.dtype) for a in list(srcs) + list(lands)],
        input_output_aliases={i: i for i in range(2 * n)},
        compiler_params=pltpu.CompilerParams(has_side_effects=_EFFECT),
    )(*srcs, *lands, *sems, *after)
    return res[:n], res[n:]


def _allreduce_small(pack, after):
    R, C = pack.shape

    def body(in_ref, after_ref, out_ref, slots, send_sems, recv_sems):
        x, y, c = lax.axis_index("x"), lax.axis_index("y"), lax.axis_index("c")
        me = 4 * x + 2 * y + c
        slots[me] = in_ref[...]
        copies = []
        for k in range(1, N_DEV):
            peer = (x ^ ((k >> 2) & 1), y ^ ((k >> 1) & 1), c ^ (k & 1))
            cp = pltpu.make_async_remote_copy(in_ref, slots.at[me], send_sems.at[k - 1], recv_sems.at[k - 1],
                                              device_id=peer, device_id_type=MESH)
            cp.start()
            copies.append(cp)
        for cp in copies:
            cp.wait()
        acc = slots[0]
        for d in range(1, N_DEV):
            acc = acc + slots[d]
        out_ref[...] = acc

    return pl.pallas_call(
        body, name="allreduce_small", out_shape=_sds((R, C), F32),
        in_specs=[pl.BlockSpec(memory_space=pltpu.VMEM), _ANY], out_specs=pl.BlockSpec(memory_space=pltpu.VMEM),
        scratch_shapes=[pltpu.VMEM((N_DEV, R, C), F32), pltpu.SemaphoreType.DMA((N_DEV - 1,)),
                        pltpu.SemaphoreType.DMA((N_DEV - 1,))],
        compiler_params=_params(),
    )(pack, after)


_Z_CQ, _Z_CKV, _Z_HQ, _Z_HFF, _Z_HFB, _Z_HI, _Z_HG, _Z_KR, _Z_END = 0, 256, 512, 1024, 1536, 2048, 2560, 3072, 3200


def _to_z_layout(wt):
    pad = jnp.zeros((_Z_END - _Z_KR - QK_ROPE, wt.shape[1]), wt.dtype)
    return jnp.concatenate([wt[:512], wt[512 + QK_ROPE:], wt[512:512 + QK_ROPE], pad], axis=0)


def _from_z_layout(wt):
    return jnp.concatenate([wt[:512], wt[_Z_KR:_Z_KR + QK_ROPE], wt[512:_Z_KR]], axis=0)


def _col_shards_to_full(g):
    return jnp.transpose(g, (1, 0, 2)).reshape(g.shape[1], -1)


def _full_to_col_shards(w):
    r, c = w.shape
    return jnp.transpose(w.reshape(r, N_CHIPS, c // N_CHIPS), (1, 0, 2))


def _full_to_row_shards(w):
    r, c = w.shape
    return w.reshape(N_CHIPS, r // N_CHIPS, c)


def kernel(x, p, positions, g_mix, w_in, g_qa, g_kva, w_qb, w_kvb, g_qn, g_kn, lb_param, g_hgo, w_o, g_ffn, w_gate, w_up, w_down, g_ple, w_ple_gate, w_ple_proj, loss_target, m_g_mix, m_w_in, m_g_qa, m_g_kva, m_w_qb, m_w_kvb, m_g_qn, m_g_kn, m_lb_param, m_g_hgo, m_w_o, m_g_ffn, m_w_gate, m_w_up, m_w_down, m_g_ple, m_w_ple_gate, m_w_ple_proj, v_g_mix, v_w_in, v_g_qa, v_g_kva, v_w_qb, v_w_kvb, v_g_qn, v_g_kn, v_lb_param, v_g_hgo, v_w_o, v_g_ffn, v_w_gate, v_w_up, v_w_down, v_g_ple, v_w_ple_gate, v_w_ple_proj):
    w_named = dict(g_mix=g_mix, w_in=w_in, g_qa=g_qa, g_kva=g_kva, w_qb=w_qb, w_kvb=w_kvb, g_qn=g_qn, g_kn=g_kn,
                   lb_param=lb_param, g_hgo=g_hgo, w_o=w_o, g_ffn=g_ffn, w_gate=w_gate, w_up=w_up, w_down=w_down,
                   g_ple=g_ple, w_ple_gate=w_ple_gate, w_ple_proj=w_ple_proj)
    m_named = dict(g_mix=m_g_mix, w_in=m_w_in, g_qa=m_g_qa, g_kva=m_g_kva, w_qb=m_w_qb, w_kvb=m_w_kvb, g_qn=m_g_qn,
                   g_kn=m_g_kn, lb_param=m_lb_param, g_hgo=m_g_hgo, w_o=m_w_o, g_ffn=m_g_ffn, w_gate=m_w_gate,
                   w_up=m_w_up, w_down=m_w_down, g_ple=m_g_ple, w_ple_gate=m_w_ple_gate, w_ple_proj=m_w_ple_proj)
    v_named = dict(g_mix=v_g_mix, w_in=v_w_in, g_qa=v_g_qa, g_kva=v_g_kva, w_qb=v_w_qb, w_kvb=v_w_kvb, g_qn=v_g_qn,
                   g_kn=v_g_kn, lb_param=v_lb_param, g_hgo=v_g_hgo, w_o=v_w_o, g_ffn=v_g_ffn, w_gate=v_w_gate,
                   w_up=v_w_up, w_down=v_w_down, g_ple=v_g_ple, w_ple_gate=v_w_ple_gate, w_ple_proj=v_w_ple_proj)
    order = list(w_named)
    transposed = ("w_in", "w_qb", "w_gate", "w_up")
    col_sharded = ("w_kvb", "w_ple_proj")
    row_sharded = ("w_o", "w_down", "w_ple_gate")
    big = transposed + col_sharded + row_sharded

    def view(n, a):
        return jnp.transpose(a[0]) if n in transposed else a[0]

    def unview(n, a):
        return (jnp.transpose(a) if n in transposed else a)[None]

    def to_shards(n, g):
        return _full_to_col_shards(g) if n in col_sharded else _full_to_row_shards(g)

    x2d, p2d, tgt = x[0], p[0, 0], loss_target[0]
    T, D = x2d.shape

    lb_flat = lb_param.reshape(-1, lb_param.shape[-1])
    gather_groups = (("w_in",), ("w_qb", "w_kvb"), ("w_o", "w_gate", "w_up", "w_down", "w_ple_gate", "w_ple_proj"))
    gather_started = []

    casts = {n: view(n, w_named[n]).astype(BF16) for n in big}

    def gather_start(gi, after):
        srcs = [casts[n] for n in gather_groups[gi]] + ([lb_flat] if gi == 0 else [])
        started, token = _exchange_start(srcs, mode="gather", name=f"gather_start_{gi}", after=after)
        gather_started.append(started)
        return token

    full = {}

    def gather_wait(gi, after):
        _, got = _exchange_wait(gather_started[gi], after, name=f"gather_wait_{gi}")
        for n, g in zip(gather_groups[gi], got):
            full[n] = _col_shards_to_full(g) if n in col_sharded else g.reshape(-1, g.shape[-1])
        return got

    g_hgo_row = g_hgo.reshape(1, -1)

    inv_freq = ROPE_THETA ** (-jnp.arange(0, QK_ROPE, 2, dtype=F32) / QK_ROPE)
    ang = positions[0].astype(F32)[:, None] * inv_freq
    cos, sin = jnp.cos(ang), jnp.sin(ang)
    token = gather_start(0, None)
    h1 = _stage(_rms, [x2d], [g_mix], [_sds((T, D), BF16)], [], name="norm_mix", after=token, tile=TOK_TILE_LIGHT)[0]
    got = gather_wait(0, [h1, cos, sin] + [casts[n] for g in gather_groups[1:] for n in g])
    token = got[0]
    for gi in range(1, len(gather_groups)):
        token = gather_start(gi, token)
    lb_full = _col_shards_to_full(got[-1]).reshape(lb_param.shape[0], lb_param.shape[1], -1)
    w_in_zt = _to_z_layout(full["w_in"])
    z = _mm(h1, w_in_zt, tb=True, name="in_proj", after=token)
    qn, kvn = _stage(_mla_a_fn, [_cols(z, 256, 0), _cols(z, 256, 1)], [g_qa, g_kva],
                     [_sds((T, 256), BF16), _sds((T, 256), BF16)], [], name="mla_latent_norm", tile=TOK_TILE_LIGHT)
    gather_wait(1, qn)
    q_raw = _mm(qn, full["w_qb"], tb=True, name="q_up")
    kv_raw = _mm(kvn, full["w_kvb"], name="kv_up")
    kr = _cols(z, LANE, _Z_KR // LANE)
    q, k, v = _stage(_mla_b_fn, [q_raw, kv_raw, kr, cos, sin], [g_qn, g_kn],
                     [_sds((MLA_HEADS, T, QK_HEAD), BF16), _sds((MLA_HEADS, T, QK_HEAD), BF16),
                      _sds((MLA_HEADS, T, V_HEAD), BF16)], [], name="mla_qk_norm_rope", tile=TOK_TILE_LIGHT)
    att, lse = _attention_fwd(q, k, v)

    lower = _lower_bounds(lb_full)
    lower3 = lower.reshape(2, HG_HEADS, 1, HG_DK)
    o_f, st_f = _gla_fwd(z, lower3[0], rev=False, col_q=_Z_HQ, col_f=_Z_HFF, col_v=_Z_HI, hp=GLA_FWD_HEADS)
    o_b, st_b = _gla_fwd(z, lower3[1], rev=True, col_q=_Z_HQ, col_f=_Z_HFB, col_v=_Z_HI, hp=GLA_FWD_HEADS)
    hg = _cols(z, 512, _Z_HG // 512)
    mix = _stage(_post_fn, [att, o_f, o_b, hg], [g_hgo_row], [_sds((T, att.shape[1] + o_f.shape[1]), BF16)], [],
                 name="mix_out", tile=TOK_TILE_LIGHT)[0]
    gather_wait(2, mix)
    x1, h2 = _mm_fused(mix, full["w_o"], _resid_norm_fn, [x2d], [g_ffn], [F32, BF16], [], full_rows=True,
                       name="out_proj")
    gt, up, act = _mm_fused(h2, full["w_gate"], lambda gt, up: (gt, up, _swiglu_fn(gt, up)), [], [], [BF16, BF16, BF16],
                            [], tb=True, b2=full["w_up"], name="ffn_gate_up")
    x2, h3 = _mm_fused(act, full["w_down"], _resid_norm_fn, [x1], [g_ple], [F32, BF16], [], full_rows=True,
                       name="ffn_down")
    pp = _mm(p2d, full["w_ple_proj"], name="ple_proj")
    dx3, dpg, dpp, loss_part = _mm_fused(
        h3, full["w_ple_gate"], lambda acc, pp, x2, tgt: _ple_loss_fn(x2, acc, pp, tgt), [pp, x2, tgt], [],
        [F32, BF16, BF16], [_sds((1, 1), F32)], full_rows=True, name="ple_gate_loss")

    grads = {}
    scatter_groups = (("w_ple_proj", "w_ple_gate", "w_down", "w_gate", "w_up", "w_o"), ("w_qb", "w_kvb", "w_in"))
    scatter_started = []

    def scatter_start(gi):
        srcs = [to_shards(n, grads[n]) for n in scatter_groups[gi]]
        started, token = _exchange_start(srcs, mode="scatter", name=f"scatter_start_{gi}")
        scatter_started.append(started)
        return token

    chip = 2 * lax.axis_index("x") + lax.axis_index("y")
    swap_started = []

    def reduce_group(gi, after):
        shards, recvs = _exchange_wait(scatter_started[gi], after, name=f"scatter_wait_{gi}")
        sums = [_sum4(s, r, chip.reshape(1), name="sum_" + n) for n, s, r in zip(scatter_groups[gi], shards, recvs)]
        started, token = _exchange_start(sums, mode="swap", name=f"swap_start_{gi}")
        swap_started.append(started)
        return token

    grads["w_ple_proj"] = _mm(p2d, dpp, ta=True, out_dtype=BF16, name="d_w_ple_proj")
    grads["w_ple_gate"] = _mm(h3, dpg, ta=True, out_dtype=BF16, name="d_w_ple_gate")
    dx2, grads["g_ple"] = _mm_fused(
        dpg, full["w_ple_gate"], lambda acc, x2, dx3, g: _norm_bwd_fn(x2, acc, dx3, g), [x2, dx3], [g_ple],
        [F32], [_sds((1, D), F32)], tb=True, full_rows=True, name="d_h3_norm_ple_bwd")
    dgt, dup = _mm_fused(dx2, full["w_down"], lambda acc, gt, up: _swiglu_bwd_fn(gt.astype(F32), up.astype(F32), acc), [gt, up], [],
                         [BF16, BF16], [], tb=True, name="d_act_swiglu_bwd")
    grads["w_down"] = _mm(act, dx2, ta=True, out_dtype=BF16, name="d_w_down")
    grads["w_gate"] = _mm(dgt, h2, ta=True, out_dtype=BF16, name="d_w_gate")
    grads["w_up"] = _mm(dup, h2, ta=True, out_dtype=BF16, name="d_w_up")
    dx1, grads["g_ffn"] = _mm_fused(
        dgt, full["w_gate"], lambda acc, x1, dx2, g: _norm_bwd_fn(x1, acc, dx2, g), [x1, dx2], [g_ffn],
        [F32], [_sds((1, D), F32)], full_rows=True, a2=dup, b2=full["w_up"], name="d_h2_norm_ffn_bwd")
    grads["w_o"] = _mm(mix, dx1, ta=True, out_dtype=BF16, name="d_w_o")
    token = scatter_start(0)
    dmix = _mm(dx1, full["w_o"], tb=True, name="d_mix", after=token)

    half = MLA_HEADS * V_HEAD
    do, dhg, dg_hgo = _stage(_post_bwd_fn, [o_f, o_b, hg, _cols(dmix, half, 1)], [g_hgo_row],
                             [_sds((T, half), F32), _sds((T, half), BF16)], [_sds((1, half), F32)], name="mix_out_bwd",
                             tile=TOK_TILE_LIGHT)
    grads["g_hgo"] = dg_hgo
    dhq_f, dhi_f, dhf_f, dlow_f = _gla_bwd(z, lower3[0], st_f, do, None, rev=False,
                                           col_q=_Z_HQ, col_f=_Z_HFF, col_v=_Z_HI, hp=GLA_BWD_HEADS)
    dhq, dhi, dhf_b, dlow_b = _gla_bwd(z, lower3[1], st_b, do, (dhq_f, dhi_f), rev=True,
                                       col_q=_Z_HQ, col_f=_Z_HFB, col_v=_Z_HI, hp=GLA_BWD_HEADS)

    dq, dk, dv = _attention_bwd(q, k, v, att, lse, dmix)
    dq_raw, dkv_raw, dkr, grads["g_qn"], grads["g_kn"] = _stage(
        _mla_b_bwd_fn, [q_raw, kv_raw, kr, cos, sin, dq, dk, dv], [g_qn, g_kn],
        [_sds(q_raw.shape, BF16), _sds(kv_raw.shape, BF16), _sds((T, LANE), BF16)],
        [_sds(g_qn.shape, F32), _sds(g_kn.shape, F32)], name="mla_qk_norm_rope_bwd")
    grads["w_qb"] = _mm(dq_raw, qn, ta=True, out_dtype=BF16, name="d_w_qb")
    grads["w_kvb"] = _mm(kvn, dkv_raw, ta=True, out_dtype=BF16, name="d_w_kvb")
    dqn = _mm(dq_raw, full["w_qb"], name="d_qn")
    dkvn = _mm(dkv_raw, full["w_kvb"], tb=True, name="d_kvn")
    dcq, dckv, grads["g_qa"], grads["g_kva"] = _stage(
        _mla_a_bwd_fn, [_cols(z, 256, 0), _cols(z, 256, 1), dqn, dkvn], [g_qa, g_kva],
        [_sds((T, 256), BF16), _sds((T, 256), BF16)], [_sds(g_qa.shape, F32), _sds(g_kva.shape, F32)],
        name="mla_latent_norm_bwd", tile=TOK_TILE_LIGHT)
    token = reduce_group(0, dcq)
    dz = jnp.concatenate([dcq, dckv, dhq, dhf_f, dhf_b, dhi, dhg, dkr], axis=1)
    grads["w_in"] = _from_z_layout(_mm(dz, h1, ta=True, out_dtype=BF16, name="d_w_in", after=token))
    token = scatter_start(1)
    grad_x, grads["g_mix"] = _mm_fused(
        dz, w_in_zt, lambda acc, x, dx1, g: _norm_bwd_fn(x, acc, dx1, g), [x2d, dx1], [g_mix],
        [F32], [_sds((1, D), F32)], full_rows=True, name="d_h1_norm_mix_bwd", after=token)

    out_g, out_d, out_m, out_v = {}, {}, {}, {}

    def update_group(gi, after):
        mine, theirs = _exchange_wait(swap_started[gi], after, name=f"swap_wait_{gi}")
        tokens = []
        for n, a, b in zip(scatter_groups[gi], mine, theirs):
            *res, token = _adamw(view(n, w_named[n]), a, b, view(n, m_named[n]), view(n, v_named[n]), name="adamw_" + n)
            out_g[n], out_d[n], out_m[n], out_v[n] = (unview(n, t) for t in res)
            tokens.append(token)
        return tokens

    done = update_group(0, grads["g_mix"])
    token = reduce_group(1, done)
    done = update_group(1, token)

    small = ("g_mix", "g_qa", "g_kva", "g_qn", "g_kn", "g_hgo", "g_ffn", "g_ple")
    small_all = small + ("lb_param",)
    width = -(-max(w_named[n].size for n in small_all) // LANE) * LANE

    def row(a):
        a = a.reshape(1, -1)
        return jnp.pad(a, ((0, 0), (0, width - a.shape[1])))

    dlower = jnp.concatenate([dlow_f.reshape(1, -1), dlow_b.reshape(1, -1)], axis=0)
    pack = jnp.concatenate([row(grads[n]) for n in small] + [row(dlower[0]), row(dlower[1]), row(loss_part)]
                           + [jnp.zeros((5, width), F32)], axis=0)
    red = _allreduce_small(pack, done[-1])
    loss = red[10, 0]

    outs = _adamw_small(red, lb_full, [w_named[n] for n in small_all], [m_named[n] for n in small_all],
                        [v_named[n] for n in small_all])
    for i, n in enumerate(small_all):
        out_g[n], out_d[n], out_m[n], out_v[n] = outs[4 * i:4 * i + 4]

    return (loss, grad_x[None], *[out_g[n] for n in order], *[out_d[n] for n in order],
            *[out_m[n] for n in order], *[out_v[n] for n in order])
```

```python
import functools
import itertools

import jax
import jax.numpy as jnp
from jax import lax
from jax.experimental import pallas as pl
from jax.experimental.pallas import tpu as pltpu

F32 = jnp.float32
BF16 = jnp.bfloat16
MESH = pl.DeviceIdType.MESH

EPS = 1e-6
ROPE_THETA = 10000.0
MLA_HEADS = 4
QK_NOPE = 128
QK_ROPE = 64
QK_HEAD = QK_NOPE + QK_ROPE
V_HEAD = 128
HG_HEADS = 4
HG_DK = 128
CHUNK = 64
ADAM_LR = 0.001
ADAM_B1 = 0.9
ADAM_B2 = 0.999
ADAM_EPS = 1e-08
ADAM_WD = 0.01
ADAM_STEP = 10

LANE = 128
VMEM_LIMIT = 56 * 1024 * 1024
TOK_TILE = 512
TOK_TILE_LIGHT = 1024
GLA_GROUP = 16
GLA_FWD_HEADS = 4
GLA_BWD_HEADS = 2
ATT_TQ = 1024
ATT_TK = 1024
ATT_CHUNK = 1024
LOG2_E = 1.4426950408889634
N_CHIPS = 4
N_DEV = 8


_ANY = pl.BlockSpec(memory_space=pl.ANY)


def _params(dims=None, **kw):
    return pltpu.CompilerParams(dimension_semantics=dims, vmem_limit_bytes=VMEM_LIMIT, **kw)


def _tile_candidates(n, cap):
    out = [d for d in range(LANE, min(n, cap) + 1, LANE) if n % d == 0]
    if n <= cap and n not in out:
        out.append(n)
    return out or [n]


MM_VMEM_BUDGET = 40 * 1024 * 1024
MM_MIN_ROWS = 256
MM_MAX_ROWS = 1536
HBM_BYTES_PER_S = 2.8e12
MXU_FLOPS_PER_S = 8e14
STEP_OVERHEAD_S = 0.35e-6


def _mm_tiles(M, N, K, a_bytes, b_bytes, o_bytes, has_add, full_rows=False, full_k=False, n_a=1, n_b=1):
    cast_a, cast_b = a_bytes > 2, b_bytes > 2
    a_bytes, b_bytes = n_a * a_bytes, n_b * b_bytes
    best = None
    for tm in [t for t in _tile_candidates(M, MM_MAX_ROWS) if t >= min(M, MM_MIN_ROWS)]:
        for tn in ([N] if full_rows else _tile_candidates(N, N)):
            for tk in ([K] if full_k else _tile_candidates(K, K)):
                ni, nj, nk = M // tm, N // tn, K // tk
                vmem = 2 * (tm * tk * a_bytes + tk * tn * b_bytes + tm * tn * o_bytes * (2 if has_add else 1))
                vmem += tm * tn * 4 * (2 if nk > 1 else 1)
                vmem += (tm * tk * 2 * n_a if cast_a else 0) + (tk * tn * 2 * n_b if cast_b else 0)
                if vmem > MM_VMEM_BUDGET:
                    continue
                moved = M * K * a_bytes * (nj if nk > 1 else 1) + K * N * b_bytes * (1 if nj == nk == 1 else ni)
                moved += M * N * o_bytes * (2 if has_add else 1)
                t = max(moved / HBM_BYTES_PER_S, 2 * M * N * K / MXU_FLOPS_PER_S) + ni * nj * nk * STEP_OVERHEAD_S
                if best is None or t < best[0]:
                    best = (t, tm, tn, tk)
    assert best is not None, (M, N, K)
    return best[1:]


def _dot_raw(a, b, kind):
    nb = a.ndim - 2
    batch = ((0,), (0,)) if nb else ((), ())
    ca = nb if kind == "tn" else nb + 1
    cb = nb + 1 if kind == "nt" else nb
    return lax.dot_general(a.astype(BF16), b.astype(BF16), (((ca,), (cb,)), batch), preferred_element_type=F32)


@functools.partial(jax.custom_vjp, nondiff_argnums=(2,))
def _bdot(a, b, kind):
    return _dot_raw(a, b, kind)


def _bdot_fwd(a, b, kind):
    return _dot_raw(a, b, kind), (a, b)


def _bdot_bwd(kind, res, g):
    a, b = res
    if kind == "nn":
        da, db = _bdot(g, b, "nt"), _bdot(a, g, "tn")
    elif kind == "nt":
        da, db = _bdot(g, b, "nn"), _bdot(g, a, "tn")
    else:
        da, db = _bdot(b, g, "nt"), _bdot(a, g, "nn")
    return da.astype(a.dtype), db.astype(b.dtype)


_bdot.defvjp(_bdot_fwd, _bdot_bwd)


def _mm(a, b, *, name, ta=False, tb=False, add=None, out_dtype=F32, after=None):
    K, M = a.shape if ta else a.shape[::-1]
    N, Kb = b.shape if tb else b.shape[::-1]
    assert K == Kb, (a.shape, b.shape, ta, tb)
    tm, tn, tk = _mm_tiles(M, N, K, a.dtype.itemsize, b.dtype.itemsize, jnp.dtype(out_dtype).itemsize, add is not None)
    nk = K // tk
    kind = "tn" if ta else ("nt" if tb else "nn")
    assert not (ta and tb)
    a_spec = pl.BlockSpec((tk, tm), lambda i, j, k: (k, i)) if ta else pl.BlockSpec((tm, tk), lambda i, j, k: (i, k))
    b_spec = pl.BlockSpec((tn, tk), lambda i, j, k: (j, k)) if tb else pl.BlockSpec((tk, tn), lambda i, j, k: (k, j))
    o_spec = pl.BlockSpec((tm, tn), lambda i, j, k: (i, j))
    has_add = add is not None

    def body(*refs):
        a_ref, b_ref = refs[0], refs[1]
        add_ref = refs[2] if has_add else None
        o_ref = refs[n_in]
        part = _dot_raw(a_ref[...], b_ref[...], kind)
        if nk == 1:
            if has_add:
                part = part + add_ref[...].astype(F32)
            o_ref[...] = part.astype(o_ref.dtype)
            return
        acc_ref = refs[-1]
        k = pl.program_id(2)

        @pl.when(k == 0)
        def _():
            acc_ref[...] = part

        @pl.when(k > 0)
        def _():
            acc_ref[...] += part

        @pl.when(k == nk - 1)
        def _():
            r = acc_ref[...]
            if has_add:
                r = r + add_ref[...].astype(F32)
            o_ref[...] = r.astype(o_ref.dtype)

    ins = [a, b] + ([add] if has_add else []) + ([after] if after is not None else [])
    in_specs = [a_spec, b_spec] + ([o_spec] if has_add else []) + ([_ANY] if after is not None else [])
    n_in = len(ins)
    return pl.pallas_call(
        body, name=name, grid=(M // tm, N // tn, nk), in_specs=in_specs, out_specs=o_spec,
        out_shape=jax.ShapeDtypeStruct((M, N), out_dtype),
        scratch_shapes=[pltpu.VMEM((tm, tn), F32)] if nk > 1 else [],
        compiler_params=_params(("parallel", "parallel", "arbitrary")),
    )(*ins)


def _mm_fused(a, b, fn, tiles, params, out_dtypes, sums, *, name, ta=False, tb=False, full_rows=False, after=None,
              b2=None, a2=None):
    K, M = a.shape if ta else a.shape[::-1]
    N, Kb = b.shape if tb else b.shape[::-1]
    assert K == Kb and not (ta and tb), (a.shape, b.shape, ta, tb)
    per_elem = sum(t.dtype.itemsize for t in tiles) + sum(jnp.dtype(d).itemsize for d in out_dtypes)
    n_b = 1 if b2 is None else 2
    n_a = 1 if a2 is None else 2
    tm, tn, tk = _mm_tiles(M, N, K, a.dtype.itemsize, b.dtype.itemsize, per_elem, False, full_rows, b2 is not None,
                           n_a, n_b)
    nk = K // tk
    kind = "tn" if ta else ("nt" if tb else "nn")
    a_spec = pl.BlockSpec((tk, tm), lambda i, j, k: (k, i)) if ta else pl.BlockSpec((tm, tk), lambda i, j, k: (i, k))
    b_spec = pl.BlockSpec((tn, tk), lambda i, j, k: (j, k)) if tb else pl.BlockSpec((tk, tn), lambda i, j, k: (k, j))
    o_spec = pl.BlockSpec((tm, tn), lambda i, j, k: (i, j))
    ins = [a, b] + ([b2] if b2 is not None else []) + ([a2] if a2 is not None else [])
    ins += list(tiles) + list(params) + ([after] if after is not None else [])
    in_specs = [a_spec] + [b_spec] * n_b + [a_spec] * (n_a - 1) + [o_spec] * len(tiles)
    in_specs += [pl.BlockSpec(p.shape, lambda i, j, k, nd=p.ndim: (0,) * nd) for p in params]
    in_specs += [_ANY] if after is not None else []
    n_in, n_t, n_p, n_o = len(ins), len(tiles), len(params), len(out_dtypes)

    def body(*refs):
        outs, sum_refs = refs[n_in:n_in + n_o], refs[n_in + n_o:n_in + n_o + len(sums)]

        def finish(*products):
            res = fn(*products, *[t[...] for t in refs[n_a + n_b:n_a + n_b + n_t + n_p]])
            for o_ref, v in zip(outs, res[:n_o]):
                o_ref[...] = v.astype(o_ref.dtype)
            first = jnp.logical_and(pl.program_id(0) == 0, pl.program_id(1) == 0)
            for s_ref, v in zip(sum_refs, res[n_o:]):
                @pl.when(first)
                def _(s_ref=s_ref, v=v):
                    s_ref[...] = v

                @pl.when(jnp.logical_not(first))
                def _(s_ref=s_ref, v=v):
                    s_ref[...] += v

        part = _dot_raw(refs[0][...], refs[1][...], kind)
        if nk == 1 and a2 is not None:
            finish(part + _dot_raw(refs[3][...], refs[2][...], kind))
            return
        if nk == 1:
            finish(part, *([_dot_raw(refs[0][...], refs[2][...], kind)] if b2 is not None else []))
            return
        acc_ref = refs[-1]
        k = pl.program_id(2)

        @pl.when(k == 0)
        def _():
            acc_ref[...] = part

        @pl.when(k > 0)
        def _():
            acc_ref[...] += part

        @pl.when(k == nk - 1)
        def _():
            finish(acc_ref[...])

    out_shape = [_sds((M, N), d) for d in out_dtypes] + list(sums)
    out_specs = [o_spec] * n_o + [pl.BlockSpec(s.shape, lambda i, j, k, nd=len(s.shape): (0,) * nd) for s in sums]
    order = ("arbitrary",) * 3 if sums else ("parallel", "parallel", "arbitrary")
    return pl.pallas_call(
        body, name=name, grid=(M // tm, N // tn, nk), in_specs=in_specs, out_specs=out_specs, out_shape=out_shape,
        scratch_shapes=[pltpu.VMEM((tm, tn), F32)] if nk > 1 else [], compiler_params=_params(order),
    )(*ins)


def _cols(arr, width, block):
    return (arr, width, block)


def _stage(fn, tiles, params, out_tiles, out_sums, *, name, tile=TOK_TILE, after=None):
    first = tiles[0][0] if isinstance(tiles[0], tuple) else tiles[0]
    tile = min(tile, first.shape[0] if first.ndim == 2 else first.shape[1])

    def tok_spec(shape, width=None, block=0):
        if len(shape) == 2:
            w = shape[1] if width is None else width
            return pl.BlockSpec((tile, w), lambda i: (i, block))
        return pl.BlockSpec((shape[0], tile, shape[2]), lambda i: (0, i, 0))

    arrays, in_specs = [], []
    for t in tiles:
        if isinstance(t, tuple):
            arr, width, block = t
            arrays.append(arr)
            in_specs.append(tok_spec(arr.shape, width, block))
        else:
            arrays.append(t)
            in_specs.append(tok_spec(t.shape))
    n_tok = arrays[0].shape[0] if arrays[0].ndim == 2 else arrays[0].shape[1]
    for p in params:
        arrays.append(p)
        in_specs.append(pl.BlockSpec(p.shape, lambda i, nd=p.ndim: (0,) * nd))
    out_shape = list(out_tiles) + list(out_sums)
    out_specs = [tok_spec(o.shape) for o in out_tiles]
    out_specs += [pl.BlockSpec(o.shape, lambda i, nd=len(o.shape): (0,) * nd) for o in out_sums]
    n_fn, n_ot = len(arrays), len(out_tiles)
    if after is not None:
        arrays.append(after)
        in_specs.append(_ANY)
    n_in = len(arrays)

    def body(*refs):
        res = fn(*[r[...] for r in refs[:n_fn]])
        if not isinstance(res, (tuple, list)):
            res = (res,)
        outs = refs[n_in:]
        for o_ref, r in zip(outs[:n_ot], res[:n_ot]):
            o_ref[...] = r.astype(o_ref.dtype)
        i = pl.program_id(0)
        for o_ref, r in zip(outs[n_ot:], res[n_ot:]):
            @pl.when(i == 0)
            def _(o_ref=o_ref, r=r):
                o_ref[...] = r.astype(o_ref.dtype)

            @pl.when(i > 0)
            def _(o_ref=o_ref, r=r):
                o_ref[...] += r.astype(o_ref.dtype)

    res = pl.pallas_call(
        body, name=name, grid=(n_tok // tile,), in_specs=in_specs, out_specs=out_specs, out_shape=out_shape,
        compiler_params=_params(("arbitrary",)),
    )(*arrays)
    return res


def _sds(shape, dtype):
    return jax.ShapeDtypeStruct(tuple(shape), dtype)


def _sigmoid(x):
    return 0.5 * jnp.tanh(0.5 * x) + 0.5


def _rms(x, g):
    return x * lax.rsqrt(jnp.mean(x * x, axis=-1, keepdims=True) + EPS) * g


def _norm_bwd_fn(x, dh, dres, g):
    r = lax.rsqrt(jnp.mean(x * x, axis=-1, keepdims=True) + EPS)
    xr = x * r
    dhg = dh * g
    dx = r * (dhg - xr * jnp.mean(xr * dhg, axis=-1, keepdims=True))
    return dx + dres, jnp.sum(dh * xr, axis=0, keepdims=True)


def _norm_bwd_both_fn(x, dh, dres, g):
    dx, dg = _norm_bwd_fn(x, dh, dres, g)
    return dx, dx, dg


def _mla_a_fn(cq, ckv, g_qa, g_kva):
    return _rms(cq, g_qa), _rms(ckv, g_kva)


def _mla_a_bwd_fn(cq, ckv, dqn, dkvn, g_qa, g_kva):
    _, vjp = jax.vjp(_mla_a_fn, cq, ckv, g_qa, g_kva)
    return vjp((dqn, dkvn))


def _rope(t, cos, sin):
    t1, t2 = t[:, :QK_ROPE // 2], t[:, QK_ROPE // 2:]
    return jnp.concatenate([t1 * cos - t2 * sin, t1 * sin + t2 * cos], axis=-1)


def _mla_b_fn(q_raw, kv_raw, kr, cos, sin, g_qn, g_kn):
    krope = kr[:, :QK_ROPE]
    qs, ks, vs = [], [], []
    for h in range(MLA_HEADS):
        qh = _rms(q_raw[:, h * QK_HEAD:(h + 1) * QK_HEAD], g_qn)
        kvh = kv_raw[:, h * (QK_NOPE + V_HEAD):(h + 1) * (QK_NOPE + V_HEAD)]
        kh = _rms(jnp.concatenate([kvh[:, :QK_NOPE], krope], axis=-1), g_kn)
        qs.append(jnp.concatenate([qh[:, :QK_NOPE], _rope(qh[:, QK_NOPE:], cos, sin)], axis=-1))
        ks.append(jnp.concatenate([kh[:, :QK_NOPE], _rope(kh[:, QK_NOPE:], cos, sin)], axis=-1))
        vs.append(kvh[:, QK_NOPE:])
    return jnp.stack(qs), jnp.stack(ks), jnp.stack(vs)


def _mla_b_bwd_fn(q_raw, kv_raw, kr, cos, sin, dq, dk, dv, g_qn, g_kn):
    _, vjp = jax.vjp(lambda a, b, c, d, e: _mla_b_fn(a, b, c, cos, sin, d, e), q_raw, kv_raw, kr, g_qn, g_kn)
    return vjp((dq, dk, dv))


def _post_fn(a, o_f, o_b, hg, g_hgo):
    o = o_f + o_b
    parts = [a]
    for h in range(HG_HEADS):
        s = slice(h * HG_DK, (h + 1) * HG_DK)
        gate = hg[:, s]
        parts.append(_rms(o[:, s], g_hgo[:, s]) * (gate * _sigmoid(gate)))
    return jnp.concatenate(parts, axis=-1)


def _post_bwd_fn(o_f, o_b, hg, dr, g_hgo):
    def f(o, hg, g):
        return _post_fn(jnp.zeros_like(o), o, jnp.zeros_like(o), hg, g)[:, o.shape[1]:]
    _, vjp = jax.vjp(f, o_f + o_b, hg, g_hgo)
    return vjp(dr)


def _swiglu_fn(gt, up):
    return gt * _sigmoid(gt) * up


def _resid_norm_fn(acc, x, g):
    x_new = acc + x
    return x_new, _rms(x_new, g)


def _swiglu_bwd_fn(gt, up, dact):
    s = _sigmoid(gt)
    silu = gt * s
    return dact * up * (s + silu * (1.0 - s)), dact * silu


def _ple_loss_fn(x2, pg, pp, target):
    gate = _sigmoid(pg)
    err = x2 + gate * pp - target
    dx3 = err * (1.0 / err.shape[-1])
    loss = 0.5 * jnp.sum(jnp.mean(err * err, axis=-1, keepdims=True), axis=0, keepdims=True)
    return dx3, dx3 * pp * gate * (1.0 - gate), dx3 * gate, loss


def _attention_fwd(q, k, v):
    H, T, D = q.shape
    DV = v.shape[-1]
    tq, ck = min(ATT_TQ, T), min(ATT_CHUNK, T)
    c2 = (D ** -0.5) * LOG2_E

    def body(q_ref, k_ref, v_ref, o_ref, lse_ref):
        q_i = q_ref[0]

        def chunk(c, carry):
            m, l, acc = carry
            rows = pl.ds(pl.multiple_of(c * ck, ck), ck)
            s = _dot_raw(q_i, k_ref[0, rows, :], "nt")
            m_new = jnp.maximum(m, jnp.max(s, axis=-1, keepdims=True))
            p = jnp.exp2((s - m_new) * c2)
            alpha = jnp.exp2((m - m_new) * c2)
            l = l * alpha + jnp.sum(p, axis=-1, keepdims=True)
            acc = acc * alpha + _dot_raw(p, v_ref[0, rows, :], "nn")
            return m_new, l, acc

        init = (jnp.full((tq, 1), -jnp.inf, F32), jnp.zeros((tq, 1), F32), jnp.zeros((tq, DV), F32))
        m, l, acc = lax.fori_loop(0, T // ck, chunk, init, unroll=True)
        o_ref[...] = acc / l
        lse_ref[0] = m * c2 + jnp.log2(l)

    return pl.pallas_call(
        body, name="attention_fwd", grid=(H, T // tq),
        in_specs=[pl.BlockSpec((1, tq, D), lambda h, i: (h, i, 0)),
                  pl.BlockSpec((1, T, D), lambda h, i: (h, 0, 0)),
                  pl.BlockSpec((1, T, DV), lambda h, i: (h, 0, 0))],
        out_specs=[pl.BlockSpec((tq, DV), lambda h, i: (i, h)),
                   pl.BlockSpec((1, tq, 1), lambda h, i: (h, i, 0))],
        out_shape=[_sds((T, H * DV), F32), _sds((H, T, 1), F32)],
        compiler_params=_params(("parallel", "parallel")),
    )(q, k, v)


def _attention_bwd(q, k, v, o, lse2, dmix):
    H, T, D = q.shape
    DV = v.shape[-1]
    tk, cq = min(ATT_TK, T), min(ATT_CHUNK, T)
    scale = D ** -0.5
    c2 = scale * LOG2_E

    def body(q_ref, k_ref, v_ref, o_ref, lse_ref, do_ref, dq_ref, dk_ref, dv_ref, delta_ref):
        j = pl.program_id(1)

        @pl.when(j == 0)
        def _():
            delta = lax.dot_general(jnp.ones((8, DV), F32), do_ref[...] * o_ref[...], (((1,), (1,)), ((), ())),
                                    precision=lax.Precision.HIGHEST, preferred_element_type=F32)
            for i in range(T // cq):
                delta_ref[i] = delta[:, i * cq:(i + 1) * cq]
            dq_ref[0] = jnp.zeros((T, D), F32)

        k_j, v_j = k_ref[0], v_ref[0]
        dk_ref[0] = jnp.zeros((tk, D), F32)
        dv_ref[0] = jnp.zeros((tk, DV), F32)

        def chunk(c, carry):
            rows = pl.ds(pl.multiple_of(c * cq, cq), cq)
            q_c = q_ref[0, rows, :]
            do_c = do_ref[rows, :].astype(BF16)
            st = _dot_raw(k_j, q_c, "nt")
            pt = jnp.exp2(st * c2 - lse_ref[0, c])
            dv_ref[0] += _dot_raw(pt, do_c, "nn")
            dpt = _dot_raw(v_j, do_c, "nt")
            dst = pt * (dpt - delta_ref[c, 0:1, :]) * scale
            dk_ref[0] += _dot_raw(dst, q_c, "nn")
            dq_ref[0, rows, :] += _dot_raw(dst, k_j, "tn")
            return carry

        lax.fori_loop(0, T // cq, chunk, 0, unroll=True)

    return pl.pallas_call(
        body, name="attention_bwd", grid=(H, T // tk),
        in_specs=[pl.BlockSpec((1, T, D), lambda h, j: (h, 0, 0)),
                  pl.BlockSpec((1, tk, D), lambda h, j: (h, j, 0)),
                  pl.BlockSpec((1, tk, DV), lambda h, j: (h, j, 0)),
                  pl.BlockSpec((T, DV), lambda h, j: (0, h)),
                  pl.BlockSpec((1, T // cq, 1, cq), lambda h, j: (h, 0, 0, 0)),
                  pl.BlockSpec((T, DV), lambda h, j: (0, h))],
        out_specs=[pl.BlockSpec((1, T, D), lambda h, j: (h, 0, 0)),
                   pl.BlockSpec((1, tk, D), lambda h, j: (h, j, 0)),
                   pl.BlockSpec((1, tk, DV), lambda h, j: (h, j, 0))],
        out_shape=[_sds((H, T, D), F32), _sds((H, T, D), F32), _sds((H, T, DV), F32)],
        scratch_shapes=[pltpu.VMEM((T // cq, 8, cq), F32)],
        compiler_params=_params(("parallel", "arbitrary")),
    )(q, k, v, o, lse2.reshape(H, T // cq, 1, cq), dmix)


def _split3_dot(ones, x, kind):
    hi = x.astype(BF16)
    rest = x - hi.astype(F32)
    mid = rest.astype(BF16)
    lo = (rest - mid.astype(F32)).astype(BF16)
    return (_dot_raw(ones, hi, kind) + _dot_raw(ones, mid, kind)) + _dot_raw(ones, lo, kind)


@jax.custom_vjp
def _running_sum(x, tri):
    return _split3_dot(tri, x, "nn")


def _running_sum_fwd(x, tri):
    return _split3_dot(tri, x, "nn"), tri


def _running_sum_bwd(tri, g):
    return _split3_dot(tri, g, "tn"), jnp.zeros_like(tri)


_running_sum.defvjp(_running_sum_fwd, _running_sum_bwd)


def _gla_block(hq, hf, hi, lower, st_in, *, rev, dot):
    rows, dk = hq.shape
    G, C = rows // CHUNK, CHUNK
    q = hq * _sigmoid(hq)
    f = lower + (1.0 - lower) * _sigmoid(hf)
    k = 1.0 - f
    logf = jnp.log2(f)
    q3, k3, v3, lf3 = (t.reshape(G, C, dk) for t in (q, k, hi, logf))
    r = lax.broadcasted_iota(jnp.int32, (C, C), 0)
    c = lax.broadcasted_iota(jnp.int32, (C, C), 1)
    tri = ((r <= c) if rev else (r >= c)).astype(F32)
    b = _running_sum(lf3, jnp.broadcast_to(tri, (G, C, C)))
    tpos = lax.broadcasted_iota(jnp.int32, (1, C, 1), 1)
    first_half = (tpos >= C // 2) if rev else (tpos <= C // 2 - 1)
    b_mid = jnp.sum(jnp.where(first_half, lf3, 0.0), axis=1, keepdims=True)
    b_last = jnp.sum(lf3, axis=1, keepdims=True)
    a = dot(q3 * jnp.exp2(b - b_mid), k3 * jnp.exp2(b_mid - b), "nt") * tri
    o_intra = dot(a, v3, "nn")
    kv_t = dot(v3, k3 * jnp.exp2(b_last - b), "tn")
    decay = jnp.exp2(b_last)
    qd = q3 * jnp.exp2(b)
    st = st_in
    o_inter = [None] * G
    for g in (reversed(range(G)) if rev else range(G)):
        o_inter[g] = dot(qd[g], st, "nt")
        st = st * decay[g] + kv_t[g]
    o = o_intra.reshape(rows, dk) + jnp.concatenate(o_inter, axis=0)
    return o, st


def _gla_fwd(z, lower3, *, rev, col_q, col_f, col_v, hp):
    T = z.shape[0]
    rows = min(GLA_GROUP * CHUNK, T)
    nb = T // rows
    wide = hp * HG_DK
    blk = (lambda n: nb - 1 - n) if rev else (lambda n: n)

    def body(hq_ref, hf_ref, hi_ref, low_ref, o_ref, st_out_ref, st_ref):
        @pl.when(pl.program_id(1) == 0)
        def _():
            st_ref[...] = jnp.zeros_like(st_ref)

        st_in = [st_ref[i] for i in range(hp)]
        heads = []
        for i in range(hp):
            cols = slice(i * HG_DK, (i + 1) * HG_DK)
            heads.append(_gla_block(hq_ref[:, cols], hf_ref[:, cols], hi_ref[:, cols], low_ref[i], st_in[i], rev=rev,
                                    dot=_dot_raw))
        for i, (o, st) in enumerate(heads):
            st_out_ref[i, 0] = st_in[i]
            o_ref[:, i * HG_DK:(i + 1) * HG_DK] = o
            st_ref[i] = st

    def zspec(col):
        return pl.BlockSpec((rows, wide), lambda h, n: (blk(n), col // wide + h))

    return pl.pallas_call(
        body, name="gla_fwd_rev" if rev else "gla_fwd", grid=(HG_HEADS // hp, nb),
        in_specs=[zspec(col_q), zspec(col_f), zspec(col_v), pl.BlockSpec((hp, 1, HG_DK), lambda h, n: (h, 0, 0))],
        out_specs=[pl.BlockSpec((rows, wide), lambda h, n: (blk(n), h)),
                   pl.BlockSpec((hp, 1, HG_DK, HG_DK), lambda h, n: (h, blk(n), 0, 0))],
        out_shape=[_sds((T, HG_HEADS * HG_DK), F32), _sds((HG_HEADS, nb, HG_DK, HG_DK), F32)],
        scratch_shapes=[pltpu.VMEM((hp, HG_DK, HG_DK), F32)],
        compiler_params=_params(("parallel", "arbitrary")),
    )(z, z, z, lower3)


def _gla_bwd(z, lower3, states, do, prev, *, rev, col_q, col_f, col_v, hp):
    T = z.shape[0]
    rows = min(GLA_GROUP * CHUNK, T)
    nb = T // rows
    wide = hp * HG_DK
    blk = (lambda n: n) if rev else (lambda n: nb - 1 - n)
    has_prev = prev is not None
    fn = functools.partial(_gla_block, rev=rev, dot=_bdot)

    def body(*refs):
        hq_ref, hf_ref, hi_ref, low_ref, st_ref, do_ref = refs[:6]
        rest = refs[6:]
        if has_prev:
            pq_ref, pi_ref = rest[:2]
            rest = rest[2:]
        dhq_ref, dhi_ref, dhf_ref, dlow_ref, dst_ref = rest
        n = pl.program_id(1)

        @pl.when(n == 0)
        def _():
            dst_ref[...] = jnp.zeros_like(dst_ref)

        dst_in = [dst_ref[i] for i in range(hp)]
        heads = []
        for i in range(hp):
            cols = slice(i * HG_DK, (i + 1) * HG_DK)
            _, vjp = jax.vjp(fn, hq_ref[:, cols], hf_ref[:, cols], hi_ref[:, cols], low_ref[i], st_ref[i, 0])
            dhq, dhf, dhi, dlow, dst = vjp((do_ref[:, cols], dst_in[i]))
            if has_prev:
                dhq = dhq + pq_ref[:, cols]
                dhi = dhi + pi_ref[:, cols]
            heads.append((dhq, dhf, dhi, dlow, dst))
        for i, (dhq, dhf, dhi, dlow, dst) in enumerate(heads):
            cols = slice(i * HG_DK, (i + 1) * HG_DK)
            dst_ref[i] = dst
            dhq_ref[:, cols] = dhq.astype(dhq_ref.dtype)
            dhi_ref[:, cols] = dhi.astype(dhi_ref.dtype)
            dhf_ref[:, cols] = dhf.astype(dhf_ref.dtype)

        @pl.when(n == 0)
        def _():
            for i in range(hp):
                dlow_ref[i] = heads[i][3]

        @pl.when(n > 0)
        def _():
            for i in range(hp):
                dlow_ref[i] += heads[i][3]

    def zspec(col):
        return pl.BlockSpec((rows, wide), lambda h, n: (blk(n), col // wide + h))

    hspec = pl.BlockSpec((rows, wide), lambda h, n: (blk(n), h))
    in_specs = [zspec(col_q), zspec(col_f), zspec(col_v), pl.BlockSpec((hp, 1, HG_DK), lambda h, n: (h, 0, 0)),
                pl.BlockSpec((hp, 1, HG_DK, HG_DK), lambda h, n: (h, blk(n), 0, 0)), hspec]
    ins = [z, z, z, lower3, states, do]
    if has_prev:
        in_specs += [hspec, hspec]
        ins += list(prev)
    full_wide = HG_HEADS * HG_DK
    acc_dtype = BF16 if has_prev else F32
    return pl.pallas_call(
        body, name="gla_bwd_rev" if rev else "gla_bwd", grid=(HG_HEADS // hp, nb),
        in_specs=in_specs,
        out_specs=[hspec, hspec, hspec, pl.BlockSpec((hp, 1, HG_DK), lambda h, n: (h, 0, 0))],
        out_shape=[_sds((T, full_wide), acc_dtype), _sds((T, full_wide), acc_dtype), _sds((T, full_wide), BF16),
                   _sds((HG_HEADS, 1, HG_DK), F32)],
        scratch_shapes=[pltpu.VMEM((hp, HG_DK, HG_DK), F32)],
        compiler_params=_params(("parallel", "arbitrary")),
    )(*ins)


def _lower_fn(lb):
    e = jnp.exp(lb - jnp.max(lb, axis=0, keepdims=True))
    return (e / jnp.sum(e, axis=0, keepdims=True))[0]


def _lower_bounds(lb):
    def body(lb_ref, o_ref):
        o_ref[...] = _lower_fn(lb_ref[...])
    return pl.pallas_call(body, name="lower_bounds", out_shape=_sds(lb.shape[1:], F32))(lb)


def _row_tile(r, cap=1024):
    best = None
    for t in range(16, min(r, cap) + 1, 16):
        if r % t == 0:
            best = t
    return best if best is not None else r


def _sum4(shards, recv, chip, *, name):
    _, R, C = shards.shape
    tr = _row_tile(R)

    def body(chip_ref, o_ref, r_ref, out_ref):
        out_ref[...] = ((o_ref[0].astype(F32) + r_ref[0].astype(F32)) + r_ref[1].astype(F32)) + r_ref[2].astype(F32)

    grid_spec = pltpu.PrefetchScalarGridSpec(
        num_scalar_prefetch=1, grid=(R // tr,),
        in_specs=[pl.BlockSpec((1, tr, C), lambda i, chip_ref: (chip_ref[0], i, 0)),
                  pl.BlockSpec((3, tr, C), lambda i, chip_ref: (0, i, 0))],
        out_specs=pl.BlockSpec((tr, C), lambda i, chip_ref: (i, 0)))
    return pl.pallas_call(
        body, name=name, grid_spec=grid_spec, out_shape=_sds((R, C), F32), compiler_params=_params(("parallel",)),
    )(chip, shards, recv)


def _adamw_math(w, g, m, v):
    m = ADAM_B1 * m + (1.0 - ADAM_B1) * g
    v = ADAM_B2 * v + (1.0 - ADAM_B2) * (g * g)
    m_hat = m / (1.0 - ADAM_B1 ** ADAM_STEP)
    v_hat = v / (1.0 - ADAM_B2 ** ADAM_STEP)
    delta = -ADAM_LR * (m_hat / (jnp.sqrt(v_hat) + ADAM_EPS) + ADAM_WD * w)
    return delta, m, v


def _adamw(w, g_a, g_b, m, v, *, name):
    R, C = w.shape
    tr = _row_tile(R)
    two = g_b is not None

    def body(*refs):
        w_ref, ga_ref = refs[0], refs[1]
        rest = refs[2:]
        g = ga_ref[...]
        if two:
            g = g + rest[0][...]
            rest = rest[1:]
        m_ref, v_ref, g_out, d_out, m_out, v_out, token = rest
        delta, m_new, v_new = _adamw_math(w_ref[...], g, m_ref[...], v_ref[...])
        g_out[...] = g
        d_out[...] = delta
        m_out[...] = m_new
        v_out[...] = v_new
        token[...] = jnp.zeros_like(token)

    spec = pl.BlockSpec((tr, C), lambda i: (i, 0))
    ins = [w, g_a] + ([g_b] if two else []) + [m, v]
    return pl.pallas_call(
        body, name=name, grid=(R // tr,), in_specs=[spec] * len(ins),
        out_specs=[spec] * 4 + [pl.BlockSpec((8, LANE), lambda i: (0, 0))],
        out_shape=[_sds((R, C), F32)] * 4 + [_sds((8, LANE), F32)], compiler_params=_params(("arbitrary",)),
    )(*ins)


def _adamw_small(red, lb_full, ws, ms, vs):
    n = len(ws)

    def pieces(shape):
        out = []
        for j, idx in enumerate(itertools.product(*[range(d) for d in shape[:-1]])):
            out.append((idx[:-1] + (slice(idx[-1], idx[-1] + 1), slice(None)), j * shape[-1]))
        return out

    def body(*refs):
        red_ref, lb_ref = refs[0], refs[1]
        w_refs, m_refs, v_refs = refs[2:2 + n], refs[2 + n:2 + 2 * n], refs[2 + 2 * n:2 + 3 * n]
        out_refs = refs[2 + 3 * n:]
        chip = 2 * lax.axis_index("x") + lax.axis_index("y")
        n_f, shard = lb_ref.shape[-1], w_refs[n - 1].shape[-1]
        _, vjp = jax.vjp(_lower_fn, lb_ref[...])
        dlb = vjp(red_ref[8:10, 0:n_f])[0]
        for i in range(n):
            width = w_refs[i].shape[-1]
            for j, (at, lane) in enumerate(pieces(w_refs[i].shape)):
                if i < n - 1:
                    g = red_ref[i:i + 1, lane:lane + width]
                else:
                    row = dlb[j // 2][j % 2:j % 2 + 1]
                    g = sum(jnp.where(chip == q, row[:, q * shard:(q + 1) * shard], 0.0) for q in range(N_CHIPS))
                delta, m_new, v_new = _adamw_math(w_refs[i][at], g, m_refs[i][at], v_refs[i][at])
                for o_ref, val in zip(out_refs[4 * i:4 * i + 4], (g, delta, m_new, v_new)):
                    o_ref[at] = val

    return pl.pallas_call(
        body, name="adamw_small", out_shape=[_sds(w.shape, F32) for w in ws for _ in range(4)],
    )(red, lb_full, *ws, *ms, *vs)


def _chip_peers():
    x, y, c = lax.axis_index("x"), lax.axis_index("y"), lax.axis_index("c")
    return (x, y, c), 2 * x + y, [(1 - x, y), (x, 1 - y), (1 - x, 1 - y)]


_HBM = pl.BlockSpec(memory_space=pltpu.HBM)
_SEM = pl.BlockSpec(memory_space=pltpu.SEMAPHORE)
_EFFECT = pltpu.SideEffectType.DATAFLOW_SIDE_EFFECTING


def _exchange_copies(srcs, lands, sems, mode):
    (x, y, c), me, chips = _chip_peers()
    copies = []
    for t, (src, land) in enumerate(zip(srcs, lands)):
        if mode == "swap":
            copies.append(pltpu.make_async_remote_copy(src, land, sems[0].at[3 * t], sems[1].at[3 * t],
                                                       device_id=(x, y, 1 - c), device_id_type=MESH))
            continue
        for k, (px, py) in enumerate(chips):
            gather = mode == "gather"
            copies.append(pltpu.make_async_remote_copy(
                src if gather else src.at[2 * px + py], land.at[me] if gather else land.at[k],
                sems[0].at[3 * t + k], sems[1].at[3 * t + k], device_id=(px, py, c), device_id_type=MESH))
        if mode == "gather":
            copies.append(pltpu.make_async_copy(src, land.at[me], sems[2].at[t]))
    return copies


def _exchange_start(srcs, *, mode, name, after=None):
    n = len(srcs)
    n_sem = 3 if mode == "gather" else 2
    n_in = 2 * n + (after is not None)
    land_shape = {"gather": lambda s: (N_CHIPS,) + s.shape, "scatter": lambda s: (3,) + s.shape[1:], "swap": lambda s: s.shape}
    lands = [_sds(land_shape[mode](s), s.dtype) for s in srcs]

    def body(*refs):
        for cp in _exchange_copies(refs[:n], refs[n:2 * n], refs[n_in:n_in + n_sem], mode):
            cp.start()
        token = refs[-1]
        token[...] = jnp.zeros_like(token)

    sem_shapes = [pltpu.SemaphoreType.DMA((3 * n,)), pltpu.SemaphoreType.DMA((3 * n,))]
    sem_shapes += [pltpu.SemaphoreType.DMA((n,))] if mode == "gather" else []
    thru = [pltpu.HBM(s.shape, s.dtype) for s in srcs] + [pltpu.HBM(l.shape, l.dtype) for l in lands]
    res = pl.pallas_call(
        body, name=name, in_specs=[_HBM] * (2 * n) + [_ANY] * (after is not None),
        out_specs=[_SEM] * n_sem + [_HBM] * (2 * n) + [pl.BlockSpec(memory_space=pltpu.VMEM)],
        out_shape=sem_shapes + thru + [_sds((8, LANE), F32)], input_output_aliases={i: n_sem + i for i in range(2 * n)},
        compiler_params=pltpu.CompilerParams(has_side_effects=_EFFECT),
    )(*[pltpu.with_memory_space_constraint(s, pltpu.HBM) for s in srcs],
      *[pltpu.with_memory_space_constraint(lax.empty(l.shape, l.dtype), pltpu.HBM) for l in lands],
      *([after] if after is not None else []))
    return (res[:n_sem], res[n_sem:n_sem + n], res[n_sem + n:n_sem + 2 * n], mode), res[-1]


def _exchange_wait(started, after, *, name):
    sems, srcs, lands, mode = started
    n, n_sem = len(srcs), len(sems)
    after = list(after) if isinstance(after, (list, tuple)) else [after]

    def body(*refs):
        for cp in _exchange_copies(refs[:n], refs[n:2 * n], refs[2 * n:2 * n + n_sem], mode):
            cp.wait()

    res = pl.pallas_call(
        body, name=name, in_specs=[_HBM] * (2 * n) + [_SEM] * n_sem + [_ANY] * len(after), out_specs=[_HBM] * (2 * n),
        out_shape=[pltpu.HBM(a.shape, a.dtype) for a in list(srcs) + list(lands)],
        input_output_aliases={i: i for i in range(2 * n)},
        compiler_params=pltpu.CompilerParams(has_side_effects=_EFFECT),
    )(*srcs, *lands, *sems, *after)
    return res[:n], res[n:]


def _allreduce_small(pack, after):
    R, C = pack.shape

    def body(in_ref, after_ref, out_ref, slots, send_sems, recv_sems):
        x, y, c = lax.axis_index("x"), lax.axis_index("y"), lax.axis_index("c")
        me = 4 * x + 2 * y + c
        slots[me] = in_ref[...]
        copies = []
        for k in range(1, N_DEV):
            peer = (x ^ ((k >> 2) & 1), y ^ ((k >> 1) & 1), c ^ (k & 1))
            cp = pltpu.make_async_remote_copy(in_ref, slots.at[me], send_sems.at[k - 1], recv_sems.at[k - 1],
                                              device_id=peer, device_id_type=MESH)
            cp.start()
            copies.append(cp)
        for cp in copies:
            cp.wait()
        acc = slots[0]
        for d in range(1, N_DEV):
            acc = acc + slots[d]
        out_ref[...] = acc

    return pl.pallas_call(
        body, name="allreduce_small", out_shape=_sds((R, C), F32),
        in_specs=[pl.BlockSpec(memory_space=pltpu.VMEM), _ANY], out_specs=pl.BlockSpec(memory_space=pltpu.VMEM),
        scratch_shapes=[pltpu.VMEM((N_DEV, R, C), F32), pltpu.SemaphoreType.DMA((N_DEV - 1,)),
                        pltpu.SemaphoreType.DMA((N_DEV - 1,))],
        compiler_params=_params(),
    )(pack, after)


_Z_CQ, _Z_CKV, _Z_HQ, _Z_HFF, _Z_HFB, _Z_HI, _Z_HG, _Z_KR, _Z_END = 0, 256, 512, 1024, 1536, 2048, 2560, 3072, 3200


def _to_z_layout(wt):
    pad = jnp.zeros((_Z_END - _Z_KR - QK_ROPE, wt.shape[1]), wt.dtype)
    return jnp.concatenate([wt[:512], wt[512 + QK_ROPE:], wt[512:512 + QK_ROPE], pad], axis=0)


def _from_z_layout(wt):
    return jnp.concatenate([wt[:512], wt[_Z_KR:_Z_KR + QK_ROPE], wt[512:_Z_KR]], axis=0)


def _col_shards_to_full(g):
    return jnp.transpose(g, (1, 0, 2)).reshape(g.shape[1], -1)


def _full_to_col_shards(w):
    r, c = w.shape
    return jnp.transpose(w.reshape(r, N_CHIPS, c // N_CHIPS), (1, 0, 2))


def _full_to_row_shards(w):
    r, c = w.shape
    return w.reshape(N_CHIPS, r // N_CHIPS, c)


def kernel(x, p, positions, g_mix, w_in, g_qa, g_kva, w_qb, w_kvb, g_qn, g_kn, lb_param, g_hgo, w_o, g_ffn, w_gate, w_up, w_down, g_ple, w_ple_gate, w_ple_proj, loss_target, m_g_mix, m_w_in, m_g_qa, m_g_kva, m_w_qb, m_w_kvb, m_g_qn, m_g_kn, m_lb_param, m_g_hgo, m_w_o, m_g_ffn, m_w_gate, m_w_up, m_w_down, m_g_ple, m_w_ple_gate, m_w_ple_proj, v_g_mix, v_w_in, v_g_qa, v_g_kva, v_w_qb, v_w_kvb, v_g_qn, v_g_kn, v_lb_param, v_g_hgo, v_w_o, v_g_ffn, v_w_gate, v_w_up, v_w_down, v_g_ple, v_w_ple_gate, v_w_ple_proj):
    w_named = dict(g_mix=g_mix, w_in=w_in, g_qa=g_qa, g_kva=g_kva, w_qb=w_qb, w_kvb=w_kvb, g_qn=g_qn, g_kn=g_kn,
                   lb_param=lb_param, g_hgo=g_hgo, w_o=w_o, g_ffn=g_ffn, w_gate=w_gate, w_up=w_up, w_down=w_down,
                   g_ple=g_ple, w_ple_gate=w_ple_gate, w_ple_proj=w_ple_proj)
    m_named = dict(g_mix=m_g_mix, w_in=m_w_in, g_qa=m_g_qa, g_kva=m_g_kva, w_qb=m_w_qb, w_kvb=m_w_kvb, g_qn=m_g_qn,
                   g_kn=m_g_kn, lb_param=m_lb_param, g_hgo=m_g_hgo, w_o=m_w_o, g_ffn=m_g_ffn, w_gate=m_w_gate,
                   w_up=m_w_up, w_down=m_w_down, g_ple=m_g_ple, w_ple_gate=m_w_ple_gate, w_ple_proj=m_w_ple_proj)
    v_named = dict(g_mix=v_g_mix, w_in=v_w_in, g_qa=v_g_qa, g_kva=v_g_kva, w_qb=v_w_qb, w_kvb=v_w_kvb, g_qn=v_g_qn,
                   g_kn=v_g_kn, lb_param=v_lb_param, g_hgo=v_g_hgo, w_o=v_w_o, g_ffn=v_g_ffn, w_gate=v_w_gate,
                   w_up=v_w_up, w_down=v_w_down, g_ple=v_g_ple, w_ple_gate=v_w_ple_gate, w_ple_proj=v_w_ple_proj)
    order = list(w_named)
    transposed = ("w_in", "w_qb", "w_gate", "w_up")
    col_sharded = ("w_kvb", "w_ple_proj")
    row_sharded = ("w_o", "w_down", "w_ple_gate")
    big = transposed + col_sharded + row_sharded

    def view(n, a):
        return jnp.transpose(a[0]) if n in transposed else a[0]

    def unview(n, a):
        return (jnp.transpose(a) if n in transposed else a)[None]

    def to_shards(n, g):
        return _full_to_col_shards(g) if n in col_sharded else _full_to_row_shards(g)

    x2d, p2d, tgt = x[0], p[0, 0], loss_target[0]
    T, D = x2d.shape

    lb_flat = lb_param.reshape(-1, lb_param.shape[-1])
    gather_groups = (("w_in",), ("w_qb", "w_kvb"), ("w_o", "w_gate", "w_up", "w_down", "w_ple_gate", "w_ple_proj"))
    gather_started = []

    casts = {n: view(n, w_named[n]).astype(BF16) for n in big}

    def gather_start(gi, after):
        srcs = [casts[n] for n in gather_groups[gi]] + ([lb_flat] if gi == 0 else [])
        started, token = _exchange_start(srcs, mode="gather", name=f"gather_start_{gi}", after=after)
        gather_started.append(started)
        return token

    full = {}

    def gather_wait(gi, after):
        _, got = _exchange_wait(gather_started[gi], after, name=f"gather_wait_{gi}")
        for n, g in zip(gather_groups[gi], got):
            full[n] = _col_shards_to_full(g) if n in col_sharded else g.reshape(-1, g.shape[-1])
        return got

    g_hgo_row = g_hgo.reshape(1, -1)

    inv_freq = ROPE_THETA ** (-jnp.arange(0, QK_ROPE, 2, dtype=F32) / QK_ROPE)
    ang = positions[0].astype(F32)[:, None] * inv_freq
    cos, sin = jnp.cos(ang), jnp.sin(ang)
    token = gather_start(0, None)
    h1 = _stage(_rms, [x2d], [g_mix], [_sds((T, D), BF16)], [], name="norm_mix", after=token, tile=TOK_TILE_LIGHT)[0]
    got = gather_wait(0, [h1, cos, sin] + [casts[n] for g in gather_groups[1:] for n in g])
    token = got[0]
    for gi in range(1, len(gather_groups)):
        token = gather_start(gi, token)
    lb_full = _col_shards_to_full(got[-1]).reshape(lb_param.shape[0], lb_param.shape[1], -1)
    w_in_zt = _to_z_layout(full["w_in"])
    z = _mm(h1, w_in_zt, tb=True, name="in_proj", after=token)
    qn, kvn = _stage(_mla_a_fn, [_cols(z, 256, 0), _cols(z, 256, 1)], [g_qa, g_kva],
                     [_sds((T, 256), BF16), _sds((T, 256), BF16)], [], name="mla_latent_norm", tile=TOK_TILE_LIGHT)
    gather_wait(1, qn)
    q_raw = _mm(qn, full["w_qb"], tb=True, name="q_up")
    kv_raw = _mm(kvn, full["w_kvb"], name="kv_up")
    kr = _cols(z, LANE, _Z_KR // LANE)
    q, k, v = _stage(_mla_b_fn, [q_raw, kv_raw, kr, cos, sin], [g_qn, g_kn],
                     [_sds((MLA_HEADS, T, QK_HEAD), BF16), _sds((MLA_HEADS, T, QK_HEAD), BF16),
                      _sds((MLA_HEADS, T, V_HEAD), BF16)], [], name="mla_qk_norm_rope")
    att, lse = _attention_fwd(q, k, v)

    lower = _lower_bounds(lb_full)
    lower3 = lower.reshape(2, HG_HEADS, 1, HG_DK)
    o_f, st_f = _gla_fwd(z, lower3[0], rev=False, col_q=_Z_HQ, col_f=_Z_HFF, col_v=_Z_HI, hp=GLA_FWD_HEADS)
    o_b, st_b = _gla_fwd(z, lower3[1], rev=True, col_q=_Z_HQ, col_f=_Z_HFB, col_v=_Z_HI, hp=GLA_FWD_HEADS)
    hg = _cols(z, 512, _Z_HG // 512)
    mix = _stage(_post_fn, [att, o_f, o_b, hg], [g_hgo_row], [_sds((T, att.shape[1] + o_f.shape[1]), BF16)], [],
                 name="mix_out", tile=TOK_TILE_LIGHT)[0]
    gather_wait(2, mix)
    x1, h2 = _mm_fused(mix, full["w_o"], _resid_norm_fn, [x2d], [g_ffn], [F32, BF16], [], full_rows=True,
                       name="out_proj")
    gt, up, act = _mm_fused(h2, full["w_gate"], lambda gt, up: (gt, up, _swiglu_fn(gt, up)), [], [], [BF16, BF16, BF16],
                            [], tb=True, b2=full["w_up"], name="ffn_gate_up")
    x2, h3 = _mm_fused(act, full["w_down"], _resid_norm_fn, [x1], [g_ple], [F32, BF16], [], full_rows=True,
                       name="ffn_down")
    pp = _mm(p2d, full["w_ple_proj"], name="ple_proj")
    dx3, dpg, dpp, loss_part = _mm_fused(
        h3, full["w_ple_gate"], lambda acc, pp, x2, tgt: _ple_loss_fn(x2, acc, pp, tgt), [pp, x2, tgt], [],
        [F32, BF16, BF16], [_sds((1, 1), F32)], full_rows=True, name="ple_gate_loss")

    grads = {}
    scatter_groups = (("w_ple_proj", "w_ple_gate", "w_down", "w_gate", "w_up", "w_o"), ("w_qb", "w_kvb", "w_in"))
    scatter_started = []

    def scatter_start(gi):
        srcs = [to_shards(n, grads[n]) for n in scatter_groups[gi]]
        started, token = _exchange_start(srcs, mode="scatter", name=f"scatter_start_{gi}")
        scatter_started.append(started)
        return token

    chip = 2 * lax.axis_index("x") + lax.axis_index("y")
    swap_started = []

    def reduce_group(gi, after):
        shards, recvs = _exchange_wait(scatter_started[gi], after, name=f"scatter_wait_{gi}")
        sums = [_sum4(s, r, chip.reshape(1), name="sum_" + n) for n, s, r in zip(scatter_groups[gi], shards, recvs)]
        started, token = _exchange_start(sums, mode="swap", name=f"swap_start_{gi}")
        swap_started.append(started)
        return token

    grads["w_ple_proj"] = _mm(p2d, dpp, ta=True, out_dtype=BF16, name="d_w_ple_proj")
    grads["w_ple_gate"] = _mm(h3, dpg, ta=True, out_dtype=BF16, name="d_w_ple_gate")
    dx2, dx2_b, grads["g_ple"] = _mm_fused(
        dpg, full["w_ple_gate"], lambda acc, x2, dx3, g: _norm_bwd_both_fn(x2, acc, dx3, g), [x2, dx3], [g_ple],
        [F32, BF16], [_sds((1, D), F32)], tb=True, full_rows=True, name="d_h3_norm_ple_bwd")
    dgt, dup = _mm_fused(dx2_b, full["w_down"], lambda acc, gt, up: _swiglu_bwd_fn(gt.astype(F32), up.astype(F32), acc), [gt, up], [],
                         [BF16, BF16], [], tb=True, name="d_act_swiglu_bwd")
    grads["w_down"] = _mm(act, dx2_b, ta=True, out_dtype=BF16, name="d_w_down")
    grads["w_gate"] = _mm(dgt, h2, ta=True, out_dtype=BF16, name="d_w_gate")
    grads["w_up"] = _mm(dup, h2, ta=True, out_dtype=BF16, name="d_w_up")
    dx1, dx1_b, grads["g_ffn"] = _mm_fused(
        dgt, full["w_gate"], lambda acc, x1, dx2, g: _norm_bwd_both_fn(x1, acc, dx2, g), [x1, dx2], [g_ffn],
        [F32, BF16], [_sds((1, D), F32)], full_rows=True, a2=dup, b2=full["w_up"], name="d_h2_norm_ffn_bwd")
    grads["w_o"] = _mm(mix, dx1_b, ta=True, out_dtype=BF16, name="d_w_o")
    token = scatter_start(0)
    dmix = _mm(dx1_b, full["w_o"], tb=True, name="d_mix", after=token)

    half = MLA_HEADS * V_HEAD
    do, dhg, dg_hgo = _stage(_post_bwd_fn, [o_f, o_b, hg, _cols(dmix, half, 1)], [g_hgo_row],
                             [_sds((T, half), F32), _sds((T, half), BF16)], [_sds((1, half), F32)], name="mix_out_bwd",
                             tile=TOK_TILE_LIGHT)
    grads["g_hgo"] = dg_hgo
    dhq_f, dhi_f, dhf_f, dlow_f = _gla_bwd(z, lower3[0], st_f, do, None, rev=False,
                                           col_q=_Z_HQ, col_f=_Z_HFF, col_v=_Z_HI, hp=GLA_BWD_HEADS)
    dhq, dhi, dhf_b, dlow_b = _gla_bwd(z, lower3[1], st_b, do, (dhq_f, dhi_f), rev=True,
                                       col_q=_Z_HQ, col_f=_Z_HFB, col_v=_Z_HI, hp=GLA_BWD_HEADS)

    dq, dk, dv = _attention_bwd(q, k, v, att, lse, dmix)
    dq_raw, dkv_raw, dkr, grads["g_qn"], grads["g_kn"] = _stage(
        _mla_b_bwd_fn, [q_raw, kv_raw, kr, cos, sin, dq, dk, dv], [g_qn, g_kn],
        [_sds(q_raw.shape, BF16), _sds(kv_raw.shape, BF16), _sds((T, LANE), BF16)],
        [_sds(g_qn.shape, F32), _sds(g_kn.shape, F32)], name="mla_qk_norm_rope_bwd")
    grads["w_qb"] = _mm(dq_raw, qn, ta=True, out_dtype=BF16, name="d_w_qb")
    grads["w_kvb"] = _mm(kvn, dkv_raw, ta=True, out_dtype=BF16, name="d_w_kvb")
    dqn = _mm(dq_raw, full["w_qb"], name="d_qn")
    dkvn = _mm(dkv_raw, full["w_kvb"], tb=True, name="d_kvn")
    dcq, dckv, grads["g_qa"], grads["g_kva"] = _stage(
        _mla_a_bwd_fn, [_cols(z, 256, 0), _cols(z, 256, 1), dqn, dkvn], [g_qa, g_kva],
        [_sds((T, 256), BF16), _sds((T, 256), BF16)], [_sds(g_qa.shape, F32), _sds(g_kva.shape, F32)],
        name="mla_latent_norm_bwd", tile=TOK_TILE_LIGHT)
    token = reduce_group(0, dcq)
    dz = jnp.concatenate([dcq, dckv, dhq, dhf_f, dhf_b, dhi, dhg, dkr], axis=1)
    grads["w_in"] = _from_z_layout(_mm(dz, h1, ta=True, out_dtype=BF16, name="d_w_in", after=token))
    token = scatter_start(1)
    grad_x, grads["g_mix"] = _mm_fused(
        dz, w_in_zt, lambda acc, x, dx1, g: _norm_bwd_fn(x, acc, dx1, g), [x2d, dx1], [g_mix],
        [F32], [_sds((1, D), F32)], full_rows=True, name="d_h1_norm_mix_bwd", after=token)

    out_g, out_d, out_m, out_v = {}, {}, {}, {}

    def update_group(gi, after):
        mine, theirs = _exchange_wait(swap_started[gi], after, name=f"swap_wait_{gi}")
        tokens = []
        for n, a, b in zip(scatter_groups[gi], mine, theirs):
            *res, token = _adamw(view(n, w_named[n]), a, b, view(n, m_named[n]), view(n, v_named[n]), name="adamw_" + n)
            out_g[n], out_d[n], out_m[n], out_v[n] = (unview(n, t) for t in res)
            tokens.append(token)
        return tokens

    done = update_group(0, grads["g_mix"])
    token = reduce_group(1, done)
    done = update_group(1, token)

    small = ("g_mix", "g_qa", "g_kva", "g_qn", "g_kn", "g_hgo", "g_ffn", "g_ple")
    small_all = small + ("lb_param",)
    width = -(-max(w_named[n].size for n in small_all) // LANE) * LANE

    def row(a):
        a = a.reshape(1, -1)
        return jnp.pad(a, ((0, 0), (0, width - a.shape[1])))

    dlower = jnp.concatenate([dlow_f.reshape(1, -1), dlow_b.reshape(1, -1)], axis=0)
    pack = jnp.concatenate([row(grads[n]) for n in small] + [row(dlower[0]), row(dlower[1]), row(loss_part)]
                           + [jnp.zeros((5, width), F32)], axis=0)
    red = _allreduce_small(pack, done[-1])
    loss = red[10, 0]

    outs = _adamw_small(red, lb_full, [w_named[n] for n in small_all], [m_named[n] for n in small_all],
                        [v_named[n] for n in small_all])
    for i, n in enumerate(small_all):
        out_g[n], out_d[n], out_m[n], out_v[n] = outs[4 * i:4 * i + 4]

    return (loss, grad_x[None], *[out_g[n] for n in order], *[out_d[n] for n in order],
            *[out_m[n] for n in order], *[out_v[n] for n in order])
```

```python
import functools
import itertools

import jax
import jax.numpy as jnp
from jax import lax
from jax.experimental import pallas as pl
from jax.experimental.pallas import tpu as pltpu

F32 = jnp.float32
BF16 = jnp.bfloat16
MESH = pl.DeviceIdType.MESH

EPS = 1e-6
ROPE_THETA = 10000.0
MLA_HEADS = 4
QK_NOPE = 128
QK_ROPE = 64
QK_HEAD = QK_NOPE + QK_ROPE
V_HEAD = 128
HG_HEADS = 4
HG_DK = 128
CHUNK = 64
ADAM_LR = 0.001
ADAM_B1 = 0.9
ADAM_B2 = 0.999
ADAM_EPS = 1e-08
ADAM_WD = 0.01
ADAM_STEP = 10

LANE = 128
VMEM_LIMIT = 56 * 1024 * 1024
TOK_TILE = 512
TOK_TILE_LIGHT = 1024
GLA_GROUP = 16
GLA_FWD_HEADS = 4
GLA_BWD_HEADS = 4
ATT_TQ = 1024
ATT_TK = 1024
ATT_CHUNK = 1024
LOG2_E = 1.4426950408889634
N_CHIPS = 4
N_DEV = 8


_ANY = pl.BlockSpec(memory_space=pl.ANY)


def _params(dims=None, **kw):
    return pltpu.CompilerParams(dimension_semantics=dims, vmem_limit_bytes=VMEM_LIMIT, **kw)


def _tile_candidates(n, cap):
    out = [d for d in range(LANE, min(n, cap) + 1, LANE) if n % d == 0]
    if n <= cap and n not in out:
        out.append(n)
    return out or [n]


MM_VMEM_BUDGET = 40 * 1024 * 1024
MM_MIN_ROWS = 256
MM_MAX_ROWS = 1536
HBM_BYTES_PER_S = 2.8e12
MXU_FLOPS_PER_S = 8e14
STEP_OVERHEAD_S = 0.35e-6


def _mm_tiles(M, N, K, a_bytes, b_bytes, o_bytes, has_add, full_rows=False, full_k=False, n_a=1, n_b=1):
    cast_a, cast_b = a_bytes > 2, b_bytes > 2
    a_bytes, b_bytes = n_a * a_bytes, n_b * b_bytes
    best = None
    for tm in [t for t in _tile_candidates(M, MM_MAX_ROWS) if t >= min(M, MM_MIN_ROWS)]:
        for tn in ([N] if full_rows else _tile_candidates(N, N)):
            for tk in ([K] if full_k else _tile_candidates(K, K)):
                ni, nj, nk = M // tm, N // tn, K // tk
                vmem = 2 * (tm * tk * a_bytes + tk * tn * b_bytes + tm * tn * o_bytes * (2 if has_add else 1))
                vmem += tm * tn * 4 * (2 if nk > 1 else 1)
                vmem += (tm * tk * 2 * n_a if cast_a else 0) + (tk * tn * 2 * n_b if cast_b else 0)
                if vmem > MM_VMEM_BUDGET:
                    continue
                moved = M * K * a_bytes * (nj if nk > 1 else 1) + K * N * b_bytes * (1 if nj == nk == 1 else ni)
                moved += M * N * o_bytes * (2 if has_add else 1)
                t = max(moved / HBM_BYTES_PER_S, 2 * M * N * K / MXU_FLOPS_PER_S) + ni * nj * nk * STEP_OVERHEAD_S
                if best is None or t < best[0]:
                    best = (t, tm, tn, tk)
    assert best is not None, (M, N, K)
    return best[1:]


def _dot_raw(a, b, kind):
    nb = a.ndim - 2
    batch = ((0,), (0,)) if nb else ((), ())
    ca = nb if kind == "tn" else nb + 1
    cb = nb + 1 if kind == "nt" else nb
    return lax.dot_general(a.astype(BF16), b.astype(BF16), (((ca,), (cb,)), batch), preferred_element_type=F32)


@functools.partial(jax.custom_vjp, nondiff_argnums=(2,))
def _bdot(a, b, kind):
    return _dot_raw(a, b, kind)


def _bdot_fwd(a, b, kind):
    return _dot_raw(a, b, kind), (a, b)


def _bdot_bwd(kind, res, g):
    a, b = res
    if kind == "nn":
        da, db = _bdot(g, b, "nt"), _bdot(a, g, "tn")
    elif kind == "nt":
        da, db = _bdot(g, b, "nn"), _bdot(g, a, "tn")
    else:
        da, db = _bdot(b, g, "nt"), _bdot(a, g, "nn")
    return da.astype(a.dtype), db.astype(b.dtype)


_bdot.defvjp(_bdot_fwd, _bdot_bwd)


def _mm(a, b, *, name, ta=False, tb=False, add=None, out_dtype=F32, after=None):
    K, M = a.shape if ta else a.shape[::-1]
    N, Kb = b.shape if tb else b.shape[::-1]
    assert K == Kb, (a.shape, b.shape, ta, tb)
    tm, tn, tk = _mm_tiles(M, N, K, a.dtype.itemsize, b.dtype.itemsize, jnp.dtype(out_dtype).itemsize, add is not None)
    nk = K // tk
    kind = "tn" if ta else ("nt" if tb else "nn")
    assert not (ta and tb)
    a_spec = pl.BlockSpec((tk, tm), lambda i, j, k: (k, i)) if ta else pl.BlockSpec((tm, tk), lambda i, j, k: (i, k))
    b_spec = pl.BlockSpec((tn, tk), lambda i, j, k: (j, k)) if tb else pl.BlockSpec((tk, tn), lambda i, j, k: (k, j))
    o_spec = pl.BlockSpec((tm, tn), lambda i, j, k: (i, j))
    has_add = add is not None

    def body(*refs):
        a_ref, b_ref = refs[0], refs[1]
        add_ref = refs[2] if has_add else None
        o_ref = refs[n_in]
        part = _dot_raw(a_ref[...], b_ref[...], kind)
        if nk == 1:
            if has_add:
                part = part + add_ref[...].astype(F32)
            o_ref[...] = part.astype(o_ref.dtype)
            return
        acc_ref = refs[-1]
        k = pl.program_id(2)

        @pl.when(k == 0)
        def _():
            acc_ref[...] = part

        @pl.when(k > 0)
        def _():
            acc_ref[...] += part

        @pl.when(k == nk - 1)
        def _():
            r = acc_ref[...]
            if has_add:
                r = r + add_ref[...].astype(F32)
            o_ref[...] = r.astype(o_ref.dtype)

    ins = [a, b] + ([add] if has_add else []) + ([after] if after is not None else [])
    in_specs = [a_spec, b_spec] + ([o_spec] if has_add else []) + ([_ANY] if after is not None else [])
    n_in = len(ins)
    return pl.pallas_call(
        body, name=name, grid=(M // tm, N // tn, nk), in_specs=in_specs, out_specs=o_spec,
        out_shape=jax.ShapeDtypeStruct((M, N), out_dtype),
        scratch_shapes=[pltpu.VMEM((tm, tn), F32)] if nk > 1 else [],
        compiler_params=_params(("parallel", "parallel", "arbitrary")),
    )(*ins)


def _mm_fused(a, b, fn, tiles, params, out_dtypes, sums, *, name, ta=False, tb=False, full_rows=False, after=None,
              b2=None, a2=None):
    K, M = a.shape if ta else a.shape[::-1]
    N, Kb = b.shape if tb else b.shape[::-1]
    assert K == Kb and not (ta and tb), (a.shape, b.shape, ta, tb)
    per_elem = sum(t.dtype.itemsize for t in tiles) + sum(jnp.dtype(d).itemsize for d in out_dtypes)
    n_b = 1 if b2 is None else 2
    n_a = 1 if a2 is None else 2
    tm, tn, tk = _mm_tiles(M, N, K, a.dtype.itemsize, b.dtype.itemsize, per_elem, False, full_rows, b2 is not None,
                           n_a, n_b)
    nk = K // tk
    kind = "tn" if ta else ("nt" if tb else "nn")
    a_spec = pl.BlockSpec((tk, tm), lambda i, j, k: (k, i)) if ta else pl.BlockSpec((tm, tk), lambda i, j, k: (i, k))
    b_spec = pl.BlockSpec((tn, tk), lambda i, j, k: (j, k)) if tb else pl.BlockSpec((tk, tn), lambda i, j, k: (k, j))
    o_spec = pl.BlockSpec((tm, tn), lambda i, j, k: (i, j))
    ins = [a, b] + ([b2] if b2 is not None else []) + ([a2] if a2 is not None else [])
    ins += list(tiles) + list(params) + ([after] if after is not None else [])
    in_specs = [a_spec] + [b_spec] * n_b + [a_spec] * (n_a - 1) + [o_spec] * len(tiles)
    in_specs += [pl.BlockSpec(p.shape, lambda i, j, k, nd=p.ndim: (0,) * nd) for p in params]
    in_specs += [_ANY] if after is not None else []
    n_in, n_t, n_p, n_o = len(ins), len(tiles), len(params), len(out_dtypes)

    def body(*refs):
        outs, sum_refs = refs[n_in:n_in + n_o], refs[n_in + n_o:n_in + n_o + len(sums)]

        def finish(*products):
            res = fn(*products, *[t[...] for t in refs[n_a + n_b:n_a + n_b + n_t + n_p]])
            for o_ref, v in zip(outs, res[:n_o]):
                o_ref[...] = v.astype(o_ref.dtype)
            first = jnp.logical_and(pl.program_id(0) == 0, pl.program_id(1) == 0)
            for s_ref, v in zip(sum_refs, res[n_o:]):
                @pl.when(first)
                def _(s_ref=s_ref, v=v):
                    s_ref[...] = v

                @pl.when(jnp.logical_not(first))
                def _(s_ref=s_ref, v=v):
                    s_ref[...] += v

        part = _dot_raw(refs[0][...], refs[1][...], kind)
        if nk == 1 and a2 is not None:
            finish(part + _dot_raw(refs[3][...], refs[2][...], kind))
            return
        if nk == 1:
            finish(part, *([_dot_raw(refs[0][...], refs[2][...], kind)] if b2 is not None else []))
            return
        acc_ref = refs[-1]
        k = pl.program_id(2)

        @pl.when(k == 0)
        def _():
            acc_ref[...] = part

        @pl.when(k > 0)
        def _():
            acc_ref[...] += part

        @pl.when(k == nk - 1)
        def _():
            finish(acc_ref[...])

    out_shape = [_sds((M, N), d) for d in out_dtypes] + list(sums)
    out_specs = [o_spec] * n_o + [pl.BlockSpec(s.shape, lambda i, j, k, nd=len(s.shape): (0,) * nd) for s in sums]
    order = ("arbitrary",) * 3 if sums else ("parallel", "parallel", "arbitrary")
    return pl.pallas_call(
        body, name=name, grid=(M // tm, N // tn, nk), in_specs=in_specs, out_specs=out_specs, out_shape=out_shape,
        scratch_shapes=[pltpu.VMEM((tm, tn), F32)] if nk > 1 else [], compiler_params=_params(order),
    )(*ins)


def _cols(arr, width, block):
    return (arr, width, block)


def _stage(fn, tiles, params, out_tiles, out_sums, *, name, tile=TOK_TILE, after=None):
    first = tiles[0][0] if isinstance(tiles[0], tuple) else tiles[0]
    tile = min(tile, first.shape[0] if first.ndim == 2 else first.shape[1])

    def tok_spec(shape, width=None, block=0):
        if len(shape) == 2:
            w = shape[1] if width is None else width
            return pl.BlockSpec((tile, w), lambda i: (i, block))
        return pl.BlockSpec((shape[0], tile, shape[2]), lambda i: (0, i, 0))

    arrays, in_specs = [], []
    for t in tiles:
        if isinstance(t, tuple):
            arr, width, block = t
            arrays.append(arr)
            in_specs.append(tok_spec(arr.shape, width, block))
        else:
            arrays.append(t)
            in_specs.append(tok_spec(t.shape))
    n_tok = arrays[0].shape[0] if arrays[0].ndim == 2 else arrays[0].shape[1]
    for p in params:
        arrays.append(p)
        in_specs.append(pl.BlockSpec(p.shape, lambda i, nd=p.ndim: (0,) * nd))
    out_shape = list(out_tiles) + list(out_sums)
    out_specs = [tok_spec(o.shape) for o in out_tiles]
    out_specs += [pl.BlockSpec(o.shape, lambda i, nd=len(o.shape): (0,) * nd) for o in out_sums]
    n_fn, n_ot = len(arrays), len(out_tiles)
    if after is not None:
        arrays.append(after)
        in_specs.append(_ANY)
    n_in = len(arrays)

    def body(*refs):
        res = fn(*[r[...] for r in refs[:n_fn]])
        if not isinstance(res, (tuple, list)):
            res = (res,)
        outs = refs[n_in:]
        for o_ref, r in zip(outs[:n_ot], res[:n_ot]):
            o_ref[...] = r.astype(o_ref.dtype)
        i = pl.program_id(0)
        for o_ref, r in zip(outs[n_ot:], res[n_ot:]):
            @pl.when(i == 0)
            def _(o_ref=o_ref, r=r):
                o_ref[...] = r.astype(o_ref.dtype)

            @pl.when(i > 0)
            def _(o_ref=o_ref, r=r):
                o_ref[...] += r.astype(o_ref.dtype)

    res = pl.pallas_call(
        body, name=name, grid=(n_tok // tile,), in_specs=in_specs, out_specs=out_specs, out_shape=out_shape,
        compiler_params=_params(("arbitrary",)),
    )(*arrays)
    return res


def _sds(shape, dtype):
    return jax.ShapeDtypeStruct(tuple(shape), dtype)


def _sigmoid(x):
    return 0.5 * jnp.tanh(0.5 * x) + 0.5


def _rms(x, g):
    return x * lax.rsqrt(jnp.mean(x * x, axis=-1, keepdims=True) + EPS) * g


def _norm_bwd_fn(x, dh, dres, g):
    r = lax.rsqrt(jnp.mean(x * x, axis=-1, keepdims=True) + EPS)
    xr = x * r
    dhg = dh * g
    dx = r * (dhg - xr * jnp.mean(xr * dhg, axis=-1, keepdims=True))
    return dx + dres, jnp.sum(dh * xr, axis=0, keepdims=True)


def _norm_bwd_both_fn(x, dh, dres, g):
    dx, dg = _norm_bwd_fn(x, dh, dres, g)
    return dx, dx, dg


def _mla_a_fn(cq, ckv, g_qa, g_kva):
    return _rms(cq, g_qa), _rms(ckv, g_kva)


def _mla_a_bwd_fn(cq, ckv, dqn, dkvn, g_qa, g_kva):
    _, vjp = jax.vjp(_mla_a_fn, cq, ckv, g_qa, g_kva)
    return vjp((dqn, dkvn))


def _rope(t, cos, sin):
    t1, t2 = t[:, :QK_ROPE // 2], t[:, QK_ROPE // 2:]
    return jnp.concatenate([t1 * cos - t2 * sin, t1 * sin + t2 * cos], axis=-1)


def _mla_b_fn(q_raw, kv_raw, kr, cos, sin, g_qn, g_kn):
    krope = kr[:, :QK_ROPE]
    qs, ks, vs = [], [], []
    for h in range(MLA_HEADS):
        qh = _rms(q_raw[:, h * QK_HEAD:(h + 1) * QK_HEAD], g_qn)
        kvh = kv_raw[:, h * (QK_NOPE + V_HEAD):(h + 1) * (QK_NOPE + V_HEAD)]
        kh = _rms(jnp.concatenate([kvh[:, :QK_NOPE], krope], axis=-1), g_kn)
        qs.append(jnp.concatenate([qh[:, :QK_NOPE], _rope(qh[:, QK_NOPE:], cos, sin)], axis=-1))
        ks.append(jnp.concatenate([kh[:, :QK_NOPE], _rope(kh[:, QK_NOPE:], cos, sin)], axis=-1))
        vs.append(kvh[:, QK_NOPE:])
    return jnp.stack(qs), jnp.stack(ks), jnp.stack(vs)


def _mla_b_bwd_fn(q_raw, kv_raw, kr, cos, sin, dq, dk, dv, g_qn, g_kn):
    _, vjp = jax.vjp(lambda a, b, c, d, e: _mla_b_fn(a, b, c, cos, sin, d, e), q_raw, kv_raw, kr, g_qn, g_kn)
    return vjp((dq, dk, dv))


def _post_fn(a, o_f, o_b, hg, g_hgo):
    o = o_f + o_b
    parts = [a]
    for h in range(HG_HEADS):
        s = slice(h * HG_DK, (h + 1) * HG_DK)
        gate = hg[:, s]
        parts.append(_rms(o[:, s], g_hgo[:, s]) * (gate * _sigmoid(gate)))
    return jnp.concatenate(parts, axis=-1)


def _post_bwd_fn(o_f, o_b, hg, dr, g_hgo):
    def f(o, hg, g):
        return _post_fn(jnp.zeros_like(o), o, jnp.zeros_like(o), hg, g)[:, o.shape[1]:]
    _, vjp = jax.vjp(f, o_f + o_b, hg, g_hgo)
    return vjp(dr)


def _swiglu_fn(gt, up):
    return gt * _sigmoid(gt) * up


def _resid_norm_fn(acc, x, g):
    x_new = acc + x
    return x_new, _rms(x_new, g)


def _swiglu_bwd_fn(gt, up, dact):
    s = _sigmoid(gt)
    silu = gt * s
    return dact * up * (s + silu * (1.0 - s)), dact * silu


def _ple_loss_fn(x2, pg, pp, target):
    gate = _sigmoid(pg)
    err = x2 + gate * pp - target
    dx3 = err * (1.0 / err.shape[-1])
    loss = 0.5 * jnp.sum(jnp.mean(err * err, axis=-1, keepdims=True), axis=0, keepdims=True)
    return dx3, dx3 * pp * gate * (1.0 - gate), dx3 * gate, loss


def _attention_fwd(q, k, v):
    H, T, D = q.shape
    DV = v.shape[-1]
    tq, ck = min(ATT_TQ, T), min(ATT_CHUNK, T)
    c2 = (D ** -0.5) * LOG2_E

    def body(q_ref, k_ref, v_ref, o_ref, lse_ref):
        q_i = q_ref[0]

        def chunk(c, carry):
            m, l, acc = carry
            rows = pl.ds(pl.multiple_of(c * ck, ck), ck)
            s = _dot_raw(q_i, k_ref[0, rows, :], "nt")
            m_new = jnp.maximum(m, jnp.max(s, axis=-1, keepdims=True))
            p = jnp.exp2((s - m_new) * c2)
            alpha = jnp.exp2((m - m_new) * c2)
            l = l * alpha + jnp.sum(p, axis=-1, keepdims=True)
            acc = acc * alpha + _dot_raw(p, v_ref[0, rows, :], "nn")
            return m_new, l, acc

        init = (jnp.full((tq, 1), -jnp.inf, F32), jnp.zeros((tq, 1), F32), jnp.zeros((tq, DV), F32))
        m, l, acc = lax.fori_loop(0, T // ck, chunk, init, unroll=True)
        o_ref[...] = acc / l
        lse_ref[0] = m * c2 + jnp.log2(l)

    return pl.pallas_call(
        body, name="attention_fwd", grid=(H, T // tq),
        in_specs=[pl.BlockSpec((1, tq, D), lambda h, i: (h, i, 0)),
                  pl.BlockSpec((1, T, D), lambda h, i: (h, 0, 0)),
                  pl.BlockSpec((1, T, DV), lambda h, i: (h, 0, 0))],
        out_specs=[pl.BlockSpec((tq, DV), lambda h, i: (i, h)),
                   pl.BlockSpec((1, tq, 1), lambda h, i: (h, i, 0))],
        out_shape=[_sds((T, H * DV), F32), _sds((H, T, 1), F32)],
        compiler_params=_params(("parallel", "parallel")),
    )(q, k, v)


def _attention_bwd(q, k, v, o, lse2, dmix):
    H, T, D = q.shape
    DV = v.shape[-1]
    tk, cq = min(ATT_TK, T), min(ATT_CHUNK, T)
    scale = D ** -0.5
    c2 = scale * LOG2_E

    def body(q_ref, k_ref, v_ref, o_ref, lse_ref, do_ref, dq_ref, dk_ref, dv_ref, delta_ref):
        j = pl.program_id(1)

        @pl.when(j == 0)
        def _():
            delta = lax.dot_general(jnp.ones((8, DV), F32), do_ref[...] * o_ref[...], (((1,), (1,)), ((), ())),
                                    precision=lax.Precision.HIGHEST, preferred_element_type=F32)
            for i in range(T // cq):
                delta_ref[i] = delta[:, i * cq:(i + 1) * cq]
            dq_ref[0] = jnp.zeros((T, D), F32)

        k_j, v_j = k_ref[0], v_ref[0]
        dk_ref[0] = jnp.zeros((tk, D), F32)
        dv_ref[0] = jnp.zeros((tk, DV), F32)

        def chunk(c, carry):
            rows = pl.ds(pl.multiple_of(c * cq, cq), cq)
            q_c = q_ref[0, rows, :]
            do_c = do_ref[rows, :].astype(BF16)
            st = _dot_raw(k_j, q_c, "nt")
            pt = jnp.exp2(st * c2 - lse_ref[0, c])
            dv_ref[0] += _dot_raw(pt, do_c, "nn")
            dpt = _dot_raw(v_j, do_c, "nt")
            dst = pt * (dpt - delta_ref[c, 0:1, :]) * scale
            dk_ref[0] += _dot_raw(dst, q_c, "nn")
            dq_ref[0, rows, :] += _dot_raw(dst, k_j, "tn")
            return carry

        lax.fori_loop(0, T // cq, chunk, 0, unroll=True)

    return pl.pallas_call(
        body, name="attention_bwd", grid=(H, T // tk),
        in_specs=[pl.BlockSpec((1, T, D), lambda h, j: (h, 0, 0)),
                  pl.BlockSpec((1, tk, D), lambda h, j: (h, j, 0)),
                  pl.BlockSpec((1, tk, DV), lambda h, j: (h, j, 0)),
                  pl.BlockSpec((T, DV), lambda h, j: (0, h)),
                  pl.BlockSpec((1, T // cq, 1, cq), lambda h, j: (h, 0, 0, 0)),
                  pl.BlockSpec((T, DV), lambda h, j: (0, h))],
        out_specs=[pl.BlockSpec((1, T, D), lambda h, j: (h, 0, 0)),
                   pl.BlockSpec((1, tk, D), lambda h, j: (h, j, 0)),
                   pl.BlockSpec((1, tk, DV), lambda h, j: (h, j, 0))],
        out_shape=[_sds((H, T, D), F32), _sds((H, T, D), F32), _sds((H, T, DV), F32)],
        scratch_shapes=[pltpu.VMEM((T // cq, 8, cq), F32)],
        compiler_params=_params(("parallel", "arbitrary")),
    )(q, k, v, o, lse2.reshape(H, T // cq, 1, cq), dmix)


def _split3_dot(ones, x, kind):
    hi = x.astype(BF16)
    rest = x - hi.astype(F32)
    mid = rest.astype(BF16)
    lo = (rest - mid.astype(F32)).astype(BF16)
    return (_dot_raw(ones, hi, kind) + _dot_raw(ones, mid, kind)) + _dot_raw(ones, lo, kind)


@jax.custom_vjp
def _running_sum(x, tri):
    return _split3_dot(tri, x, "nn")


def _running_sum_fwd(x, tri):
    return _split3_dot(tri, x, "nn"), tri


def _running_sum_bwd(tri, g):
    return _split3_dot(tri, g, "tn"), jnp.zeros_like(tri)


_running_sum.defvjp(_running_sum_fwd, _running_sum_bwd)


def _gla_block(hq, hf, hi, lower, st_in, *, rev, dot):
    rows, dk = hq.shape
    G, C = rows // CHUNK, CHUNK
    q = hq * _sigmoid(hq)
    f = lower + (1.0 - lower) * _sigmoid(hf)
    k = 1.0 - f
    logf = jnp.log2(f)
    q3, k3, v3, lf3 = (t.reshape(G, C, dk) for t in (q, k, hi, logf))
    r = lax.broadcasted_iota(jnp.int32, (C, C), 0)
    c = lax.broadcasted_iota(jnp.int32, (C, C), 1)
    tri = ((r <= c) if rev else (r >= c)).astype(F32)
    b = _running_sum(lf3, jnp.broadcast_to(tri, (G, C, C)))
    tpos = lax.broadcasted_iota(jnp.int32, (1, C, 1), 1)
    first_half = (tpos >= C // 2) if rev else (tpos <= C // 2 - 1)
    b_mid = jnp.sum(jnp.where(first_half, lf3, 0.0), axis=1, keepdims=True)
    b_last = jnp.sum(lf3, axis=1, keepdims=True)
    a = dot(q3 * jnp.exp2(b - b_mid), k3 * jnp.exp2(b_mid - b), "nt") * tri
    o_intra = dot(a, v3, "nn")
    kv_t = dot(v3, k3 * jnp.exp2(b_last - b), "tn")
    decay = jnp.exp2(b_last)
    qd = q3 * jnp.exp2(b)
    st = st_in
    o_inter = [None] * G
    for g in (reversed(range(G)) if rev else range(G)):
        o_inter[g] = dot(qd[g], st, "nt")
        st = st * decay[g] + kv_t[g]
    o = o_intra.reshape(rows, dk) + jnp.concatenate(o_inter, axis=0)
    return o, st


def _gla_fwd(z, lower3, *, rev, col_q, col_f, col_v, hp):
    T = z.shape[0]
    rows = min(GLA_GROUP * CHUNK, T)
    nb = T // rows
    wide = hp * HG_DK
    blk = (lambda n: nb - 1 - n) if rev else (lambda n: n)

    def body(hq_ref, hf_ref, hi_ref, low_ref, o_ref, st_out_ref, st_ref):
        @pl.when(pl.program_id(1) == 0)
        def _():
            st_ref[...] = jnp.zeros_like(st_ref)

        st_in = [st_ref[i] for i in range(hp)]
        heads = []
        for i in range(hp):
            cols = slice(i * HG_DK, (i + 1) * HG_DK)
            heads.append(_gla_block(hq_ref[:, cols], hf_ref[:, cols], hi_ref[:, cols], low_ref[i], st_in[i], rev=rev,
                                    dot=_dot_raw))
        for i, (o, st) in enumerate(heads):
            st_out_ref[i, 0] = st_in[i]
            o_ref[:, i * HG_DK:(i + 1) * HG_DK] = o
            st_ref[i] = st

    def zspec(col):
        return pl.BlockSpec((rows, wide), lambda h, n: (blk(n), col // wide + h))

    return pl.pallas_call(
        body, name="gla_fwd_rev" if rev else "gla_fwd", grid=(HG_HEADS // hp, nb),
        in_specs=[zspec(col_q), zspec(col_f), zspec(col_v), pl.BlockSpec((hp, 1, HG_DK), lambda h, n: (h, 0, 0))],
        out_specs=[pl.BlockSpec((rows, wide), lambda h, n: (blk(n), h)),
                   pl.BlockSpec((hp, 1, HG_DK, HG_DK), lambda h, n: (h, blk(n), 0, 0))],
        out_shape=[_sds((T, HG_HEADS * HG_DK), F32), _sds((HG_HEADS, nb, HG_DK, HG_DK), F32)],
        scratch_shapes=[pltpu.VMEM((hp, HG_DK, HG_DK), F32)],
        compiler_params=_params(("parallel", "arbitrary")),
    )(z, z, z, lower3)


def _gla_bwd(z, lower3, states, do, prev, *, rev, col_q, col_f, col_v, hp):
    T = z.shape[0]
    rows = min(GLA_GROUP * CHUNK, T)
    nb = T // rows
    wide = hp * HG_DK
    blk = (lambda n: n) if rev else (lambda n: nb - 1 - n)
    has_prev = prev is not None
    fn = functools.partial(_gla_block, rev=rev, dot=_bdot)

    def body(*refs):
        hq_ref, hf_ref, hi_ref, low_ref, st_ref, do_ref = refs[:6]
        rest = refs[6:]
        if has_prev:
            pq_ref, pi_ref = rest[:2]
            rest = rest[2:]
        dhq_ref, dhi_ref, dhf_ref, dlow_ref, dst_ref = rest
        n = pl.program_id(1)

        @pl.when(n == 0)
        def _():
            dst_ref[...] = jnp.zeros_like(dst_ref)

        dst_in = [dst_ref[i] for i in range(hp)]
        heads = []
        for i in range(hp):
            cols = slice(i * HG_DK, (i + 1) * HG_DK)
            _, vjp = jax.vjp(fn, hq_ref[:, cols], hf_ref[:, cols], hi_ref[:, cols], low_ref[i], st_ref[i, 0])
            dhq, dhf, dhi, dlow, dst = vjp((do_ref[:, cols], dst_in[i]))
            if has_prev:
                dhq = dhq + pq_ref[:, cols]
                dhi = dhi + pi_ref[:, cols]
            heads.append((dhq, dhf, dhi, dlow, dst))
        for i, (dhq, dhf, dhi, dlow, dst) in enumerate(heads):
            cols = slice(i * HG_DK, (i + 1) * HG_DK)
            dst_ref[i] = dst
            dhq_ref[:, cols] = dhq.astype(dhq_ref.dtype)
            dhi_ref[:, cols] = dhi.astype(dhi_ref.dtype)
            dhf_ref[:, cols] = dhf.astype(dhf_ref.dtype)

        @pl.when(n == 0)
        def _():
            for i in range(hp):
                dlow_ref[i] = heads[i][3]

        @pl.when(n > 0)
        def _():
            for i in range(hp):
                dlow_ref[i] += heads[i][3]

    def zspec(col):
        return pl.BlockSpec((rows, wide), lambda h, n: (blk(n), col // wide + h))

    hspec = pl.BlockSpec((rows, wide), lambda h, n: (blk(n), h))
    in_specs = [zspec(col_q), zspec(col_f), zspec(col_v), pl.BlockSpec((hp, 1, HG_DK), lambda h, n: (h, 0, 0)),
                pl.BlockSpec((hp, 1, HG_DK, HG_DK), lambda h, n: (h, blk(n), 0, 0)), hspec]
    ins = [z, z, z, lower3, states, do]
    if has_prev:
        in_specs += [hspec, hspec]
        ins += list(prev)
    full_wide = HG_HEADS * HG_DK
    acc_dtype = BF16 if has_prev else F32
    return pl.pallas_call(
        body, name="gla_bwd_rev" if rev else "gla_bwd", grid=(HG_HEADS // hp, nb),
        in_specs=in_specs,
        out_specs=[hspec, hspec, hspec, pl.BlockSpec((hp, 1, HG_DK), lambda h, n: (h, 0, 0))],
        out_shape=[_sds((T, full_wide), acc_dtype), _sds((T, full_wide), acc_dtype), _sds((T, full_wide), BF16),
                   _sds((HG_HEADS, 1, HG_DK), F32)],
        scratch_shapes=[pltpu.VMEM((hp, HG_DK, HG_DK), F32)],
        compiler_params=_params(("parallel", "arbitrary")),
    )(*ins)


def _lower_fn(lb):
    e = jnp.exp(lb - jnp.max(lb, axis=0, keepdims=True))
    return (e / jnp.sum(e, axis=0, keepdims=True))[0]


def _lower_bounds(lb):
    def body(lb_ref, o_ref):
        o_ref[...] = _lower_fn(lb_ref[...])
    return pl.pallas_call(body, name="lower_bounds", out_shape=_sds(lb.shape[1:], F32))(lb)


def _row_tile(r, cap=1024):
    best = None
    for t in range(16, min(r, cap) + 1, 16):
        if r % t == 0:
            best = t
    return best if best is not None else r


def _sum4(shards, recv, chip, *, name):
    _, R, C = shards.shape
    tr = _row_tile(R)

    def body(chip_ref, o_ref, r_ref, out_ref):
        out_ref[...] = ((o_ref[0].astype(F32) + r_ref[0].astype(F32)) + r_ref[1].astype(F32)) + r_ref[2].astype(F32)

    grid_spec = pltpu.PrefetchScalarGridSpec(
        num_scalar_prefetch=1, grid=(R // tr,),
        in_specs=[pl.BlockSpec((1, tr, C), lambda i, chip_ref: (chip_ref[0], i, 0)),
                  pl.BlockSpec((3, tr, C), lambda i, chip_ref: (0, i, 0))],
        out_specs=pl.BlockSpec((tr, C), lambda i, chip_ref: (i, 0)))
    return pl.pallas_call(
        body, name=name, grid_spec=grid_spec, out_shape=_sds((R, C), F32), compiler_params=_params(("parallel",)),
    )(chip, shards, recv)


def _adamw_math(w, g, m, v):
    m = ADAM_B1 * m + (1.0 - ADAM_B1) * g
    v = ADAM_B2 * v + (1.0 - ADAM_B2) * (g * g)
    m_hat = m / (1.0 - ADAM_B1 ** ADAM_STEP)
    v_hat = v / (1.0 - ADAM_B2 ** ADAM_STEP)
    delta = -ADAM_LR * (m_hat / (jnp.sqrt(v_hat) + ADAM_EPS) + ADAM_WD * w)
    return delta, m, v


def _adamw(w, g_a, g_b, m, v, *, name):
    R, C = w.shape
    tr = _row_tile(R)
    two = g_b is not None

    def body(*refs):
        w_ref, ga_ref = refs[0], refs[1]
        rest = refs[2:]
        g = ga_ref[...]
        if two:
            g = g + rest[0][...]
            rest = rest[1:]
        m_ref, v_ref, g_out, d_out, m_out, v_out, token = rest
        delta, m_new, v_new = _adamw_math(w_ref[...], g, m_ref[...], v_ref[...])
        g_out[...] = g
        d_out[...] = delta
        m_out[...] = m_new
        v_out[...] = v_new
        token[...] = jnp.zeros_like(token)

    spec = pl.BlockSpec((tr, C), lambda i: (i, 0))
    ins = [w, g_a] + ([g_b] if two else []) + [m, v]
    return pl.pallas_call(
        body, name=name, grid=(R // tr,), in_specs=[spec] * len(ins),
        out_specs=[spec] * 4 + [pl.BlockSpec((8, LANE), lambda i: (0, 0))],
        out_shape=[_sds((R, C), F32)] * 4 + [_sds((8, LANE), F32)], compiler_params=_params(("arbitrary",)),
    )(*ins)


def _adamw_small(red, lb_full, ws, ms, vs):
    n = len(ws)

    def pieces(shape):
        out = []
        for j, idx in enumerate(itertools.product(*[range(d) for d in shape[:-1]])):
            out.append((idx[:-1] + (slice(idx[-1], idx[-1] + 1), slice(None)), j * shape[-1]))
        return out

    def body(*refs):
        red_ref, lb_ref = refs[0], refs[1]
        w_refs, m_refs, v_refs = refs[2:2 + n], refs[2 + n:2 + 2 * n], refs[2 + 2 * n:2 + 3 * n]
        out_refs = refs[2 + 3 * n:]
        chip = 2 * lax.axis_index("x") + lax.axis_index("y")
        n_f, shard = lb_ref.shape[-1], w_refs[n - 1].shape[-1]
        _, vjp = jax.vjp(_lower_fn, lb_ref[...])
        dlb = vjp(red_ref[8:10, 0:n_f])[0]
        for i in range(n):
            width = w_refs[i].shape[-1]
            for j, (at, lane) in enumerate(pieces(w_refs[i].shape)):
                if i < n - 1:
                    g = red_ref[i:i + 1, lane:lane + width]
                else:
                    row = dlb[j // 2][j % 2:j % 2 + 1]
                    g = sum(jnp.where(chip == q, row[:, q * shard:(q + 1) * shard], 0.0) for q in range(N_CHIPS))
                delta, m_new, v_new = _adamw_math(w_refs[i][at], g, m_refs[i][at], v_refs[i][at])
                for o_ref, val in zip(out_refs[4 * i:4 * i + 4], (g, delta, m_new, v_new)):
                    o_ref[at] = val

    return pl.pallas_call(
        body, name="adamw_small", out_shape=[_sds(w.shape, F32) for w in ws for _ in range(4)],
    )(red, lb_full, *ws, *ms, *vs)


def _chip_peers():
    x, y, c = lax.axis_index("x"), lax.axis_index("y"), lax.axis_index("c")
    return (x, y, c), 2 * x + y, [(1 - x, y), (x, 1 - y), (1 - x, 1 - y)]


_HBM = pl.BlockSpec(memory_space=pltpu.HBM)
_SEM = pl.BlockSpec(memory_space=pltpu.SEMAPHORE)
_EFFECT = pltpu.SideEffectType.DATAFLOW_SIDE_EFFECTING


def _exchange_copies(srcs, lands, sems, mode):
    (x, y, c), me, chips = _chip_peers()
    copies = []
    for t, (src, land) in enumerate(zip(srcs, lands)):
        if mode == "swap":
            copies.append(pltpu.make_async_remote_copy(src, land, sems[0].at[3 * t], sems[1].at[3 * t],
                                                       device_id=(x, y, 1 - c), device_id_type=MESH))
            continue
        for k, (px, py) in enumerate(chips):
            gather = mode == "gather"
            copies.append(pltpu.make_async_remote_copy(
                src if gather else src.at[2 * px + py], land.at[me] if gather else land.at[k],
                sems[0].at[3 * t + k], sems[1].at[3 * t + k], device_id=(px, py, c), device_id_type=MESH))
        if mode == "gather":
            copies.append(pltpu.make_async_copy(src, land.at[me], sems[2].at[t]))
    return copies


def _exchange_start(srcs, *, mode, name, after=None):
    n = len(srcs)
    n_sem = 3 if mode == "gather" else 2
    n_in = 2 * n + (after is not None)
    land_shape = {"gather": lambda s: (N_CHIPS,) + s.shape, "scatter": lambda s: (3,) + s.shape[1:], "swap": lambda s: s.shape}
    lands = [_sds(land_shape[mode](s), s.dtype) for s in srcs]

    def body(*refs):
        for cp in _exchange_copies(refs[:n], refs[n:2 * n], refs[n_in:n_in + n_sem], mode):
            cp.start()
        token = refs[-1]
        token[...] = jnp.zeros_like(token)

    sem_shapes = [pltpu.SemaphoreType.DMA((3 * n,)), pltpu.SemaphoreType.DMA((3 * n,))]
    sem_shapes += [pltpu.SemaphoreType.DMA((n,))] if mode == "gather" else []
    thru = [pltpu.HBM(s.shape, s.dtype) for s in srcs] + [pltpu.HBM(l.shape, l.dtype) for l in lands]
    res = pl.pallas_call(
        body, name=name, in_specs=[_HBM] * (2 * n) + [_ANY] * (after is not None),
        out_specs=[_SEM] * n_sem + [_HBM] * (2 * n) + [pl.BlockSpec(memory_space=pltpu.VMEM)],
        out_shape=sem_shapes + thru + [_sds((8, LANE), F32)], input_output_aliases={i: n_sem + i for i in range(2 * n)},
        compiler_params=pltpu.CompilerParams(has_side_effects=_EFFECT),
    )(*[pltpu.with_memory_space_constraint(s, pltpu.HBM) for s in srcs],
      *[pltpu.with_memory_space_constraint(lax.empty(l.shape, l.dtype), pltpu.HBM) for l in lands],
      *([after] if after is not None else []))
    return (res[:n_sem], res[n_sem:n_sem + n], res[n_sem + n:n_sem + 2 * n], mode), res[-1]


def _exchange_wait(started, after, *, name):
    sems, srcs, lands, mode = started
    n, n_sem = len(srcs), len(sems)
    after = list(after) if isinstance(after, (list, tuple)) else [after]

    def body(*refs):
        for cp in _exchange_copies(refs[:n], refs[n:2 * n], refs[2 * n:2 * n + n_sem], mode):
            cp.wait()

    res = pl.pallas_call(
        body, name=name, in_specs=[_HBM] * (2 * n) + [_SEM] * n_sem + [_ANY] * len(after), out_specs=[_HBM] * (2 * n),
        out_shape=[pltpu.HBM(a.shape, a.dtype) for a in list(srcs) + list(lands)],
        input_output_aliases={i: i for i in range(2 * n)},
        compiler_params=pltpu.CompilerParams(has_side_effects=_EFFECT),
    )(*srcs, *lands, *sems, *after)
    return res[:n], res[n:]


def _allreduce_small(pack, after):
    R, C = pack.shape

    def body(in_ref, after_ref, out_ref, slots, send_sems, recv_sems):
        x, y, c = lax.axis_index("x"), lax.axis_index("y"), lax.axis_index("c")
        me = 4 * x + 2 * y + c
        slots[me] = in_ref[...]
        copies = []
        for k in range(1, N_DEV):
            peer = (x ^ ((k >> 2) & 1), y ^ ((k >> 1) & 1), c ^ (k & 1))
            cp = pltpu.make_async_remote_copy(in_ref, slots.at[me], send_sems.at[k - 1], recv_sems.at[k - 1],
                                              device_id=peer, device_id_type=MESH)
            cp.start()
            copies.append(cp)
        for cp in copies:
            cp.wait()
        acc = slots[0]
        for d in range(1, N_DEV):
            acc = acc + slots[d]
        out_ref[...] = acc

    return pl.pallas_call(
        body, name="allreduce_small", out_shape=_sds((R, C), F32),
        in_specs=[pl.BlockSpec(memory_space=pltpu.VMEM), _ANY], out_specs=pl.BlockSpec(memory_space=pltpu.VMEM),
        scratch_shapes=[pltpu.VMEM((N_DEV, R, C), F32), pltpu.SemaphoreType.DMA((N_DEV - 1,)),
                        pltpu.SemaphoreType.DMA((N_DEV - 1,))],
        compiler_params=_params(),
    )(pack, after)


_Z_CQ, _Z_CKV, _Z_HQ, _Z_HFF, _Z_HFB, _Z_HI, _Z_HG, _Z_KR, _Z_END = 0, 256, 512, 1024, 1536, 2048, 2560, 3072, 3200


def _to_z_layout(wt):
    pad = jnp.zeros((_Z_END - _Z_KR - QK_ROPE, wt.shape[1]), wt.dtype)
    return jnp.concatenate([wt[:512], wt[512 + QK_ROPE:], wt[512:512 + QK_ROPE], pad], axis=0)


def _from_z_layout(wt):
    return jnp.concatenate([wt[:512], wt[_Z_KR:_Z_KR + QK_ROPE], wt[512:_Z_KR]], axis=0)


def _col_shards_to_full(g):
    return jnp.transpose(g, (1, 0, 2)).reshape(g.shape[1], -1)


def _full_to_col_shards(w):
    r, c = w.shape
    return jnp.transpose(w.reshape(r, N_CHIPS, c // N_CHIPS), (1, 0, 2))


def _full_to_row_shards(w):
    r, c = w.shape
    return w.reshape(N_CHIPS, r // N_CHIPS, c)


def kernel(x, p, positions, g_mix, w_in, g_qa, g_kva, w_qb, w_kvb, g_qn, g_kn, lb_param, g_hgo, w_o, g_ffn, w_gate, w_up, w_down, g_ple, w_ple_gate, w_ple_proj, loss_target, m_g_mix, m_w_in, m_g_qa, m_g_kva, m_w_qb, m_w_kvb, m_g_qn, m_g_kn, m_lb_param, m_g_hgo, m_w_o, m_g_ffn, m_w_gate, m_w_up, m_w_down, m_g_ple, m_w_ple_gate, m_w_ple_proj, v_g_mix, v_w_in, v_g_qa, v_g_kva, v_w_qb, v_w_kvb, v_g_qn, v_g_kn, v_lb_param, v_g_hgo, v_w_o, v_g_ffn, v_w_gate, v_w_up, v_w_down, v_g_ple, v_w_ple_gate, v_w_ple_proj):
    w_named = dict(g_mix=g_mix, w_in=w_in, g_qa=g_qa, g_kva=g_kva, w_qb=w_qb, w_kvb=w_kvb, g_qn=g_qn, g_kn=g_kn,
                   lb_param=lb_param, g_hgo=g_hgo, w_o=w_o, g_ffn=g_ffn, w_gate=w_gate, w_up=w_up, w_down=w_down,
                   g_ple=g_ple, w_ple_gate=w_ple_gate, w_ple_proj=w_ple_proj)
    m_named = dict(g_mix=m_g_mix, w_in=m_w_in, g_qa=m_g_qa, g_kva=m_g_kva, w_qb=m_w_qb, w_kvb=m_w_kvb, g_qn=m_g_qn,
                   g_kn=m_g_kn, lb_param=m_lb_param, g_hgo=m_g_hgo, w_o=m_w_o, g_ffn=m_g_ffn, w_gate=m_w_gate,
                   w_up=m_w_up, w_down=m_w_down, g_ple=m_g_ple, w_ple_gate=m_w_ple_gate, w_ple_proj=m_w_ple_proj)
    v_named = dict(g_mix=v_g_mix, w_in=v_w_in, g_qa=v_g_qa, g_kva=v_g_kva, w_qb=v_w_qb, w_kvb=v_w_kvb, g_qn=v_g_qn,
                   g_kn=v_g_kn, lb_param=v_lb_param, g_hgo=v_g_hgo, w_o=v_w_o, g_ffn=v_g_ffn, w_gate=v_w_gate,
                   w_up=v_w_up, w_down=v_w_down, g_ple=v_g_ple, w_ple_gate=v_w_ple_gate, w_ple_proj=v_w_ple_proj)
    order = list(w_named)
    transposed = ("w_in", "w_qb", "w_gate", "w_up")
    col_sharded = ("w_kvb", "w_ple_proj")
    row_sharded = ("w_o", "w_down", "w_ple_gate")
    big = transposed + col_sharded + row_sharded

    def view(n, a):
        return jnp.transpose(a[0]) if n in transposed else a[0]

    def unview(n, a):
        return (jnp.transpose(a) if n in transposed else a)[None]

    def to_shards(n, g):
        return _full_to_col_shards(g) if n in col_sharded else _full_to_row_shards(g)

    x2d, p2d, tgt = x[0], p[0, 0], loss_target[0]
    T, D = x2d.shape

    lb_flat = lb_param.reshape(-1, lb_param.shape[-1])
    gather_groups = (("w_in",), ("w_qb", "w_kvb"), ("w_o", "w_gate", "w_up", "w_down", "w_ple_gate", "w_ple_proj"))
    gather_started = []

    casts = {n: view(n, w_named[n]).astype(BF16) for n in big}

    def gather_start(gi, after):
        srcs = [casts[n] for n in gather_groups[gi]] + ([lb_flat] if gi == 0 else [])
        started, token = _exchange_start(srcs, mode="gather", name=f"gather_start_{gi}", after=after)
        gather_started.append(started)
        return token

    full = {}

    def gather_wait(gi, after):
        _, got = _exchange_wait(gather_started[gi], after, name=f"gather_wait_{gi}")
        for n, g in zip(gather_groups[gi], got):
            full[n] = _col_shards_to_full(g) if n in col_sharded else g.reshape(-1, g.shape[-1])
        return got

    g_hgo_row = g_hgo.reshape(1, -1)

    inv_freq = ROPE_THETA ** (-jnp.arange(0, QK_ROPE, 2, dtype=F32) / QK_ROPE)
    ang = positions[0].astype(F32)[:, None] * inv_freq
    cos, sin = jnp.cos(ang), jnp.sin(ang)
    token = gather_start(0, None)
    h1 = _stage(_rms, [x2d], [g_mix], [_sds((T, D), BF16)], [], name="norm_mix", after=token, tile=TOK_TILE_LIGHT)[0]
    got = gather_wait(0, [h1, cos, sin] + [casts[n] for g in gather_groups[1:] for n in g])
    token = got[0]
    for gi in range(1, len(gather_groups)):
        token = gather_start(gi, token)
    lb_full = _col_shards_to_full(got[-1]).reshape(lb_param.shape[0], lb_param.shape[1], -1)
    w_in_zt = _to_z_layout(full["w_in"])
    z = _mm(h1, w_in_zt, tb=True, name="in_proj", after=token)
    qn, kvn = _stage(_mla_a_fn, [_cols(z, 256, 0), _cols(z, 256, 1)], [g_qa, g_kva],
                     [_sds((T, 256), BF16), _sds((T, 256), BF16)], [], name="mla_latent_norm", tile=TOK_TILE_LIGHT)
    gather_wait(1, qn)
    q_raw = _mm(qn, full["w_qb"], tb=True, name="q_up")
    kv_raw = _mm(kvn, full["w_kvb"], name="kv_up")
    kr = _cols(z, LANE, _Z_KR // LANE)
    q, k, v = _stage(_mla_b_fn, [q_raw, kv_raw, kr, cos, sin], [g_qn, g_kn],
                     [_sds((MLA_HEADS, T, QK_HEAD), BF16), _sds((MLA_HEADS, T, QK_HEAD), BF16),
                      _sds((MLA_HEADS, T, V_HEAD), BF16)], [], name="mla_qk_norm_rope")
    att, lse = _attention_fwd(q, k, v)

    lower = _lower_bounds(lb_full)
    lower3 = lower.reshape(2, HG_HEADS, 1, HG_DK)
    o_f, st_f = _gla_fwd(z, lower3[0], rev=False, col_q=_Z_HQ, col_f=_Z_HFF, col_v=_Z_HI, hp=GLA_FWD_HEADS)
    o_b, st_b = _gla_fwd(z, lower3[1], rev=True, col_q=_Z_HQ, col_f=_Z_HFB, col_v=_Z_HI, hp=GLA_FWD_HEADS)
    hg = _cols(z, 512, _Z_HG // 512)
    mix = _stage(_post_fn, [att, o_f, o_b, hg], [g_hgo_row], [_sds((T, att.shape[1] + o_f.shape[1]), BF16)], [],
                 name="mix_out", tile=TOK_TILE_LIGHT)[0]
    gather_wait(2, mix)
    x1, h2 = _mm_fused(mix, full["w_o"], _resid_norm_fn, [x2d], [g_ffn], [F32, BF16], [], full_rows=True,
                       name="out_proj")
    gt, up, act = _mm_fused(h2, full["w_gate"], lambda gt, up: (gt, up, _swiglu_fn(gt, up)), [], [], [BF16, BF16, BF16],
                            [], tb=True, b2=full["w_up"], name="ffn_gate_up")
    x2, h3 = _mm_fused(act, full["w_down"], _resid_norm_fn, [x1], [g_ple], [F32, BF16], [], full_rows=True,
                       name="ffn_down")
    pp = _mm(p2d, full["w_ple_proj"], name="ple_proj")
    dx3, dpg, dpp, loss_part = _mm_fused(
        h3, full["w_ple_gate"], lambda acc, pp, x2, tgt: _ple_loss_fn(x2, acc, pp, tgt), [pp, x2, tgt], [],
        [F32, BF16, BF16], [_sds((1, 1), F32)], full_rows=True, name="ple_gate_loss")

    grads = {}
    scatter_groups = (("w_ple_proj", "w_ple_gate", "w_down", "w_gate", "w_up", "w_o"), ("w_qb", "w_kvb", "w_in"))
    scatter_started = []

    def scatter_start(gi):
        srcs = [to_shards(n, grads[n]) for n in scatter_groups[gi]]
        started, token = _exchange_start(srcs, mode="scatter", name=f"scatter_start_{gi}")
        scatter_started.append(started)
        return token

    chip = 2 * lax.axis_index("x") + lax.axis_index("y")
    swap_started = []

    def reduce_group(gi, after):
        shards, recvs = _exchange_wait(scatter_started[gi], after, name=f"scatter_wait_{gi}")
        sums = [_sum4(s, r, chip.reshape(1), name="sum_" + n) for n, s, r in zip(scatter_groups[gi], shards, recvs)]
        started, token = _exchange_start(sums, mode="swap", name=f"swap_start_{gi}")
        swap_started.append(started)
        return token

    grads["w_ple_proj"] = _mm(p2d, dpp, ta=True, out_dtype=BF16, name="d_w_ple_proj")
    grads["w_ple_gate"] = _mm(h3, dpg, ta=True, out_dtype=BF16, name="d_w_ple_gate")
    dx2, dx2_b, grads["g_ple"] = _mm_fused(
        dpg, full["w_ple_gate"], lambda acc, x2, dx3, g: _norm_bwd_both_fn(x2, acc, dx3, g), [x2, dx3], [g_ple],
        [F32, BF16], [_sds((1, D), F32)], tb=True, full_rows=True, name="d_h3_norm_ple_bwd")
    dgt, dup = _mm_fused(dx2_b, full["w_down"], lambda acc, gt, up: _swiglu_bwd_fn(gt.astype(F32), up.astype(F32), acc), [gt, up], [],
                         [BF16, BF16], [], tb=True, name="d_act_swiglu_bwd")
    grads["w_down"] = _mm(act, dx2_b, ta=True, out_dtype=BF16, name="d_w_down")
    grads["w_gate"] = _mm(dgt, h2, ta=True, out_dtype=BF16, name="d_w_gate")
    grads["w_up"] = _mm(dup, h2, ta=True, out_dtype=BF16, name="d_w_up")
    dx1, dx1_b, grads["g_ffn"] = _mm_fused(
        dgt, full["w_gate"], lambda acc, x1, dx2, g: _norm_bwd_both_fn(x1, acc, dx2, g), [x1, dx2], [g_ffn],
        [F32, BF16], [_sds((1, D), F32)], full_rows=True, a2=dup, b2=full["w_up"], name="d_h2_norm_ffn_bwd")
    grads["w_o"] = _mm(mix, dx1_b, ta=True, out_dtype=BF16, name="d_w_o")
    token = scatter_start(0)
    dmix = _mm(dx1_b, full["w_o"], tb=True, name="d_mix", after=token)

    half = MLA_HEADS * V_HEAD
    do, dhg, dg_hgo = _stage(_post_bwd_fn, [o_f, o_b, hg, _cols(dmix, half, 1)], [g_hgo_row],
                             [_sds((T, half), F32), _sds((T, half), BF16)], [_sds((1, half), F32)], name="mix_out_bwd",
                             tile=TOK_TILE_LIGHT)
    grads["g_hgo"] = dg_hgo
    dhq_f, dhi_f, dhf_f, dlow_f = _gla_bwd(z, lower3[0], st_f, do, None, rev=False,
                                           col_q=_Z_HQ, col_f=_Z_HFF, col_v=_Z_HI, hp=GLA_BWD_HEADS)
    dhq, dhi, dhf_b, dlow_b = _gla_bwd(z, lower3[1], st_b, do, (dhq_f, dhi_f), rev=True,
                                       col_q=_Z_HQ, col_f=_Z_HFB, col_v=_Z_HI, hp=GLA_BWD_HEADS)

    dq, dk, dv = _attention_bwd(q, k, v, att, lse, dmix)
    dq_raw, dkv_raw, dkr, grads["g_qn"], grads["g_kn"] = _stage(
        _mla_b_bwd_fn, [q_raw, kv_raw, kr, cos, sin, dq, dk, dv], [g_qn, g_kn],
        [_sds(q_raw.shape, BF16), _sds(kv_raw.shape, BF16), _sds((T, LANE), BF16)],
        [_sds(g_qn.shape, F32), _sds(g_kn.shape, F32)], name="mla_qk_norm_rope_bwd")
    grads["w_qb"] = _mm(dq_raw, qn, ta=True, out_dtype=BF16, name="d_w_qb")
    grads["w_kvb"] = _mm(kvn, dkv_raw, ta=True, out_dtype=BF16, name="d_w_kvb")
    dqn = _mm(dq_raw, full["w_qb"], name="d_qn")
    dkvn = _mm(dkv_raw, full["w_kvb"], tb=True, name="d_kvn")
    dcq, dckv, grads["g_qa"], grads["g_kva"] = _stage(
        _mla_a_bwd_fn, [_cols(z, 256, 0), _cols(z, 256, 1), dqn, dkvn], [g_qa, g_kva],
        [_sds((T, 256), BF16), _sds((T, 256), BF16)], [_sds(g_qa.shape, F32), _sds(g_kva.shape, F32)],
        name="mla_latent_norm_bwd", tile=TOK_TILE_LIGHT)
    token = reduce_group(0, dcq)
    dz = jnp.concatenate([dcq, dckv, dhq, dhf_f, dhf_b, dhi, dhg, dkr], axis=1)
    grads["w_in"] = _from_z_layout(_mm(dz, h1, ta=True, out_dtype=BF16, name="d_w_in", after=token))
    token = scatter_start(1)
    grad_x, grads["g_mix"] = _mm_fused(
        dz, w_in_zt, lambda acc, x, dx1, g: _norm_bwd_fn(x, acc, dx1, g), [x2d, dx1], [g_mix],
        [F32], [_sds((1, D), F32)], full_rows=True, name="d_h1_norm_mix_bwd", after=token)

    out_g, out_d, out_m, out_v = {}, {}, {}, {}

    def update_group(gi, after):
        mine, theirs = _exchange_wait(swap_started[gi], after, name=f"swap_wait_{gi}")
        tokens = []
        for n, a, b in zip(scatter_groups[gi], mine, theirs):
            *res, token = _adamw(view(n, w_named[n]), a, b, view(n, m_named[n]), view(n, v_named[n]), name="adamw_" + n)
            out_g[n], out_d[n], out_m[n], out_v[n] = (unview(n, t) for t in res)
            tokens.append(token)
        return tokens

    done = update_group(0, grads["g_mix"])
    token = reduce_group(1, done)
    done = update_group(1, token)

    small = ("g_mix", "g_qa", "g_kva", "g_qn", "g_kn", "g_hgo", "g_ffn", "g_ple")
    small_all = small + ("lb_param",)
    width = -(-max(w_named[n].size for n in small_all) // LANE) * LANE

    def row(a):
        a = a.reshape(1, -1)
        return jnp.pad(a, ((0, 0), (0, width - a.shape[1])))

    dlower = jnp.concatenate([dlow_f.reshape(1, -1), dlow_b.reshape(1, -1)], axis=0)
    pack = jnp.concatenate([row(grads[n]) for n in small] + [row(dlower[0]), row(dlower[1]), row(loss_part)]
                           + [jnp.zeros((5, width), F32)], axis=0)
    red = _allreduce_small(pack, done[-1])
    loss = red[10, 0]

    outs = _adamw_small(red, lb_full, [w_named[n] for n in small_all], [m_named[n] for n in small_all],
                        [v_named[n] for n in small_all])
    for i, n in enumerate(small_all):
        out_g[n], out_d[n], out_m[n], out_v[n] = outs[4 * i:4 * i + 4]

    return (loss, grad_x[None], *[out_g[n] for n in order], *[out_d[n] for n in order],
            *[out_m[n] for n in order], *[out_v[n] for n in order])
```
